```python
import math
import jax, jax.numpy as jnp
from jax import lax
import numpy as np

D_MODEL = 1024
BATCH = 8
SEQ = 8192
DEPTH = 4

N_MIXERS = 2
N_A_LAYERS = (DEPTH + 1) // 2
N_B_LAYERS = DEPTH // 2

A_HEADS = 16
A_HEAD_DIM = 64
A_WIDTH = A_HEADS * A_HEAD_DIM
DILATION_PAIRS = ((128, 1), (512, 4), (2048, 16))
N_DIL = len(DILATION_PAIRS)
A_QBLOCK = 128
A_IN_COLS = N_DIL * 3 * A_WIDTH + A_WIDTH

SSM_INNER = 2 * D_MODEL
SSM_HEAD_DIM = 64
SSM_HEADS = SSM_INNER // SSM_HEAD_DIM
SSM_STATE = 128
SSM_GROUPS = 4
SSM_CONV = 5
SSM_CHUNK = 128
SSM_CONV_DIM = SSM_INNER + 2 * SSM_GROUPS * SSM_STATE
SSM_IN_COLS = SSM_INNER + SSM_CONV_DIM + 2 * SSM_HEADS

DEEPNORM_ALPHA = (2 * DEPTH) ** 0.25
DEEPNORM_BETA = (8 * DEPTH) ** -0.25
LN_EPS = 1e-5
RMS_EPS = 1e-5

kernel_name = "hybrid_dilated_attn_ssd_encoder"


def _layer_norm(x, g, b):
    xf = x.astype(jnp.float32)
    mu = jnp.mean(xf, axis=-1, keepdims=True)
    var = jnp.mean(jnp.square(xf - mu), axis=-1, keepdims=True)
    y = (xf - mu) * lax.rsqrt(var + LN_EPS) * g.astype(jnp.float32) + b.astype(jnp.float32)
    return y.astype(x.dtype)


def _rms_norm(x, g):
    xf = x.astype(jnp.float32)
    y = xf * lax.rsqrt(jnp.mean(jnp.square(xf), axis=-1, keepdims=True) + RMS_EPS)
    return (y * g.astype(jnp.float32)).astype(x.dtype)


def _alibi_slopes(n):
    return jnp.asarray(2.0 ** (-8.0 * (np.arange(n, dtype=np.float32) + 1.0) / n), dtype=jnp.float32)


def _dilated_group(q, k, v, window, dil, slopes):
    bsz, s, h, dh = q.shape
    length = s // dil
    radius = window // (2 * dil)
    qb = math.gcd(length, A_QBLOCK)
    nblk = length // qb
    span = qb + 2 * radius

    def to_sub(t):
        return t.reshape(bsz, length, dil, h, dh).transpose(0, 2, 3, 1, 4)

    qs = to_sub(q).reshape(bsz, dil, h, nblk, qb, dh)
    pad = ((0, 0), (0, 0), (0, 0), (radius, radius), (0, 0))
    kp = jnp.pad(to_sub(k), pad)
    vp = jnp.pad(to_sub(v), pad)
    idx = np.arange(nblk)[:, None] * qb + np.arange(span)[None, :]
    kb = kp[:, :, :, idx]
    vb = vp[:, :, :, idx]

    scores = jnp.einsum('bdhnqe,bdhnke->bdhnqk', qs, kb).astype(jnp.float32) / math.sqrt(dh)
    delta = np.arange(span)[None, :] - radius - np.arange(qb)[:, None]
    keypos = np.arange(nblk)[:, None, None] * qb + np.arange(span)[None, None, :] - radius
    valid = (np.abs(delta)[None] <= radius) & (keypos >= 0) & (keypos < length)
    dist = jnp.asarray(np.abs(delta) * dil, dtype=jnp.float32)
    alibi = -slopes[:, None, None] * dist[None]
    scores = jnp.where(valid, scores + alibi[None, None, :, None], -jnp.inf)

    m = jnp.max(scores, axis=-1, keepdims=True)
    p = jnp.exp(scores - m)
    z = jnp.sum(p, axis=-1, keepdims=True)
    o = jnp.einsum('bdhnqk,bdhnke->bdhnqe', p.astype(v.dtype), vb) / z.astype(v.dtype)
    lse = (m + jnp.log(z))[..., 0]

    o = o.reshape(bsz, dil, h, length, dh).transpose(0, 3, 1, 2, 4).reshape(bsz, s, h, dh)
    lse = lse.reshape(bsz, dil, h, length).transpose(0, 3, 1, 2).reshape(bsz, s, h)
    return o, lse


def _dilated_attention_mixer(h, w_in, w_out):
    bsz, s, _ = h.shape
    proj = h @ w_in
    qkv = proj[..., :N_DIL * 3 * A_WIDTH].reshape(bsz, s, N_DIL, 3, A_HEADS, A_HEAD_DIM)
    gate = proj[..., N_DIL * 3 * A_WIDTH:]
    slopes = _alibi_slopes(A_HEADS)
    outs, lses = [], []
    for g, (window, dil) in enumerate(DILATION_PAIRS):
        o, l = _dilated_group(qkv[:, :, g, 0], qkv[:, :, g, 1], qkv[:, :, g, 2], window, dil, slopes)
        outs.append(o)
        lses.append(l)
    wts = jax.nn.softmax(jnp.stack(lses, axis=-1), axis=-1).astype(h.dtype)
    o = jnp.einsum('bshg,bshgd->bshd', wts, jnp.stack(outs, axis=3))
    y = o.reshape(bsz, s, A_WIDTH) * jax.nn.silu(gate)
    return y @ w_out


def _segsum_exp(a_cum):
    t = a_cum.shape[-1]
    mask = np.tril(np.ones((t, t), dtype=bool))
    diff = a_cum[..., :, None] - a_cum[..., None, :]
    return jnp.where(mask, jnp.exp(jnp.where(mask, diff, 0.0)), 0.0)


def _ssd_scan(x, dt, a, bm, cm):
    bsz, s, h, p = x.shape
    g, n = bm.shape[-2:]
    rep = h // g
    nc = s // SSM_CHUNK
    xf = x.astype(jnp.float32)
    xg = (xf * dt[..., None]).reshape(bsz, nc, SSM_CHUNK, g, rep, p)
    bc = bm.astype(jnp.float32).reshape(bsz, nc, SSM_CHUNK, g, n)
    cc = cm.astype(jnp.float32).reshape(bsz, nc, SSM_CHUNK, g, n)
    ac = (dt * a).reshape(bsz, nc, SSM_CHUNK, h).transpose(0, 3, 1, 2)
    a_cum = jnp.cumsum(ac, axis=-1)

    lmat = _segsum_exp(a_cum).reshape(bsz, g, rep, nc, SSM_CHUNK, SSM_CHUNK)
    cb = jnp.einsum('bclgn,bcsgn->bgcls', cc, bc)
    y_diag = jnp.einsum('bgrcls,bcsgrp->bclgrp', cb[:, :, None] * lmat, xg)

    decay_states = jnp.exp(a_cum[..., -1:] - a_cum).reshape(bsz, g, rep, nc, SSM_CHUNK)
    states = jnp.einsum('bcsgn,bgrcs,bcsgrp->bcgrpn', bc, decay_states, xg)

    chunk_decay = jnp.moveaxis(jnp.exp(a_cum[..., -1]).reshape(bsz, g, rep, nc), 3, 0)

    def step(hstate, inp):
        dec, st = inp
        return hstate * dec[..., None, None] + st, hstate

    h0 = jnp.zeros((bsz, g, rep, p, n), jnp.float32)
    _, h_prev = lax.scan(step, h0, (chunk_decay, jnp.moveaxis(states, 1, 0)))
    h_prev = jnp.moveaxis(h_prev, 0, 1)

    out_decay = jnp.exp(a_cum).reshape(bsz, g, rep, nc, SSM_CHUNK)
    y_off = jnp.einsum('bclgn,bcgrpn,bgrcl->bclgrp', cc, h_prev, out_decay)
    return (y_diag + y_off).reshape(bsz, s, h, p)


def _ssd_mixer(h, w_in, conv_w, conv_b, dt_bias, a_log, d_skip, norm_w, w_out):
    bsz, s, _ = h.shape
    proj = h @ w_in
    z = proj[..., :SSM_INNER]
    xbc = proj[..., SSM_INNER:SSM_INNER + SSM_CONV_DIM]
    dt_raw = proj[..., SSM_INNER + SSM_CONV_DIM:].astype(jnp.float32)

    half = SSM_CONV // 2
    xbc = lax.conv_general_dilated(
        xbc, conv_w[:, None, :].astype(xbc.dtype), window_strides=(1,), padding=[(half, half)],
        dimension_numbers=('NWC', 'WIO', 'NWC'), feature_group_count=SSM_CONV_DIM)
    xbc = jax.nn.silu(xbc + conv_b)
    xs = xbc[..., :SSM_INNER].reshape(bsz, s, SSM_HEADS, SSM_HEAD_DIM)
    bm = xbc[..., SSM_INNER:SSM_INNER + SSM_GROUPS * SSM_STATE].reshape(bsz, s, SSM_GROUPS, SSM_STATE)
    cm = xbc[..., SSM_INNER + SSM_GROUPS * SSM_STATE:].reshape(bsz, s, SSM_GROUPS, SSM_STATE)

    dtb = dt_bias.astype(jnp.float32)
    dt_f = jax.nn.softplus(dt_raw[..., :SSM_HEADS] + dtb[0])
    dt_b = jax.nn.softplus(dt_raw[..., SSM_HEADS:] + dtb[1])
    a = -jnp.exp(a_log.astype(jnp.float32))

    y_f = _ssd_scan(xs, dt_f, a[0], bm, cm)
    flip = lambda t: jnp.flip(t, axis=1)
    y_b = flip(_ssd_scan(flip(xs), flip(dt_b), a[1], flip(bm), flip(cm)))
    y = y_f + y_b + d_skip.astype(jnp.float32)[:, None] * xs.astype(jnp.float32)
    y = y.astype(h.dtype).reshape(bsz, s, SSM_INNER) * jax.nn.silu(z)
    y = _rms_norm(y, norm_w)
    return y @ w_out


def _fwd_setup_inputs(seed: int = 0) -> dict:
    key = jax.random.key(seed)
    ks = jax.random.split(key, 20)
    f32 = jnp.float32
    nrm = lambda k, shape, scale: jax.random.normal(k, shape, f32) * scale

    x = jax.random.normal(ks[0], (BATCH, SEQ, D_MODEL), f32)
    c = jax.random.normal(ks[1], (BATCH, D_MODEL), f32)
    ada_w = nrm(ks[2], (DEPTH, D_MODEL, 3 * D_MODEL), D_MODEL ** -0.5)
    ada_b = nrm(ks[3], (DEPTH, 3 * D_MODEL), 0.02)
    ln_g = 1.0 + nrm(ks[4], (DEPTH, D_MODEL), 0.02)
    ln_b = nrm(ks[5], (DEPTH, D_MODEL), 0.02)

    a_w_in = nrm(ks[6], (N_A_LAYERS, D_MODEL, A_IN_COLS), D_MODEL ** -0.5)
    a_w_out = nrm(ks[7], (N_A_LAYERS, A_WIDTH, D_MODEL), DEEPNORM_BETA * A_WIDTH ** -0.5)

    b_w_in = nrm(ks[8], (N_B_LAYERS, D_MODEL, SSM_IN_COLS), D_MODEL ** -0.5)
    b_conv_w = nrm(ks[9], (N_B_LAYERS, SSM_CONV, SSM_CONV_DIM), SSM_CONV ** -0.5)
    b_conv_b = nrm(ks[10], (N_B_LAYERS, SSM_CONV_DIM), 0.02)
    dt0 = jnp.exp(jax.random.uniform(ks[11], (N_B_LAYERS, 2, SSM_HEADS), f32,
                                     math.log(1e-3), math.log(1e-1)))
    b_dt_bias = dt0 + jnp.log(-jnp.expm1(-dt0))
    b_a_log = jnp.log(jax.random.uniform(ks[12], (N_B_LAYERS, 2, SSM_HEADS), f32, 1.0, 16.0))
    b_d = 1.0 + nrm(ks[13], (N_B_LAYERS, SSM_HEADS), 0.02)
    b_norm_w = 1.0 + nrm(ks[14], (N_B_LAYERS, SSM_INNER), 0.02)
    b_w_out = nrm(ks[15], (N_B_LAYERS, SSM_INNER, D_MODEL), DEEPNORM_BETA * SSM_INNER ** -0.5)

    return {"x": x, "c": c, "ada_w": ada_w, "ada_b": ada_b, "ln_g": ln_g, "ln_b": ln_b,
            "a_w_in": a_w_in, "a_w_out": a_w_out,
            "b_w_in": b_w_in, "b_conv_w": b_conv_w, "b_conv_b": b_conv_b,
            "b_dt_bias": b_dt_bias, "b_a_log": b_a_log, "b_d": b_d,
            "b_norm_w": b_norm_w, "b_w_out": b_w_out}


def _fwd_reference(x, c, ada_w, ada_b, ln_g, ln_b, a_w_in, a_w_out,
              b_w_in, b_conv_w, b_conv_b, b_dt_bias, b_a_log, b_d, b_norm_w, b_w_out):
    cond = jax.nn.silu(c)
    for i in range(DEPTH):
        mod = cond @ ada_w[i] + ada_b[i]
        shift, scale, gate = jnp.split(mod, 3, axis=-1)
        h = x * (1.0 + scale[:, None, :]) + shift[:, None, :]
        j = i // N_MIXERS
        if i % N_MIXERS == 0:
            y = _dilated_attention_mixer(h, a_w_in[j], a_w_out[j])
        else:
            y = _ssd_mixer(h, b_w_in[j], b_conv_w[j], b_conv_b[j], b_dt_bias[j],
                           b_a_log[j], b_d[j], b_norm_w[j], b_w_out[j])
        x = _layer_norm(DEEPNORM_ALPHA * x + gate[:, None, :] * y, ln_g[i], ln_b[i])
    return x


import jax as _jax
import jax.numpy as _jnp

TWIN_FORMAT = 'train_step'
FWD_PARAMS = ['x', 'c', 'ada_w', 'ada_b', 'ln_g', 'ln_b', 'a_w_in', 'a_w_out', 'b_w_in', 'b_conv_w', 'b_conv_b', 'b_dt_bias', 'b_a_log', 'b_d', 'b_norm_w', 'b_w_out']
TWIN_WEIGHTS = ['ada_w', 'ada_b', 'ln_g', 'ln_b', 'a_w_in', 'a_w_out', 'b_w_in', 'b_conv_w', 'b_conv_b', 'b_dt_bias', 'b_a_log', 'b_d', 'b_norm_w', 'b_w_out']
TWIN_DIFF_INPUT = 'x'
TWIN_INPUTS = ['x', 'c', 'ada_w', 'ada_b', 'ln_g', 'ln_b', 'a_w_in', 'a_w_out', 'b_w_in', 'b_conv_w', 'b_conv_b', 'b_dt_bias', 'b_a_log', 'b_d', 'b_norm_w', 'b_w_out', 'loss_target', 'm_ada_w', 'm_ada_b', 'm_ln_g', 'm_ln_b', 'm_a_w_in', 'm_a_w_out', 'm_b_w_in', 'm_b_conv_w', 'm_b_conv_b', 'm_b_dt_bias', 'm_b_a_log', 'm_b_d', 'm_b_norm_w', 'm_b_w_out', 'v_ada_w', 'v_ada_b', 'v_ln_g', 'v_ln_b', 'v_a_w_in', 'v_a_w_out', 'v_b_w_in', 'v_b_conv_w', 'v_b_conv_b', 'v_b_dt_bias', 'v_b_a_log', 'v_b_d', 'v_b_norm_w', 'v_b_w_out']
TWIN_OUTPUTS = ['loss', 'grad_x', 'grad_ada_w', 'grad_ada_b', 'grad_ln_g', 'grad_ln_b', 'grad_a_w_in', 'grad_a_w_out', 'grad_b_w_in', 'grad_b_conv_w', 'grad_b_conv_b', 'grad_b_dt_bias', 'grad_b_a_log', 'grad_b_d', 'grad_b_norm_w', 'grad_b_w_out', 'delta_ada_w', 'delta_ada_b', 'delta_ln_g', 'delta_ln_b', 'delta_a_w_in', 'delta_a_w_out', 'delta_b_w_in', 'delta_b_conv_w', 'delta_b_conv_b', 'delta_b_dt_bias', 'delta_b_a_log', 'delta_b_d', 'delta_b_norm_w', 'delta_b_w_out', 'new_m_ada_w', 'new_m_ada_b', 'new_m_ln_g', 'new_m_ln_b', 'new_m_a_w_in', 'new_m_a_w_out', 'new_m_b_w_in', 'new_m_b_conv_w', 'new_m_b_conv_b', 'new_m_b_dt_bias', 'new_m_b_a_log', 'new_m_b_d', 'new_m_b_norm_w', 'new_m_b_w_out', 'new_v_ada_w', 'new_v_ada_b', 'new_v_ln_g', 'new_v_ln_b', 'new_v_a_w_in', 'new_v_a_w_out', 'new_v_b_w_in', 'new_v_b_conv_w', 'new_v_b_conv_b', 'new_v_b_dt_bias', 'new_v_b_a_log', 'new_v_b_d', 'new_v_b_norm_w', 'new_v_b_w_out']
TWIN_LEAF_KINDS = {'loss': 'loss', 'grad_x': 'grad_x', 'grad_ada_w': 'grad_w', 'grad_ada_b': 'grad_w', 'grad_ln_g': 'grad_w', 'grad_ln_b': 'grad_w', 'grad_a_w_in': 'grad_w', 'grad_a_w_out': 'grad_w', 'grad_b_w_in': 'grad_w', 'grad_b_conv_w': 'grad_w', 'grad_b_conv_b': 'grad_w', 'grad_b_dt_bias': 'grad_w', 'grad_b_a_log': 'grad_w', 'grad_b_d': 'grad_w', 'grad_b_norm_w': 'grad_w', 'grad_b_w_out': 'grad_w', 'delta_ada_w': 'delta_w', 'delta_ada_b': 'delta_w', 'delta_ln_g': 'delta_w', 'delta_ln_b': 'delta_w', 'delta_a_w_in': 'delta_w', 'delta_a_w_out': 'delta_w', 'delta_b_w_in': 'delta_w', 'delta_b_conv_w': 'delta_w', 'delta_b_conv_b': 'delta_w', 'delta_b_dt_bias': 'delta_w', 'delta_b_a_log': 'delta_w', 'delta_b_d': 'delta_w', 'delta_b_norm_w': 'delta_w', 'delta_b_w_out': 'delta_w', 'new_m_ada_w': 'new_m', 'new_m_ada_b': 'new_m', 'new_m_ln_g': 'new_m', 'new_m_ln_b': 'new_m', 'new_m_a_w_in': 'new_m', 'new_m_a_w_out': 'new_m', 'new_m_b_w_in': 'new_m', 'new_m_b_conv_w': 'new_m', 'new_m_b_conv_b': 'new_m', 'new_m_b_dt_bias': 'new_m', 'new_m_b_a_log': 'new_m', 'new_m_b_d': 'new_m', 'new_m_b_norm_w': 'new_m', 'new_m_b_w_out': 'new_m', 'new_v_ada_w': 'new_v', 'new_v_ada_b': 'new_v', 'new_v_ln_g': 'new_v', 'new_v_ln_b': 'new_v', 'new_v_a_w_in': 'new_v', 'new_v_a_w_out': 'new_v', 'new_v_b_w_in': 'new_v', 'new_v_b_conv_w': 'new_v', 'new_v_b_conv_b': 'new_v', 'new_v_b_dt_bias': 'new_v', 'new_v_b_a_log': 'new_v', 'new_v_b_d': 'new_v', 'new_v_b_norm_w': 'new_v', 'new_v_b_w_out': 'new_v'}


def _forward(args):
    return _fwd_reference(*[args[k] for k in FWD_PARAMS])


def _output_shape():
    def fwd():
        inp = _fwd_setup_inputs(0)
        return _fwd_reference(*[inp[k] for k in FWD_PARAMS])
    out = _jax.eval_shape(fwd)
    return out.shape, out.dtype

N_MICROBATCH = 1
ADAM_LR = 0.001
ADAM_B1 = 0.9
ADAM_B2 = 0.999
ADAM_EPS = 1e-08
ADAM_WD = 0.01
ADAM_STEP = 10
PER_EXAMPLE_BATCH_AXIS = {'x': 0, 'c': 0, 'loss_target': 0}
SHARED_INPUTS = []
_WEIGHT_DTYPES = {'ada_w': _jnp.float32, 'ada_b': _jnp.float32, 'ln_g': _jnp.float32, 'ln_b': _jnp.float32, 'a_w_in': _jnp.float32, 'a_w_out': _jnp.float32, 'b_w_in': _jnp.float32, 'b_conv_w': _jnp.float32, 'b_conv_b': _jnp.float32, 'b_dt_bias': _jnp.float32, 'b_a_log': _jnp.float32, 'b_d': _jnp.float32, 'b_norm_w': _jnp.float32, 'b_w_out': _jnp.float32}
MOMENT_SCALE = {'ada_w': 2.868974e-02, 'ada_b': 5.123073e-02, 'ln_g': 3.213031e+01, 'ln_b': 1.756392e+00, 'a_w_in': 1.191373e-02, 'a_w_out': 4.873894e-02, 'b_w_in': 2.675991e-02, 'b_conv_w': 2.508821e-02, 'b_conv_b': 2.758461e-02, 'b_dt_bias': 5.184367e-02, 'b_a_log': 9.121411e-02, 'b_d': 5.010540e-02, 'b_norm_w': 2.863824e-02, 'b_w_out': 9.794032e-02}


def _to_microbatches(a, axis):
    t = _jnp.moveaxis(a, axis, 0)
    t = t.reshape((N_MICROBATCH, t.shape[0] // N_MICROBATCH) + t.shape[1:])
    return _jnp.moveaxis(t, 1, axis + 1)


def setup_inputs(seed: int = 0) -> dict:
    inp = _fwd_setup_inputs(seed)
    key = _jax.random.fold_in(_jax.random.key(seed), 7919)
    shape, _ = _output_shape()
    out = dict(inp)
    out["loss_target"] = _jax.random.normal(_jax.random.fold_in(key, 0), shape, _jnp.float32)
    for i, name in enumerate(TWIN_WEIGHTS):
        w = inp[name].astype(_jnp.float32)
        if MOMENT_SCALE is None:
            s = _jnp.sqrt(_jnp.mean(_jnp.square(w)) + 1e-30)
        else:
            s = MOMENT_SCALE[name]
        km, kv = _jax.random.split(_jax.random.fold_in(key, i + 1))
        out[name] = w
        out["m_" + name] = s * _jax.random.normal(km, w.shape, _jnp.float32)
        out["v_" + name] = (s * s) * _jax.random.uniform(kv, w.shape, _jnp.float32, 0.5, 1.5)
    if N_MICROBATCH > 1:
        for name, axis in PER_EXAMPLE_BATCH_AXIS.items():
            out[name] = _to_microbatches(out[name], axis)
    return {'x': out['x'], 'c': out['c'], 'ada_w': out['ada_w'], 'ada_b': out['ada_b'], 'ln_g': out['ln_g'], 'ln_b': out['ln_b'], 'a_w_in': out['a_w_in'], 'a_w_out': out['a_w_out'], 'b_w_in': out['b_w_in'], 'b_conv_w': out['b_conv_w'], 'b_conv_b': out['b_conv_b'], 'b_dt_bias': out['b_dt_bias'], 'b_a_log': out['b_a_log'], 'b_d': out['b_d'], 'b_norm_w': out['b_norm_w'], 'b_w_out': out['b_w_out'], 'loss_target': out['loss_target'], 'm_ada_w': out['m_ada_w'], 'm_ada_b': out['m_ada_b'], 'm_ln_g': out['m_ln_g'], 'm_ln_b': out['m_ln_b'], 'm_a_w_in': out['m_a_w_in'], 'm_a_w_out': out['m_a_w_out'], 'm_b_w_in': out['m_b_w_in'], 'm_b_conv_w': out['m_b_conv_w'], 'm_b_conv_b': out['m_b_conv_b'], 'm_b_dt_bias': out['m_b_dt_bias'], 'm_b_a_log': out['m_b_a_log'], 'm_b_d': out['m_b_d'], 'm_b_norm_w': out['m_b_norm_w'], 'm_b_w_out': out['m_b_w_out'], 'v_ada_w': out['v_ada_w'], 'v_ada_b': out['v_ada_b'], 'v_ln_g': out['v_ln_g'], 'v_ln_b': out['v_ln_b'], 'v_a_w_in': out['v_a_w_in'], 'v_a_w_out': out['v_a_w_out'], 'v_b_w_in': out['v_b_w_in'], 'v_b_conv_w': out['v_b_conv_w'], 'v_b_conv_b': out['v_b_conv_b'], 'v_b_dt_bias': out['v_b_dt_bias'], 'v_b_a_log': out['v_b_a_log'], 'v_b_d': out['v_b_d'], 'v_b_norm_w': out['v_b_norm_w'], 'v_b_w_out': out['v_b_w_out']}


def _loss(weights, diff, rest, loss_target):
    with _jax.named_scope("forward"):
        args = {**rest, TWIN_DIFF_INPUT: diff, **{k: w.astype(_WEIGHT_DTYPES[k]) for k, w in weights.items()}}
        y = _forward(args)
    with _jax.named_scope("loss_head"):
        err = _jnp.square(y.astype(_jnp.float32) - loss_target)
        return 0.5 * _jnp.sum(_jnp.mean(err, axis=-1)) if err.ndim else 0.5 * err


def _adamw(w, g, m, v):
    m = ADAM_B1 * m + (1.0 - ADAM_B1) * g
    v = ADAM_B2 * v + (1.0 - ADAM_B2) * _jnp.square(g)
    m_hat = m / (1.0 - ADAM_B1 ** ADAM_STEP)
    v_hat = v / (1.0 - ADAM_B2 ** ADAM_STEP)
    delta = -ADAM_LR * (m_hat / (_jnp.sqrt(v_hat) + ADAM_EPS) + ADAM_WD * w)
    return delta, m, v


def reference(x, c, ada_w, ada_b, ln_g, ln_b, a_w_in, a_w_out, b_w_in, b_conv_w, b_conv_b, b_dt_bias, b_a_log, b_d, b_norm_w, b_w_out, loss_target, m_ada_w, m_ada_b, m_ln_g, m_ln_b, m_a_w_in, m_a_w_out, m_b_w_in, m_b_conv_w, m_b_conv_b, m_b_dt_bias, m_b_a_log, m_b_d, m_b_norm_w, m_b_w_out, v_ada_w, v_ada_b, v_ln_g, v_ln_b, v_a_w_in, v_a_w_out, v_b_w_in, v_b_conv_w, v_b_conv_b, v_b_dt_bias, v_b_a_log, v_b_d, v_b_norm_w, v_b_w_out):
    given = dict(x=x, c=c, ada_w=ada_w, ada_b=ada_b, ln_g=ln_g, ln_b=ln_b, a_w_in=a_w_in, a_w_out=a_w_out, b_w_in=b_w_in, b_conv_w=b_conv_w, b_conv_b=b_conv_b, b_dt_bias=b_dt_bias, b_a_log=b_a_log, b_d=b_d, b_norm_w=b_norm_w, b_w_out=b_w_out, loss_target=loss_target, m_ada_w=m_ada_w, m_ada_b=m_ada_b, m_ln_g=m_ln_g, m_ln_b=m_ln_b, m_a_w_in=m_a_w_in, m_a_w_out=m_a_w_out, m_b_w_in=m_b_w_in, m_b_conv_w=m_b_conv_w, m_b_conv_b=m_b_conv_b, m_b_dt_bias=m_b_dt_bias, m_b_a_log=m_b_a_log, m_b_d=m_b_d, m_b_norm_w=m_b_norm_w, m_b_w_out=m_b_w_out, v_ada_w=v_ada_w, v_ada_b=v_ada_b, v_ln_g=v_ln_g, v_ln_b=v_ln_b, v_a_w_in=v_a_w_in, v_a_w_out=v_a_w_out, v_b_w_in=v_b_w_in, v_b_conv_w=v_b_conv_w, v_b_conv_b=v_b_conv_b, v_b_dt_bias=v_b_dt_bias, v_b_a_log=v_b_a_log, v_b_d=v_b_d, v_b_norm_w=v_b_norm_w, v_b_w_out=v_b_w_out)
    weights = {n: given[n] for n in TWIN_WEIGHTS}
    shared = {n: given[n] for n in SHARED_INPUTS}
    per_example = {n: given[n] for n in ['x', 'c']}
    grad_fn = _jax.value_and_grad(_loss, argnums=(0, 1))

    def one_microbatch(ex, loss_target):
        ex = dict(ex)
        diff = ex.pop(TWIN_DIFF_INPUT)
        return grad_fn(weights, diff, {**shared, **ex}, loss_target)

    if N_MICROBATCH == 1:
        loss, (grad_w, grad_x) = one_microbatch(per_example, given["loss_target"])
    else:
        def body(carry, xs):
            loss_sum, grad_sum = carry
            l_k, (gw_k, gx_k) = one_microbatch(xs[0], xs[1])
            with _jax.named_scope("update"):
                return (loss_sum + l_k, _jax.tree.map(_jnp.add, grad_sum, gw_k)), gx_k

        init = (_jnp.zeros((), _jnp.float32), _jax.tree.map(_jnp.zeros_like, weights))
        (loss, grad_w), grad_x = _jax.lax.scan(body, init, (per_example, given["loss_target"]))
    with _jax.named_scope("update"):
        delta_w, new_m, new_v = {}, {}, {}
        for n in TWIN_WEIGHTS:
            delta_w[n], new_m[n], new_v[n] = _adamw(weights[n], grad_w[n], given["m_" + n], given["v_" + n])
    return (loss, grad_x, *[grad_w[n] for n in TWIN_WEIGHTS], *[delta_w[n] for n in TWIN_WEIGHTS],
            *[new_m[n] for n in TWIN_WEIGHTS], *[new_v[n] for n in TWIN_WEIGHTS])
```

```python
import functools
import math

import jax
import jax.numpy as jnp
import numpy as np
from jax import lax
from jax.experimental import pallas as pl
from jax.experimental.pallas import tpu as pltpu

F32 = jnp.float32
BF16 = jnp.bfloat16
HIGHEST = lax.Precision.HIGHEST
MESH = pl.DeviceIdType.MESH

D_MODEL = 1024
DEPTH = 4
A_HEADS = 16
A_HEAD_DIM = 64
A_WIDTH = 1024
DILATIONS = (1, 4, 16)
A_RADIUS = 64
A_QBLOCK = 128
A_IN_COLS = 10240
SSM_INNER = 2048
SSM_HEADS = 32
SSM_HEAD_DIM = 64
SSM_STATE = 128
SSM_GROUPS = 4
SSM_CHUNK = 128
SSM_CONV = 5
SSM_CONV_DIM = 3072
SSM_IN_COLS = 5184
SSM_MAIN_COLS = 5120
SSM_PAD_COLS = 5376
CONV_HALO = 16
ALPHA = (2 * DEPTH) ** 0.25
LN_EPS = 1e-5
RMS_EPS = 1e-5
ADAM_LR, ADAM_B1, ADAM_B2, ADAM_EPS, ADAM_WD, ADAM_STEP = 0.001, 0.9, 0.999, 1e-08, 0.01, 10
N_DEV = 8
VMEM_LIMIT_BYTES = 56 * 1024 * 1024
NEG_BIG = -1e30


def _params(*sem):
    return pltpu.CompilerParams(dimension_semantics=sem, vmem_limit_bytes=VMEM_LIMIT_BYTES)


def _sigmoid(x):
    return 1.0 / (1.0 + jnp.exp(-x))


def _silu_and_grad(x):
    sg = _sigmoid(x)
    return x * sg, sg * (1.0 + x * (1.0 - sg))


def _softplus(x):
    e = jnp.exp(-jnp.abs(x))
    u = 1.0 + e
    log1p = jnp.where(u == 1.0, e, jnp.log(u) * (e / jnp.where(u == 1.0, 1.0, u - 1.0)))
    return jnp.maximum(x, 0.0) + log1p


_DIMS = {"nn": (((1,), (0,)), ((), ())), "nt": (((1,), (1,)), ((), ())), "tn": (((0,), (0,)), ((), ()))}


def _mm(a, b, *, mode, out_dtype, tm, tn, tk, name):
    if mode == "nn":
        (m, k), (_, n) = a.shape, b.shape
    elif mode == "nt":
        (m, k), (n, _) = a.shape, b.shape
    else:
        (k, m), (_, n) = a.shape, b.shape
    tm, tn, tk = min(tm, m), min(tn, n), min(tk, k)
    assert m % tm == 0 and n % tn == 0 and k % tk == 0, (name, a.shape, b.shape)
    nk = k // tk
    dims = _DIMS[mode]

    def body(a_ref, b_ref, o_ref, *scratch):
        part = lax.dot_general(a_ref[...], b_ref[...], dims, preferred_element_type=F32)
        if nk == 1:
            o_ref[...] = part.astype(o_ref.dtype)
            return
        acc_ref, = scratch
        kk = pl.program_id(2)

        @pl.when(kk == 0)
        def _():
            acc_ref[...] = part

        @pl.when(kk > 0)
        def _():
            acc_ref[...] += part

        @pl.when(kk == nk - 1)
        def _():
            o_ref[...] = acc_ref[...].astype(o_ref.dtype)

    if mode == "tn":
        a_spec = pl.BlockSpec((tk, tm), lambda i, j, kk: (kk, i))
    else:
        a_spec = pl.BlockSpec((tm, tk), lambda i, j, kk: (i, kk))
    if mode == "nt":
        b_spec = pl.BlockSpec((tn, tk), lambda i, j, kk: (j, kk))
    else:
        b_spec = pl.BlockSpec((tk, tn), lambda i, j, kk: (kk, j))
    return pl.pallas_call(
        body, name=name, grid=(m // tm, n // tn, nk),
        in_specs=[a_spec, b_spec], out_specs=pl.BlockSpec((tm, tn), lambda i, j, kk: (i, j)),
        out_shape=jax.ShapeDtypeStruct((m, n), out_dtype),
        scratch_shapes=[] if nk == 1 else [pltpu.VMEM((tm, tn), F32)],
        compiler_params=_params("parallel", "parallel", "arbitrary"),
    )(a, b)


def _mm_dh(dproj, w, dx_part, x, scale, *, tm, tk, name):
    s, k = dproj.shape
    d = w.shape[0]
    tk = min(tk, k)
    assert s % tm == 0 and k % tk == 0
    nk = k // tk

    def body(a_ref, w_ref, dxp_ref, x_ref, sc_ref, dx_ref, dsc_ref, dsh_ref, acc_ref):
        i, kk = pl.program_id(0), pl.program_id(1)
        part = lax.dot_general(a_ref[...], w_ref[...], _DIMS["nt"], preferred_element_type=F32)

        @pl.when(kk == 0)
        def _():
            acc_ref[...] = part

        @pl.when(kk > 0)
        def _():
            acc_ref[...] += part

        @pl.when(jnp.logical_and(i == 0, kk == 0))
        def _():
            dsc_ref[...] = jnp.zeros_like(dsc_ref)
            dsh_ref[...] = jnp.zeros_like(dsh_ref)

        @pl.when(kk == nk - 1)
        def _():
            dh = acc_ref[...]
            dx_ref[...] = dxp_ref[...] + dh * (1.0 + sc_ref[...])
            dsc_ref[...] += jnp.sum(dh * x_ref[...], axis=0, keepdims=True)
            dsh_ref[...] += jnp.sum(dh, axis=0, keepdims=True)

    row = pl.BlockSpec((tm, d), lambda i, kk: (i, 0))
    vec = pl.BlockSpec((1, d), lambda i, kk: (0, 0))
    return pl.pallas_call(
        body, name=name, grid=(s // tm, nk),
        in_specs=[pl.BlockSpec((tm, tk), lambda i, kk: (i, kk)), pl.BlockSpec((d, tk), lambda i, kk: (0, kk)),
                  row, row, vec],
        out_specs=[row, vec, vec],
        out_shape=[jax.ShapeDtypeStruct((s, d), F32), jax.ShapeDtypeStruct((1, d), F32),
                   jax.ShapeDtypeStruct((1, d), F32)],
        scratch_shapes=[pltpu.VMEM((tm, d), F32)],
        compiler_params=_params("arbitrary", "arbitrary"),
    )(dproj, w, dx_part, x, scale)


ROW_TILE = 512


def _modulate(x, scale, shift, *, name):
    s, d = x.shape

    def body(x_ref, sc_ref, sh_ref, h_ref):
        h_ref[...] = (x_ref[...] * (1.0 + sc_ref[...]) + sh_ref[...]).astype(BF16)

    row = pl.BlockSpec((ROW_TILE, d), lambda i: (i, 0))
    vec = pl.BlockSpec((1, d), lambda i: (0, 0))
    return pl.pallas_call(
        body, name=name, grid=(s // ROW_TILE,), in_specs=[row, vec, vec], out_specs=row,
        out_shape=jax.ShapeDtypeStruct((s, d), BF16), compiler_params=_params("parallel"),
    )(x, scale, shift)


def _resid_ln_fwd(x, out, gate, g, b, *, name):
    s, d = x.shape

    def body(x_ref, o_ref, gate_ref, g_ref, b_ref, y_ref):
        r = ALPHA * x_ref[...] + gate_ref[...] * o_ref[...]
        mu = jnp.mean(r, axis=-1, keepdims=True)
        rc = r - mu
        var = jnp.mean(rc * rc, axis=-1, keepdims=True)
        y_ref[...] = rc * lax.rsqrt(var + LN_EPS) * g_ref[...] + b_ref[...]

    row = pl.BlockSpec((ROW_TILE, d), lambda i: (i, 0))
    vec = pl.BlockSpec((1, d), lambda i: (0, 0))
    return pl.pallas_call(
        body, name=name, grid=(s // ROW_TILE,), in_specs=[row, row, vec, vec, vec], out_specs=row,
        out_shape=jax.ShapeDtypeStruct((s, d), F32), compiler_params=_params("parallel"),
    )(x, out, gate, g, b)


def _resid_ln_bwd(x, out, gate, g, dy, *, name):
    s, d = x.shape

    def body(x_ref, o_ref, gate_ref, g_ref, dy_ref, dxp_ref, dout_ref, dgate_ref, dg_ref, db_ref):
        @pl.when(pl.program_id(0) == 0)
        def _():
            dgate_ref[...] = jnp.zeros_like(dgate_ref)
            dg_ref[...] = jnp.zeros_like(dg_ref)
            db_ref[...] = jnp.zeros_like(db_ref)

        o = o_ref[...]
        r = ALPHA * x_ref[...] + gate_ref[...] * o
        mu = jnp.mean(r, axis=-1, keepdims=True)
        rc = r - mu
        var = jnp.mean(rc * rc, axis=-1, keepdims=True)
        rstd = lax.rsqrt(var + LN_EPS)
        xhat = rc * rstd
        dy = dy_ref[...]
        dxh = dy * g_ref[...]
        dr = rstd * (dxh - jnp.mean(dxh, axis=-1, keepdims=True) - xhat * jnp.mean(dxh * xhat, axis=-1, keepdims=True))
        dxp_ref[...] = ALPHA * dr
        dout_ref[...] = (gate_ref[...] * dr).astype(BF16)
        dgate_ref[...] += jnp.sum(dr * o, axis=0, keepdims=True)
        dg_ref[...] += jnp.sum(dy * xhat, axis=0, keepdims=True)
        db_ref[...] += jnp.sum(dy, axis=0, keepdims=True)

    row = pl.BlockSpec((ROW_TILE, d), lambda i: (i, 0))
    vec = pl.BlockSpec((1, d), lambda i: (0, 0))
    vshape = jax.ShapeDtypeStruct((1, d), F32)
    return pl.pallas_call(
        body, name=name, grid=(s // ROW_TILE,), in_specs=[row, row, vec, vec, row],
        out_specs=[row, row, vec, vec, vec],
        out_shape=[jax.ShapeDtypeStruct((s, d), F32), jax.ShapeDtypeStruct((s, d), BF16), vshape, vshape, vshape],
        compiler_params=_params("arbitrary"),
    )(x, out, gate, g, dy)


def _loss_and_grad(y, target, *, name):
    s, d = y.shape

    def body(y_ref, t_ref, dy_ref, loss_ref):
        @pl.when(pl.program_id(0) == 0)
        def _():
            loss_ref[...] = jnp.zeros_like(loss_ref)

        e = y_ref[...] - t_ref[...]
        dy_ref[...] = e * (1.0 / d)
        loss_ref[...] += jnp.sum(jnp.sum(e * e, axis=0, keepdims=True), axis=1, keepdims=True) * (0.5 / d)

    row = pl.BlockSpec((ROW_TILE, d), lambda i: (i, 0))
    return pl.pallas_call(
        body, name=name, grid=(s // ROW_TILE,), in_specs=[row, row],
        out_specs=[row, pl.BlockSpec((1, 128), lambda i: (0, 0))],
        out_shape=[jax.ShapeDtypeStruct((s, d), F32), jax.ShapeDtypeStruct((1, 128), F32)],
        compiler_params=_params("arbitrary"),
    )(y, target)


_SLOPES = np.asarray(2.0 ** (-8.0 * (np.arange(A_HEADS, dtype=np.float32) + 1.0) / A_HEADS), dtype=np.float32)


def _attn_scores(q, kw, slope, dist, valid):
    s = lax.dot_general(q, kw, _DIMS["nt"], preferred_element_type=F32) * (1.0 / math.sqrt(A_HEAD_DIM))
    return jnp.where(valid, s - slope * dist, NEG_BIG)


def _attn_window(blk, length, win, dil):
    start = pl.multiple_of(jnp.clip(blk * A_QBLOCK - A_RADIUS, 0, length - win), A_RADIUS)
    qpos = blk * A_QBLOCK + lax.broadcasted_iota(jnp.int32, (A_QBLOCK, win), 0)
    kpos = start + lax.broadcasted_iota(jnp.int32, (A_QBLOCK, win), 1)
    delta = jnp.abs(kpos - qpos)
    return start, (delta * dil).astype(F32), delta <= A_RADIUS


def _attn_fwd(proj, group, *, name):
    s, c = proj.shape
    dil = DILATIONS[group]
    length = s // dil
    win = min(2 * A_QBLOCK, length)
    nblk = length // A_QBLOCK
    cb = c // 128
    qoff = group * 24
    pv = proj.reshape(length, dil * c)

    def body(slope_ref, q_ref, k_ref, v_ref, o_ref, lse_ref):
        hp, blk = pl.program_id(1), pl.program_id(2)
        start, dist, valid = _attn_window(blk, length, win, dil)
        kw = k_ref[pl.ds(start, win), :]
        vw = v_ref[pl.ds(start, win), :]
        q = q_ref[...]
        outs, lses = [], []
        for hh in range(2):
            sl = slice(hh * A_HEAD_DIM, (hh + 1) * A_HEAD_DIM)
            sc = _attn_scores(q[:, sl], kw[:, sl], slope_ref[hp * 2 + hh], dist, valid)
            m = jnp.max(sc, axis=-1, keepdims=True)
            p = jnp.exp(sc - m)
            z = jnp.sum(p, axis=-1, keepdims=True)
            o = jnp.dot(p.astype(BF16), vw[:, sl], preferred_element_type=F32) / z
            outs.append(o)
            lses.append(jnp.broadcast_to(m + jnp.log(z), (A_QBLOCK, A_HEAD_DIM)))
        o_ref[...] = jnp.concatenate(outs, axis=1)
        lse_ref[...] = jnp.concatenate(lses, axis=1)

    qspec = pl.BlockSpec((A_QBLOCK, 128), lambda r, hp, b: (b, r * cb + qoff + hp))
    kspec = pl.BlockSpec((length, 128), lambda r, hp, b: (0, r * cb + qoff + 8 + hp))
    vspec = pl.BlockSpec((length, 128), lambda r, hp, b: (0, r * cb + qoff + 16 + hp))
    ospec = pl.BlockSpec((A_QBLOCK, 128), lambda r, hp, b: (b, r * 8 + hp))
    oshape = jax.ShapeDtypeStruct((length, dil * A_WIDTH), F32)
    o, lse = pl.pallas_call(
        body, name=name, grid=(dil, 8, nblk),
        in_specs=[pl.BlockSpec(memory_space=pltpu.SMEM), qspec, kspec, vspec], out_specs=[ospec, ospec],
        out_shape=[oshape, oshape], compiler_params=_params("parallel", "parallel", "arbitrary"),
    )(jnp.asarray(_SLOPES), pv, pv, pv)
    return o.reshape(s, A_WIDTH), lse.reshape(s, A_WIDTH)


def _attn_bwd(proj, group, do, o, lse, *, name):
    s, c = proj.shape
    dil = DILATIONS[group]
    length = s // dil
    win = min(2 * A_QBLOCK, length)
    nblk = length // A_QBLOCK
    cb = c // 128
    qoff = group * 24
    scale = 1.0 / math.sqrt(A_HEAD_DIM)
    pv = proj.reshape(length, dil * c)
    view = lambda t: t.reshape(length, dil * A_WIDTH)

    def body(slope_ref, q_ref, k_ref, v_ref, do_ref, o_ref, lse_ref, dq_ref, dk_ref, dv_ref, dk_acc, dv_acc):
        hp, blk = pl.program_id(1), pl.program_id(2)

        @pl.when(blk == 0)
        def _():
            dk_acc[...] = jnp.zeros_like(dk_acc)
            dv_acc[...] = jnp.zeros_like(dv_acc)

        start, dist, valid = _attn_window(blk, length, win, dil)
        kw = k_ref[pl.ds(start, win), :]
        vw = v_ref[pl.ds(start, win), :]
        q = q_ref[...]
        do_b = do_ref[...]
        dsum = do_b.astype(F32) * o_ref[...]
        lse_b = lse_ref[...]
        dqs, dks, dvs = [], [], []
        for hh in range(2):
            sl = slice(hh * A_HEAD_DIM, (hh + 1) * A_HEAD_DIM)
            sc = _attn_scores(q[:, sl], kw[:, sl], slope_ref[hp * 2 + hh], dist, valid)
            p = jnp.exp(sc - lse_b[:, hh * A_HEAD_DIM:hh * A_HEAD_DIM + 1])
            dp = lax.dot_general(do_b[:, sl], vw[:, sl], _DIMS["nt"], preferred_element_type=F32)
            ds = (p * (dp - jnp.sum(dsum[:, sl], axis=-1, keepdims=True))).astype(BF16)
            dqs.append(jnp.dot(ds, kw[:, sl], preferred_element_type=F32) * scale)
            dks.append(lax.dot_general(ds, q[:, sl], _DIMS["tn"], preferred_element_type=F32) * scale)
            dvs.append(lax.dot_general(p.astype(BF16), do_b[:, sl], _DIMS["tn"], preferred_element_type=F32))
        dq_ref[...] = jnp.concatenate(dqs, axis=1).astype(BF16)
        dk_acc[pl.ds(start, win), :] += jnp.concatenate(dks, axis=1)
        dv_acc[pl.ds(start, win), :] += jnp.concatenate(dvs, axis=1)

        @pl.when(blk == nblk - 1)
        def _():
            dk_ref[...] = dk_acc[...].astype(BF16)
            dv_ref[...] = dv_acc[...].astype(BF16)

    qspec = pl.BlockSpec((A_QBLOCK, 128), lambda r, hp, b: (b, r * cb + qoff + hp))
    kspec = pl.BlockSpec((length, 128), lambda r, hp, b: (0, r * cb + qoff + 8 + hp))
    vspec = pl.BlockSpec((length, 128), lambda r, hp, b: (0, r * cb + qoff + 16 + hp))
    bspec = pl.BlockSpec((A_QBLOCK, 128), lambda r, hp, b: (b, r * 8 + hp))
    fspec = pl.BlockSpec((length, 128), lambda r, hp, b: (0, r * 8 + hp))
    oshape = jax.ShapeDtypeStruct((length, dil * A_WIDTH), BF16)
    dq, dk, dv = pl.pallas_call(
        body, name=name, grid=(dil, 8, nblk),
        in_specs=[pl.BlockSpec(memory_space=pltpu.SMEM), qspec, kspec, vspec, bspec, bspec, bspec],
        out_specs=[bspec, fspec, fspec], out_shape=[oshape, oshape, oshape],
        scratch_shapes=[pltpu.VMEM((length, 128), F32), pltpu.VMEM((length, 128), F32)],
        compiler_params=_params("parallel", "parallel", "arbitrary"),
    )(jnp.asarray(_SLOPES), pv, pv, pv, view(do), view(o), view(lse))
    return dq.reshape(s, A_WIDTH), dk.reshape(s, A_WIDTH), dv.reshape(s, A_WIDTH)


A_GATE_BLOCK = 9


def _attn_combine(os_, lses, proj, *, name):
    s = proj.shape[0]
    tr = 256

    def body(o0, o1, o2, l0, l1, l2, gate_ref, y_ref, o_ref, lse_ref):
        la, lb, lc = l0[...], l1[...], l2[...]
        m = jnp.maximum(jnp.maximum(la, lb), lc)
        ea, eb, ec = jnp.exp(la - m), jnp.exp(lb - m), jnp.exp(lc - m)
        den = ea + eb + ec
        o = (ea * o0[...] + eb * o1[...] + ec * o2[...]) / den
        o_ref[...] = o
        lse_ref[...] = m + jnp.log(den)
        y_ref[...] = (o * _silu_and_grad(gate_ref[...].astype(F32))[0]).astype(BF16)

    row = pl.BlockSpec((tr, A_WIDTH), lambda i: (i, 0))
    gspec = pl.BlockSpec((tr, A_WIDTH), lambda i: (i, A_GATE_BLOCK))
    return pl.pallas_call(
        body, name=name, grid=(s // tr,), in_specs=[row] * 6 + [gspec], out_specs=[row, row, row],
        out_shape=[jax.ShapeDtypeStruct((s, A_WIDTH), BF16), jax.ShapeDtypeStruct((s, A_WIDTH), F32),
                   jax.ShapeDtypeStruct((s, A_WIDTH), F32)],
        compiler_params=_params("parallel"),
    )(*os_, *lses, proj)


def _attn_combine_bwd(dy, o, proj, *, name):
    s = proj.shape[0]
    tr = 256

    def body(dy_ref, o_ref, gate_ref, do_ref, dg_ref):
        si, dsi = _silu_and_grad(gate_ref[...].astype(F32))
        dyv = dy_ref[...]
        do_ref[...] = (dyv * si).astype(BF16)
        dg_ref[...] = (dyv * o_ref[...] * dsi).astype(BF16)

    row = pl.BlockSpec((tr, A_WIDTH), lambda i: (i, 0))
    gspec = pl.BlockSpec((tr, A_WIDTH), lambda i: (i, A_GATE_BLOCK))
    shp = jax.ShapeDtypeStruct((s, A_WIDTH), BF16)
    return pl.pallas_call(
        body, name=name, grid=(s // tr,), in_specs=[row, row, gspec], out_specs=[row, row], out_shape=[shp, shp],
        compiler_params=_params("parallel"),
    )(dy, o, proj)


CONV_TILE = 256


def _conv_taps(xe, n):
    return [xe if j == 2 else pltpu.roll(xe, (2 - j) % n, 0) for j in range(SSM_CONV)]


def _conv_fwd(xpad, w, b, *, name):
    s = xpad.shape[0] - 2 * CONV_HALO
    n = CONV_TILE + 2 * CONV_HALO
    ncol = SSM_CONV_DIM // 128

    def body(x_ref, w_ref, b_ref, o_ref):
        t0 = pl.multiple_of(pl.program_id(1) * CONV_TILE, CONV_TILE)
        taps = _conv_taps(x_ref[pl.ds(t0, n), :].astype(F32), n)
        pre = b_ref[...]
        for j in range(SSM_CONV):
            pre = pre + w_ref[j:j + 1, :] * taps[j]
        o_ref[...] = _silu_and_grad(pre[CONV_HALO:CONV_HALO + CONV_TILE])[0].astype(BF16)

    return pl.pallas_call(
        body, name=name, grid=(ncol, s // CONV_TILE),
        in_specs=[pl.BlockSpec((s + 2 * CONV_HALO, 128), lambda j, i: (0, j)),
                  pl.BlockSpec((SSM_CONV, 128), lambda j, i: (0, j)), pl.BlockSpec((1, 128), lambda j, i: (0, j))],
        out_specs=pl.BlockSpec((CONV_TILE, 128), lambda j, i: (i, j)),
        out_shape=jax.ShapeDtypeStruct((s, SSM_CONV_DIM), BF16), compiler_params=_params("parallel", "arbitrary"),
    )(xpad, w, b)


def _conv_bwd(xpad, dapad, w, b, *, name):
    s = xpad.shape[0] - 2 * CONV_HALO
    n = CONV_TILE + 2 * CONV_HALO
    ncol = SSM_CONV_DIM // 128
    mid = slice(CONV_HALO, CONV_HALO + CONV_TILE)

    def body(x_ref, da_ref, w_ref, b_ref, dx_ref, dw_ref, db_ref):
        @pl.when(pl.program_id(1) == 0)
        def _():
            dw_ref[...] = jnp.zeros_like(dw_ref)
            db_ref[...] = jnp.zeros_like(db_ref)

        t0 = pl.multiple_of(pl.program_id(1) * CONV_TILE, CONV_TILE)
        taps = _conv_taps(x_ref[pl.ds(t0, n), :].astype(F32), n)
        pre = b_ref[...]
        for j in range(SSM_CONV):
            pre = pre + w_ref[j:j + 1, :] * taps[j]
        dpre = da_ref[pl.ds(t0, n), :] * _silu_and_grad(pre)[1]
        dx = jnp.zeros((CONV_TILE, 128), F32)
        for j in range(SSM_CONV):
            back = dpre if j == 2 else pltpu.roll(dpre, (j - 2) % n, 0)
            dx = dx + w_ref[j:j + 1, :] * back[mid]
            dw_ref[j:j + 1, :] += jnp.sum(dpre[mid] * taps[j][mid], axis=0, keepdims=True)
        dx_ref[...] = dx.astype(BF16)
        db_ref[...] += jnp.sum(dpre[mid], axis=0, keepdims=True)

    full = pl.BlockSpec((s + 2 * CONV_HALO, 128), lambda j, i: (0, j))
    wspec = pl.BlockSpec((SSM_CONV, 128), lambda j, i: (0, j))
    bspec = pl.BlockSpec((1, 128), lambda j, i: (0, j))
    return pl.pallas_call(
        body, name=name, grid=(ncol, s // CONV_TILE), in_specs=[full, full, wspec, bspec],
        out_specs=[pl.BlockSpec((CONV_TILE, 128), lambda j, i: (i, j)), wspec, bspec],
        out_shape=[jax.ShapeDtypeStruct((s, SSM_CONV_DIM), BF16), jax.ShapeDtypeStruct((SSM_CONV, SSM_CONV_DIM), F32),
                   jax.ShapeDtypeStruct((1, SSM_CONV_DIM), F32)],
        compiler_params=_params("parallel", "arbitrary"),
    )(xpad, dapad, w, b)


HPG = SSM_HEADS // SSM_GROUPS
GW = HPG * SSM_HEAD_DIM
T = SSM_CHUNK


def _ssd_specs(nc):
    ceff = lambda d, c: jnp.where(d == 0, c, nc - 1 - c)
    return ceff, [
        pl.BlockSpec((T, GW), lambda d, g, c: (ceff(d, c), g)),
        pl.BlockSpec((T, SSM_STATE), lambda d, g, c: (ceff(d, c), SSM_INNER // 128 + g)),
        pl.BlockSpec((T, SSM_STATE), lambda d, g, c: (ceff(d, c), SSM_INNER // 128 + SSM_GROUPS + g)),
        pl.BlockSpec((None, None, T, HPG), lambda d, g, c: (d, g, ceff(d, c), 0)),
        pl.BlockSpec((None, None, HPG, T), lambda d, g, c: (d, g, 0, ceff(d, c))),
        pl.BlockSpec((None, None, 2, HPG), lambda d, g, c: (d, g, 0, 0)),
        pl.BlockSpec((None, None, HPG, 2), lambda d, g, c: (d, g, 0, 0)),
    ]


def _ssd_chunk_common(d, dt_ref, dtt_ref, prr_ref, prc_ref):
    sgn = 1 - 2 * d
    ri = lax.broadcasted_iota(jnp.int32, (T, T), 0)
    ci = lax.broadcasted_iota(jnp.int32, (T, T), 1)
    mask = ((ri - ci) * sgn) >= 0
    maskf = mask.astype(F32)
    bias_r, a_r = prr_ref[0:1, :], prr_ref[1:2, :]
    bias_c, a_c = prc_ref[:, 0:1], prc_ref[:, 1:2]
    raw = dt_ref[...] + bias_r
    dt_rows = _softplus(raw)
    dt_lanes = _softplus(dtt_ref[...] + bias_c)
    a_rows = dt_rows * a_r
    acum_rows = jnp.dot(maskf, a_rows, precision=HIGHEST, preferred_element_type=F32)
    acum_lanes = lax.dot_general(dt_lanes * a_c, maskf, _DIMS["nt"], precision=HIGHEST, preferred_element_type=F32)
    tot = jnp.sum(a_rows, axis=0, keepdims=True)
    return mask, maskf, raw, dt_rows, a_r, acum_rows, acum_lanes, tot


def _lanes_per_head(pieces):
    return jnp.concatenate([jnp.broadcast_to(p, (p.shape[0], SSM_HEAD_DIM)) for p in pieces], axis=1)


def _ssd_fwd(xbc, dtr, dtt, pr_rows, pr_cols, *, name):
    s = xbc.shape[0]
    nc = s // T
    ceff, in_specs = _ssd_specs(nc)

    def body(x_ref, b_ref, c_ref, dt_ref, dtt_ref, prr_ref, prc_ref, y_ref, hs_ref, st_ref):
        d, c = pl.program_id(0), pl.program_id(2)

        @pl.when(c == 0)
        def _():
            st_ref[...] = jnp.zeros_like(st_ref)

        mask, _, _, dt_rows, _, acum_rows, acum_lanes, tot = _ssd_chunk_common(d, dt_ref, dtt_ref, prr_ref, prc_ref)
        xs = x_ref[...].astype(F32)
        bm, cm = b_ref[...], c_ref[...]
        hprev = st_ref[...]
        hs_ref[...] = hprev
        cb = lax.dot_general(cm, bm, _DIMS["nt"], preferred_element_type=F32)
        ch = jnp.dot(cm, hprev.astype(BF16), preferred_element_type=F32)
        ys, xgds, etots = [], [], []
        for j in range(HPG):
            sl = slice(j * SSM_HEAD_DIM, (j + 1) * SSM_HEAD_DIM)
            ac, al = acum_rows[:, j:j + 1], acum_lanes[j:j + 1, :]
            lm = jnp.where(mask, jnp.exp(jnp.minimum(ac - al, 0.0)), 0.0)
            xg = xs[:, sl] * dt_rows[:, j:j + 1]
            yd = jnp.dot((cb * lm).astype(BF16), xg.astype(BF16), preferred_element_type=F32)
            ys.append(yd + jnp.exp(ac) * ch[:, sl])
            xgds.append(xg * jnp.exp(tot[:, j:j + 1] - ac))
            etots.append(jnp.exp(tot[:, j:j + 1]))
        y_ref[...] = jnp.concatenate(ys, axis=1)
        new = lax.dot_general(bm, jnp.concatenate(xgds, axis=1).astype(BF16), _DIMS["tn"], preferred_element_type=F32)
        st_ref[...] = hprev * _lanes_per_head(etots) + new

    return pl.pallas_call(
        body, name=name, grid=(2, SSM_GROUPS, nc), in_specs=in_specs,
        out_specs=[pl.BlockSpec((None, T, GW), lambda d, g, c: (d, ceff(d, c), g)),
                   pl.BlockSpec((None, None, None, SSM_STATE, GW), lambda d, g, c: (d, ceff(d, c), g, 0, 0))],
        out_shape=[jax.ShapeDtypeStruct((2, s, SSM_INNER), F32),
                   jax.ShapeDtypeStruct((2, nc, SSM_GROUPS, SSM_STATE, GW), F32)],
        scratch_shapes=[pltpu.VMEM((SSM_STATE, GW), F32)],
        compiler_params=_params("parallel", "parallel", "arbitrary"),
    )(xbc, xbc, xbc, dtr, dtt, pr_rows, pr_cols)


def _put_lane(j, col):
    lane = lax.broadcasted_iota(jnp.int32, (col.shape[0], HPG), 1)
    return jnp.where(lane == j, col, 0.0)


def _ssd_bwd(xbc, dtr, dtt, pr_rows, pr_cols, dvec, hs, dy, *, name):
    s = xbc.shape[0]
    nc = s // T
    cb_of = lambda d, c: jnp.where(d == 0, nc - 1 - c, c)
    in_specs = [
        pl.BlockSpec((T, GW), lambda d, g, c: (cb_of(d, c), g)),
        pl.BlockSpec((T, SSM_STATE), lambda d, g, c: (cb_of(d, c), SSM_INNER // 128 + g)),
        pl.BlockSpec((T, SSM_STATE), lambda d, g, c: (cb_of(d, c), SSM_INNER // 128 + SSM_GROUPS + g)),
        pl.BlockSpec((None, None, T, HPG), lambda d, g, c: (d, g, cb_of(d, c), 0)),
        pl.BlockSpec((None, None, HPG, T), lambda d, g, c: (d, g, 0, cb_of(d, c))),
        pl.BlockSpec((None, None, 2, HPG), lambda d, g, c: (d, g, 0, 0)),
        pl.BlockSpec((None, None, HPG, 2), lambda d, g, c: (d, g, 0, 0)),
        pl.BlockSpec((1, GW), lambda d, g, c: (0, g)),
        pl.BlockSpec((None, None, None, SSM_STATE, GW), lambda d, g, c: (d, cb_of(d, c), g, 0, 0)),
        pl.BlockSpec((T, GW), lambda d, g, c: (cb_of(d, c), g)),
    ]

    def body(x_ref, b_ref, c_ref, dt_ref, dtt_ref, prr_ref, prc_ref, dvec_ref, hs_ref, dy_ref,
             dxs_ref, db_ref, dc_ref, ddt_ref, dalog_ref, dbias_ref, g_ref):
        d, c = pl.program_id(0), pl.program_id(2)

        @pl.when(c == 0)
        def _():
            g_ref[...] = jnp.zeros_like(g_ref)
            dalog_ref[...] = jnp.zeros_like(dalog_ref)
            dbias_ref[...] = jnp.zeros_like(dbias_ref)

        mask, maskf, raw, dt_rows, a_r, acum_rows, acum_lanes, tot = _ssd_chunk_common(
            d, dt_ref, dtt_ref, prr_ref, prc_ref)
        xs = x_ref[...].astype(F32)
        bm, cm = b_ref[...], c_ref[...]
        hst = hs_ref[...]
        gst = g_ref[...]
        dyv = dy_ref[...]
        dyb = dyv.astype(BF16)
        dv = dvec_ref[...] * (1 - d).astype(F32)
        cb = lax.dot_general(cm, bm, _DIMS["nt"], preferred_element_type=F32)
        ch = jnp.dot(cm, hst.astype(BF16), preferred_element_type=F32)
        bg = jnp.dot(bm, gst.astype(BF16), preferred_element_type=F32)
        hg = hst * gst
        dcb = jnp.zeros((T, T), F32)
        dacum = jnp.zeros((T, HPG), F32)
        rx = jnp.zeros((T, HPG), F32)
        dtot = jnp.zeros((1, HPG), F32)
        dxss, dyes, xgds, etots = [], [], [], []
        for j in range(HPG):
            sl = slice(j * SSM_HEAD_DIM, (j + 1) * SSM_HEAD_DIM)
            ac, al = acum_rows[:, j:j + 1], acum_lanes[j:j + 1, :]
            lm = jnp.where(mask, jnp.exp(jnp.minimum(ac - al, 0.0)), 0.0)
            m = cb * lm
            dtj = dt_rows[:, j:j + 1]
            xsj = xs[:, sl]
            xg = xsj * dtj
            dyj = dyv[:, sl]
            ec = jnp.exp(ac)
            etot = jnp.exp(tot[:, j:j + 1])
            decay = jnp.exp(tot[:, j:j + 1] - ac)
            dm = lax.dot_general(dyb[:, sl], xg.astype(BF16), _DIMS["nt"], preferred_element_type=F32)
            w = dm * m
            dcb = dcb + dm * lm
            bgj = bg[:, sl]
            dxg = lax.dot_general(m.astype(BF16), dyb[:, sl], _DIMS["tn"], preferred_element_type=F32) + decay * bgj
            xb = decay * jnp.sum(xg * bgj, axis=-1, keepdims=True)
            da_j = (jnp.sum(w, axis=-1, keepdims=True) - jnp.sum(w.T, axis=-1, keepdims=True)
                    + jnp.sum(ec * ch[:, sl] * dyj, axis=-1, keepdims=True) - xb)
            dacum = dacum + _put_lane(j, da_j)
            rx = rx + _put_lane(j, jnp.sum(dxg * xsj, axis=-1, keepdims=True))
            dtot_j = (etot * jnp.sum(jnp.sum(hg[:, sl], axis=0, keepdims=True), axis=1, keepdims=True)
                      + jnp.sum(xb, axis=0, keepdims=True))
            dtot = dtot + _put_lane(j, dtot_j)
            dxss.append(dxg * dtj + dv[:, sl] * dyj)
            dyes.append(dyj * ec)
            xgds.append(xg * decay)
            etots.append(etot)
        da = lax.dot_general(maskf, dacum, _DIMS["tn"], precision=HIGHEST, preferred_element_type=F32) + dtot
        ddt = da * a_r + rx
        draw = ddt * _sigmoid(raw)
        ddt_ref[...] = draw
        dbias_ref[...] += jnp.sum(draw, axis=0, keepdims=True)
        dalog_ref[...] += jnp.sum(da * dt_rows, axis=0, keepdims=True) * a_r
        dxs_ref[...] = jnp.concatenate(dxss, axis=1)
        dye = jnp.concatenate(dyes, axis=1).astype(BF16)
        xgd = jnp.concatenate(xgds, axis=1).astype(BF16)
        dcbb = dcb.astype(BF16)
        dc_ref[...] = (jnp.dot(dcbb, bm, preferred_element_type=F32)
                       + lax.dot_general(dye, hst.astype(BF16), _DIMS["nt"], preferred_element_type=F32))
        db_ref[...] = (lax.dot_general(dcbb, cm, _DIMS["tn"], preferred_element_type=F32)
                       + lax.dot_general(xgd, gst.astype(BF16), _DIMS["nt"], preferred_element_type=F32))
        g_ref[...] = lax.dot_general(cm, dye, _DIMS["tn"], preferred_element_type=F32) + gst * _lanes_per_head(etots)

    small = pl.BlockSpec((None, None, 1, HPG), lambda d, g, c: (d, g, 0, 0))
    sshape = jax.ShapeDtypeStruct((2, SSM_GROUPS, 1, HPG), F32)
    return pl.pallas_call(
        body, name=name, grid=(2, SSM_GROUPS, nc), in_specs=in_specs,
        out_specs=[pl.BlockSpec((None, T, GW), lambda d, g, c: (d, cb_of(d, c), g)),
                   pl.BlockSpec((None, T, SSM_STATE), lambda d, g, c: (d, cb_of(d, c), g)),
                   pl.BlockSpec((None, T, SSM_STATE), lambda d, g, c: (d, cb_of(d, c), g)),
                   pl.BlockSpec((None, None, T, HPG), lambda d, g, c: (d, g, cb_of(d, c), 0)), small, small],
        out_shape=[jax.ShapeDtypeStruct((2, s, SSM_INNER), F32),
                   jax.ShapeDtypeStruct((2, s, SSM_GROUPS * SSM_STATE), F32),
                   jax.ShapeDtypeStruct((2, s, SSM_GROUPS * SSM_STATE), F32),
                   jax.ShapeDtypeStruct((2, SSM_GROUPS, s, HPG), F32), sshape, sshape],
        scratch_shapes=[pltpu.VMEM((SSM_STATE, GW), F32)],
        compiler_params=_params("parallel", "parallel", "arbitrary"),
    )(xbc, xbc, xbc, dtr, dtt, pr_rows, pr_cols, dvec, hs, dy)


def _gate_norm_fwd(y2, xbc, proj, dvec, nw, *, name):
    s = xbc.shape[0]
    tr = 256

    def body(y_ref, xs_ref, z_ref, dv_ref, w_ref, u_ref):
        yt = y_ref[0] + y_ref[1] + dv_ref[...] * xs_ref[...].astype(F32)
        yg = yt * _silu_and_grad(z_ref[...].astype(F32))[0]
        u_ref[...] = (yg * lax.rsqrt(jnp.mean(yg * yg, axis=-1, keepdims=True) + RMS_EPS) * w_ref[...]).astype(BF16)

    row = pl.BlockSpec((tr, SSM_INNER), lambda i: (i, 0))
    vec = pl.BlockSpec((1, SSM_INNER), lambda i: (0, 0))
    return pl.pallas_call(
        body, name=name, grid=(s // tr,),
        in_specs=[pl.BlockSpec((2, tr, SSM_INNER), lambda i: (0, i, 0)), row, row, vec, vec], out_specs=row,
        out_shape=jax.ShapeDtypeStruct((s, SSM_INNER), BF16), compiler_params=_params("parallel"),
    )(y2, xbc, proj, dvec, nw)


def _gate_norm_bwd(du, y2, xbc, proj, dvec, nw, *, name):
    s = xbc.shape[0]
    tr = 256

    def body(du_ref, y_ref, xs_ref, z_ref, dv_ref, w_ref, dy_ref, dz_ref, dw_ref, dd_ref):
        @pl.when(pl.program_id(0) == 0)
        def _():
            dw_ref[...] = jnp.zeros_like(dw_ref)
            dd_ref[...] = jnp.zeros_like(dd_ref)

        xs = xs_ref[...].astype(F32)
        yt = y_ref[0] + y_ref[1] + dv_ref[...] * xs
        si, dsi = _silu_and_grad(z_ref[...].astype(F32))
        yg = yt * si
        rstd = lax.rsqrt(jnp.mean(yg * yg, axis=-1, keepdims=True) + RMS_EPS)
        yhat = yg * rstd
        du = du_ref[...]
        dyn = du * w_ref[...]
        dyg = rstd * (dyn - yhat * jnp.mean(dyn * yhat, axis=-1, keepdims=True))
        dyt = dyg * si
        dy_ref[...] = dyt
        dz_ref[...] = (dyg * yt * dsi).astype(BF16)
        dw_ref[...] += jnp.sum(du * yhat, axis=0, keepdims=True)
        dd_ref[...] += jnp.sum(dyt * xs, axis=0, keepdims=True)

    row = pl.BlockSpec((tr, SSM_INNER), lambda i: (i, 0))
    vec = pl.BlockSpec((1, SSM_INNER), lambda i: (0, 0))
    vshape = jax.ShapeDtypeStruct((1, SSM_INNER), F32)
    return pl.pallas_call(
        body, name=name, grid=(s // tr,),
        in_specs=[row, pl.BlockSpec((2, tr, SSM_INNER), lambda i: (0, i, 0)), row, row, vec, vec],
        out_specs=[row, row, vec, vec],
        out_shape=[jax.ShapeDtypeStruct((s, SSM_INNER), F32), jax.ShapeDtypeStruct((s, SSM_INNER), BF16), vshape, vshape],
        compiler_params=_params("arbitrary"),
    )(du, y2, xbc, proj, dvec, nw)


def _layer_a_fwd(x, mod, w_in, w_out, ln_g, ln_b, tag):
    shift, scale, gate = mod
    h = _modulate(x, scale, shift, name=f"{tag}_modulate")
    proj = _mm(h, w_in, mode="nn", out_dtype=BF16, tm=512, tn=1024, tk=1024, name=f"{tag}_mm_in")
    os_, lses = [], []
    for grp in range(3):
        o, l = _attn_fwd(proj, grp, name=f"{tag}_attn_fwd{grp}")
        os_.append(o)
        lses.append(l)
    y, o, lse = _attn_combine(os_, lses, proj, name=f"{tag}_combine")
    out = _mm(y, w_out, mode="nn", out_dtype=F32, tm=512, tn=1024, tk=1024, name=f"{tag}_mm_out")
    xn = _resid_ln_fwd(x, out, gate, ln_g, ln_b, name=f"{tag}_resid_ln")
    return xn, (x, h, proj, y, o, lse, out)


def _layer_a_bwd(dxn, saved, mod, w_in, w_out, ln_g, tag):
    x, h, proj, y, o, lse, out = saved
    shift, scale, gate = mod
    dx_part, dout, dgate, dln_g, dln_b = _resid_ln_bwd(x, out, gate, ln_g, dxn, name=f"{tag}_resid_ln_bwd")
    dw_out = _mm(y, dout, mode="tn", out_dtype=F32, tm=1024, tn=1024, tk=512, name=f"{tag}_mm_dw_out")
    dy = _mm(dout, w_out, mode="nt", out_dtype=F32, tm=512, tn=1024, tk=1024, name=f"{tag}_mm_dy")
    do, dgp = _attn_combine_bwd(dy, o, proj, name=f"{tag}_combine_bwd")
    parts = []
    for grp in range(3):
        parts.extend(_attn_bwd(proj, grp, do, o, lse, name=f"{tag}_attn_bwd{grp}"))
    dproj = jnp.concatenate(parts + [dgp], axis=1)
    dw_in = _mm(h, dproj, mode="tn", out_dtype=F32, tm=1024, tn=1024, tk=512, name=f"{tag}_mm_dw_in")
    dx, dscale, dshift = _mm_dh(dproj, w_in, dx_part, x, scale, tm=512, tk=2048, name=f"{tag}_mm_dh")
    grads = dict(w_in=dw_in, w_out=dw_out, ln_g=dln_g, ln_b=dln_b, mod=jnp.concatenate([dshift, dscale, dgate], axis=1))
    return dx, grads


def _ssd_param_views(dt_raw, dt_bias, a_log):
    s = dt_raw.shape[0]
    r4 = dt_raw.reshape(s, 2, SSM_GROUPS, HPG)
    dtr = r4.transpose(1, 2, 0, 3)
    dtt = r4.transpose(1, 2, 3, 0)
    a = -jnp.exp(a_log)
    pr_rows = jnp.stack([dt_bias.reshape(2, SSM_GROUPS, HPG), a.reshape(2, SSM_GROUPS, HPG)], axis=2)
    return dtr, dtt, pr_rows, pr_rows.transpose(0, 1, 3, 2)


def _layer_b_fwd(x, mod, p, ln_g, ln_b, tag):
    shift, scale, gate = mod
    s = x.shape[0]
    h = _modulate(x, scale, shift, name=f"{tag}_modulate")
    proj = _mm(h, p["w_in"][:, :SSM_MAIN_COLS], mode="nn", out_dtype=BF16, tm=512, tn=1024, tk=1024, name=f"{tag}_mm_in")
    dt_raw = _mm(h, p["w_in"][:, SSM_MAIN_COLS:SSM_IN_COLS], mode="nn", out_dtype=F32, tm=512, tn=64, tk=1024,
                 name=f"{tag}_mm_dt")
    xpad = jnp.pad(proj[:, SSM_INNER:], ((CONV_HALO, CONV_HALO), (0, 0)))
    xbc = _conv_fwd(xpad, p["conv_w"], p["conv_b"], name=f"{tag}_conv")
    views = _ssd_param_views(dt_raw, p["dt_bias"], p["a_log"])
    y2, hs = _ssd_fwd(xbc, *views, name=f"{tag}_ssd_fwd")
    u = _gate_norm_fwd(y2, xbc, proj, p["dvec"], p["norm_w"], name=f"{tag}_gate_norm")
    out = _mm(u, p["w_out"], mode="nn", out_dtype=F32, tm=512, tn=1024, tk=2048, name=f"{tag}_mm_out")
    xn = _resid_ln_fwd(x, out, gate, ln_g, ln_b, name=f"{tag}_resid_ln")
    return xn, (x, h, proj, xpad, xbc, views, y2, hs, u, out)


def _layer_b_bwd(dxn, saved, mod, p, ln_g, tag):
    x, h, proj, xpad, xbc, views, y2, hs, u, out = saved
    shift, scale, gate = mod
    s = x.shape[0]
    dx_part, dout, dgate, dln_g, dln_b = _resid_ln_bwd(x, out, gate, ln_g, dxn, name=f"{tag}_resid_ln_bwd")
    dw_out = _mm(u, dout, mode="tn", out_dtype=F32, tm=1024, tn=1024, tk=512, name=f"{tag}_mm_dw_out")
    du = _mm(dout, p["w_out"], mode="nt", out_dtype=F32, tm=512, tn=1024, tk=1024, name=f"{tag}_mm_du")
    dy, dz, dnorm_w, dd_lanes = _gate_norm_bwd(du, y2, xbc, proj, p["dvec"], p["norm_w"], name=f"{tag}_gate_norm_bwd")
    dxs2, db2, dc2, ddt4, dalog, dbias = _ssd_bwd(xbc, *views, p["dvec"], hs, dy, name=f"{tag}_ssd_bwd")
    dact = jnp.concatenate([dxs2[0] + dxs2[1], db2[0] + db2[1], dc2[0] + dc2[1]], axis=1)
    dapad = jnp.pad(dact, ((CONV_HALO, CONV_HALO), (0, 0)))
    dxbc, dconv_w, dconv_b = _conv_bwd(xpad, dapad, p["conv_w"], p["conv_b"], name=f"{tag}_conv_bwd")
    ddt_raw = ddt4.transpose(2, 0, 1, 3).reshape(s, 2 * SSM_HEADS).astype(BF16)
    dproj = jnp.concatenate([dz, dxbc, ddt_raw, jnp.zeros((s, SSM_PAD_COLS - SSM_IN_COLS), BF16)], axis=1)
    dw_in = _mm(h, dproj, mode="tn", out_dtype=F32, tm=1024, tn=896, tk=512, name=f"{tag}_mm_dw_in")[:, :SSM_IN_COLS]
    w_pad = jnp.pad(p["w_in"], ((0, 0), (0, SSM_PAD_COLS - SSM_IN_COLS)))
    dx, dscale, dshift = _mm_dh(dproj, w_pad, dx_part, x, scale, tm=512, tk=1792, name=f"{tag}_mm_dh")
    grads = dict(
        w_in=dw_in, w_out=dw_out, ln_g=dln_g, ln_b=dln_b, mod=jnp.concatenate([dshift, dscale, dgate], axis=1),
        conv_w=dconv_w, conv_b=dconv_b, norm_w=dnorm_w, dt_bias=dbias.reshape(2, SSM_HEADS),
        a_log=dalog.reshape(2, SSM_HEADS), d=jnp.sum(dd_lanes.reshape(SSM_HEADS, SSM_HEAD_DIM), axis=1))
    return dx, grads


def _local_step(x, target, mods, ln_g, ln_b, a_w_in, a_w_out, b_params):
    saved = []
    for i in range(DEPTH):
        j = i // 2
        g, b = ln_g[i:i + 1], ln_b[i:i + 1]
        if i % 2 == 0:
            x, sv = _layer_a_fwd(x, mods[i], a_w_in[j], a_w_out[j], g, b, f"l{i}")
        else:
            x, sv = _layer_b_fwd(x, mods[i], b_params[j], g, b, f"l{i}")
        saved.append(sv)
    dx, loss = _loss_and_grad(x, target, name="loss")
    grads = [None] * DEPTH
    for i in reversed(range(DEPTH)):
        j = i // 2
        g = ln_g[i:i + 1]
        if i % 2 == 0:
            dx, grads[i] = _layer_a_bwd(dx, saved[i], mods[i], a_w_in[j], a_w_out[j], g, f"l{i}")
        else:
            dx, grads[i] = _layer_b_bwd(dx, saved[i], mods[i], b_params[j], g, f"l{i}")
    return loss, dx, grads


def _mesh_pos():
    return lax.axis_index("x"), lax.axis_index("y"), lax.axis_index("c")


def _all_gather(x, *, name):
    def body(x_ref, out_ref, send_sems, recv_sems, local_sem):
        ax, ay, ac = _mesh_pos()
        me, sibling = (ax, ay, ac), (ax, ay, 1 - ac)
        chips = [(1 - ax, ay), (ax, 1 - ay), (1 - ax, 1 - ay)]

        def slot(px, py, pc):
            return out_ref.at[4 * px + 2 * py + pc]

        def copy(k, block, to, src=None):
            return pltpu.make_async_remote_copy(
                src_ref=slot(*block) if src is None else src, dst_ref=slot(*block),
                send_sem=send_sems.at[k], recv_sem=recv_sems.at[k], device_id=to, device_id_type=MESH)

        mine = pltpu.make_async_copy(x_ref, slot(*me), local_sem)
        mine.start()
        first = [copy(0, me, sibling, src=x_ref)]
        first += [copy(1 + j, me, (*chip, ac), src=x_ref) for j, chip in enumerate(chips)]
        for cp in first:
            cp.start()
        passed = [copy(4 + j, (*chip, ac), sibling) for j, chip in enumerate(chips)]
        for j, chip in enumerate(chips):
            copy(1 + j, (*chip, ac), me).wait_recv()
            passed[j].start()
        copy(0, sibling, me).wait_recv()
        for j, chip in enumerate(chips):
            copy(4 + j, (*chip, 1 - ac), me).wait_recv()
        for cp in first + passed:
            cp.wait_send()
        mine.wait()

    return pl.pallas_call(
        body, name=name, out_shape=jax.ShapeDtypeStruct((N_DEV,) + x.shape, x.dtype),
        in_specs=[pl.BlockSpec(memory_space=pl.ANY)], out_specs=pl.BlockSpec(memory_space=pl.ANY),
        scratch_shapes=[pltpu.SemaphoreType.DMA((7,)), pltpu.SemaphoreType.DMA((7,)), pltpu.SemaphoreType.DMA],
    )(x)


def _all_to_all(x, *, name):
    def body(x_ref, out_ref, send_sems, recv_sems, local_sem):
        ax, ay, ac = _mesh_pos()
        me = 4 * ax + 2 * ay + ac
        mine = pltpu.make_async_copy(x_ref.at[me], out_ref.at[me], local_sem)
        mine.start()
        copies = []
        for k in range(1, N_DEV):
            px = 1 - ax if k & 4 else ax
            py = 1 - ay if k & 2 else ay
            pc = 1 - ac if k & 1 else ac
            copies.append(pltpu.make_async_remote_copy(
                src_ref=x_ref.at[4 * px + 2 * py + pc], dst_ref=out_ref.at[me],
                send_sem=send_sems.at[k - 1], recv_sem=recv_sems.at[k - 1], device_id=(px, py, pc), device_id_type=MESH))
        for cp in copies:
            cp.start()
        for cp in copies:
            cp.wait()
        mine.wait()

    return pl.pallas_call(
        body, name=name, out_shape=jax.ShapeDtypeStruct(x.shape, x.dtype),
        in_specs=[pl.BlockSpec(memory_space=pl.ANY)], out_specs=pl.BlockSpec(memory_space=pl.ANY),
        scratch_shapes=[pltpu.SemaphoreType.DMA((7,)), pltpu.SemaphoreType.DMA((7,)), pltpu.SemaphoreType.DMA],
    )(x)


ADA_LOCAL = 3 * D_MODEL // N_DEV


def _ada_mod(c_all, ada_w, ada_b_local, *, name):
    def body(c_ref, w_ref, b_ref, o_ref):
        cond = _silu_and_grad(c_ref[...])[0]
        o_ref[...] = jnp.dot(cond, w_ref[...], precision=HIGHEST, preferred_element_type=F32) + b_ref[...]

    return pl.pallas_call(
        body, name=name, grid=(DEPTH,),
        in_specs=[pl.BlockSpec((N_DEV, D_MODEL), lambda i: (0, 0)), pl.BlockSpec((None, D_MODEL, ADA_LOCAL), lambda i: (i, 0, 0)),
                  pl.BlockSpec((None, 1, ADA_LOCAL), lambda i: (i, 0, 0))],
        out_specs=pl.BlockSpec((None, N_DEV, ADA_LOCAL), lambda i: (i, 0, 0)),
        out_shape=jax.ShapeDtypeStruct((DEPTH, N_DEV, ADA_LOCAL), F32), compiler_params=_params("parallel"),
    )(c_all, ada_w, ada_b_local)


def _ada_grad(c_all_t, dmod_local, *, name):
    def body(ct_ref, dm_ref, o_ref):
        cond_t = _silu_and_grad(ct_ref[...])[0]
        dm = dm_ref[...]
        acc = cond_t[:, 0:1] * dm[0:1, :]
        for smp in range(1, N_DEV):
            acc = acc + cond_t[:, smp:smp + 1] * dm[smp:smp + 1, :]
        o_ref[...] = acc

    return pl.pallas_call(
        body, name=name, grid=(DEPTH,),
        in_specs=[pl.BlockSpec((D_MODEL, N_DEV), lambda i: (0, 0)), pl.BlockSpec((None, N_DEV, ADA_LOCAL), lambda i: (i, 0, 0))],
        out_specs=pl.BlockSpec((None, D_MODEL, ADA_LOCAL), lambda i: (i, 0, 0)),
        out_shape=jax.ShapeDtypeStruct((DEPTH, D_MODEL, ADA_LOCAL), F32), compiler_params=_params("parallel"),
    )(c_all_t, dmod_local)


def _sum_devices(parts, *, name):
    n = parts.shape[1]

    def body(p_ref, o_ref):
        acc = p_ref[0:1, :]
        for dev in range(1, N_DEV):
            acc = acc + p_ref[dev:dev + 1, :]
        o_ref[...] = acc

    return pl.pallas_call(
        body, name=name, out_shape=jax.ShapeDtypeStruct((1, n), F32),
        in_specs=[pl.BlockSpec(memory_space=pltpu.VMEM)], out_specs=pl.BlockSpec(memory_space=pltpu.VMEM),
        compiler_params=pltpu.CompilerParams(vmem_limit_bytes=VMEM_LIMIT_BYTES),
    )(parts)


ADAMW_VMEM_BYTES = 24 * 1024 * 1024


def _adamw(w, m, v, g, *, name):
    r, c = w.shape
    summed = g.ndim == 3
    tr = r
    arrays = 7 + (N_DEV if summed else 1)
    while tr % 16 == 0 and 2 * arrays * tr * c * 4 > ADAMW_VMEM_BYTES:
        tr //= 2

    def body(w_ref, m_ref, v_ref, g_ref, go_ref, d_ref, mo_ref, vo_ref):
        if summed:
            g = g_ref[0]
            for dev in range(1, N_DEV):
                g = g + g_ref[dev]
        else:
            g = g_ref[...]
        mn = ADAM_B1 * m_ref[...] + (1.0 - ADAM_B1) * g
        vn = ADAM_B2 * v_ref[...] + (1.0 - ADAM_B2) * (g * g)
        m_hat = mn / (1.0 - ADAM_B1 ** ADAM_STEP)
        v_hat = vn / (1.0 - ADAM_B2 ** ADAM_STEP)
        go_ref[...] = g
        d_ref[...] = -ADAM_LR * (m_hat / (jnp.sqrt(v_hat) + ADAM_EPS) + ADAM_WD * w_ref[...])
        mo_ref[...] = mn
        vo_ref[...] = vn

    row = pl.BlockSpec((tr, c), lambda i: (i, 0))
    gspec = pl.BlockSpec((N_DEV, tr, c), lambda i: (0, i, 0)) if summed else row
    shp = jax.ShapeDtypeStruct((r, c), F32)
    return pl.pallas_call(
        body, name=name, grid=(r // tr,), in_specs=[row, row, row, gspec], out_specs=[row] * 4, out_shape=[shp] * 4,
        compiler_params=_params("parallel"),
    )(w, m, v, g)


def _pack(arrays):
    flat = jnp.concatenate([a.reshape(-1) for a in arrays])
    n = flat.shape[0]
    return jnp.pad(flat, (0, -n % 128)).reshape(1, -1)


def _unpack(vec, shapes):
    out, at = [], 0
    for shp in shapes:
        n = math.prod(shp)
        out.append(vec[at:at + n].reshape(shp))
        at += n
    return out


def _unpack_rows(rows, shapes):
    out, at = [], 0
    for shp in shapes:
        n = math.prod(shp)
        out.append(rows[:, at:at + n].reshape((rows.shape[0],) + tuple(shp)))
        at += n
    return out


def _my_shard(full, me, axis):
    width = full.shape[axis] // N_DEV
    return lax.dynamic_slice_in_dim(full, me * width, width, axis)


def _gather_cols(g, lead):
    nd = g.ndim
    perm = tuple(range(1, nd - 1)) + (0, nd - 1)
    t = g.transpose(perm)
    return t.reshape(t.shape[:-2] + (t.shape[-2] * t.shape[-1],))


def kernel(x, c, ada_w, ada_b, ln_g, ln_b, a_w_in, a_w_out, b_w_in, b_conv_w, b_conv_b, b_dt_bias, b_a_log, b_d, b_norm_w, b_w_out, loss_target, m_ada_w, m_ada_b, m_ln_g, m_ln_b, m_a_w_in, m_a_w_out, m_b_w_in, m_b_conv_w, m_b_conv_b, m_b_dt_bias, m_b_a_log, m_b_d, m_b_norm_w, m_b_w_out, v_ada_w, v_ada_b, v_ln_g, v_ln_b, v_a_w_in, v_a_w_out, v_b_w_in, v_b_conv_w, v_b_conv_b, v_b_dt_bias, v_b_a_log, v_b_d, v_b_norm_w, v_b_w_out):
    ax, ay, ac = _mesh_pos()
    me = 4 * ax + 2 * ay + ac
    seq = x.shape[1]

    small_shapes = [(1, D_MODEL), (2, SSM_CONV, ADA_LOCAL), (2, ADA_LOCAL), (2, SSM_INNER // N_DEV)]
    sg = _all_gather(_pack([c, b_conv_w, b_conv_b, b_norm_w]), name="gather_small")[:, 0, :]
    c_all, conv_w_g, conv_b_g, norm_w_g = _unpack_rows(sg, small_shapes)
    c_all = c_all[:, 0, :]
    conv_w = _gather_cols(conv_w_g, 2)
    conv_b = _gather_cols(conv_b_g[:, :, None, :], 2)
    norm_w = _gather_cols(norm_w_g[:, :, None, :], 2)

    def gathered(w, name):
        lead = w.shape[0]
        flat = w.astype(BF16).reshape(lead * w.shape[1], w.shape[2])
        return _all_gather(flat, name=name).reshape(N_DEV, lead, w.shape[1], w.shape[2])

    a_w_in_f = _gather_cols(gathered(a_w_in, "gather_a_w_in"), 2)
    b_w_in_f = _gather_cols(gathered(b_w_in, "gather_b_w_in"), 2)
    a_w_out_f = gathered(a_w_out, "gather_a_w_out").transpose(1, 0, 2, 3).reshape(2, A_WIDTH, D_MODEL)
    b_w_out_f = gathered(b_w_out, "gather_b_w_out").transpose(1, 0, 2, 3).reshape(2, SSM_INNER, D_MODEL)

    ada_b_local = _my_shard(ada_b, me, 1)[:, None, :]
    mod_cols = _ada_mod(c_all, ada_w, ada_b_local, name="ada_mod")
    mod_g = _all_gather(mod_cols.reshape(1, -1), name="gather_mod").reshape(N_DEV, DEPTH, N_DEV, ADA_LOCAL)
    mod = lax.dynamic_index_in_dim(mod_g, me, axis=2, keepdims=False).transpose(1, 0, 2).reshape(DEPTH, 3 * D_MODEL)
    mods = [tuple(mod[i:i + 1, k * D_MODEL:(k + 1) * D_MODEL] for k in range(3)) for i in range(DEPTH)]

    b_params = [dict(w_in=b_w_in_f[j], w_out=b_w_out_f[j], conv_w=conv_w[j], conv_b=conv_b[j], norm_w=norm_w[j],
                     dt_bias=b_dt_bias[j], a_log=b_a_log[j], dvec=jnp.repeat(b_d[j], SSM_HEAD_DIM)[None, :])
                for j in range(2)]
    loss_lanes, dx, grads = _local_step(x[0], loss_target[0], mods, ln_g, ln_b, a_w_in_f, a_w_out_f, b_params)
    loss = lax.psum(loss_lanes[0, 0], ("x", "y", "c"))
    grad_x = dx[None]

    a_layers, b_layers = (grads[0], grads[2]), (grads[1], grads[3])
    part_shapes = [(DEPTH, 3 * D_MODEL), (DEPTH, D_MODEL), (DEPTH, D_MODEL), (2, SSM_CONV, SSM_CONV_DIM),
                   (2, SSM_CONV_DIM), (2, SSM_INNER), (2, 2, SSM_HEADS), (2, 2, SSM_HEADS), (2, SSM_HEADS)]
    parts = _pack([
        jnp.concatenate([g["mod"] for g in grads]), jnp.concatenate([g["ln_g"] for g in grads]),
        jnp.concatenate([g["ln_b"] for g in grads]), jnp.stack([g["conv_w"] for g in b_layers]),
        jnp.stack([g["conv_b"][0] for g in b_layers]), jnp.stack([g["norm_w"][0] for g in b_layers]),
        jnp.stack([g["dt_bias"] for g in b_layers]), jnp.stack([g["a_log"] for g in b_layers]),
        jnp.stack([g["d"] for g in b_layers])])
    parts_g = _all_gather(parts, name="gather_small_grads")[:, 0, :]
    (g_ada_b, g_ln_g, g_ln_b, g_conv_w, g_conv_b, g_norm_w, g_dt_bias, g_a_log, g_d) = _unpack(
        _sum_devices(parts_g, name="sum_small_grads")[0], part_shapes)
    dmod_all = parts_g[:, :DEPTH * 3 * D_MODEL].reshape(N_DEV, DEPTH, N_DEV, ADA_LOCAL)
    dmod_local = lax.dynamic_index_in_dim(dmod_all, me, axis=2, keepdims=False).transpose(1, 0, 2)
    g_ada_w = _ada_grad(c_all.T, dmod_local, name="ada_grad")

    def scatter_cols(per_layer, name):
        st = jnp.stack(per_layer)
        lead, r, cols = st.shape
        blocks = st.reshape(lead, r, N_DEV, cols // N_DEV).transpose(2, 0, 1, 3).reshape(N_DEV, lead * r, cols // N_DEV)
        return _all_to_all(blocks, name=name)

    def scatter_rows(per_layer, name):
        st = jnp.stack(per_layer)
        lead, rows, cols = st.shape
        blocks = st.reshape(lead, N_DEV, rows // N_DEV, cols).transpose(1, 0, 2, 3).reshape(N_DEV, lead * rows // N_DEV, cols)
        return _all_to_all(blocks, name=name)

    r_a_w_in = scatter_cols([g["w_in"] for g in a_layers], "scatter_a_w_in")
    r_a_w_out = scatter_rows([g["w_out"] for g in a_layers], "scatter_a_w_out")
    r_b_w_in = scatter_cols([g["w_in"] for g in b_layers], "scatter_b_w_in")
    r_b_w_out = scatter_rows([g["w_out"] for g in b_layers], "scatter_b_w_out")

    def update(w, m, v, g, name):
        two_d = (-1, w.shape[-1])
        outs = _adamw(w.reshape(two_d), m.reshape(two_d), v.reshape(two_d), g, name=name)
        return [o.reshape(w.shape) for o in outs]

    up_ada_w = update(ada_w, m_ada_w, v_ada_w, g_ada_w.reshape(-1, ADA_LOCAL), "adamw_ada_w")
    up_a_w_in = update(a_w_in, m_a_w_in, v_a_w_in, r_a_w_in, "adamw_a_w_in")
    up_a_w_out = update(a_w_out, m_a_w_out, v_a_w_out, r_a_w_out, "adamw_a_w_out")
    up_b_w_in = update(b_w_in, m_b_w_in, v_b_w_in, r_b_w_in, "adamw_b_w_in")
    up_b_w_out = update(b_w_out, m_b_w_out, v_b_w_out, r_b_w_out, "adamw_b_w_out")

    small_w = [ada_b, ln_g, ln_b, b_conv_w, b_conv_b, b_dt_bias, b_a_log, b_d, b_norm_w]
    small_m = [m_ada_b, m_ln_g, m_ln_b, m_b_conv_w, m_b_conv_b, m_b_dt_bias, m_b_a_log, m_b_d, m_b_norm_w]
    small_v = [v_ada_b, v_ln_g, v_ln_b, v_b_conv_w, v_b_conv_b, v_b_dt_bias, v_b_a_log, v_b_d, v_b_norm_w]
    small_g = [g_ada_b, g_ln_g, g_ln_b, _my_shard(g_conv_w, me, 2), _my_shard(g_conv_b, me, 1), g_dt_bias, g_a_log, g_d,
               _my_shard(g_norm_w, me, 1)]
    shapes = [w.shape for w in small_w]
    packed = _adamw(_pack(small_w), _pack(small_m), _pack(small_v), _pack(small_g), name="adamw_small")
    (up_ada_b, up_ln_g, up_ln_b, up_conv_w, up_conv_b, up_dt_bias, up_a_log, up_d, up_norm_w) = zip(
        *[_unpack(p[0], shapes) for p in packed])

    ordered = [up_ada_w, up_ada_b, up_ln_g, up_ln_b, up_a_w_in, up_a_w_out, up_b_w_in, up_conv_w, up_conv_b,
               up_dt_bias, up_a_log, up_d, up_norm_w, up_b_w_out]
    return (loss, grad_x, *[u[0] for u in ordered], *[u[1] for u in ordered], *[u[2] for u in ordered],
            *[u[3] for u in ordered])
```

```python
import functools
import math

import jax
import jax.numpy as jnp
import numpy as np
from jax import lax
from jax.experimental import pallas as pl
from jax.experimental.pallas import tpu as pltpu

F32 = jnp.float32
BF16 = jnp.bfloat16
HIGHEST = lax.Precision.HIGHEST
MESH = pl.DeviceIdType.MESH

D_MODEL = 1024
DEPTH = 4
A_HEADS = 16
A_HEAD_DIM = 64
A_WIDTH = 1024
DILATIONS = (1, 4, 16)
A_RADIUS = 64
A_QBLOCK = 128
A_IN_COLS = 10240
SSM_INNER = 2048
SSM_HEADS = 32
SSM_HEAD_DIM = 64
SSM_STATE = 128
SSM_GROUPS = 4
SSM_CHUNK = 128
SSM_CONV = 5
SSM_CONV_DIM = 3072
SSM_IN_COLS = 5184
SSM_MAIN_COLS = 5120
SSM_PAD_COLS = 5376
CONV_HALO = 16
ALPHA = (2 * DEPTH) ** 0.25
LN_EPS = 1e-5
RMS_EPS = 1e-5
ADAM_LR, ADAM_B1, ADAM_B2, ADAM_EPS, ADAM_WD, ADAM_STEP = 0.001, 0.9, 0.999, 1e-08, 0.01, 10
N_DEV = 8
VMEM_LIMIT_BYTES = 56 * 1024 * 1024
NEG_BIG = -1e30


def _params(*sem):
    return pltpu.CompilerParams(dimension_semantics=sem, vmem_limit_bytes=VMEM_LIMIT_BYTES)


def _sigmoid(x):
    return 1.0 / (1.0 + jnp.exp(-x))


def _silu_and_grad(x):
    sg = _sigmoid(x)
    return x * sg, sg * (1.0 + x * (1.0 - sg))


def _softplus(x):
    e = jnp.exp(-jnp.abs(x))
    u = 1.0 + e
    log1p = jnp.where(u == 1.0, e, jnp.log(u) * (e / jnp.where(u == 1.0, 1.0, u - 1.0)))
    return jnp.maximum(x, 0.0) + log1p


_DIMS = {"nn": (((1,), (0,)), ((), ())), "nt": (((1,), (1,)), ((), ())), "tn": (((0,), (0,)), ((), ()))}


def _mm(a, b, *, mode, out_dtype, tm, tn, tk, name):
    if mode == "nn":
        (m, k), (_, n) = a.shape, b.shape
    elif mode == "nt":
        (m, k), (n, _) = a.shape, b.shape
    else:
        (k, m), (_, n) = a.shape, b.shape
    tm, tn, tk = min(tm, m), min(tn, n), min(tk, k)
    assert m % tm == 0 and n % tn == 0 and k % tk == 0, (name, a.shape, b.shape)
    nk = k // tk
    dims = _DIMS[mode]

    def body(a_ref, b_ref, o_ref, *scratch):
        part = lax.dot_general(a_ref[...], b_ref[...], dims, preferred_element_type=F32)
        if nk == 1:
            o_ref[...] = part.astype(o_ref.dtype)
            return
        acc_ref, = scratch
        kk = pl.program_id(2)

        @pl.when(kk == 0)
        def _():
            acc_ref[...] = part

        @pl.when(kk > 0)
        def _():
            acc_ref[...] += part

        @pl.when(kk == nk - 1)
        def _():
            o_ref[...] = acc_ref[...].astype(o_ref.dtype)

    if mode == "tn":
        a_spec = pl.BlockSpec((tk, tm), lambda i, j, kk: (kk, i))
    else:
        a_spec = pl.BlockSpec((tm, tk), lambda i, j, kk: (i, kk))
    if mode == "nt":
        b_spec = pl.BlockSpec((tn, tk), lambda i, j, kk: (j, kk))
    else:
        b_spec = pl.BlockSpec((tk, tn), lambda i, j, kk: (kk, j))
    return pl.pallas_call(
        body, name=name, grid=(m // tm, n // tn, nk),
        in_specs=[a_spec, b_spec], out_specs=pl.BlockSpec((tm, tn), lambda i, j, kk: (i, j)),
        out_shape=jax.ShapeDtypeStruct((m, n), out_dtype),
        scratch_shapes=[] if nk == 1 else [pltpu.VMEM((tm, tn), F32)],
        compiler_params=_params("parallel", "parallel", "arbitrary"),
    )(a, b)


def _mm_dh(dproj, w, dx_part, x, scale, *, tm, tk, name):
    s, k = dproj.shape
    d = w.shape[0]
    tk = min(tk, k)
    assert s % tm == 0 and k % tk == 0
    nk = k // tk

    def body(a_ref, w_ref, dxp_ref, x_ref, sc_ref, dx_ref, dsc_ref, dsh_ref, acc_ref):
        i, kk = pl.program_id(0), pl.program_id(1)
        part = lax.dot_general(a_ref[...], w_ref[...], _DIMS["nt"], preferred_element_type=F32)

        @pl.when(kk == 0)
        def _():
            acc_ref[...] = part

        @pl.when(kk > 0)
        def _():
            acc_ref[...] += part

        @pl.when(jnp.logical_and(i == 0, kk == 0))
        def _():
            dsc_ref[...] = jnp.zeros_like(dsc_ref)
            dsh_ref[...] = jnp.zeros_like(dsh_ref)

        @pl.when(kk == nk - 1)
        def _():
            dh = acc_ref[...]
            dx_ref[...] = dxp_ref[...] + dh * (1.0 + sc_ref[...])
            dsc_ref[...] += jnp.sum(dh * x_ref[...], axis=0, keepdims=True)
            dsh_ref[...] += jnp.sum(dh, axis=0, keepdims=True)

    row = pl.BlockSpec((tm, d), lambda i, kk: (i, 0))
    vec = pl.BlockSpec((1, d), lambda i, kk: (0, 0))
    return pl.pallas_call(
        body, name=name, grid=(s // tm, nk),
        in_specs=[pl.BlockSpec((tm, tk), lambda i, kk: (i, kk)), pl.BlockSpec((d, tk), lambda i, kk: (0, kk)),
                  row, row, vec],
        out_specs=[row, vec, vec],
        out_shape=[jax.ShapeDtypeStruct((s, d), F32), jax.ShapeDtypeStruct((1, d), F32),
                   jax.ShapeDtypeStruct((1, d), F32)],
        scratch_shapes=[pltpu.VMEM((tm, d), F32)],
        compiler_params=_params("arbitrary", "arbitrary"),
    )(dproj, w, dx_part, x, scale)


ROW_TILE = 512


def _modulate(x, scale, shift, *, name):
    s, d = x.shape

    def body(x_ref, sc_ref, sh_ref, h_ref):
        h_ref[...] = (x_ref[...] * (1.0 + sc_ref[...]) + sh_ref[...]).astype(BF16)

    row = pl.BlockSpec((ROW_TILE, d), lambda i: (i, 0))
    vec = pl.BlockSpec((1, d), lambda i: (0, 0))
    return pl.pallas_call(
        body, name=name, grid=(s // ROW_TILE,), in_specs=[row, vec, vec], out_specs=row,
        out_shape=jax.ShapeDtypeStruct((s, d), BF16), compiler_params=_params("parallel"),
    )(x, scale, shift)


def _resid_ln_fwd(x, out, gate, g, b, *, name):
    s, d = x.shape

    def body(x_ref, o_ref, gate_ref, g_ref, b_ref, y_ref):
        r = ALPHA * x_ref[...] + gate_ref[...] * o_ref[...]
        mu = jnp.mean(r, axis=-1, keepdims=True)
        rc = r - mu
        var = jnp.mean(rc * rc, axis=-1, keepdims=True)
        y_ref[...] = rc * lax.rsqrt(var + LN_EPS) * g_ref[...] + b_ref[...]

    row = pl.BlockSpec((ROW_TILE, d), lambda i: (i, 0))
    vec = pl.BlockSpec((1, d), lambda i: (0, 0))
    return pl.pallas_call(
        body, name=name, grid=(s // ROW_TILE,), in_specs=[row, row, vec, vec, vec], out_specs=row,
        out_shape=jax.ShapeDtypeStruct((s, d), F32), compiler_params=_params("parallel"),
    )(x, out, gate, g, b)


def _resid_ln_bwd(x, out, gate, g, dy, *, name):
    s, d = x.shape

    def body(x_ref, o_ref, gate_ref, g_ref, dy_ref, dxp_ref, dout_ref, dgate_ref, dg_ref, db_ref):
        @pl.when(pl.program_id(0) == 0)
        def _():
            dgate_ref[...] = jnp.zeros_like(dgate_ref)
            dg_ref[...] = jnp.zeros_like(dg_ref)
            db_ref[...] = jnp.zeros_like(db_ref)

        o = o_ref[...]
        r = ALPHA * x_ref[...] + gate_ref[...] * o
        mu = jnp.mean(r, axis=-1, keepdims=True)
        rc = r - mu
        var = jnp.mean(rc * rc, axis=-1, keepdims=True)
        rstd = lax.rsqrt(var + LN_EPS)
        xhat = rc * rstd
        dy = dy_ref[...]
        dxh = dy * g_ref[...]
        dr = rstd * (dxh - jnp.mean(dxh, axis=-1, keepdims=True) - xhat * jnp.mean(dxh * xhat, axis=-1, keepdims=True))
        dxp_ref[...] = ALPHA * dr
        dout_ref[...] = (gate_ref[...] * dr).astype(BF16)
        dgate_ref[...] += jnp.sum(dr * o, axis=0, keepdims=True)
        dg_ref[...] += jnp.sum(dy * xhat, axis=0, keepdims=True)
        db_ref[...] += jnp.sum(dy, axis=0, keepdims=True)

    row = pl.BlockSpec((ROW_TILE, d), lambda i: (i, 0))
    vec = pl.BlockSpec((1, d), lambda i: (0, 0))
    vshape = jax.ShapeDtypeStruct((1, d), F32)
    return pl.pallas_call(
        body, name=name, grid=(s // ROW_TILE,), in_specs=[row, row, vec, vec, row],
        out_specs=[row, row, vec, vec, vec],
        out_shape=[jax.ShapeDtypeStruct((s, d), F32), jax.ShapeDtypeStruct((s, d), BF16), vshape, vshape, vshape],
        compiler_params=_params("arbitrary"),
    )(x, out, gate, g, dy)


def _loss_and_grad(y, target, *, name):
    s, d = y.shape

    def body(y_ref, t_ref, dy_ref, loss_ref):
        @pl.when(pl.program_id(0) == 0)
        def _():
            loss_ref[...] = jnp.zeros_like(loss_ref)

        e = y_ref[...] - t_ref[...]
        dy_ref[...] = e * (1.0 / d)
        loss_ref[...] += jnp.sum(jnp.sum(e * e, axis=0, keepdims=True), axis=1, keepdims=True) * (0.5 / d)

    row = pl.BlockSpec((ROW_TILE, d), lambda i: (i, 0))
    return pl.pallas_call(
        body, name=name, grid=(s // ROW_TILE,), in_specs=[row, row],
        out_specs=[row, pl.BlockSpec((1, 128), lambda i: (0, 0))],
        out_shape=[jax.ShapeDtypeStruct((s, d), F32), jax.ShapeDtypeStruct((1, 128), F32)],
        compiler_params=_params("arbitrary"),
    )(y, target)


_SLOPES = np.asarray(2.0 ** (-8.0 * (np.arange(A_HEADS, dtype=np.float32) + 1.0) / A_HEADS), dtype=np.float32)


def _attn_scores(q, kw, slope, dist, valid):
    s = lax.dot_general(q, kw, _DIMS["nt"], preferred_element_type=F32) * (1.0 / math.sqrt(A_HEAD_DIM))
    return jnp.where(valid, s - slope * dist, NEG_BIG)


def _attn_window(blk, length, win, dil):
    start = pl.multiple_of(jnp.clip(blk * A_QBLOCK - A_RADIUS, 0, length - win), A_RADIUS)
    qpos = blk * A_QBLOCK + lax.broadcasted_iota(jnp.int32, (A_QBLOCK, win), 0)
    kpos = start + lax.broadcasted_iota(jnp.int32, (A_QBLOCK, win), 1)
    delta = jnp.abs(kpos - qpos)
    return start, (delta * dil).astype(F32), delta <= A_RADIUS


A_BLOCKS_PER_STEP = 2
A_GROUP_COLS = 3 * A_WIDTH


def _attn_view(proj, group):
    s = proj.shape[0]
    dil = DILATIONS[group]
    if dil == 1:
        return proj, proj.shape[1] // 128, group * (A_GROUP_COLS // 128)
    cols = proj[:, group * A_GROUP_COLS:(group + 1) * A_GROUP_COLS]
    return cols.reshape(s // dil, dil * A_GROUP_COLS), A_GROUP_COLS // 128, 0


def _attn_fwd(proj, group, *, name):
    pv, cb, qoff = _attn_view(proj, group)
    s = proj.shape[0]
    dil = DILATIONS[group]
    length = s // dil
    win = min(2 * A_QBLOCK, length)
    nblk = length // A_QBLOCK
    per = A_BLOCKS_PER_STEP if nblk % A_BLOCKS_PER_STEP == 0 else 1

    def body(slope_ref, q_ref, k_ref, v_ref, o_ref, lse_ref):
        hp = pl.program_id(1)
        for u in range(per):
            rows = slice(u * A_QBLOCK, (u + 1) * A_QBLOCK)
            start, dist, valid = _attn_window(pl.program_id(2) * per + u, length, win, dil)
            kw = k_ref[pl.ds(start, win), :]
            vw = v_ref[pl.ds(start, win), :]
            q = q_ref[rows, :]
            outs, lses = [], []
            for hh in range(2):
                sl = slice(hh * A_HEAD_DIM, (hh + 1) * A_HEAD_DIM)
                sc = _attn_scores(q[:, sl], kw[:, sl], slope_ref[hp * 2 + hh], dist, valid)
                m = jnp.max(sc, axis=-1, keepdims=True)
                p = jnp.exp(sc - m)
                z = jnp.sum(p, axis=-1, keepdims=True)
                o = jnp.dot(p.astype(BF16), vw[:, sl], preferred_element_type=F32) / z
                outs.append(o)
                lses.append(jnp.broadcast_to(m + jnp.log(z), (A_QBLOCK, A_HEAD_DIM)))
            o_ref[rows, :] = jnp.concatenate(outs, axis=1)
            lse_ref[rows, :] = jnp.concatenate(lses, axis=1)

    qspec = pl.BlockSpec((per * A_QBLOCK, 128), lambda r, hp, b: (b, r * cb + qoff + hp))
    kspec = pl.BlockSpec((length, 128), lambda r, hp, b: (0, r * cb + qoff + 8 + hp))
    vspec = pl.BlockSpec((length, 128), lambda r, hp, b: (0, r * cb + qoff + 16 + hp))
    ospec = pl.BlockSpec((per * A_QBLOCK, 128), lambda r, hp, b: (b, r * 8 + hp))
    oshape = jax.ShapeDtypeStruct((length, dil * A_WIDTH), F32)
    o, lse = pl.pallas_call(
        body, name=name, grid=(dil, 8, nblk // per),
        in_specs=[pl.BlockSpec(memory_space=pltpu.SMEM), qspec, kspec, vspec], out_specs=[ospec, ospec],
        out_shape=[oshape, oshape], compiler_params=_params("parallel", "parallel", "arbitrary"),
    )(jnp.asarray(_SLOPES), pv, pv, pv)
    return o.reshape(s, A_WIDTH), lse.reshape(s, A_WIDTH)


def _attn_bwd(proj, group, do, o, lse, *, name):
    pv, cb, qoff = _attn_view(proj, group)
    s = proj.shape[0]
    dil = DILATIONS[group]
    length = s // dil
    win = min(2 * A_QBLOCK, length)
    nblk = length // A_QBLOCK
    per = A_BLOCKS_PER_STEP if nblk % A_BLOCKS_PER_STEP == 0 else 1
    nstep = nblk // per
    scale = 1.0 / math.sqrt(A_HEAD_DIM)
    view = lambda t: t.reshape(length, dil * A_WIDTH)

    def body(slope_ref, q_ref, k_ref, v_ref, do_ref, o_ref, lse_ref, dq_ref, dk_ref, dv_ref, dk_acc, dv_acc):
        hp, step = pl.program_id(1), pl.program_id(2)

        @pl.when(step == 0)
        def _():
            dk_acc[...] = jnp.zeros_like(dk_acc)
            dv_acc[...] = jnp.zeros_like(dv_acc)

        for u in range(per):
            rows = slice(u * A_QBLOCK, (u + 1) * A_QBLOCK)
            start, dist, valid = _attn_window(step * per + u, length, win, dil)
            kw = k_ref[pl.ds(start, win), :]
            vw = v_ref[pl.ds(start, win), :]
            q = q_ref[rows, :]
            do_b = do_ref[rows, :]
            dsum = do_b.astype(F32) * o_ref[rows, :]
            lse_b = lse_ref[rows, :]
            dqs, dks, dvs = [], [], []
            for hh in range(2):
                sl = slice(hh * A_HEAD_DIM, (hh + 1) * A_HEAD_DIM)
                sc = _attn_scores(q[:, sl], kw[:, sl], slope_ref[hp * 2 + hh], dist, valid)
                p = jnp.exp(sc - lse_b[:, hh * A_HEAD_DIM:hh * A_HEAD_DIM + 1])
                dp = lax.dot_general(do_b[:, sl], vw[:, sl], _DIMS["nt"], preferred_element_type=F32)
                ds = (p * (dp - jnp.sum(dsum[:, sl], axis=-1, keepdims=True))).astype(BF16)
                dqs.append(jnp.dot(ds, kw[:, sl], preferred_element_type=F32) * scale)
                dks.append(lax.dot_general(ds, q[:, sl], _DIMS["tn"], preferred_element_type=F32) * scale)
                dvs.append(lax.dot_general(p.astype(BF16), do_b[:, sl], _DIMS["tn"], preferred_element_type=F32))
            dq_ref[rows, :] = jnp.concatenate(dqs, axis=1).astype(BF16)
            dk_acc[pl.ds(start, win), :] += jnp.concatenate(dks, axis=1)
            dv_acc[pl.ds(start, win), :] += jnp.concatenate(dvs, axis=1)

        @pl.when(step == nstep - 1)
        def _():
            dk_ref[...] = dk_acc[...].astype(BF16)
            dv_ref[...] = dv_acc[...].astype(BF16)

    qspec = pl.BlockSpec((per * A_QBLOCK, 128), lambda r, hp, b: (b, r * cb + qoff + hp))
    kspec = pl.BlockSpec((length, 128), lambda r, hp, b: (0, r * cb + qoff + 8 + hp))
    vspec = pl.BlockSpec((length, 128), lambda r, hp, b: (0, r * cb + qoff + 16 + hp))
    bspec = pl.BlockSpec((per * A_QBLOCK, 128), lambda r, hp, b: (b, r * 8 + hp))
    fspec = pl.BlockSpec((length, 128), lambda r, hp, b: (0, r * 8 + hp))
    oshape = jax.ShapeDtypeStruct((length, dil * A_WIDTH), BF16)
    dq, dk, dv = pl.pallas_call(
        body, name=name, grid=(dil, 8, nstep),
        in_specs=[pl.BlockSpec(memory_space=pltpu.SMEM), qspec, kspec, vspec, bspec, bspec, bspec],
        out_specs=[bspec, fspec, fspec], out_shape=[oshape, oshape, oshape],
        scratch_shapes=[pltpu.VMEM((length, 128), F32), pltpu.VMEM((length, 128), F32)],
        compiler_params=_params("parallel", "parallel", "arbitrary"),
    )(jnp.asarray(_SLOPES), pv, pv, pv, view(do), view(o), view(lse))
    return dq.reshape(s, A_WIDTH), dk.reshape(s, A_WIDTH), dv.reshape(s, A_WIDTH)


A_GATE_BLOCK = 9


def _attn_combine(os_, lses, proj, *, name):
    s = proj.shape[0]
    tr = 256

    def body(o0, o1, o2, l0, l1, l2, gate_ref, y_ref, o_ref, lse_ref):
        la, lb, lc = l0[...], l1[...], l2[...]
        m = jnp.maximum(jnp.maximum(la, lb), lc)
        ea, eb, ec = jnp.exp(la - m), jnp.exp(lb - m), jnp.exp(lc - m)
        den = ea + eb + ec
        o = (ea * o0[...] + eb * o1[...] + ec * o2[...]) / den
        o_ref[...] = o
        lse_ref[...] = m + jnp.log(den)
        y_ref[...] = (o * _silu_and_grad(gate_ref[...].astype(F32))[0]).astype(BF16)

    row = pl.BlockSpec((tr, A_WIDTH), lambda i: (i, 0))
    gspec = pl.BlockSpec((tr, A_WIDTH), lambda i: (i, A_GATE_BLOCK))
    return pl.pallas_call(
        body, name=name, grid=(s // tr,), in_specs=[row] * 6 + [gspec], out_specs=[row, row, row],
        out_shape=[jax.ShapeDtypeStruct((s, A_WIDTH), BF16), jax.ShapeDtypeStruct((s, A_WIDTH), F32),
                   jax.ShapeDtypeStruct((s, A_WIDTH), F32)],
        compiler_params=_params("parallel"),
    )(*os_, *lses, proj)


def _attn_combine_bwd(dy, o, proj, *, name):
    s = proj.shape[0]
    tr = 256

    def body(dy_ref, o_ref, gate_ref, do_ref, dg_ref):
        si, dsi = _silu_and_grad(gate_ref[...].astype(F32))
        dyv = dy_ref[...]
        do_ref[...] = (dyv * si).astype(BF16)
        dg_ref[...] = (dyv * o_ref[...] * dsi).astype(BF16)

    row = pl.BlockSpec((tr, A_WIDTH), lambda i: (i, 0))
    gspec = pl.BlockSpec((tr, A_WIDTH), lambda i: (i, A_GATE_BLOCK))
    shp = jax.ShapeDtypeStruct((s, A_WIDTH), BF16)
    return pl.pallas_call(
        body, name=name, grid=(s // tr,), in_specs=[row, row, gspec], out_specs=[row, row], out_shape=[shp, shp],
        compiler_params=_params("parallel"),
    )(dy, o, proj)


CONV_TILE = 256
CONV_SUB = 4


def _conv_taps(xe, n):
    return [xe if j == 2 else pltpu.roll(xe, (2 - j) % n, 0) for j in range(SSM_CONV)]


def _conv_fwd(xpad, w, b, *, name):
    s = xpad.shape[0] - 2 * CONV_HALO
    n = CONV_TILE + 2 * CONV_HALO
    ncol = SSM_CONV_DIM // 128

    sub = min(CONV_SUB, s // CONV_TILE)

    def body(x_ref, w_ref, b_ref, o_ref):
        base = pl.program_id(1) * (sub * CONV_TILE)

        def tile(k, carry):
            r0 = pl.multiple_of(k * CONV_TILE, CONV_TILE)
            t0 = pl.multiple_of(base + r0, CONV_TILE)
            taps = _conv_taps(x_ref[pl.ds(t0, n), :].astype(F32), n)
            pre = b_ref[...]
            for j in range(SSM_CONV):
                pre = pre + w_ref[j:j + 1, :] * taps[j]
            o_ref[pl.ds(r0, CONV_TILE), :] = _silu_and_grad(pre[CONV_HALO:CONV_HALO + CONV_TILE])[0].astype(BF16)
            return carry

        lax.fori_loop(0, sub, tile, 0)

    return pl.pallas_call(
        body, name=name, grid=(ncol, s // (sub * CONV_TILE)),
        in_specs=[pl.BlockSpec((s + 2 * CONV_HALO, 128), lambda j, i: (0, j)),
                  pl.BlockSpec((SSM_CONV, 128), lambda j, i: (0, j)), pl.BlockSpec((1, 128), lambda j, i: (0, j))],
        out_specs=pl.BlockSpec((sub * CONV_TILE, 128), lambda j, i: (i, j)),
        out_shape=jax.ShapeDtypeStruct((s, SSM_CONV_DIM), BF16), compiler_params=_params("parallel", "arbitrary"),
    )(xpad, w, b)


def _conv_bwd(xpad, dapad, w, b, *, name):
    s = xpad.shape[0] - 2 * CONV_HALO
    n = CONV_TILE + 2 * CONV_HALO
    ncol = SSM_CONV_DIM // 128
    mid = slice(CONV_HALO, CONV_HALO + CONV_TILE)
    sub = min(CONV_SUB, s // CONV_TILE)

    def body(x_ref, da_ref, w_ref, b_ref, dx_ref, dw_ref, db_ref):
        @pl.when(pl.program_id(1) == 0)
        def _():
            dw_ref[...] = jnp.zeros_like(dw_ref)
            db_ref[...] = jnp.zeros_like(db_ref)

        base = pl.program_id(1) * (sub * CONV_TILE)

        def tile(k, carry):
            r0 = pl.multiple_of(k * CONV_TILE, CONV_TILE)
            t0 = pl.multiple_of(base + r0, CONV_TILE)
            taps = _conv_taps(x_ref[pl.ds(t0, n), :].astype(F32), n)
            pre = b_ref[...]
            for j in range(SSM_CONV):
                pre = pre + w_ref[j:j + 1, :] * taps[j]
            dpre = da_ref[pl.ds(t0, n), :] * _silu_and_grad(pre)[1]
            dx = jnp.zeros((CONV_TILE, 128), F32)
            for j in range(SSM_CONV):
                back = dpre if j == 2 else pltpu.roll(dpre, (j - 2) % n, 0)
                dx = dx + w_ref[j:j + 1, :] * back[mid]
                dw_ref[j:j + 1, :] += jnp.sum(dpre[mid] * taps[j][mid], axis=0, keepdims=True)
            dx_ref[pl.ds(r0, CONV_TILE), :] = dx.astype(BF16)
            db_ref[...] += jnp.sum(dpre[mid], axis=0, keepdims=True)
            return carry

        lax.fori_loop(0, sub, tile, 0)

    full = pl.BlockSpec((s + 2 * CONV_HALO, 128), lambda j, i: (0, j))
    wspec = pl.BlockSpec((SSM_CONV, 128), lambda j, i: (0, j))
    bspec = pl.BlockSpec((1, 128), lambda j, i: (0, j))
    return pl.pallas_call(
        body, name=name, grid=(ncol, s // (sub * CONV_TILE)), in_specs=[full, full, wspec, bspec],
        out_specs=[pl.BlockSpec((sub * CONV_TILE, 128), lambda j, i: (i, j)), wspec, bspec],
        out_shape=[jax.ShapeDtypeStruct((s, SSM_CONV_DIM), BF16), jax.ShapeDtypeStruct((SSM_CONV, SSM_CONV_DIM), F32),
                   jax.ShapeDtypeStruct((1, SSM_CONV_DIM), F32)],
        compiler_params=_params("parallel", "arbitrary"),
    )(xpad, dapad, w, b)


HPG = SSM_HEADS // SSM_GROUPS
GW = HPG * SSM_HEAD_DIM
T = SSM_CHUNK


def _ssd_specs(nc):
    ceff = lambda d, c: jnp.where(d == 0, c, nc - 1 - c)
    return ceff, [
        pl.BlockSpec((T, GW), lambda d, g, c: (ceff(d, c), g)),
        pl.BlockSpec((T, SSM_STATE), lambda d, g, c: (ceff(d, c), SSM_INNER // 128 + g)),
        pl.BlockSpec((T, SSM_STATE), lambda d, g, c: (ceff(d, c), SSM_INNER // 128 + SSM_GROUPS + g)),
        pl.BlockSpec((None, None, T, HPG), lambda d, g, c: (d, g, ceff(d, c), 0)),
        pl.BlockSpec((None, None, HPG, T), lambda d, g, c: (d, g, 0, ceff(d, c))),
        pl.BlockSpec((None, None, 2, HPG), lambda d, g, c: (d, g, 0, 0)),
        pl.BlockSpec((None, None, HPG, 2), lambda d, g, c: (d, g, 0, 0)),
    ]


def _ssd_chunk_common(d, dt_ref, dtt_ref, prr_ref, prc_ref):
    sgn = 1 - 2 * d
    ri = lax.broadcasted_iota(jnp.int32, (T, T), 0)
    ci = lax.broadcasted_iota(jnp.int32, (T, T), 1)
    mask = ((ri - ci) * sgn) >= 0
    maskf = mask.astype(F32)
    bias_r, a_r = prr_ref[0:1, :], prr_ref[1:2, :]
    bias_c, a_c = prc_ref[:, 0:1], prc_ref[:, 1:2]
    raw = dt_ref[...] + bias_r
    dt_rows = _softplus(raw)
    dt_lanes = _softplus(dtt_ref[...] + bias_c)
    a_rows = dt_rows * a_r
    acum_rows = jnp.dot(maskf, a_rows, precision=HIGHEST, preferred_element_type=F32)
    acum_lanes = lax.dot_general(dt_lanes * a_c, maskf, _DIMS["nt"], precision=HIGHEST, preferred_element_type=F32)
    tot = jnp.sum(a_rows, axis=0, keepdims=True)
    return mask, maskf, raw, dt_rows, a_r, acum_rows, acum_lanes, tot


def _lanes_per_head(pieces):
    return jnp.concatenate([jnp.broadcast_to(p, (p.shape[0], SSM_HEAD_DIM)) for p in pieces], axis=1)


def _ssd_fwd_v1(xbc, dtr, dtt, pr_rows, pr_cols, *, name):
    s = xbc.shape[0]
    nc = s // T
    ceff, in_specs = _ssd_specs(nc)

    def body(x_ref, b_ref, c_ref, dt_ref, dtt_ref, prr_ref, prc_ref, y_ref, hs_ref, st_ref):
        d, c = pl.program_id(0), pl.program_id(2)

        @pl.when(c == 0)
        def _():
            st_ref[...] = jnp.zeros_like(st_ref)

        mask, _, _, dt_rows, _, acum_rows, acum_lanes, tot = _ssd_chunk_common(d, dt_ref, dtt_ref, prr_ref, prc_ref)
        xs = x_ref[...].astype(F32)
        bm, cm = b_ref[...], c_ref[...]
        hprev = st_ref[...]
        hs_ref[...] = hprev
        cb = lax.dot_general(cm, bm, _DIMS["nt"], preferred_element_type=F32)
        ch = jnp.dot(cm, hprev.astype(BF16), preferred_element_type=F32)
        ys, xgds, etots = [], [], []
        for j in range(HPG):
            sl = slice(j * SSM_HEAD_DIM, (j + 1) * SSM_HEAD_DIM)
            ac, al = acum_rows[:, j:j + 1], acum_lanes[j:j + 1, :]
            lm = jnp.where(mask, jnp.exp(jnp.minimum(ac - al, 0.0)), 0.0)
            xg = xs[:, sl] * dt_rows[:, j:j + 1]
            yd = jnp.dot((cb * lm).astype(BF16), xg.astype(BF16), preferred_element_type=F32)
            ys.append(yd + jnp.exp(ac) * ch[:, sl])
            xgds.append(xg * jnp.exp(tot[:, j:j + 1] - ac))
            etots.append(jnp.exp(tot[:, j:j + 1]))
        y_ref[...] = jnp.concatenate(ys, axis=1)
        new = lax.dot_general(bm, jnp.concatenate(xgds, axis=1).astype(BF16), _DIMS["tn"], preferred_element_type=F32)
        st_ref[...] = hprev * _lanes_per_head(etots) + new

    return pl.pallas_call(
        body, name=name, grid=(2, SSM_GROUPS, nc), in_specs=in_specs,
        out_specs=[pl.BlockSpec((None, T, GW), lambda d, g, c: (d, ceff(d, c), g)),
                   pl.BlockSpec((None, None, None, SSM_STATE, GW), lambda d, g, c: (d, ceff(d, c), g, 0, 0))],
        out_shape=[jax.ShapeDtypeStruct((2, s, SSM_INNER), F32),
                   jax.ShapeDtypeStruct((2, nc, SSM_GROUPS, SSM_STATE, GW), F32)],
        scratch_shapes=[pltpu.VMEM((SSM_STATE, GW), F32)],
        compiler_params=_params("parallel", "parallel", "arbitrary"),
    )(xbc, xbc, xbc, dtr, dtt, pr_rows, pr_cols)


def _put_lane(j, col):
    lane = lax.broadcasted_iota(jnp.int32, (col.shape[0], HPG), 1)
    return jnp.where(lane == j, col, 0.0)


def _ssd_bwd_v1(xbc, dtr, dtt, pr_rows, pr_cols, dvec, hs, dy, *, name):
    s = xbc.shape[0]
    nc = s // T
    cb_of = lambda d, c: jnp.where(d == 0, nc - 1 - c, c)
    in_specs = [
        pl.BlockSpec((T, GW), lambda d, g, c: (cb_of(d, c), g)),
        pl.BlockSpec((T, SSM_STATE), lambda d, g, c: (cb_of(d, c), SSM_INNER // 128 + g)),
        pl.BlockSpec((T, SSM_STATE), lambda d, g, c: (cb_of(d, c), SSM_INNER // 128 + SSM_GROUPS + g)),
        pl.BlockSpec((None, None, T, HPG), lambda d, g, c: (d, g, cb_of(d, c), 0)),
        pl.BlockSpec((None, None, HPG, T), lambda d, g, c: (d, g, 0, cb_of(d, c))),
        pl.BlockSpec((None, None, 2, HPG), lambda d, g, c: (d, g, 0, 0)),
        pl.BlockSpec((None, None, HPG, 2), lambda d, g, c: (d, g, 0, 0)),
        pl.BlockSpec((1, GW), lambda d, g, c: (0, g)),
        pl.BlockSpec((None, None, None, SSM_STATE, GW), lambda d, g, c: (d, cb_of(d, c), g, 0, 0)),
        pl.BlockSpec((T, GW), lambda d, g, c: (cb_of(d, c), g)),
    ]

    def body(x_ref, b_ref, c_ref, dt_ref, dtt_ref, prr_ref, prc_ref, dvec_ref, hs_ref, dy_ref,
             dxs_ref, db_ref, dc_ref, ddt_ref, dalog_ref, dbias_ref, g_ref):
        d, c = pl.program_id(0), pl.program_id(2)

        @pl.when(c == 0)
        def _():
            g_ref[...] = jnp.zeros_like(g_ref)
            dalog_ref[...] = jnp.zeros_like(dalog_ref)
            dbias_ref[...] = jnp.zeros_like(dbias_ref)

        mask, maskf, raw, dt_rows, a_r, acum_rows, acum_lanes, tot = _ssd_chunk_common(
            d, dt_ref, dtt_ref, prr_ref, prc_ref)
        xs = x_ref[...].astype(F32)
        bm, cm = b_ref[...], c_ref[...]
        hst = hs_ref[...]
        gst = g_ref[...]
        dyv = dy_ref[...]
        dyb = dyv.astype(BF16)
        dv = dvec_ref[...] * (1 - d).astype(F32)
        cb = lax.dot_general(cm, bm, _DIMS["nt"], preferred_element_type=F32)
        ch = jnp.dot(cm, hst.astype(BF16), preferred_element_type=F32)
        bg = jnp.dot(bm, gst.astype(BF16), preferred_element_type=F32)
        hg = hst * gst
        dcb = jnp.zeros((T, T), F32)
        dacum = jnp.zeros((T, HPG), F32)
        rx = jnp.zeros((T, HPG), F32)
        dtot = jnp.zeros((1, HPG), F32)
        dxss, dyes, xgds, etots = [], [], [], []
        for j in range(HPG):
            sl = slice(j * SSM_HEAD_DIM, (j + 1) * SSM_HEAD_DIM)
            ac, al = acum_rows[:, j:j + 1], acum_lanes[j:j + 1, :]
            lm = jnp.where(mask, jnp.exp(jnp.minimum(ac - al, 0.0)), 0.0)
            m = cb * lm
            dtj = dt_rows[:, j:j + 1]
            xsj = xs[:, sl]
            xg = xsj * dtj
            dyj = dyv[:, sl]
            ec = jnp.exp(ac)
            etot = jnp.exp(tot[:, j:j + 1])
            decay = jnp.exp(tot[:, j:j + 1] - ac)
            dm = lax.dot_general(dyb[:, sl], xg.astype(BF16), _DIMS["nt"], preferred_element_type=F32)
            w = dm * m
            dcb = dcb + dm * lm
            bgj = bg[:, sl]
            dxg = lax.dot_general(m.astype(BF16), dyb[:, sl], _DIMS["tn"], preferred_element_type=F32) + decay * bgj
            xb = decay * jnp.sum(xg * bgj, axis=-1, keepdims=True)
            da_j = (jnp.sum(w, axis=-1, keepdims=True) - jnp.sum(w.T, axis=-1, keepdims=True)
                    + jnp.sum(ec * ch[:, sl] * dyj, axis=-1, keepdims=True) - xb)
            dacum = dacum + _put_lane(j, da_j)
            rx = rx + _put_lane(j, jnp.sum(dxg * xsj, axis=-1, keepdims=True))
            dtot_j = (etot * jnp.sum(jnp.sum(hg[:, sl], axis=0, keepdims=True), axis=1, keepdims=True)
                      + jnp.sum(xb, axis=0, keepdims=True))
            dtot = dtot + _put_lane(j, dtot_j)
            dxss.append(dxg * dtj + dv[:, sl] * dyj)
            dyes.append(dyj * ec)
            xgds.append(xg * decay)
            etots.append(etot)
        da = lax.dot_general(maskf, dacum, _DIMS["tn"], precision=HIGHEST, preferred_element_type=F32) + dtot
        ddt = da * a_r + rx
        draw = ddt * _sigmoid(raw)
        ddt_ref[...] = draw
        dbias_ref[...] += jnp.sum(draw, axis=0, keepdims=True)
        dalog_ref[...] += jnp.sum(da * dt_rows, axis=0, keepdims=True) * a_r
        dxs_ref[...] = jnp.concatenate(dxss, axis=1)
        dye = jnp.concatenate(dyes, axis=1).astype(BF16)
        xgd = jnp.concatenate(xgds, axis=1).astype(BF16)
        dcbb = dcb.astype(BF16)
        dc_ref[...] = (jnp.dot(dcbb, bm, preferred_element_type=F32)
                       + lax.dot_general(dye, hst.astype(BF16), _DIMS["nt"], preferred_element_type=F32))
        db_ref[...] = (lax.dot_general(dcbb, cm, _DIMS["tn"], preferred_element_type=F32)
                       + lax.dot_general(xgd, gst.astype(BF16), _DIMS["nt"], preferred_element_type=F32))
        g_ref[...] = lax.dot_general(cm, dye, _DIMS["tn"], preferred_element_type=F32) + gst * _lanes_per_head(etots)

    small = pl.BlockSpec((None, None, 1, HPG), lambda d, g, c: (d, g, 0, 0))
    sshape = jax.ShapeDtypeStruct((2, SSM_GROUPS, 1, HPG), F32)
    return pl.pallas_call(
        body, name=name, grid=(2, SSM_GROUPS, nc), in_specs=in_specs,
        out_specs=[pl.BlockSpec((None, T, GW), lambda d, g, c: (d, cb_of(d, c), g)),
                   pl.BlockSpec((None, T, SSM_STATE), lambda d, g, c: (d, cb_of(d, c), g)),
                   pl.BlockSpec((None, T, SSM_STATE), lambda d, g, c: (d, cb_of(d, c), g)),
                   pl.BlockSpec((None, None, T, HPG), lambda d, g, c: (d, g, cb_of(d, c), 0)), small, small],
        out_shape=[jax.ShapeDtypeStruct((2, s, SSM_INNER), F32),
                   jax.ShapeDtypeStruct((2, s, SSM_GROUPS * SSM_STATE), F32),
                   jax.ShapeDtypeStruct((2, s, SSM_GROUPS * SSM_STATE), F32),
                   jax.ShapeDtypeStruct((2, SSM_GROUPS, s, HPG), F32), sshape, sshape],
        scratch_shapes=[pltpu.VMEM((SSM_STATE, GW), F32)],
        compiler_params=_params("parallel", "parallel", "arbitrary"),
    )(xbc, xbc, xbc, dtr, dtt, pr_rows, pr_cols, dvec, hs, dy)


PAIRS = HPG // 2


def _scan_lanes(x, forward):
    lane = lax.broadcasted_iota(jnp.int32, x.shape, 1)
    p = x
    k = 1
    while k < T:
        p = p + jnp.where(lane >= k, pltpu.roll(p, k, 1), 0.0)
        k *= 2
    tot = p[:, T - 1:T]
    return jnp.where(forward, p, tot - p + x), tot


def _ssd_chunk(d, dt_ref, dtt_ref, prr_ref, prc_ref):
    sgn = 1 - 2 * d
    ri = lax.broadcasted_iota(jnp.int32, (T, T), 0)
    ci = lax.broadcasted_iota(jnp.int32, (T, T), 1)
    mask = ((ri - ci) * sgn) >= 0
    mask_t = ((ci - ri) * sgn) >= 0
    bias_r = prr_ref[0:1, :]
    bias_c, a_c = prc_ref[:, 0:1], prc_ref[:, 1:2]
    dt_rows = _softplus(dt_ref[...] + bias_r)
    raw_lanes = dtt_ref[...] + bias_c
    dt_lanes = _softplus(raw_lanes)
    acum_lanes, tot = _scan_lanes(dt_lanes * a_c, d == 0)
    return dict(mask=mask, mask_t=mask_t, head0=ci < SSM_HEAD_DIM, dt_rows=dt_rows, raw_lanes=raw_lanes,
                dt_lanes=dt_lanes, a_c=a_c, acum_lanes=acum_lanes, acum_rows=acum_lanes.T, tot=tot)


def _ssd_pair(ck, q):
    h0 = ck["head0"]
    colb = lambda rows, j: jnp.broadcast_to(rows[:, j:j + 1], (T, T))
    rowb = lambda lanes, j: jnp.broadcast_to(lanes[j:j + 1, :], (T, T))
    lms, lmts, acs = [], [], []
    for j in (2 * q, 2 * q + 1):
        ac, al = colb(ck["acum_rows"], j), rowb(ck["acum_lanes"], j)
        lms.append(jnp.where(ck["mask"], jnp.exp(jnp.minimum(ac - al, 0.0)), 0.0))
        lmts.append(jnp.where(ck["mask_t"], jnp.exp(jnp.minimum(al - ac, 0.0)), 0.0))
        acs.append(ac)
    ac_pair = jnp.where(h0, acs[0], acs[1])
    dt_pair = jnp.where(h0, colb(ck["dt_rows"], 2 * q), colb(ck["dt_rows"], 2 * q + 1))
    tot_pair = jnp.where(h0[0:1], ck["tot"][2 * q:2 * q + 1, :], ck["tot"][2 * q + 1:2 * q + 2, :])
    return dict(lm=lms, lmt=lmts, dt=dt_pair, ec=jnp.exp(ac_pair), decay=jnp.exp(tot_pair - ac_pair),
                etot=jnp.exp(tot_pair))


def _split_heads(h0, v):
    zero = jnp.zeros_like(v)
    return jnp.where(h0, v, zero), jnp.where(h0, zero, v)


def _ssd_in_specs(chunk):
    return [
        pl.BlockSpec((T, GW), lambda d, g, c: (chunk(d, c), g)),
        pl.BlockSpec((T, SSM_STATE), lambda d, g, c: (chunk(d, c), SSM_INNER // 128 + g)),
        pl.BlockSpec((T, SSM_STATE), lambda d, g, c: (chunk(d, c), SSM_INNER // 128 + SSM_GROUPS + g)),
        pl.BlockSpec((SSM_STATE, T), lambda d, g, c: (g, chunk(d, c))),
        pl.BlockSpec((SSM_STATE, T), lambda d, g, c: (g, chunk(d, c))),
        pl.BlockSpec((None, None, T, HPG), lambda d, g, c: (d, g, chunk(d, c), 0)),
        pl.BlockSpec((None, None, HPG, T), lambda d, g, c: (d, g, 0, chunk(d, c))),
        pl.BlockSpec((None, None, 2, HPG), lambda d, g, c: (d, g, 0, 0)),
        pl.BlockSpec((None, None, HPG, 2), lambda d, g, c: (d, g, 0, 0)),
    ]


def _ssd_fwd(xbc, bt, ct, dtr, dtt, pr_rows, pr_cols, *, name):
    s = xbc.shape[0]
    nc = s // T
    chunk = lambda d, c: jnp.where(d == 0, c, nc - 1 - c)

    def body(x_ref, b_ref, c_ref, bt_ref, ct_ref, dt_ref, dtt_ref, prr_ref, prc_ref, y_ref, hs_ref, st_ref):
        d, c = pl.program_id(0), pl.program_id(2)

        @pl.when(c == 0)
        def _():
            st_ref[...] = jnp.zeros_like(st_ref)

        ck = _ssd_chunk(d, dt_ref, dtt_ref, prr_ref, prc_ref)
        xs = x_ref[...].astype(F32)
        cm = c_ref[...]
        hprev = st_ref[...]
        hs_ref[...] = hprev
        cb = lax.dot_general(cm, b_ref[...], _DIMS["nt"], preferred_element_type=F32)
        ch = jnp.dot(cm, hprev.astype(BF16), preferred_element_type=F32)
        ys, xgds, etots = [], [], []
        for q in range(PAIRS):
            sl = slice(q * 128, (q + 1) * 128)
            pr = _ssd_pair(ck, q)
            xg = xs[:, sl] * pr["dt"]
            xg0, xg1 = _split_heads(ck["head0"], xg.astype(BF16))
            yd = (jnp.dot((cb * pr["lm"][0]).astype(BF16), xg0, preferred_element_type=F32)
                  + jnp.dot((cb * pr["lm"][1]).astype(BF16), xg1, preferred_element_type=F32))
            ys.append(yd + pr["ec"] * ch[:, sl])
            xgds.append(xg * pr["decay"])
            etots.append(pr["etot"])
        y_ref[...] = jnp.concatenate(ys, axis=1)
        new = jnp.dot(bt_ref[...], jnp.concatenate(xgds, axis=1).astype(BF16), preferred_element_type=F32)
        st_ref[...] = hprev * jnp.concatenate(etots, axis=1) + new

    return pl.pallas_call(
        body, name=name, grid=(2, SSM_GROUPS, nc), in_specs=_ssd_in_specs(chunk),
        out_specs=[pl.BlockSpec((None, T, GW), lambda d, g, c: (d, chunk(d, c), g)),
                   pl.BlockSpec((None, None, None, SSM_STATE, GW), lambda d, g, c: (d, chunk(d, c), g, 0, 0))],
        out_shape=[jax.ShapeDtypeStruct((2, s, SSM_INNER), F32),
                   jax.ShapeDtypeStruct((2, nc, SSM_GROUPS, SSM_STATE, GW), F32)],
        scratch_shapes=[pltpu.VMEM((SSM_STATE, GW), F32)],
        compiler_params=_params("parallel", "parallel", "arbitrary"),
    )(xbc, xbc, xbc, bt, ct, dtr, dtt, pr_rows, pr_cols)


def _ssd_bwd(xbc, bt, ct, dtr, dtt, pr_rows, pr_cols, dvec, hs, y2, dy, *, name):
    s = xbc.shape[0]
    nc = s // T
    chunk = lambda d, c: jnp.where(d == 0, nc - 1 - c, c)
    in_specs = _ssd_in_specs(chunk) + [
        pl.BlockSpec((1, GW), lambda d, g, c: (0, g)),
        pl.BlockSpec((None, None, None, SSM_STATE, GW), lambda d, g, c: (d, chunk(d, c), g, 0, 0)),
        pl.BlockSpec((None, T, GW), lambda d, g, c: (d, chunk(d, c), g)),
        pl.BlockSpec((T, GW), lambda d, g, c: (chunk(d, c), g)),
    ]

    def body(x_ref, b_ref, c_ref, bt_ref, ct_ref, dt_ref, dtt_ref, prr_ref, prc_ref, dvec_ref, hs_ref, y_ref, dy_ref,
             dxs_ref, db_ref, dc_ref, ddt_ref, dalog_ref, dbias_ref, g_ref):
        d, c = pl.program_id(0), pl.program_id(2)

        @pl.when(c == 0)
        def _():
            g_ref[...] = jnp.zeros_like(g_ref)
            dalog_ref[...] = jnp.zeros_like(dalog_ref)
            dbias_ref[...] = jnp.zeros_like(dbias_ref)

        ck = _ssd_chunk(d, dt_ref, dtt_ref, prr_ref, prc_ref)
        h0 = ck["head0"]
        xs = x_ref[...].astype(F32)
        bm, cm = b_ref[...], c_ref[...]
        hst = hs_ref[...]
        gst = g_ref[...]
        dyv = dy_ref[...]
        yv = y_ref[...]
        dv = dvec_ref[...] * (1 - d).astype(F32)
        cb = lax.dot_general(cm, bm, _DIMS["nt"], preferred_element_type=F32)
        cbt = jnp.dot(bm, ct_ref[...], preferred_element_type=F32)
        ch = jnp.dot(cm, hst.astype(BF16), preferred_element_type=F32)
        bg = jnp.dot(bm, gst.astype(BF16), preferred_element_type=F32)
        hg_cols = jnp.sum(hst * gst, axis=0, keepdims=True)
        lane16 = lax.broadcasted_iota(jnp.int32, (T, 2 * HPG), 1)
        sub8 = lax.broadcasted_iota(jnp.int32, (HPG, 1), 0)
        dcb = jnp.zeros((T, T), F32)
        acc16 = jnp.zeros((T, 2 * HPG), F32)
        dtot = jnp.zeros((HPG, 1), F32)
        dxss, dyes, xgds, etots = [], [], [], []
        for q in range(PAIRS):
            sl = slice(q * 128, (q + 1) * 128)
            pr = _ssd_pair(ck, q)
            xsp, dyp = xs[:, sl], dyv[:, sl]
            xg = xsp * pr["dt"]
            xgb = xg.astype(BF16)
            dyb = dyp.astype(BF16)
            dy0, dy1 = _split_heads(h0, dyb)
            dcb = dcb + (lax.dot_general(dy0, xgb, _DIMS["nt"], preferred_element_type=F32) * pr["lm"][0]
                         + lax.dot_general(dy1, xgb, _DIMS["nt"], preferred_element_type=F32) * pr["lm"][1])
            dxg_in = (jnp.dot((cbt * pr["lmt"][0]).astype(BF16), dy0, preferred_element_type=F32)
                      + jnp.dot((cbt * pr["lmt"][1]).astype(BF16), dy1, preferred_element_type=F32))
            xgd = xg * pr["decay"]
            xb = xgd * bg[:, sl]
            dxg = dxg_in + pr["decay"] * bg[:, sl]
            yo = pr["ec"] * ch[:, sl]
            dac = dyb.astype(F32) * (yv[:, sl] - yo) + dyp * yo - xgb.astype(F32) * dxg_in - xb
            d_0, d_1 = _split_heads(h0, dac)
            r_0, r_1 = _split_heads(h0, dxg * xsp)
            for hh, (d_h, r_h) in enumerate(((d_0, r_0), (d_1, r_1))):
                j = 2 * q + hh
                acc16 = (acc16 + jnp.where(lane16 == j, jnp.sum(d_h, axis=-1, keepdims=True), 0.0)
                         + jnp.where(lane16 == HPG + j, jnp.sum(r_h, axis=-1, keepdims=True), 0.0))
            tcols = pr["etot"] * hg_cols[:, sl] + jnp.sum(xb, axis=0, keepdims=True)
            t0, t1 = _split_heads(h0[0:1], tcols)
            dtot = (dtot + jnp.where(sub8 == 2 * q, jnp.sum(t0, axis=-1, keepdims=True), 0.0)
                    + jnp.where(sub8 == 2 * q + 1, jnp.sum(t1, axis=-1, keepdims=True), 0.0))
            dxss.append(dxg * pr["dt"] + dv[:, sl] * dyp)
            dyes.append(dyp * pr["ec"])
            xgds.append(xgd)
            etots.append(pr["etot"])
        acc_t = acc16.T
        da_lanes = _scan_lanes(acc_t[0:HPG], d != 0)[0] + dtot
        ddt = da_lanes * ck["a_c"] + acc_t[HPG:2 * HPG]
        draw = ddt * _sigmoid(ck["raw_lanes"])
        ddt_ref[...] = draw
        dbias_ref[...] += jnp.sum(draw, axis=-1, keepdims=True)
        dalog_ref[...] += jnp.sum(da_lanes * ck["dt_lanes"], axis=-1, keepdims=True) * ck["a_c"]
        dxs_ref[...] = jnp.concatenate(dxss, axis=1)
        dye = jnp.concatenate(dyes, axis=1).astype(BF16)
        xgd_all = jnp.concatenate(xgds, axis=1).astype(BF16)
        dcbb = dcb.astype(BF16)
        dc_ref[...] = (jnp.dot(dcbb, bm, preferred_element_type=F32)
                       + lax.dot_general(dye, hst.astype(BF16), _DIMS["nt"], preferred_element_type=F32))
        db_ref[...] = (lax.dot_general(dcbb, cm, _DIMS["tn"], preferred_element_type=F32)
                       + lax.dot_general(xgd_all, gst.astype(BF16), _DIMS["nt"], preferred_element_type=F32))
        g_ref[...] = jnp.dot(ct_ref[...], dye, preferred_element_type=F32) + gst * jnp.concatenate(etots, axis=1)

    small = pl.BlockSpec((None, None, HPG, 1), lambda d, g, c: (d, g, 0, 0))
    sshape = jax.ShapeDtypeStruct((2, SSM_GROUPS, HPG, 1), F32)
    return pl.pallas_call(
        body, name=name, grid=(2, SSM_GROUPS, nc), in_specs=in_specs,
        out_specs=[pl.BlockSpec((None, T, GW), lambda d, g, c: (d, chunk(d, c), g)),
                   pl.BlockSpec((None, T, SSM_STATE), lambda d, g, c: (d, chunk(d, c), g)),
                   pl.BlockSpec((None, T, SSM_STATE), lambda d, g, c: (d, chunk(d, c), g)),
                   pl.BlockSpec((None, None, HPG, T), lambda d, g, c: (d, g, 0, chunk(d, c))), small, small],
        out_shape=[jax.ShapeDtypeStruct((2, s, SSM_INNER), F32),
                   jax.ShapeDtypeStruct((2, s, SSM_GROUPS * SSM_STATE), F32),
                   jax.ShapeDtypeStruct((2, s, SSM_GROUPS * SSM_STATE), F32),
                   jax.ShapeDtypeStruct((2, SSM_GROUPS, HPG, s), F32), sshape, sshape],
        scratch_shapes=[pltpu.VMEM((SSM_STATE, GW), F32)],
        compiler_params=_params("parallel", "parallel", "arbitrary"),
    )(xbc, xbc, xbc, bt, ct, dtr, dtt, pr_rows, pr_cols, dvec, hs, y2, dy)


def _gate_norm_fwd(y2, xbc, proj, dvec, nw, *, name):
    s = xbc.shape[0]
    tr = 256

    def body(y_ref, xs_ref, z_ref, dv_ref, w_ref, u_ref):
        yt = y_ref[0] + y_ref[1] + dv_ref[...] * xs_ref[...].astype(F32)
        yg = yt * _silu_and_grad(z_ref[...].astype(F32))[0]
        u_ref[...] = (yg * lax.rsqrt(jnp.mean(yg * yg, axis=-1, keepdims=True) + RMS_EPS) * w_ref[...]).astype(BF16)

    row = pl.BlockSpec((tr, SSM_INNER), lambda i: (i, 0))
    vec = pl.BlockSpec((1, SSM_INNER), lambda i: (0, 0))
    return pl.pallas_call(
        body, name=name, grid=(s // tr,),
        in_specs=[pl.BlockSpec((2, tr, SSM_INNER), lambda i: (0, i, 0)), row, row, vec, vec], out_specs=row,
        out_shape=jax.ShapeDtypeStruct((s, SSM_INNER), BF16), compiler_params=_params("parallel"),
    )(y2, xbc, proj, dvec, nw)


def _gate_norm_bwd(du, y2, xbc, proj, dvec, nw, *, name):
    s = xbc.shape[0]
    tr = 256

    def body(du_ref, y_ref, xs_ref, z_ref, dv_ref, w_ref, dy_ref, dz_ref, dw_ref, dd_ref):
        @pl.when(pl.program_id(0) == 0)
        def _():
            dw_ref[...] = jnp.zeros_like(dw_ref)
            dd_ref[...] = jnp.zeros_like(dd_ref)

        xs = xs_ref[...].astype(F32)
        yt = y_ref[0] + y_ref[1] + dv_ref[...] * xs
        si, dsi = _silu_and_grad(z_ref[...].astype(F32))
        yg = yt * si
        rstd = lax.rsqrt(jnp.mean(yg * yg, axis=-1, keepdims=True) + RMS_EPS)
        yhat = yg * rstd
        du = du_ref[...]
        dyn = du * w_ref[...]
        dyg = rstd * (dyn - yhat * jnp.mean(dyn * yhat, axis=-1, keepdims=True))
        dyt = dyg * si
        dy_ref[...] = dyt
        dz_ref[...] = (dyg * yt * dsi).astype(BF16)
        dw_ref[...] += jnp.sum(du * yhat, axis=0, keepdims=True)
        dd_ref[...] += jnp.sum(dyt * xs, axis=0, keepdims=True)

    row = pl.BlockSpec((tr, SSM_INNER), lambda i: (i, 0))
    vec = pl.BlockSpec((1, SSM_INNER), lambda i: (0, 0))
    vshape = jax.ShapeDtypeStruct((1, SSM_INNER), F32)
    return pl.pallas_call(
        body, name=name, grid=(s // tr,),
        in_specs=[row, pl.BlockSpec((2, tr, SSM_INNER), lambda i: (0, i, 0)), row, row, vec, vec],
        out_specs=[row, row, vec, vec],
        out_shape=[jax.ShapeDtypeStruct((s, SSM_INNER), F32), jax.ShapeDtypeStruct((s, SSM_INNER), BF16), vshape, vshape],
        compiler_params=_params("arbitrary"),
    )(du, y2, xbc, proj, dvec, nw)


def _layer_a_fwd(x, mod, w_in, w_out, ln_g, ln_b, tag):
    shift, scale, gate = mod
    h = _modulate(x, scale, shift, name=f"{tag}_modulate")
    proj = _mm(h, w_in, mode="nn", out_dtype=BF16, tm=512, tn=1024, tk=1024, name=f"{tag}_mm_in")
    os_, lses = [], []
    for grp in range(3):
        o, l = _attn_fwd(proj, grp, name=f"{tag}_attn_fwd{grp}")
        os_.append(o)
        lses.append(l)
    y, o, lse = _attn_combine(os_, lses, proj, name=f"{tag}_combine")
    out = _mm(y, w_out, mode="nn", out_dtype=F32, tm=512, tn=1024, tk=1024, name=f"{tag}_mm_out")
    xn = _resid_ln_fwd(x, out, gate, ln_g, ln_b, name=f"{tag}_resid_ln")
    return xn, (x, h, proj, y, o, lse, out)


def _layer_a_bwd(dxn, saved, mod, w_in, w_out, ln_g, tag):
    x, h, proj, y, o, lse, out = saved
    shift, scale, gate = mod
    dx_part, dout, dgate, dln_g, dln_b = _resid_ln_bwd(x, out, gate, ln_g, dxn, name=f"{tag}_resid_ln_bwd")
    dw_out = _mm(y, dout, mode="tn", out_dtype=F32, tm=1024, tn=1024, tk=512, name=f"{tag}_mm_dw_out")
    dy = _mm(dout, w_out, mode="nt", out_dtype=F32, tm=512, tn=1024, tk=1024, name=f"{tag}_mm_dy")
    do, dgp = _attn_combine_bwd(dy, o, proj, name=f"{tag}_combine_bwd")
    parts = []
    for grp in range(3):
        parts.extend(_attn_bwd(proj, grp, do, o, lse, name=f"{tag}_attn_bwd{grp}"))
    dproj = jnp.concatenate(parts + [dgp], axis=1)
    dw_in = _mm(h, dproj, mode="tn", out_dtype=F32, tm=1024, tn=1024, tk=512, name=f"{tag}_mm_dw_in")
    dx, dscale, dshift = _mm_dh(dproj, w_in, dx_part, x, scale, tm=512, tk=2048, name=f"{tag}_mm_dh")
    grads = dict(w_in=dw_in, w_out=dw_out, ln_g=dln_g, ln_b=dln_b, mod=jnp.concatenate([dshift, dscale, dgate], axis=1))
    return dx, grads


def _ssd_param_views(dt_raw, dt_bias, a_log):
    s = dt_raw.shape[0]
    r4 = dt_raw.reshape(s, 2, SSM_GROUPS, HPG)
    dtr = r4.transpose(1, 2, 0, 3)
    dtt = r4.transpose(1, 2, 3, 0)
    a = -jnp.exp(a_log)
    pr_rows = jnp.stack([dt_bias.reshape(2, SSM_GROUPS, HPG), a.reshape(2, SSM_GROUPS, HPG)], axis=2)
    return dtr, dtt, pr_rows, pr_rows.transpose(0, 1, 3, 2)


def _layer_b_fwd(x, mod, p, ln_g, ln_b, tag):
    shift, scale, gate = mod
    s = x.shape[0]
    h = _modulate(x, scale, shift, name=f"{tag}_modulate")
    proj = _mm(h, p["w_in"][:, :SSM_MAIN_COLS], mode="nn", out_dtype=BF16, tm=512, tn=1024, tk=1024, name=f"{tag}_mm_in")
    dt_raw = _mm(h, p["w_in"][:, SSM_MAIN_COLS:SSM_IN_COLS], mode="nn", out_dtype=F32, tm=512, tn=64, tk=1024,
                 name=f"{tag}_mm_dt")
    xpad = jnp.pad(proj[:, SSM_INNER:], ((CONV_HALO, CONV_HALO), (0, 0)))
    xbc = _conv_fwd(xpad, p["conv_w"], p["conv_b"], name=f"{tag}_conv")
    views = (xbc[:, SSM_INNER:SSM_INNER + SSM_GROUPS * SSM_STATE].T, xbc[:, SSM_INNER + SSM_GROUPS * SSM_STATE:].T,
             *_ssd_param_views(dt_raw, p["dt_bias"], p["a_log"]))
    y2, hs = _ssd_fwd(xbc, *views, name=f"{tag}_ssd_fwd")
    u = _gate_norm_fwd(y2, xbc, proj, p["dvec"], p["norm_w"], name=f"{tag}_gate_norm")
    out = _mm(u, p["w_out"], mode="nn", out_dtype=F32, tm=512, tn=1024, tk=2048, name=f"{tag}_mm_out")
    xn = _resid_ln_fwd(x, out, gate, ln_g, ln_b, name=f"{tag}_resid_ln")
    return xn, (x, h, proj, xpad, xbc, views, y2, hs, u, out)


def _layer_b_bwd(dxn, saved, mod, p, ln_g, tag):
    x, h, proj, xpad, xbc, views, y2, hs, u, out = saved
    shift, scale, gate = mod
    s = x.shape[0]
    dx_part, dout, dgate, dln_g, dln_b = _resid_ln_bwd(x, out, gate, ln_g, dxn, name=f"{tag}_resid_ln_bwd")
    dw_out = _mm(u, dout, mode="tn", out_dtype=F32, tm=1024, tn=1024, tk=512, name=f"{tag}_mm_dw_out")
    du = _mm(dout, p["w_out"], mode="nt", out_dtype=F32, tm=512, tn=1024, tk=1024, name=f"{tag}_mm_du")
    dy, dz, dnorm_w, dd_lanes = _gate_norm_bwd(du, y2, xbc, proj, p["dvec"], p["norm_w"], name=f"{tag}_gate_norm_bwd")
    dxs2, db2, dc2, ddt4, dalog, dbias = _ssd_bwd(xbc, *views, p["dvec"], hs, y2, dy, name=f"{tag}_ssd_bwd")
    dact = jnp.concatenate([dxs2[0] + dxs2[1], db2[0] + db2[1], dc2[0] + dc2[1]], axis=1)
    dapad = jnp.pad(dact, ((CONV_HALO, CONV_HALO), (0, 0)))
    dxbc, dconv_w, dconv_b = _conv_bwd(xpad, dapad, p["conv_w"], p["conv_b"], name=f"{tag}_conv_bwd")
    ddt_raw = ddt4.transpose(3, 0, 1, 2).reshape(s, 2 * SSM_HEADS).astype(BF16)
    dproj = jnp.concatenate([dz, dxbc, ddt_raw, jnp.zeros((s, SSM_PAD_COLS - SSM_IN_COLS), BF16)], axis=1)
    dw_in = _mm(h, dproj, mode="tn", out_dtype=F32, tm=1024, tn=896, tk=512, name=f"{tag}_mm_dw_in")[:, :SSM_IN_COLS]
    w_pad = jnp.pad(p["w_in"], ((0, 0), (0, SSM_PAD_COLS - SSM_IN_COLS)))
    dx, dscale, dshift = _mm_dh(dproj, w_pad, dx_part, x, scale, tm=512, tk=1792, name=f"{tag}_mm_dh")
    grads = dict(
        w_in=dw_in, w_out=dw_out, ln_g=dln_g, ln_b=dln_b, mod=jnp.concatenate([dshift, dscale, dgate], axis=1),
        conv_w=dconv_w, conv_b=dconv_b, norm_w=dnorm_w, dt_bias=dbias.reshape(2, SSM_HEADS),
        a_log=dalog.reshape(2, SSM_HEADS), d=jnp.sum(dd_lanes.reshape(SSM_HEADS, SSM_HEAD_DIM), axis=1))
    return dx, grads


def _local_step(x, target, mods, ln_g, ln_b, a_w_in, a_w_out, b_params):
    saved = []
    for i in range(DEPTH):
        j = i // 2
        g, b = ln_g[i:i + 1], ln_b[i:i + 1]
        if i % 2 == 0:
            x, sv = _layer_a_fwd(x, mods[i], a_w_in[j], a_w_out[j], g, b, f"l{i}")
        else:
            x, sv = _layer_b_fwd(x, mods[i], b_params[j], g, b, f"l{i}")
        saved.append(sv)
    dx, loss = _loss_and_grad(x, target, name="loss")
    grads = [None] * DEPTH
    for i in reversed(range(DEPTH)):
        j = i // 2
        g = ln_g[i:i + 1]
        if i % 2 == 0:
            dx, grads[i] = _layer_a_bwd(dx, saved[i], mods[i], a_w_in[j], a_w_out[j], g, f"l{i}")
        else:
            dx, grads[i] = _layer_b_bwd(dx, saved[i], mods[i], b_params[j], g, f"l{i}")
    return loss, dx, grads


def _mesh_pos():
    return lax.axis_index("x"), lax.axis_index("y"), lax.axis_index("c")


def _all_gather(x, *, name):
    def body(x_ref, out_ref, send_sems, recv_sems, local_sem):
        ax, ay, ac = _mesh_pos()
        me, sibling = (ax, ay, ac), (ax, ay, 1 - ac)
        chips = [(1 - ax, ay), (ax, 1 - ay), (1 - ax, 1 - ay)]

        def slot(px, py, pc):
            return out_ref.at[4 * px + 2 * py + pc]

        def copy(k, block, to, src=None):
            return pltpu.make_async_remote_copy(
                src_ref=slot(*block) if src is None else src, dst_ref=slot(*block),
                send_sem=send_sems.at[k], recv_sem=recv_sems.at[k], device_id=to, device_id_type=MESH)

        mine = pltpu.make_async_copy(x_ref, slot(*me), local_sem)
        mine.start()
        first = [copy(0, me, sibling, src=x_ref)]
        first += [copy(1 + j, me, (*chip, ac), src=x_ref) for j, chip in enumerate(chips)]
        for cp in first:
            cp.start()
        passed = [copy(4 + j, (*chip, ac), sibling) for j, chip in enumerate(chips)]
        for j, chip in enumerate(chips):
            copy(1 + j, (*chip, ac), me).wait_recv()
            passed[j].start()
        copy(0, sibling, me).wait_recv()
        for j, chip in enumerate(chips):
            copy(4 + j, (*chip, 1 - ac), me).wait_recv()
        for cp in first + passed:
            cp.wait_send()
        mine.wait()

    return pl.pallas_call(
        body, name=name, out_shape=jax.ShapeDtypeStruct((N_DEV,) + x.shape, x.dtype),
        in_specs=[pl.BlockSpec(memory_space=pl.ANY)], out_specs=pl.BlockSpec(memory_space=pl.ANY),
        scratch_shapes=[pltpu.SemaphoreType.DMA((7,)), pltpu.SemaphoreType.DMA((7,)), pltpu.SemaphoreType.DMA],
    )(x)


def _all_to_all(x, *, name):
    def body(x_ref, out_ref, send_sems, recv_sems, local_sem):
        ax, ay, ac = _mesh_pos()
        me = 4 * ax + 2 * ay + ac
        mine = pltpu.make_async_copy(x_ref.at[me], out_ref.at[me], local_sem)
        mine.start()
        copies = []
        for k in range(1, N_DEV):
            px = 1 - ax if k & 4 else ax
            py = 1 - ay if k & 2 else ay
            pc = 1 - ac if k & 1 else ac
            copies.append(pltpu.make_async_remote_copy(
                src_ref=x_ref.at[4 * px + 2 * py + pc], dst_ref=out_ref.at[me],
                send_sem=send_sems.at[k - 1], recv_sem=recv_sems.at[k - 1], device_id=(px, py, pc), device_id_type=MESH))
        for cp in copies:
            cp.start()
        for cp in copies:
            cp.wait()
        mine.wait()

    return pl.pallas_call(
        body, name=name, out_shape=jax.ShapeDtypeStruct(x.shape, x.dtype),
        in_specs=[pl.BlockSpec(memory_space=pl.ANY)], out_specs=pl.BlockSpec(memory_space=pl.ANY),
        scratch_shapes=[pltpu.SemaphoreType.DMA((7,)), pltpu.SemaphoreType.DMA((7,)), pltpu.SemaphoreType.DMA],
    )(x)


ADA_LOCAL = 3 * D_MODEL // N_DEV


def _ada_mod(c_all, ada_w, ada_b_local, *, name):
    def body(c_ref, w_ref, b_ref, o_ref):
        cond = _silu_and_grad(c_ref[...])[0]
        o_ref[...] = jnp.dot(cond, w_ref[...], precision=HIGHEST, preferred_element_type=F32) + b_ref[...]

    return pl.pallas_call(
        body, name=name, grid=(DEPTH,),
        in_specs=[pl.BlockSpec((N_DEV, D_MODEL), lambda i: (0, 0)), pl.BlockSpec((None, D_MODEL, ADA_LOCAL), lambda i: (i, 0, 0)),
                  pl.BlockSpec((None, 1, ADA_LOCAL), lambda i: (i, 0, 0))],
        out_specs=pl.BlockSpec((None, N_DEV, ADA_LOCAL), lambda i: (i, 0, 0)),
        out_shape=jax.ShapeDtypeStruct((DEPTH, N_DEV, ADA_LOCAL), F32), compiler_params=_params("parallel"),
    )(c_all, ada_w, ada_b_local)


def _ada_grad(c_all_t, dmod_local, *, name):
    def body(ct_ref, dm_ref, o_ref):
        cond_t = _silu_and_grad(ct_ref[...])[0]
        dm = dm_ref[...]
        acc = cond_t[:, 0:1] * dm[0:1, :]
        for smp in range(1, N_DEV):
            acc = acc + cond_t[:, smp:smp + 1] * dm[smp:smp + 1, :]
        o_ref[...] = acc

    return pl.pallas_call(
        body, name=name, grid=(DEPTH,),
        in_specs=[pl.BlockSpec((D_MODEL, N_DEV), lambda i: (0, 0)), pl.BlockSpec((None, N_DEV, ADA_LOCAL), lambda i: (i, 0, 0))],
        out_specs=pl.BlockSpec((None, D_MODEL, ADA_LOCAL), lambda i: (i, 0, 0)),
        out_shape=jax.ShapeDtypeStruct((DEPTH, D_MODEL, ADA_LOCAL), F32), compiler_params=_params("parallel"),
    )(c_all_t, dmod_local)


def _sum_devices(parts, *, name):
    n = parts.shape[1]

    def body(p_ref, o_ref):
        acc = p_ref[0:1, :]
        for dev in range(1, N_DEV):
            acc = acc + p_ref[dev:dev + 1, :]
        o_ref[...] = acc

    return pl.pallas_call(
        body, name=name, out_shape=jax.ShapeDtypeStruct((1, n), F32),
        in_specs=[pl.BlockSpec(memory_space=pltpu.VMEM)], out_specs=pl.BlockSpec(memory_space=pltpu.VMEM),
        compiler_params=pltpu.CompilerParams(vmem_limit_bytes=VMEM_LIMIT_BYTES),
    )(parts)


ADAMW_VMEM_BYTES = 24 * 1024 * 1024


def _adamw(w, m, v, g, *, name):
    r, c = w.shape
    summed = g.ndim == 3
    tr = r
    arrays = 7 + (N_DEV if summed else 1)
    while tr % 16 == 0 and 2 * arrays * tr * c * 4 > ADAMW_VMEM_BYTES:
        tr //= 2

    def body(w_ref, m_ref, v_ref, g_ref, go_ref, d_ref, mo_ref, vo_ref):
        if summed:
            g = g_ref[0].astype(F32)
            for dev in range(1, N_DEV):
                g = g + g_ref[dev].astype(F32)
        else:
            g = g_ref[...]
        mn = ADAM_B1 * m_ref[...] + (1.0 - ADAM_B1) * g
        vn = ADAM_B2 * v_ref[...] + (1.0 - ADAM_B2) * (g * g)
        m_hat = mn / (1.0 - ADAM_B1 ** ADAM_STEP)
        v_hat = vn / (1.0 - ADAM_B2 ** ADAM_STEP)
        go_ref[...] = g
        d_ref[...] = -ADAM_LR * (m_hat / (jnp.sqrt(v_hat) + ADAM_EPS) + ADAM_WD * w_ref[...])
        mo_ref[...] = mn
        vo_ref[...] = vn

    row = pl.BlockSpec((tr, c), lambda i: (i, 0))
    gspec = pl.BlockSpec((N_DEV, tr, c), lambda i: (0, i, 0)) if summed else row
    shp = jax.ShapeDtypeStruct((r, c), F32)
    return pl.pallas_call(
        body, name=name, grid=(r // tr,), in_specs=[row, row, row, gspec], out_specs=[row] * 4, out_shape=[shp] * 4,
        compiler_params=_params("parallel"),
    )(w, m, v, g)


def _pack(arrays):
    flat = jnp.concatenate([a.reshape(-1) for a in arrays])
    n = flat.shape[0]
    return jnp.pad(flat, (0, -n % 128)).reshape(1, -1)


def _unpack(vec, shapes):
    out, at = [], 0
    for shp in shapes:
        n = math.prod(shp)
        out.append(vec[at:at + n].reshape(shp))
        at += n
    return out


def _unpack_rows(rows, shapes):
    out, at = [], 0
    for shp in shapes:
        n = math.prod(shp)
        out.append(rows[:, at:at + n].reshape((rows.shape[0],) + tuple(shp)))
        at += n
    return out


def _my_shard(full, me, axis):
    width = full.shape[axis] // N_DEV
    return lax.dynamic_slice_in_dim(full, me * width, width, axis)


def _gather_cols(g, lead):
    nd = g.ndim
    perm = tuple(range(1, nd - 1)) + (0, nd - 1)
    t = g.transpose(perm)
    return t.reshape(t.shape[:-2] + (t.shape[-2] * t.shape[-1],))


def kernel(x, c, ada_w, ada_b, ln_g, ln_b, a_w_in, a_w_out, b_w_in, b_conv_w, b_conv_b, b_dt_bias, b_a_log, b_d, b_norm_w, b_w_out, loss_target, m_ada_w, m_ada_b, m_ln_g, m_ln_b, m_a_w_in, m_a_w_out, m_b_w_in, m_b_conv_w, m_b_conv_b, m_b_dt_bias, m_b_a_log, m_b_d, m_b_norm_w, m_b_w_out, v_ada_w, v_ada_b, v_ln_g, v_ln_b, v_a_w_in, v_a_w_out, v_b_w_in, v_b_conv_w, v_b_conv_b, v_b_dt_bias, v_b_a_log, v_b_d, v_b_norm_w, v_b_w_out):
    ax, ay, ac = _mesh_pos()
    me = 4 * ax + 2 * ay + ac
    seq = x.shape[1]

    small_shapes = [(1, D_MODEL), (2, SSM_CONV, ADA_LOCAL), (2, ADA_LOCAL), (2, SSM_INNER // N_DEV)]
    sg = _all_gather(_pack([c, b_conv_w, b_conv_b, b_norm_w]), name="gather_small")[:, 0, :]
    c_all, conv_w_g, conv_b_g, norm_w_g = _unpack_rows(sg, small_shapes)
    c_all = c_all[:, 0, :]
    conv_w = _gather_cols(conv_w_g, 2)
    conv_b = _gather_cols(conv_b_g[:, :, None, :], 2)
    norm_w = _gather_cols(norm_w_g[:, :, None, :], 2)

    def gathered(w, name):
        lead = w.shape[0]
        flat = w.astype(BF16).reshape(lead * w.shape[1], w.shape[2])
        return _all_gather(flat, name=name).reshape(N_DEV, lead, w.shape[1], w.shape[2])

    a_w_in_f = _gather_cols(gathered(a_w_in, "gather_a_w_in"), 2)
    b_w_in_f = _gather_cols(gathered(b_w_in, "gather_b_w_in"), 2)
    a_w_out_f = gathered(a_w_out, "gather_a_w_out").transpose(1, 0, 2, 3).reshape(2, A_WIDTH, D_MODEL)
    b_w_out_f = gathered(b_w_out, "gather_b_w_out").transpose(1, 0, 2, 3).reshape(2, SSM_INNER, D_MODEL)

    ada_b_local = _my_shard(ada_b, me, 1)[:, None, :]
    mod_cols = _ada_mod(c_all, ada_w, ada_b_local, name="ada_mod")
    mod_g = _all_gather(mod_cols.reshape(1, -1), name="gather_mod").reshape(N_DEV, DEPTH, N_DEV, ADA_LOCAL)
    mod = lax.dynamic_index_in_dim(mod_g, me, axis=2, keepdims=False).transpose(1, 0, 2).reshape(DEPTH, 3 * D_MODEL)
    mods = [tuple(mod[i:i + 1, k * D_MODEL:(k + 1) * D_MODEL] for k in range(3)) for i in range(DEPTH)]

    b_params = [dict(w_in=b_w_in_f[j], w_out=b_w_out_f[j], conv_w=conv_w[j], conv_b=conv_b[j], norm_w=norm_w[j],
                     dt_bias=b_dt_bias[j], a_log=b_a_log[j], dvec=jnp.repeat(b_d[j], SSM_HEAD_DIM)[None, :])
                for j in range(2)]
    loss_lanes, dx, grads = _local_step(x[0], loss_target[0], mods, ln_g, ln_b, a_w_in_f, a_w_out_f, b_params)
    loss = lax.psum(loss_lanes[0, 0], ("x", "y", "c"))
    grad_x = dx[None]

    a_layers, b_layers = (grads[0], grads[2]), (grads[1], grads[3])
    part_shapes = [(DEPTH, 3 * D_MODEL), (DEPTH, D_MODEL), (DEPTH, D_MODEL), (2, SSM_CONV, SSM_CONV_DIM),
                   (2, SSM_CONV_DIM), (2, SSM_INNER), (2, 2, SSM_HEADS), (2, 2, SSM_HEADS), (2, SSM_HEADS)]
    parts = _pack([
        jnp.concatenate([g["mod"] for g in grads]), jnp.concatenate([g["ln_g"] for g in grads]),
        jnp.concatenate([g["ln_b"] for g in grads]), jnp.stack([g["conv_w"] for g in b_layers]),
        jnp.stack([g["conv_b"][0] for g in b_layers]), jnp.stack([g["norm_w"][0] for g in b_layers]),
        jnp.stack([g["dt_bias"] for g in b_layers]), jnp.stack([g["a_log"] for g in b_layers]),
        jnp.stack([g["d"] for g in b_layers])])
    parts_g = _all_gather(parts, name="gather_small_grads")[:, 0, :]
    (g_ada_b, g_ln_g, g_ln_b, g_conv_w, g_conv_b, g_norm_w, g_dt_bias, g_a_log, g_d) = _unpack(
        _sum_devices(parts_g, name="sum_small_grads")[0], part_shapes)
    dmod_all = parts_g[:, :DEPTH * 3 * D_MODEL].reshape(N_DEV, DEPTH, N_DEV, ADA_LOCAL)
    dmod_local = lax.dynamic_index_in_dim(dmod_all, me, axis=2, keepdims=False).transpose(1, 0, 2)
    g_ada_w = _ada_grad(c_all.T, dmod_local, name="ada_grad")

    def scatter_cols(per_layer, name):
        st = jnp.stack(per_layer)
        lead, r, cols = st.shape
        blocks = st.reshape(lead, r, N_DEV, cols // N_DEV).transpose(2, 0, 1, 3).reshape(N_DEV, lead * r, cols // N_DEV)
        return _all_to_all(blocks.astype(BF16), name=name)

    def scatter_rows(per_layer, name):
        st = jnp.stack(per_layer)
        lead, rows, cols = st.shape
        blocks = st.reshape(lead, N_DEV, rows // N_DEV, cols).transpose(1, 0, 2, 3).reshape(N_DEV, lead * rows // N_DEV, cols)
        return _all_to_all(blocks.astype(BF16), name=name)

    r_a_w_in = scatter_cols([g["w_in"] for g in a_layers], "scatter_a_w_in")
    r_a_w_out = scatter_rows([g["w_out"] for g in a_layers], "scatter_a_w_out")
    r_b_w_in = scatter_cols([g["w_in"] for g in b_layers], "scatter_b_w_in")
    r_b_w_out = scatter_rows([g["w_out"] for g in b_layers], "scatter_b_w_out")

    def update(w, m, v, g, name):
        two_d = (-1, w.shape[-1])
        outs = _adamw(w.reshape(two_d), m.reshape(two_d), v.reshape(two_d), g, name=name)
        return [o.reshape(w.shape) for o in outs]

    up_ada_w = update(ada_w, m_ada_w, v_ada_w, g_ada_w.reshape(-1, ADA_LOCAL), "adamw_ada_w")
    up_a_w_in = update(a_w_in, m_a_w_in, v_a_w_in, r_a_w_in, "adamw_a_w_in")
    up_a_w_out = update(a_w_out, m_a_w_out, v_a_w_out, r_a_w_out, "adamw_a_w_out")
    up_b_w_in = update(b_w_in, m_b_w_in, v_b_w_in, r_b_w_in, "adamw_b_w_in")
    up_b_w_out = update(b_w_out, m_b_w_out, v_b_w_out, r_b_w_out, "adamw_b_w_out")

    small_w = [ada_b, ln_g, ln_b, b_conv_w, b_conv_b, b_dt_bias, b_a_log, b_d, b_norm_w]
    small_m = [m_ada_b, m_ln_g, m_ln_b, m_b_conv_w, m_b_conv_b, m_b_dt_bias, m_b_a_log, m_b_d, m_b_norm_w]
    small_v = [v_ada_b, v_ln_g, v_ln_b, v_b_conv_w, v_b_conv_b, v_b_dt_bias, v_b_a_log, v_b_d, v_b_norm_w]
    small_g = [g_ada_b, g_ln_g, g_ln_b, _my_shard(g_conv_w, me, 2), _my_shard(g_conv_b, me, 1), g_dt_bias, g_a_log, g_d,
               _my_shard(g_norm_w, me, 1)]
    shapes = [w.shape for w in small_w]
    packed = _adamw(_pack(small_w), _pack(small_m), _pack(small_v), _pack(small_g), name="adamw_small")
    (up_ada_b, up_ln_g, up_ln_b, up_conv_w, up_conv_b, up_dt_bias, up_a_log, up_d, up_norm_w) = zip(
        *[_unpack(p[0], shapes) for p in packed])

    ordered = [up_ada_w, up_ada_b, up_ln_g, up_ln_b, up_a_w_in, up_a_w_out, up_b_w_in, up_conv_w, up_conv_b,
               up_dt_bias, up_a_log, up_d, up_norm_w, up_b_w_out]
    return (loss, grad_x, *[u[0] for u in ordered], *[u[1] for u in ordered], *[u[2] for u in ordered],
            *[u[3] for u in ordered])
```

```python
import functools
import math

import jax
import jax.numpy as jnp
import numpy as np
from jax import lax
from jax.experimental import pallas as pl
from jax.experimental.pallas import tpu as pltpu

F32 = jnp.float32
BF16 = jnp.bfloat16
HIGHEST = lax.Precision.HIGHEST
MESH = pl.DeviceIdType.MESH

D_MODEL = 1024
DEPTH = 4
A_HEADS = 16
A_HEAD_DIM = 64
A_WIDTH = 1024
DILATIONS = (1, 4, 16)
A_RADIUS = 64
A_QBLOCK = 128
A_IN_COLS = 10240
SSM_INNER = 2048
SSM_HEADS = 32
SSM_HEAD_DIM = 64
SSM_STATE = 128
SSM_GROUPS = 4
SSM_CHUNK = 128
SSM_CONV = 5
SSM_CONV_DIM = 3072
SSM_IN_COLS = 5184
SSM_MAIN_COLS = 5120
SSM_PAD_COLS = 5376
CONV_HALO = 16
ALPHA = (2 * DEPTH) ** 0.25
LN_EPS = 1e-5
RMS_EPS = 1e-5
ADAM_LR, ADAM_B1, ADAM_B2, ADAM_EPS, ADAM_WD, ADAM_STEP = 0.001, 0.9, 0.999, 1e-08, 0.01, 10
N_DEV = 8
VMEM_LIMIT_BYTES = 56 * 1024 * 1024
NEG_BIG = -1e30


def _params(*sem):
    return pltpu.CompilerParams(dimension_semantics=sem, vmem_limit_bytes=VMEM_LIMIT_BYTES)


def _sigmoid(x):
    return 1.0 / (1.0 + jnp.exp(-x))


def _silu_and_grad(x):
    sg = _sigmoid(x)
    return x * sg, sg * (1.0 + x * (1.0 - sg))


def _softplus(x):
    e = jnp.exp(-jnp.abs(x))
    u = 1.0 + e
    log1p = jnp.where(u == 1.0, e, jnp.log(u) * (e / jnp.where(u == 1.0, 1.0, u - 1.0)))
    return jnp.maximum(x, 0.0) + log1p


_DIMS = {"nn": (((1,), (0,)), ((), ())), "nt": (((1,), (1,)), ((), ())), "tn": (((0,), (0,)), ((), ()))}


def _mm(a, b, *, mode, out_dtype, tm, tn, tk, name):
    if mode == "nn":
        (m, k), (_, n) = a.shape, b.shape
    elif mode == "nt":
        (m, k), (n, _) = a.shape, b.shape
    else:
        (k, m), (_, n) = a.shape, b.shape
    tm, tn, tk = min(tm, m), min(tn, n), min(tk, k)
    assert m % tm == 0 and n % tn == 0 and k % tk == 0, (name, a.shape, b.shape)
    nk = k // tk
    dims = _DIMS[mode]

    def body(a_ref, b_ref, o_ref, *scratch):
        part = lax.dot_general(a_ref[...], b_ref[...], dims, preferred_element_type=F32)
        if nk == 1:
            o_ref[...] = part.astype(o_ref.dtype)
            return
        acc_ref, = scratch
        kk = pl.program_id(2)

        @pl.when(kk == 0)
        def _():
            acc_ref[...] = part

        @pl.when(kk > 0)
        def _():
            acc_ref[...] += part

        @pl.when(kk == nk - 1)
        def _():
            o_ref[...] = acc_ref[...].astype(o_ref.dtype)

    if mode == "tn":
        a_spec = pl.BlockSpec((tk, tm), lambda i, j, kk: (kk, i))
    else:
        a_spec = pl.BlockSpec((tm, tk), lambda i, j, kk: (i, kk))
    if mode == "nt":
        b_spec = pl.BlockSpec((tn, tk), lambda i, j, kk: (j, kk))
    else:
        b_spec = pl.BlockSpec((tk, tn), lambda i, j, kk: (kk, j))
    return pl.pallas_call(
        body, name=name, grid=(m // tm, n // tn, nk),
        in_specs=[a_spec, b_spec], out_specs=pl.BlockSpec((tm, tn), lambda i, j, kk: (i, j)),
        out_shape=jax.ShapeDtypeStruct((m, n), out_dtype),
        scratch_shapes=[] if nk == 1 else [pltpu.VMEM((tm, tn), F32)],
        compiler_params=_params("parallel", "parallel", "arbitrary"),
    )(a, b)


def _mm_dh(dproj, w, dx_part, x, scale, *, tm, tk, name):
    s, k = dproj.shape
    d = w.shape[0]
    tk = min(tk, k)
    assert s % tm == 0 and k % tk == 0
    nk = k // tk

    def body(a_ref, w_ref, dxp_ref, x_ref, sc_ref, dx_ref, dsc_ref, dsh_ref, acc_ref):
        i, kk = pl.program_id(0), pl.program_id(1)
        part = lax.dot_general(a_ref[...], w_ref[...], _DIMS["nt"], preferred_element_type=F32)

        @pl.when(kk == 0)
        def _():
            acc_ref[...] = part

        @pl.when(kk > 0)
        def _():
            acc_ref[...] += part

        @pl.when(jnp.logical_and(i == 0, kk == 0))
        def _():
            dsc_ref[...] = jnp.zeros_like(dsc_ref)
            dsh_ref[...] = jnp.zeros_like(dsh_ref)

        @pl.when(kk == nk - 1)
        def _():
            dh = acc_ref[...]
            dx_ref[...] = dxp_ref[...] + dh * (1.0 + sc_ref[...])
            dsc_ref[...] += jnp.sum(dh * x_ref[...], axis=0, keepdims=True)
            dsh_ref[...] += jnp.sum(dh, axis=0, keepdims=True)

    row = pl.BlockSpec((tm, d), lambda i, kk: (i, 0))
    vec = pl.BlockSpec((1, d), lambda i, kk: (0, 0))
    return pl.pallas_call(
        body, name=name, grid=(s // tm, nk),
        in_specs=[pl.BlockSpec((tm, tk), lambda i, kk: (i, kk)), pl.BlockSpec((d, tk), lambda i, kk: (0, kk)),
                  row, row, vec],
        out_specs=[row, vec, vec],
        out_shape=[jax.ShapeDtypeStruct((s, d), F32), jax.ShapeDtypeStruct((1, d), F32),
                   jax.ShapeDtypeStruct((1, d), F32)],
        scratch_shapes=[pltpu.VMEM((tm, d), F32)],
        compiler_params=_params("arbitrary", "arbitrary"),
    )(dproj, w, dx_part, x, scale)


ROW_TILE = 512


def _modulate(x, scale, shift, *, name):
    s, d = x.shape

    def body(x_ref, sc_ref, sh_ref, h_ref):
        h_ref[...] = (x_ref[...] * (1.0 + sc_ref[...]) + sh_ref[...]).astype(BF16)

    row = pl.BlockSpec((ROW_TILE, d), lambda i: (i, 0))
    vec = pl.BlockSpec((1, d), lambda i: (0, 0))
    return pl.pallas_call(
        body, name=name, grid=(s // ROW_TILE,), in_specs=[row, vec, vec], out_specs=row,
        out_shape=jax.ShapeDtypeStruct((s, d), BF16), compiler_params=_params("parallel"),
    )(x, scale, shift)


def _resid_ln_fwd(x, out, gate, g, b, *, name):
    s, d = x.shape

    def body(x_ref, o_ref, gate_ref, g_ref, b_ref, y_ref):
        r = ALPHA * x_ref[...] + gate_ref[...] * o_ref[...]
        mu = jnp.mean(r, axis=-1, keepdims=True)
        rc = r - mu
        var = jnp.mean(rc * rc, axis=-1, keepdims=True)
        y_ref[...] = rc * lax.rsqrt(var + LN_EPS) * g_ref[...] + b_ref[...]

    row = pl.BlockSpec((ROW_TILE, d), lambda i: (i, 0))
    vec = pl.BlockSpec((1, d), lambda i: (0, 0))
    return pl.pallas_call(
        body, name=name, grid=(s // ROW_TILE,), in_specs=[row, row, vec, vec, vec], out_specs=row,
        out_shape=jax.ShapeDtypeStruct((s, d), F32), compiler_params=_params("parallel"),
    )(x, out, gate, g, b)


def _resid_ln_bwd(x, out, gate, g, dy, *, name):
    s, d = x.shape

    def body(x_ref, o_ref, gate_ref, g_ref, dy_ref, dxp_ref, dout_ref, dgate_ref, dg_ref, db_ref):
        @pl.when(pl.program_id(0) == 0)
        def _():
            dgate_ref[...] = jnp.zeros_like(dgate_ref)
            dg_ref[...] = jnp.zeros_like(dg_ref)
            db_ref[...] = jnp.zeros_like(db_ref)

        o = o_ref[...]
        r = ALPHA * x_ref[...] + gate_ref[...] * o
        mu = jnp.mean(r, axis=-1, keepdims=True)
        rc = r - mu
        var = jnp.mean(rc * rc, axis=-1, keepdims=True)
        rstd = lax.rsqrt(var + LN_EPS)
        xhat = rc * rstd
        dy = dy_ref[...]
        dxh = dy * g_ref[...]
        dr = rstd * (dxh - jnp.mean(dxh, axis=-1, keepdims=True) - xhat * jnp.mean(dxh * xhat, axis=-1, keepdims=True))
        dxp_ref[...] = ALPHA * dr
        dout_ref[...] = (gate_ref[...] * dr).astype(BF16)
        dgate_ref[...] += jnp.sum(dr * o, axis=0, keepdims=True)
        dg_ref[...] += jnp.sum(dy * xhat, axis=0, keepdims=True)
        db_ref[...] += jnp.sum(dy, axis=0, keepdims=True)

    row = pl.BlockSpec((ROW_TILE, d), lambda i: (i, 0))
    vec = pl.BlockSpec((1, d), lambda i: (0, 0))
    vshape = jax.ShapeDtypeStruct((1, d), F32)
    return pl.pallas_call(
        body, name=name, grid=(s // ROW_TILE,), in_specs=[row, row, vec, vec, row],
        out_specs=[row, row, vec, vec, vec],
        out_shape=[jax.ShapeDtypeStruct((s, d), F32), jax.ShapeDtypeStruct((s, d), BF16), vshape, vshape, vshape],
        compiler_params=_params("arbitrary"),
    )(x, out, gate, g, dy)


def _loss_and_grad(y, target, *, name):
    s, d = y.shape

    def body(y_ref, t_ref, dy_ref, loss_ref):
        @pl.when(pl.program_id(0) == 0)
        def _():
            loss_ref[...] = jnp.zeros_like(loss_ref)

        e = y_ref[...] - t_ref[...]
        dy_ref[...] = e * (1.0 / d)
        loss_ref[...] += jnp.sum(jnp.sum(e * e, axis=0, keepdims=True), axis=1, keepdims=True) * (0.5 / d)

    row = pl.BlockSpec((ROW_TILE, d), lambda i: (i, 0))
    return pl.pallas_call(
        body, name=name, grid=(s // ROW_TILE,), in_specs=[row, row],
        out_specs=[row, pl.BlockSpec((1, 128), lambda i: (0, 0))],
        out_shape=[jax.ShapeDtypeStruct((s, d), F32), jax.ShapeDtypeStruct((1, 128), F32)],
        compiler_params=_params("arbitrary"),
    )(y, target)


_SLOPES = np.asarray(2.0 ** (-8.0 * (np.arange(A_HEADS, dtype=np.float32) + 1.0) / A_HEADS), dtype=np.float32)


def _attn_scores(q, kw, slope, dist, valid):
    s = lax.dot_general(q, kw, _DIMS["nt"], preferred_element_type=F32) * (1.0 / math.sqrt(A_HEAD_DIM))
    return jnp.where(valid, s - slope * dist, NEG_BIG)


def _attn_window(blk, length, win, dil):
    start = pl.multiple_of(jnp.clip(blk * A_QBLOCK - A_RADIUS, 0, length - win), A_RADIUS)
    qpos = blk * A_QBLOCK + lax.broadcasted_iota(jnp.int32, (A_QBLOCK, win), 0)
    kpos = start + lax.broadcasted_iota(jnp.int32, (A_QBLOCK, win), 1)
    delta = jnp.abs(kpos - qpos)
    return start, (delta * dil).astype(F32), delta <= A_RADIUS


A_BLOCKS_PER_STEP = 4
A_GROUP_COLS = 3 * A_WIDTH


def _attn_view(proj, group):
    s = proj.shape[0]
    dil = DILATIONS[group]
    if dil == 1:
        return proj, proj.shape[1] // 128, group * (A_GROUP_COLS // 128)
    cols = proj[:, group * A_GROUP_COLS:(group + 1) * A_GROUP_COLS]
    return cols.reshape(s // dil, dil * A_GROUP_COLS), A_GROUP_COLS // 128, 0


def _attn_fwd(proj, group, *, name):
    pv, cb, qoff = _attn_view(proj, group)
    s = proj.shape[0]
    dil = DILATIONS[group]
    length = s // dil
    win = min(2 * A_QBLOCK, length)
    nblk = length // A_QBLOCK
    per = A_BLOCKS_PER_STEP if nblk % A_BLOCKS_PER_STEP == 0 else 1

    def body(slope_ref, q_ref, k_ref, v_ref, o_ref, lse_ref):
        hp = pl.program_id(1)
        for u in range(per):
            rows = slice(u * A_QBLOCK, (u + 1) * A_QBLOCK)
            start, dist, valid = _attn_window(pl.program_id(2) * per + u, length, win, dil)
            kw = k_ref[pl.ds(start, win), :]
            vw = v_ref[pl.ds(start, win), :]
            q = q_ref[rows, :]
            outs, lses = [], []
            for hh in range(2):
                sl = slice(hh * A_HEAD_DIM, (hh + 1) * A_HEAD_DIM)
                sc = _attn_scores(q[:, sl], kw[:, sl], slope_ref[hp * 2 + hh], dist, valid)
                m = jnp.max(sc, axis=-1, keepdims=True)
                p = jnp.exp(sc - m)
                z = jnp.sum(p, axis=-1, keepdims=True)
                o = jnp.dot(p.astype(BF16), vw[:, sl], preferred_element_type=F32) / z
                outs.append(o)
                lses.append(jnp.broadcast_to(m + jnp.log(z), (A_QBLOCK, A_HEAD_DIM)))
            o_ref[rows, :] = jnp.concatenate(outs, axis=1)
            lse_ref[rows, :] = jnp.concatenate(lses, axis=1)

    qspec = pl.BlockSpec((per * A_QBLOCK, 128), lambda r, hp, b: (b, r * cb + qoff + hp))
    kspec = pl.BlockSpec((length, 128), lambda r, hp, b: (0, r * cb + qoff + 8 + hp))
    vspec = pl.BlockSpec((length, 128), lambda r, hp, b: (0, r * cb + qoff + 16 + hp))
    ospec = pl.BlockSpec((per * A_QBLOCK, 128), lambda r, hp, b: (b, r * 8 + hp))
    oshape = jax.ShapeDtypeStruct((length, dil * A_WIDTH), F32)
    o, lse = pl.pallas_call(
        body, name=name, grid=(dil, 8, nblk // per),
        in_specs=[pl.BlockSpec(memory_space=pltpu.SMEM), qspec, kspec, vspec], out_specs=[ospec, ospec],
        out_shape=[oshape, oshape], compiler_params=_params("parallel", "parallel", "arbitrary"),
    )(jnp.asarray(_SLOPES), pv, pv, pv)
    return o.reshape(s, A_WIDTH), lse.reshape(s, A_WIDTH)


def _attn_bwd(proj, group, do, o, lse, *, name):
    pv, cb, qoff = _attn_view(proj, group)
    s = proj.shape[0]
    dil = DILATIONS[group]
    length = s // dil
    win = min(2 * A_QBLOCK, length)
    nblk = length // A_QBLOCK
    per = A_BLOCKS_PER_STEP if nblk % A_BLOCKS_PER_STEP == 0 else 1
    nstep = nblk // per
    scale = 1.0 / math.sqrt(A_HEAD_DIM)
    view = lambda t: t.reshape(length, dil * A_WIDTH)

    def body(slope_ref, q_ref, k_ref, v_ref, do_ref, o_ref, lse_ref, dq_ref, dk_ref, dv_ref, dk_acc, dv_acc):
        hp, step = pl.program_id(1), pl.program_id(2)

        @pl.when(step == 0)
        def _():
            dk_acc[...] = jnp.zeros_like(dk_acc)
            dv_acc[...] = jnp.zeros_like(dv_acc)

        for u in range(per):
            rows = slice(u * A_QBLOCK, (u + 1) * A_QBLOCK)
            start, dist, valid = _attn_window(step * per + u, length, win, dil)
            kw = k_ref[pl.ds(start, win), :]
            vw = v_ref[pl.ds(start, win), :]
            q = q_ref[rows, :]
            do_b = do_ref[rows, :]
            dsum = do_b.astype(F32) * o_ref[rows, :]
            lse_b = lse_ref[rows, :]
            dqs, dks, dvs = [], [], []
            for hh in range(2):
                sl = slice(hh * A_HEAD_DIM, (hh + 1) * A_HEAD_DIM)
                sc = _attn_scores(q[:, sl], kw[:, sl], slope_ref[hp * 2 + hh], dist, valid)
                p = jnp.exp(sc - lse_b[:, hh * A_HEAD_DIM:hh * A_HEAD_DIM + 1])
                dp = lax.dot_general(do_b[:, sl], vw[:, sl], _DIMS["nt"], preferred_element_type=F32)
                ds = (p * (dp - jnp.sum(dsum[:, sl], axis=-1, keepdims=True))).astype(BF16)
                dqs.append(jnp.dot(ds, kw[:, sl], preferred_element_type=F32) * scale)
                dks.append(lax.dot_general(ds, q[:, sl], _DIMS["tn"], preferred_element_type=F32) * scale)
                dvs.append(lax.dot_general(p.astype(BF16), do_b[:, sl], _DIMS["tn"], preferred_element_type=F32))
            dq_ref[rows, :] = jnp.concatenate(dqs, axis=1).astype(BF16)
            dk_acc[pl.ds(start, win), :] += jnp.concatenate(dks, axis=1)
            dv_acc[pl.ds(start, win), :] += jnp.concatenate(dvs, axis=1)

        @pl.when(step == nstep - 1)
        def _():
            dk_ref[...] = dk_acc[...].astype(BF16)
            dv_ref[...] = dv_acc[...].astype(BF16)

    qspec = pl.BlockSpec((per * A_QBLOCK, 128), lambda r, hp, b: (b, r * cb + qoff + hp))
    kspec = pl.BlockSpec((length, 128), lambda r, hp, b: (0, r * cb + qoff + 8 + hp))
    vspec = pl.BlockSpec((length, 128), lambda r, hp, b: (0, r * cb + qoff + 16 + hp))
    bspec = pl.BlockSpec((per * A_QBLOCK, 128), lambda r, hp, b: (b, r * 8 + hp))
    fspec = pl.BlockSpec((length, 128), lambda r, hp, b: (0, r * 8 + hp))
    oshape = jax.ShapeDtypeStruct((length, dil * A_WIDTH), BF16)
    dq, dk, dv = pl.pallas_call(
        body, name=name, grid=(dil, 8, nstep),
        in_specs=[pl.BlockSpec(memory_space=pltpu.SMEM), qspec, kspec, vspec, bspec, bspec, bspec],
        out_specs=[bspec, fspec, fspec], out_shape=[oshape, oshape, oshape],
        scratch_shapes=[pltpu.VMEM((length, 128), F32), pltpu.VMEM((length, 128), F32)],
        compiler_params=_params("parallel", "parallel", "arbitrary"),
    )(jnp.asarray(_SLOPES), pv, pv, pv, view(do), view(o), view(lse))
    return dq.reshape(s, A_WIDTH), dk.reshape(s, A_WIDTH), dv.reshape(s, A_WIDTH)


A_GATE_BLOCK = 9


def _attn_combine(os_, lses, proj, *, name):
    s = proj.shape[0]
    tr = 256

    def body(o0, o1, o2, l0, l1, l2, gate_ref, y_ref, o_ref, lse_ref):
        la, lb, lc = l0[...], l1[...], l2[...]
        m = jnp.maximum(jnp.maximum(la, lb), lc)
        ea, eb, ec = jnp.exp(la - m), jnp.exp(lb - m), jnp.exp(lc - m)
        den = ea + eb + ec
        o = (ea * o0[...] + eb * o1[...] + ec * o2[...]) / den
        o_ref[...] = o
        lse_ref[...] = m + jnp.log(den)
        y_ref[...] = (o * _silu_and_grad(gate_ref[...].astype(F32))[0]).astype(BF16)

    row = pl.BlockSpec((tr, A_WIDTH), lambda i: (i, 0))
    gspec = pl.BlockSpec((tr, A_WIDTH), lambda i: (i, A_GATE_BLOCK))
    return pl.pallas_call(
        body, name=name, grid=(s // tr,), in_specs=[row] * 6 + [gspec], out_specs=[row, row, row],
        out_shape=[jax.ShapeDtypeStruct((s, A_WIDTH), BF16), jax.ShapeDtypeStruct((s, A_WIDTH), F32),
                   jax.ShapeDtypeStruct((s, A_WIDTH), F32)],
        compiler_params=_params("parallel"),
    )(*os_, *lses, proj)


def _attn_combine_bwd(dy, o, proj, *, name):
    s = proj.shape[0]
    tr = 256

    def body(dy_ref, o_ref, gate_ref, do_ref, dg_ref):
        si, dsi = _silu_and_grad(gate_ref[...].astype(F32))
        dyv = dy_ref[...]
        do_ref[...] = (dyv * si).astype(BF16)
        dg_ref[...] = (dyv * o_ref[...] * dsi).astype(BF16)

    row = pl.BlockSpec((tr, A_WIDTH), lambda i: (i, 0))
    gspec = pl.BlockSpec((tr, A_WIDTH), lambda i: (i, A_GATE_BLOCK))
    shp = jax.ShapeDtypeStruct((s, A_WIDTH), BF16)
    return pl.pallas_call(
        body, name=name, grid=(s // tr,), in_specs=[row, row, gspec], out_specs=[row, row], out_shape=[shp, shp],
        compiler_params=_params("parallel"),
    )(dy, o, proj)


CONV_TILE = 256
CONV_SUB = 4


def _conv_taps(xe, n):
    return [xe if j == 2 else pltpu.roll(xe, (2 - j) % n, 0) for j in range(SSM_CONV)]


def _conv_fwd(xpad, w, b, *, name):
    s = xpad.shape[0] - 2 * CONV_HALO
    n = CONV_TILE + 2 * CONV_HALO
    ncol = SSM_CONV_DIM // 128

    sub = min(CONV_SUB, s // CONV_TILE)

    def body(x_ref, w_ref, b_ref, o_ref):
        base = pl.program_id(1) * (sub * CONV_TILE)

        def tile(k, carry):
            r0 = pl.multiple_of(k * CONV_TILE, CONV_TILE)
            t0 = pl.multiple_of(base + r0, CONV_TILE)
            taps = _conv_taps(x_ref[pl.ds(t0, n), :].astype(F32), n)
            pre = b_ref[...]
            for j in range(SSM_CONV):
                pre = pre + w_ref[j:j + 1, :] * taps[j]
            o_ref[pl.ds(r0, CONV_TILE), :] = _silu_and_grad(pre[CONV_HALO:CONV_HALO + CONV_TILE])[0].astype(BF16)
            return carry

        lax.fori_loop(0, sub, tile, 0)

    return pl.pallas_call(
        body, name=name, grid=(ncol, s // (sub * CONV_TILE)),
        in_specs=[pl.BlockSpec((s + 2 * CONV_HALO, 128), lambda j, i: (0, j)),
                  pl.BlockSpec((SSM_CONV, 128), lambda j, i: (0, j)), pl.BlockSpec((1, 128), lambda j, i: (0, j))],
        out_specs=pl.BlockSpec((sub * CONV_TILE, 128), lambda j, i: (i, j)),
        out_shape=jax.ShapeDtypeStruct((s, SSM_CONV_DIM), BF16), compiler_params=_params("parallel", "arbitrary"),
    )(xpad, w, b)


def _conv_bwd(xpad, dapad, w, b, *, name):
    s = xpad.shape[0] - 2 * CONV_HALO
    n = CONV_TILE + 2 * CONV_HALO
    ncol = SSM_CONV_DIM // 128
    mid = slice(CONV_HALO, CONV_HALO + CONV_TILE)
    sub = min(CONV_SUB, s // CONV_TILE)

    def body(x_ref, da_ref, w_ref, b_ref, dx_ref, dw_ref, db_ref):
        @pl.when(pl.program_id(1) == 0)
        def _():
            dw_ref[...] = jnp.zeros_like(dw_ref)
            db_ref[...] = jnp.zeros_like(db_ref)

        base = pl.program_id(1) * (sub * CONV_TILE)

        def tile(k, carry):
            r0 = pl.multiple_of(k * CONV_TILE, CONV_TILE)
            t0 = pl.multiple_of(base + r0, CONV_TILE)
            taps = _conv_taps(x_ref[pl.ds(t0, n), :].astype(F32), n)
            pre = b_ref[...]
            for j in range(SSM_CONV):
                pre = pre + w_ref[j:j + 1, :] * taps[j]
            dpre = da_ref[pl.ds(t0, n), :] * _silu_and_grad(pre)[1]
            dx = jnp.zeros((CONV_TILE, 128), F32)
            for j in range(SSM_CONV):
                back = dpre if j == 2 else pltpu.roll(dpre, (j - 2) % n, 0)
                dx = dx + w_ref[j:j + 1, :] * back[mid]
                dw_ref[j:j + 1, :] += jnp.sum(dpre[mid] * taps[j][mid], axis=0, keepdims=True)
            dx_ref[pl.ds(r0, CONV_TILE), :] = dx.astype(BF16)
            db_ref[...] += jnp.sum(dpre[mid], axis=0, keepdims=True)
            return carry

        lax.fori_loop(0, sub, tile, 0)

    full = pl.BlockSpec((s + 2 * CONV_HALO, 128), lambda j, i: (0, j))
    wspec = pl.BlockSpec((SSM_CONV, 128), lambda j, i: (0, j))
    bspec = pl.BlockSpec((1, 128), lambda j, i: (0, j))
    return pl.pallas_call(
        body, name=name, grid=(ncol, s // (sub * CONV_TILE)), in_specs=[full, full, wspec, bspec],
        out_specs=[pl.BlockSpec((sub * CONV_TILE, 128), lambda j, i: (i, j)), wspec, bspec],
        out_shape=[jax.ShapeDtypeStruct((s, SSM_CONV_DIM), BF16), jax.ShapeDtypeStruct((SSM_CONV, SSM_CONV_DIM), F32),
                   jax.ShapeDtypeStruct((1, SSM_CONV_DIM), F32)],
        compiler_params=_params("parallel", "arbitrary"),
    )(xpad, dapad, w, b)


HPG = SSM_HEADS // SSM_GROUPS
GW = HPG * SSM_HEAD_DIM
T = SSM_CHUNK


def _ssd_specs(nc):
    ceff = lambda d, c: jnp.where(d == 0, c, nc - 1 - c)
    return ceff, [
        pl.BlockSpec((T, GW), lambda d, g, c: (ceff(d, c), g)),
        pl.BlockSpec((T, SSM_STATE), lambda d, g, c: (ceff(d, c), SSM_INNER // 128 + g)),
        pl.BlockSpec((T, SSM_STATE), lambda d, g, c: (ceff(d, c), SSM_INNER // 128 + SSM_GROUPS + g)),
        pl.BlockSpec((None, None, T, HPG), lambda d, g, c: (d, g, ceff(d, c), 0)),
        pl.BlockSpec((None, None, HPG, T), lambda d, g, c: (d, g, 0, ceff(d, c))),
        pl.BlockSpec((None, None, 2, HPG), lambda d, g, c: (d, g, 0, 0)),
        pl.BlockSpec((None, None, HPG, 2), lambda d, g, c: (d, g, 0, 0)),
    ]


def _ssd_chunk_common(d, dt_ref, dtt_ref, prr_ref, prc_ref):
    sgn = 1 - 2 * d
    ri = lax.broadcasted_iota(jnp.int32, (T, T), 0)
    ci = lax.broadcasted_iota(jnp.int32, (T, T), 1)
    mask = ((ri - ci) * sgn) >= 0
    maskf = mask.astype(F32)
    bias_r, a_r = prr_ref[0:1, :], prr_ref[1:2, :]
    bias_c, a_c = prc_ref[:, 0:1], prc_ref[:, 1:2]
    raw = dt_ref[...] + bias_r
    dt_rows = _softplus(raw)
    dt_lanes = _softplus(dtt_ref[...] + bias_c)
    a_rows = dt_rows * a_r
    acum_rows = jnp.dot(maskf, a_rows, precision=HIGHEST, preferred_element_type=F32)
    acum_lanes = lax.dot_general(dt_lanes * a_c, maskf, _DIMS["nt"], precision=HIGHEST, preferred_element_type=F32)
    tot = jnp.sum(a_rows, axis=0, keepdims=True)
    return mask, maskf, raw, dt_rows, a_r, acum_rows, acum_lanes, tot


def _lanes_per_head(pieces):
    return jnp.concatenate([jnp.broadcast_to(p, (p.shape[0], SSM_HEAD_DIM)) for p in pieces], axis=1)


def _ssd_fwd_v1(xbc, dtr, dtt, pr_rows, pr_cols, *, name):
    s = xbc.shape[0]
    nc = s // T
    ceff, in_specs = _ssd_specs(nc)

    def body(x_ref, b_ref, c_ref, dt_ref, dtt_ref, prr_ref, prc_ref, y_ref, hs_ref, st_ref):
        d, c = pl.program_id(0), pl.program_id(2)

        @pl.when(c == 0)
        def _():
            st_ref[...] = jnp.zeros_like(st_ref)

        mask, _, _, dt_rows, _, acum_rows, acum_lanes, tot = _ssd_chunk_common(d, dt_ref, dtt_ref, prr_ref, prc_ref)
        xs = x_ref[...].astype(F32)
        bm, cm = b_ref[...], c_ref[...]
        hprev = st_ref[...]
        hs_ref[...] = hprev
        cb = lax.dot_general(cm, bm, _DIMS["nt"], preferred_element_type=F32)
        ch = jnp.dot(cm, hprev.astype(BF16), preferred_element_type=F32)
        ys, xgds, etots = [], [], []
        for j in range(HPG):
            sl = slice(j * SSM_HEAD_DIM, (j + 1) * SSM_HEAD_DIM)
            ac, al = acum_rows[:, j:j + 1], acum_lanes[j:j + 1, :]
            lm = jnp.where(mask, jnp.exp(jnp.minimum(ac - al, 0.0)), 0.0)
            xg = xs[:, sl] * dt_rows[:, j:j + 1]
            yd = jnp.dot((cb * lm).astype(BF16), xg.astype(BF16), preferred_element_type=F32)
            ys.append(yd + jnp.exp(ac) * ch[:, sl])
            xgds.append(xg * jnp.exp(tot[:, j:j + 1] - ac))
            etots.append(jnp.exp(tot[:, j:j + 1]))
        y_ref[...] = jnp.concatenate(ys, axis=1)
        new = lax.dot_general(bm, jnp.concatenate(xgds, axis=1).astype(BF16), _DIMS["tn"], preferred_element_type=F32)
        st_ref[...] = hprev * _lanes_per_head(etots) + new

    return pl.pallas_call(
        body, name=name, grid=(2, SSM_GROUPS, nc), in_specs=in_specs,
        out_specs=[pl.BlockSpec((None, T, GW), lambda d, g, c: (d, ceff(d, c), g)),
                   pl.BlockSpec((None, None, None, SSM_STATE, GW), lambda d, g, c: (d, ceff(d, c), g, 0, 0))],
        out_shape=[jax.ShapeDtypeStruct((2, s, SSM_INNER), F32),
                   jax.ShapeDtypeStruct((2, nc, SSM_GROUPS, SSM_STATE, GW), F32)],
        scratch_shapes=[pltpu.VMEM((SSM_STATE, GW), F32)],
        compiler_params=_params("parallel", "parallel", "arbitrary"),
    )(xbc, xbc, xbc, dtr, dtt, pr_rows, pr_cols)


def _put_lane(j, col):
    lane = lax.broadcasted_iota(jnp.int32, (col.shape[0], HPG), 1)
    return jnp.where(lane == j, col, 0.0)


def _ssd_bwd_v1(xbc, dtr, dtt, pr_rows, pr_cols, dvec, hs, dy, *, name):
    s = xbc.shape[0]
    nc = s // T
    cb_of = lambda d, c: jnp.where(d == 0, nc - 1 - c, c)
    in_specs = [
        pl.BlockSpec((T, GW), lambda d, g, c: (cb_of(d, c), g)),
        pl.BlockSpec((T, SSM_STATE), lambda d, g, c: (cb_of(d, c), SSM_INNER // 128 + g)),
        pl.BlockSpec((T, SSM_STATE), lambda d, g, c: (cb_of(d, c), SSM_INNER // 128 + SSM_GROUPS + g)),
        pl.BlockSpec((None, None, T, HPG), lambda d, g, c: (d, g, cb_of(d, c), 0)),
        pl.BlockSpec((None, None, HPG, T), lambda d, g, c: (d, g, 0, cb_of(d, c))),
        pl.BlockSpec((None, None, 2, HPG), lambda d, g, c: (d, g, 0, 0)),
        pl.BlockSpec((None, None, HPG, 2), lambda d, g, c: (d, g, 0, 0)),
        pl.BlockSpec((1, GW), lambda d, g, c: (0, g)),
        pl.BlockSpec((None, None, None, SSM_STATE, GW), lambda d, g, c: (d, cb_of(d, c), g, 0, 0)),
        pl.BlockSpec((T, GW), lambda d, g, c: (cb_of(d, c), g)),
    ]

    def body(x_ref, b_ref, c_ref, dt_ref, dtt_ref, prr_ref, prc_ref, dvec_ref, hs_ref, dy_ref,
             dxs_ref, db_ref, dc_ref, ddt_ref, dalog_ref, dbias_ref, g_ref):
        d, c = pl.program_id(0), pl.program_id(2)

        @pl.when(c == 0)
        def _():
            g_ref[...] = jnp.zeros_like(g_ref)
            dalog_ref[...] = jnp.zeros_like(dalog_ref)
            dbias_ref[...] = jnp.zeros_like(dbias_ref)

        mask, maskf, raw, dt_rows, a_r, acum_rows, acum_lanes, tot = _ssd_chunk_common(
            d, dt_ref, dtt_ref, prr_ref, prc_ref)
        xs = x_ref[...].astype(F32)
        bm, cm = b_ref[...], c_ref[...]
        hst = hs_ref[...]
        gst = g_ref[...]
        dyv = dy_ref[...]
        dyb = dyv.astype(BF16)
        dv = dvec_ref[...] * (1 - d).astype(F32)
        cb = lax.dot_general(cm, bm, _DIMS["nt"], preferred_element_type=F32)
        ch = jnp.dot(cm, hst.astype(BF16), preferred_element_type=F32)
        bg = jnp.dot(bm, gst.astype(BF16), preferred_element_type=F32)
        hg = hst * gst
        dcb = jnp.zeros((T, T), F32)
        dacum = jnp.zeros((T, HPG), F32)
        rx = jnp.zeros((T, HPG), F32)
        dtot = jnp.zeros((1, HPG), F32)
        dxss, dyes, xgds, etots = [], [], [], []
        for j in range(HPG):
            sl = slice(j * SSM_HEAD_DIM, (j + 1) * SSM_HEAD_DIM)
            ac, al = acum_rows[:, j:j + 1], acum_lanes[j:j + 1, :]
            lm = jnp.where(mask, jnp.exp(jnp.minimum(ac - al, 0.0)), 0.0)
            m = cb * lm
            dtj = dt_rows[:, j:j + 1]
            xsj = xs[:, sl]
            xg = xsj * dtj
            dyj = dyv[:, sl]
            ec = jnp.exp(ac)
            etot = jnp.exp(tot[:, j:j + 1])
            decay = jnp.exp(tot[:, j:j + 1] - ac)
            dm = lax.dot_general(dyb[:, sl], xg.astype(BF16), _DIMS["nt"], preferred_element_type=F32)
            w = dm * m
            dcb = dcb + dm * lm
            bgj = bg[:, sl]
            dxg = lax.dot_general(m.astype(BF16), dyb[:, sl], _DIMS["tn"], preferred_element_type=F32) + decay * bgj
            xb = decay * jnp.sum(xg * bgj, axis=-1, keepdims=True)
            da_j = (jnp.sum(w, axis=-1, keepdims=True) - jnp.sum(w.T, axis=-1, keepdims=True)
                    + jnp.sum(ec * ch[:, sl] * dyj, axis=-1, keepdims=True) - xb)
            dacum = dacum + _put_lane(j, da_j)
            rx = rx + _put_lane(j, jnp.sum(dxg * xsj, axis=-1, keepdims=True))
            dtot_j = (etot * jnp.sum(jnp.sum(hg[:, sl], axis=0, keepdims=True), axis=1, keepdims=True)
                      + jnp.sum(xb, axis=0, keepdims=True))
            dtot = dtot + _put_lane(j, dtot_j)
            dxss.append(dxg * dtj + dv[:, sl] * dyj)
            dyes.append(dyj * ec)
            xgds.append(xg * decay)
            etots.append(etot)
        da = lax.dot_general(maskf, dacum, _DIMS["tn"], precision=HIGHEST, preferred_element_type=F32) + dtot
        ddt = da * a_r + rx
        draw = ddt * _sigmoid(raw)
        ddt_ref[...] = draw
        dbias_ref[...] += jnp.sum(draw, axis=0, keepdims=True)
        dalog_ref[...] += jnp.sum(da * dt_rows, axis=0, keepdims=True) * a_r
        dxs_ref[...] = jnp.concatenate(dxss, axis=1)
        dye = jnp.concatenate(dyes, axis=1).astype(BF16)
        xgd = jnp.concatenate(xgds, axis=1).astype(BF16)
        dcbb = dcb.astype(BF16)
        dc_ref[...] = (jnp.dot(dcbb, bm, preferred_element_type=F32)
                       + lax.dot_general(dye, hst.astype(BF16), _DIMS["nt"], preferred_element_type=F32))
        db_ref[...] = (lax.dot_general(dcbb, cm, _DIMS["tn"], preferred_element_type=F32)
                       + lax.dot_general(xgd, gst.astype(BF16), _DIMS["nt"], preferred_element_type=F32))
        g_ref[...] = lax.dot_general(cm, dye, _DIMS["tn"], preferred_element_type=F32) + gst * _lanes_per_head(etots)

    small = pl.BlockSpec((None, None, 1, HPG), lambda d, g, c: (d, g, 0, 0))
    sshape = jax.ShapeDtypeStruct((2, SSM_GROUPS, 1, HPG), F32)
    return pl.pallas_call(
        body, name=name, grid=(2, SSM_GROUPS, nc), in_specs=in_specs,
        out_specs=[pl.BlockSpec((None, T, GW), lambda d, g, c: (d, cb_of(d, c), g)),
                   pl.BlockSpec((None, T, SSM_STATE), lambda d, g, c: (d, cb_of(d, c), g)),
                   pl.BlockSpec((None, T, SSM_STATE), lambda d, g, c: (d, cb_of(d, c), g)),
                   pl.BlockSpec((None, None, T, HPG), lambda d, g, c: (d, g, cb_of(d, c), 0)), small, small],
        out_shape=[jax.ShapeDtypeStruct((2, s, SSM_INNER), F32),
                   jax.ShapeDtypeStruct((2, s, SSM_GROUPS * SSM_STATE), F32),
                   jax.ShapeDtypeStruct((2, s, SSM_GROUPS * SSM_STATE), F32),
                   jax.ShapeDtypeStruct((2, SSM_GROUPS, s, HPG), F32), sshape, sshape],
        scratch_shapes=[pltpu.VMEM((SSM_STATE, GW), F32)],
        compiler_params=_params("parallel", "parallel", "arbitrary"),
    )(xbc, xbc, xbc, dtr, dtt, pr_rows, pr_cols, dvec, hs, dy)


PAIRS = HPG // 2


def _scan_lanes(x, forward):
    lane = lax.broadcasted_iota(jnp.int32, x.shape, 1)
    p = x
    k = 1
    while k < T:
        p = p + jnp.where(lane >= k, pltpu.roll(p, k, 1), 0.0)
        k *= 2
    tot = p[:, T - 1:T]
    return jnp.where(forward, p, tot - p + x), tot


def _ssd_chunk(d, dt_ref, dtt_ref, prr_ref, prc_ref):
    sgn = 1 - 2 * d
    ri = lax.broadcasted_iota(jnp.int32, (T, T), 0)
    ci = lax.broadcasted_iota(jnp.int32, (T, T), 1)
    mask = ((ri - ci) * sgn) >= 0
    mask_t = ((ci - ri) * sgn) >= 0
    bias_r = prr_ref[0:1, :]
    bias_c, a_c = prc_ref[:, 0:1], prc_ref[:, 1:2]
    dt_rows = _softplus(dt_ref[...] + bias_r)
    raw_lanes = dtt_ref[...] + bias_c
    dt_lanes = _softplus(raw_lanes)
    acum_lanes, tot = _scan_lanes(dt_lanes * a_c, d == 0)
    return dict(mask=mask, mask_t=mask_t, head0=ci < SSM_HEAD_DIM, dt_rows=dt_rows, raw_lanes=raw_lanes,
                dt_lanes=dt_lanes, a_c=a_c, acum_lanes=acum_lanes, acum_rows=acum_lanes.T, tot=tot)


def _ssd_pair(ck, q):
    h0 = ck["head0"]
    colb = lambda rows, j: jnp.broadcast_to(rows[:, j:j + 1], (T, T))
    rowb = lambda lanes, j: jnp.broadcast_to(lanes[j:j + 1, :], (T, T))
    lms, lmts, acs = [], [], []
    for j in (2 * q, 2 * q + 1):
        ac, al = colb(ck["acum_rows"], j), rowb(ck["acum_lanes"], j)
        lms.append(jnp.where(ck["mask"], jnp.exp(jnp.minimum(ac - al, 0.0)), 0.0))
        lmts.append(jnp.where(ck["mask_t"], jnp.exp(jnp.minimum(al - ac, 0.0)), 0.0))
        acs.append(ac)
    ac_pair = jnp.where(h0, acs[0], acs[1])
    dt_pair = jnp.where(h0, colb(ck["dt_rows"], 2 * q), colb(ck["dt_rows"], 2 * q + 1))
    tot_pair = jnp.where(h0[0:1], ck["tot"][2 * q:2 * q + 1, :], ck["tot"][2 * q + 1:2 * q + 2, :])
    return dict(lm=lms, lmt=lmts, dt=dt_pair, ec=jnp.exp(ac_pair), decay=jnp.exp(tot_pair - ac_pair),
                etot=jnp.exp(tot_pair))


def _split_heads(h0, v):
    zero = jnp.zeros_like(v)
    return jnp.where(h0, v, zero), jnp.where(h0, zero, v)


GPS = 2
GSTEPS = SSM_GROUPS // GPS
B_BLOCK0 = SSM_INNER // (GPS * SSM_STATE)
C_BLOCK0 = (SSM_INNER + SSM_GROUPS * SSM_STATE) // (GPS * SSM_STATE)


def _ssd_in_specs(chunk):
    return [
        pl.BlockSpec((T, GPS * GW), lambda d, g, c: (chunk(d, c), g)),
        pl.BlockSpec((T, GPS * SSM_STATE), lambda d, g, c: (chunk(d, c), B_BLOCK0 + g)),
        pl.BlockSpec((T, GPS * SSM_STATE), lambda d, g, c: (chunk(d, c), C_BLOCK0 + g)),
        pl.BlockSpec((GPS * SSM_STATE, T), lambda d, g, c: (g, chunk(d, c))),
        pl.BlockSpec((GPS * SSM_STATE, T), lambda d, g, c: (g, chunk(d, c))),
        pl.BlockSpec((None, GPS, T, HPG), lambda d, g, c: (d, g, chunk(d, c), 0)),
        pl.BlockSpec((None, GPS, HPG, T), lambda d, g, c: (d, g, 0, chunk(d, c))),
        pl.BlockSpec((None, GPS, 2, HPG), lambda d, g, c: (d, g, 0, 0)),
        pl.BlockSpec((None, GPS, HPG, 2), lambda d, g, c: (d, g, 0, 0)),
    ]


def _group_refs(gi, wide, state_wide, t_wide, lead):
    return ([r.at[:, pl.ds(gi * GW, GW)] for r in wide] + [r.at[:, pl.ds(gi * SSM_STATE, SSM_STATE)] for r in state_wide]
            + [r.at[pl.ds(gi * SSM_STATE, SSM_STATE), :] for r in t_wide] + [r.at[gi] for r in lead])


def _ssd_fwd(xbc, bt, ct, dtr, dtt, pr_rows, pr_cols, *, name):
    s = xbc.shape[0]
    nc = s // T
    chunk = lambda d, c: jnp.where(d == 0, c, nc - 1 - c)

    def body(x_ref, b_ref, c_ref, bt_ref, ct_ref, dt_ref, dtt_ref, prr_ref, prc_ref, y_ref, hs_ref, st_ref):
        @pl.when(pl.program_id(2) == 0)
        def _():
            st_ref[...] = jnp.zeros_like(st_ref)

        for gi in range(GPS):
            group(*_group_refs(gi, [x_ref, y_ref], [b_ref, c_ref], [bt_ref, ct_ref],
                               [dt_ref, dtt_ref, prr_ref, prc_ref, hs_ref, st_ref]))

    def group(x_ref, y_ref, b_ref, c_ref, bt_ref, ct_ref, dt_ref, dtt_ref, prr_ref, prc_ref, hs_ref, st_ref):
        d = pl.program_id(0)
        ck = _ssd_chunk(d, dt_ref, dtt_ref, prr_ref, prc_ref)
        xs = x_ref[...].astype(F32)
        cm = c_ref[...]
        hprev = st_ref[...]
        hs_ref[...] = hprev
        cb = lax.dot_general(cm, b_ref[...], _DIMS["nt"], preferred_element_type=F32)
        ch = jnp.dot(cm, hprev.astype(BF16), preferred_element_type=F32)
        ys, xgds, etots = [], [], []
        for q in range(PAIRS):
            sl = slice(q * 128, (q + 1) * 128)
            pr = _ssd_pair(ck, q)
            xg = xs[:, sl] * pr["dt"]
            xg0, xg1 = _split_heads(ck["head0"], xg.astype(BF16))
            yd = (jnp.dot((cb * pr["lm"][0]).astype(BF16), xg0, preferred_element_type=F32)
                  + jnp.dot((cb * pr["lm"][1]).astype(BF16), xg1, preferred_element_type=F32))
            ys.append(yd + pr["ec"] * ch[:, sl])
            xgds.append(xg * pr["decay"])
            etots.append(pr["etot"])
        y_ref[...] = jnp.concatenate(ys, axis=1)
        new = jnp.dot(bt_ref[...], jnp.concatenate(xgds, axis=1).astype(BF16), preferred_element_type=F32)
        st_ref[...] = hprev * jnp.concatenate(etots, axis=1) + new

    return pl.pallas_call(
        body, name=name, grid=(2, GSTEPS, nc), in_specs=_ssd_in_specs(chunk),
        out_specs=[pl.BlockSpec((None, T, GPS * GW), lambda d, g, c: (d, chunk(d, c), g)),
                   pl.BlockSpec((None, None, GPS, SSM_STATE, GW), lambda d, g, c: (d, chunk(d, c), g, 0, 0))],
        out_shape=[jax.ShapeDtypeStruct((2, s, SSM_INNER), F32),
                   jax.ShapeDtypeStruct((2, nc, SSM_GROUPS, SSM_STATE, GW), F32)],
        scratch_shapes=[pltpu.VMEM((GPS, SSM_STATE, GW), F32)],
        compiler_params=_params("parallel", "parallel", "arbitrary"),
    )(xbc, xbc, xbc, bt, ct, dtr, dtt, pr_rows, pr_cols)


def _ssd_bwd(xbc, bt, ct, dtr, dtt, pr_rows, pr_cols, dvec, hs, y2, dy, *, name):
    s = xbc.shape[0]
    nc = s // T
    chunk = lambda d, c: jnp.where(d == 0, nc - 1 - c, c)
    in_specs = _ssd_in_specs(chunk) + [
        pl.BlockSpec((1, GPS * GW), lambda d, g, c: (0, g)),
        pl.BlockSpec((None, None, GPS, SSM_STATE, GW), lambda d, g, c: (d, chunk(d, c), g, 0, 0)),
        pl.BlockSpec((None, T, GPS * GW), lambda d, g, c: (d, chunk(d, c), g)),
        pl.BlockSpec((T, GPS * GW), lambda d, g, c: (chunk(d, c), g)),
    ]

    def body(x_ref, b_ref, c_ref, bt_ref, ct_ref, dt_ref, dtt_ref, prr_ref, prc_ref, dvec_ref, hs_ref, y_ref, dy_ref,
             dxs_ref, db_ref, dc_ref, ddt_ref, dalog_ref, dbias_ref, g_ref):
        @pl.when(pl.program_id(2) == 0)
        def _():
            g_ref[...] = jnp.zeros_like(g_ref)
            dalog_ref[...] = jnp.zeros_like(dalog_ref)
            dbias_ref[...] = jnp.zeros_like(dbias_ref)

        for gi in range(GPS):
            group(*_group_refs(gi, [x_ref, dvec_ref, y_ref, dy_ref, dxs_ref], [b_ref, c_ref, db_ref, dc_ref], [bt_ref, ct_ref],
                               [dt_ref, dtt_ref, prr_ref, prc_ref, hs_ref, ddt_ref, dalog_ref, dbias_ref, g_ref]))

    def group(x_ref, dvec_ref, y_ref, dy_ref, dxs_ref, b_ref, c_ref, db_ref, dc_ref, bt_ref, ct_ref,
              dt_ref, dtt_ref, prr_ref, prc_ref, hs_ref, ddt_ref, dalog_ref, dbias_ref, g_ref):
        d = pl.program_id(0)
        ck = _ssd_chunk(d, dt_ref, dtt_ref, prr_ref, prc_ref)
        h0 = ck["head0"]
        xs = x_ref[...].astype(F32)
        bm, cm = b_ref[...], c_ref[...]
        hst = hs_ref[...]
        gst = g_ref[...]
        dyv = dy_ref[...]
        yv = y_ref[...]
        dv = dvec_ref[...] * (1 - d).astype(F32)
        cb = lax.dot_general(cm, bm, _DIMS["nt"], preferred_element_type=F32)
        cbt = jnp.dot(bm, ct_ref[...], preferred_element_type=F32)
        ch = jnp.dot(cm, hst.astype(BF16), preferred_element_type=F32)
        bg = jnp.dot(bm, gst.astype(BF16), preferred_element_type=F32)
        hg_cols = jnp.sum(hst * gst, axis=0, keepdims=True)
        lane16 = lax.broadcasted_iota(jnp.int32, (T, 2 * HPG), 1)
        sub8 = lax.broadcasted_iota(jnp.int32, (HPG, 1), 0)
        dcb = jnp.zeros((T, T), F32)
        acc16 = jnp.zeros((T, 2 * HPG), F32)
        dtot = jnp.zeros((HPG, 1), F32)
        dxss, dyes, xgds, etots = [], [], [], []
        for q in range(PAIRS):
            sl = slice(q * 128, (q + 1) * 128)
            pr = _ssd_pair(ck, q)
            xsp, dyp = xs[:, sl], dyv[:, sl]
            xg = xsp * pr["dt"]
            xgb = xg.astype(BF16)
            dyb = dyp.astype(BF16)
            dy0, dy1 = _split_heads(h0, dyb)
            dcb = dcb + (lax.dot_general(dy0, xgb, _DIMS["nt"], preferred_element_type=F32) * pr["lm"][0]
                         + lax.dot_general(dy1, xgb, _DIMS["nt"], preferred_element_type=F32) * pr["lm"][1])
            dxg_in = (jnp.dot((cbt * pr["lmt"][0]).astype(BF16), dy0, preferred_element_type=F32)
                      + jnp.dot((cbt * pr["lmt"][1]).astype(BF16), dy1, preferred_element_type=F32))
            xgd = xg * pr["decay"]
            xb = xgd * bg[:, sl]
            dxg = dxg_in + pr["decay"] * bg[:, sl]
            yo = pr["ec"] * ch[:, sl]
            dac = dyb.astype(F32) * (yv[:, sl] - yo) + dyp * yo - xgb.astype(F32) * dxg_in - xb
            d_0, d_1 = _split_heads(h0, dac)
            r_0, r_1 = _split_heads(h0, dxg * xsp)
            for hh, (d_h, r_h) in enumerate(((d_0, r_0), (d_1, r_1))):
                j = 2 * q + hh
                acc16 = (acc16 + jnp.where(lane16 == j, jnp.sum(d_h, axis=-1, keepdims=True), 0.0)
                         + jnp.where(lane16 == HPG + j, jnp.sum(r_h, axis=-1, keepdims=True), 0.0))
            tcols = pr["etot"] * hg_cols[:, sl] + jnp.sum(xb, axis=0, keepdims=True)
            t0, t1 = _split_heads(h0[0:1], tcols)
            dtot = (dtot + jnp.where(sub8 == 2 * q, jnp.sum(t0, axis=-1, keepdims=True), 0.0)
                    + jnp.where(sub8 == 2 * q + 1, jnp.sum(t1, axis=-1, keepdims=True), 0.0))
            dxss.append(dxg * pr["dt"] + dv[:, sl] * dyp)
            dyes.append(dyp * pr["ec"])
            xgds.append(xgd)
            etots.append(pr["etot"])
        acc_t = acc16.T
        da_lanes = _scan_lanes(acc_t[0:HPG], d != 0)[0] + dtot
        ddt = da_lanes * ck["a_c"] + acc_t[HPG:2 * HPG]
        draw = ddt * _sigmoid(ck["raw_lanes"])
        ddt_ref[...] = draw
        dbias_ref[...] += jnp.sum(draw, axis=-1, keepdims=True)
        dalog_ref[...] += jnp.sum(da_lanes * ck["dt_lanes"], axis=-1, keepdims=True) * ck["a_c"]
        dxs_ref[...] = jnp.concatenate(dxss, axis=1)
        dye = jnp.concatenate(dyes, axis=1).astype(BF16)
        xgd_all = jnp.concatenate(xgds, axis=1).astype(BF16)
        dcbb = dcb.astype(BF16)
        dc_ref[...] = (jnp.dot(dcbb, bm, preferred_element_type=F32)
                       + lax.dot_general(dye, hst.astype(BF16), _DIMS["nt"], preferred_element_type=F32))
        db_ref[...] = (lax.dot_general(dcbb, cm, _DIMS["tn"], preferred_element_type=F32)
                       + lax.dot_general(xgd_all, gst.astype(BF16), _DIMS["nt"], preferred_element_type=F32))
        g_ref[...] = jnp.dot(ct_ref[...], dye, preferred_element_type=F32) + gst * jnp.concatenate(etots, axis=1)

    small = pl.BlockSpec((None, GPS, HPG, 1), lambda d, g, c: (d, g, 0, 0))
    sshape = jax.ShapeDtypeStruct((2, SSM_GROUPS, HPG, 1), F32)
    return pl.pallas_call(
        body, name=name, grid=(2, GSTEPS, nc), in_specs=in_specs,
        out_specs=[pl.BlockSpec((None, T, GPS * GW), lambda d, g, c: (d, chunk(d, c), g)),
                   pl.BlockSpec((None, T, GPS * SSM_STATE), lambda d, g, c: (d, chunk(d, c), g)),
                   pl.BlockSpec((None, T, GPS * SSM_STATE), lambda d, g, c: (d, chunk(d, c), g)),
                   pl.BlockSpec((None, GPS, HPG, T), lambda d, g, c: (d, g, 0, chunk(d, c))), small, small],
        out_shape=[jax.ShapeDtypeStruct((2, s, SSM_INNER), F32),
                   jax.ShapeDtypeStruct((2, s, SSM_GROUPS * SSM_STATE), F32),
                   jax.ShapeDtypeStruct((2, s, SSM_GROUPS * SSM_STATE), F32),
                   jax.ShapeDtypeStruct((2, SSM_GROUPS, HPG, s), F32), sshape, sshape],
        scratch_shapes=[pltpu.VMEM((GPS, SSM_STATE, GW), F32)],
        compiler_params=_params("parallel", "parallel", "arbitrary"),
    )(xbc, xbc, xbc, bt, ct, dtr, dtt, pr_rows, pr_cols, dvec, hs, y2, dy)


def _gate_norm_fwd(y2, xbc, proj, dvec, nw, *, name):
    s = xbc.shape[0]
    tr = 256

    def body(y_ref, xs_ref, z_ref, dv_ref, w_ref, u_ref):
        yt = y_ref[0] + y_ref[1] + dv_ref[...] * xs_ref[...].astype(F32)
        yg = yt * _silu_and_grad(z_ref[...].astype(F32))[0]
        u_ref[...] = (yg * lax.rsqrt(jnp.mean(yg * yg, axis=-1, keepdims=True) + RMS_EPS) * w_ref[...]).astype(BF16)

    row = pl.BlockSpec((tr, SSM_INNER), lambda i: (i, 0))
    vec = pl.BlockSpec((1, SSM_INNER), lambda i: (0, 0))
    return pl.pallas_call(
        body, name=name, grid=(s // tr,),
        in_specs=[pl.BlockSpec((2, tr, SSM_INNER), lambda i: (0, i, 0)), row, row, vec, vec], out_specs=row,
        out_shape=jax.ShapeDtypeStruct((s, SSM_INNER), BF16), compiler_params=_params("parallel"),
    )(y2, xbc, proj, dvec, nw)


def _gate_norm_bwd(du, y2, xbc, proj, dvec, nw, *, name):
    s = xbc.shape[0]
    tr = 256

    def body(du_ref, y_ref, xs_ref, z_ref, dv_ref, w_ref, dy_ref, dz_ref, dw_ref, dd_ref):
        @pl.when(pl.program_id(0) == 0)
        def _():
            dw_ref[...] = jnp.zeros_like(dw_ref)
            dd_ref[...] = jnp.zeros_like(dd_ref)

        xs = xs_ref[...].astype(F32)
        yt = y_ref[0] + y_ref[1] + dv_ref[...] * xs
        si, dsi = _silu_and_grad(z_ref[...].astype(F32))
        yg = yt * si
        rstd = lax.rsqrt(jnp.mean(yg * yg, axis=-1, keepdims=True) + RMS_EPS)
        yhat = yg * rstd
        du = du_ref[...]
        dyn = du * w_ref[...]
        dyg = rstd * (dyn - yhat * jnp.mean(dyn * yhat, axis=-1, keepdims=True))
        dyt = dyg * si
        dy_ref[...] = dyt
        dz_ref[...] = (dyg * yt * dsi).astype(BF16)
        dw_ref[...] += jnp.sum(du * yhat, axis=0, keepdims=True)
        dd_ref[...] += jnp.sum(dyt * xs, axis=0, keepdims=True)

    row = pl.BlockSpec((tr, SSM_INNER), lambda i: (i, 0))
    vec = pl.BlockSpec((1, SSM_INNER), lambda i: (0, 0))
    vshape = jax.ShapeDtypeStruct((1, SSM_INNER), F32)
    return pl.pallas_call(
        body, name=name, grid=(s // tr,),
        in_specs=[row, pl.BlockSpec((2, tr, SSM_INNER), lambda i: (0, i, 0)), row, row, vec, vec],
        out_specs=[row, row, vec, vec],
        out_shape=[jax.ShapeDtypeStruct((s, SSM_INNER), F32), jax.ShapeDtypeStruct((s, SSM_INNER), BF16), vshape, vshape],
        compiler_params=_params("arbitrary"),
    )(du, y2, xbc, proj, dvec, nw)


def _layer_a_fwd(x, mod, w_in, w_out, ln_g, ln_b, tag):
    shift, scale, gate = mod
    h = _modulate(x, scale, shift, name=f"{tag}_modulate")
    proj = _mm(h, w_in, mode="nn", out_dtype=BF16, tm=512, tn=1024, tk=1024, name=f"{tag}_mm_in")
    os_, lses = [], []
    for grp in range(3):
        o, l = _attn_fwd(proj, grp, name=f"{tag}_attn_fwd{grp}")
        os_.append(o)
        lses.append(l)
    y, o, lse = _attn_combine(os_, lses, proj, name=f"{tag}_combine")
    out = _mm(y, w_out, mode="nn", out_dtype=F32, tm=512, tn=1024, tk=1024, name=f"{tag}_mm_out")
    xn = _resid_ln_fwd(x, out, gate, ln_g, ln_b, name=f"{tag}_resid_ln")
    return xn, (x, h, proj, y, o, lse, out)


def _layer_a_bwd(dxn, saved, mod, w_in, w_out, ln_g, tag):
    x, h, proj, y, o, lse, out = saved
    shift, scale, gate = mod
    dx_part, dout, dgate, dln_g, dln_b = _resid_ln_bwd(x, out, gate, ln_g, dxn, name=f"{tag}_resid_ln_bwd")
    dw_out = _mm(y, dout, mode="tn", out_dtype=F32, tm=1024, tn=1024, tk=512, name=f"{tag}_mm_dw_out")
    dy = _mm(dout, w_out, mode="nt", out_dtype=F32, tm=512, tn=1024, tk=1024, name=f"{tag}_mm_dy")
    do, dgp = _attn_combine_bwd(dy, o, proj, name=f"{tag}_combine_bwd")
    parts = []
    for grp in range(3):
        parts.extend(_attn_bwd(proj, grp, do, o, lse, name=f"{tag}_attn_bwd{grp}"))
    dproj = jnp.concatenate(parts + [dgp], axis=1)
    dw_in = _mm(h, dproj, mode="tn", out_dtype=F32, tm=1024, tn=1024, tk=512, name=f"{tag}_mm_dw_in")
    dx, dscale, dshift = _mm_dh(dproj, w_in, dx_part, x, scale, tm=512, tk=2048, name=f"{tag}_mm_dh")
    grads = dict(w_in=dw_in, w_out=dw_out, ln_g=dln_g, ln_b=dln_b, mod=jnp.concatenate([dshift, dscale, dgate], axis=1))
    return dx, grads


def _ssd_param_views(dt_raw, dt_bias, a_log):
    s = dt_raw.shape[0]
    r4 = dt_raw.reshape(s, 2, SSM_GROUPS, HPG)
    dtr = r4.transpose(1, 2, 0, 3)
    dtt = r4.transpose(1, 2, 3, 0)
    a = -jnp.exp(a_log)
    pr_rows = jnp.stack([dt_bias.reshape(2, SSM_GROUPS, HPG), a.reshape(2, SSM_GROUPS, HPG)], axis=2)
    return dtr, dtt, pr_rows, pr_rows.transpose(0, 1, 3, 2)


def _layer_b_fwd(x, mod, p, ln_g, ln_b, tag):
    shift, scale, gate = mod
    s = x.shape[0]
    h = _modulate(x, scale, shift, name=f"{tag}_modulate")
    proj = _mm(h, p["w_in"][:, :SSM_MAIN_COLS], mode="nn", out_dtype=BF16, tm=512, tn=1024, tk=1024, name=f"{tag}_mm_in")
    dt_raw = _mm(h, p["w_in"][:, SSM_MAIN_COLS:SSM_IN_COLS], mode="nn", out_dtype=F32, tm=512, tn=64, tk=1024,
                 name=f"{tag}_mm_dt")
    xpad = jnp.pad(proj[:, SSM_INNER:], ((CONV_HALO, CONV_HALO), (0, 0)))
    xbc = _conv_fwd(xpad, p["conv_w"], p["conv_b"], name=f"{tag}_conv")
    views = (xbc[:, SSM_INNER:SSM_INNER + SSM_GROUPS * SSM_STATE].T, xbc[:, SSM_INNER + SSM_GROUPS * SSM_STATE:].T,
             *_ssd_param_views(dt_raw, p["dt_bias"], p["a_log"]))
    y2, hs = _ssd_fwd(xbc, *views, name=f"{tag}_ssd_fwd")
    u = _gate_norm_fwd(y2, xbc, proj, p["dvec"], p["norm_w"], name=f"{tag}_gate_norm")
    out = _mm(u, p["w_out"], mode="nn", out_dtype=F32, tm=512, tn=1024, tk=2048, name=f"{tag}_mm_out")
    xn = _resid_ln_fwd(x, out, gate, ln_g, ln_b, name=f"{tag}_resid_ln")
    return xn, (x, h, proj, xpad, xbc, views, y2, hs, u, out)


def _layer_b_bwd(dxn, saved, mod, p, ln_g, tag):
    x, h, proj, xpad, xbc, views, y2, hs, u, out = saved
    shift, scale, gate = mod
    s = x.shape[0]
    dx_part, dout, dgate, dln_g, dln_b = _resid_ln_bwd(x, out, gate, ln_g, dxn, name=f"{tag}_resid_ln_bwd")
    dw_out = _mm(u, dout, mode="tn", out_dtype=F32, tm=1024, tn=1024, tk=512, name=f"{tag}_mm_dw_out")
    du = _mm(dout, p["w_out"], mode="nt", out_dtype=F32, tm=512, tn=1024, tk=1024, name=f"{tag}_mm_du")
    dy, dz, dnorm_w, dd_lanes = _gate_norm_bwd(du, y2, xbc, proj, p["dvec"], p["norm_w"], name=f"{tag}_gate_norm_bwd")
    dxs2, db2, dc2, ddt4, dalog, dbias = _ssd_bwd(xbc, *views, p["dvec"], hs, y2, dy, name=f"{tag}_ssd_bwd")
    dact = jnp.concatenate([dxs2[0] + dxs2[1], db2[0] + db2[1], dc2[0] + dc2[1]], axis=1)
    dapad = jnp.pad(dact, ((CONV_HALO, CONV_HALO), (0, 0)))
    dxbc, dconv_w, dconv_b = _conv_bwd(xpad, dapad, p["conv_w"], p["conv_b"], name=f"{tag}_conv_bwd")
    ddt_raw = ddt4.transpose(3, 0, 1, 2).reshape(s, 2 * SSM_HEADS).astype(BF16)
    dproj = jnp.concatenate([dz, dxbc, ddt_raw, jnp.zeros((s, SSM_PAD_COLS - SSM_IN_COLS), BF16)], axis=1)
    dw_in = _mm(h, dproj, mode="tn", out_dtype=F32, tm=1024, tn=896, tk=512, name=f"{tag}_mm_dw_in")[:, :SSM_IN_COLS]
    w_pad = jnp.pad(p["w_in"], ((0, 0), (0, SSM_PAD_COLS - SSM_IN_COLS)))
    dx, dscale, dshift = _mm_dh(dproj, w_pad, dx_part, x, scale, tm=512, tk=1792, name=f"{tag}_mm_dh")
    grads = dict(
        w_in=dw_in, w_out=dw_out, ln_g=dln_g, ln_b=dln_b, mod=jnp.concatenate([dshift, dscale, dgate], axis=1),
        conv_w=dconv_w, conv_b=dconv_b, norm_w=dnorm_w, dt_bias=dbias.reshape(2, SSM_HEADS),
        a_log=dalog.reshape(2, SSM_HEADS), d=jnp.sum(dd_lanes.reshape(SSM_HEADS, SSM_HEAD_DIM), axis=1))
    return dx, grads


def _local_step(x, target, mods, ln_g, ln_b, a_w_in, a_w_out, b_params):
    saved = []
    for i in range(DEPTH):
        j = i // 2
        g, b = ln_g[i:i + 1], ln_b[i:i + 1]
        if i % 2 == 0:
            x, sv = _layer_a_fwd(x, mods[i], a_w_in[j], a_w_out[j], g, b, f"l{i}")
        else:
            x, sv = _layer_b_fwd(x, mods[i], b_params[j], g, b, f"l{i}")
        saved.append(sv)
    dx, loss = _loss_and_grad(x, target, name="loss")
    grads = [None] * DEPTH
    for i in reversed(range(DEPTH)):
        j = i // 2
        g = ln_g[i:i + 1]
        if i % 2 == 0:
            dx, grads[i] = _layer_a_bwd(dx, saved[i], mods[i], a_w_in[j], a_w_out[j], g, f"l{i}")
        else:
            dx, grads[i] = _layer_b_bwd(dx, saved[i], mods[i], b_params[j], g, f"l{i}")
    return loss, dx, grads


def _mesh_pos():
    return lax.axis_index("x"), lax.axis_index("y"), lax.axis_index("c")


def _all_gather(x, *, name):
    def body(x_ref, out_ref, send_sems, recv_sems, local_sem):
        ax, ay, ac = _mesh_pos()
        me, sibling = (ax, ay, ac), (ax, ay, 1 - ac)
        chips = [(1 - ax, ay), (ax, 1 - ay), (1 - ax, 1 - ay)]

        def slot(px, py, pc):
            return out_ref.at[4 * px + 2 * py + pc]

        def copy(k, block, to, src=None):
            return pltpu.make_async_remote_copy(
                src_ref=slot(*block) if src is None else src, dst_ref=slot(*block),
                send_sem=send_sems.at[k], recv_sem=recv_sems.at[k], device_id=to, device_id_type=MESH)

        mine = pltpu.make_async_copy(x_ref, slot(*me), local_sem)
        mine.start()
        first = [copy(0, me, sibling, src=x_ref)]
        first += [copy(1 + j, me, (*chip, ac), src=x_ref) for j, chip in enumerate(chips)]
        for cp in first:
            cp.start()
        passed = [copy(4 + j, (*chip, ac), sibling) for j, chip in enumerate(chips)]
        for j, chip in enumerate(chips):
            copy(1 + j, (*chip, ac), me).wait_recv()
            passed[j].start()
        copy(0, sibling, me).wait_recv()
        for j, chip in enumerate(chips):
            copy(4 + j, (*chip, 1 - ac), me).wait_recv()
        for cp in first + passed:
            cp.wait_send()
        mine.wait()

    return pl.pallas_call(
        body, name=name, out_shape=jax.ShapeDtypeStruct((N_DEV,) + x.shape, x.dtype),
        in_specs=[pl.BlockSpec(memory_space=pl.ANY)], out_specs=pl.BlockSpec(memory_space=pl.ANY),
        scratch_shapes=[pltpu.SemaphoreType.DMA((7,)), pltpu.SemaphoreType.DMA((7,)), pltpu.SemaphoreType.DMA],
    )(x)


def _all_to_all(x, *, name):
    def body(x_ref, out_ref, send_sems, recv_sems, local_sem):
        ax, ay, ac = _mesh_pos()
        me = 4 * ax + 2 * ay + ac
        mine = pltpu.make_async_copy(x_ref.at[me], out_ref.at[me], local_sem)
        mine.start()
        copies = []
        for k in range(1, N_DEV):
            px = 1 - ax if k & 4 else ax
            py = 1 - ay if k & 2 else ay
            pc = 1 - ac if k & 1 else ac
            copies.append(pltpu.make_async_remote_copy(
                src_ref=x_ref.at[4 * px + 2 * py + pc], dst_ref=out_ref.at[me],
                send_sem=send_sems.at[k - 1], recv_sem=recv_sems.at[k - 1], device_id=(px, py, pc), device_id_type=MESH))
        for cp in copies:
            cp.start()
        for cp in copies:
            cp.wait()
        mine.wait()

    return pl.pallas_call(
        body, name=name, out_shape=jax.ShapeDtypeStruct(x.shape, x.dtype),
        in_specs=[pl.BlockSpec(memory_space=pl.ANY)], out_specs=pl.BlockSpec(memory_space=pl.ANY),
        scratch_shapes=[pltpu.SemaphoreType.DMA((7,)), pltpu.SemaphoreType.DMA((7,)), pltpu.SemaphoreType.DMA],
    )(x)


ADA_LOCAL = 3 * D_MODEL // N_DEV


def _ada_mod(c_all, ada_w, ada_b_local, *, name):
    def body(c_ref, w_ref, b_ref, o_ref):
        cond = _silu_and_grad(c_ref[...])[0]
        o_ref[...] = jnp.dot(cond, w_ref[...], precision=HIGHEST, preferred_element_type=F32) + b_ref[...]

    return pl.pallas_call(
        body, name=name, grid=(DEPTH,),
        in_specs=[pl.BlockSpec((N_DEV, D_MODEL), lambda i: (0, 0)), pl.BlockSpec((None, D_MODEL, ADA_LOCAL), lambda i: (i, 0, 0)),
                  pl.BlockSpec((None, 1, ADA_LOCAL), lambda i: (i, 0, 0))],
        out_specs=pl.BlockSpec((None, N_DEV, ADA_LOCAL), lambda i: (i, 0, 0)),
        out_shape=jax.ShapeDtypeStruct((DEPTH, N_DEV, ADA_LOCAL), F32), compiler_params=_params("parallel"),
    )(c_all, ada_w, ada_b_local)


def _ada_grad(c_all_t, dmod_local, *, name):
    def body(ct_ref, dm_ref, o_ref):
        cond_t = _silu_and_grad(ct_ref[...])[0]
        dm = dm_ref[...]
        acc = cond_t[:, 0:1] * dm[0:1, :]
        for smp in range(1, N_DEV):
            acc = acc + cond_t[:, smp:smp + 1] * dm[smp:smp + 1, :]
        o_ref[...] = acc

    return pl.pallas_call(
        body, name=name, grid=(DEPTH,),
        in_specs=[pl.BlockSpec((D_MODEL, N_DEV), lambda i: (0, 0)), pl.BlockSpec((None, N_DEV, ADA_LOCAL), lambda i: (i, 0, 0))],
        out_specs=pl.BlockSpec((None, D_MODEL, ADA_LOCAL), lambda i: (i, 0, 0)),
        out_shape=jax.ShapeDtypeStruct((DEPTH, D_MODEL, ADA_LOCAL), F32), compiler_params=_params("parallel"),
    )(c_all_t, dmod_local)


def _sum_devices(parts, *, name):
    n = parts.shape[1]

    def body(p_ref, o_ref):
        acc = p_ref[0:1, :]
        for dev in range(1, N_DEV):
            acc = acc + p_ref[dev:dev + 1, :]
        o_ref[...] = acc

    return pl.pallas_call(
        body, name=name, out_shape=jax.ShapeDtypeStruct((1, n), F32),
        in_specs=[pl.BlockSpec(memory_space=pltpu.VMEM)], out_specs=pl.BlockSpec(memory_space=pltpu.VMEM),
        compiler_params=pltpu.CompilerParams(vmem_limit_bytes=VMEM_LIMIT_BYTES),
    )(parts)


ADAMW_VMEM_BYTES = 24 * 1024 * 1024


def _adamw(w, m, v, g, *, name):
    r, c = w.shape
    summed = g.ndim == 3
    tr = r
    arrays = 7 + (N_DEV if summed else 1)
    while tr % 16 == 0 and 2 * arrays * tr * c * 4 > ADAMW_VMEM_BYTES:
        tr //= 2

    def body(w_ref, m_ref, v_ref, g_ref, go_ref, d_ref, mo_ref, vo_ref):
        if summed:
            g = g_ref[0].astype(F32)
            for dev in range(1, N_DEV):
                g = g + g_ref[dev].astype(F32)
        else:
            g = g_ref[...]
        mn = ADAM_B1 * m_ref[...] + (1.0 - ADAM_B1) * g
        vn = ADAM_B2 * v_ref[...] + (1.0 - ADAM_B2) * (g * g)
        m_hat = mn / (1.0 - ADAM_B1 ** ADAM_STEP)
        v_hat = vn / (1.0 - ADAM_B2 ** ADAM_STEP)
        go_ref[...] = g
        d_ref[...] = -ADAM_LR * (m_hat / (jnp.sqrt(v_hat) + ADAM_EPS) + ADAM_WD * w_ref[...])
        mo_ref[...] = mn
        vo_ref[...] = vn

    row = pl.BlockSpec((tr, c), lambda i: (i, 0))
    gspec = pl.BlockSpec((N_DEV, tr, c), lambda i: (0, i, 0)) if summed else row
    shp = jax.ShapeDtypeStruct((r, c), F32)
    return pl.pallas_call(
        body, name=name, grid=(r // tr,), in_specs=[row, row, row, gspec], out_specs=[row] * 4, out_shape=[shp] * 4,
        compiler_params=_params("parallel"),
    )(w, m, v, g)


def _pack(arrays):
    flat = jnp.concatenate([a.reshape(-1) for a in arrays])
    n = flat.shape[0]
    return jnp.pad(flat, (0, -n % 128)).reshape(1, -1)


def _unpack(vec, shapes):
    out, at = [], 0
    for shp in shapes:
        n = math.prod(shp)
        out.append(vec[at:at + n].reshape(shp))
        at += n
    return out


def _unpack_rows(rows, shapes):
    out, at = [], 0
    for shp in shapes:
        n = math.prod(shp)
        out.append(rows[:, at:at + n].reshape((rows.shape[0],) + tuple(shp)))
        at += n
    return out


def _my_shard(full, me, axis):
    width = full.shape[axis] // N_DEV
    return lax.dynamic_slice_in_dim(full, me * width, width, axis)


def _gather_cols(g, lead):
    nd = g.ndim
    perm = tuple(range(1, nd - 1)) + (0, nd - 1)
    t = g.transpose(perm)
    return t.reshape(t.shape[:-2] + (t.shape[-2] * t.shape[-1],))


def kernel(x, c, ada_w, ada_b, ln_g, ln_b, a_w_in, a_w_out, b_w_in, b_conv_w, b_conv_b, b_dt_bias, b_a_log, b_d, b_norm_w, b_w_out, loss_target, m_ada_w, m_ada_b, m_ln_g, m_ln_b, m_a_w_in, m_a_w_out, m_b_w_in, m_b_conv_w, m_b_conv_b, m_b_dt_bias, m_b_a_log, m_b_d, m_b_norm_w, m_b_w_out, v_ada_w, v_ada_b, v_ln_g, v_ln_b, v_a_w_in, v_a_w_out, v_b_w_in, v_b_conv_w, v_b_conv_b, v_b_dt_bias, v_b_a_log, v_b_d, v_b_norm_w, v_b_w_out):
    ax, ay, ac = _mesh_pos()
    me = 4 * ax + 2 * ay + ac
    seq = x.shape[1]

    small_shapes = [(1, D_MODEL), (2, SSM_CONV, ADA_LOCAL), (2, ADA_LOCAL), (2, SSM_INNER // N_DEV)]
    sg = _all_gather(_pack([c, b_conv_w, b_conv_b, b_norm_w]), name="gather_small")[:, 0, :]
    c_all, conv_w_g, conv_b_g, norm_w_g = _unpack_rows(sg, small_shapes)
    c_all = c_all[:, 0, :]
    conv_w = _gather_cols(conv_w_g, 2)
    conv_b = _gather_cols(conv_b_g[:, :, None, :], 2)
    norm_w = _gather_cols(norm_w_g[:, :, None, :], 2)

    def gathered(w, name):
        lead = w.shape[0]
        flat = w.astype(BF16).reshape(lead * w.shape[1], w.shape[2])
        return _all_gather(flat, name=name).reshape(N_DEV, lead, w.shape[1], w.shape[2])

    a_w_in_f = _gather_cols(gathered(a_w_in, "gather_a_w_in"), 2)
    b_w_in_f = _gather_cols(gathered(b_w_in, "gather_b_w_in"), 2)
    a_w_out_f = gathered(a_w_out, "gather_a_w_out").transpose(1, 0, 2, 3).reshape(2, A_WIDTH, D_MODEL)
    b_w_out_f = gathered(b_w_out, "gather_b_w_out").transpose(1, 0, 2, 3).reshape(2, SSM_INNER, D_MODEL)

    ada_b_local = _my_shard(ada_b, me, 1)[:, None, :]
    mod_cols = _ada_mod(c_all, ada_w, ada_b_local, name="ada_mod")
    mod_g = _all_gather(mod_cols.reshape(1, -1), name="gather_mod").reshape(N_DEV, DEPTH, N_DEV, ADA_LOCAL)
    mod = lax.dynamic_index_in_dim(mod_g, me, axis=2, keepdims=False).transpose(1, 0, 2).reshape(DEPTH, 3 * D_MODEL)
    mods = [tuple(mod[i:i + 1, k * D_MODEL:(k + 1) * D_MODEL] for k in range(3)) for i in range(DEPTH)]

    b_params = [dict(w_in=b_w_in_f[j], w_out=b_w_out_f[j], conv_w=conv_w[j], conv_b=conv_b[j], norm_w=norm_w[j],
                     dt_bias=b_dt_bias[j], a_log=b_a_log[j], dvec=jnp.repeat(b_d[j], SSM_HEAD_DIM)[None, :])
                for j in range(2)]
    loss_lanes, dx, grads = _local_step(x[0], loss_target[0], mods, ln_g, ln_b, a_w_in_f, a_w_out_f, b_params)
    loss = lax.psum(loss_lanes[0, 0], ("x", "y", "c"))
    grad_x = dx[None]

    a_layers, b_layers = (grads[0], grads[2]), (grads[1], grads[3])
    part_shapes = [(DEPTH, 3 * D_MODEL), (DEPTH, D_MODEL), (DEPTH, D_MODEL), (2, SSM_CONV, SSM_CONV_DIM),
                   (2, SSM_CONV_DIM), (2, SSM_INNER), (2, 2, SSM_HEADS), (2, 2, SSM_HEADS), (2, SSM_HEADS)]
    parts = _pack([
        jnp.concatenate([g["mod"] for g in grads]), jnp.concatenate([g["ln_g"] for g in grads]),
        jnp.concatenate([g["ln_b"] for g in grads]), jnp.stack([g["conv_w"] for g in b_layers]),
        jnp.stack([g["conv_b"][0] for g in b_layers]), jnp.stack([g["norm_w"][0] for g in b_layers]),
        jnp.stack([g["dt_bias"] for g in b_layers]), jnp.stack([g["a_log"] for g in b_layers]),
        jnp.stack([g["d"] for g in b_layers])])
    parts_g = _all_gather(parts, name="gather_small_grads")[:, 0, :]
    (g_ada_b, g_ln_g, g_ln_b, g_conv_w, g_conv_b, g_norm_w, g_dt_bias, g_a_log, g_d) = _unpack(
        _sum_devices(parts_g, name="sum_small_grads")[0], part_shapes)
    dmod_all = parts_g[:, :DEPTH * 3 * D_MODEL].reshape(N_DEV, DEPTH, N_DEV, ADA_LOCAL)
    dmod_local = lax.dynamic_index_in_dim(dmod_all, me, axis=2, keepdims=False).transpose(1, 0, 2)
    g_ada_w = _ada_grad(c_all.T, dmod_local, name="ada_grad")

    def scatter_cols(per_layer, name):
        st = jnp.stack(per_layer)
        lead, r, cols = st.shape
        blocks = st.reshape(lead, r, N_DEV, cols // N_DEV).transpose(2, 0, 1, 3).reshape(N_DEV, lead * r, cols // N_DEV)
        return _all_to_all(blocks.astype(BF16), name=name)

    def scatter_rows(per_layer, name):
        st = jnp.stack(per_layer)
        lead, rows, cols = st.shape
        blocks = st.reshape(lead, N_DEV, rows // N_DEV, cols).transpose(1, 0, 2, 3).reshape(N_DEV, lead * rows // N_DEV, cols)
        return _all_to_all(blocks.astype(BF16), name=name)

    r_a_w_in = scatter_cols([g["w_in"] for g in a_layers], "scatter_a_w_in")
    r_a_w_out = scatter_rows([g["w_out"] for g in a_layers], "scatter_a_w_out")
    r_b_w_in = scatter_cols([g["w_in"] for g in b_layers], "scatter_b_w_in")
    r_b_w_out = scatter_rows([g["w_out"] for g in b_layers], "scatter_b_w_out")

    def update(w, m, v, g, name):
        two_d = (-1, w.shape[-1])
        outs = _adamw(w.reshape(two_d), m.reshape(two_d), v.reshape(two_d), g, name=name)
        return [o.reshape(w.shape) for o in outs]

    up_ada_w = update(ada_w, m_ada_w, v_ada_w, g_ada_w.reshape(-1, ADA_LOCAL), "adamw_ada_w")
    up_a_w_in = update(a_w_in, m_a_w_in, v_a_w_in, r_a_w_in, "adamw_a_w_in")
    up_a_w_out = update(a_w_out, m_a_w_out, v_a_w_out, r_a_w_out, "adamw_a_w_out")
    up_b_w_in = update(b_w_in, m_b_w_in, v_b_w_in, r_b_w_in, "adamw_b_w_in")
    up_b_w_out = update(b_w_out, m_b_w_out, v_b_w_out, r_b_w_out, "adamw_b_w_out")

    small_w = [ada_b, ln_g, ln_b, b_conv_w, b_conv_b, b_dt_bias, b_a_log, b_d, b_norm_w]
    small_m = [m_ada_b, m_ln_g, m_ln_b, m_b_conv_w, m_b_conv_b, m_b_dt_bias, m_b_a_log, m_b_d, m_b_norm_w]
    small_v = [v_ada_b, v_ln_g, v_ln_b, v_b_conv_w, v_b_conv_b, v_b_dt_bias, v_b_a_log, v_b_d, v_b_norm_w]
    small_g = [g_ada_b, g_ln_g, g_ln_b, _my_shard(g_conv_w, me, 2), _my_shard(g_conv_b, me, 1), g_dt_bias, g_a_log, g_d,
               _my_shard(g_norm_w, me, 1)]
    shapes = [w.shape for w in small_w]
    packed = _adamw(_pack(small_w), _pack(small_m), _pack(small_v), _pack(small_g), name="adamw_small")
    (up_ada_b, up_ln_g, up_ln_b, up_conv_w, up_conv_b, up_dt_bias, up_a_log, up_d, up_norm_w) = zip(
        *[_unpack(p[0], shapes) for p in packed])

    ordered = [up_ada_w, up_ada_b, up_ln_g, up_ln_b, up_a_w_in, up_a_w_out, up_b_w_in, up_conv_w, up_conv_b,
               up_dt_bias, up_a_log, up_d, up_norm_w, up_b_w_out]
    return (loss, grad_x, *[u[0] for u in ordered], *[u[1] for u in ordered], *[u[2] for u in ordered],
            *[u[3] for u in ordered])
```

```python
import functools
import math

import jax
import jax.numpy as jnp
import numpy as np
from jax import lax
from jax.experimental import pallas as pl
from jax.experimental.pallas import tpu as pltpu

F32 = jnp.float32
BF16 = jnp.bfloat16
HIGHEST = lax.Precision.HIGHEST
MESH = pl.DeviceIdType.MESH

D_MODEL = 1024
DEPTH = 4
A_HEADS = 16
A_HEAD_DIM = 64
A_WIDTH = 1024
DILATIONS = (1, 4, 16)
A_RADIUS = 64
A_QBLOCK = 128
A_IN_COLS = 10240
SSM_INNER = 2048
SSM_HEADS = 32
SSM_HEAD_DIM = 64
SSM_STATE = 128
SSM_GROUPS = 4
SSM_CHUNK = 128
SSM_CONV = 5
SSM_CONV_DIM = 3072
SSM_IN_COLS = 5184
SSM_MAIN_COLS = 5120
SSM_PAD_COLS = 5376
CONV_HALO = 16
ALPHA = (2 * DEPTH) ** 0.25
LN_EPS = 1e-5
RMS_EPS = 1e-5
ADAM_LR, ADAM_B1, ADAM_B2, ADAM_EPS, ADAM_WD, ADAM_STEP = 0.001, 0.9, 0.999, 1e-08, 0.01, 10
N_DEV = 8
VMEM_LIMIT_BYTES = 56 * 1024 * 1024
NEG_BIG = -1e30


def _params(*sem):
    return pltpu.CompilerParams(dimension_semantics=sem, vmem_limit_bytes=VMEM_LIMIT_BYTES)


def _sigmoid(x):
    return 1.0 / (1.0 + jnp.exp(-x))


def _silu_and_grad(x):
    sg = _sigmoid(x)
    return x * sg, sg * (1.0 + x * (1.0 - sg))


def _softplus(x):
    e = jnp.exp(-jnp.abs(x))
    u = 1.0 + e
    log1p = jnp.where(u == 1.0, e, jnp.log(u) * (e / jnp.where(u == 1.0, 1.0, u - 1.0)))
    return jnp.maximum(x, 0.0) + log1p


_DIMS = {"nn": (((1,), (0,)), ((), ())), "nt": (((1,), (1,)), ((), ())), "tn": (((0,), (0,)), ((), ()))}


def _mm(a, b, *, mode, out_dtype, tm, tn, tk, name):
    if mode == "nn":
        (m, k), (_, n) = a.shape, b.shape
    elif mode == "nt":
        (m, k), (n, _) = a.shape, b.shape
    else:
        (k, m), (_, n) = a.shape, b.shape
    tm, tn, tk = min(tm, m), min(tn, n), min(tk, k)
    assert m % tm == 0 and n % tn == 0 and k % tk == 0, (name, a.shape, b.shape)
    nk = k // tk
    dims = _DIMS[mode]

    def body(a_ref, b_ref, o_ref, *scratch):
        part = lax.dot_general(a_ref[...], b_ref[...], dims, preferred_element_type=F32)
        if nk == 1:
            o_ref[...] = part.astype(o_ref.dtype)
            return
        acc_ref, = scratch
        kk = pl.program_id(2)

        @pl.when(kk == 0)
        def _():
            acc_ref[...] = part

        @pl.when(kk > 0)
        def _():
            acc_ref[...] += part

        @pl.when(kk == nk - 1)
        def _():
            o_ref[...] = acc_ref[...].astype(o_ref.dtype)

    if mode == "tn":
        a_spec = pl.BlockSpec((tk, tm), lambda i, j, kk: (kk, i))
    else:
        a_spec = pl.BlockSpec((tm, tk), lambda i, j, kk: (i, kk))
    if mode == "nt":
        b_spec = pl.BlockSpec((tn, tk), lambda i, j, kk: (j, kk))
    else:
        b_spec = pl.BlockSpec((tk, tn), lambda i, j, kk: (kk, j))
    return pl.pallas_call(
        body, name=name, grid=(m // tm, n // tn, nk),
        in_specs=[a_spec, b_spec], out_specs=pl.BlockSpec((tm, tn), lambda i, j, kk: (i, j)),
        out_shape=jax.ShapeDtypeStruct((m, n), out_dtype),
        scratch_shapes=[] if nk == 1 else [pltpu.VMEM((tm, tn), F32)],
        compiler_params=_params("parallel", "parallel", "arbitrary"),
    )(a, b)


def _mm_dh(dproj, w, dx_part, x, scale, *, tm, tk, name):
    s, k = dproj.shape
    d = w.shape[0]
    tk = min(tk, k)
    assert s % tm == 0 and k % tk == 0
    nk = k // tk

    def body(a_ref, w_ref, dxp_ref, x_ref, sc_ref, dx_ref, dsc_ref, dsh_ref, acc_ref):
        i, kk = pl.program_id(0), pl.program_id(1)
        part = lax.dot_general(a_ref[...], w_ref[...], _DIMS["nt"], preferred_element_type=F32)

        @pl.when(kk == 0)
        def _():
            acc_ref[...] = part

        @pl.when(kk > 0)
        def _():
            acc_ref[...] += part

        @pl.when(jnp.logical_and(i == 0, kk == 0))
        def _():
            dsc_ref[...] = jnp.zeros_like(dsc_ref)
            dsh_ref[...] = jnp.zeros_like(dsh_ref)

        @pl.when(kk == nk - 1)
        def _():
            dh = acc_ref[...]
            dx_ref[...] = dxp_ref[...] + dh * (1.0 + sc_ref[...])
            dsc_ref[...] += jnp.sum(dh * x_ref[...], axis=0, keepdims=True)
            dsh_ref[...] += jnp.sum(dh, axis=0, keepdims=True)

    row = pl.BlockSpec((tm, d), lambda i, kk: (i, 0))
    vec = pl.BlockSpec((1, d), lambda i, kk: (0, 0))
    return pl.pallas_call(
        body, name=name, grid=(s // tm, nk),
        in_specs=[pl.BlockSpec((tm, tk), lambda i, kk: (i, kk)), pl.BlockSpec((d, tk), lambda i, kk: (0, kk)),
                  row, row, vec],
        out_specs=[row, vec, vec],
        out_shape=[jax.ShapeDtypeStruct((s, d), F32), jax.ShapeDtypeStruct((1, d), F32),
                   jax.ShapeDtypeStruct((1, d), F32)],
        scratch_shapes=[pltpu.VMEM((tm, d), F32)],
        compiler_params=_params("arbitrary", "arbitrary"),
    )(dproj, w, dx_part, x, scale)


ROW_TILE = 512


def _modulate(x, scale, shift, *, name):
    s, d = x.shape

    def body(x_ref, sc_ref, sh_ref, h_ref):
        h_ref[...] = (x_ref[...] * (1.0 + sc_ref[...]) + sh_ref[...]).astype(BF16)

    row = pl.BlockSpec((ROW_TILE, d), lambda i: (i, 0))
    vec = pl.BlockSpec((1, d), lambda i: (0, 0))
    return pl.pallas_call(
        body, name=name, grid=(s // ROW_TILE,), in_specs=[row, vec, vec], out_specs=row,
        out_shape=jax.ShapeDtypeStruct((s, d), BF16), compiler_params=_params("parallel"),
    )(x, scale, shift)


def _resid_ln_fwd(x, out, gate, g, b, *, name):
    s, d = x.shape

    def body(x_ref, o_ref, gate_ref, g_ref, b_ref, y_ref):
        r = ALPHA * x_ref[...] + gate_ref[...] * o_ref[...]
        mu = jnp.mean(r, axis=-1, keepdims=True)
        rc = r - mu
        var = jnp.mean(rc * rc, axis=-1, keepdims=True)
        y_ref[...] = rc * lax.rsqrt(var + LN_EPS) * g_ref[...] + b_ref[...]

    row = pl.BlockSpec((ROW_TILE, d), lambda i: (i, 0))
    vec = pl.BlockSpec((1, d), lambda i: (0, 0))
    return pl.pallas_call(
        body, name=name, grid=(s // ROW_TILE,), in_specs=[row, row, vec, vec, vec], out_specs=row,
        out_shape=jax.ShapeDtypeStruct((s, d), F32), compiler_params=_params("parallel"),
    )(x, out, gate, g, b)


def _resid_ln_bwd(x, out, gate, g, dy, *, name):
    s, d = x.shape

    def body(x_ref, o_ref, gate_ref, g_ref, dy_ref, dxp_ref, dout_ref, dgate_ref, dg_ref, db_ref):
        @pl.when(pl.program_id(0) == 0)
        def _():
            dgate_ref[...] = jnp.zeros_like(dgate_ref)
            dg_ref[...] = jnp.zeros_like(dg_ref)
            db_ref[...] = jnp.zeros_like(db_ref)

        o = o_ref[...]
        r = ALPHA * x_ref[...] + gate_ref[...] * o
        mu = jnp.mean(r, axis=-1, keepdims=True)
        rc = r - mu
        var = jnp.mean(rc * rc, axis=-1, keepdims=True)
        rstd = lax.rsqrt(var + LN_EPS)
        xhat = rc * rstd
        dy = dy_ref[...]
        dxh = dy * g_ref[...]
        dr = rstd * (dxh - jnp.mean(dxh, axis=-1, keepdims=True) - xhat * jnp.mean(dxh * xhat, axis=-1, keepdims=True))
        dxp_ref[...] = ALPHA * dr
        dout_ref[...] = (gate_ref[...] * dr).astype(BF16)
        dgate_ref[...] += jnp.sum(dr * o, axis=0, keepdims=True)
        dg_ref[...] += jnp.sum(dy * xhat, axis=0, keepdims=True)
        db_ref[...] += jnp.sum(dy, axis=0, keepdims=True)

    row = pl.BlockSpec((ROW_TILE, d), lambda i: (i, 0))
    vec = pl.BlockSpec((1, d), lambda i: (0, 0))
    vshape = jax.ShapeDtypeStruct((1, d), F32)
    return pl.pallas_call(
        body, name=name, grid=(s // ROW_TILE,), in_specs=[row, row, vec, vec, row],
        out_specs=[row, row, vec, vec, vec],
        out_shape=[jax.ShapeDtypeStruct((s, d), F32), jax.ShapeDtypeStruct((s, d), BF16), vshape, vshape, vshape],
        compiler_params=_params("arbitrary"),
    )(x, out, gate, g, dy)


def _loss_and_grad(y, target, *, name):
    s, d = y.shape

    def body(y_ref, t_ref, dy_ref, loss_ref):
        @pl.when(pl.program_id(0) == 0)
        def _():
            loss_ref[...] = jnp.zeros_like(loss_ref)

        e = y_ref[...] - t_ref[...]
        dy_ref[...] = e * (1.0 / d)
        loss_ref[...] += jnp.sum(jnp.sum(e * e, axis=0, keepdims=True), axis=1, keepdims=True) * (0.5 / d)

    row = pl.BlockSpec((ROW_TILE, d), lambda i: (i, 0))
    return pl.pallas_call(
        body, name=name, grid=(s // ROW_TILE,), in_specs=[row, row],
        out_specs=[row, pl.BlockSpec((1, 128), lambda i: (0, 0))],
        out_shape=[jax.ShapeDtypeStruct((s, d), F32), jax.ShapeDtypeStruct((1, 128), F32)],
        compiler_params=_params("arbitrary"),
    )(y, target)


_SLOPES = np.asarray(2.0 ** (-8.0 * (np.arange(A_HEADS, dtype=np.float32) + 1.0) / A_HEADS), dtype=np.float32)


def _attn_scores(q, kw, slope, dist, valid):
    s = lax.dot_general(q, kw, _DIMS["nt"], preferred_element_type=F32) * (1.0 / math.sqrt(A_HEAD_DIM))
    return jnp.where(valid, s - slope * dist, NEG_BIG)


def _attn_window(blk, length, win, dil):
    start = pl.multiple_of(jnp.clip(blk * A_QBLOCK - A_RADIUS, 0, length - win), A_RADIUS)
    qpos = blk * A_QBLOCK + lax.broadcasted_iota(jnp.int32, (A_QBLOCK, win), 0)
    kpos = start + lax.broadcasted_iota(jnp.int32, (A_QBLOCK, win), 1)
    delta = jnp.abs(kpos - qpos)
    return start, (delta * dil).astype(F32), delta <= A_RADIUS


A_BLOCKS_PER_STEP = 4
A_GROUP_COLS = 3 * A_WIDTH


def _tile_scratch(rows, width):
    return pltpu.VMEM((width // 128, rows, 128), F32)


def _put_tile(scr, val):
    for j in range(scr.shape[0]):
        scr[j] = val[:, j * 128:(j + 1) * 128]


def _get_tile(scr):
    return jnp.concatenate([scr[j] for j in range(scr.shape[0])], axis=1)


def _get_residue(scr, r, dil):
    rows = pl.ds(r, scr.shape[1] // dil, stride=dil)
    return jnp.concatenate([scr.at[j][rows, :] for j in range(scr.shape[0])], axis=1)


def _put_residue(scr, r, dil, val):
    rows = pl.ds(r, scr.shape[1] // dil, stride=dil)
    for j in range(scr.shape[0]):
        scr.at[j][rows, :] = val[:, j * 128:(j + 1) * 128]


def _mm_dilated(a, b, dil, *, name):
    m, k = a.shape
    n = b.shape[1]
    tm = 512

    def body(a_ref, b_ref, o_ref, acc_ref):
        _put_tile(acc_ref, jnp.dot(a_ref[...], b_ref[...], preferred_element_type=F32))
        for r in range(dil):
            o_ref[:, r * n:(r + 1) * n] = _get_residue(acc_ref, r, dil).astype(BF16)

    return pl.pallas_call(
        body, name=name, grid=(m // tm,),
        in_specs=[pl.BlockSpec((tm, k), lambda i: (i, 0)), pl.BlockSpec((k, n), lambda i: (0, 0))],
        out_specs=pl.BlockSpec((tm // dil, dil * n), lambda i: (i, 0)),
        out_shape=jax.ShapeDtypeStruct((m // dil, dil * n), BF16), scratch_shapes=[_tile_scratch(tm, n)],
        compiler_params=_params("parallel"),
    )(a, b)


def _attn_fwd(pv, group, *, name):
    dil = DILATIONS[group]
    length = pv.shape[0]
    cb, qoff = pv.shape[1] // (128 * dil), 0
    win = min(2 * A_QBLOCK, length)
    nblk = length // A_QBLOCK
    per = A_BLOCKS_PER_STEP if nblk % A_BLOCKS_PER_STEP == 0 else 1

    def body(slope_ref, q_ref, k_ref, v_ref, o_ref, lse_ref):
        hp = pl.program_id(1)
        for u in range(per):
            rows = slice(u * A_QBLOCK, (u + 1) * A_QBLOCK)
            start, dist, valid = _attn_window(pl.program_id(2) * per + u, length, win, dil)
            kw = k_ref[pl.ds(start, win), :]
            vw = v_ref[pl.ds(start, win), :]
            q = q_ref[rows, :]
            outs, lses = [], []
            for hh in range(2):
                sl = slice(hh * A_HEAD_DIM, (hh + 1) * A_HEAD_DIM)
                sc = _attn_scores(q[:, sl], kw[:, sl], slope_ref[hp * 2 + hh], dist, valid)
                m = jnp.max(sc, axis=-1, keepdims=True)
                p = jnp.exp(sc - m)
                z = jnp.sum(p, axis=-1, keepdims=True)
                o = jnp.dot(p.astype(BF16), vw[:, sl], preferred_element_type=F32) / z
                outs.append(o)
                lses.append(jnp.broadcast_to(m + jnp.log(z), (A_QBLOCK, A_HEAD_DIM)))
            o_ref[rows, :] = jnp.concatenate(outs, axis=1)
            lse_ref[rows, :] = jnp.concatenate(lses, axis=1)

    qspec = pl.BlockSpec((per * A_QBLOCK, 128), lambda r, hp, b: (b, r * cb + qoff + hp))
    kspec = pl.BlockSpec((length, 128), lambda r, hp, b: (0, r * cb + qoff + 8 + hp))
    vspec = pl.BlockSpec((length, 128), lambda r, hp, b: (0, r * cb + qoff + 16 + hp))
    ospec = pl.BlockSpec((per * A_QBLOCK, 128), lambda r, hp, b: (b, r * 8 + hp))
    oshape = jax.ShapeDtypeStruct((length, dil * A_WIDTH), F32)
    o, lse = pl.pallas_call(
        body, name=name, grid=(dil, 8, nblk // per),
        in_specs=[pl.BlockSpec(memory_space=pltpu.SMEM), qspec, kspec, vspec], out_specs=[ospec, ospec],
        out_shape=[oshape, oshape], compiler_params=_params("parallel", "parallel", "arbitrary"),
    )(jnp.asarray(_SLOPES), pv, pv, pv)
    return o, lse


def _attn_bwd(pv, group, do, o, lse, *, name):
    dil = DILATIONS[group]
    length = pv.shape[0]
    cb, qoff = pv.shape[1] // (128 * dil), 0
    win = min(2 * A_QBLOCK, length)
    nblk = length // A_QBLOCK
    per = A_BLOCKS_PER_STEP if nblk % A_BLOCKS_PER_STEP == 0 else 1
    nstep = nblk // per
    scale = 1.0 / math.sqrt(A_HEAD_DIM)

    def body(slope_ref, q_ref, k_ref, v_ref, do_ref, o_ref, lse_ref, dq_ref, dk_ref, dv_ref, dk_acc, dv_acc):
        hp, step = pl.program_id(1), pl.program_id(2)

        @pl.when(step == 0)
        def _():
            dk_acc[...] = jnp.zeros_like(dk_acc)
            dv_acc[...] = jnp.zeros_like(dv_acc)

        for u in range(per):
            rows = slice(u * A_QBLOCK, (u + 1) * A_QBLOCK)
            start, dist, valid = _attn_window(step * per + u, length, win, dil)
            kw = k_ref[pl.ds(start, win), :]
            vw = v_ref[pl.ds(start, win), :]
            q = q_ref[rows, :]
            do_b = do_ref[rows, :]
            dsum = do_b.astype(F32) * o_ref[rows, :]
            lse_b = lse_ref[rows, :]
            dqs, dks, dvs = [], [], []
            for hh in range(2):
                sl = slice(hh * A_HEAD_DIM, (hh + 1) * A_HEAD_DIM)
                sc = _attn_scores(q[:, sl], kw[:, sl], slope_ref[hp * 2 + hh], dist, valid)
                p = jnp.exp(sc - lse_b[:, hh * A_HEAD_DIM:hh * A_HEAD_DIM + 1])
                dp = lax.dot_general(do_b[:, sl], vw[:, sl], _DIMS["nt"], preferred_element_type=F32)
                ds = (p * (dp - jnp.sum(dsum[:, sl], axis=-1, keepdims=True))).astype(BF16)
                dqs.append(jnp.dot(ds, kw[:, sl], preferred_element_type=F32) * scale)
                dks.append(lax.dot_general(ds, q[:, sl], _DIMS["tn"], preferred_element_type=F32) * scale)
                dvs.append(lax.dot_general(p.astype(BF16), do_b[:, sl], _DIMS["tn"], preferred_element_type=F32))
            dq_ref[rows, :] = jnp.concatenate(dqs, axis=1).astype(BF16)
            dk_acc[pl.ds(start, win), :] += jnp.concatenate(dks, axis=1)
            dv_acc[pl.ds(start, win), :] += jnp.concatenate(dvs, axis=1)

        @pl.when(step == nstep - 1)
        def _():
            dk_ref[...] = dk_acc[...].astype(BF16)
            dv_ref[...] = dv_acc[...].astype(BF16)

    qspec = pl.BlockSpec((per * A_QBLOCK, 128), lambda r, hp, b: (b, r * cb + qoff + hp))
    kspec = pl.BlockSpec((length, 128), lambda r, hp, b: (0, r * cb + qoff + 8 + hp))
    vspec = pl.BlockSpec((length, 128), lambda r, hp, b: (0, r * cb + qoff + 16 + hp))
    bspec = pl.BlockSpec((per * A_QBLOCK, 128), lambda r, hp, b: (b, r * 8 + hp))
    fspec = pl.BlockSpec((length, 128), lambda r, hp, b: (0, r * 8 + hp))
    oshape = jax.ShapeDtypeStruct((length, dil * A_WIDTH), BF16)
    dq, dk, dv = pl.pallas_call(
        body, name=name, grid=(dil, 8, nstep),
        in_specs=[pl.BlockSpec(memory_space=pltpu.SMEM), qspec, kspec, vspec, bspec, bspec, bspec],
        out_specs=[bspec, fspec, fspec], out_shape=[oshape, oshape, oshape],
        scratch_shapes=[pltpu.VMEM((length, 128), F32), pltpu.VMEM((length, 128), F32)],
        compiler_params=_params("parallel", "parallel", "arbitrary"),
    )(jnp.asarray(_SLOPES), pv, pv, pv, do, o, lse)
    return dq, dk, dv


A_GATE_BLOCK = 3
A_ROWS = 256


def _lanes_of(r):
    return slice(r * A_WIDTH, (r + 1) * A_WIDTH)


def _dilated_spec(dil):
    return pl.BlockSpec((A_ROWS // dil, dil * A_WIDTH), lambda i: (i, 0))


def _attn_combine(o0, l0, o1, l1, o2, l2, proj0, *, name):
    s = proj0.shape[0]

    def body(o0_ref, l0_ref, o1_ref, l1_ref, o2_ref, l2_ref, gate_ref, y_ref, o_ref, lse_ref, so1, sl1, so2, sl2):
        for src, dst, dil in ((o1_ref, so1, DILATIONS[1]), (l1_ref, sl1, DILATIONS[1]),
                              (o2_ref, so2, DILATIONS[2]), (l2_ref, sl2, DILATIONS[2])):
            for r in range(dil):
                _put_residue(dst, r, dil, src[:, _lanes_of(r)])
        la, lb, lc = l0_ref[...], _get_tile(sl1), _get_tile(sl2)
        m = jnp.maximum(jnp.maximum(la, lb), lc)
        ea, eb, ec = jnp.exp(la - m), jnp.exp(lb - m), jnp.exp(lc - m)
        den = ea + eb + ec
        o = (ea * o0_ref[...] + eb * _get_tile(so1) + ec * _get_tile(so2)) / den
        o_ref[...] = o
        lse_ref[...] = m + jnp.log(den)
        y_ref[...] = (o * _silu_and_grad(gate_ref[...].astype(F32))[0]).astype(BF16)

    row = pl.BlockSpec((A_ROWS, A_WIDTH), lambda i: (i, 0))
    gspec = pl.BlockSpec((A_ROWS, A_WIDTH), lambda i: (i, A_GATE_BLOCK))
    d1, d2 = _dilated_spec(DILATIONS[1]), _dilated_spec(DILATIONS[2])
    return pl.pallas_call(
        body, name=name, grid=(s // A_ROWS,), in_specs=[row, row, d1, d1, d2, d2, gspec], out_specs=[row, row, row],
        out_shape=[jax.ShapeDtypeStruct((s, A_WIDTH), BF16), jax.ShapeDtypeStruct((s, A_WIDTH), F32),
                   jax.ShapeDtypeStruct((s, A_WIDTH), F32)],
        scratch_shapes=[_tile_scratch(A_ROWS, A_WIDTH)] * 4, compiler_params=_params("parallel"),
    )(o0, l0, o1, l1, o2, l2, proj0)


def _attn_combine_bwd(dy, o, lse, proj0, *, name):
    s = proj0.shape[0]

    def body(dy_ref, o_ref, lse_ref, gate_ref, dg_ref, do_ref, do1, o1, l1, do2, o2, l2, s_do, s_o, s_l):
        si, dsi = _silu_and_grad(gate_ref[...].astype(F32))
        dyv = dy_ref[...]
        ov = o_ref[...]
        do = dyv * si
        _put_tile(s_do, do)
        _put_tile(s_o, ov)
        _put_tile(s_l, lse_ref[...])
        do_ref[...] = do.astype(BF16)
        dg_ref[...] = (dyv * ov * dsi).astype(BF16)
        for (do_d, o_d, l_d), dil in (((do1, o1, l1), DILATIONS[1]), ((do2, o2, l2), DILATIONS[2])):
            for r in range(dil):
                do_d[:, _lanes_of(r)] = _get_residue(s_do, r, dil).astype(BF16)
                o_d[:, _lanes_of(r)] = _get_residue(s_o, r, dil)
                l_d[:, _lanes_of(r)] = _get_residue(s_l, r, dil)

    row = pl.BlockSpec((A_ROWS, A_WIDTH), lambda i: (i, 0))
    gspec = pl.BlockSpec((A_ROWS, A_WIDTH), lambda i: (i, A_GATE_BLOCK))
    shp = jax.ShapeDtypeStruct((s, A_WIDTH), BF16)
    dilated = lambda dil, dtype: jax.ShapeDtypeStruct((s // dil, dil * A_WIDTH), dtype)
    d1, d2 = _dilated_spec(DILATIONS[1]), _dilated_spec(DILATIONS[2])
    return pl.pallas_call(
        body, name=name, grid=(s // A_ROWS,), in_specs=[row, row, row, gspec],
        out_specs=[row, row, d1, d1, d1, d2, d2, d2],
        out_shape=[shp, shp, dilated(DILATIONS[1], BF16), dilated(DILATIONS[1], F32), dilated(DILATIONS[1], F32),
                   dilated(DILATIONS[2], BF16), dilated(DILATIONS[2], F32), dilated(DILATIONS[2], F32)],
        scratch_shapes=[_tile_scratch(A_ROWS, A_WIDTH)] * 3, compiler_params=_params("parallel"),
    )(dy, o, lse, proj0)


def _assemble_dproj(parts0, parts1, parts2, dgate, *, name):
    s = dgate.shape[0]

    def body(*refs):
        ins, out_ref, scr = refs[:10], refs[10], refs[11]
        for p in range(3):
            out_ref[:, _lanes_of(p)] = ins[p][...]
        for g, dil in ((1, DILATIONS[1]), (2, DILATIONS[2])):
            for p in range(3):
                src = ins[3 * g + p]
                for r in range(dil):
                    _put_residue(scr, r, dil, src[:, _lanes_of(r)].astype(F32))
                out_ref[:, _lanes_of(3 * g + p)] = _get_tile(scr).astype(BF16)
        out_ref[:, _lanes_of(9)] = ins[9][...]

    row = pl.BlockSpec((A_ROWS, A_WIDTH), lambda i: (i, 0))
    d1, d2 = _dilated_spec(DILATIONS[1]), _dilated_spec(DILATIONS[2])
    return pl.pallas_call(
        body, name=name, grid=(s // A_ROWS,), in_specs=[row] * 3 + [d1] * 3 + [d2] * 3 + [row],
        out_specs=pl.BlockSpec((A_ROWS, A_IN_COLS), lambda i: (i, 0)),
        out_shape=jax.ShapeDtypeStruct((s, A_IN_COLS), BF16),
        scratch_shapes=[_tile_scratch(A_ROWS, A_WIDTH)], compiler_params=_params("parallel"),
    )(*parts0, *parts1, *parts2, dgate)


CONV_TILE = 256
CONV_SUB = 4


def _conv_taps(xe, n):
    return [xe if j == 2 else pltpu.roll(xe, (2 - j) % n, 0) for j in range(SSM_CONV)]


def _conv_fwd(xpad, w, b, *, name):
    s = xpad.shape[0] - 2 * CONV_HALO
    n = CONV_TILE + 2 * CONV_HALO
    ncol = SSM_CONV_DIM // 128

    sub = min(CONV_SUB, s // CONV_TILE)

    def body(x_ref, w_ref, b_ref, o_ref):
        base = pl.program_id(1) * (sub * CONV_TILE)

        def tile(k, carry):
            r0 = pl.multiple_of(k * CONV_TILE, CONV_TILE)
            t0 = pl.multiple_of(base + r0, CONV_TILE)
            taps = _conv_taps(x_ref[pl.ds(t0, n), :].astype(F32), n)
            pre = b_ref[...]
            for j in range(SSM_CONV):
                pre = pre + w_ref[j:j + 1, :] * taps[j]
            o_ref[pl.ds(r0, CONV_TILE), :] = _silu_and_grad(pre[CONV_HALO:CONV_HALO + CONV_TILE])[0].astype(BF16)
            return carry

        lax.fori_loop(0, sub, tile, 0)

    return pl.pallas_call(
        body, name=name, grid=(ncol, s // (sub * CONV_TILE)),
        in_specs=[pl.BlockSpec((s + 2 * CONV_HALO, 128), lambda j, i: (0, j)),
                  pl.BlockSpec((SSM_CONV, 128), lambda j, i: (0, j)), pl.BlockSpec((1, 128), lambda j, i: (0, j))],
        out_specs=pl.BlockSpec((sub * CONV_TILE, 128), lambda j, i: (i, j)),
        out_shape=jax.ShapeDtypeStruct((s, SSM_CONV_DIM), BF16), compiler_params=_params("parallel", "arbitrary"),
    )(xpad, w, b)


def _conv_bwd(xpad, dapad, w, b, *, name):
    s = xpad.shape[0] - 2 * CONV_HALO
    n = CONV_TILE + 2 * CONV_HALO
    ncol = SSM_CONV_DIM // 128
    mid = slice(CONV_HALO, CONV_HALO + CONV_TILE)
    sub = min(CONV_SUB, s // CONV_TILE)

    def body(x_ref, da_ref, w_ref, b_ref, dx_ref, dw_ref, db_ref):
        @pl.when(pl.program_id(1) == 0)
        def _():
            dw_ref[...] = jnp.zeros_like(dw_ref)
            db_ref[...] = jnp.zeros_like(db_ref)

        base = pl.program_id(1) * (sub * CONV_TILE)

        def tile(k, carry):
            r0 = pl.multiple_of(k * CONV_TILE, CONV_TILE)
            t0 = pl.multiple_of(base + r0, CONV_TILE)
            taps = _conv_taps(x_ref[pl.ds(t0, n), :].astype(F32), n)
            pre = b_ref[...]
            for j in range(SSM_CONV):
                pre = pre + w_ref[j:j + 1, :] * taps[j]
            dpre = da_ref[pl.ds(t0, n), :] * _silu_and_grad(pre)[1]
            dx = jnp.zeros((CONV_TILE, 128), F32)
            for j in range(SSM_CONV):
                back = dpre if j == 2 else pltpu.roll(dpre, (j - 2) % n, 0)
                dx = dx + w_ref[j:j + 1, :] * back[mid]
                dw_ref[j:j + 1, :] += jnp.sum(dpre[mid] * taps[j][mid], axis=0, keepdims=True)
            dx_ref[pl.ds(r0, CONV_TILE), :] = dx.astype(BF16)
            db_ref[...] += jnp.sum(dpre[mid], axis=0, keepdims=True)
            return carry

        lax.fori_loop(0, sub, tile, 0)

    full = pl.BlockSpec((s + 2 * CONV_HALO, 128), lambda j, i: (0, j))
    wspec = pl.BlockSpec((SSM_CONV, 128), lambda j, i: (0, j))
    bspec = pl.BlockSpec((1, 128), lambda j, i: (0, j))
    return pl.pallas_call(
        body, name=name, grid=(ncol, s // (sub * CONV_TILE)), in_specs=[full, full, wspec, bspec],
        out_specs=[pl.BlockSpec((sub * CONV_TILE, 128), lambda j, i: (i, j)), wspec, bspec],
        out_shape=[jax.ShapeDtypeStruct((s, SSM_CONV_DIM), BF16), jax.ShapeDtypeStruct((SSM_CONV, SSM_CONV_DIM), F32),
                   jax.ShapeDtypeStruct((1, SSM_CONV_DIM), F32)],
        compiler_params=_params("parallel", "arbitrary"),
    )(xpad, dapad, w, b)


HPG = SSM_HEADS // SSM_GROUPS
GW = HPG * SSM_HEAD_DIM
T = SSM_CHUNK


def _ssd_specs(nc):
    ceff = lambda d, c: jnp.where(d == 0, c, nc - 1 - c)
    return ceff, [
        pl.BlockSpec((T, GW), lambda d, g, c: (ceff(d, c), g)),
        pl.BlockSpec((T, SSM_STATE), lambda d, g, c: (ceff(d, c), SSM_INNER // 128 + g)),
        pl.BlockSpec((T, SSM_STATE), lambda d, g, c: (ceff(d, c), SSM_INNER // 128 + SSM_GROUPS + g)),
        pl.BlockSpec((None, None, T, HPG), lambda d, g, c: (d, g, ceff(d, c), 0)),
        pl.BlockSpec((None, None, HPG, T), lambda d, g, c: (d, g, 0, ceff(d, c))),
        pl.BlockSpec((None, None, 2, HPG), lambda d, g, c: (d, g, 0, 0)),
        pl.BlockSpec((None, None, HPG, 2), lambda d, g, c: (d, g, 0, 0)),
    ]


def _ssd_chunk_common(d, dt_ref, dtt_ref, prr_ref, prc_ref):
    sgn = 1 - 2 * d
    ri = lax.broadcasted_iota(jnp.int32, (T, T), 0)
    ci = lax.broadcasted_iota(jnp.int32, (T, T), 1)
    mask = ((ri - ci) * sgn) >= 0
    maskf = mask.astype(F32)
    bias_r, a_r = prr_ref[0:1, :], prr_ref[1:2, :]
    bias_c, a_c = prc_ref[:, 0:1], prc_ref[:, 1:2]
    raw = dt_ref[...] + bias_r
    dt_rows = _softplus(raw)
    dt_lanes = _softplus(dtt_ref[...] + bias_c)
    a_rows = dt_rows * a_r
    acum_rows = jnp.dot(maskf, a_rows, precision=HIGHEST, preferred_element_type=F32)
    acum_lanes = lax.dot_general(dt_lanes * a_c, maskf, _DIMS["nt"], precision=HIGHEST, preferred_element_type=F32)
    tot = jnp.sum(a_rows, axis=0, keepdims=True)
    return mask, maskf, raw, dt_rows, a_r, acum_rows, acum_lanes, tot


def _lanes_per_head(pieces):
    return jnp.concatenate([jnp.broadcast_to(p, (p.shape[0], SSM_HEAD_DIM)) for p in pieces], axis=1)


def _ssd_fwd_v1(xbc, dtr, dtt, pr_rows, pr_cols, *, name):
    s = xbc.shape[0]
    nc = s // T
    ceff, in_specs = _ssd_specs(nc)

    def body(x_ref, b_ref, c_ref, dt_ref, dtt_ref, prr_ref, prc_ref, y_ref, hs_ref, st_ref):
        d, c = pl.program_id(0), pl.program_id(2)

        @pl.when(c == 0)
        def _():
            st_ref[...] = jnp.zeros_like(st_ref)

        mask, _, _, dt_rows, _, acum_rows, acum_lanes, tot = _ssd_chunk_common(d, dt_ref, dtt_ref, prr_ref, prc_ref)
        xs = x_ref[...].astype(F32)
        bm, cm = b_ref[...], c_ref[...]
        hprev = st_ref[...]
        hs_ref[...] = hprev
        cb = lax.dot_general(cm, bm, _DIMS["nt"], preferred_element_type=F32)
        ch = jnp.dot(cm, hprev.astype(BF16), preferred_element_type=F32)
        ys, xgds, etots = [], [], []
        for j in range(HPG):
            sl = slice(j * SSM_HEAD_DIM, (j + 1) * SSM_HEAD_DIM)
            ac, al = acum_rows[:, j:j + 1], acum_lanes[j:j + 1, :]
            lm = jnp.where(mask, jnp.exp(jnp.minimum(ac - al, 0.0)), 0.0)
            xg = xs[:, sl] * dt_rows[:, j:j + 1]
            yd = jnp.dot((cb * lm).astype(BF16), xg.astype(BF16), preferred_element_type=F32)
            ys.append(yd + jnp.exp(ac) * ch[:, sl])
            xgds.append(xg * jnp.exp(tot[:, j:j + 1] - ac))
            etots.append(jnp.exp(tot[:, j:j + 1]))
        y_ref[...] = jnp.concatenate(ys, axis=1)
        new = lax.dot_general(bm, jnp.concatenate(xgds, axis=1).astype(BF16), _DIMS["tn"], preferred_element_type=F32)
        st_ref[...] = hprev * _lanes_per_head(etots) + new

    return pl.pallas_call(
        body, name=name, grid=(2, SSM_GROUPS, nc), in_specs=in_specs,
        out_specs=[pl.BlockSpec((None, T, GW), lambda d, g, c: (d, ceff(d, c), g)),
                   pl.BlockSpec((None, None, None, SSM_STATE, GW), lambda d, g, c: (d, ceff(d, c), g, 0, 0))],
        out_shape=[jax.ShapeDtypeStruct((2, s, SSM_INNER), F32),
                   jax.ShapeDtypeStruct((2, nc, SSM_GROUPS, SSM_STATE, GW), F32)],
        scratch_shapes=[pltpu.VMEM((SSM_STATE, GW), F32)],
        compiler_params=_params("parallel", "parallel", "arbitrary"),
    )(xbc, xbc, xbc, dtr, dtt, pr_rows, pr_cols)


def _put_lane(j, col):
    lane = lax.broadcasted_iota(jnp.int32, (col.shape[0], HPG), 1)
    return jnp.where(lane == j, col, 0.0)


def _ssd_bwd_v1(xbc, dtr, dtt, pr_rows, pr_cols, dvec, hs, dy, *, name):
    s = xbc.shape[0]
    nc = s // T
    cb_of = lambda d, c: jnp.where(d == 0, nc - 1 - c, c)
    in_specs = [
        pl.BlockSpec((T, GW), lambda d, g, c: (cb_of(d, c), g)),
        pl.BlockSpec((T, SSM_STATE), lambda d, g, c: (cb_of(d, c), SSM_INNER // 128 + g)),
        pl.BlockSpec((T, SSM_STATE), lambda d, g, c: (cb_of(d, c), SSM_INNER // 128 + SSM_GROUPS + g)),
        pl.BlockSpec((None, None, T, HPG), lambda d, g, c: (d, g, cb_of(d, c), 0)),
        pl.BlockSpec((None, None, HPG, T), lambda d, g, c: (d, g, 0, cb_of(d, c))),
        pl.BlockSpec((None, None, 2, HPG), lambda d, g, c: (d, g, 0, 0)),
        pl.BlockSpec((None, None, HPG, 2), lambda d, g, c: (d, g, 0, 0)),
        pl.BlockSpec((1, GW), lambda d, g, c: (0, g)),
        pl.BlockSpec((None, None, None, SSM_STATE, GW), lambda d, g, c: (d, cb_of(d, c), g, 0, 0)),
        pl.BlockSpec((T, GW), lambda d, g, c: (cb_of(d, c), g)),
    ]

    def body(x_ref, b_ref, c_ref, dt_ref, dtt_ref, prr_ref, prc_ref, dvec_ref, hs_ref, dy_ref,
             dxs_ref, db_ref, dc_ref, ddt_ref, dalog_ref, dbias_ref, g_ref):
        d, c = pl.program_id(0), pl.program_id(2)

        @pl.when(c == 0)
        def _():
            g_ref[...] = jnp.zeros_like(g_ref)
            dalog_ref[...] = jnp.zeros_like(dalog_ref)
            dbias_ref[...] = jnp.zeros_like(dbias_ref)

        mask, maskf, raw, dt_rows, a_r, acum_rows, acum_lanes, tot = _ssd_chunk_common(
            d, dt_ref, dtt_ref, prr_ref, prc_ref)
        xs = x_ref[...].astype(F32)
        bm, cm = b_ref[...], c_ref[...]
        hst = hs_ref[...]
        gst = g_ref[...]
        dyv = dy_ref[...]
        dyb = dyv.astype(BF16)
        dv = dvec_ref[...] * (1 - d).astype(F32)
        cb = lax.dot_general(cm, bm, _DIMS["nt"], preferred_element_type=F32)
        ch = jnp.dot(cm, hst.astype(BF16), preferred_element_type=F32)
        bg = jnp.dot(bm, gst.astype(BF16), preferred_element_type=F32)
        hg = hst * gst
        dcb = jnp.zeros((T, T), F32)
        dacum = jnp.zeros((T, HPG), F32)
        rx = jnp.zeros((T, HPG), F32)
        dtot = jnp.zeros((1, HPG), F32)
        dxss, dyes, xgds, etots = [], [], [], []
        for j in range(HPG):
            sl = slice(j * SSM_HEAD_DIM, (j + 1) * SSM_HEAD_DIM)
            ac, al = acum_rows[:, j:j + 1], acum_lanes[j:j + 1, :]
            lm = jnp.where(mask, jnp.exp(jnp.minimum(ac - al, 0.0)), 0.0)
            m = cb * lm
            dtj = dt_rows[:, j:j + 1]
            xsj = xs[:, sl]
            xg = xsj * dtj
            dyj = dyv[:, sl]
            ec = jnp.exp(ac)
            etot = jnp.exp(tot[:, j:j + 1])
            decay = jnp.exp(tot[:, j:j + 1] - ac)
            dm = lax.dot_general(dyb[:, sl], xg.astype(BF16), _DIMS["nt"], preferred_element_type=F32)
            w = dm * m
            dcb = dcb + dm * lm
            bgj = bg[:, sl]
            dxg = lax.dot_general(m.astype(BF16), dyb[:, sl], _DIMS["tn"], preferred_element_type=F32) + decay * bgj
            xb = decay * jnp.sum(xg * bgj, axis=-1, keepdims=True)
            da_j = (jnp.sum(w, axis=-1, keepdims=True) - jnp.sum(w.T, axis=-1, keepdims=True)
                    + jnp.sum(ec * ch[:, sl] * dyj, axis=-1, keepdims=True) - xb)
            dacum = dacum + _put_lane(j, da_j)
            rx = rx + _put_lane(j, jnp.sum(dxg * xsj, axis=-1, keepdims=True))
            dtot_j = (etot * jnp.sum(jnp.sum(hg[:, sl], axis=0, keepdims=True), axis=1, keepdims=True)
                      + jnp.sum(xb, axis=0, keepdims=True))
            dtot = dtot + _put_lane(j, dtot_j)
            dxss.append(dxg * dtj + dv[:, sl] * dyj)
            dyes.append(dyj * ec)
            xgds.append(xg * decay)
            etots.append(etot)
        da = lax.dot_general(maskf, dacum, _DIMS["tn"], precision=HIGHEST, preferred_element_type=F32) + dtot
        ddt = da * a_r + rx
        draw = ddt * _sigmoid(raw)
        ddt_ref[...] = draw
        dbias_ref[...] += jnp.sum(draw, axis=0, keepdims=True)
        dalog_ref[...] += jnp.sum(da * dt_rows, axis=0, keepdims=True) * a_r
        dxs_ref[...] = jnp.concatenate(dxss, axis=1)
        dye = jnp.concatenate(dyes, axis=1).astype(BF16)
        xgd = jnp.concatenate(xgds, axis=1).astype(BF16)
        dcbb = dcb.astype(BF16)
        dc_ref[...] = (jnp.dot(dcbb, bm, preferred_element_type=F32)
                       + lax.dot_general(dye, hst.astype(BF16), _DIMS["nt"], preferred_element_type=F32))
        db_ref[...] = (lax.dot_general(dcbb, cm, _DIMS["tn"], preferred_element_type=F32)
                       + lax.dot_general(xgd, gst.astype(BF16), _DIMS["nt"], preferred_element_type=F32))
        g_ref[...] = lax.dot_general(cm, dye, _DIMS["tn"], preferred_element_type=F32) + gst * _lanes_per_head(etots)

    small = pl.BlockSpec((None, None, 1, HPG), lambda d, g, c: (d, g, 0, 0))
    sshape = jax.ShapeDtypeStruct((2, SSM_GROUPS, 1, HPG), F32)
    return pl.pallas_call(
        body, name=name, grid=(2, SSM_GROUPS, nc), in_specs=in_specs,
        out_specs=[pl.BlockSpec((None, T, GW), lambda d, g, c: (d, cb_of(d, c), g)),
                   pl.BlockSpec((None, T, SSM_STATE), lambda d, g, c: (d, cb_of(d, c), g)),
                   pl.BlockSpec((None, T, SSM_STATE), lambda d, g, c: (d, cb_of(d, c), g)),
                   pl.BlockSpec((None, None, T, HPG), lambda d, g, c: (d, g, cb_of(d, c), 0)), small, small],
        out_shape=[jax.ShapeDtypeStruct((2, s, SSM_INNER), F32),
                   jax.ShapeDtypeStruct((2, s, SSM_GROUPS * SSM_STATE), F32),
                   jax.ShapeDtypeStruct((2, s, SSM_GROUPS * SSM_STATE), F32),
                   jax.ShapeDtypeStruct((2, SSM_GROUPS, s, HPG), F32), sshape, sshape],
        scratch_shapes=[pltpu.VMEM((SSM_STATE, GW), F32)],
        compiler_params=_params("parallel", "parallel", "arbitrary"),
    )(xbc, xbc, xbc, dtr, dtt, pr_rows, pr_cols, dvec, hs, dy)


PAIRS = HPG // 2


def _scan_lanes(x, forward):
    lane = lax.broadcasted_iota(jnp.int32, x.shape, 1)
    p = x
    k = 1
    while k < T:
        p = p + jnp.where(lane >= k, pltpu.roll(p, k, 1), 0.0)
        k *= 2
    tot = p[:, T - 1:T]
    return jnp.where(forward, p, tot - p + x), tot


def _ssd_chunk(d, dt_ref, dtt_ref, prr_ref, prc_ref):
    sgn = 1 - 2 * d
    ri = lax.broadcasted_iota(jnp.int32, (T, T), 0)
    ci = lax.broadcasted_iota(jnp.int32, (T, T), 1)
    mask = ((ri - ci) * sgn) >= 0
    mask_t = ((ci - ri) * sgn) >= 0
    bias_r = prr_ref[0:1, :]
    bias_c, a_c = prc_ref[:, 0:1], prc_ref[:, 1:2]
    dt_rows = _softplus(dt_ref[...] + bias_r)
    raw_lanes = dtt_ref[...] + bias_c
    dt_lanes = _softplus(raw_lanes)
    acum_lanes, tot = _scan_lanes(dt_lanes * a_c, d == 0)
    return dict(mask=mask, mask_t=mask_t, head0=ci < SSM_HEAD_DIM, dt_rows=dt_rows, raw_lanes=raw_lanes,
                dt_lanes=dt_lanes, a_c=a_c, acum_lanes=acum_lanes, acum_rows=acum_lanes.T, tot=tot)


def _ssd_pair(ck, q):
    h0 = ck["head0"]
    colb = lambda rows, j: jnp.broadcast_to(rows[:, j:j + 1], (T, T))
    rowb = lambda lanes, j: jnp.broadcast_to(lanes[j:j + 1, :], (T, T))
    lms, lmts, acs = [], [], []
    for j in (2 * q, 2 * q + 1):
        ac, al = colb(ck["acum_rows"], j), rowb(ck["acum_lanes"], j)
        lms.append(jnp.where(ck["mask"], jnp.exp(jnp.minimum(ac - al, 0.0)), 0.0))
        lmts.append(jnp.where(ck["mask_t"], jnp.exp(jnp.minimum(al - ac, 0.0)), 0.0))
        acs.append(ac)
    ac_pair = jnp.where(h0, acs[0], acs[1])
    dt_pair = jnp.where(h0, colb(ck["dt_rows"], 2 * q), colb(ck["dt_rows"], 2 * q + 1))
    tot_pair = jnp.where(h0[0:1], ck["tot"][2 * q:2 * q + 1, :], ck["tot"][2 * q + 1:2 * q + 2, :])
    return dict(lm=lms, lmt=lmts, dt=dt_pair, ec=jnp.exp(ac_pair), decay=jnp.exp(tot_pair - ac_pair),
                etot=jnp.exp(tot_pair))


def _split_heads(h0, v):
    zero = jnp.zeros_like(v)
    return jnp.where(h0, v, zero), jnp.where(h0, zero, v)


GPS = 2
GSTEPS = SSM_GROUPS // GPS
B_BLOCK0 = SSM_INNER // (GPS * SSM_STATE)
C_BLOCK0 = (SSM_INNER + SSM_GROUPS * SSM_STATE) // (GPS * SSM_STATE)


def _ssd_in_specs(chunk):
    return [
        pl.BlockSpec((T, GPS * GW), lambda d, g, c: (chunk(d, c), g)),
        pl.BlockSpec((T, GPS * SSM_STATE), lambda d, g, c: (chunk(d, c), B_BLOCK0 + g)),
        pl.BlockSpec((T, GPS * SSM_STATE), lambda d, g, c: (chunk(d, c), C_BLOCK0 + g)),
        pl.BlockSpec((GPS * SSM_STATE, T), lambda d, g, c: (g, chunk(d, c))),
        pl.BlockSpec((GPS * SSM_STATE, T), lambda d, g, c: (g, chunk(d, c))),
        pl.BlockSpec((None, GPS, T, HPG), lambda d, g, c: (d, g, chunk(d, c), 0)),
        pl.BlockSpec((None, GPS, HPG, T), lambda d, g, c: (d, g, 0, chunk(d, c))),
        pl.BlockSpec((None, GPS, 2, HPG), lambda d, g, c: (d, g, 0, 0)),
        pl.BlockSpec((None, GPS, HPG, 2), lambda d, g, c: (d, g, 0, 0)),
    ]


def _group_refs(gi, wide, state_wide, t_wide, lead):
    return ([r.at[:, pl.ds(gi * GW, GW)] for r in wide] + [r.at[:, pl.ds(gi * SSM_STATE, SSM_STATE)] for r in state_wide]
            + [r.at[pl.ds(gi * SSM_STATE, SSM_STATE), :] for r in t_wide] + [r.at[gi] for r in lead])


def _ssd_fwd(xbc, bt, ct, dtr, dtt, pr_rows, pr_cols, *, name):
    s = xbc.shape[0]
    nc = s // T
    chunk = lambda d, c: jnp.where(d == 0, c, nc - 1 - c)

    def body(x_ref, b_ref, c_ref, bt_ref, ct_ref, dt_ref, dtt_ref, prr_ref, prc_ref, y_ref, hs_ref, st_ref):
        @pl.when(pl.program_id(2) == 0)
        def _():
            st_ref[...] = jnp.zeros_like(st_ref)

        for gi in range(GPS):
            group(*_group_refs(gi, [x_ref, y_ref], [b_ref, c_ref], [bt_ref, ct_ref],
                               [dt_ref, dtt_ref, prr_ref, prc_ref, hs_ref, st_ref]))

    def group(x_ref, y_ref, b_ref, c_ref, bt_ref, ct_ref, dt_ref, dtt_ref, prr_ref, prc_ref, hs_ref, st_ref):
        d = pl.program_id(0)
        ck = _ssd_chunk(d, dt_ref, dtt_ref, prr_ref, prc_ref)
        xs = x_ref[...].astype(F32)
        cm = c_ref[...]
        hprev = st_ref[...]
        hs_ref[...] = hprev
        cb = lax.dot_general(cm, b_ref[...], _DIMS["nt"], preferred_element_type=F32)
        ch = jnp.dot(cm, hprev.astype(BF16), preferred_element_type=F32)
        ys, xgds, etots = [], [], []
        for q in range(PAIRS):
            sl = slice(q * 128, (q + 1) * 128)
            pr = _ssd_pair(ck, q)
            xg = xs[:, sl] * pr["dt"]
            xg0, xg1 = _split_heads(ck["head0"], xg.astype(BF16))
            yd = (jnp.dot((cb * pr["lm"][0]).astype(BF16), xg0, preferred_element_type=F32)
                  + jnp.dot((cb * pr["lm"][1]).astype(BF16), xg1, preferred_element_type=F32))
            ys.append(yd + pr["ec"] * ch[:, sl])
            xgds.append(xg * pr["decay"])
            etots.append(pr["etot"])
        y_ref[...] = jnp.concatenate(ys, axis=1)
        new = jnp.dot(bt_ref[...], jnp.concatenate(xgds, axis=1).astype(BF16), preferred_element_type=F32)
        st_ref[...] = hprev * jnp.concatenate(etots, axis=1) + new

    return pl.pallas_call(
        body, name=name, grid=(2, GSTEPS, nc), in_specs=_ssd_in_specs(chunk),
        out_specs=[pl.BlockSpec((None, T, GPS * GW), lambda d, g, c: (d, chunk(d, c), g)),
                   pl.BlockSpec((None, None, GPS, SSM_STATE, GW), lambda d, g, c: (d, chunk(d, c), g, 0, 0))],
        out_shape=[jax.ShapeDtypeStruct((2, s, SSM_INNER), F32),
                   jax.ShapeDtypeStruct((2, nc, SSM_GROUPS, SSM_STATE, GW), F32)],
        scratch_shapes=[pltpu.VMEM((GPS, SSM_STATE, GW), F32)],
        compiler_params=_params("parallel", "parallel", "arbitrary"),
    )(xbc, xbc, xbc, bt, ct, dtr, dtt, pr_rows, pr_cols)


def _ssd_bwd(xbc, bt, ct, dtr, dtt, pr_rows, pr_cols, dvec, hs, y2, dy, *, name):
    s = xbc.shape[0]
    nc = s // T
    chunk = lambda d, c: jnp.where(d == 0, nc - 1 - c, c)
    in_specs = _ssd_in_specs(chunk) + [
        pl.BlockSpec((1, GPS * GW), lambda d, g, c: (0, g)),
        pl.BlockSpec((None, None, GPS, SSM_STATE, GW), lambda d, g, c: (d, chunk(d, c), g, 0, 0)),
        pl.BlockSpec((None, T, GPS * GW), lambda d, g, c: (d, chunk(d, c), g)),
        pl.BlockSpec((T, GPS * GW), lambda d, g, c: (chunk(d, c), g)),
    ]

    def body(x_ref, b_ref, c_ref, bt_ref, ct_ref, dt_ref, dtt_ref, prr_ref, prc_ref, dvec_ref, hs_ref, y_ref, dy_ref,
             dxs_ref, db_ref, dc_ref, ddt_ref, dalog_ref, dbias_ref, g_ref):
        @pl.when(pl.program_id(2) == 0)
        def _():
            g_ref[...] = jnp.zeros_like(g_ref)
            dalog_ref[...] = jnp.zeros_like(dalog_ref)
            dbias_ref[...] = jnp.zeros_like(dbias_ref)

        for gi in range(GPS):
            group(*_group_refs(gi, [x_ref, dvec_ref, y_ref, dy_ref, dxs_ref], [b_ref, c_ref, db_ref, dc_ref], [bt_ref, ct_ref],
                               [dt_ref, dtt_ref, prr_ref, prc_ref, hs_ref, ddt_ref, dalog_ref, dbias_ref, g_ref]))

    def group(x_ref, dvec_ref, y_ref, dy_ref, dxs_ref, b_ref, c_ref, db_ref, dc_ref, bt_ref, ct_ref,
              dt_ref, dtt_ref, prr_ref, prc_ref, hs_ref, ddt_ref, dalog_ref, dbias_ref, g_ref):
        d = pl.program_id(0)
        ck = _ssd_chunk(d, dt_ref, dtt_ref, prr_ref, prc_ref)
        h0 = ck["head0"]
        xs = x_ref[...].astype(F32)
        bm, cm = b_ref[...], c_ref[...]
        hst = hs_ref[...]
        gst = g_ref[...]
        dyv = dy_ref[...]
        yv = y_ref[...]
        dv = dvec_ref[...] * (1 - d).astype(F32)
        cb = lax.dot_general(cm, bm, _DIMS["nt"], preferred_element_type=F32)
        cbt = jnp.dot(bm, ct_ref[...], preferred_element_type=F32)
        ch = jnp.dot(cm, hst.astype(BF16), preferred_element_type=F32)
        bg = jnp.dot(bm, gst.astype(BF16), preferred_element_type=F32)
        hg_cols = jnp.sum(hst * gst, axis=0, keepdims=True)
        lane16 = lax.broadcasted_iota(jnp.int32, (T, 2 * HPG), 1)
        sub8 = lax.broadcasted_iota(jnp.int32, (HPG, 1), 0)
        dcb = jnp.zeros((T, T), F32)
        acc16 = jnp.zeros((T, 2 * HPG), F32)
        dtot = jnp.zeros((HPG, 1), F32)
        dxss, dyes, xgds, etots = [], [], [], []
        for q in range(PAIRS):
            sl = slice(q * 128, (q + 1) * 128)
            pr = _ssd_pair(ck, q)
            xsp, dyp = xs[:, sl], dyv[:, sl]
            xg = xsp * pr["dt"]
            xgb = xg.astype(BF16)
            dyb = dyp.astype(BF16)
            dy0, dy1 = _split_heads(h0, dyb)
            dcb = dcb + (lax.dot_general(dy0, xgb, _DIMS["nt"], preferred_element_type=F32) * pr["lm"][0]
                         + lax.dot_general(dy1, xgb, _DIMS["nt"], preferred_element_type=F32) * pr["lm"][1])
            dxg_in = (jnp.dot((cbt * pr["lmt"][0]).astype(BF16), dy0, preferred_element_type=F32)
                      + jnp.dot((cbt * pr["lmt"][1]).astype(BF16), dy1, preferred_element_type=F32))
            xgd = xg * pr["decay"]
            xb = xgd * bg[:, sl]
            dxg = dxg_in + pr["decay"] * bg[:, sl]
            yo = pr["ec"] * ch[:, sl]
            dac = dyb.astype(F32) * (yv[:, sl] - yo) + dyp * yo - xgb.astype(F32) * dxg_in - xb
            d_0, d_1 = _split_heads(h0, dac)
            r_0, r_1 = _split_heads(h0, dxg * xsp)
            for hh, (d_h, r_h) in enumerate(((d_0, r_0), (d_1, r_1))):
                j = 2 * q + hh
                acc16 = (acc16 + jnp.where(lane16 == j, jnp.sum(d_h, axis=-1, keepdims=True), 0.0)
                         + jnp.where(lane16 == HPG + j, jnp.sum(r_h, axis=-1, keepdims=True), 0.0))
            tcols = pr["etot"] * hg_cols[:, sl] + jnp.sum(xb, axis=0, keepdims=True)
            t0, t1 = _split_heads(h0[0:1], tcols)
            dtot = (dtot + jnp.where(sub8 == 2 * q, jnp.sum(t0, axis=-1, keepdims=True), 0.0)
                    + jnp.where(sub8 == 2 * q + 1, jnp.sum(t1, axis=-1, keepdims=True), 0.0))
            dxss.append(dxg * pr["dt"] + dv[:, sl] * dyp)
            dyes.append(dyp * pr["ec"])
            xgds.append(xgd)
            etots.append(pr["etot"])
        acc_t = acc16.T
        da_lanes = _scan_lanes(acc_t[0:HPG], d != 0)[0] + dtot
        ddt = da_lanes * ck["a_c"] + acc_t[HPG:2 * HPG]
        draw = ddt * _sigmoid(ck["raw_lanes"])
        ddt_ref[...] = draw
        dbias_ref[...] += jnp.sum(draw, axis=-1, keepdims=True)
        dalog_ref[...] += jnp.sum(da_lanes * ck["dt_lanes"], axis=-1, keepdims=True) * ck["a_c"]
        dxs_ref[...] = jnp.concatenate(dxss, axis=1)
        dye = jnp.concatenate(dyes, axis=1).astype(BF16)
        xgd_all = jnp.concatenate(xgds, axis=1).astype(BF16)
        dcbb = dcb.astype(BF16)
        dc_ref[...] = (jnp.dot(dcbb, bm, preferred_element_type=F32)
                       + lax.dot_general(dye, hst.astype(BF16), _DIMS["nt"], preferred_element_type=F32))
        db_ref[...] = (lax.dot_general(dcbb, cm, _DIMS["tn"], preferred_element_type=F32)
                       + lax.dot_general(xgd_all, gst.astype(BF16), _DIMS["nt"], preferred_element_type=F32))
        g_ref[...] = jnp.dot(ct_ref[...], dye, preferred_element_type=F32) + gst * jnp.concatenate(etots, axis=1)

    small = pl.BlockSpec((None, GPS, HPG, 1), lambda d, g, c: (d, g, 0, 0))
    sshape = jax.ShapeDtypeStruct((2, SSM_GROUPS, HPG, 1), F32)
    return pl.pallas_call(
        body, name=name, grid=(2, GSTEPS, nc), in_specs=in_specs,
        out_specs=[pl.BlockSpec((None, T, GPS * GW), lambda d, g, c: (d, chunk(d, c), g)),
                   pl.BlockSpec((None, T, GPS * SSM_STATE), lambda d, g, c: (d, chunk(d, c), g)),
                   pl.BlockSpec((None, T, GPS * SSM_STATE), lambda d, g, c: (d, chunk(d, c), g)),
                   pl.BlockSpec((None, GPS, HPG, T), lambda d, g, c: (d, g, 0, chunk(d, c))), small, small],
        out_shape=[jax.ShapeDtypeStruct((2, s, SSM_INNER), F32),
                   jax.ShapeDtypeStruct((2, s, SSM_GROUPS * SSM_STATE), F32),
                   jax.ShapeDtypeStruct((2, s, SSM_GROUPS * SSM_STATE), F32),
                   jax.ShapeDtypeStruct((2, SSM_GROUPS, HPG, s), F32), sshape, sshape],
        scratch_shapes=[pltpu.VMEM((GPS, SSM_STATE, GW), F32)],
        compiler_params=_params("parallel", "parallel", "arbitrary"),
    )(xbc, xbc, xbc, bt, ct, dtr, dtt, pr_rows, pr_cols, dvec, hs, y2, dy)


def _gate_norm_fwd(y2, xbc, proj, dvec, nw, *, name):
    s = xbc.shape[0]
    tr = 256

    def body(y_ref, xs_ref, z_ref, dv_ref, w_ref, u_ref):
        yt = y_ref[0] + y_ref[1] + dv_ref[...] * xs_ref[...].astype(F32)
        yg = yt * _silu_and_grad(z_ref[...].astype(F32))[0]
        u_ref[...] = (yg * lax.rsqrt(jnp.mean(yg * yg, axis=-1, keepdims=True) + RMS_EPS) * w_ref[...]).astype(BF16)

    row = pl.BlockSpec((tr, SSM_INNER), lambda i: (i, 0))
    vec = pl.BlockSpec((1, SSM_INNER), lambda i: (0, 0))
    return pl.pallas_call(
        body, name=name, grid=(s // tr,),
        in_specs=[pl.BlockSpec((2, tr, SSM_INNER), lambda i: (0, i, 0)), row, row, vec, vec], out_specs=row,
        out_shape=jax.ShapeDtypeStruct((s, SSM_INNER), BF16), compiler_params=_params("parallel"),
    )(y2, xbc, proj, dvec, nw)


def _gate_norm_bwd(du, y2, xbc, proj, dvec, nw, *, name):
    s = xbc.shape[0]
    tr = 256

    def body(du_ref, y_ref, xs_ref, z_ref, dv_ref, w_ref, dy_ref, dz_ref, dw_ref, dd_ref):
        @pl.when(pl.program_id(0) == 0)
        def _():
            dw_ref[...] = jnp.zeros_like(dw_ref)
            dd_ref[...] = jnp.zeros_like(dd_ref)

        xs = xs_ref[...].astype(F32)
        yt = y_ref[0] + y_ref[1] + dv_ref[...] * xs
        si, dsi = _silu_and_grad(z_ref[...].astype(F32))
        yg = yt * si
        rstd = lax.rsqrt(jnp.mean(yg * yg, axis=-1, keepdims=True) + RMS_EPS)
        yhat = yg * rstd
        du = du_ref[...]
        dyn = du * w_ref[...]
        dyg = rstd * (dyn - yhat * jnp.mean(dyn * yhat, axis=-1, keepdims=True))
        dyt = dyg * si
        dy_ref[...] = dyt
        dz_ref[...] = (dyg * yt * dsi).astype(BF16)
        dw_ref[...] += jnp.sum(du * yhat, axis=0, keepdims=True)
        dd_ref[...] += jnp.sum(dyt * xs, axis=0, keepdims=True)

    row = pl.BlockSpec((tr, SSM_INNER), lambda i: (i, 0))
    vec = pl.BlockSpec((1, SSM_INNER), lambda i: (0, 0))
    vshape = jax.ShapeDtypeStruct((1, SSM_INNER), F32)
    return pl.pallas_call(
        body, name=name, grid=(s // tr,),
        in_specs=[row, pl.BlockSpec((2, tr, SSM_INNER), lambda i: (0, i, 0)), row, row, vec, vec],
        out_specs=[row, row, vec, vec],
        out_shape=[jax.ShapeDtypeStruct((s, SSM_INNER), F32), jax.ShapeDtypeStruct((s, SSM_INNER), BF16), vshape, vshape],
        compiler_params=_params("arbitrary"),
    )(du, y2, xbc, proj, dvec, nw)


def _layer_a_fwd(x, mod, w_in, w_out, ln_g, ln_b, tag):
    shift, scale, gate = mod
    h = _modulate(x, scale, shift, name=f"{tag}_modulate")
    w0 = jnp.concatenate([w_in[:, :A_GROUP_COLS], w_in[:, 3 * A_GROUP_COLS:]], axis=1)
    projs = [_mm(h, w0, mode="nn", out_dtype=BF16, tm=1024, tn=1024, tk=1024, name=f"{tag}_mm_in0")]
    for grp in (1, 2):
        projs.append(_mm_dilated(h, w_in[:, grp * A_GROUP_COLS:(grp + 1) * A_GROUP_COLS], DILATIONS[grp],
                                 name=f"{tag}_mm_in{grp}"))
    ol = []
    for grp in range(3):
        ol.extend(_attn_fwd(projs[grp], grp, name=f"{tag}_attn_fwd{grp}"))
    y, o, lse = _attn_combine(*ol, projs[0], name=f"{tag}_combine")
    out = _mm(y, w_out, mode="nn", out_dtype=F32, tm=512, tn=1024, tk=1024, name=f"{tag}_mm_out")
    xn = _resid_ln_fwd(x, out, gate, ln_g, ln_b, name=f"{tag}_resid_ln")
    return xn, (x, h, projs, y, o, lse, out)


def _layer_a_bwd(dxn, saved, mod, w_in, w_out, ln_g, tag):
    x, h, projs, y, o, lse, out = saved
    shift, scale, gate = mod
    dx_part, dout, dgate, dln_g, dln_b = _resid_ln_bwd(x, out, gate, ln_g, dxn, name=f"{tag}_resid_ln_bwd")
    dw_out = _mm(y, dout, mode="tn", out_dtype=F32, tm=1024, tn=1024, tk=512, name=f"{tag}_mm_dw_out")
    dy = _mm(dout, w_out, mode="nt", out_dtype=F32, tm=512, tn=1024, tk=1024, name=f"{tag}_mm_dy")
    dgp, do0, do1, o1, lse1, do2, o2, lse2 = _attn_combine_bwd(dy, o, lse, projs[0], name=f"{tag}_combine_bwd")
    parts = [_attn_bwd(projs[grp], grp, *dol, name=f"{tag}_attn_bwd{grp}")
             for grp, dol in enumerate(((do0, o, lse), (do1, o1, lse1), (do2, o2, lse2)))]
    dproj = _assemble_dproj(*parts, dgp, name=f"{tag}_assemble_dproj")
    dw_in = _mm(h, dproj, mode="tn", out_dtype=F32, tm=1024, tn=1024, tk=1024, name=f"{tag}_mm_dw_in")
    dx, dscale, dshift = _mm_dh(dproj, w_in, dx_part, x, scale, tm=512, tk=2048, name=f"{tag}_mm_dh")
    grads = dict(w_in=dw_in, w_out=dw_out, ln_g=dln_g, ln_b=dln_b, mod=jnp.concatenate([dshift, dscale, dgate], axis=1))
    return dx, grads


def _ssd_param_views(dt_raw, dt_bias, a_log):
    s = dt_raw.shape[0]
    r4 = dt_raw.reshape(s, 2, SSM_GROUPS, HPG)
    dtr = r4.transpose(1, 2, 0, 3)
    dtt = r4.transpose(1, 2, 3, 0)
    a = -jnp.exp(a_log)
    pr_rows = jnp.stack([dt_bias.reshape(2, SSM_GROUPS, HPG), a.reshape(2, SSM_GROUPS, HPG)], axis=2)
    return dtr, dtt, pr_rows, pr_rows.transpose(0, 1, 3, 2)


def _layer_b_fwd(x, mod, p, ln_g, ln_b, tag):
    shift, scale, gate = mod
    s = x.shape[0]
    h = _modulate(x, scale, shift, name=f"{tag}_modulate")
    proj = _mm(h, p["w_in"][:, :SSM_MAIN_COLS], mode="nn", out_dtype=BF16, tm=512, tn=1024, tk=1024, name=f"{tag}_mm_in")
    dt_raw = _mm(h, p["w_in"][:, SSM_MAIN_COLS:SSM_IN_COLS], mode="nn", out_dtype=F32, tm=512, tn=64, tk=1024,
                 name=f"{tag}_mm_dt")
    xpad = jnp.pad(proj[:, SSM_INNER:], ((CONV_HALO, CONV_HALO), (0, 0)))
    xbc = _conv_fwd(xpad, p["conv_w"], p["conv_b"], name=f"{tag}_conv")
    views = (xbc[:, SSM_INNER:SSM_INNER + SSM_GROUPS * SSM_STATE].T, xbc[:, SSM_INNER + SSM_GROUPS * SSM_STATE:].T,
             *_ssd_param_views(dt_raw, p["dt_bias"], p["a_log"]))
    y2, hs = _ssd_fwd(xbc, *views, name=f"{tag}_ssd_fwd")
    u = _gate_norm_fwd(y2, xbc, proj, p["dvec"], p["norm_w"], name=f"{tag}_gate_norm")
    out = _mm(u, p["w_out"], mode="nn", out_dtype=F32, tm=512, tn=1024, tk=2048, name=f"{tag}_mm_out")
    xn = _resid_ln_fwd(x, out, gate, ln_g, ln_b, name=f"{tag}_resid_ln")
    return xn, (x, h, proj, xpad, xbc, views, y2, hs, u, out)


def _layer_b_bwd(dxn, saved, mod, p, ln_g, tag):
    x, h, proj, xpad, xbc, views, y2, hs, u, out = saved
    shift, scale, gate = mod
    s = x.shape[0]
    dx_part, dout, dgate, dln_g, dln_b = _resid_ln_bwd(x, out, gate, ln_g, dxn, name=f"{tag}_resid_ln_bwd")
    dw_out = _mm(u, dout, mode="tn", out_dtype=F32, tm=1024, tn=1024, tk=512, name=f"{tag}_mm_dw_out")
    du = _mm(dout, p["w_out"], mode="nt", out_dtype=F32, tm=512, tn=1024, tk=1024, name=f"{tag}_mm_du")
    dy, dz, dnorm_w, dd_lanes = _gate_norm_bwd(du, y2, xbc, proj, p["dvec"], p["norm_w"], name=f"{tag}_gate_norm_bwd")
    dxs2, db2, dc2, ddt4, dalog, dbias = _ssd_bwd(xbc, *views, p["dvec"], hs, y2, dy, name=f"{tag}_ssd_bwd")
    dact = jnp.concatenate([dxs2[0] + dxs2[1], db2[0] + db2[1], dc2[0] + dc2[1]], axis=1)
    dapad = jnp.pad(dact, ((CONV_HALO, CONV_HALO), (0, 0)))
    dxbc, dconv_w, dconv_b = _conv_bwd(xpad, dapad, p["conv_w"], p["conv_b"], name=f"{tag}_conv_bwd")
    ddt_raw = ddt4.transpose(3, 0, 1, 2).reshape(s, 2 * SSM_HEADS).astype(BF16)
    dproj = jnp.concatenate([dz, dxbc, ddt_raw, jnp.zeros((s, SSM_PAD_COLS - SSM_IN_COLS), BF16)], axis=1)
    dw_in = _mm(h, dproj, mode="tn", out_dtype=F32, tm=1024, tn=896, tk=512, name=f"{tag}_mm_dw_in")[:, :SSM_IN_COLS]
    w_pad = jnp.pad(p["w_in"], ((0, 0), (0, SSM_PAD_COLS - SSM_IN_COLS)))
    dx, dscale, dshift = _mm_dh(dproj, w_pad, dx_part, x, scale, tm=512, tk=1792, name=f"{tag}_mm_dh")
    grads = dict(
        w_in=dw_in, w_out=dw_out, ln_g=dln_g, ln_b=dln_b, mod=jnp.concatenate([dshift, dscale, dgate], axis=1),
        conv_w=dconv_w, conv_b=dconv_b, norm_w=dnorm_w, dt_bias=dbias.reshape(2, SSM_HEADS),
        a_log=dalog.reshape(2, SSM_HEADS), d=jnp.sum(dd_lanes.reshape(SSM_HEADS, SSM_HEAD_DIM), axis=1))
    return dx, grads


def _local_step(x, target, mods, ln_g, ln_b, a_w_in, a_w_out, b_params):
    saved = []
    for i in range(DEPTH):
        j = i // 2
        g, b = ln_g[i:i + 1], ln_b[i:i + 1]
        if i % 2 == 0:
            x, sv = _layer_a_fwd(x, mods[i], a_w_in[j], a_w_out[j], g, b, f"l{i}")
        else:
            x, sv = _layer_b_fwd(x, mods[i], b_params[j], g, b, f"l{i}")
        saved.append(sv)
    dx, loss = _loss_and_grad(x, target, name="loss")
    grads = [None] * DEPTH
    for i in reversed(range(DEPTH)):
        j = i // 2
        g = ln_g[i:i + 1]
        if i % 2 == 0:
            dx, grads[i] = _layer_a_bwd(dx, saved[i], mods[i], a_w_in[j], a_w_out[j], g, f"l{i}")
        else:
            dx, grads[i] = _layer_b_bwd(dx, saved[i], mods[i], b_params[j], g, f"l{i}")
    return loss, dx, grads


def _mesh_pos():
    return lax.axis_index("x"), lax.axis_index("y"), lax.axis_index("c")


def _all_gather(x, *, name):
    def body(x_ref, out_ref, send_sems, recv_sems, local_sem):
        ax, ay, ac = _mesh_pos()
        me, sibling = (ax, ay, ac), (ax, ay, 1 - ac)
        chips = [(1 - ax, ay), (ax, 1 - ay), (1 - ax, 1 - ay)]

        def slot(px, py, pc):
            return out_ref.at[4 * px + 2 * py + pc]

        def copy(k, block, to, src=None):
            return pltpu.make_async_remote_copy(
                src_ref=slot(*block) if src is None else src, dst_ref=slot(*block),
                send_sem=send_sems.at[k], recv_sem=recv_sems.at[k], device_id=to, device_id_type=MESH)

        mine = pltpu.make_async_copy(x_ref, slot(*me), local_sem)
        mine.start()
        first = [copy(0, me, sibling, src=x_ref)]
        first += [copy(1 + j, me, (*chip, ac), src=x_ref) for j, chip in enumerate(chips)]
        for cp in first:
            cp.start()
        passed = [copy(4 + j, (*chip, ac), sibling) for j, chip in enumerate(chips)]
        for j, chip in enumerate(chips):
            copy(1 + j, (*chip, ac), me).wait_recv()
            passed[j].start()
        copy(0, sibling, me).wait_recv()
        for j, chip in enumerate(chips):
            copy(4 + j, (*chip, 1 - ac), me).wait_recv()
        for cp in first + passed:
            cp.wait_send()
        mine.wait()

    return pl.pallas_call(
        body, name=name, out_shape=jax.ShapeDtypeStruct((N_DEV,) + x.shape, x.dtype),
        in_specs=[pl.BlockSpec(memory_space=pl.ANY)], out_specs=pl.BlockSpec(memory_space=pl.ANY),
        scratch_shapes=[pltpu.SemaphoreType.DMA((7,)), pltpu.SemaphoreType.DMA((7,)), pltpu.SemaphoreType.DMA],
    )(x)


def _all_to_all(x, *, name):
    def body(x_ref, out_ref, send_sems, recv_sems, local_sem):
        ax, ay, ac = _mesh_pos()
        me = 4 * ax + 2 * ay + ac
        mine = pltpu.make_async_copy(x_ref.at[me], out_ref.at[me], local_sem)
        mine.start()
        copies = []
        for k in range(1, N_DEV):
            px = 1 - ax if k & 4 else ax
            py = 1 - ay if k & 2 else ay
            pc = 1 - ac if k & 1 else ac
            copies.append(pltpu.make_async_remote_copy(
                src_ref=x_ref.at[4 * px + 2 * py + pc], dst_ref=out_ref.at[me],
                send_sem=send_sems.at[k - 1], recv_sem=recv_sems.at[k - 1], device_id=(px, py, pc), device_id_type=MESH))
        for cp in copies:
            cp.start()
        for cp in copies:
            cp.wait()
        mine.wait()

    return pl.pallas_call(
        body, name=name, out_shape=jax.ShapeDtypeStruct(x.shape, x.dtype),
        in_specs=[pl.BlockSpec(memory_space=pl.ANY)], out_specs=pl.BlockSpec(memory_space=pl.ANY),
        scratch_shapes=[pltpu.SemaphoreType.DMA((7,)), pltpu.SemaphoreType.DMA((7,)), pltpu.SemaphoreType.DMA],
    )(x)


ADA_LOCAL = 3 * D_MODEL // N_DEV


def _ada_mod(c_all, ada_w, ada_b_local, *, name):
    def body(c_ref, w_ref, b_ref, o_ref):
        cond = _silu_and_grad(c_ref[...])[0]
        o_ref[...] = jnp.dot(cond, w_ref[...], precision=HIGHEST, preferred_element_type=F32) + b_ref[...]

    return pl.pallas_call(
        body, name=name, grid=(DEPTH,),
        in_specs=[pl.BlockSpec((N_DEV, D_MODEL), lambda i: (0, 0)), pl.BlockSpec((None, D_MODEL, ADA_LOCAL), lambda i: (i, 0, 0)),
                  pl.BlockSpec((None, 1, ADA_LOCAL), lambda i: (i, 0, 0))],
        out_specs=pl.BlockSpec((None, N_DEV, ADA_LOCAL), lambda i: (i, 0, 0)),
        out_shape=jax.ShapeDtypeStruct((DEPTH, N_DEV, ADA_LOCAL), F32), compiler_params=_params("parallel"),
    )(c_all, ada_w, ada_b_local)


def _ada_grad(c_all_t, dmod_local, *, name):
    def body(ct_ref, dm_ref, o_ref):
        cond_t = _silu_and_grad(ct_ref[...])[0]
        dm = dm_ref[...]
        acc = cond_t[:, 0:1] * dm[0:1, :]
        for smp in range(1, N_DEV):
            acc = acc + cond_t[:, smp:smp + 1] * dm[smp:smp + 1, :]
        o_ref[...] = acc

    return pl.pallas_call(
        body, name=name, grid=(DEPTH,),
        in_specs=[pl.BlockSpec((D_MODEL, N_DEV), lambda i: (0, 0)), pl.BlockSpec((None, N_DEV, ADA_LOCAL), lambda i: (i, 0, 0))],
        out_specs=pl.BlockSpec((None, D_MODEL, ADA_LOCAL), lambda i: (i, 0, 0)),
        out_shape=jax.ShapeDtypeStruct((DEPTH, D_MODEL, ADA_LOCAL), F32), compiler_params=_params("parallel"),
    )(c_all_t, dmod_local)


def _sum_devices(parts, *, name):
    n = parts.shape[1]

    def body(p_ref, o_ref):
        acc = p_ref[0:1, :]
        for dev in range(1, N_DEV):
            acc = acc + p_ref[dev:dev + 1, :]
        o_ref[...] = acc

    return pl.pallas_call(
        body, name=name, out_shape=jax.ShapeDtypeStruct((1, n), F32),
        in_specs=[pl.BlockSpec(memory_space=pltpu.VMEM)], out_specs=pl.BlockSpec(memory_space=pltpu.VMEM),
        compiler_params=pltpu.CompilerParams(vmem_limit_bytes=VMEM_LIMIT_BYTES),
    )(parts)


ADAMW_VMEM_BYTES = 24 * 1024 * 1024


def _adamw(w, m, v, g, *, name):
    r, c = w.shape
    summed = g.ndim == 3
    tr = r
    arrays = 7 + (N_DEV if summed else 1)
    while tr % 16 == 0 and 2 * arrays * tr * c * 4 > ADAMW_VMEM_BYTES:
        tr //= 2

    def body(w_ref, m_ref, v_ref, g_ref, go_ref, d_ref, mo_ref, vo_ref):
        if summed:
            g = g_ref[0].astype(F32)
            for dev in range(1, N_DEV):
                g = g + g_ref[dev].astype(F32)
        else:
            g = g_ref[...]
        mn = ADAM_B1 * m_ref[...] + (1.0 - ADAM_B1) * g
        vn = ADAM_B2 * v_ref[...] + (1.0 - ADAM_B2) * (g * g)
        m_hat = mn / (1.0 - ADAM_B1 ** ADAM_STEP)
        v_hat = vn / (1.0 - ADAM_B2 ** ADAM_STEP)
        go_ref[...] = g
        d_ref[...] = -ADAM_LR * (m_hat / (jnp.sqrt(v_hat) + ADAM_EPS) + ADAM_WD * w_ref[...])
        mo_ref[...] = mn
        vo_ref[...] = vn

    row = pl.BlockSpec((tr, c), lambda i: (i, 0))
    gspec = pl.BlockSpec((N_DEV, tr, c), lambda i: (0, i, 0)) if summed else row
    shp = jax.ShapeDtypeStruct((r, c), F32)
    return pl.pallas_call(
        body, name=name, grid=(r // tr,), in_specs=[row, row, row, gspec], out_specs=[row] * 4, out_shape=[shp] * 4,
        compiler_params=_params("parallel"),
    )(w, m, v, g)


def _pack(arrays):
    flat = jnp.concatenate([a.reshape(-1) for a in arrays])
    n = flat.shape[0]
    return jnp.pad(flat, (0, -n % 128)).reshape(1, -1)


def _unpack(vec, shapes):
    out, at = [], 0
    for shp in shapes:
        n = math.prod(shp)
        out.append(vec[at:at + n].reshape(shp))
        at += n
    return out


def _unpack_rows(rows, shapes):
    out, at = [], 0
    for shp in shapes:
        n = math.prod(shp)
        out.append(rows[:, at:at + n].reshape((rows.shape[0],) + tuple(shp)))
        at += n
    return out


def _my_shard(full, me, axis):
    width = full.shape[axis] // N_DEV
    return lax.dynamic_slice_in_dim(full, me * width, width, axis)


def _gather_cols(g, lead):
    nd = g.ndim
    perm = tuple(range(1, nd - 1)) + (0, nd - 1)
    t = g.transpose(perm)
    return t.reshape(t.shape[:-2] + (t.shape[-2] * t.shape[-1],))


def kernel(x, c, ada_w, ada_b, ln_g, ln_b, a_w_in, a_w_out, b_w_in, b_conv_w, b_conv_b, b_dt_bias, b_a_log, b_d, b_norm_w, b_w_out, loss_target, m_ada_w, m_ada_b, m_ln_g, m_ln_b, m_a_w_in, m_a_w_out, m_b_w_in, m_b_conv_w, m_b_conv_b, m_b_dt_bias, m_b_a_log, m_b_d, m_b_norm_w, m_b_w_out, v_ada_w, v_ada_b, v_ln_g, v_ln_b, v_a_w_in, v_a_w_out, v_b_w_in, v_b_conv_w, v_b_conv_b, v_b_dt_bias, v_b_a_log, v_b_d, v_b_norm_w, v_b_w_out):
    ax, ay, ac = _mesh_pos()
    me = 4 * ax + 2 * ay + ac
    seq = x.shape[1]

    small_shapes = [(1, D_MODEL), (2, SSM_CONV, ADA_LOCAL), (2, ADA_LOCAL), (2, SSM_INNER // N_DEV)]
    sg = _all_gather(_pack([c, b_conv_w, b_conv_b, b_norm_w]), name="gather_small")[:, 0, :]
    c_all, conv_w_g, conv_b_g, norm_w_g = _unpack_rows(sg, small_shapes)
    c_all = c_all[:, 0, :]
    conv_w = _gather_cols(conv_w_g, 2)
    conv_b = _gather_cols(conv_b_g[:, :, None, :], 2)
    norm_w = _gather_cols(norm_w_g[:, :, None, :], 2)

    def gathered(w, name):
        lead = w.shape[0]
        flat = w.astype(BF16).reshape(lead * w.shape[1], w.shape[2])
        return _all_gather(flat, name=name).reshape(N_DEV, lead, w.shape[1], w.shape[2])

    a_w_in_f = _gather_cols(gathered(a_w_in, "gather_a_w_in"), 2)
    b_w_in_f = _gather_cols(gathered(b_w_in, "gather_b_w_in"), 2)
    a_w_out_f = gathered(a_w_out, "gather_a_w_out").transpose(1, 0, 2, 3).reshape(2, A_WIDTH, D_MODEL)
    b_w_out_f = gathered(b_w_out, "gather_b_w_out").transpose(1, 0, 2, 3).reshape(2, SSM_INNER, D_MODEL)

    ada_b_local = _my_shard(ada_b, me, 1)[:, None, :]
    mod_cols = _ada_mod(c_all, ada_w, ada_b_local, name="ada_mod")
    mod_g = _all_gather(mod_cols.reshape(1, -1), name="gather_mod").reshape(N_DEV, DEPTH, N_DEV, ADA_LOCAL)
    mod = lax.dynamic_index_in_dim(mod_g, me, axis=2, keepdims=False).transpose(1, 0, 2).reshape(DEPTH, 3 * D_MODEL)
    mods = [tuple(mod[i:i + 1, k * D_MODEL:(k + 1) * D_MODEL] for k in range(3)) for i in range(DEPTH)]

    b_params = [dict(w_in=b_w_in_f[j], w_out=b_w_out_f[j], conv_w=conv_w[j], conv_b=conv_b[j], norm_w=norm_w[j],
                     dt_bias=b_dt_bias[j], a_log=b_a_log[j], dvec=jnp.repeat(b_d[j], SSM_HEAD_DIM)[None, :])
                for j in range(2)]
    loss_lanes, dx, grads = _local_step(x[0], loss_target[0], mods, ln_g, ln_b, a_w_in_f, a_w_out_f, b_params)
    loss = lax.psum(loss_lanes[0, 0], ("x", "y", "c"))
    grad_x = dx[None]

    a_layers, b_layers = (grads[0], grads[2]), (grads[1], grads[3])
    part_shapes = [(DEPTH, 3 * D_MODEL), (DEPTH, D_MODEL), (DEPTH, D_MODEL), (2, SSM_CONV, SSM_CONV_DIM),
                   (2, SSM_CONV_DIM), (2, SSM_INNER), (2, 2, SSM_HEADS), (2, 2, SSM_HEADS), (2, SSM_HEADS)]
    parts = _pack([
        jnp.concatenate([g["mod"] for g in grads]), jnp.concatenate([g["ln_g"] for g in grads]),
        jnp.concatenate([g["ln_b"] for g in grads]), jnp.stack([g["conv_w"] for g in b_layers]),
        jnp.stack([g["conv_b"][0] for g in b_layers]), jnp.stack([g["norm_w"][0] for g in b_layers]),
        jnp.stack([g["dt_bias"] for g in b_layers]), jnp.stack([g["a_log"] for g in b_layers]),
        jnp.stack([g["d"] for g in b_layers])])
    parts_g = _all_gather(parts, name="gather_small_grads")[:, 0, :]
    (g_ada_b, g_ln_g, g_ln_b, g_conv_w, g_conv_b, g_norm_w, g_dt_bias, g_a_log, g_d) = _unpack(
        _sum_devices(parts_g, name="sum_small_grads")[0], part_shapes)
    dmod_all = parts_g[:, :DEPTH * 3 * D_MODEL].reshape(N_DEV, DEPTH, N_DEV, ADA_LOCAL)
    dmod_local = lax.dynamic_index_in_dim(dmod_all, me, axis=2, keepdims=False).transpose(1, 0, 2)
    g_ada_w = _ada_grad(c_all.T, dmod_local, name="ada_grad")

    def scatter_cols(per_layer, name):
        st = jnp.stack(per_layer)
        lead, r, cols = st.shape
        blocks = st.reshape(lead, r, N_DEV, cols // N_DEV).transpose(2, 0, 1, 3).reshape(N_DEV, lead * r, cols // N_DEV)
        return _all_to_all(blocks.astype(BF16), name=name)

    def scatter_rows(per_layer, name):
        st = jnp.stack(per_layer)
        lead, rows, cols = st.shape
        blocks = st.reshape(lead, N_DEV, rows // N_DEV, cols).transpose(1, 0, 2, 3).reshape(N_DEV, lead * rows // N_DEV, cols)
        return _all_to_all(blocks.astype(BF16), name=name)

    r_a_w_in = scatter_cols([g["w_in"] for g in a_layers], "scatter_a_w_in")
    r_a_w_out = scatter_rows([g["w_out"] for g in a_layers], "scatter_a_w_out")
    r_b_w_in = scatter_cols([g["w_in"] for g in b_layers], "scatter_b_w_in")
    r_b_w_out = scatter_rows([g["w_out"] for g in b_layers], "scatter_b_w_out")

    def update(w, m, v, g, name):
        two_d = (-1, w.shape[-1])
        outs = _adamw(w.reshape(two_d), m.reshape(two_d), v.reshape(two_d), g, name=name)
        return [o.reshape(w.shape) for o in outs]

    up_ada_w = update(ada_w, m_ada_w, v_ada_w, g_ada_w.reshape(-1, ADA_LOCAL), "adamw_ada_w")
    up_a_w_in = update(a_w_in, m_a_w_in, v_a_w_in, r_a_w_in, "adamw_a_w_in")
    up_a_w_out = update(a_w_out, m_a_w_out, v_a_w_out, r_a_w_out, "adamw_a_w_out")
    up_b_w_in = update(b_w_in, m_b_w_in, v_b_w_in, r_b_w_in, "adamw_b_w_in")
    up_b_w_out = update(b_w_out, m_b_w_out, v_b_w_out, r_b_w_out, "adamw_b_w_out")

    small_w = [ada_b, ln_g, ln_b, b_conv_w, b_conv_b, b_dt_bias, b_a_log, b_d, b_norm_w]
    small_m = [m_ada_b, m_ln_g, m_ln_b, m_b_conv_w, m_b_conv_b, m_b_dt_bias, m_b_a_log, m_b_d, m_b_norm_w]
    small_v = [v_ada_b, v_ln_g, v_ln_b, v_b_conv_w, v_b_conv_b, v_b_dt_bias, v_b_a_log, v_b_d, v_b_norm_w]
    small_g = [g_ada_b, g_ln_g, g_ln_b, _my_shard(g_conv_w, me, 2), _my_shard(g_conv_b, me, 1), g_dt_bias, g_a_log, g_d,
               _my_shard(g_norm_w, me, 1)]
    shapes = [w.shape for w in small_w]
    packed = _adamw(_pack(small_w), _pack(small_m), _pack(small_v), _pack(small_g), name="adamw_small")
    (up_ada_b, up_ln_g, up_ln_b, up_conv_w, up_conv_b, up_dt_bias, up_a_log, up_d, up_norm_w) = zip(
        *[_unpack(p[0], shapes) for p in packed])

    ordered = [up_ada_w, up_ada_b, up_ln_g, up_ln_b, up_a_w_in, up_a_w_out, up_b_w_in, up_conv_w, up_conv_b,
               up_dt_bias, up_a_log, up_d, up_norm_w, up_b_w_out]
    return (loss, grad_x, *[u[0] for u in ordered], *[u[1] for u in ordered], *[u[2] for u in ordered],
            *[u[3] for u in ordered])
```

```python
import functools
import math

import jax
import jax.numpy as jnp
import numpy as np
from jax import lax
from jax.experimental import pallas as pl
from jax.experimental.pallas import tpu as pltpu

F32 = jnp.float32
BF16 = jnp.bfloat16
HIGHEST = lax.Precision.HIGHEST
MESH = pl.DeviceIdType.MESH

D_MODEL = 1024
DEPTH = 4
A_HEADS = 16
A_HEAD_DIM = 64
A_WIDTH = 1024
DILATIONS = (1, 4, 16)
A_RADIUS = 64
A_QBLOCK = 128
A_IN_COLS = 10240
SSM_INNER = 2048
SSM_HEADS = 32
SSM_HEAD_DIM = 64
SSM_STATE = 128
SSM_GROUPS = 4
SSM_CHUNK = 128
SSM_CONV = 5
SSM_CONV_DIM = 3072
SSM_IN_COLS = 5184
SSM_MAIN_COLS = 5120
SSM_PAD_COLS = 5376
CONV_HALO = 16
ALPHA = (2 * DEPTH) ** 0.25
LN_EPS = 1e-5
RMS_EPS = 1e-5
ADAM_LR, ADAM_B1, ADAM_B2, ADAM_EPS, ADAM_WD, ADAM_STEP = 0.001, 0.9, 0.999, 1e-08, 0.01, 10
N_DEV = 8
VMEM_LIMIT_BYTES = 56 * 1024 * 1024
NEG_BIG = -1e30


def _params(*sem):
    return pltpu.CompilerParams(dimension_semantics=sem, vmem_limit_bytes=VMEM_LIMIT_BYTES)


def _sigmoid(x):
    return 1.0 / (1.0 + jnp.exp(-x))


def _silu_and_grad(x):
    sg = _sigmoid(x)
    return x * sg, sg * (1.0 + x * (1.0 - sg))


def _softplus(x):
    e = jnp.exp(-jnp.abs(x))
    u = 1.0 + e
    log1p = jnp.where(u == 1.0, e, jnp.log(u) * (e / jnp.where(u == 1.0, 1.0, u - 1.0)))
    return jnp.maximum(x, 0.0) + log1p


_DIMS = {"nn": (((1,), (0,)), ((), ())), "nt": (((1,), (1,)), ((), ())), "tn": (((0,), (0,)), ((), ()))}


def _mm(a, b, *, mode, out_dtype, tm, tn, tk, name):
    if mode == "nn":
        (m, k), (_, n) = a.shape, b.shape
    elif mode == "nt":
        (m, k), (n, _) = a.shape, b.shape
    else:
        (k, m), (_, n) = a.shape, b.shape
    tm, tn, tk = min(tm, m), min(tn, n), min(tk, k)
    assert m % tm == 0 and n % tn == 0 and k % tk == 0, (name, a.shape, b.shape)
    nk = k // tk
    dims = _DIMS[mode]

    def body(a_ref, b_ref, o_ref, *scratch):
        part = lax.dot_general(a_ref[...], b_ref[...], dims, preferred_element_type=F32)
        if nk == 1:
            o_ref[...] = part.astype(o_ref.dtype)
            return
        acc_ref, = scratch
        kk = pl.program_id(2)

        @pl.when(kk == 0)
        def _():
            acc_ref[...] = part

        @pl.when(kk > 0)
        def _():
            acc_ref[...] += part

        @pl.when(kk == nk - 1)
        def _():
            o_ref[...] = acc_ref[...].astype(o_ref.dtype)

    if mode == "tn":
        a_spec = pl.BlockSpec((tk, tm), lambda i, j, kk: (kk, i))
    else:
        a_spec = pl.BlockSpec((tm, tk), lambda i, j, kk: (i, kk))
    if mode == "nt":
        b_spec = pl.BlockSpec((tn, tk), lambda i, j, kk: (j, kk))
    else:
        b_spec = pl.BlockSpec((tk, tn), lambda i, j, kk: (kk, j))
    return pl.pallas_call(
        body, name=name, grid=(m // tm, n // tn, nk),
        in_specs=[a_spec, b_spec], out_specs=pl.BlockSpec((tm, tn), lambda i, j, kk: (i, j)),
        out_shape=jax.ShapeDtypeStruct((m, n), out_dtype),
        scratch_shapes=[] if nk == 1 else [pltpu.VMEM((tm, tn), F32)],
        compiler_params=_params("parallel", "parallel", "arbitrary"),
    )(a, b)


def _mm_dh(dproj, w_t, dx_part, x, scale, *, tm, tk, name):
    s, k = dproj.shape
    d = w_t.shape[1]
    tk = min(tk, k)
    assert s % tm == 0 and k % tk == 0
    nk = k // tk

    def body(a_ref, w_ref, dxp_ref, x_ref, sc_ref, dx_ref, dsc_ref, dsh_ref, acc_ref):
        i, kk = pl.program_id(0), pl.program_id(1)
        part = jnp.dot(a_ref[...], w_ref[...], preferred_element_type=F32)

        @pl.when(kk == 0)
        def _():
            acc_ref[...] = part

        @pl.when(kk > 0)
        def _():
            acc_ref[...] += part

        @pl.when(jnp.logical_and(i == 0, kk == 0))
        def _():
            dsc_ref[...] = jnp.zeros_like(dsc_ref)
            dsh_ref[...] = jnp.zeros_like(dsh_ref)

        @pl.when(kk == nk - 1)
        def _():
            dh = acc_ref[...]
            dx_ref[...] = dxp_ref[...] + dh * (1.0 + sc_ref[...])
            dsc_ref[...] += jnp.sum(dh * x_ref[...], axis=0, keepdims=True)
            dsh_ref[...] += jnp.sum(dh, axis=0, keepdims=True)

    row = pl.BlockSpec((tm, d), lambda i, kk: (i, 0))
    vec = pl.BlockSpec((1, d), lambda i, kk: (0, 0))
    return pl.pallas_call(
        body, name=name, grid=(s // tm, nk),
        in_specs=[pl.BlockSpec((tm, tk), lambda i, kk: (i, kk)), pl.BlockSpec((tk, d), lambda i, kk: (kk, 0)),
                  row, row, vec],
        out_specs=[row, vec, vec],
        out_shape=[jax.ShapeDtypeStruct((s, d), F32), jax.ShapeDtypeStruct((1, d), F32),
                   jax.ShapeDtypeStruct((1, d), F32)],
        scratch_shapes=[pltpu.VMEM((tm, d), F32)],
        compiler_params=_params("arbitrary", "arbitrary"),
    )(dproj, w_t, dx_part, x, scale)


ROW_TILE = 512


def _modulate(x, scale, shift, *, name):
    s, d = x.shape

    def body(x_ref, sc_ref, sh_ref, h_ref):
        h_ref[...] = (x_ref[...] * (1.0 + sc_ref[...]) + sh_ref[...]).astype(BF16)

    row = pl.BlockSpec((ROW_TILE, d), lambda i: (i, 0))
    vec = pl.BlockSpec((1, d), lambda i: (0, 0))
    return pl.pallas_call(
        body, name=name, grid=(s // ROW_TILE,), in_specs=[row, vec, vec], out_specs=row,
        out_shape=jax.ShapeDtypeStruct((s, d), BF16), compiler_params=_params("parallel"),
    )(x, scale, shift)


def _resid_ln_fwd(x, out, gate, g, b, *, name):
    s, d = x.shape

    def body(x_ref, o_ref, gate_ref, g_ref, b_ref, y_ref):
        r = ALPHA * x_ref[...] + gate_ref[...] * o_ref[...]
        mu = jnp.mean(r, axis=-1, keepdims=True)
        rc = r - mu
        var = jnp.mean(rc * rc, axis=-1, keepdims=True)
        y_ref[...] = rc * lax.rsqrt(var + LN_EPS) * g_ref[...] + b_ref[...]

    row = pl.BlockSpec((ROW_TILE, d), lambda i: (i, 0))
    vec = pl.BlockSpec((1, d), lambda i: (0, 0))
    return pl.pallas_call(
        body, name=name, grid=(s // ROW_TILE,), in_specs=[row, row, vec, vec, vec], out_specs=row,
        out_shape=jax.ShapeDtypeStruct((s, d), F32), compiler_params=_params("parallel"),
    )(x, out, gate, g, b)


def _resid_ln_bwd(x, out, gate, g, dy, *, name):
    s, d = x.shape

    def body(x_ref, o_ref, gate_ref, g_ref, dy_ref, dxp_ref, dout_ref, dgate_ref, dg_ref, db_ref):
        @pl.when(pl.program_id(0) == 0)
        def _():
            dgate_ref[...] = jnp.zeros_like(dgate_ref)
            dg_ref[...] = jnp.zeros_like(dg_ref)
            db_ref[...] = jnp.zeros_like(db_ref)

        o = o_ref[...]
        r = ALPHA * x_ref[...] + gate_ref[...] * o
        mu = jnp.mean(r, axis=-1, keepdims=True)
        rc = r - mu
        var = jnp.mean(rc * rc, axis=-1, keepdims=True)
        rstd = lax.rsqrt(var + LN_EPS)
        xhat = rc * rstd
        dy = dy_ref[...]
        dxh = dy * g_ref[...]
        dr = rstd * (dxh - jnp.mean(dxh, axis=-1, keepdims=True) - xhat * jnp.mean(dxh * xhat, axis=-1, keepdims=True))
        dxp_ref[...] = ALPHA * dr
        dout_ref[...] = (gate_ref[...] * dr).astype(BF16)
        dgate_ref[...] += jnp.sum(dr * o, axis=0, keepdims=True)
        dg_ref[...] += jnp.sum(dy * xhat, axis=0, keepdims=True)
        db_ref[...] += jnp.sum(dy, axis=0, keepdims=True)

    row = pl.BlockSpec((ROW_TILE, d), lambda i: (i, 0))
    vec = pl.BlockSpec((1, d), lambda i: (0, 0))
    vshape = jax.ShapeDtypeStruct((1, d), F32)
    return pl.pallas_call(
        body, name=name, grid=(s // ROW_TILE,), in_specs=[row, row, vec, vec, row],
        out_specs=[row, row, vec, vec, vec],
        out_shape=[jax.ShapeDtypeStruct((s, d), F32), jax.ShapeDtypeStruct((s, d), BF16), vshape, vshape, vshape],
        compiler_params=_params("arbitrary"),
    )(x, out, gate, g, dy)


def _loss_and_grad(y, target, *, name):
    s, d = y.shape

    def body(y_ref, t_ref, dy_ref, loss_ref):
        @pl.when(pl.program_id(0) == 0)
        def _():
            loss_ref[...] = jnp.zeros_like(loss_ref)

        e = y_ref[...] - t_ref[...]
        dy_ref[...] = e * (1.0 / d)
        loss_ref[...] += jnp.sum(jnp.sum(e * e, axis=0, keepdims=True), axis=1, keepdims=True) * (0.5 / d)

    row = pl.BlockSpec((ROW_TILE, d), lambda i: (i, 0))
    return pl.pallas_call(
        body, name=name, grid=(s // ROW_TILE,), in_specs=[row, row],
        out_specs=[row, pl.BlockSpec((1, 128), lambda i: (0, 0))],
        out_shape=[jax.ShapeDtypeStruct((s, d), F32), jax.ShapeDtypeStruct((1, 128), F32)],
        compiler_params=_params("arbitrary"),
    )(y, target)


_SLOPES = np.asarray(2.0 ** (-8.0 * (np.arange(A_HEADS, dtype=np.float32) + 1.0) / A_HEADS), dtype=np.float32)


def _attn_scores(q, kw, slope, dist, valid):
    s = lax.dot_general(q, kw, _DIMS["nt"], preferred_element_type=F32) * (1.0 / math.sqrt(A_HEAD_DIM))
    return jnp.where(valid, s - slope * dist, NEG_BIG)


def _attn_window(blk, length, win, dil):
    start = pl.multiple_of(jnp.clip(blk * A_QBLOCK - A_RADIUS, 0, length - win), A_RADIUS)
    qpos = blk * A_QBLOCK + lax.broadcasted_iota(jnp.int32, (A_QBLOCK, win), 0)
    kpos = start + lax.broadcasted_iota(jnp.int32, (A_QBLOCK, win), 1)
    delta = jnp.abs(kpos - qpos)
    return start, (delta * dil).astype(F32), delta <= A_RADIUS


A_BLOCKS_PER_STEP = 4
A_GROUP_COLS = 3 * A_WIDTH


def _tile_scratch(rows, width):
    return pltpu.VMEM((width // 128, rows, 128), F32)


def _put_tile(scr, val):
    for j in range(scr.shape[0]):
        scr[j] = val[:, j * 128:(j + 1) * 128]


def _get_tile(scr):
    return jnp.concatenate([scr[j] for j in range(scr.shape[0])], axis=1)


def _get_residue(scr, r, dil):
    rows = pl.ds(r, scr.shape[1] // dil, stride=dil)
    return jnp.concatenate([scr.at[j][rows, :] for j in range(scr.shape[0])], axis=1)


def _put_residue(scr, r, dil, val):
    rows = pl.ds(r, scr.shape[1] // dil, stride=dil)
    for j in range(scr.shape[0]):
        scr.at[j][rows, :] = val[:, j * 128:(j + 1) * 128]


def _mm_dilated(a, b, dil, *, name):
    m, k = a.shape
    n = b.shape[1]
    tm = 512

    def body(a_ref, b_ref, o_ref, acc_ref):
        _put_tile(acc_ref, jnp.dot(a_ref[...], b_ref[...], preferred_element_type=F32))
        for r in range(dil):
            o_ref[:, r * n:(r + 1) * n] = _get_residue(acc_ref, r, dil).astype(BF16)

    return pl.pallas_call(
        body, name=name, grid=(m // tm,),
        in_specs=[pl.BlockSpec((tm, k), lambda i: (i, 0)), pl.BlockSpec((k, n), lambda i: (0, 0))],
        out_specs=pl.BlockSpec((tm // dil, dil * n), lambda i: (i, 0)),
        out_shape=jax.ShapeDtypeStruct((m // dil, dil * n), BF16), scratch_shapes=[_tile_scratch(tm, n)],
        compiler_params=_params("parallel"),
    )(a, b)


def _attn_fwd(pv, group, *, name):
    dil = DILATIONS[group]
    length = pv.shape[0]
    cb, qoff = pv.shape[1] // (128 * dil), 0
    win = min(2 * A_QBLOCK, length)
    nblk = length // A_QBLOCK
    per = A_BLOCKS_PER_STEP if nblk % A_BLOCKS_PER_STEP == 0 else 1

    def body(slope_ref, q_ref, k_ref, v_ref, o_ref, lse_ref):
        hp = pl.program_id(1)
        for u in range(per):
            rows = slice(u * A_QBLOCK, (u + 1) * A_QBLOCK)
            start, dist, valid = _attn_window(pl.program_id(2) * per + u, length, win, dil)
            kw = k_ref[pl.ds(start, win), :]
            vw = v_ref[pl.ds(start, win), :]
            q = q_ref[rows, :]
            outs, lses = [], []
            for hh in range(2):
                sl = slice(hh * A_HEAD_DIM, (hh + 1) * A_HEAD_DIM)
                sc = _attn_scores(q[:, sl], kw[:, sl], slope_ref[hp * 2 + hh], dist, valid)
                m = jnp.max(sc, axis=-1, keepdims=True)
                p = jnp.exp(sc - m)
                z = jnp.sum(p, axis=-1, keepdims=True)
                o = jnp.dot(p.astype(BF16), vw[:, sl], preferred_element_type=F32) / z
                outs.append(o)
                lses.append(jnp.broadcast_to(m + jnp.log(z), (A_QBLOCK, A_HEAD_DIM)))
            o_ref[rows, :] = jnp.concatenate(outs, axis=1)
            lse_ref[rows, :] = jnp.concatenate(lses, axis=1)

    qspec = pl.BlockSpec((per * A_QBLOCK, 128), lambda r, hp, b: (b, r * cb + qoff + hp))
    kspec = pl.BlockSpec((length, 128), lambda r, hp, b: (0, r * cb + qoff + 8 + hp))
    vspec = pl.BlockSpec((length, 128), lambda r, hp, b: (0, r * cb + qoff + 16 + hp))
    ospec = pl.BlockSpec((per * A_QBLOCK, 128), lambda r, hp, b: (b, r * 8 + hp))
    oshape = jax.ShapeDtypeStruct((length, dil * A_WIDTH), F32)
    o, lse = pl.pallas_call(
        body, name=name, grid=(dil, 8, nblk // per),
        in_specs=[pl.BlockSpec(memory_space=pltpu.SMEM), qspec, kspec, vspec], out_specs=[ospec, ospec],
        out_shape=[oshape, oshape], compiler_params=_params("parallel", "parallel", "arbitrary"),
    )(jnp.asarray(_SLOPES), pv, pv, pv)
    return o, lse


def _attn_bwd(pv, group, do, o, lse, *, name):
    dil = DILATIONS[group]
    length = pv.shape[0]
    cb, qoff = pv.shape[1] // (128 * dil), 0
    win = min(2 * A_QBLOCK, length)
    nblk = length // A_QBLOCK
    per = A_BLOCKS_PER_STEP if nblk % A_BLOCKS_PER_STEP == 0 else 1
    nstep = nblk // per
    scale = 1.0 / math.sqrt(A_HEAD_DIM)

    def body(slope_ref, q_ref, k_ref, v_ref, do_ref, o_ref, lse_ref, dq_ref, dk_ref, dv_ref, dk_acc, dv_acc):
        hp, step = pl.program_id(1), pl.program_id(2)

        @pl.when(step == 0)
        def _():
            dk_acc[...] = jnp.zeros_like(dk_acc)
            dv_acc[...] = jnp.zeros_like(dv_acc)

        for u in range(per):
            rows = slice(u * A_QBLOCK, (u + 1) * A_QBLOCK)
            start, dist, valid = _attn_window(step * per + u, length, win, dil)
            kw = k_ref[pl.ds(start, win), :]
            vw = v_ref[pl.ds(start, win), :]
            q = q_ref[rows, :]
            do_b = do_ref[rows, :]
            dsum = do_b.astype(F32) * o_ref[rows, :]
            lse_b = lse_ref[rows, :]
            dqs, dks, dvs = [], [], []
            for hh in range(2):
                sl = slice(hh * A_HEAD_DIM, (hh + 1) * A_HEAD_DIM)
                sc = _attn_scores(q[:, sl], kw[:, sl], slope_ref[hp * 2 + hh], dist, valid)
                p = jnp.exp(sc - lse_b[:, hh * A_HEAD_DIM:hh * A_HEAD_DIM + 1])
                dp = lax.dot_general(do_b[:, sl], vw[:, sl], _DIMS["nt"], preferred_element_type=F32)
                ds = (p * (dp - jnp.sum(dsum[:, sl], axis=-1, keepdims=True))).astype(BF16)
                dqs.append(jnp.dot(ds, kw[:, sl], preferred_element_type=F32) * scale)
                dks.append(lax.dot_general(ds, q[:, sl], _DIMS["tn"], preferred_element_type=F32) * scale)
                dvs.append(lax.dot_general(p.astype(BF16), do_b[:, sl], _DIMS["tn"], preferred_element_type=F32))
            dq_ref[rows, :] = jnp.concatenate(dqs, axis=1).astype(BF16)
            dk_acc[pl.ds(start, win), :] += jnp.concatenate(dks, axis=1)
            dv_acc[pl.ds(start, win), :] += jnp.concatenate(dvs, axis=1)

        @pl.when(step == nstep - 1)
        def _():
            dk_ref[...] = dk_acc[...].astype(BF16)
            dv_ref[...] = dv_acc[...].astype(BF16)

    qspec = pl.BlockSpec((per * A_QBLOCK, 128), lambda r, hp, b: (b, r * cb + qoff + hp))
    kspec = pl.BlockSpec((length, 128), lambda r, hp, b: (0, r * cb + qoff + 8 + hp))
    vspec = pl.BlockSpec((length, 128), lambda r, hp, b: (0, r * cb + qoff + 16 + hp))
    bspec = pl.BlockSpec((per * A_QBLOCK, 128), lambda r, hp, b: (b, r * 8 + hp))
    fspec = pl.BlockSpec((length, 128), lambda r, hp, b: (0, r * 8 + hp))
    oshape = jax.ShapeDtypeStruct((length, dil * A_WIDTH), BF16)
    dq, dk, dv = pl.pallas_call(
        body, name=name, grid=(dil, 8, nstep),
        in_specs=[pl.BlockSpec(memory_space=pltpu.SMEM), qspec, kspec, vspec, bspec, bspec, bspec],
        out_specs=[bspec, fspec, fspec], out_shape=[oshape, oshape, oshape],
        scratch_shapes=[pltpu.VMEM((length, 128), F32), pltpu.VMEM((length, 128), F32)],
        compiler_params=_params("parallel", "parallel", "arbitrary"),
    )(jnp.asarray(_SLOPES), pv, pv, pv, do, o, lse)
    return dq, dk, dv


A_GATE_BLOCK = 3
A_ROWS = 256


def _lanes_of(r):
    return slice(r * A_WIDTH, (r + 1) * A_WIDTH)


def _dilated_spec(dil):
    return pl.BlockSpec((A_ROWS // dil, dil * A_WIDTH), lambda i: (i, 0))


def _attn_combine(o0, l0, o1, l1, o2, l2, proj0, *, name):
    s = proj0.shape[0]

    def body(o0_ref, l0_ref, o1_ref, l1_ref, o2_ref, l2_ref, gate_ref, y_ref, o_ref, lse_ref, so1, sl1, so2, sl2):
        for src, dst, dil in ((o1_ref, so1, DILATIONS[1]), (l1_ref, sl1, DILATIONS[1]),
                              (o2_ref, so2, DILATIONS[2]), (l2_ref, sl2, DILATIONS[2])):
            for r in range(dil):
                _put_residue(dst, r, dil, src[:, _lanes_of(r)])
        la, lb, lc = l0_ref[...], _get_tile(sl1), _get_tile(sl2)
        m = jnp.maximum(jnp.maximum(la, lb), lc)
        ea, eb, ec = jnp.exp(la - m), jnp.exp(lb - m), jnp.exp(lc - m)
        den = ea + eb + ec
        o = (ea * o0_ref[...] + eb * _get_tile(so1) + ec * _get_tile(so2)) / den
        o_ref[...] = o
        lse_ref[...] = m + jnp.log(den)
        y_ref[...] = (o * _silu_and_grad(gate_ref[...].astype(F32))[0]).astype(BF16)

    row = pl.BlockSpec((A_ROWS, A_WIDTH), lambda i: (i, 0))
    gspec = pl.BlockSpec((A_ROWS, A_WIDTH), lambda i: (i, A_GATE_BLOCK))
    d1, d2 = _dilated_spec(DILATIONS[1]), _dilated_spec(DILATIONS[2])
    return pl.pallas_call(
        body, name=name, grid=(s // A_ROWS,), in_specs=[row, row, d1, d1, d2, d2, gspec], out_specs=[row, row, row],
        out_shape=[jax.ShapeDtypeStruct((s, A_WIDTH), BF16), jax.ShapeDtypeStruct((s, A_WIDTH), F32),
                   jax.ShapeDtypeStruct((s, A_WIDTH), F32)],
        scratch_shapes=[_tile_scratch(A_ROWS, A_WIDTH)] * 4, compiler_params=_params("parallel"),
    )(o0, l0, o1, l1, o2, l2, proj0)


def _attn_combine_bwd(dy, o, lse, proj0, *, name):
    s = proj0.shape[0]

    def body(dy_ref, o_ref, lse_ref, gate_ref, dg_ref, do_ref, do1, o1, l1, do2, o2, l2, s_do, s_o, s_l):
        si, dsi = _silu_and_grad(gate_ref[...].astype(F32))
        dyv = dy_ref[...]
        ov = o_ref[...]
        do = dyv * si
        _put_tile(s_do, do)
        _put_tile(s_o, ov)
        _put_tile(s_l, lse_ref[...])
        do_ref[...] = do.astype(BF16)
        dg_ref[...] = (dyv * ov * dsi).astype(BF16)
        for (do_d, o_d, l_d), dil in (((do1, o1, l1), DILATIONS[1]), ((do2, o2, l2), DILATIONS[2])):
            for r in range(dil):
                do_d[:, _lanes_of(r)] = _get_residue(s_do, r, dil).astype(BF16)
                o_d[:, _lanes_of(r)] = _get_residue(s_o, r, dil)
                l_d[:, _lanes_of(r)] = _get_residue(s_l, r, dil)

    row = pl.BlockSpec((A_ROWS, A_WIDTH), lambda i: (i, 0))
    gspec = pl.BlockSpec((A_ROWS, A_WIDTH), lambda i: (i, A_GATE_BLOCK))
    shp = jax.ShapeDtypeStruct((s, A_WIDTH), BF16)
    dilated = lambda dil, dtype: jax.ShapeDtypeStruct((s // dil, dil * A_WIDTH), dtype)
    d1, d2 = _dilated_spec(DILATIONS[1]), _dilated_spec(DILATIONS[2])
    return pl.pallas_call(
        body, name=name, grid=(s // A_ROWS,), in_specs=[row, row, row, gspec],
        out_specs=[row, row, d1, d1, d1, d2, d2, d2],
        out_shape=[shp, shp, dilated(DILATIONS[1], BF16), dilated(DILATIONS[1], F32), dilated(DILATIONS[1], F32),
                   dilated(DILATIONS[2], BF16), dilated(DILATIONS[2], F32), dilated(DILATIONS[2], F32)],
        scratch_shapes=[_tile_scratch(A_ROWS, A_WIDTH)] * 3, compiler_params=_params("parallel"),
    )(dy, o, lse, proj0)


def _assemble_dproj(parts0, parts1, parts2, dgate, *, name):
    s = dgate.shape[0]

    def body(*refs):
        ins, out_ref, scr = refs[:10], refs[10], refs[11]
        for p in range(3):
            out_ref[:, _lanes_of(p)] = ins[p][...]
        for g, dil in ((1, DILATIONS[1]), (2, DILATIONS[2])):
            for p in range(3):
                src = ins[3 * g + p]
                for r in range(dil):
                    _put_residue(scr, r, dil, src[:, _lanes_of(r)].astype(F32))
                out_ref[:, _lanes_of(3 * g + p)] = _get_tile(scr).astype(BF16)
        out_ref[:, _lanes_of(9)] = ins[9][...]

    row = pl.BlockSpec((A_ROWS, A_WIDTH), lambda i: (i, 0))
    d1, d2 = _dilated_spec(DILATIONS[1]), _dilated_spec(DILATIONS[2])
    return pl.pallas_call(
        body, name=name, grid=(s // A_ROWS,), in_specs=[row] * 3 + [d1] * 3 + [d2] * 3 + [row],
        out_specs=pl.BlockSpec((A_ROWS, A_IN_COLS), lambda i: (i, 0)),
        out_shape=jax.ShapeDtypeStruct((s, A_IN_COLS), BF16),
        scratch_shapes=[_tile_scratch(A_ROWS, A_WIDTH)], compiler_params=_params("parallel"),
    )(*parts0, *parts1, *parts2, dgate)


CONV_TILE = 256
CONV_SUB = 4


def _conv_taps(xe, n):
    return [xe if j == 2 else pltpu.roll(xe, (2 - j) % n, 0) for j in range(SSM_CONV)]


def _conv_fwd(xpad, w, b, *, name):
    s = xpad.shape[0] - 2 * CONV_HALO
    n = CONV_TILE + 2 * CONV_HALO
    ncol = SSM_CONV_DIM // 128

    sub = min(CONV_SUB, s // CONV_TILE)

    def body(x_ref, w_ref, b_ref, o_ref):
        base = pl.program_id(1) * (sub * CONV_TILE)

        def tile(k, carry):
            r0 = pl.multiple_of(k * CONV_TILE, CONV_TILE)
            t0 = pl.multiple_of(base + r0, CONV_TILE)
            taps = _conv_taps(x_ref[pl.ds(t0, n), :].astype(F32), n)
            pre = b_ref[...]
            for j in range(SSM_CONV):
                pre = pre + w_ref[j:j + 1, :] * taps[j]
            o_ref[pl.ds(r0, CONV_TILE), :] = _silu_and_grad(pre[CONV_HALO:CONV_HALO + CONV_TILE])[0].astype(BF16)
            return carry

        lax.fori_loop(0, sub, tile, 0)

    return pl.pallas_call(
        body, name=name, grid=(ncol, s // (sub * CONV_TILE)),
        in_specs=[pl.BlockSpec((s + 2 * CONV_HALO, 128), lambda j, i: (0, j)),
                  pl.BlockSpec((SSM_CONV, 128), lambda j, i: (0, j)), pl.BlockSpec((1, 128), lambda j, i: (0, j))],
        out_specs=pl.BlockSpec((sub * CONV_TILE, 128), lambda j, i: (i, j)),
        out_shape=jax.ShapeDtypeStruct((s, SSM_CONV_DIM), BF16), compiler_params=_params("parallel", "arbitrary"),
    )(xpad, w, b)


def _conv_bwd(xpad, dapad, w, b, *, name):
    s = xpad.shape[0] - 2 * CONV_HALO
    n = CONV_TILE + 2 * CONV_HALO
    ncol = SSM_CONV_DIM // 128
    mid = slice(CONV_HALO, CONV_HALO + CONV_TILE)
    sub = min(CONV_SUB, s // CONV_TILE)

    def body(x_ref, da_ref, w_ref, b_ref, dx_ref, dw_ref, db_ref):
        @pl.when(pl.program_id(1) == 0)
        def _():
            dw_ref[...] = jnp.zeros_like(dw_ref)
            db_ref[...] = jnp.zeros_like(db_ref)

        base = pl.program_id(1) * (sub * CONV_TILE)

        def tile(k, carry):
            r0 = pl.multiple_of(k * CONV_TILE, CONV_TILE)
            t0 = pl.multiple_of(base + r0, CONV_TILE)
            taps = _conv_taps(x_ref[pl.ds(t0, n), :].astype(F32), n)
            pre = b_ref[...]
            for j in range(SSM_CONV):
                pre = pre + w_ref[j:j + 1, :] * taps[j]
            dpre = da_ref[pl.ds(t0, n), :] * _silu_and_grad(pre)[1]
            dx = jnp.zeros((CONV_TILE, 128), F32)
            for j in range(SSM_CONV):
                back = dpre if j == 2 else pltpu.roll(dpre, (j - 2) % n, 0)
                dx = dx + w_ref[j:j + 1, :] * back[mid]
                dw_ref[j:j + 1, :] += jnp.sum(dpre[mid] * taps[j][mid], axis=0, keepdims=True)
            dx_ref[pl.ds(r0, CONV_TILE), :] = dx.astype(BF16)
            db_ref[...] += jnp.sum(dpre[mid], axis=0, keepdims=True)
            return carry

        lax.fori_loop(0, sub, tile, 0)

    full = pl.BlockSpec((s + 2 * CONV_HALO, 128), lambda j, i: (0, j))
    wspec = pl.BlockSpec((SSM_CONV, 128), lambda j, i: (0, j))
    bspec = pl.BlockSpec((1, 128), lambda j, i: (0, j))
    return pl.pallas_call(
        body, name=name, grid=(ncol, s // (sub * CONV_TILE)), in_specs=[full, full, wspec, bspec],
        out_specs=[pl.BlockSpec((sub * CONV_TILE, 128), lambda j, i: (i, j)), wspec, bspec],
        out_shape=[jax.ShapeDtypeStruct((s, SSM_CONV_DIM), BF16), jax.ShapeDtypeStruct((SSM_CONV, SSM_CONV_DIM), F32),
                   jax.ShapeDtypeStruct((1, SSM_CONV_DIM), F32)],
        compiler_params=_params("parallel", "arbitrary"),
    )(xpad, dapad, w, b)


HPG = SSM_HEADS // SSM_GROUPS
GW = HPG * SSM_HEAD_DIM
T = SSM_CHUNK


def _ssd_specs(nc):
    ceff = lambda d, c: jnp.where(d == 0, c, nc - 1 - c)
    return ceff, [
        pl.BlockSpec((T, GW), lambda d, g, c: (ceff(d, c), g)),
        pl.BlockSpec((T, SSM_STATE), lambda d, g, c: (ceff(d, c), SSM_INNER // 128 + g)),
        pl.BlockSpec((T, SSM_STATE), lambda d, g, c: (ceff(d, c), SSM_INNER // 128 + SSM_GROUPS + g)),
        pl.BlockSpec((None, None, T, HPG), lambda d, g, c: (d, g, ceff(d, c), 0)),
        pl.BlockSpec((None, None, HPG, T), lambda d, g, c: (d, g, 0, ceff(d, c))),
        pl.BlockSpec((None, None, 2, HPG), lambda d, g, c: (d, g, 0, 0)),
        pl.BlockSpec((None, None, HPG, 2), lambda d, g, c: (d, g, 0, 0)),
    ]


def _ssd_chunk_common(d, dt_ref, dtt_ref, prr_ref, prc_ref):
    sgn = 1 - 2 * d
    ri = lax.broadcasted_iota(jnp.int32, (T, T), 0)
    ci = lax.broadcasted_iota(jnp.int32, (T, T), 1)
    mask = ((ri - ci) * sgn) >= 0
    maskf = mask.astype(F32)
    bias_r, a_r = prr_ref[0:1, :], prr_ref[1:2, :]
    bias_c, a_c = prc_ref[:, 0:1], prc_ref[:, 1:2]
    raw = dt_ref[...] + bias_r
    dt_rows = _softplus(raw)
    dt_lanes = _softplus(dtt_ref[...] + bias_c)
    a_rows = dt_rows * a_r
    acum_rows = jnp.dot(maskf, a_rows, precision=HIGHEST, preferred_element_type=F32)
    acum_lanes = lax.dot_general(dt_lanes * a_c, maskf, _DIMS["nt"], precision=HIGHEST, preferred_element_type=F32)
    tot = jnp.sum(a_rows, axis=0, keepdims=True)
    return mask, maskf, raw, dt_rows, a_r, acum_rows, acum_lanes, tot


def _lanes_per_head(pieces):
    return jnp.concatenate([jnp.broadcast_to(p, (p.shape[0], SSM_HEAD_DIM)) for p in pieces], axis=1)


def _ssd_fwd_v1(xbc, dtr, dtt, pr_rows, pr_cols, *, name):
    s = xbc.shape[0]
    nc = s // T
    ceff, in_specs = _ssd_specs(nc)

    def body(x_ref, b_ref, c_ref, dt_ref, dtt_ref, prr_ref, prc_ref, y_ref, hs_ref, st_ref):
        d, c = pl.program_id(0), pl.program_id(2)

        @pl.when(c == 0)
        def _():
            st_ref[...] = jnp.zeros_like(st_ref)

        mask, _, _, dt_rows, _, acum_rows, acum_lanes, tot = _ssd_chunk_common(d, dt_ref, dtt_ref, prr_ref, prc_ref)
        xs = x_ref[...].astype(F32)
        bm, cm = b_ref[...], c_ref[...]
        hprev = st_ref[...]
        hs_ref[...] = hprev
        cb = lax.dot_general(cm, bm, _DIMS["nt"], preferred_element_type=F32)
        ch = jnp.dot(cm, hprev.astype(BF16), preferred_element_type=F32)
        ys, xgds, etots = [], [], []
        for j in range(HPG):
            sl = slice(j * SSM_HEAD_DIM, (j + 1) * SSM_HEAD_DIM)
            ac, al = acum_rows[:, j:j + 1], acum_lanes[j:j + 1, :]
            lm = jnp.where(mask, jnp.exp(jnp.minimum(ac - al, 0.0)), 0.0)
            xg = xs[:, sl] * dt_rows[:, j:j + 1]
            yd = jnp.dot((cb * lm).astype(BF16), xg.astype(BF16), preferred_element_type=F32)
            ys.append(yd + jnp.exp(ac) * ch[:, sl])
            xgds.append(xg * jnp.exp(tot[:, j:j + 1] - ac))
            etots.append(jnp.exp(tot[:, j:j + 1]))
        y_ref[...] = jnp.concatenate(ys, axis=1)
        new = lax.dot_general(bm, jnp.concatenate(xgds, axis=1).astype(BF16), _DIMS["tn"], preferred_element_type=F32)
        st_ref[...] = hprev * _lanes_per_head(etots) + new

    return pl.pallas_call(
        body, name=name, grid=(2, SSM_GROUPS, nc), in_specs=in_specs,
        out_specs=[pl.BlockSpec((None, T, GW), lambda d, g, c: (d, ceff(d, c), g)),
                   pl.BlockSpec((None, None, None, SSM_STATE, GW), lambda d, g, c: (d, ceff(d, c), g, 0, 0))],
        out_shape=[jax.ShapeDtypeStruct((2, s, SSM_INNER), F32),
                   jax.ShapeDtypeStruct((2, nc, SSM_GROUPS, SSM_STATE, GW), F32)],
        scratch_shapes=[pltpu.VMEM((SSM_STATE, GW), F32)],
        compiler_params=_params("parallel", "parallel", "arbitrary"),
    )(xbc, xbc, xbc, dtr, dtt, pr_rows, pr_cols)


def _put_lane(j, col):
    lane = lax.broadcasted_iota(jnp.int32, (col.shape[0], HPG), 1)
    return jnp.where(lane == j, col, 0.0)


def _ssd_bwd_v1(xbc, dtr, dtt, pr_rows, pr_cols, dvec, hs, dy, *, name):
    s = xbc.shape[0]
    nc = s // T
    cb_of = lambda d, c: jnp.where(d == 0, nc - 1 - c, c)
    in_specs = [
        pl.BlockSpec((T, GW), lambda d, g, c: (cb_of(d, c), g)),
        pl.BlockSpec((T, SSM_STATE), lambda d, g, c: (cb_of(d, c), SSM_INNER // 128 + g)),
        pl.BlockSpec((T, SSM_STATE), lambda d, g, c: (cb_of(d, c), SSM_INNER // 128 + SSM_GROUPS + g)),
        pl.BlockSpec((None, None, T, HPG), lambda d, g, c: (d, g, cb_of(d, c), 0)),
        pl.BlockSpec((None, None, HPG, T), lambda d, g, c: (d, g, 0, cb_of(d, c))),
        pl.BlockSpec((None, None, 2, HPG), lambda d, g, c: (d, g, 0, 0)),
        pl.BlockSpec((None, None, HPG, 2), lambda d, g, c: (d, g, 0, 0)),
        pl.BlockSpec((1, GW), lambda d, g, c: (0, g)),
        pl.BlockSpec((None, None, None, SSM_STATE, GW), lambda d, g, c: (d, cb_of(d, c), g, 0, 0)),
        pl.BlockSpec((T, GW), lambda d, g, c: (cb_of(d, c), g)),
    ]

    def body(x_ref, b_ref, c_ref, dt_ref, dtt_ref, prr_ref, prc_ref, dvec_ref, hs_ref, dy_ref,
             dxs_ref, db_ref, dc_ref, ddt_ref, dalog_ref, dbias_ref, g_ref):
        d, c = pl.program_id(0), pl.program_id(2)

        @pl.when(c == 0)
        def _():
            g_ref[...] = jnp.zeros_like(g_ref)
            dalog_ref[...] = jnp.zeros_like(dalog_ref)
            dbias_ref[...] = jnp.zeros_like(dbias_ref)

        mask, maskf, raw, dt_rows, a_r, acum_rows, acum_lanes, tot = _ssd_chunk_common(
            d, dt_ref, dtt_ref, prr_ref, prc_ref)
        xs = x_ref[...].astype(F32)
        bm, cm = b_ref[...], c_ref[...]
        hst = hs_ref[...]
        gst = g_ref[...]
        dyv = dy_ref[...]
        dyb = dyv.astype(BF16)
        dv = dvec_ref[...] * (1 - d).astype(F32)
        cb = lax.dot_general(cm, bm, _DIMS["nt"], preferred_element_type=F32)
        ch = jnp.dot(cm, hst.astype(BF16), preferred_element_type=F32)
        bg = jnp.dot(bm, gst.astype(BF16), preferred_element_type=F32)
        hg = hst * gst
        dcb = jnp.zeros((T, T), F32)
        dacum = jnp.zeros((T, HPG), F32)
        rx = jnp.zeros((T, HPG), F32)
        dtot = jnp.zeros((1, HPG), F32)
        dxss, dyes, xgds, etots = [], [], [], []
        for j in range(HPG):
            sl = slice(j * SSM_HEAD_DIM, (j + 1) * SSM_HEAD_DIM)
            ac, al = acum_rows[:, j:j + 1], acum_lanes[j:j + 1, :]
            lm = jnp.where(mask, jnp.exp(jnp.minimum(ac - al, 0.0)), 0.0)
            m = cb * lm
            dtj = dt_rows[:, j:j + 1]
            xsj = xs[:, sl]
            xg = xsj * dtj
            dyj = dyv[:, sl]
            ec = jnp.exp(ac)
            etot = jnp.exp(tot[:, j:j + 1])
            decay = jnp.exp(tot[:, j:j + 1] - ac)
            dm = lax.dot_general(dyb[:, sl], xg.astype(BF16), _DIMS["nt"], preferred_element_type=F32)
            w = dm * m
            dcb = dcb + dm * lm
            bgj = bg[:, sl]
            dxg = lax.dot_general(m.astype(BF16), dyb[:, sl], _DIMS["tn"], preferred_element_type=F32) + decay * bgj
            xb = decay * jnp.sum(xg * bgj, axis=-1, keepdims=True)
            da_j = (jnp.sum(w, axis=-1, keepdims=True) - jnp.sum(w.T, axis=-1, keepdims=True)
                    + jnp.sum(ec * ch[:, sl] * dyj, axis=-1, keepdims=True) - xb)
            dacum = dacum + _put_lane(j, da_j)
            rx = rx + _put_lane(j, jnp.sum(dxg * xsj, axis=-1, keepdims=True))
            dtot_j = (etot * jnp.sum(jnp.sum(hg[:, sl], axis=0, keepdims=True), axis=1, keepdims=True)
                      + jnp.sum(xb, axis=0, keepdims=True))
            dtot = dtot + _put_lane(j, dtot_j)
            dxss.append(dxg * dtj + dv[:, sl] * dyj)
            dyes.append(dyj * ec)
            xgds.append(xg * decay)
            etots.append(etot)
        da = lax.dot_general(maskf, dacum, _DIMS["tn"], precision=HIGHEST, preferred_element_type=F32) + dtot
        ddt = da * a_r + rx
        draw = ddt * _sigmoid(raw)
        ddt_ref[...] = draw
        dbias_ref[...] += jnp.sum(draw, axis=0, keepdims=True)
        dalog_ref[...] += jnp.sum(da * dt_rows, axis=0, keepdims=True) * a_r
        dxs_ref[...] = jnp.concatenate(dxss, axis=1)
        dye = jnp.concatenate(dyes, axis=1).astype(BF16)
        xgd = jnp.concatenate(xgds, axis=1).astype(BF16)
        dcbb = dcb.astype(BF16)
        dc_ref[...] = (jnp.dot(dcbb, bm, preferred_element_type=F32)
                       + lax.dot_general(dye, hst.astype(BF16), _DIMS["nt"], preferred_element_type=F32))
        db_ref[...] = (lax.dot_general(dcbb, cm, _DIMS["tn"], preferred_element_type=F32)
                       + lax.dot_general(xgd, gst.astype(BF16), _DIMS["nt"], preferred_element_type=F32))
        g_ref[...] = lax.dot_general(cm, dye, _DIMS["tn"], preferred_element_type=F32) + gst * _lanes_per_head(etots)

    small = pl.BlockSpec((None, None, 1, HPG), lambda d, g, c: (d, g, 0, 0))
    sshape = jax.ShapeDtypeStruct((2, SSM_GROUPS, 1, HPG), F32)
    return pl.pallas_call(
        body, name=name, grid=(2, SSM_GROUPS, nc), in_specs=in_specs,
        out_specs=[pl.BlockSpec((None, T, GW), lambda d, g, c: (d, cb_of(d, c), g)),
                   pl.BlockSpec((None, T, SSM_STATE), lambda d, g, c: (d, cb_of(d, c), g)),
                   pl.BlockSpec((None, T, SSM_STATE), lambda d, g, c: (d, cb_of(d, c), g)),
                   pl.BlockSpec((None, None, T, HPG), lambda d, g, c: (d, g, cb_of(d, c), 0)), small, small],
        out_shape=[jax.ShapeDtypeStruct((2, s, SSM_INNER), F32),
                   jax.ShapeDtypeStruct((2, s, SSM_GROUPS * SSM_STATE), F32),
                   jax.ShapeDtypeStruct((2, s, SSM_GROUPS * SSM_STATE), F32),
                   jax.ShapeDtypeStruct((2, SSM_GROUPS, s, HPG), F32), sshape, sshape],
        scratch_shapes=[pltpu.VMEM((SSM_STATE, GW), F32)],
        compiler_params=_params("parallel", "parallel", "arbitrary"),
    )(xbc, xbc, xbc, dtr, dtt, pr_rows, pr_cols, dvec, hs, dy)


PAIRS = HPG // 2


def _scan_lanes(x, forward):
    lane = lax.broadcasted_iota(jnp.int32, x.shape, 1)
    p = x
    k = 1
    while k < T:
        p = p + jnp.where(lane >= k, pltpu.roll(p, k, 1), 0.0)
        k *= 2
    tot = p[:, T - 1:T]
    return jnp.where(forward, p, tot - p + x), tot


def _ssd_chunk(d, dt_ref, dtt_ref, prr_ref, prc_ref):
    sgn = 1 - 2 * d
    ri = lax.broadcasted_iota(jnp.int32, (T, T), 0)
    ci = lax.broadcasted_iota(jnp.int32, (T, T), 1)
    mask = ((ri - ci) * sgn) >= 0
    mask_t = ((ci - ri) * sgn) >= 0
    bias_r = prr_ref[0:1, :]
    bias_c, a_c = prc_ref[:, 0:1], prc_ref[:, 1:2]
    dt_rows = _softplus(dt_ref[...] + bias_r)
    raw_lanes = dtt_ref[...] + bias_c
    dt_lanes = _softplus(raw_lanes)
    acum_lanes, tot = _scan_lanes(dt_lanes * a_c, d == 0)
    return dict(mask=mask, mask_t=mask_t, head0=ci < SSM_HEAD_DIM, dt_rows=dt_rows, raw_lanes=raw_lanes,
                dt_lanes=dt_lanes, a_c=a_c, acum_lanes=acum_lanes, acum_rows=acum_lanes.T, tot=tot)


def _ssd_pair(ck, q):
    h0 = ck["head0"]
    colb = lambda rows, j: jnp.broadcast_to(rows[:, j:j + 1], (T, T))
    rowb = lambda lanes, j: jnp.broadcast_to(lanes[j:j + 1, :], (T, T))
    lms, lmts, acs = [], [], []
    for j in (2 * q, 2 * q + 1):
        ac, al = colb(ck["acum_rows"], j), rowb(ck["acum_lanes"], j)
        lms.append(jnp.where(ck["mask"], jnp.exp(jnp.minimum(ac - al, 0.0)), 0.0))
        lmts.append(jnp.where(ck["mask_t"], jnp.exp(jnp.minimum(al - ac, 0.0)), 0.0))
        acs.append(ac)
    ac_pair = jnp.where(h0, acs[0], acs[1])
    dt_pair = jnp.where(h0, colb(ck["dt_rows"], 2 * q), colb(ck["dt_rows"], 2 * q + 1))
    tot_pair = jnp.where(h0[0:1], ck["tot"][2 * q:2 * q + 1, :], ck["tot"][2 * q + 1:2 * q + 2, :])
    return dict(lm=lms, lmt=lmts, dt=dt_pair, ec=jnp.exp(ac_pair), decay=jnp.exp(tot_pair - ac_pair),
                etot=jnp.exp(tot_pair))


def _split_heads(h0, v):
    zero = jnp.zeros_like(v)
    return jnp.where(h0, v, zero), jnp.where(h0, zero, v)


GPS = 2
GSTEPS = SSM_GROUPS // GPS
B_BLOCK0 = SSM_INNER // (GPS * SSM_STATE)
C_BLOCK0 = (SSM_INNER + SSM_GROUPS * SSM_STATE) // (GPS * SSM_STATE)


def _ssd_in_specs(chunk):
    return [
        pl.BlockSpec((T, GPS * GW), lambda d, g, c: (chunk(d, c), g)),
        pl.BlockSpec((T, GPS * SSM_STATE), lambda d, g, c: (chunk(d, c), B_BLOCK0 + g)),
        pl.BlockSpec((T, GPS * SSM_STATE), lambda d, g, c: (chunk(d, c), C_BLOCK0 + g)),
        pl.BlockSpec((GPS * SSM_STATE, T), lambda d, g, c: (g, chunk(d, c))),
        pl.BlockSpec((GPS * SSM_STATE, T), lambda d, g, c: (g, chunk(d, c))),
        pl.BlockSpec((None, GPS, T, HPG), lambda d, g, c: (d, g, chunk(d, c), 0)),
        pl.BlockSpec((None, GPS, HPG, T), lambda d, g, c: (d, g, 0, chunk(d, c))),
        pl.BlockSpec((None, GPS, 2, HPG), lambda d, g, c: (d, g, 0, 0)),
        pl.BlockSpec((None, GPS, HPG, 2), lambda d, g, c: (d, g, 0, 0)),
    ]


def _group_refs(gi, wide, state_wide, t_wide, lead):
    return ([r.at[:, pl.ds(gi * GW, GW)] for r in wide] + [r.at[:, pl.ds(gi * SSM_STATE, SSM_STATE)] for r in state_wide]
            + [r.at[pl.ds(gi * SSM_STATE, SSM_STATE), :] for r in t_wide] + [r.at[gi] for r in lead])


def _ssd_fwd(xbc, bt, ct, dtr, dtt, pr_rows, pr_cols, *, name):
    s = xbc.shape[0]
    nc = s // T
    chunk = lambda d, c: jnp.where(d == 0, c, nc - 1 - c)

    def body(x_ref, b_ref, c_ref, bt_ref, ct_ref, dt_ref, dtt_ref, prr_ref, prc_ref, y_ref, hs_ref, st_ref):
        @pl.when(pl.program_id(2) == 0)
        def _():
            st_ref[...] = jnp.zeros_like(st_ref)

        for gi in range(GPS):
            group(*_group_refs(gi, [x_ref, y_ref], [b_ref, c_ref], [bt_ref, ct_ref],
                               [dt_ref, dtt_ref, prr_ref, prc_ref, hs_ref, st_ref]))

    def group(x_ref, y_ref, b_ref, c_ref, bt_ref, ct_ref, dt_ref, dtt_ref, prr_ref, prc_ref, hs_ref, st_ref):
        d = pl.program_id(0)
        ck = _ssd_chunk(d, dt_ref, dtt_ref, prr_ref, prc_ref)
        xs = x_ref[...].astype(F32)
        cm = c_ref[...]
        hprev = st_ref[...]
        hs_ref[...] = hprev
        cb = lax.dot_general(cm, b_ref[...], _DIMS["nt"], preferred_element_type=F32)
        ch = jnp.dot(cm, hprev.astype(BF16), preferred_element_type=F32)
        ys, xgds, etots = [], [], []
        for q in range(PAIRS):
            sl = slice(q * 128, (q + 1) * 128)
            pr = _ssd_pair(ck, q)
            xg = xs[:, sl] * pr["dt"]
            xg0, xg1 = _split_heads(ck["head0"], xg.astype(BF16))
            yd = (jnp.dot((cb * pr["lm"][0]).astype(BF16), xg0, preferred_element_type=F32)
                  + jnp.dot((cb * pr["lm"][1]).astype(BF16), xg1, preferred_element_type=F32))
            ys.append(yd + pr["ec"] * ch[:, sl])
            xgds.append(xg * pr["decay"])
            etots.append(pr["etot"])
        y_ref[...] = jnp.concatenate(ys, axis=1)
        new = jnp.dot(bt_ref[...], jnp.concatenate(xgds, axis=1).astype(BF16), preferred_element_type=F32)
        st_ref[...] = hprev * jnp.concatenate(etots, axis=1) + new

    return pl.pallas_call(
        body, name=name, grid=(2, GSTEPS, nc), in_specs=_ssd_in_specs(chunk),
        out_specs=[pl.BlockSpec((None, T, GPS * GW), lambda d, g, c: (d, chunk(d, c), g)),
                   pl.BlockSpec((None, None, GPS, SSM_STATE, GW), lambda d, g, c: (d, chunk(d, c), g, 0, 0))],
        out_shape=[jax.ShapeDtypeStruct((2, s, SSM_INNER), F32),
                   jax.ShapeDtypeStruct((2, nc, SSM_GROUPS, SSM_STATE, GW), F32)],
        scratch_shapes=[pltpu.VMEM((GPS, SSM_STATE, GW), F32)],
        compiler_params=_params("parallel", "parallel", "arbitrary"),
    )(xbc, xbc, xbc, bt, ct, dtr, dtt, pr_rows, pr_cols)


def _ssd_bwd(xbc, bt, ct, dtr, dtt, pr_rows, pr_cols, dvec, hs, y2, dy, *, name):
    s = xbc.shape[0]
    nc = s // T
    chunk = lambda d, c: jnp.where(d == 0, nc - 1 - c, c)
    in_specs = _ssd_in_specs(chunk) + [
        pl.BlockSpec((1, GPS * GW), lambda d, g, c: (0, g)),
        pl.BlockSpec((None, None, GPS, SSM_STATE, GW), lambda d, g, c: (d, chunk(d, c), g, 0, 0)),
        pl.BlockSpec((None, T, GPS * GW), lambda d, g, c: (d, chunk(d, c), g)),
        pl.BlockSpec((T, GPS * GW), lambda d, g, c: (chunk(d, c), g)),
    ]

    def body(x_ref, b_ref, c_ref, bt_ref, ct_ref, dt_ref, dtt_ref, prr_ref, prc_ref, dvec_ref, hs_ref, y_ref, dy_ref,
             dxs_ref, db_ref, dc_ref, ddt_ref, dalog_ref, dbias_ref, g_ref):
        @pl.when(pl.program_id(2) == 0)
        def _():
            g_ref[...] = jnp.zeros_like(g_ref)
            dalog_ref[...] = jnp.zeros_like(dalog_ref)
            dbias_ref[...] = jnp.zeros_like(dbias_ref)

        for gi in range(GPS):
            group(*_group_refs(gi, [x_ref, dvec_ref, y_ref, dy_ref, dxs_ref], [b_ref, c_ref, db_ref, dc_ref], [bt_ref, ct_ref],
                               [dt_ref, dtt_ref, prr_ref, prc_ref, hs_ref, ddt_ref, dalog_ref, dbias_ref, g_ref]))

    def group(x_ref, dvec_ref, y_ref, dy_ref, dxs_ref, b_ref, c_ref, db_ref, dc_ref, bt_ref, ct_ref,
              dt_ref, dtt_ref, prr_ref, prc_ref, hs_ref, ddt_ref, dalog_ref, dbias_ref, g_ref):
        d = pl.program_id(0)
        ck = _ssd_chunk(d, dt_ref, dtt_ref, prr_ref, prc_ref)
        h0 = ck["head0"]
        xs = x_ref[...].astype(F32)
        bm, cm = b_ref[...], c_ref[...]
        hst = hs_ref[...]
        gst = g_ref[...]
        dyv = dy_ref[...]
        yv = y_ref[...]
        dv = dvec_ref[...] * (1 - d).astype(F32)
        cb = lax.dot_general(cm, bm, _DIMS["nt"], preferred_element_type=F32)
        cbt = jnp.dot(bm, ct_ref[...], preferred_element_type=F32)
        ch = jnp.dot(cm, hst.astype(BF16), preferred_element_type=F32)
        bg = jnp.dot(bm, gst.astype(BF16), preferred_element_type=F32)
        hg_cols = jnp.sum(hst * gst, axis=0, keepdims=True)
        lane16 = lax.broadcasted_iota(jnp.int32, (T, 2 * HPG), 1)
        sub8 = lax.broadcasted_iota(jnp.int32, (HPG, 1), 0)
        dcb = jnp.zeros((T, T), F32)
        acc16 = jnp.zeros((T, 2 * HPG), F32)
        dtot = jnp.zeros((HPG, 1), F32)
        dxss, dyes, xgds, etots = [], [], [], []
        for q in range(PAIRS):
            sl = slice(q * 128, (q + 1) * 128)
            pr = _ssd_pair(ck, q)
            xsp, dyp = xs[:, sl], dyv[:, sl]
            xg = xsp * pr["dt"]
            xgb = xg.astype(BF16)
            dyb = dyp.astype(BF16)
            dy0, dy1 = _split_heads(h0, dyb)
            dcb = dcb + (lax.dot_general(dy0, xgb, _DIMS["nt"], preferred_element_type=F32) * pr["lm"][0]
                         + lax.dot_general(dy1, xgb, _DIMS["nt"], preferred_element_type=F32) * pr["lm"][1])
            dxg_in = (jnp.dot((cbt * pr["lmt"][0]).astype(BF16), dy0, preferred_element_type=F32)
                      + jnp.dot((cbt * pr["lmt"][1]).astype(BF16), dy1, preferred_element_type=F32))
            xgd = xg * pr["decay"]
            xb = xgd * bg[:, sl]
            dxg = dxg_in + pr["decay"] * bg[:, sl]
            yo = pr["ec"] * ch[:, sl]
            dac = dyb.astype(F32) * (yv[:, sl] - yo) + dyp * yo - xgb.astype(F32) * dxg_in - xb
            d_0, d_1 = _split_heads(h0, dac)
            r_0, r_1 = _split_heads(h0, dxg * xsp)
            for hh, (d_h, r_h) in enumerate(((d_0, r_0), (d_1, r_1))):
                j = 2 * q + hh
                acc16 = (acc16 + jnp.where(lane16 == j, jnp.sum(d_h, axis=-1, keepdims=True), 0.0)
                         + jnp.where(lane16 == HPG + j, jnp.sum(r_h, axis=-1, keepdims=True), 0.0))
            tcols = pr["etot"] * hg_cols[:, sl] + jnp.sum(xb, axis=0, keepdims=True)
            t0, t1 = _split_heads(h0[0:1], tcols)
            dtot = (dtot + jnp.where(sub8 == 2 * q, jnp.sum(t0, axis=-1, keepdims=True), 0.0)
                    + jnp.where(sub8 == 2 * q + 1, jnp.sum(t1, axis=-1, keepdims=True), 0.0))
            dxss.append(dxg * pr["dt"] + dv[:, sl] * dyp)
            dyes.append(dyp * pr["ec"])
            xgds.append(xgd)
            etots.append(pr["etot"])
        acc_t = acc16.T
        da_lanes = _scan_lanes(acc_t[0:HPG], d != 0)[0] + dtot
        ddt = da_lanes * ck["a_c"] + acc_t[HPG:2 * HPG]
        draw = ddt * _sigmoid(ck["raw_lanes"])
        ddt_ref[...] = draw
        dbias_ref[...] += jnp.sum(draw, axis=-1, keepdims=True)
        dalog_ref[...] += jnp.sum(da_lanes * ck["dt_lanes"], axis=-1, keepdims=True) * ck["a_c"]
        dxs_ref[...] = jnp.concatenate(dxss, axis=1)
        dye = jnp.concatenate(dyes, axis=1).astype(BF16)
        xgd_all = jnp.concatenate(xgds, axis=1).astype(BF16)
        dcbb = dcb.astype(BF16)
        dc_ref[...] = (jnp.dot(dcbb, bm, preferred_element_type=F32)
                       + lax.dot_general(dye, hst.astype(BF16), _DIMS["nt"], preferred_element_type=F32))
        db_ref[...] = (lax.dot_general(dcbb, cm, _DIMS["tn"], preferred_element_type=F32)
                       + lax.dot_general(xgd_all, gst.astype(BF16), _DIMS["nt"], preferred_element_type=F32))
        g_ref[...] = jnp.dot(ct_ref[...], dye, preferred_element_type=F32) + gst * jnp.concatenate(etots, axis=1)

    small = pl.BlockSpec((None, GPS, HPG, 1), lambda d, g, c: (d, g, 0, 0))
    sshape = jax.ShapeDtypeStruct((2, SSM_GROUPS, HPG, 1), F32)
    return pl.pallas_call(
        body, name=name, grid=(2, GSTEPS, nc), in_specs=in_specs,
        out_specs=[pl.BlockSpec((None, T, GPS * GW), lambda d, g, c: (d, chunk(d, c), g)),
                   pl.BlockSpec((None, T, GPS * SSM_STATE), lambda d, g, c: (d, chunk(d, c), g)),
                   pl.BlockSpec((None, T, GPS * SSM_STATE), lambda d, g, c: (d, chunk(d, c), g)),
                   pl.BlockSpec((None, GPS, HPG, T), lambda d, g, c: (d, g, 0, chunk(d, c))), small, small],
        out_shape=[jax.ShapeDtypeStruct((2, s, SSM_INNER), F32),
                   jax.ShapeDtypeStruct((2, s, SSM_GROUPS * SSM_STATE), F32),
                   jax.ShapeDtypeStruct((2, s, SSM_GROUPS * SSM_STATE), F32),
                   jax.ShapeDtypeStruct((2, SSM_GROUPS, HPG, s), F32), sshape, sshape],
        scratch_shapes=[pltpu.VMEM((GPS, SSM_STATE, GW), F32)],
        compiler_params=_params("parallel", "parallel", "arbitrary"),
    )(xbc, xbc, xbc, bt, ct, dtr, dtt, pr_rows, pr_cols, dvec, hs, y2, dy)


def _gate_norm_fwd(y2, xbc, proj, dvec, nw, *, name):
    s = xbc.shape[0]
    tr = 256

    def body(y_ref, xs_ref, z_ref, dv_ref, w_ref, u_ref):
        yt = y_ref[0] + y_ref[1] + dv_ref[...] * xs_ref[...].astype(F32)
        yg = yt * _silu_and_grad(z_ref[...].astype(F32))[0]
        u_ref[...] = (yg * lax.rsqrt(jnp.mean(yg * yg, axis=-1, keepdims=True) + RMS_EPS) * w_ref[...]).astype(BF16)

    row = pl.BlockSpec((tr, SSM_INNER), lambda i: (i, 0))
    vec = pl.BlockSpec((1, SSM_INNER), lambda i: (0, 0))
    return pl.pallas_call(
        body, name=name, grid=(s // tr,),
        in_specs=[pl.BlockSpec((2, tr, SSM_INNER), lambda i: (0, i, 0)), row, row, vec, vec], out_specs=row,
        out_shape=jax.ShapeDtypeStruct((s, SSM_INNER), BF16), compiler_params=_params("parallel"),
    )(y2, xbc, proj, dvec, nw)


def _gate_norm_bwd(du, y2, xbc, proj, dvec, nw, *, name):
    s = xbc.shape[0]
    tr = 256

    def body(du_ref, y_ref, xs_ref, z_ref, dv_ref, w_ref, dy_ref, dz_ref, dw_ref, dd_ref):
        @pl.when(pl.program_id(0) == 0)
        def _():
            dw_ref[...] = jnp.zeros_like(dw_ref)
            dd_ref[...] = jnp.zeros_like(dd_ref)

        xs = xs_ref[...].astype(F32)
        yt = y_ref[0] + y_ref[1] + dv_ref[...] * xs
        si, dsi = _silu_and_grad(z_ref[...].astype(F32))
        yg = yt * si
        rstd = lax.rsqrt(jnp.mean(yg * yg, axis=-1, keepdims=True) + RMS_EPS)
        yhat = yg * rstd
        du = du_ref[...]
        dyn = du * w_ref[...]
        dyg = rstd * (dyn - yhat * jnp.mean(dyn * yhat, axis=-1, keepdims=True))
        dyt = dyg * si
        dy_ref[...] = dyt
        dz_ref[...] = (dyg * yt * dsi).astype(BF16)
        dw_ref[...] += jnp.sum(du * yhat, axis=0, keepdims=True)
        dd_ref[...] += jnp.sum(dyt * xs, axis=0, keepdims=True)

    row = pl.BlockSpec((tr, SSM_INNER), lambda i: (i, 0))
    vec = pl.BlockSpec((1, SSM_INNER), lambda i: (0, 0))
    vshape = jax.ShapeDtypeStruct((1, SSM_INNER), F32)
    return pl.pallas_call(
        body, name=name, grid=(s // tr,),
        in_specs=[row, pl.BlockSpec((2, tr, SSM_INNER), lambda i: (0, i, 0)), row, row, vec, vec],
        out_specs=[row, row, vec, vec],
        out_shape=[jax.ShapeDtypeStruct((s, SSM_INNER), F32), jax.ShapeDtypeStruct((s, SSM_INNER), BF16), vshape, vshape],
        compiler_params=_params("arbitrary"),
    )(du, y2, xbc, proj, dvec, nw)


def _layer_a_fwd(x, mod, w_in, w_out, ln_g, ln_b, tag):
    shift, scale, gate = mod
    h = _modulate(x, scale, shift, name=f"{tag}_modulate")
    w0 = jnp.concatenate([w_in[:, :A_GROUP_COLS], w_in[:, 3 * A_GROUP_COLS:]], axis=1)
    projs = [_mm(h, w0, mode="nn", out_dtype=BF16, tm=1024, tn=1024, tk=1024, name=f"{tag}_mm_in0")]
    for grp in (1, 2):
        projs.append(_mm_dilated(h, w_in[:, grp * A_GROUP_COLS:(grp + 1) * A_GROUP_COLS], DILATIONS[grp],
                                 name=f"{tag}_mm_in{grp}"))
    ol = []
    for grp in range(3):
        ol.extend(_attn_fwd(projs[grp], grp, name=f"{tag}_attn_fwd{grp}"))
    y, o, lse = _attn_combine(*ol, projs[0], name=f"{tag}_combine")
    out = _mm(y, w_out, mode="nn", out_dtype=F32, tm=512, tn=1024, tk=1024, name=f"{tag}_mm_out")
    xn = _resid_ln_fwd(x, out, gate, ln_g, ln_b, name=f"{tag}_resid_ln")
    return xn, (x, h, projs, y, o, lse, out)


def _layer_a_bwd(dxn, saved, mod, w_in, w_out, ln_g, tag):
    x, h, projs, y, o, lse, out = saved
    shift, scale, gate = mod
    dx_part, dout, dgate, dln_g, dln_b = _resid_ln_bwd(x, out, gate, ln_g, dxn, name=f"{tag}_resid_ln_bwd")
    dw_out = _mm(y.T, dout, mode="nn", out_dtype=F32, tm=1024, tn=1024, tk=1024, name=f"{tag}_mm_dw_out")
    dy = _mm(dout, w_out.T, mode="nn", out_dtype=F32, tm=512, tn=1024, tk=1024, name=f"{tag}_mm_dy")
    dgp, do0, do1, o1, lse1, do2, o2, lse2 = _attn_combine_bwd(dy, o, lse, projs[0], name=f"{tag}_combine_bwd")
    parts = [_attn_bwd(projs[grp], grp, *dol, name=f"{tag}_attn_bwd{grp}")
             for grp, dol in enumerate(((do0, o, lse), (do1, o1, lse1), (do2, o2, lse2)))]
    dproj = _assemble_dproj(*parts, dgp, name=f"{tag}_assemble_dproj")
    dw_in = _mm(h.T, dproj, mode="nn", out_dtype=F32, tm=1024, tn=1024, tk=1024, name=f"{tag}_mm_dw_in")
    dx, dscale, dshift = _mm_dh(dproj, w_in.T, dx_part, x, scale, tm=512, tk=2048, name=f"{tag}_mm_dh")
    grads = dict(w_in=dw_in, w_out=dw_out, ln_g=dln_g, ln_b=dln_b, mod=jnp.concatenate([dshift, dscale, dgate], axis=1))
    return dx, grads


def _ssd_param_views(dt_raw, dt_bias, a_log):
    s = dt_raw.shape[0]
    r4 = dt_raw.reshape(s, 2, SSM_GROUPS, HPG)
    dtr = r4.transpose(1, 2, 0, 3)
    dtt = r4.transpose(1, 2, 3, 0)
    a = -jnp.exp(a_log)
    pr_rows = jnp.stack([dt_bias.reshape(2, SSM_GROUPS, HPG), a.reshape(2, SSM_GROUPS, HPG)], axis=2)
    return dtr, dtt, pr_rows, pr_rows.transpose(0, 1, 3, 2)


def _layer_b_fwd(x, mod, p, ln_g, ln_b, tag):
    shift, scale, gate = mod
    s = x.shape[0]
    h = _modulate(x, scale, shift, name=f"{tag}_modulate")
    proj = _mm(h, p["w_in"][:, :SSM_MAIN_COLS], mode="nn", out_dtype=BF16, tm=512, tn=1024, tk=1024, name=f"{tag}_mm_in")
    dt_raw = _mm(h, p["w_in"][:, SSM_MAIN_COLS:SSM_IN_COLS], mode="nn", out_dtype=F32, tm=512, tn=64, tk=1024,
                 name=f"{tag}_mm_dt")
    xpad = jnp.pad(proj[:, SSM_INNER:], ((CONV_HALO, CONV_HALO), (0, 0)))
    xbc = _conv_fwd(xpad, p["conv_w"], p["conv_b"], name=f"{tag}_conv")
    views = (xbc[:, SSM_INNER:SSM_INNER + SSM_GROUPS * SSM_STATE].T, xbc[:, SSM_INNER + SSM_GROUPS * SSM_STATE:].T,
             *_ssd_param_views(dt_raw, p["dt_bias"], p["a_log"]))
    y2, hs = _ssd_fwd(xbc, *views, name=f"{tag}_ssd_fwd")
    u = _gate_norm_fwd(y2, xbc, proj, p["dvec"], p["norm_w"], name=f"{tag}_gate_norm")
    out = _mm(u, p["w_out"], mode="nn", out_dtype=F32, tm=512, tn=1024, tk=2048, name=f"{tag}_mm_out")
    xn = _resid_ln_fwd(x, out, gate, ln_g, ln_b, name=f"{tag}_resid_ln")
    return xn, (x, h, proj, xpad, xbc, views, y2, hs, u, out)


def _layer_b_bwd(dxn, saved, mod, p, ln_g, tag):
    x, h, proj, xpad, xbc, views, y2, hs, u, out = saved
    shift, scale, gate = mod
    s = x.shape[0]
    dx_part, dout, dgate, dln_g, dln_b = _resid_ln_bwd(x, out, gate, ln_g, dxn, name=f"{tag}_resid_ln_bwd")
    dw_out = _mm(u.T, dout, mode="nn", out_dtype=F32, tm=1024, tn=1024, tk=1024, name=f"{tag}_mm_dw_out")
    du = _mm(dout, p["w_out"].T, mode="nn", out_dtype=F32, tm=512, tn=1024, tk=1024, name=f"{tag}_mm_du")
    dy, dz, dnorm_w, dd_lanes = _gate_norm_bwd(du, y2, xbc, proj, p["dvec"], p["norm_w"], name=f"{tag}_gate_norm_bwd")
    dxs2, db2, dc2, ddt4, dalog, dbias = _ssd_bwd(xbc, *views, p["dvec"], hs, y2, dy, name=f"{tag}_ssd_bwd")
    dact = jnp.concatenate([dxs2[0] + dxs2[1], db2[0] + db2[1], dc2[0] + dc2[1]], axis=1)
    dapad = jnp.pad(dact, ((CONV_HALO, CONV_HALO), (0, 0)))
    dxbc, dconv_w, dconv_b = _conv_bwd(xpad, dapad, p["conv_w"], p["conv_b"], name=f"{tag}_conv_bwd")
    ddt_raw = ddt4.transpose(3, 0, 1, 2).reshape(s, 2 * SSM_HEADS).astype(BF16)
    dproj = jnp.concatenate([dz, dxbc, ddt_raw, jnp.zeros((s, SSM_PAD_COLS - SSM_IN_COLS), BF16)], axis=1)
    dw_in = _mm(h.T, dproj, mode="nn", out_dtype=F32, tm=1024, tn=896, tk=1024, name=f"{tag}_mm_dw_in")[:, :SSM_IN_COLS]
    w_pad_t = jnp.pad(p["w_in"].T, ((0, SSM_PAD_COLS - SSM_IN_COLS), (0, 0)))
    dx, dscale, dshift = _mm_dh(dproj, w_pad_t, dx_part, x, scale, tm=512, tk=1792, name=f"{tag}_mm_dh")
    grads = dict(
        w_in=dw_in, w_out=dw_out, ln_g=dln_g, ln_b=dln_b, mod=jnp.concatenate([dshift, dscale, dgate], axis=1),
        conv_w=dconv_w, conv_b=dconv_b, norm_w=dnorm_w, dt_bias=dbias.reshape(2, SSM_HEADS),
        a_log=dalog.reshape(2, SSM_HEADS), d=jnp.sum(dd_lanes.reshape(SSM_HEADS, SSM_HEAD_DIM), axis=1))
    return dx, grads


def _local_step(x, target, mods, ln_g, ln_b, a_w_in, a_w_out, b_params):
    saved = []
    for i in range(DEPTH):
        j = i // 2
        g, b = ln_g[i:i + 1], ln_b[i:i + 1]
        if i % 2 == 0:
            x, sv = _layer_a_fwd(x, mods[i], a_w_in[j], a_w_out[j], g, b, f"l{i}")
        else:
            x, sv = _layer_b_fwd(x, mods[i], b_params[j], g, b, f"l{i}")
        saved.append(sv)
    dx, loss = _loss_and_grad(x, target, name="loss")
    grads = [None] * DEPTH
    for i in reversed(range(DEPTH)):
        j = i // 2
        g = ln_g[i:i + 1]
        if i % 2 == 0:
            dx, grads[i] = _layer_a_bwd(dx, saved[i], mods[i], a_w_in[j], a_w_out[j], g, f"l{i}")
        else:
            dx, grads[i] = _layer_b_bwd(dx, saved[i], mods[i], b_params[j], g, f"l{i}")
    return loss, dx, grads


def _mesh_pos():
    return lax.axis_index("x"), lax.axis_index("y"), lax.axis_index("c")


def _all_gather(x, *, name):
    def body(x_ref, out_ref, send_sems, recv_sems, local_sem):
        ax, ay, ac = _mesh_pos()
        me, sibling = (ax, ay, ac), (ax, ay, 1 - ac)
        chips = [(1 - ax, ay), (ax, 1 - ay), (1 - ax, 1 - ay)]

        def slot(px, py, pc):
            return out_ref.at[4 * px + 2 * py + pc]

        def copy(k, block, to, src=None):
            return pltpu.make_async_remote_copy(
                src_ref=slot(*block) if src is None else src, dst_ref=slot(*block),
                send_sem=send_sems.at[k], recv_sem=recv_sems.at[k], device_id=to, device_id_type=MESH)

        mine = pltpu.make_async_copy(x_ref, slot(*me), local_sem)
        mine.start()
        first = [copy(0, me, sibling, src=x_ref)]
        first += [copy(1 + j, me, (*chip, ac), src=x_ref) for j, chip in enumerate(chips)]
        for cp in first:
            cp.start()
        passed = [copy(4 + j, (*chip, ac), sibling) for j, chip in enumerate(chips)]
        for j, chip in enumerate(chips):
            copy(1 + j, (*chip, ac), me).wait_recv()
            passed[j].start()
        copy(0, sibling, me).wait_recv()
        for j, chip in enumerate(chips):
            copy(4 + j, (*chip, 1 - ac), me).wait_recv()
        for cp in first + passed:
            cp.wait_send()
        mine.wait()

    return pl.pallas_call(
        body, name=name, out_shape=jax.ShapeDtypeStruct((N_DEV,) + x.shape, x.dtype),
        in_specs=[pl.BlockSpec(memory_space=pl.ANY)], out_specs=pl.BlockSpec(memory_space=pl.ANY),
        scratch_shapes=[pltpu.SemaphoreType.DMA((7,)), pltpu.SemaphoreType.DMA((7,)), pltpu.SemaphoreType.DMA],
    )(x)


def _all_to_all(x, *, name):
    def body(x_ref, out_ref, send_sems, recv_sems, local_sem):
        ax, ay, ac = _mesh_pos()
        me = 4 * ax + 2 * ay + ac
        mine = pltpu.make_async_copy(x_ref.at[me], out_ref.at[me], local_sem)
        mine.start()
        copies = []
        for k in range(1, N_DEV):
            px = 1 - ax if k & 4 else ax
            py = 1 - ay if k & 2 else ay
            pc = 1 - ac if k & 1 else ac
            copies.append(pltpu.make_async_remote_copy(
                src_ref=x_ref.at[4 * px + 2 * py + pc], dst_ref=out_ref.at[me],
                send_sem=send_sems.at[k - 1], recv_sem=recv_sems.at[k - 1], device_id=(px, py, pc), device_id_type=MESH))
        for cp in copies:
            cp.start()
        for cp in copies:
            cp.wait()
        mine.wait()

    return pl.pallas_call(
        body, name=name, out_shape=jax.ShapeDtypeStruct(x.shape, x.dtype),
        in_specs=[pl.BlockSpec(memory_space=pl.ANY)], out_specs=pl.BlockSpec(memory_space=pl.ANY),
        scratch_shapes=[pltpu.SemaphoreType.DMA((7,)), pltpu.SemaphoreType.DMA((7,)), pltpu.SemaphoreType.DMA],
    )(x)


ADA_LOCAL = 3 * D_MODEL // N_DEV


def _ada_mod(c_all, ada_w, ada_b_local, *, name):
    def body(c_ref, w_ref, b_ref, o_ref):
        cond = _silu_and_grad(c_ref[...])[0]
        o_ref[...] = jnp.dot(cond, w_ref[...], precision=HIGHEST, preferred_element_type=F32) + b_ref[...]

    return pl.pallas_call(
        body, name=name, grid=(DEPTH,),
        in_specs=[pl.BlockSpec((N_DEV, D_MODEL), lambda i: (0, 0)), pl.BlockSpec((None, D_MODEL, ADA_LOCAL), lambda i: (i, 0, 0)),
                  pl.BlockSpec((None, 1, ADA_LOCAL), lambda i: (i, 0, 0))],
        out_specs=pl.BlockSpec((None, N_DEV, ADA_LOCAL), lambda i: (i, 0, 0)),
        out_shape=jax.ShapeDtypeStruct((DEPTH, N_DEV, ADA_LOCAL), F32), compiler_params=_params("parallel"),
    )(c_all, ada_w, ada_b_local)


def _ada_grad(c_all_t, dmod_local, *, name):
    def body(ct_ref, dm_ref, o_ref):
        cond_t = _silu_and_grad(ct_ref[...])[0]
        dm = dm_ref[...]
        acc = cond_t[:, 0:1] * dm[0:1, :]
        for smp in range(1, N_DEV):
            acc = acc + cond_t[:, smp:smp + 1] * dm[smp:smp + 1, :]
        o_ref[...] = acc

    return pl.pallas_call(
        body, name=name, grid=(DEPTH,),
        in_specs=[pl.BlockSpec((D_MODEL, N_DEV), lambda i: (0, 0)), pl.BlockSpec((None, N_DEV, ADA_LOCAL), lambda i: (i, 0, 0))],
        out_specs=pl.BlockSpec((None, D_MODEL, ADA_LOCAL), lambda i: (i, 0, 0)),
        out_shape=jax.ShapeDtypeStruct((DEPTH, D_MODEL, ADA_LOCAL), F32), compiler_params=_params("parallel"),
    )(c_all_t, dmod_local)


def _sum_devices(parts, *, name):
    n = parts.shape[1]

    def body(p_ref, o_ref):
        acc = p_ref[0:1, :]
        for dev in range(1, N_DEV):
            acc = acc + p_ref[dev:dev + 1, :]
        o_ref[...] = acc

    return pl.pallas_call(
        body, name=name, out_shape=jax.ShapeDtypeStruct((1, n), F32),
        in_specs=[pl.BlockSpec(memory_space=pltpu.VMEM)], out_specs=pl.BlockSpec(memory_space=pltpu.VMEM),
        compiler_params=pltpu.CompilerParams(vmem_limit_bytes=VMEM_LIMIT_BYTES),
    )(parts)


ADAMW_VMEM_BYTES = 24 * 1024 * 1024


def _adamw(w, m, v, g, *, name):
    r, c = w.shape
    summed = g.ndim == 3
    tr = r
    arrays = 7 + (N_DEV if summed else 1)
    while tr % 16 == 0 and 2 * arrays * tr * c * 4 > ADAMW_VMEM_BYTES:
        tr //= 2

    def body(w_ref, m_ref, v_ref, g_ref, go_ref, d_ref, mo_ref, vo_ref):
        if summed:
            g = g_ref[0].astype(F32)
            for dev in range(1, N_DEV):
                g = g + g_ref[dev].astype(F32)
        else:
            g = g_ref[...]
        mn = ADAM_B1 * m_ref[...] + (1.0 - ADAM_B1) * g
        vn = ADAM_B2 * v_ref[...] + (1.0 - ADAM_B2) * (g * g)
        m_hat = mn / (1.0 - ADAM_B1 ** ADAM_STEP)
        v_hat = vn / (1.0 - ADAM_B2 ** ADAM_STEP)
        go_ref[...] = g
        d_ref[...] = -ADAM_LR * (m_hat / (jnp.sqrt(v_hat) + ADAM_EPS) + ADAM_WD * w_ref[...])
        mo_ref[...] = mn
        vo_ref[...] = vn

    row = pl.BlockSpec((tr, c), lambda i: (i, 0))
    gspec = pl.BlockSpec((N_DEV, tr, c), lambda i: (0, i, 0)) if summed else row
    shp = jax.ShapeDtypeStruct((r, c), F32)
    return pl.pallas_call(
        body, name=name, grid=(r // tr,), in_specs=[row, row, row, gspec], out_specs=[row] * 4, out_shape=[shp] * 4,
        compiler_params=_params("parallel"),
    )(w, m, v, g)


def _pack(arrays):
    flat = jnp.concatenate([a.reshape(-1) for a in arrays])
    n = flat.shape[0]
    return jnp.pad(flat, (0, -n % 128)).reshape(1, -1)


def _unpack(vec, shapes):
    out, at = [], 0
    for shp in shapes:
        n = math.prod(shp)
        out.append(vec[at:at + n].reshape(shp))
        at += n
    return out


def _unpack_rows(rows, shapes):
    out, at = [], 0
    for shp in shapes:
        n = math.prod(shp)
        out.append(rows[:, at:at + n].reshape((rows.shape[0],) + tuple(shp)))
        at += n
    return out


def _my_shard(full, me, axis):
    width = full.shape[axis] // N_DEV
    return lax.dynamic_slice_in_dim(full, me * width, width, axis)


def _gather_cols(g, lead):
    nd = g.ndim
    perm = tuple(range(1, nd - 1)) + (0, nd - 1)
    t = g.transpose(perm)
    return t.reshape(t.shape[:-2] + (t.shape[-2] * t.shape[-1],))


def kernel(x, c, ada_w, ada_b, ln_g, ln_b, a_w_in, a_w_out, b_w_in, b_conv_w, b_conv_b, b_dt_bias, b_a_log, b_d, b_norm_w, b_w_out, loss_target, m_ada_w, m_ada_b, m_ln_g, m_ln_b, m_a_w_in, m_a_w_out, m_b_w_in, m_b_conv_w, m_b_conv_b, m_b_dt_bias, m_b_a_log, m_b_d, m_b_norm_w, m_b_w_out, v_ada_w, v_ada_b, v_ln_g, v_ln_b, v_a_w_in, v_a_w_out, v_b_w_in, v_b_conv_w, v_b_conv_b, v_b_dt_bias, v_b_a_log, v_b_d, v_b_norm_w, v_b_w_out):
    ax, ay, ac = _mesh_pos()
    me = 4 * ax + 2 * ay + ac
    seq = x.shape[1]

    small_shapes = [(1, D_MODEL), (2, SSM_CONV, ADA_LOCAL), (2, ADA_LOCAL), (2, SSM_INNER // N_DEV)]
    sg = _all_gather(_pack([c, b_conv_w, b_conv_b, b_norm_w]), name="gather_small")[:, 0, :]
    c_all, conv_w_g, conv_b_g, norm_w_g = _unpack_rows(sg, small_shapes)
    c_all = c_all[:, 0, :]
    conv_w = _gather_cols(conv_w_g, 2)
    conv_b = _gather_cols(conv_b_g[:, :, None, :], 2)
    norm_w = _gather_cols(norm_w_g[:, :, None, :], 2)

    def gathered(w, name):
        lead = w.shape[0]
        flat = w.astype(BF16).reshape(lead * w.shape[1], w.shape[2])
        return _all_gather(flat, name=name).reshape(N_DEV, lead, w.shape[1], w.shape[2])

    a_w_in_f = _gather_cols(gathered(a_w_in, "gather_a_w_in"), 2)
    b_w_in_f = _gather_cols(gathered(b_w_in, "gather_b_w_in"), 2)
    a_w_out_f = gathered(a_w_out, "gather_a_w_out").transpose(1, 0, 2, 3).reshape(2, A_WIDTH, D_MODEL)
    b_w_out_f = gathered(b_w_out, "gather_b_w_out").transpose(1, 0, 2, 3).reshape(2, SSM_INNER, D_MODEL)

    ada_b_local = _my_shard(ada_b, me, 1)[:, None, :]
    mod_cols = _ada_mod(c_all, ada_w, ada_b_local, name="ada_mod")
    mod_g = _all_gather(mod_cols.reshape(1, -1), name="gather_mod").reshape(N_DEV, DEPTH, N_DEV, ADA_LOCAL)
    mod = lax.dynamic_index_in_dim(mod_g, me, axis=2, keepdims=False).transpose(1, 0, 2).reshape(DEPTH, 3 * D_MODEL)
    mods = [tuple(mod[i:i + 1, k * D_MODEL:(k + 1) * D_MODEL] for k in range(3)) for i in range(DEPTH)]

    b_params = [dict(w_in=b_w_in_f[j], w_out=b_w_out_f[j], conv_w=conv_w[j], conv_b=conv_b[j], norm_w=norm_w[j],
                     dt_bias=b_dt_bias[j], a_log=b_a_log[j], dvec=jnp.repeat(b_d[j], SSM_HEAD_DIM)[None, :])
                for j in range(2)]
    loss_lanes, dx, grads = _local_step(x[0], loss_target[0], mods, ln_g, ln_b, a_w_in_f, a_w_out_f, b_params)
    loss = lax.psum(loss_lanes[0, 0], ("x", "y", "c"))
    grad_x = dx[None]

    a_layers, b_layers = (grads[0], grads[2]), (grads[1], grads[3])
    part_shapes = [(DEPTH, 3 * D_MODEL), (DEPTH, D_MODEL), (DEPTH, D_MODEL), (2, SSM_CONV, SSM_CONV_DIM),
                   (2, SSM_CONV_DIM), (2, SSM_INNER), (2, 2, SSM_HEADS), (2, 2, SSM_HEADS), (2, SSM_HEADS)]
    parts = _pack([
        jnp.concatenate([g["mod"] for g in grads]), jnp.concatenate([g["ln_g"] for g in grads]),
        jnp.concatenate([g["ln_b"] for g in grads]), jnp.stack([g["conv_w"] for g in b_layers]),
        jnp.stack([g["conv_b"][0] for g in b_layers]), jnp.stack([g["norm_w"][0] for g in b_layers]),
        jnp.stack([g["dt_bias"] for g in b_layers]), jnp.stack([g["a_log"] for g in b_layers]),
        jnp.stack([g["d"] for g in b_layers])])
    parts_g = _all_gather(parts, name="gather_small_grads")[:, 0, :]
    (g_ada_b, g_ln_g, g_ln_b, g_conv_w, g_conv_b, g_norm_w, g_dt_bias, g_a_log, g_d) = _unpack(
        _sum_devices(parts_g, name="sum_small_grads")[0], part_shapes)
    dmod_all = parts_g[:, :DEPTH * 3 * D_MODEL].reshape(N_DEV, DEPTH, N_DEV, ADA_LOCAL)
    dmod_local = lax.dynamic_index_in_dim(dmod_all, me, axis=2, keepdims=False).transpose(1, 0, 2)
    g_ada_w = _ada_grad(c_all.T, dmod_local, name="ada_grad")

    def scatter_cols(per_layer, name):
        st = jnp.stack(per_layer)
        lead, r, cols = st.shape
        blocks = st.reshape(lead, r, N_DEV, cols // N_DEV).transpose(2, 0, 1, 3).reshape(N_DEV, lead * r, cols // N_DEV)
        return _all_to_all(blocks.astype(BF16), name=name)

    def scatter_rows(per_layer, name):
        st = jnp.stack(per_layer)
        lead, rows, cols = st.shape
        blocks = st.reshape(lead, N_DEV, rows // N_DEV, cols).transpose(1, 0, 2, 3).reshape(N_DEV, lead * rows // N_DEV, cols)
        return _all_to_all(blocks.astype(BF16), name=name)

    r_a_w_in = scatter_cols([g["w_in"] for g in a_layers], "scatter_a_w_in")
    r_a_w_out = scatter_rows([g["w_out"] for g in a_layers], "scatter_a_w_out")
    r_b_w_in = scatter_cols([g["w_in"] for g in b_layers], "scatter_b_w_in")
    r_b_w_out = scatter_rows([g["w_out"] for g in b_layers], "scatter_b_w_out")

    def update(w, m, v, g, name):
        two_d = (-1, w.shape[-1])
        outs = _adamw(w.reshape(two_d), m.reshape(two_d), v.reshape(two_d), g, name=name)
        return [o.reshape(w.shape) for o in outs]

    up_ada_w = update(ada_w, m_ada_w, v_ada_w, g_ada_w.reshape(-1, ADA_LOCAL), "adamw_ada_w")
    up_a_w_in = update(a_w_in, m_a_w_in, v_a_w_in, r_a_w_in, "adamw_a_w_in")
    up_a_w_out = update(a_w_out, m_a_w_out, v_a_w_out, r_a_w_out, "adamw_a_w_out")
    up_b_w_in = update(b_w_in, m_b_w_in, v_b_w_in, r_b_w_in, "adamw_b_w_in")
    up_b_w_out = update(b_w_out, m_b_w_out, v_b_w_out, r_b_w_out, "adamw_b_w_out")

    small_w = [ada_b, ln_g, ln_b, b_conv_w, b_conv_b, b_dt_bias, b_a_log, b_d, b_norm_w]
    small_m = [m_ada_b, m_ln_g, m_ln_b, m_b_conv_w, m_b_conv_b, m_b_dt_bias, m_b_a_log, m_b_d, m_b_norm_w]
    small_v = [v_ada_b, v_ln_g, v_ln_b, v_b_conv_w, v_b_conv_b, v_b_dt_bias, v_b_a_log, v_b_d, v_b_norm_w]
    small_g = [g_ada_b, g_ln_g, g_ln_b, _my_shard(g_conv_w, me, 2), _my_shard(g_conv_b, me, 1), g_dt_bias, g_a_log, g_d,
               _my_shard(g_norm_w, me, 1)]
    shapes = [w.shape for w in small_w]
    packed = _adamw(_pack(small_w), _pack(small_m), _pack(small_v), _pack(small_g), name="adamw_small")
    (up_ada_b, up_ln_g, up_ln_b, up_conv_w, up_conv_b, up_dt_bias, up_a_log, up_d, up_norm_w) = zip(
        *[_unpack(p[0], shapes) for p in packed])

    ordered = [up_ada_w, up_ada_b, up_ln_g, up_ln_b, up_a_w_in, up_a_w_out, up_b_w_in, up_conv_w, up_conv_b,
               up_dt_bias, up_a_log, up_d, up_norm_w, up_b_w_out]
    return (loss, grad_x, *[u[0] for u in ordered], *[u[1] for u in ordered], *[u[2] for u in ordered],
            *[u[3] for u in ordered])
```

```python
import functools
import math

import jax
import jax.numpy as jnp
import numpy as np
from jax import lax
from jax.experimental import pallas as pl
from jax.experimental.pallas import tpu as pltpu

F32 = jnp.float32
BF16 = jnp.bfloat16
HIGHEST = lax.Precision.HIGHEST
MESH = pl.DeviceIdType.MESH

D_MODEL = 1024
DEPTH = 4
A_HEADS = 16
A_HEAD_DIM = 64
A_WIDTH = 1024
DILATIONS = (1, 4, 16)
A_RADIUS = 64
A_QBLOCK = 128
A_IN_COLS = 10240
SSM_INNER = 2048
SSM_HEADS = 32
SSM_HEAD_DIM = 64
SSM_STATE = 128
SSM_GROUPS = 4
SSM_CHUNK = 128
SSM_CONV = 5
SSM_CONV_DIM = 3072
SSM_IN_COLS = 5184
SSM_MAIN_COLS = 5120
SSM_PAD_COLS = 5376
CONV_HALO = 16
ALPHA = (2 * DEPTH) ** 0.25
LN_EPS = 1e-5
RMS_EPS = 1e-5
ADAM_LR, ADAM_B1, ADAM_B2, ADAM_EPS, ADAM_WD, ADAM_STEP = 0.001, 0.9, 0.999, 1e-08, 0.01, 10
N_DEV = 8
VMEM_LIMIT_BYTES = 56 * 1024 * 1024
NEG_BIG = -1e30


def _params(*sem):
    return pltpu.CompilerParams(dimension_semantics=sem, vmem_limit_bytes=VMEM_LIMIT_BYTES)


def _sigmoid(x):
    return 1.0 / (1.0 + jnp.exp(-x))


def _silu_and_grad(x):
    sg = _sigmoid(x)
    return x * sg, sg * (1.0 + x * (1.0 - sg))


def _softplus(x):
    e = jnp.exp(-jnp.abs(x))
    u = 1.0 + e
    log1p = jnp.where(u == 1.0, e, jnp.log(u) * (e / jnp.where(u == 1.0, 1.0, u - 1.0)))
    return jnp.maximum(x, 0.0) + log1p


_DIMS = {"nn": (((1,), (0,)), ((), ())), "nt": (((1,), (1,)), ((), ())), "tn": (((0,), (0,)), ((), ()))}


def _exchange_copies(x_ref, out_ref, send_sems, recv_sems, local_sem, gather):
    ax, ay, ac = lax.axis_index("x"), lax.axis_index("y"), lax.axis_index("c")
    me = 4 * ax + 2 * ay + ac
    copies = [pltpu.make_async_copy(x_ref if gather else x_ref.at[me], out_ref.at[me], local_sem)]
    for k in range(1, N_DEV):
        px = 1 - ax if k & 4 else ax
        py = 1 - ay if k & 2 else ay
        pc = 1 - ac if k & 1 else ac
        copies.append(pltpu.make_async_remote_copy(
            src_ref=x_ref if gather else x_ref.at[4 * px + 2 * py + pc], dst_ref=out_ref.at[me],
            send_sem=send_sems.at[k - 1], recv_sem=recv_sems.at[k - 1], device_id=(px, py, pc), device_id_type=MESH))
    return copies


def _mm(a, b, *, mode, out_dtype, tm, tn, tk, name, riders=()):
    if mode == "nn":
        (m, k), (_, n) = a.shape, b.shape
    elif mode == "nt":
        (m, k), (n, _) = a.shape, b.shape
    else:
        (k, m), (_, n) = a.shape, b.shape
    tm, tn, tk = min(tm, m), min(tn, n), min(tk, k)
    assert m % tm == 0 and n % tn == 0 and k % tk == 0, (name, a.shape, b.shape)
    nk = k // tk
    dims = _DIMS[mode]
    nr = len(riders)
    grid = (m // tm, n // tn, nk)

    def body(a_ref, b_ref, *rest):
        xs, o_ref, ys, scratch = rest[:nr], rest[nr], rest[nr + 1:2 * nr + 1], rest[2 * nr + 1:]
        sems = scratch[len(scratch) - 3 * nr:]
        ids = [pl.program_id(ax) for ax in range(3)]

        def exchanges():
            return [_exchange_copies(xs[r], ys[r], *sems[3 * r:3 * r + 3], riders[r][1]) for r in range(nr)]

        if nr:
            @pl.when(jnp.logical_and(jnp.logical_and(ids[0] == 0, ids[1] == 0), ids[2] == 0))
            def _():
                for copies in exchanges():
                    for cp in copies:
                        cp.start()

        part = lax.dot_general(a_ref[...], b_ref[...], dims, preferred_element_type=F32)
        if nk == 1:
            o_ref[...] = part.astype(o_ref.dtype)
        else:
            acc_ref = scratch[0]
            kk = ids[2]

            @pl.when(kk == 0)
            def _():
                acc_ref[...] = part

            @pl.when(kk > 0)
            def _():
                acc_ref[...] += part

            @pl.when(kk == nk - 1)
            def _():
                o_ref[...] = acc_ref[...].astype(o_ref.dtype)

        if nr:
            @pl.when(jnp.logical_and(jnp.logical_and(ids[0] == grid[0] - 1, ids[1] == grid[1] - 1), ids[2] == grid[2] - 1))
            def _():
                for copies in exchanges():
                    for cp in copies:
                        cp.wait()

    if mode == "tn":
        a_spec = pl.BlockSpec((tk, tm), lambda i, j, kk: (kk, i))
    else:
        a_spec = pl.BlockSpec((tm, tk), lambda i, j, kk: (i, kk))
    if mode == "nt":
        b_spec = pl.BlockSpec((tn, tk), lambda i, j, kk: (j, kk))
    else:
        b_spec = pl.BlockSpec((tk, tn), lambda i, j, kk: (kk, j))
    anywhere = pl.BlockSpec(memory_space=pl.ANY)
    exchanged = [jax.ShapeDtypeStruct(((N_DEV,) + x.shape) if gather else x.shape, x.dtype) for x, gather in riders]
    sems = [pltpu.SemaphoreType.DMA((N_DEV - 1,)), pltpu.SemaphoreType.DMA((N_DEV - 1,)), pltpu.SemaphoreType.DMA] * nr
    outs = pl.pallas_call(
        body, name=name, grid=grid,
        in_specs=[a_spec, b_spec] + [anywhere] * nr,
        out_specs=[pl.BlockSpec((tm, tn), lambda i, j, kk: (i, j))] + [anywhere] * nr,
        out_shape=[jax.ShapeDtypeStruct((m, n), out_dtype)] + exchanged,
        scratch_shapes=([] if nk == 1 else [pltpu.VMEM((tm, tn), F32)]) + sems,
        compiler_params=_params(*(("arbitrary",) * 3 if nr else ("parallel", "parallel", "arbitrary"))),
    )(a, b, *[x for x, _ in riders])
    return (outs[0], list(outs[1:])) if nr else outs[0]


def _mm_dh(dproj, w, dx_part, x, scale, *, tm, tk, name):
    s, k = dproj.shape
    d = w.shape[0]
    tk = min(tk, k)
    assert s % tm == 0 and k % tk == 0
    nk = k // tk

    def body(a_ref, w_ref, dxp_ref, x_ref, sc_ref, dx_ref, dsc_ref, dsh_ref, acc_ref):
        i, kk = pl.program_id(0), pl.program_id(1)
        part = lax.dot_general(a_ref[...], w_ref[...], _DIMS["nt"], preferred_element_type=F32)

        @pl.when(kk == 0)
        def _():
            acc_ref[...] = part

        @pl.when(kk > 0)
        def _():
            acc_ref[...] += part

        @pl.when(jnp.logical_and(i == 0, kk == 0))
        def _():
            dsc_ref[...] = jnp.zeros_like(dsc_ref)
            dsh_ref[...] = jnp.zeros_like(dsh_ref)

        @pl.when(kk == nk - 1)
        def _():
            dh = acc_ref[...]
            dx_ref[...] = dxp_ref[...] + dh * (1.0 + sc_ref[...])
            dsc_ref[...] += jnp.sum(dh * x_ref[...], axis=0, keepdims=True)
            dsh_ref[...] += jnp.sum(dh, axis=0, keepdims=True)

    row = pl.BlockSpec((tm, d), lambda i, kk: (i, 0))
    vec = pl.BlockSpec((1, d), lambda i, kk: (0, 0))
    return pl.pallas_call(
        body, name=name, grid=(s // tm, nk),
        in_specs=[pl.BlockSpec((tm, tk), lambda i, kk: (i, kk)), pl.BlockSpec((d, tk), lambda i, kk: (0, kk)),
                  row, row, vec],
        out_specs=[row, vec, vec],
        out_shape=[jax.ShapeDtypeStruct((s, d), F32), jax.ShapeDtypeStruct((1, d), F32),
                   jax.ShapeDtypeStruct((1, d), F32)],
        scratch_shapes=[pltpu.VMEM((tm, d), F32)],
        compiler_params=_params("arbitrary", "arbitrary"),
    )(dproj, w, dx_part, x, scale)


ROW_TILE = 512


def _modulate(x, scale, shift, *, name):
    s, d = x.shape

    def body(x_ref, sc_ref, sh_ref, h_ref):
        h_ref[...] = (x_ref[...] * (1.0 + sc_ref[...]) + sh_ref[...]).astype(BF16)

    row = pl.BlockSpec((ROW_TILE, d), lambda i: (i, 0))
    vec = pl.BlockSpec((1, d), lambda i: (0, 0))
    return pl.pallas_call(
        body, name=name, grid=(s // ROW_TILE,), in_specs=[row, vec, vec], out_specs=row,
        out_shape=jax.ShapeDtypeStruct((s, d), BF16), compiler_params=_params("parallel"),
    )(x, scale, shift)


def _resid_ln_fwd(x, out, gate, g, b, *, name):
    s, d = x.shape

    def body(x_ref, o_ref, gate_ref, g_ref, b_ref, y_ref):
        r = ALPHA * x_ref[...] + gate_ref[...] * o_ref[...]
        mu = jnp.mean(r, axis=-1, keepdims=True)
        rc = r - mu
        var = jnp.mean(rc * rc, axis=-1, keepdims=True)
        y_ref[...] = rc * lax.rsqrt(var + LN_EPS) * g_ref[...] + b_ref[...]

    row = pl.BlockSpec((ROW_TILE, d), lambda i: (i, 0))
    vec = pl.BlockSpec((1, d), lambda i: (0, 0))
    return pl.pallas_call(
        body, name=name, grid=(s // ROW_TILE,), in_specs=[row, row, vec, vec, vec], out_specs=row,
        out_shape=jax.ShapeDtypeStruct((s, d), F32), compiler_params=_params("parallel"),
    )(x, out, gate, g, b)


def _resid_ln_bwd(x, out, gate, g, dy, *, name):
    s, d = x.shape

    def body(x_ref, o_ref, gate_ref, g_ref, dy_ref, dxp_ref, dout_ref, dgate_ref, dg_ref, db_ref):
        @pl.when(pl.program_id(0) == 0)
        def _():
            dgate_ref[...] = jnp.zeros_like(dgate_ref)
            dg_ref[...] = jnp.zeros_like(dg_ref)
            db_ref[...] = jnp.zeros_like(db_ref)

        o = o_ref[...]
        r = ALPHA * x_ref[...] + gate_ref[...] * o
        mu = jnp.mean(r, axis=-1, keepdims=True)
        rc = r - mu
        var = jnp.mean(rc * rc, axis=-1, keepdims=True)
        rstd = lax.rsqrt(var + LN_EPS)
        xhat = rc * rstd
        dy = dy_ref[...]
        dxh = dy * g_ref[...]
        dr = rstd * (dxh - jnp.mean(dxh, axis=-1, keepdims=True) - xhat * jnp.mean(dxh * xhat, axis=-1, keepdims=True))
        dxp_ref[...] = ALPHA * dr
        dout_ref[...] = (gate_ref[...] * dr).astype(BF16)
        dgate_ref[...] += jnp.sum(dr * o, axis=0, keepdims=True)
        dg_ref[...] += jnp.sum(dy * xhat, axis=0, keepdims=True)
        db_ref[...] += jnp.sum(dy, axis=0, keepdims=True)

    row = pl.BlockSpec((ROW_TILE, d), lambda i: (i, 0))
    vec = pl.BlockSpec((1, d), lambda i: (0, 0))
    vshape = jax.ShapeDtypeStruct((1, d), F32)
    return pl.pallas_call(
        body, name=name, grid=(s // ROW_TILE,), in_specs=[row, row, vec, vec, row],
        out_specs=[row, row, vec, vec, vec],
        out_shape=[jax.ShapeDtypeStruct((s, d), F32), jax.ShapeDtypeStruct((s, d), BF16), vshape, vshape, vshape],
        compiler_params=_params("arbitrary"),
    )(x, out, gate, g, dy)


def _loss_and_grad(y, target, *, name):
    s, d = y.shape

    def body(y_ref, t_ref, dy_ref, loss_ref):
        @pl.when(pl.program_id(0) == 0)
        def _():
            loss_ref[...] = jnp.zeros_like(loss_ref)

        e = y_ref[...] - t_ref[...]
        dy_ref[...] = e * (1.0 / d)
        loss_ref[...] += jnp.sum(jnp.sum(e * e, axis=0, keepdims=True), axis=1, keepdims=True) * (0.5 / d)

    row = pl.BlockSpec((ROW_TILE, d), lambda i: (i, 0))
    return pl.pallas_call(
        body, name=name, grid=(s // ROW_TILE,), in_specs=[row, row],
        out_specs=[row, pl.BlockSpec((1, 128), lambda i: (0, 0))],
        out_shape=[jax.ShapeDtypeStruct((s, d), F32), jax.ShapeDtypeStruct((1, 128), F32)],
        compiler_params=_params("arbitrary"),
    )(y, target)


_SLOPES = np.asarray(2.0 ** (-8.0 * (np.arange(A_HEADS, dtype=np.float32) + 1.0) / A_HEADS), dtype=np.float32)


def _attn_scores(q, kw, slope, dist, valid):
    s = lax.dot_general(q, kw, _DIMS["nt"], preferred_element_type=F32) * (1.0 / math.sqrt(A_HEAD_DIM))
    return jnp.where(valid, s - slope * dist, NEG_BIG)


def _attn_window(blk, length, win, dil):
    start = pl.multiple_of(jnp.clip(blk * A_QBLOCK - A_RADIUS, 0, length - win), A_RADIUS)
    qpos = blk * A_QBLOCK + lax.broadcasted_iota(jnp.int32, (A_QBLOCK, win), 0)
    kpos = start + lax.broadcasted_iota(jnp.int32, (A_QBLOCK, win), 1)
    delta = jnp.abs(kpos - qpos)
    return start, (delta * dil).astype(F32), delta <= A_RADIUS


A_BLOCKS_PER_STEP = 4
A_GROUP_COLS = 3 * A_WIDTH


def _tile_scratch(rows, width):
    return pltpu.VMEM((width // 128, rows, 128), F32)


def _put_tile(scr, val):
    for j in range(scr.shape[0]):
        scr[j] = val[:, j * 128:(j + 1) * 128]


def _get_tile(scr):
    return jnp.concatenate([scr[j] for j in range(scr.shape[0])], axis=1)


def _get_residue(scr, r, dil):
    rows = pl.ds(r, scr.shape[1] // dil, stride=dil)
    return jnp.concatenate([scr.at[j][rows, :] for j in range(scr.shape[0])], axis=1)


def _put_residue(scr, r, dil, val):
    rows = pl.ds(r, scr.shape[1] // dil, stride=dil)
    for j in range(scr.shape[0]):
        scr.at[j][rows, :] = val[:, j * 128:(j + 1) * 128]


def _mm_dilated(a, b, dil, *, name):
    m, k = a.shape
    n = b.shape[1]
    tm = 512

    def body(a_ref, b_ref, o_ref, acc_ref):
        _put_tile(acc_ref, jnp.dot(a_ref[...], b_ref[...], preferred_element_type=F32))
        for r in range(dil):
            o_ref[:, r * n:(r + 1) * n] = _get_residue(acc_ref, r, dil).astype(BF16)

    return pl.pallas_call(
        body, name=name, grid=(m // tm,),
        in_specs=[pl.BlockSpec((tm, k), lambda i: (i, 0)), pl.BlockSpec((k, n), lambda i: (0, 0))],
        out_specs=pl.BlockSpec((tm // dil, dil * n), lambda i: (i, 0)),
        out_shape=jax.ShapeDtypeStruct((m // dil, dil * n), BF16), scratch_shapes=[_tile_scratch(tm, n)],
        compiler_params=_params("parallel"),
    )(a, b)


def _attn_fwd(pv, group, *, name):
    dil = DILATIONS[group]
    length = pv.shape[0]
    cb, qoff = pv.shape[1] // (128 * dil), 0
    win = min(2 * A_QBLOCK, length)
    nblk = length // A_QBLOCK
    per = A_BLOCKS_PER_STEP if nblk % A_BLOCKS_PER_STEP == 0 else 1

    def body(slope_ref, q_ref, k_ref, v_ref, o_ref, lse_ref):
        hp = pl.program_id(1)
        for u in range(per):
            rows = slice(u * A_QBLOCK, (u + 1) * A_QBLOCK)
            start, dist, valid = _attn_window(pl.program_id(2) * per + u, length, win, dil)
            kw = k_ref[pl.ds(start, win), :]
            vw = v_ref[pl.ds(start, win), :]
            q = q_ref[rows, :]
            outs, lses = [], []
            for hh in range(2):
                sl = slice(hh * A_HEAD_DIM, (hh + 1) * A_HEAD_DIM)
                sc = _attn_scores(q[:, sl], kw[:, sl], slope_ref[hp * 2 + hh], dist, valid)
                m = jnp.max(sc, axis=-1, keepdims=True)
                p = jnp.exp(sc - m)
                z = jnp.sum(p, axis=-1, keepdims=True)
                o = jnp.dot(p.astype(BF16), vw[:, sl], preferred_element_type=F32) / z
                outs.append(o)
                lses.append(jnp.broadcast_to(m + jnp.log(z), (A_QBLOCK, A_HEAD_DIM)))
            o_ref[rows, :] = jnp.concatenate(outs, axis=1)
            lse_ref[rows, :] = jnp.concatenate(lses, axis=1)

    qspec = pl.BlockSpec((per * A_QBLOCK, 128), lambda r, hp, b: (b, r * cb + qoff + hp))
    kspec = pl.BlockSpec((length, 128), lambda r, hp, b: (0, r * cb + qoff + 8 + hp))
    vspec = pl.BlockSpec((length, 128), lambda r, hp, b: (0, r * cb + qoff + 16 + hp))
    ospec = pl.BlockSpec((per * A_QBLOCK, 128), lambda r, hp, b: (b, r * 8 + hp))
    oshape = jax.ShapeDtypeStruct((length, dil * A_WIDTH), F32)
    o, lse = pl.pallas_call(
        body, name=name, grid=(dil, 8, nblk // per),
        in_specs=[pl.BlockSpec(memory_space=pltpu.SMEM), qspec, kspec, vspec], out_specs=[ospec, ospec],
        out_shape=[oshape, oshape], compiler_params=_params("parallel", "parallel", "arbitrary"),
    )(jnp.asarray(_SLOPES), pv, pv, pv)
    return o, lse


def _attn_bwd(pv, group, do, o, lse, *, name):
    dil = DILATIONS[group]
    length = pv.shape[0]
    cb, qoff = pv.shape[1] // (128 * dil), 0
    win = min(2 * A_QBLOCK, length)
    nblk = length // A_QBLOCK
    per = A_BLOCKS_PER_STEP if nblk % A_BLOCKS_PER_STEP == 0 else 1
    nstep = nblk // per
    scale = 1.0 / math.sqrt(A_HEAD_DIM)

    def body(slope_ref, q_ref, k_ref, v_ref, do_ref, o_ref, lse_ref, dq_ref, dk_ref, dv_ref, dk_acc, dv_acc):
        hp, step = pl.program_id(1), pl.program_id(2)

        @pl.when(step == 0)
        def _():
            dk_acc[...] = jnp.zeros_like(dk_acc)
            dv_acc[...] = jnp.zeros_like(dv_acc)

        for u in range(per):
            rows = slice(u * A_QBLOCK, (u + 1) * A_QBLOCK)
            start, dist, valid = _attn_window(step * per + u, length, win, dil)
            kw = k_ref[pl.ds(start, win), :]
            vw = v_ref[pl.ds(start, win), :]
            q = q_ref[rows, :]
            do_b = do_ref[rows, :]
            dsum = do_b.astype(F32) * o_ref[rows, :]
            lse_b = lse_ref[rows, :]
            dqs, dks, dvs = [], [], []
            for hh in range(2):
                sl = slice(hh * A_HEAD_DIM, (hh + 1) * A_HEAD_DIM)
                sc = _attn_scores(q[:, sl], kw[:, sl], slope_ref[hp * 2 + hh], dist, valid)
                p = jnp.exp(sc - lse_b[:, hh * A_HEAD_DIM:hh * A_HEAD_DIM + 1])
                dp = lax.dot_general(do_b[:, sl], vw[:, sl], _DIMS["nt"], preferred_element_type=F32)
                ds = (p * (dp - jnp.sum(dsum[:, sl], axis=-1, keepdims=True))).astype(BF16)
                dqs.append(jnp.dot(ds, kw[:, sl], preferred_element_type=F32) * scale)
                dks.append(lax.dot_general(ds, q[:, sl], _DIMS["tn"], preferred_element_type=F32) * scale)
                dvs.append(lax.dot_general(p.astype(BF16), do_b[:, sl], _DIMS["tn"], preferred_element_type=F32))
            dq_ref[rows, :] = jnp.concatenate(dqs, axis=1).astype(BF16)
            dk_acc[pl.ds(start, win), :] += jnp.concatenate(dks, axis=1)
            dv_acc[pl.ds(start, win), :] += jnp.concatenate(dvs, axis=1)

        @pl.when(step == nstep - 1)
        def _():
            dk_ref[...] = dk_acc[...].astype(BF16)
            dv_ref[...] = dv_acc[...].astype(BF16)

    qspec = pl.BlockSpec((per * A_QBLOCK, 128), lambda r, hp, b: (b, r * cb + qoff + hp))
    kspec = pl.BlockSpec((length, 128), lambda r, hp, b: (0, r * cb + qoff + 8 + hp))
    vspec = pl.BlockSpec((length, 128), lambda r, hp, b: (0, r * cb + qoff + 16 + hp))
    bspec = pl.BlockSpec((per * A_QBLOCK, 128), lambda r, hp, b: (b, r * 8 + hp))
    fspec = pl.BlockSpec((length, 128), lambda r, hp, b: (0, r * 8 + hp))
    oshape = jax.ShapeDtypeStruct((length, dil * A_WIDTH), BF16)
    dq, dk, dv = pl.pallas_call(
        body, name=name, grid=(dil, 8, nstep),
        in_specs=[pl.BlockSpec(memory_space=pltpu.SMEM), qspec, kspec, vspec, bspec, bspec, bspec],
        out_specs=[bspec, fspec, fspec], out_shape=[oshape, oshape, oshape],
        scratch_shapes=[pltpu.VMEM((length, 128), F32), pltpu.VMEM((length, 128), F32)],
        compiler_params=_params("parallel", "parallel", "arbitrary"),
    )(jnp.asarray(_SLOPES), pv, pv, pv, do, o, lse)
    return dq, dk, dv


A_GATE_BLOCK = 3
A_ROWS = 256


def _lanes_of(r):
    return slice(r * A_WIDTH, (r + 1) * A_WIDTH)


def _dilated_spec(dil):
    return pl.BlockSpec((A_ROWS // dil, dil * A_WIDTH), lambda i: (i, 0))


def _attn_combine(o0, l0, o1, l1, o2, l2, proj0, *, name):
    s = proj0.shape[0]

    def body(o0_ref, l0_ref, o1_ref, l1_ref, o2_ref, l2_ref, gate_ref, y_ref, o_ref, lse_ref, so1, sl1, so2, sl2):
        for src, dst, dil in ((o1_ref, so1, DILATIONS[1]), (l1_ref, sl1, DILATIONS[1]),
                              (o2_ref, so2, DILATIONS[2]), (l2_ref, sl2, DILATIONS[2])):
            for r in range(dil):
                _put_residue(dst, r, dil, src[:, _lanes_of(r)])
        la, lb, lc = l0_ref[...], _get_tile(sl1), _get_tile(sl2)
        m = jnp.maximum(jnp.maximum(la, lb), lc)
        ea, eb, ec = jnp.exp(la - m), jnp.exp(lb - m), jnp.exp(lc - m)
        den = ea + eb + ec
        o = (ea * o0_ref[...] + eb * _get_tile(so1) + ec * _get_tile(so2)) / den
        o_ref[...] = o
        lse_ref[...] = m + jnp.log(den)
        y_ref[...] = (o * _silu_and_grad(gate_ref[...].astype(F32))[0]).astype(BF16)

    row = pl.BlockSpec((A_ROWS, A_WIDTH), lambda i: (i, 0))
    gspec = pl.BlockSpec((A_ROWS, A_WIDTH), lambda i: (i, A_GATE_BLOCK))
    d1, d2 = _dilated_spec(DILATIONS[1]), _dilated_spec(DILATIONS[2])
    return pl.pallas_call(
        body, name=name, grid=(s // A_ROWS,), in_specs=[row, row, d1, d1, d2, d2, gspec], out_specs=[row, row, row],
        out_shape=[jax.ShapeDtypeStruct((s, A_WIDTH), BF16), jax.ShapeDtypeStruct((s, A_WIDTH), F32),
                   jax.ShapeDtypeStruct((s, A_WIDTH), F32)],
        scratch_shapes=[_tile_scratch(A_ROWS, A_WIDTH)] * 4, compiler_params=_params("parallel"),
    )(o0, l0, o1, l1, o2, l2, proj0)


def _attn_combine_bwd(dy, o, lse, proj0, *, name):
    s = proj0.shape[0]

    def body(dy_ref, o_ref, lse_ref, gate_ref, dg_ref, do_ref, do1, o1, l1, do2, o2, l2, s_do, s_o, s_l):
        si, dsi = _silu_and_grad(gate_ref[...].astype(F32))
        dyv = dy_ref[...]
        ov = o_ref[...]
        do = dyv * si
        _put_tile(s_do, do)
        _put_tile(s_o, ov)
        _put_tile(s_l, lse_ref[...])
        do_ref[...] = do.astype(BF16)
        dg_ref[...] = (dyv * ov * dsi).astype(BF16)
        for (do_d, o_d, l_d), dil in (((do1, o1, l1), DILATIONS[1]), ((do2, o2, l2), DILATIONS[2])):
            for r in range(dil):
                do_d[:, _lanes_of(r)] = _get_residue(s_do, r, dil).astype(BF16)
                o_d[:, _lanes_of(r)] = _get_residue(s_o, r, dil)
                l_d[:, _lanes_of(r)] = _get_residue(s_l, r, dil)

    row = pl.BlockSpec((A_ROWS, A_WIDTH), lambda i: (i, 0))
    gspec = pl.BlockSpec((A_ROWS, A_WIDTH), lambda i: (i, A_GATE_BLOCK))
    shp = jax.ShapeDtypeStruct((s, A_WIDTH), BF16)
    dilated = lambda dil, dtype: jax.ShapeDtypeStruct((s // dil, dil * A_WIDTH), dtype)
    d1, d2 = _dilated_spec(DILATIONS[1]), _dilated_spec(DILATIONS[2])
    return pl.pallas_call(
        body, name=name, grid=(s // A_ROWS,), in_specs=[row, row, row, gspec],
        out_specs=[row, row, d1, d1, d1, d2, d2, d2],
        out_shape=[shp, shp, dilated(DILATIONS[1], BF16), dilated(DILATIONS[1], F32), dilated(DILATIONS[1], F32),
                   dilated(DILATIONS[2], BF16), dilated(DILATIONS[2], F32), dilated(DILATIONS[2], F32)],
        scratch_shapes=[_tile_scratch(A_ROWS, A_WIDTH)] * 3, compiler_params=_params("parallel"),
    )(dy, o, lse, proj0)


def _assemble_dproj(parts0, parts1, parts2, dgate, *, name):
    s = dgate.shape[0]

    def body(*refs):
        ins, out_ref, scr = refs[:10], refs[10], refs[11]
        for p in range(3):
            out_ref[:, _lanes_of(p)] = ins[p][...]
        for g, dil in ((1, DILATIONS[1]), (2, DILATIONS[2])):
            for p in range(3):
                src = ins[3 * g + p]
                for r in range(dil):
                    _put_residue(scr, r, dil, src[:, _lanes_of(r)].astype(F32))
                out_ref[:, _lanes_of(3 * g + p)] = _get_tile(scr).astype(BF16)
        out_ref[:, _lanes_of(9)] = ins[9][...]

    row = pl.BlockSpec((A_ROWS, A_WIDTH), lambda i: (i, 0))
    d1, d2 = _dilated_spec(DILATIONS[1]), _dilated_spec(DILATIONS[2])
    return pl.pallas_call(
        body, name=name, grid=(s // A_ROWS,), in_specs=[row] * 3 + [d1] * 3 + [d2] * 3 + [row],
        out_specs=pl.BlockSpec((A_ROWS, A_IN_COLS), lambda i: (i, 0)),
        out_shape=jax.ShapeDtypeStruct((s, A_IN_COLS), BF16),
        scratch_shapes=[_tile_scratch(A_ROWS, A_WIDTH)], compiler_params=_params("parallel"),
    )(*parts0, *parts1, *parts2, dgate)


CONV_TILE = 256
CONV_SUB = 4


def _conv_taps(xe, n):
    return [xe if j == 2 else pltpu.roll(xe, (2 - j) % n, 0) for j in range(SSM_CONV)]


def _conv_fwd(xpad, w, b, *, name):
    s = xpad.shape[0] - 2 * CONV_HALO
    n = CONV_TILE + 2 * CONV_HALO
    ncol = SSM_CONV_DIM // 128

    sub = min(CONV_SUB, s // CONV_TILE)

    def body(x_ref, w_ref, b_ref, o_ref):
        base = pl.program_id(1) * (sub * CONV_TILE)

        def tile(k, carry):
            r0 = pl.multiple_of(k * CONV_TILE, CONV_TILE)
            t0 = pl.multiple_of(base + r0, CONV_TILE)
            taps = _conv_taps(x_ref[pl.ds(t0, n), :].astype(F32), n)
            pre = b_ref[...]
            for j in range(SSM_CONV):
                pre = pre + w_ref[j:j + 1, :] * taps[j]
            o_ref[pl.ds(r0, CONV_TILE), :] = _silu_and_grad(pre[CONV_HALO:CONV_HALO + CONV_TILE])[0].astype(BF16)
            return carry

        lax.fori_loop(0, sub, tile, 0)

    return pl.pallas_call(
        body, name=name, grid=(ncol, s // (sub * CONV_TILE)),
        in_specs=[pl.BlockSpec((s + 2 * CONV_HALO, 128), lambda j, i: (0, j)),
                  pl.BlockSpec((SSM_CONV, 128), lambda j, i: (0, j)), pl.BlockSpec((1, 128), lambda j, i: (0, j))],
        out_specs=pl.BlockSpec((sub * CONV_TILE, 128), lambda j, i: (i, j)),
        out_shape=jax.ShapeDtypeStruct((s, SSM_CONV_DIM), BF16), compiler_params=_params("parallel", "arbitrary"),
    )(xpad, w, b)


def _conv_bwd(xpad, dapad, w, b, *, name):
    s = xpad.shape[0] - 2 * CONV_HALO
    n = CONV_TILE + 2 * CONV_HALO
    ncol = SSM_CONV_DIM // 128
    mid = slice(CONV_HALO, CONV_HALO + CONV_TILE)
    sub = min(CONV_SUB, s // CONV_TILE)

    def body(x_ref, da_ref, w_ref, b_ref, dx_ref, dw_ref, db_ref):
        @pl.when(pl.program_id(1) == 0)
        def _():
            dw_ref[...] = jnp.zeros_like(dw_ref)
            db_ref[...] = jnp.zeros_like(db_ref)

        base = pl.program_id(1) * (sub * CONV_TILE)

        def tile(k, carry):
            r0 = pl.multiple_of(k * CONV_TILE, CONV_TILE)
            t0 = pl.multiple_of(base + r0, CONV_TILE)
            taps = _conv_taps(x_ref[pl.ds(t0, n), :].astype(F32), n)
            pre = b_ref[...]
            for j in range(SSM_CONV):
                pre = pre + w_ref[j:j + 1, :] * taps[j]
            dpre = da_ref[pl.ds(t0, n), :] * _silu_and_grad(pre)[1]
            dx = jnp.zeros((CONV_TILE, 128), F32)
            for j in range(SSM_CONV):
                back = dpre if j == 2 else pltpu.roll(dpre, (j - 2) % n, 0)
                dx = dx + w_ref[j:j + 1, :] * back[mid]
                dw_ref[j:j + 1, :] += jnp.sum(dpre[mid] * taps[j][mid], axis=0, keepdims=True)
            dx_ref[pl.ds(r0, CONV_TILE), :] = dx.astype(BF16)
            db_ref[...] += jnp.sum(dpre[mid], axis=0, keepdims=True)
            return carry

        lax.fori_loop(0, sub, tile, 0)

    full = pl.BlockSpec((s + 2 * CONV_HALO, 128), lambda j, i: (0, j))
    wspec = pl.BlockSpec((SSM_CONV, 128), lambda j, i: (0, j))
    bspec = pl.BlockSpec((1, 128), lambda j, i: (0, j))
    return pl.pallas_call(
        body, name=name, grid=(ncol, s // (sub * CONV_TILE)), in_specs=[full, full, wspec, bspec],
        out_specs=[pl.BlockSpec((sub * CONV_TILE, 128), lambda j, i: (i, j)), wspec, bspec],
        out_shape=[jax.ShapeDtypeStruct((s, SSM_CONV_DIM), BF16), jax.ShapeDtypeStruct((SSM_CONV, SSM_CONV_DIM), F32),
                   jax.ShapeDtypeStruct((1, SSM_CONV_DIM), F32)],
        compiler_params=_params("parallel", "arbitrary"),
    )(xpad, dapad, w, b)


HPG = SSM_HEADS // SSM_GROUPS
GW = HPG * SSM_HEAD_DIM
T = SSM_CHUNK


def _ssd_specs(nc):
    ceff = lambda d, c: jnp.where(d == 0, c, nc - 1 - c)
    return ceff, [
        pl.BlockSpec((T, GW), lambda d, g, c: (ceff(d, c), g)),
        pl.BlockSpec((T, SSM_STATE), lambda d, g, c: (ceff(d, c), SSM_INNER // 128 + g)),
        pl.BlockSpec((T, SSM_STATE), lambda d, g, c: (ceff(d, c), SSM_INNER // 128 + SSM_GROUPS + g)),
        pl.BlockSpec((None, None, T, HPG), lambda d, g, c: (d, g, ceff(d, c), 0)),
        pl.BlockSpec((None, None, HPG, T), lambda d, g, c: (d, g, 0, ceff(d, c))),
        pl.BlockSpec((None, None, 2, HPG), lambda d, g, c: (d, g, 0, 0)),
        pl.BlockSpec((None, None, HPG, 2), lambda d, g, c: (d, g, 0, 0)),
    ]


def _ssd_chunk_common(d, dt_ref, dtt_ref, prr_ref, prc_ref):
    sgn = 1 - 2 * d
    ri = lax.broadcasted_iota(jnp.int32, (T, T), 0)
    ci = lax.broadcasted_iota(jnp.int32, (T, T), 1)
    mask = ((ri - ci) * sgn) >= 0
    maskf = mask.astype(F32)
    bias_r, a_r = prr_ref[0:1, :], prr_ref[1:2, :]
    bias_c, a_c = prc_ref[:, 0:1], prc_ref[:, 1:2]
    raw = dt_ref[...] + bias_r
    dt_rows = _softplus(raw)
    dt_lanes = _softplus(dtt_ref[...] + bias_c)
    a_rows = dt_rows * a_r
    acum_rows = jnp.dot(maskf, a_rows, precision=HIGHEST, preferred_element_type=F32)
    acum_lanes = lax.dot_general(dt_lanes * a_c, maskf, _DIMS["nt"], precision=HIGHEST, preferred_element_type=F32)
    tot = jnp.sum(a_rows, axis=0, keepdims=True)
    return mask, maskf, raw, dt_rows, a_r, acum_rows, acum_lanes, tot


def _lanes_per_head(pieces):
    return jnp.concatenate([jnp.broadcast_to(p, (p.shape[0], SSM_HEAD_DIM)) for p in pieces], axis=1)


def _ssd_fwd_v1(xbc, dtr, dtt, pr_rows, pr_cols, *, name):
    s = xbc.shape[0]
    nc = s // T
    ceff, in_specs = _ssd_specs(nc)

    def body(x_ref, b_ref, c_ref, dt_ref, dtt_ref, prr_ref, prc_ref, y_ref, hs_ref, st_ref):
        d, c = pl.program_id(0), pl.program_id(2)

        @pl.when(c == 0)
        def _():
            st_ref[...] = jnp.zeros_like(st_ref)

        mask, _, _, dt_rows, _, acum_rows, acum_lanes, tot = _ssd_chunk_common(d, dt_ref, dtt_ref, prr_ref, prc_ref)
        xs = x_ref[...].astype(F32)
        bm, cm = b_ref[...], c_ref[...]
        hprev = st_ref[...]
        hs_ref[...] = hprev
        cb = lax.dot_general(cm, bm, _DIMS["nt"], preferred_element_type=F32)
        ch = jnp.dot(cm, hprev.astype(BF16), preferred_element_type=F32)
        ys, xgds, etots = [], [], []
        for j in range(HPG):
            sl = slice(j * SSM_HEAD_DIM, (j + 1) * SSM_HEAD_DIM)
            ac, al = acum_rows[:, j:j + 1], acum_lanes[j:j + 1, :]
            lm = jnp.where(mask, jnp.exp(jnp.minimum(ac - al, 0.0)), 0.0)
            xg = xs[:, sl] * dt_rows[:, j:j + 1]
            yd = jnp.dot((cb * lm).astype(BF16), xg.astype(BF16), preferred_element_type=F32)
            ys.append(yd + jnp.exp(ac) * ch[:, sl])
            xgds.append(xg * jnp.exp(tot[:, j:j + 1] - ac))
            etots.append(jnp.exp(tot[:, j:j + 1]))
        y_ref[...] = jnp.concatenate(ys, axis=1)
        new = lax.dot_general(bm, jnp.concatenate(xgds, axis=1).astype(BF16), _DIMS["tn"], preferred_element_type=F32)
        st_ref[...] = hprev * _lanes_per_head(etots) + new

    return pl.pallas_call(
        body, name=name, grid=(2, SSM_GROUPS, nc), in_specs=in_specs,
        out_specs=[pl.BlockSpec((None, T, GW), lambda d, g, c: (d, ceff(d, c), g)),
                   pl.BlockSpec((None, None, None, SSM_STATE, GW), lambda d, g, c: (d, ceff(d, c), g, 0, 0))],
        out_shape=[jax.ShapeDtypeStruct((2, s, SSM_INNER), F32),
                   jax.ShapeDtypeStruct((2, nc, SSM_GROUPS, SSM_STATE, GW), F32)],
        scratch_shapes=[pltpu.VMEM((SSM_STATE, GW), F32)],
        compiler_params=_params("parallel", "parallel", "arbitrary"),
    )(xbc, xbc, xbc, dtr, dtt, pr_rows, pr_cols)


def _put_lane(j, col):
    lane = lax.broadcasted_iota(jnp.int32, (col.shape[0], HPG), 1)
    return jnp.where(lane == j, col, 0.0)


def _ssd_bwd_v1(xbc, dtr, dtt, pr_rows, pr_cols, dvec, hs, dy, *, name):
    s = xbc.shape[0]
    nc = s // T
    cb_of = lambda d, c: jnp.where(d == 0, nc - 1 - c, c)
    in_specs = [
        pl.BlockSpec((T, GW), lambda d, g, c: (cb_of(d, c), g)),
        pl.BlockSpec((T, SSM_STATE), lambda d, g, c: (cb_of(d, c), SSM_INNER // 128 + g)),
        pl.BlockSpec((T, SSM_STATE), lambda d, g, c: (cb_of(d, c), SSM_INNER // 128 + SSM_GROUPS + g)),
        pl.BlockSpec((None, None, T, HPG), lambda d, g, c: (d, g, cb_of(d, c), 0)),
        pl.BlockSpec((None, None, HPG, T), lambda d, g, c: (d, g, 0, cb_of(d, c))),
        pl.BlockSpec((None, None, 2, HPG), lambda d, g, c: (d, g, 0, 0)),
        pl.BlockSpec((None, None, HPG, 2), lambda d, g, c: (d, g, 0, 0)),
        pl.BlockSpec((1, GW), lambda d, g, c: (0, g)),
        pl.BlockSpec((None, None, None, SSM_STATE, GW), lambda d, g, c: (d, cb_of(d, c), g, 0, 0)),
        pl.BlockSpec((T, GW), lambda d, g, c: (cb_of(d, c), g)),
    ]

    def body(x_ref, b_ref, c_ref, dt_ref, dtt_ref, prr_ref, prc_ref, dvec_ref, hs_ref, dy_ref,
             dxs_ref, db_ref, dc_ref, ddt_ref, dalog_ref, dbias_ref, g_ref):
        d, c = pl.program_id(0), pl.program_id(2)

        @pl.when(c == 0)
        def _():
            g_ref[...] = jnp.zeros_like(g_ref)
            dalog_ref[...] = jnp.zeros_like(dalog_ref)
            dbias_ref[...] = jnp.zeros_like(dbias_ref)

        mask, maskf, raw, dt_rows, a_r, acum_rows, acum_lanes, tot = _ssd_chunk_common(
            d, dt_ref, dtt_ref, prr_ref, prc_ref)
        xs = x_ref[...].astype(F32)
        bm, cm = b_ref[...], c_ref[...]
        hst = hs_ref[...]
        gst = g_ref[...]
        dyv = dy_ref[...]
        dyb = dyv.astype(BF16)
        dv = dvec_ref[...] * (1 - d).astype(F32)
        cb = lax.dot_general(cm, bm, _DIMS["nt"], preferred_element_type=F32)
        ch = jnp.dot(cm, hst.astype(BF16), preferred_element_type=F32)
        bg = jnp.dot(bm, gst.astype(BF16), preferred_element_type=F32)
        hg = hst * gst
        dcb = jnp.zeros((T, T), F32)
        dacum = jnp.zeros((T, HPG), F32)
        rx = jnp.zeros((T, HPG), F32)
        dtot = jnp.zeros((1, HPG), F32)
        dxss, dyes, xgds, etots = [], [], [], []
        for j in range(HPG):
            sl = slice(j * SSM_HEAD_DIM, (j + 1) * SSM_HEAD_DIM)
            ac, al = acum_rows[:, j:j + 1], acum_lanes[j:j + 1, :]
            lm = jnp.where(mask, jnp.exp(jnp.minimum(ac - al, 0.0)), 0.0)
            m = cb * lm
            dtj = dt_rows[:, j:j + 1]
            xsj = xs[:, sl]
            xg = xsj * dtj
            dyj = dyv[:, sl]
            ec = jnp.exp(ac)
            etot = jnp.exp(tot[:, j:j + 1])
            decay = jnp.exp(tot[:, j:j + 1] - ac)
            dm = lax.dot_general(dyb[:, sl], xg.astype(BF16), _DIMS["nt"], preferred_element_type=F32)
            w = dm * m
            dcb = dcb + dm * lm
            bgj = bg[:, sl]
            dxg = lax.dot_general(m.astype(BF16), dyb[:, sl], _DIMS["tn"], preferred_element_type=F32) + decay * bgj
            xb = decay * jnp.sum(xg * bgj, axis=-1, keepdims=True)
            da_j = (jnp.sum(w, axis=-1, keepdims=True) - jnp.sum(w.T, axis=-1, keepdims=True)
                    + jnp.sum(ec * ch[:, sl] * dyj, axis=-1, keepdims=True) - xb)
            dacum = dacum + _put_lane(j, da_j)
            rx = rx + _put_lane(j, jnp.sum(dxg * xsj, axis=-1, keepdims=True))
            dtot_j = (etot * jnp.sum(jnp.sum(hg[:, sl], axis=0, keepdims=True), axis=1, keepdims=True)
                      + jnp.sum(xb, axis=0, keepdims=True))
            dtot = dtot + _put_lane(j, dtot_j)
            dxss.append(dxg * dtj + dv[:, sl] * dyj)
            dyes.append(dyj * ec)
            xgds.append(xg * decay)
            etots.append(etot)
        da = lax.dot_general(maskf, dacum, _DIMS["tn"], precision=HIGHEST, preferred_element_type=F32) + dtot
        ddt = da * a_r + rx
        draw = ddt * _sigmoid(raw)
        ddt_ref[...] = draw
        dbias_ref[...] += jnp.sum(draw, axis=0, keepdims=True)
        dalog_ref[...] += jnp.sum(da * dt_rows, axis=0, keepdims=True) * a_r
        dxs_ref[...] = jnp.concatenate(dxss, axis=1)
        dye = jnp.concatenate(dyes, axis=1).astype(BF16)
        xgd = jnp.concatenate(xgds, axis=1).astype(BF16)
        dcbb = dcb.astype(BF16)
        dc_ref[...] = (jnp.dot(dcbb, bm, preferred_element_type=F32)
                       + lax.dot_general(dye, hst.astype(BF16), _DIMS["nt"], preferred_element_type=F32))
        db_ref[...] = (lax.dot_general(dcbb, cm, _DIMS["tn"], preferred_element_type=F32)
                       + lax.dot_general(xgd, gst.astype(BF16), _DIMS["nt"], preferred_element_type=F32))
        g_ref[...] = lax.dot_general(cm, dye, _DIMS["tn"], preferred_element_type=F32) + gst * _lanes_per_head(etots)

    small = pl.BlockSpec((None, None, 1, HPG), lambda d, g, c: (d, g, 0, 0))
    sshape = jax.ShapeDtypeStruct((2, SSM_GROUPS, 1, HPG), F32)
    return pl.pallas_call(
        body, name=name, grid=(2, SSM_GROUPS, nc), in_specs=in_specs,
        out_specs=[pl.BlockSpec((None, T, GW), lambda d, g, c: (d, cb_of(d, c), g)),
                   pl.BlockSpec((None, T, SSM_STATE), lambda d, g, c: (d, cb_of(d, c), g)),
                   pl.BlockSpec((None, T, SSM_STATE), lambda d, g, c: (d, cb_of(d, c), g)),
                   pl.BlockSpec((None, None, T, HPG), lambda d, g, c: (d, g, cb_of(d, c), 0)), small, small],
        out_shape=[jax.ShapeDtypeStruct((2, s, SSM_INNER), F32),
                   jax.ShapeDtypeStruct((2, s, SSM_GROUPS * SSM_STATE), F32),
                   jax.ShapeDtypeStruct((2, s, SSM_GROUPS * SSM_STATE), F32),
                   jax.ShapeDtypeStruct((2, SSM_GROUPS, s, HPG), F32), sshape, sshape],
        scratch_shapes=[pltpu.VMEM((SSM_STATE, GW), F32)],
        compiler_params=_params("parallel", "parallel", "arbitrary"),
    )(xbc, xbc, xbc, dtr, dtt, pr_rows, pr_cols, dvec, hs, dy)


PAIRS = HPG // 2


def _scan_lanes(x, forward):
    lane = lax.broadcasted_iota(jnp.int32, x.shape, 1)
    p = x
    k = 1
    while k < T:
        p = p + jnp.where(lane >= k, pltpu.roll(p, k, 1), 0.0)
        k *= 2
    tot = p[:, T - 1:T]
    return jnp.where(forward, p, tot - p + x), tot


def _ssd_chunk(d, dt_ref, dtt_ref, prr_ref, prc_ref):
    sgn = 1 - 2 * d
    ri = lax.broadcasted_iota(jnp.int32, (T, T), 0)
    ci = lax.broadcasted_iota(jnp.int32, (T, T), 1)
    mask = ((ri - ci) * sgn) >= 0
    mask_t = ((ci - ri) * sgn) >= 0
    bias_r = prr_ref[0:1, :]
    bias_c, a_c = prc_ref[:, 0:1], prc_ref[:, 1:2]
    dt_rows = _softplus(dt_ref[...] + bias_r)
    raw_lanes = dtt_ref[...] + bias_c
    dt_lanes = _softplus(raw_lanes)
    acum_lanes, tot = _scan_lanes(dt_lanes * a_c, d == 0)
    return dict(mask=mask, mask_t=mask_t, head0=ci < SSM_HEAD_DIM, dt_rows=dt_rows, raw_lanes=raw_lanes,
                dt_lanes=dt_lanes, a_c=a_c, acum_lanes=acum_lanes, acum_rows=acum_lanes.T, tot=tot)


def _ssd_pair(ck, q):
    h0 = ck["head0"]
    colb = lambda rows, j: jnp.broadcast_to(rows[:, j:j + 1], (T, T))
    rowb = lambda lanes, j: jnp.broadcast_to(lanes[j:j + 1, :], (T, T))
    lms, lmts, acs = [], [], []
    for j in (2 * q, 2 * q + 1):
        ac, al = colb(ck["acum_rows"], j), rowb(ck["acum_lanes"], j)
        lms.append(jnp.where(ck["mask"], jnp.exp(jnp.minimum(ac - al, 0.0)), 0.0))
        lmts.append(jnp.where(ck["mask_t"], jnp.exp(jnp.minimum(al - ac, 0.0)), 0.0))
        acs.append(ac)
    ac_pair = jnp.where(h0, acs[0], acs[1])
    dt_pair = jnp.where(h0, colb(ck["dt_rows"], 2 * q), colb(ck["dt_rows"], 2 * q + 1))
    tot_pair = jnp.where(h0[0:1], ck["tot"][2 * q:2 * q + 1, :], ck["tot"][2 * q + 1:2 * q + 2, :])
    return dict(lm=lms, lmt=lmts, dt=dt_pair, ec=jnp.exp(ac_pair), decay=jnp.exp(tot_pair - ac_pair),
                etot=jnp.exp(tot_pair))


def _split_heads(h0, v):
    zero = jnp.zeros_like(v)
    return jnp.where(h0, v, zero), jnp.where(h0, zero, v)


GPS = 2
GSTEPS = SSM_GROUPS // GPS
B_BLOCK0 = SSM_INNER // (GPS * SSM_STATE)
C_BLOCK0 = (SSM_INNER + SSM_GROUPS * SSM_STATE) // (GPS * SSM_STATE)


def _ssd_in_specs(chunk):
    return [
        pl.BlockSpec((T, GPS * GW), lambda d, g, c: (chunk(d, c), g)),
        pl.BlockSpec((T, GPS * SSM_STATE), lambda d, g, c: (chunk(d, c), B_BLOCK0 + g)),
        pl.BlockSpec((T, GPS * SSM_STATE), lambda d, g, c: (chunk(d, c), C_BLOCK0 + g)),
        pl.BlockSpec((GPS * SSM_STATE, T), lambda d, g, c: (g, chunk(d, c))),
        pl.BlockSpec((GPS * SSM_STATE, T), lambda d, g, c: (g, chunk(d, c))),
        pl.BlockSpec((None, GPS, T, HPG), lambda d, g, c: (d, g, chunk(d, c), 0)),
        pl.BlockSpec((None, GPS, HPG, T), lambda d, g, c: (d, g, 0, chunk(d, c))),
        pl.BlockSpec((None, GPS, 2, HPG), lambda d, g, c: (d, g, 0, 0)),
        pl.BlockSpec((None, GPS, HPG, 2), lambda d, g, c: (d, g, 0, 0)),
    ]


def _group_refs(gi, wide, state_wide, t_wide, lead):
    return ([r.at[:, pl.ds(gi * GW, GW)] for r in wide] + [r.at[:, pl.ds(gi * SSM_STATE, SSM_STATE)] for r in state_wide]
            + [r.at[pl.ds(gi * SSM_STATE, SSM_STATE), :] for r in t_wide] + [r.at[gi] for r in lead])


def _ssd_fwd(xbc, bt, ct, dtr, dtt, pr_rows, pr_cols, *, name):
    s = xbc.shape[0]
    nc = s // T
    chunk = lambda d, c: jnp.where(d == 0, c, nc - 1 - c)

    def body(x_ref, b_ref, c_ref, bt_ref, ct_ref, dt_ref, dtt_ref, prr_ref, prc_ref, y_ref, hs_ref, st_ref):
        @pl.when(pl.program_id(2) == 0)
        def _():
            st_ref[...] = jnp.zeros_like(st_ref)

        for gi in range(GPS):
            group(*_group_refs(gi, [x_ref, y_ref], [b_ref, c_ref], [bt_ref, ct_ref],
                               [dt_ref, dtt_ref, prr_ref, prc_ref, hs_ref, st_ref]))

    def group(x_ref, y_ref, b_ref, c_ref, bt_ref, ct_ref, dt_ref, dtt_ref, prr_ref, prc_ref, hs_ref, st_ref):
        d = pl.program_id(0)
        ck = _ssd_chunk(d, dt_ref, dtt_ref, prr_ref, prc_ref)
        xs = x_ref[...].astype(F32)
        cm = c_ref[...]
        hprev = st_ref[...]
        hs_ref[...] = hprev
        cb = lax.dot_general(cm, b_ref[...], _DIMS["nt"], preferred_element_type=F32)
        ch = jnp.dot(cm, hprev.astype(BF16), preferred_element_type=F32)
        ys, xgds, etots = [], [], []
        for q in range(PAIRS):
            sl = slice(q * 128, (q + 1) * 128)
            pr = _ssd_pair(ck, q)
            xg = xs[:, sl] * pr["dt"]
            xg0, xg1 = _split_heads(ck["head0"], xg.astype(BF16))
            yd = (jnp.dot((cb * pr["lm"][0]).astype(BF16), xg0, preferred_element_type=F32)
                  + jnp.dot((cb * pr["lm"][1]).astype(BF16), xg1, preferred_element_type=F32))
            ys.append(yd + pr["ec"] * ch[:, sl])
            xgds.append(xg * pr["decay"])
            etots.append(pr["etot"])
        y_ref[...] = jnp.concatenate(ys, axis=1)
        new = jnp.dot(bt_ref[...], jnp.concatenate(xgds, axis=1).astype(BF16), preferred_element_type=F32)
        st_ref[...] = hprev * jnp.concatenate(etots, axis=1) + new

    return pl.pallas_call(
        body, name=name, grid=(2, GSTEPS, nc), in_specs=_ssd_in_specs(chunk),
        out_specs=[pl.BlockSpec((None, T, GPS * GW), lambda d, g, c: (d, chunk(d, c), g)),
                   pl.BlockSpec((None, None, GPS, SSM_STATE, GW), lambda d, g, c: (d, chunk(d, c), g, 0, 0))],
        out_shape=[jax.ShapeDtypeStruct((2, s, SSM_INNER), F32),
                   jax.ShapeDtypeStruct((2, nc, SSM_GROUPS, SSM_STATE, GW), F32)],
        scratch_shapes=[pltpu.VMEM((GPS, SSM_STATE, GW), F32)],
        compiler_params=_params("parallel", "parallel", "arbitrary"),
    )(xbc, xbc, xbc, bt, ct, dtr, dtt, pr_rows, pr_cols)


def _ssd_bwd(xbc, bt, ct, dtr, dtt, pr_rows, pr_cols, dvec, hs, y2, dy, *, name):
    s = xbc.shape[0]
    nc = s // T
    chunk = lambda d, c: jnp.where(d == 0, nc - 1 - c, c)
    in_specs = _ssd_in_specs(chunk) + [
        pl.BlockSpec((1, GPS * GW), lambda d, g, c: (0, g)),
        pl.BlockSpec((None, None, GPS, SSM_STATE, GW), lambda d, g, c: (d, chunk(d, c), g, 0, 0)),
        pl.BlockSpec((None, T, GPS * GW), lambda d, g, c: (d, chunk(d, c), g)),
        pl.BlockSpec((T, GPS * GW), lambda d, g, c: (chunk(d, c), g)),
    ]

    def body(x_ref, b_ref, c_ref, bt_ref, ct_ref, dt_ref, dtt_ref, prr_ref, prc_ref, dvec_ref, hs_ref, y_ref, dy_ref,
             dxs_ref, db_ref, dc_ref, ddt_ref, dalog_ref, dbias_ref, g_ref):
        @pl.when(pl.program_id(2) == 0)
        def _():
            g_ref[...] = jnp.zeros_like(g_ref)
            dalog_ref[...] = jnp.zeros_like(dalog_ref)
            dbias_ref[...] = jnp.zeros_like(dbias_ref)

        for gi in range(GPS):
            group(*_group_refs(gi, [x_ref, dvec_ref, y_ref, dy_ref, dxs_ref], [b_ref, c_ref, db_ref, dc_ref], [bt_ref, ct_ref],
                               [dt_ref, dtt_ref, prr_ref, prc_ref, hs_ref, ddt_ref, dalog_ref, dbias_ref, g_ref]))

    def group(x_ref, dvec_ref, y_ref, dy_ref, dxs_ref, b_ref, c_ref, db_ref, dc_ref, bt_ref, ct_ref,
              dt_ref, dtt_ref, prr_ref, prc_ref, hs_ref, ddt_ref, dalog_ref, dbias_ref, g_ref):
        d = pl.program_id(0)
        ck = _ssd_chunk(d, dt_ref, dtt_ref, prr_ref, prc_ref)
        h0 = ck["head0"]
        xs = x_ref[...].astype(F32)
        bm, cm = b_ref[...], c_ref[...]
        hst = hs_ref[...]
        gst = g_ref[...]
        dyv = dy_ref[...]
        yv = y_ref[...]
        dv = dvec_ref[...] * (1 - d).astype(F32)
        cb = lax.dot_general(cm, bm, _DIMS["nt"], preferred_element_type=F32)
        cbt = jnp.dot(bm, ct_ref[...], preferred_element_type=F32)
        ch = jnp.dot(cm, hst.astype(BF16), preferred_element_type=F32)
        bg = jnp.dot(bm, gst.astype(BF16), preferred_element_type=F32)
        hg_cols = jnp.sum(hst * gst, axis=0, keepdims=True)
        lane16 = lax.broadcasted_iota(jnp.int32, (T, 2 * HPG), 1)
        sub8 = lax.broadcasted_iota(jnp.int32, (HPG, 1), 0)
        dcb = jnp.zeros((T, T), F32)
        acc16 = jnp.zeros((T, 2 * HPG), F32)
        dtot = jnp.zeros((HPG, 1), F32)
        dxss, dyes, xgds, etots = [], [], [], []
        for q in range(PAIRS):
            sl = slice(q * 128, (q + 1) * 128)
            pr = _ssd_pair(ck, q)
            xsp, dyp = xs[:, sl], dyv[:, sl]
            xg = xsp * pr["dt"]
            xgb = xg.astype(BF16)
            dyb = dyp.astype(BF16)
            dy0, dy1 = _split_heads(h0, dyb)
            dcb = dcb + (lax.dot_general(dy0, xgb, _DIMS["nt"], preferred_element_type=F32) * pr["lm"][0]
                         + lax.dot_general(dy1, xgb, _DIMS["nt"], preferred_element_type=F32) * pr["lm"][1])
            dxg_in = (jnp.dot((cbt * pr["lmt"][0]).astype(BF16), dy0, preferred_element_type=F32)
                      + jnp.dot((cbt * pr["lmt"][1]).astype(BF16), dy1, preferred_element_type=F32))
            xgd = xg * pr["decay"]
            xb = xgd * bg[:, sl]
            dxg = dxg_in + pr["decay"] * bg[:, sl]
            yo = pr["ec"] * ch[:, sl]
            dac = dyb.astype(F32) * (yv[:, sl] - yo) + dyp * yo - xgb.astype(F32) * dxg_in - xb
            d_0, d_1 = _split_heads(h0, dac)
            r_0, r_1 = _split_heads(h0, dxg * xsp)
            for hh, (d_h, r_h) in enumerate(((d_0, r_0), (d_1, r_1))):
                j = 2 * q + hh
                acc16 = (acc16 + jnp.where(lane16 == j, jnp.sum(d_h, axis=-1, keepdims=True), 0.0)
                         + jnp.where(lane16 == HPG + j, jnp.sum(r_h, axis=-1, keepdims=True), 0.0))
            tcols = pr["etot"] * hg_cols[:, sl] + jnp.sum(xb, axis=0, keepdims=True)
            t0, t1 = _split_heads(h0[0:1], tcols)
            dtot = (dtot + jnp.where(sub8 == 2 * q, jnp.sum(t0, axis=-1, keepdims=True), 0.0)
                    + jnp.where(sub8 == 2 * q + 1, jnp.sum(t1, axis=-1, keepdims=True), 0.0))
            dxss.append(dxg * pr["dt"] + dv[:, sl] * dyp)
            dyes.append(dyp * pr["ec"])
            xgds.append(xgd)
            etots.append(pr["etot"])
        acc_t = acc16.T
        da_lanes = _scan_lanes(acc_t[0:HPG], d != 0)[0] + dtot
        ddt = da_lanes * ck["a_c"] + acc_t[HPG:2 * HPG]
        draw = ddt * _sigmoid(ck["raw_lanes"])
        ddt_ref[...] = draw
        dbias_ref[...] += jnp.sum(draw, axis=-1, keepdims=True)
        dalog_ref[...] += jnp.sum(da_lanes * ck["dt_lanes"], axis=-1, keepdims=True) * ck["a_c"]
        dxs_ref[...] = jnp.concatenate(dxss, axis=1)
        dye = jnp.concatenate(dyes, axis=1).astype(BF16)
        xgd_all = jnp.concatenate(xgds, axis=1).astype(BF16)
        dcbb = dcb.astype(BF16)
        dc_ref[...] = (jnp.dot(dcbb, bm, preferred_element_type=F32)
                       + lax.dot_general(dye, hst.astype(BF16), _DIMS["nt"], preferred_element_type=F32))
        db_ref[...] = (lax.dot_general(dcbb, cm, _DIMS["tn"], preferred_element_type=F32)
                       + lax.dot_general(xgd_all, gst.astype(BF16), _DIMS["nt"], preferred_element_type=F32))
        g_ref[...] = jnp.dot(ct_ref[...], dye, preferred_element_type=F32) + gst * jnp.concatenate(etots, axis=1)

    small = pl.BlockSpec((None, GPS, HPG, 1), lambda d, g, c: (d, g, 0, 0))
    sshape = jax.ShapeDtypeStruct((2, SSM_GROUPS, HPG, 1), F32)
    return pl.pallas_call(
        body, name=name, grid=(2, GSTEPS, nc), in_specs=in_specs,
        out_specs=[pl.BlockSpec((None, T, GPS * GW), lambda d, g, c: (d, chunk(d, c), g)),
                   pl.BlockSpec((None, T, GPS * SSM_STATE), lambda d, g, c: (d, chunk(d, c), g)),
                   pl.BlockSpec((None, T, GPS * SSM_STATE), lambda d, g, c: (d, chunk(d, c), g)),
                   pl.BlockSpec((None, GPS, HPG, T), lambda d, g, c: (d, g, 0, chunk(d, c))), small, small],
        out_shape=[jax.ShapeDtypeStruct((2, s, SSM_INNER), F32),
                   jax.ShapeDtypeStruct((2, s, SSM_GROUPS * SSM_STATE), F32),
                   jax.ShapeDtypeStruct((2, s, SSM_GROUPS * SSM_STATE), F32),
                   jax.ShapeDtypeStruct((2, SSM_GROUPS, HPG, s), F32), sshape, sshape],
        scratch_shapes=[pltpu.VMEM((GPS, SSM_STATE, GW), F32)],
        compiler_params=_params("parallel", "parallel", "arbitrary"),
    )(xbc, xbc, xbc, bt, ct, dtr, dtt, pr_rows, pr_cols, dvec, hs, y2, dy)


def _gate_norm_fwd(y2, xbc, proj, dvec, nw, *, name):
    s = xbc.shape[0]
    tr = 256

    def body(y_ref, xs_ref, z_ref, dv_ref, w_ref, u_ref):
        yt = y_ref[0] + y_ref[1] + dv_ref[...] * xs_ref[...].astype(F32)
        yg = yt * _silu_and_grad(z_ref[...].astype(F32))[0]
        u_ref[...] = (yg * lax.rsqrt(jnp.mean(yg * yg, axis=-1, keepdims=True) + RMS_EPS) * w_ref[...]).astype(BF16)

    row = pl.BlockSpec((tr, SSM_INNER), lambda i: (i, 0))
    vec = pl.BlockSpec((1, SSM_INNER), lambda i: (0, 0))
    return pl.pallas_call(
        body, name=name, grid=(s // tr,),
        in_specs=[pl.BlockSpec((2, tr, SSM_INNER), lambda i: (0, i, 0)), row, row, vec, vec], out_specs=row,
        out_shape=jax.ShapeDtypeStruct((s, SSM_INNER), BF16), compiler_params=_params("parallel"),
    )(y2, xbc, proj, dvec, nw)


def _gate_norm_bwd(du, y2, xbc, proj, dvec, nw, *, name):
    s = xbc.shape[0]
    tr = 256

    def body(du_ref, y_ref, xs_ref, z_ref, dv_ref, w_ref, dy_ref, dz_ref, dw_ref, dd_ref):
        @pl.when(pl.program_id(0) == 0)
        def _():
            dw_ref[...] = jnp.zeros_like(dw_ref)
            dd_ref[...] = jnp.zeros_like(dd_ref)

        xs = xs_ref[...].astype(F32)
        yt = y_ref[0] + y_ref[1] + dv_ref[...] * xs
        si, dsi = _silu_and_grad(z_ref[...].astype(F32))
        yg = yt * si
        rstd = lax.rsqrt(jnp.mean(yg * yg, axis=-1, keepdims=True) + RMS_EPS)
        yhat = yg * rstd
        du = du_ref[...]
        dyn = du * w_ref[...]
        dyg = rstd * (dyn - yhat * jnp.mean(dyn * yhat, axis=-1, keepdims=True))
        dyt = dyg * si
        dy_ref[...] = dyt
        dz_ref[...] = (dyg * yt * dsi).astype(BF16)
        dw_ref[...] += jnp.sum(du * yhat, axis=0, keepdims=True)
        dd_ref[...] += jnp.sum(dyt * xs, axis=0, keepdims=True)

    row = pl.BlockSpec((tr, SSM_INNER), lambda i: (i, 0))
    vec = pl.BlockSpec((1, SSM_INNER), lambda i: (0, 0))
    vshape = jax.ShapeDtypeStruct((1, SSM_INNER), F32)
    return pl.pallas_call(
        body, name=name, grid=(s // tr,),
        in_specs=[row, pl.BlockSpec((2, tr, SSM_INNER), lambda i: (0, i, 0)), row, row, vec, vec],
        out_specs=[row, row, vec, vec],
        out_shape=[jax.ShapeDtypeStruct((s, SSM_INNER), F32), jax.ShapeDtypeStruct((s, SSM_INNER), BF16), vshape, vshape],
        compiler_params=_params("arbitrary"),
    )(du, y2, xbc, proj, dvec, nw)


def _with_riders(result, riders):
    return result if riders else (result, [])


def _layer_a_fwd(x, mod, w_in, w_out, ln_g, ln_b, tag, riders=()):
    shift, scale, gate = mod
    h = _modulate(x, scale, shift, name=f"{tag}_modulate")
    w0 = jnp.concatenate([w_in[:, :A_GROUP_COLS], w_in[:, 3 * A_GROUP_COLS:]], axis=1)
    proj0, exchanged = _with_riders(_mm(h, w0, mode="nn", out_dtype=BF16, tm=1024, tn=1024, tk=1024, name=f"{tag}_mm_in0",
                                        riders=riders), riders)
    projs = [proj0]
    for grp in (1, 2):
        projs.append(_mm_dilated(h, w_in[:, grp * A_GROUP_COLS:(grp + 1) * A_GROUP_COLS], DILATIONS[grp],
                                 name=f"{tag}_mm_in{grp}"))
    ol = []
    for grp in range(3):
        ol.extend(_attn_fwd(projs[grp], grp, name=f"{tag}_attn_fwd{grp}"))
    y, o, lse = _attn_combine(*ol, projs[0], name=f"{tag}_combine")
    out = _mm(y, w_out, mode="nn", out_dtype=F32, tm=512, tn=1024, tk=1024, name=f"{tag}_mm_out")
    xn = _resid_ln_fwd(x, out, gate, ln_g, ln_b, name=f"{tag}_resid_ln")
    return xn, (x, h, projs, y, o, lse, out), exchanged


def _layer_a_bwd(dxn, saved, mod, w_in, w_out, ln_g, tag, riders=()):
    x, h, projs, y, o, lse, out = saved
    shift, scale, gate = mod
    dx_part, dout, dgate, dln_g, dln_b = _resid_ln_bwd(x, out, gate, ln_g, dxn, name=f"{tag}_resid_ln_bwd")
    dw_out = _mm(y, dout, mode="tn", out_dtype=F32, tm=1024, tn=1024, tk=512, name=f"{tag}_mm_dw_out")
    dy = _mm(dout, w_out, mode="nt", out_dtype=F32, tm=512, tn=1024, tk=1024, name=f"{tag}_mm_dy")
    dgp, do0, do1, o1, lse1, do2, o2, lse2 = _attn_combine_bwd(dy, o, lse, projs[0], name=f"{tag}_combine_bwd")
    parts = [_attn_bwd(projs[grp], grp, *dol, name=f"{tag}_attn_bwd{grp}")
             for grp, dol in enumerate(((do0, o, lse), (do1, o1, lse1), (do2, o2, lse2)))]
    dproj = _assemble_dproj(*parts, dgp, name=f"{tag}_assemble_dproj")
    dw_in, exchanged = _with_riders(_mm(h.T, dproj, mode="nn", out_dtype=F32, tm=1024, tn=1024, tk=1024,
                                        name=f"{tag}_mm_dw_in", riders=riders), riders)
    dx, dscale, dshift = _mm_dh(dproj, w_in, dx_part, x, scale, tm=512, tk=2048, name=f"{tag}_mm_dh")
    grads = dict(w_in=dw_in, w_out=dw_out, ln_g=dln_g, ln_b=dln_b, mod=jnp.concatenate([dshift, dscale, dgate], axis=1))
    return dx, grads, exchanged


def _ssd_param_views(dt_raw, dt_bias, a_log):
    s = dt_raw.shape[0]
    r4 = dt_raw.reshape(s, 2, SSM_GROUPS, HPG)
    dtr = r4.transpose(1, 2, 0, 3)
    dtt = r4.transpose(1, 2, 3, 0)
    a = -jnp.exp(a_log)
    pr_rows = jnp.stack([dt_bias.reshape(2, SSM_GROUPS, HPG), a.reshape(2, SSM_GROUPS, HPG)], axis=2)
    return dtr, dtt, pr_rows, pr_rows.transpose(0, 1, 3, 2)


def _layer_b_fwd(x, mod, w_in, w_out, p, ln_g, ln_b, tag, riders=()):
    shift, scale, gate = mod
    s = x.shape[0]
    h = _modulate(x, scale, shift, name=f"{tag}_modulate")
    proj, exchanged = _with_riders(_mm(h, w_in[:, :SSM_MAIN_COLS], mode="nn", out_dtype=BF16, tm=512, tn=1024, tk=1024,
                                       name=f"{tag}_mm_in", riders=riders), riders)
    dt_raw = _mm(h, w_in[:, SSM_MAIN_COLS:SSM_IN_COLS], mode="nn", out_dtype=F32, tm=512, tn=64, tk=1024,
                 name=f"{tag}_mm_dt")
    xpad = jnp.pad(proj[:, SSM_INNER:], ((CONV_HALO, CONV_HALO), (0, 0)))
    xbc = _conv_fwd(xpad, p["conv_w"], p["conv_b"], name=f"{tag}_conv")
    views = (xbc[:, SSM_INNER:SSM_INNER + SSM_GROUPS * SSM_STATE].T, xbc[:, SSM_INNER + SSM_GROUPS * SSM_STATE:].T,
             *_ssd_param_views(dt_raw, p["dt_bias"], p["a_log"]))
    y2, hs = _ssd_fwd(xbc, *views, name=f"{tag}_ssd_fwd")
    u = _gate_norm_fwd(y2, xbc, proj, p["dvec"], p["norm_w"], name=f"{tag}_gate_norm")
    out = _mm(u, w_out, mode="nn", out_dtype=F32, tm=512, tn=1024, tk=2048, name=f"{tag}_mm_out")
    xn = _resid_ln_fwd(x, out, gate, ln_g, ln_b, name=f"{tag}_resid_ln")
    return xn, (x, h, proj, xpad, xbc, views, y2, hs, u, out), exchanged


def _layer_b_bwd(dxn, saved, mod, w_in, w_out, p, ln_g, tag, riders=()):
    x, h, proj, xpad, xbc, views, y2, hs, u, out = saved
    shift, scale, gate = mod
    s = x.shape[0]
    dx_part, dout, dgate, dln_g, dln_b = _resid_ln_bwd(x, out, gate, ln_g, dxn, name=f"{tag}_resid_ln_bwd")
    dw_out = _mm(u, dout, mode="tn", out_dtype=F32, tm=1024, tn=1024, tk=512, name=f"{tag}_mm_dw_out")
    du = _mm(dout, w_out, mode="nt", out_dtype=F32, tm=512, tn=1024, tk=1024, name=f"{tag}_mm_du")
    dy, dz, dnorm_w, dd_lanes = _gate_norm_bwd(du, y2, xbc, proj, p["dvec"], p["norm_w"], name=f"{tag}_gate_norm_bwd")
    dxs2, db2, dc2, ddt4, dalog, dbias = _ssd_bwd(xbc, *views, p["dvec"], hs, y2, dy, name=f"{tag}_ssd_bwd")
    dact = jnp.concatenate([dxs2[0] + dxs2[1], db2[0] + db2[1], dc2[0] + dc2[1]], axis=1)
    dapad = jnp.pad(dact, ((CONV_HALO, CONV_HALO), (0, 0)))
    dxbc, dconv_w, dconv_b = _conv_bwd(xpad, dapad, p["conv_w"], p["conv_b"], name=f"{tag}_conv_bwd")
    ddt_raw = ddt4.transpose(3, 0, 1, 2).reshape(s, 2 * SSM_HEADS).astype(BF16)
    dproj = jnp.concatenate([dz, dxbc, ddt_raw, jnp.zeros((s, SSM_PAD_COLS - SSM_IN_COLS), BF16)], axis=1)
    dw_in, exchanged = _with_riders(_mm(h.T, dproj, mode="nn", out_dtype=F32, tm=1024, tn=896, tk=1024,
                                        name=f"{tag}_mm_dw_in", riders=riders), riders)
    dw_in = dw_in[:, :SSM_IN_COLS]
    w_pad = jnp.pad(w_in, ((0, 0), (0, SSM_PAD_COLS - SSM_IN_COLS)))
    dx, dscale, dshift = _mm_dh(dproj, w_pad, dx_part, x, scale, tm=512, tk=1792, name=f"{tag}_mm_dh")
    grads = dict(
        w_in=dw_in, w_out=dw_out, ln_g=dln_g, ln_b=dln_b, mod=jnp.concatenate([dshift, dscale, dgate], axis=1),
        conv_w=dconv_w, conv_b=dconv_b, norm_w=dnorm_w, dt_bias=dbias.reshape(2, SSM_HEADS),
        a_log=dalog.reshape(2, SSM_HEADS), d=jnp.sum(dd_lanes.reshape(SSM_HEADS, SSM_HEAD_DIM), axis=1))
    return dx, grads, exchanged


def _full_cols(g):
    return g.transpose(1, 0, 2).reshape(g.shape[1], -1)


def _full_rows(g):
    return g.reshape(-1, g.shape[2])


def _col_blocks(dw):
    r, c = dw.shape
    return dw.reshape(r, N_DEV, c // N_DEV).transpose(1, 0, 2).astype(BF16)


def _row_blocks(dw):
    r, c = dw.shape
    return dw.reshape(N_DEV, r // N_DEV, c).astype(BF16)


def _local_step(x, target, mods, ln_g, ln_b, layer_w, b_params, shards=None):
    layer_w = list(layer_w)
    saved = []
    for i in range(DEPTH):
        riders = ()
        if shards is not None and i + 1 < DEPTH:
            riders = ((shards[i + 1][0], True), (shards[i + 1][1], True))
        small = () if i % 2 == 0 else (b_params[i // 2],)
        fwd = _layer_a_fwd if i % 2 == 0 else _layer_b_fwd
        x, sv, got = fwd(x, mods[i], *layer_w[i], *small, ln_g[i:i + 1], ln_b[i:i + 1], f"l{i}", riders)
        if riders:
            layer_w.append((_full_cols(got[0]), _full_rows(got[1])))
        saved.append(sv)
    dx, loss = _loss_and_grad(x, target, name="loss")
    grads, received = [None] * DEPTH, [None] * DEPTH
    riders = ()
    for i in reversed(range(DEPTH)):
        small = () if i % 2 == 0 else (b_params[i // 2],)
        bwd = _layer_a_bwd if i % 2 == 0 else _layer_b_bwd
        dx, grads[i], got = bwd(dx, saved[i], mods[i], *layer_w[i], *small, ln_g[i:i + 1], f"l{i}", riders)
        if riders:
            received[i + 1] = got
        if shards is not None:
            riders = ((_col_blocks(grads[i]["w_in"]), False), (_row_blocks(grads[i]["w_out"]), False))
    if shards is not None:
        received[0] = [_all_to_all(riders[0][0], name="scatter_w_in0"), _all_to_all(riders[1][0], name="scatter_w_out0")]
    return loss, dx, grads, received


def _mesh_pos():
    return lax.axis_index("x"), lax.axis_index("y"), lax.axis_index("c")


def _all_gather(x, *, name):
    def body(x_ref, out_ref, send_sems, recv_sems, local_sem):
        ax, ay, ac = _mesh_pos()
        me, sibling = (ax, ay, ac), (ax, ay, 1 - ac)
        chips = [(1 - ax, ay), (ax, 1 - ay), (1 - ax, 1 - ay)]

        def slot(px, py, pc):
            return out_ref.at[4 * px + 2 * py + pc]

        def copy(k, block, to, src=None):
            return pltpu.make_async_remote_copy(
                src_ref=slot(*block) if src is None else src, dst_ref=slot(*block),
                send_sem=send_sems.at[k], recv_sem=recv_sems.at[k], device_id=to, device_id_type=MESH)

        mine = pltpu.make_async_copy(x_ref, slot(*me), local_sem)
        mine.start()
        first = [copy(0, me, sibling, src=x_ref)]
        first += [copy(1 + j, me, (*chip, ac), src=x_ref) for j, chip in enumerate(chips)]
        for cp in first:
            cp.start()
        passed = [copy(4 + j, (*chip, ac), sibling) for j, chip in enumerate(chips)]
        for j, chip in enumerate(chips):
            copy(1 + j, (*chip, ac), me).wait_recv()
            passed[j].start()
        copy(0, sibling, me).wait_recv()
        for j, chip in enumerate(chips):
            copy(4 + j, (*chip, 1 - ac), me).wait_recv()
        for cp in first + passed:
            cp.wait_send()
        mine.wait()

    return pl.pallas_call(
        body, name=name, out_shape=jax.ShapeDtypeStruct((N_DEV,) + x.shape, x.dtype),
        in_specs=[pl.BlockSpec(memory_space=pl.ANY)], out_specs=pl.BlockSpec(memory_space=pl.ANY),
        scratch_shapes=[pltpu.SemaphoreType.DMA((7,)), pltpu.SemaphoreType.DMA((7,)), pltpu.SemaphoreType.DMA],
    )(x)


def _all_to_all(x, *, name):
    def body(x_ref, out_ref, send_sems, recv_sems, local_sem):
        ax, ay, ac = _mesh_pos()
        me = 4 * ax + 2 * ay + ac
        mine = pltpu.make_async_copy(x_ref.at[me], out_ref.at[me], local_sem)
        mine.start()
        copies = []
        for k in range(1, N_DEV):
            px = 1 - ax if k & 4 else ax
            py = 1 - ay if k & 2 else ay
            pc = 1 - ac if k & 1 else ac
            copies.append(pltpu.make_async_remote_copy(
                src_ref=x_ref.at[4 * px + 2 * py + pc], dst_ref=out_ref.at[me],
                send_sem=send_sems.at[k - 1], recv_sem=recv_sems.at[k - 1], device_id=(px, py, pc), device_id_type=MESH))
        for cp in copies:
            cp.start()
        for cp in copies:
            cp.wait()
        mine.wait()

    return pl.pallas_call(
        body, name=name, out_shape=jax.ShapeDtypeStruct(x.shape, x.dtype),
        in_specs=[pl.BlockSpec(memory_space=pl.ANY)], out_specs=pl.BlockSpec(memory_space=pl.ANY),
        scratch_shapes=[pltpu.SemaphoreType.DMA((7,)), pltpu.SemaphoreType.DMA((7,)), pltpu.SemaphoreType.DMA],
    )(x)


ADA_LOCAL = 3 * D_MODEL // N_DEV


def _ada_mod(c_all, ada_w, ada_b_local, *, name):
    def body(c_ref, w_ref, b_ref, o_ref):
        cond = _silu_and_grad(c_ref[...])[0]
        o_ref[...] = jnp.dot(cond, w_ref[...], precision=HIGHEST, preferred_element_type=F32) + b_ref[...]

    return pl.pallas_call(
        body, name=name, grid=(DEPTH,),
        in_specs=[pl.BlockSpec((N_DEV, D_MODEL), lambda i: (0, 0)), pl.BlockSpec((None, D_MODEL, ADA_LOCAL), lambda i: (i, 0, 0)),
                  pl.BlockSpec((None, 1, ADA_LOCAL), lambda i: (i, 0, 0))],
        out_specs=pl.BlockSpec((None, N_DEV, ADA_LOCAL), lambda i: (i, 0, 0)),
        out_shape=jax.ShapeDtypeStruct((DEPTH, N_DEV, ADA_LOCAL), F32), compiler_params=_params("parallel"),
    )(c_all, ada_w, ada_b_local)


def _ada_grad(c_all_t, dmod_local, *, name):
    def body(ct_ref, dm_ref, o_ref):
        cond_t = _silu_and_grad(ct_ref[...])[0]
        dm = dm_ref[...]
        acc = cond_t[:, 0:1] * dm[0:1, :]
        for smp in range(1, N_DEV):
            acc = acc + cond_t[:, smp:smp + 1] * dm[smp:smp + 1, :]
        o_ref[...] = acc

    return pl.pallas_call(
        body, name=name, grid=(DEPTH,),
        in_specs=[pl.BlockSpec((D_MODEL, N_DEV), lambda i: (0, 0)), pl.BlockSpec((None, N_DEV, ADA_LOCAL), lambda i: (i, 0, 0))],
        out_specs=pl.BlockSpec((None, D_MODEL, ADA_LOCAL), lambda i: (i, 0, 0)),
        out_shape=jax.ShapeDtypeStruct((DEPTH, D_MODEL, ADA_LOCAL), F32), compiler_params=_params("parallel"),
    )(c_all_t, dmod_local)


def _sum_devices(parts, *, name):
    n = parts.shape[1]

    def body(p_ref, o_ref):
        acc = p_ref[0:1, :]
        for dev in range(1, N_DEV):
            acc = acc + p_ref[dev:dev + 1, :]
        o_ref[...] = acc

    return pl.pallas_call(
        body, name=name, out_shape=jax.ShapeDtypeStruct((1, n), F32),
        in_specs=[pl.BlockSpec(memory_space=pltpu.VMEM)], out_specs=pl.BlockSpec(memory_space=pltpu.VMEM),
        compiler_params=pltpu.CompilerParams(vmem_limit_bytes=VMEM_LIMIT_BYTES),
    )(parts)


ADAMW_VMEM_BYTES = 24 * 1024 * 1024


def _adamw(w, m, v, g, *, name):
    r, c = w.shape
    summed = g.ndim == 3
    tr = r
    arrays = 7 + (N_DEV if summed else 1)
    while tr % 16 == 0 and 2 * arrays * tr * c * 4 > ADAMW_VMEM_BYTES:
        tr //= 2

    def body(w_ref, m_ref, v_ref, g_ref, go_ref, d_ref, mo_ref, vo_ref):
        if summed:
            g = g_ref[0].astype(F32)
            for dev in range(1, N_DEV):
                g = g + g_ref[dev].astype(F32)
        else:
            g = g_ref[...]
        mn = ADAM_B1 * m_ref[...] + (1.0 - ADAM_B1) * g
        vn = ADAM_B2 * v_ref[...] + (1.0 - ADAM_B2) * (g * g)
        m_hat = mn / (1.0 - ADAM_B1 ** ADAM_STEP)
        v_hat = vn / (1.0 - ADAM_B2 ** ADAM_STEP)
        go_ref[...] = g
        d_ref[...] = -ADAM_LR * (m_hat / (jnp.sqrt(v_hat) + ADAM_EPS) + ADAM_WD * w_ref[...])
        mo_ref[...] = mn
        vo_ref[...] = vn

    row = pl.BlockSpec((tr, c), lambda i: (i, 0))
    gspec = pl.BlockSpec((N_DEV, tr, c), lambda i: (0, i, 0)) if summed else row
    shp = jax.ShapeDtypeStruct((r, c), F32)
    return pl.pallas_call(
        body, name=name, grid=(r // tr,), in_specs=[row, row, row, gspec], out_specs=[row] * 4, out_shape=[shp] * 4,
        compiler_params=_params("parallel"),
    )(w, m, v, g)


def _pack(arrays):
    flat = jnp.concatenate([a.reshape(-1) for a in arrays])
    n = flat.shape[0]
    return jnp.pad(flat, (0, -n % 128)).reshape(1, -1)


def _unpack(vec, shapes):
    out, at = [], 0
    for shp in shapes:
        n = math.prod(shp)
        out.append(vec[at:at + n].reshape(shp))
        at += n
    return out


def _unpack_rows(rows, shapes):
    out, at = [], 0
    for shp in shapes:
        n = math.prod(shp)
        out.append(rows[:, at:at + n].reshape((rows.shape[0],) + tuple(shp)))
        at += n
    return out


def _my_shard(full, me, axis):
    width = full.shape[axis] // N_DEV
    return lax.dynamic_slice_in_dim(full, me * width, width, axis)


def _gather_cols(g, lead):
    nd = g.ndim
    perm = tuple(range(1, nd - 1)) + (0, nd - 1)
    t = g.transpose(perm)
    return t.reshape(t.shape[:-2] + (t.shape[-2] * t.shape[-1],))


def kernel(x, c, ada_w, ada_b, ln_g, ln_b, a_w_in, a_w_out, b_w_in, b_conv_w, b_conv_b, b_dt_bias, b_a_log, b_d, b_norm_w, b_w_out, loss_target, m_ada_w, m_ada_b, m_ln_g, m_ln_b, m_a_w_in, m_a_w_out, m_b_w_in, m_b_conv_w, m_b_conv_b, m_b_dt_bias, m_b_a_log, m_b_d, m_b_norm_w, m_b_w_out, v_ada_w, v_ada_b, v_ln_g, v_ln_b, v_a_w_in, v_a_w_out, v_b_w_in, v_b_conv_w, v_b_conv_b, v_b_dt_bias, v_b_a_log, v_b_d, v_b_norm_w, v_b_w_out):
    ax, ay, ac = _mesh_pos()
    me = 4 * ax + 2 * ay + ac
    seq = x.shape[1]

    small_shapes = [(1, D_MODEL), (2, SSM_CONV, ADA_LOCAL), (2, ADA_LOCAL), (2, SSM_INNER // N_DEV)]
    sg = _all_gather(_pack([c, b_conv_w, b_conv_b, b_norm_w]), name="gather_small")[:, 0, :]
    c_all, conv_w_g, conv_b_g, norm_w_g = _unpack_rows(sg, small_shapes)
    c_all = c_all[:, 0, :]
    conv_w = _gather_cols(conv_w_g, 2)
    conv_b = _gather_cols(conv_b_g[:, :, None, :], 2)
    norm_w = _gather_cols(norm_w_g[:, :, None, :], 2)

    shards = [(w_in[i // 2].astype(BF16), w_out[i // 2].astype(BF16))
              for i, (w_in, w_out) in enumerate(((a_w_in, a_w_out), (b_w_in, b_w_out)) * 2)]
    layer0_w = (_full_cols(_all_gather(shards[0][0], name="gather_w_in0")),
                _full_rows(_all_gather(shards[0][1], name="gather_w_out0")))

    ada_b_local = _my_shard(ada_b, me, 1)[:, None, :]
    mod_cols = _ada_mod(c_all, ada_w, ada_b_local, name="ada_mod")
    mod_g = _all_gather(mod_cols.reshape(1, -1), name="gather_mod").reshape(N_DEV, DEPTH, N_DEV, ADA_LOCAL)
    mod = lax.dynamic_index_in_dim(mod_g, me, axis=2, keepdims=False).transpose(1, 0, 2).reshape(DEPTH, 3 * D_MODEL)
    mods = [tuple(mod[i:i + 1, k * D_MODEL:(k + 1) * D_MODEL] for k in range(3)) for i in range(DEPTH)]

    b_params = [dict(conv_w=conv_w[j], conv_b=conv_b[j], norm_w=norm_w[j],
                     dt_bias=b_dt_bias[j], a_log=b_a_log[j], dvec=jnp.repeat(b_d[j], SSM_HEAD_DIM)[None, :])
                for j in range(2)]
    loss_lanes, dx, grads, received = _local_step(x[0], loss_target[0], mods, ln_g, ln_b, [layer0_w], b_params, shards)
    loss = lax.psum(loss_lanes[0, 0], ("x", "y", "c"))
    grad_x = dx[None]

    a_layers, b_layers = (grads[0], grads[2]), (grads[1], grads[3])
    part_shapes = [(DEPTH, 3 * D_MODEL), (DEPTH, D_MODEL), (DEPTH, D_MODEL), (2, SSM_CONV, SSM_CONV_DIM),
                   (2, SSM_CONV_DIM), (2, SSM_INNER), (2, 2, SSM_HEADS), (2, 2, SSM_HEADS), (2, SSM_HEADS)]
    parts = _pack([
        jnp.concatenate([g["mod"] for g in grads]), jnp.concatenate([g["ln_g"] for g in grads]),
        jnp.concatenate([g["ln_b"] for g in grads]), jnp.stack([g["conv_w"] for g in b_layers]),
        jnp.stack([g["conv_b"][0] for g in b_layers]), jnp.stack([g["norm_w"][0] for g in b_layers]),
        jnp.stack([g["dt_bias"] for g in b_layers]), jnp.stack([g["a_log"] for g in b_layers]),
        jnp.stack([g["d"] for g in b_layers])])
    parts_g = _all_gather(parts, name="gather_small_grads")[:, 0, :]
    (g_ada_b, g_ln_g, g_ln_b, g_conv_w, g_conv_b, g_norm_w, g_dt_bias, g_a_log, g_d) = _unpack(
        _sum_devices(parts_g, name="sum_small_grads")[0], part_shapes)
    dmod_all = parts_g[:, :DEPTH * 3 * D_MODEL].reshape(N_DEV, DEPTH, N_DEV, ADA_LOCAL)
    dmod_local = lax.dynamic_index_in_dim(dmod_all, me, axis=2, keepdims=False).transpose(1, 0, 2)
    g_ada_w = _ada_grad(c_all.T, dmod_local, name="ada_grad")

    r_a_w_in = jnp.concatenate([received[0][0], received[2][0]], axis=1)
    r_a_w_out = jnp.concatenate([received[0][1], received[2][1]], axis=1)
    r_b_w_in = jnp.concatenate([received[1][0], received[3][0]], axis=1)
    r_b_w_out = jnp.concatenate([received[1][1], received[3][1]], axis=1)

    def update(w, m, v, g, name):
        two_d = (-1, w.shape[-1])
        outs = _adamw(w.reshape(two_d), m.reshape(two_d), v.reshape(two_d), g, name=name)
        return [o.reshape(w.shape) for o in outs]

    up_ada_w = update(ada_w, m_ada_w, v_ada_w, g_ada_w.reshape(-1, ADA_LOCAL), "adamw_ada_w")
    up_a_w_in = update(a_w_in, m_a_w_in, v_a_w_in, r_a_w_in, "adamw_a_w_in")
    up_a_w_out = update(a_w_out, m_a_w_out, v_a_w_out, r_a_w_out, "adamw_a_w_out")
    up_b_w_in = update(b_w_in, m_b_w_in, v_b_w_in, r_b_w_in, "adamw_b_w_in")
    up_b_w_out = update(b_w_out, m_b_w_out, v_b_w_out, r_b_w_out, "adamw_b_w_out")

    small_w = [ada_b, ln_g, ln_b, b_conv_w, b_conv_b, b_dt_bias, b_a_log, b_d, b_norm_w]
    small_m = [m_ada_b, m_ln_g, m_ln_b, m_b_conv_w, m_b_conv_b, m_b_dt_bias, m_b_a_log, m_b_d, m_b_norm_w]
    small_v = [v_ada_b, v_ln_g, v_ln_b, v_b_conv_w, v_b_conv_b, v_b_dt_bias, v_b_a_log, v_b_d, v_b_norm_w]
    small_g = [g_ada_b, g_ln_g, g_ln_b, _my_shard(g_conv_w, me, 2), _my_shard(g_conv_b, me, 1), g_dt_bias, g_a_log, g_d,
               _my_shard(g_norm_w, me, 1)]
    shapes = [w.shape for w in small_w]
    packed = _adamw(_pack(small_w), _pack(small_m), _pack(small_v), _pack(small_g), name="adamw_small")
    (up_ada_b, up_ln_g, up_ln_b, up_conv_w, up_conv_b, up_dt_bias, up_a_log, up_d, up_norm_w) = zip(
        *[_unpack(p[0], shapes) for p in packed])

    ordered = [up_ada_w, up_ada_b, up_ln_g, up_ln_b, up_a_w_in, up_a_w_out, up_b_w_in, up_conv_w, up_conv_b,
               up_dt_bias, up_a_log, up_d, up_norm_w, up_b_w_out]
    return (loss, grad_x, *[u[0] for u in ordered], *[u[1] for u in ordered], *[u[2] for u in ordered],
            *[u[3] for u in ordered])
```

```python
import functools
import math

import jax
import jax.numpy as jnp
import numpy as np
from jax import lax
from jax.experimental import pallas as pl
from jax.experimental.pallas import tpu as pltpu

F32 = jnp.float32
BF16 = jnp.bfloat16
HIGHEST = lax.Precision.HIGHEST
MESH = pl.DeviceIdType.MESH

D_MODEL = 1024
DEPTH = 4
A_HEADS = 16
A_HEAD_DIM = 64
A_WIDTH = 1024
DILATIONS = (1, 4, 16)
A_RADIUS = 64
A_QBLOCK = 128
A_IN_COLS = 10240
SSM_INNER = 2048
SSM_HEADS = 32
SSM_HEAD_DIM = 64
SSM_STATE = 128
SSM_GROUPS = 4
SSM_CHUNK = 128
SSM_CONV = 5
SSM_CONV_DIM = 3072
SSM_IN_COLS = 5184
SSM_MAIN_COLS = 5120
SSM_PAD_COLS = 5376
CONV_HALO = 16
ALPHA = (2 * DEPTH) ** 0.25
LN_EPS = 1e-5
RMS_EPS = 1e-5
ADAM_LR, ADAM_B1, ADAM_B2, ADAM_EPS, ADAM_WD, ADAM_STEP = 0.001, 0.9, 0.999, 1e-08, 0.01, 10
N_DEV = 8
VMEM_LIMIT_BYTES = 56 * 1024 * 1024
NEG_BIG = -1e30


def _params(*sem):
    return pltpu.CompilerParams(dimension_semantics=sem, vmem_limit_bytes=VMEM_LIMIT_BYTES)


def _sigmoid(x):
    return 1.0 / (1.0 + jnp.exp(-x))


def _silu_and_grad(x):
    sg = _sigmoid(x)
    return x * sg, sg * (1.0 + x * (1.0 - sg))


def _softplus(x):
    e = jnp.exp(-jnp.abs(x))
    u = 1.0 + e
    log1p = jnp.where(u == 1.0, e, jnp.log(u) * (e / jnp.where(u == 1.0, 1.0, u - 1.0)))
    return jnp.maximum(x, 0.0) + log1p


_DIMS = {"nn": (((1,), (0,)), ((), ())), "nt": (((1,), (1,)), ((), ())), "tn": (((0,), (0,)), ((), ()))}


def _exchange_copies(x_ref, out_ref, send_sems, recv_sems, local_sem, gather):
    ax, ay, ac = lax.axis_index("x"), lax.axis_index("y"), lax.axis_index("c")
    me = 4 * ax + 2 * ay + ac
    copies = [pltpu.make_async_copy(x_ref if gather else x_ref.at[me], out_ref.at[me], local_sem)]
    for k in range(1, N_DEV):
        px = 1 - ax if k & 4 else ax
        py = 1 - ay if k & 2 else ay
        pc = 1 - ac if k & 1 else ac
        copies.append(pltpu.make_async_remote_copy(
            src_ref=x_ref if gather else x_ref.at[4 * px + 2 * py + pc], dst_ref=out_ref.at[me],
            send_sem=send_sems.at[k - 1], recv_sem=recv_sems.at[k - 1], device_id=(px, py, pc), device_id_type=MESH))
    return copies


def _rider_specs(riders):
    anywhere = [pl.BlockSpec(memory_space=pl.ANY)] * len(riders)
    shapes = [jax.ShapeDtypeStruct(((N_DEV,) + x.shape) if gather else x.shape, x.dtype) for x, gather in riders]
    sems = [pltpu.SemaphoreType.DMA((N_DEV - 1,)), pltpu.SemaphoreType.DMA((N_DEV - 1,)), pltpu.SemaphoreType.DMA]
    return anywhere, shapes, sems * len(riders)


def _riding(body, riders, n_in, n_out, grid):
    nr = len(riders)
    if not nr:
        return body

    def wrapped(*refs):
        ins, xs = refs[:n_in], refs[n_in:n_in + nr]
        outs, ys = refs[n_in + nr:n_in + nr + n_out], refs[n_in + nr + n_out:n_in + 2 * nr + n_out]
        scratch = refs[n_in + 2 * nr + n_out:]
        own, sems = scratch[:len(scratch) - 3 * nr], scratch[len(scratch) - 3 * nr:]
        ids = [pl.program_id(ax) for ax in range(len(grid))]

        def at(steps):
            cond = ids[0] == steps[0]
            for i, st in zip(ids[1:], steps[1:]):
                cond = jnp.logical_and(cond, i == st)
            return cond

        def exchanges():
            return [_exchange_copies(xs[r], ys[r], *sems[3 * r:3 * r + 3], riders[r][1]) for r in range(nr)]

        @pl.when(at([0] * len(grid)))
        def _():
            for copies in exchanges():
                for cp in copies:
                    cp.start()

        body(*ins, *outs, *own)

        @pl.when(at([g - 1 for g in grid]))
        def _():
            for copies in exchanges():
                for cp in copies:
                    cp.wait()

    return wrapped


def _mm(a, b, *, mode, out_dtype, tm, tn, tk, name, riders=()):
    if mode == "nn":
        (m, k), (_, n) = a.shape, b.shape
    elif mode == "nt":
        (m, k), (n, _) = a.shape, b.shape
    else:
        (k, m), (_, n) = a.shape, b.shape
    tm, tn, tk = min(tm, m), min(tn, n), min(tk, k)
    assert m % tm == 0 and n % tn == 0 and k % tk == 0, (name, a.shape, b.shape)
    nk = k // tk
    dims = _DIMS[mode]
    nr = len(riders)
    grid = (m // tm, n // tn, nk)

    def body(a_ref, b_ref, *rest):
        xs, o_ref, ys, scratch = rest[:nr], rest[nr], rest[nr + 1:2 * nr + 1], rest[2 * nr + 1:]
        sems = scratch[len(scratch) - 3 * nr:]
        ids = [pl.program_id(ax) for ax in range(3)]

        def exchanges():
            return [_exchange_copies(xs[r], ys[r], *sems[3 * r:3 * r + 3], riders[r][1]) for r in range(nr)]

        if nr:
            @pl.when(jnp.logical_and(jnp.logical_and(ids[0] == 0, ids[1] == 0), ids[2] == 0))
            def _():
                for copies in exchanges():
                    for cp in copies:
                        cp.start()

        part = lax.dot_general(a_ref[...], b_ref[...], dims, preferred_element_type=F32)
        if nk == 1:
            o_ref[...] = part.astype(o_ref.dtype)
        else:
            acc_ref = scratch[0]
            kk = ids[2]

            @pl.when(kk == 0)
            def _():
                acc_ref[...] = part

            @pl.when(kk > 0)
            def _():
                acc_ref[...] += part

            @pl.when(kk == nk - 1)
            def _():
                o_ref[...] = acc_ref[...].astype(o_ref.dtype)

        if nr:
            @pl.when(jnp.logical_and(jnp.logical_and(ids[0] == grid[0] - 1, ids[1] == grid[1] - 1), ids[2] == grid[2] - 1))
            def _():
                for copies in exchanges():
                    for cp in copies:
                        cp.wait()

    if mode == "tn":
        a_spec = pl.BlockSpec((tk, tm), lambda i, j, kk: (kk, i))
    else:
        a_spec = pl.BlockSpec((tm, tk), lambda i, j, kk: (i, kk))
    if mode == "nt":
        b_spec = pl.BlockSpec((tn, tk), lambda i, j, kk: (j, kk))
    else:
        b_spec = pl.BlockSpec((tk, tn), lambda i, j, kk: (kk, j))
    anywhere = pl.BlockSpec(memory_space=pl.ANY)
    exchanged = [jax.ShapeDtypeStruct(((N_DEV,) + x.shape) if gather else x.shape, x.dtype) for x, gather in riders]
    sems = [pltpu.SemaphoreType.DMA((N_DEV - 1,)), pltpu.SemaphoreType.DMA((N_DEV - 1,)), pltpu.SemaphoreType.DMA] * nr
    outs = pl.pallas_call(
        body, name=name, grid=grid,
        in_specs=[a_spec, b_spec] + [anywhere] * nr,
        out_specs=[pl.BlockSpec((tm, tn), lambda i, j, kk: (i, j))] + [anywhere] * nr,
        out_shape=[jax.ShapeDtypeStruct((m, n), out_dtype)] + exchanged,
        scratch_shapes=([] if nk == 1 else [pltpu.VMEM((tm, tn), F32)]) + sems,
        compiler_params=_params(*(("arbitrary",) * 3 if nr else ("parallel", "parallel", "arbitrary"))),
    )(a, b, *[x for x, _ in riders])
    return (outs[0], list(outs[1:])) if nr else outs[0]


def _mm_dh(dproj, w, dx_part, x, scale, *, tm, tk, name):
    s, k = dproj.shape
    d = w.shape[0]
    tk = min(tk, k)
    assert s % tm == 0 and k % tk == 0
    nk = k // tk

    def body(a_ref, w_ref, dxp_ref, x_ref, sc_ref, dx_ref, dsc_ref, dsh_ref, acc_ref):
        i, kk = pl.program_id(0), pl.program_id(1)
        part = lax.dot_general(a_ref[...], w_ref[...], _DIMS["nt"], preferred_element_type=F32)

        @pl.when(kk == 0)
        def _():
            acc_ref[...] = part

        @pl.when(kk > 0)
        def _():
            acc_ref[...] += part

        @pl.when(jnp.logical_and(i == 0, kk == 0))
        def _():
            dsc_ref[...] = jnp.zeros_like(dsc_ref)
            dsh_ref[...] = jnp.zeros_like(dsh_ref)

        @pl.when(kk == nk - 1)
        def _():
            dh = acc_ref[...]
            dx_ref[...] = dxp_ref[...] + dh * (1.0 + sc_ref[...])
            dsc_ref[...] += jnp.sum(dh * x_ref[...], axis=0, keepdims=True)
            dsh_ref[...] += jnp.sum(dh, axis=0, keepdims=True)

    row = pl.BlockSpec((tm, d), lambda i, kk: (i, 0))
    vec = pl.BlockSpec((1, d), lambda i, kk: (0, 0))
    return pl.pallas_call(
        body, name=name, grid=(s // tm, nk),
        in_specs=[pl.BlockSpec((tm, tk), lambda i, kk: (i, kk)), pl.BlockSpec((d, tk), lambda i, kk: (0, kk)),
                  row, row, vec],
        out_specs=[row, vec, vec],
        out_shape=[jax.ShapeDtypeStruct((s, d), F32), jax.ShapeDtypeStruct((1, d), F32),
                   jax.ShapeDtypeStruct((1, d), F32)],
        scratch_shapes=[pltpu.VMEM((tm, d), F32)],
        compiler_params=_params("arbitrary", "arbitrary"),
    )(dproj, w, dx_part, x, scale)


ROW_TILE = 512


def _modulate(x, scale, shift, *, name):
    s, d = x.shape

    def body(x_ref, sc_ref, sh_ref, h_ref):
        h_ref[...] = (x_ref[...] * (1.0 + sc_ref[...]) + sh_ref[...]).astype(BF16)

    row = pl.BlockSpec((ROW_TILE, d), lambda i: (i, 0))
    vec = pl.BlockSpec((1, d), lambda i: (0, 0))
    return pl.pallas_call(
        body, name=name, grid=(s // ROW_TILE,), in_specs=[row, vec, vec], out_specs=row,
        out_shape=jax.ShapeDtypeStruct((s, d), BF16), compiler_params=_params("parallel"),
    )(x, scale, shift)


def _resid_ln_fwd(x, out, gate, g, b, *, name):
    s, d = x.shape

    def body(x_ref, o_ref, gate_ref, g_ref, b_ref, y_ref):
        r = ALPHA * x_ref[...] + gate_ref[...] * o_ref[...]
        mu = jnp.mean(r, axis=-1, keepdims=True)
        rc = r - mu
        var = jnp.mean(rc * rc, axis=-1, keepdims=True)
        y_ref[...] = rc * lax.rsqrt(var + LN_EPS) * g_ref[...] + b_ref[...]

    row = pl.BlockSpec((ROW_TILE, d), lambda i: (i, 0))
    vec = pl.BlockSpec((1, d), lambda i: (0, 0))
    return pl.pallas_call(
        body, name=name, grid=(s // ROW_TILE,), in_specs=[row, row, vec, vec, vec], out_specs=row,
        out_shape=jax.ShapeDtypeStruct((s, d), F32), compiler_params=_params("parallel"),
    )(x, out, gate, g, b)


def _resid_ln_bwd(x, out, gate, g, dy, *, name):
    s, d = x.shape

    def body(x_ref, o_ref, gate_ref, g_ref, dy_ref, dxp_ref, dout_ref, dgate_ref, dg_ref, db_ref):
        @pl.when(pl.program_id(0) == 0)
        def _():
            dgate_ref[...] = jnp.zeros_like(dgate_ref)
            dg_ref[...] = jnp.zeros_like(dg_ref)
            db_ref[...] = jnp.zeros_like(db_ref)

        o = o_ref[...]
        r = ALPHA * x_ref[...] + gate_ref[...] * o
        mu = jnp.mean(r, axis=-1, keepdims=True)
        rc = r - mu
        var = jnp.mean(rc * rc, axis=-1, keepdims=True)
        rstd = lax.rsqrt(var + LN_EPS)
        xhat = rc * rstd
        dy = dy_ref[...]
        dxh = dy * g_ref[...]
        dr = rstd * (dxh - jnp.mean(dxh, axis=-1, keepdims=True) - xhat * jnp.mean(dxh * xhat, axis=-1, keepdims=True))
        dxp_ref[...] = ALPHA * dr
        dout_ref[...] = (gate_ref[...] * dr).astype(BF16)
        dgate_ref[...] += jnp.sum(dr * o, axis=0, keepdims=True)
        dg_ref[...] += jnp.sum(dy * xhat, axis=0, keepdims=True)
        db_ref[...] += jnp.sum(dy, axis=0, keepdims=True)

    row = pl.BlockSpec((ROW_TILE, d), lambda i: (i, 0))
    vec = pl.BlockSpec((1, d), lambda i: (0, 0))
    vshape = jax.ShapeDtypeStruct((1, d), F32)
    return pl.pallas_call(
        body, name=name, grid=(s // ROW_TILE,), in_specs=[row, row, vec, vec, row],
        out_specs=[row, row, vec, vec, vec],
        out_shape=[jax.ShapeDtypeStruct((s, d), F32), jax.ShapeDtypeStruct((s, d), BF16), vshape, vshape, vshape],
        compiler_params=_params("arbitrary"),
    )(x, out, gate, g, dy)


def _loss_and_grad(y, target, *, name):
    s, d = y.shape

    def body(y_ref, t_ref, dy_ref, loss_ref):
        @pl.when(pl.program_id(0) == 0)
        def _():
            loss_ref[...] = jnp.zeros_like(loss_ref)

        e = y_ref[...] - t_ref[...]
        dy_ref[...] = e * (1.0 / d)
        loss_ref[...] += jnp.sum(jnp.sum(e * e, axis=0, keepdims=True), axis=1, keepdims=True) * (0.5 / d)

    row = pl.BlockSpec((ROW_TILE, d), lambda i: (i, 0))
    return pl.pallas_call(
        body, name=name, grid=(s // ROW_TILE,), in_specs=[row, row],
        out_specs=[row, pl.BlockSpec((1, 128), lambda i: (0, 0))],
        out_shape=[jax.ShapeDtypeStruct((s, d), F32), jax.ShapeDtypeStruct((1, 128), F32)],
        compiler_params=_params("arbitrary"),
    )(y, target)


_SLOPES = np.asarray(2.0 ** (-8.0 * (np.arange(A_HEADS, dtype=np.float32) + 1.0) / A_HEADS), dtype=np.float32)


def _attn_scores(q, kw, slope, dist, valid):
    s = lax.dot_general(q, kw, _DIMS["nt"], preferred_element_type=F32) * (1.0 / math.sqrt(A_HEAD_DIM))
    return jnp.where(valid, s - slope * dist, NEG_BIG)


def _attn_window(blk, length, win, dil):
    start = pl.multiple_of(jnp.clip(blk * A_QBLOCK - A_RADIUS, 0, length - win), A_RADIUS)
    qpos = blk * A_QBLOCK + lax.broadcasted_iota(jnp.int32, (A_QBLOCK, win), 0)
    kpos = start + lax.broadcasted_iota(jnp.int32, (A_QBLOCK, win), 1)
    delta = jnp.abs(kpos - qpos)
    return start, (delta * dil).astype(F32), delta <= A_RADIUS


A_BLOCKS_PER_STEP = 4
A_GROUP_COLS = 3 * A_WIDTH


def _tile_scratch(rows, width):
    return pltpu.VMEM((width // 128, rows, 128), F32)


def _put_tile(scr, val):
    for j in range(scr.shape[0]):
        scr[j] = val[:, j * 128:(j + 1) * 128]


def _get_tile(scr):
    return jnp.concatenate([scr[j] for j in range(scr.shape[0])], axis=1)


def _get_residue(scr, r, dil):
    rows = pl.ds(r, scr.shape[1] // dil, stride=dil)
    return jnp.concatenate([scr.at[j][rows, :] for j in range(scr.shape[0])], axis=1)


def _put_residue(scr, r, dil, val):
    rows = pl.ds(r, scr.shape[1] // dil, stride=dil)
    for j in range(scr.shape[0]):
        scr.at[j][rows, :] = val[:, j * 128:(j + 1) * 128]


def _mm_dilated(a, b, dil, *, name):
    m, k = a.shape
    n = b.shape[1]
    tm = 512

    def body(a_ref, b_ref, o_ref, acc_ref):
        _put_tile(acc_ref, jnp.dot(a_ref[...], b_ref[...], preferred_element_type=F32))
        for r in range(dil):
            o_ref[:, r * n:(r + 1) * n] = _get_residue(acc_ref, r, dil).astype(BF16)

    return pl.pallas_call(
        body, name=name, grid=(m // tm,),
        in_specs=[pl.BlockSpec((tm, k), lambda i: (i, 0)), pl.BlockSpec((k, n), lambda i: (0, 0))],
        out_specs=pl.BlockSpec((tm // dil, dil * n), lambda i: (i, 0)),
        out_shape=jax.ShapeDtypeStruct((m // dil, dil * n), BF16), scratch_shapes=[_tile_scratch(tm, n)],
        compiler_params=_params("parallel"),
    )(a, b)


def _attn_fwd(pv, group, *, name, riders=()):
    dil = DILATIONS[group]
    length = pv.shape[0]
    cb, qoff = pv.shape[1] // (128 * dil), 0
    win = min(2 * A_QBLOCK, length)
    nblk = length // A_QBLOCK
    per = A_BLOCKS_PER_STEP if nblk % A_BLOCKS_PER_STEP == 0 else 1

    def body(slope_ref, q_ref, k_ref, v_ref, o_ref, lse_ref):
        hp = pl.program_id(1)
        for u in range(per):
            rows = slice(u * A_QBLOCK, (u + 1) * A_QBLOCK)
            start, dist, valid = _attn_window(pl.program_id(2) * per + u, length, win, dil)
            kw = k_ref[pl.ds(start, win), :]
            vw = v_ref[pl.ds(start, win), :]
            q = q_ref[rows, :]
            outs, lses = [], []
            for hh in range(2):
                sl = slice(hh * A_HEAD_DIM, (hh + 1) * A_HEAD_DIM)
                sc = _attn_scores(q[:, sl], kw[:, sl], slope_ref[hp * 2 + hh], dist, valid)
                m = jnp.max(sc, axis=-1, keepdims=True)
                p = jnp.exp(sc - m)
                z = jnp.sum(p, axis=-1, keepdims=True)
                o = jnp.dot(p.astype(BF16), vw[:, sl], preferred_element_type=F32) / z
                outs.append(o)
                lses.append(jnp.broadcast_to(m + jnp.log(z), (A_QBLOCK, A_HEAD_DIM)))
            o_ref[rows, :] = jnp.concatenate(outs, axis=1)
            lse_ref[rows, :] = jnp.concatenate(lses, axis=1)

    qspec = pl.BlockSpec((per * A_QBLOCK, 128), lambda r, hp, b: (b, r * cb + qoff + hp))
    kspec = pl.BlockSpec((length, 128), lambda r, hp, b: (0, r * cb + qoff + 8 + hp))
    vspec = pl.BlockSpec((length, 128), lambda r, hp, b: (0, r * cb + qoff + 16 + hp))
    ospec = pl.BlockSpec((per * A_QBLOCK, 128), lambda r, hp, b: (b, r * 8 + hp))
    oshape = jax.ShapeDtypeStruct((length, dil * A_WIDTH), F32)
    grid = (dil, 8, nblk // per)
    anywhere, exchanged, sems = _rider_specs(riders)
    outs = pl.pallas_call(
        _riding(body, riders, 4, 2, grid), name=name, grid=grid,
        in_specs=[pl.BlockSpec(memory_space=pltpu.SMEM), qspec, kspec, vspec] + anywhere, out_specs=[ospec, ospec] + anywhere,
        out_shape=[oshape, oshape] + exchanged, scratch_shapes=sems,
        compiler_params=_params(*(("arbitrary",) * 3 if riders else ("parallel", "parallel", "arbitrary"))),
    )(jnp.asarray(_SLOPES), pv, pv, pv, *[x for x, _ in riders])
    return (outs[0], outs[1], list(outs[2:])) if riders else (outs[0], outs[1])


def _attn_bwd(pv, group, do, o, lse, *, name):
    dil = DILATIONS[group]
    length = pv.shape[0]
    cb, qoff = pv.shape[1] // (128 * dil), 0
    win = min(2 * A_QBLOCK, length)
    nblk = length // A_QBLOCK
    per = A_BLOCKS_PER_STEP if nblk % A_BLOCKS_PER_STEP == 0 else 1
    nstep = nblk // per
    scale = 1.0 / math.sqrt(A_HEAD_DIM)

    def body(slope_ref, q_ref, k_ref, v_ref, do_ref, o_ref, lse_ref, dq_ref, dk_ref, dv_ref, dk_acc, dv_acc):
        hp, step = pl.program_id(1), pl.program_id(2)

        @pl.when(step == 0)
        def _():
            dk_acc[...] = jnp.zeros_like(dk_acc)
            dv_acc[...] = jnp.zeros_like(dv_acc)

        for u in range(per):
            rows = slice(u * A_QBLOCK, (u + 1) * A_QBLOCK)
            start, dist, valid = _attn_window(step * per + u, length, win, dil)
            kw = k_ref[pl.ds(start, win), :]
            vw = v_ref[pl.ds(start, win), :]
            q = q_ref[rows, :]
            do_b = do_ref[rows, :]
            dsum = do_b.astype(F32) * o_ref[rows, :]
            lse_b = lse_ref[rows, :]
            dqs, dks, dvs = [], [], []
            for hh in range(2):
                sl = slice(hh * A_HEAD_DIM, (hh + 1) * A_HEAD_DIM)
                sc = _attn_scores(q[:, sl], kw[:, sl], slope_ref[hp * 2 + hh], dist, valid)
                p = jnp.exp(sc - lse_b[:, hh * A_HEAD_DIM:hh * A_HEAD_DIM + 1])
                dp = lax.dot_general(do_b[:, sl], vw[:, sl], _DIMS["nt"], preferred_element_type=F32)
                ds = (p * (dp - jnp.sum(dsum[:, sl], axis=-1, keepdims=True))).astype(BF16)
                dqs.append(jnp.dot(ds, kw[:, sl], preferred_element_type=F32) * scale)
                dks.append(lax.dot_general(ds, q[:, sl], _DIMS["tn"], preferred_element_type=F32) * scale)
                dvs.append(lax.dot_general(p.astype(BF16), do_b[:, sl], _DIMS["tn"], preferred_element_type=F32))
            dq_ref[rows, :] = jnp.concatenate(dqs, axis=1).astype(BF16)
            dk_acc[pl.ds(start, win), :] += jnp.concatenate(dks, axis=1)
            dv_acc[pl.ds(start, win), :] += jnp.concatenate(dvs, axis=1)

        @pl.when(step == nstep - 1)
        def _():
            dk_ref[...] = dk_acc[...].astype(BF16)
            dv_ref[...] = dv_acc[...].astype(BF16)

    qspec = pl.BlockSpec((per * A_QBLOCK, 128), lambda r, hp, b: (b, r * cb + qoff + hp))
    kspec = pl.BlockSpec((length, 128), lambda r, hp, b: (0, r * cb + qoff + 8 + hp))
    vspec = pl.BlockSpec((length, 128), lambda r, hp, b: (0, r * cb + qoff + 16 + hp))
    bspec = pl.BlockSpec((per * A_QBLOCK, 128), lambda r, hp, b: (b, r * 8 + hp))
    fspec = pl.BlockSpec((length, 128), lambda r, hp, b: (0, r * 8 + hp))
    oshape = jax.ShapeDtypeStruct((length, dil * A_WIDTH), BF16)
    dq, dk, dv = pl.pallas_call(
        body, name=name, grid=(dil, 8, nstep),
        in_specs=[pl.BlockSpec(memory_space=pltpu.SMEM), qspec, kspec, vspec, bspec, bspec, bspec],
        out_specs=[bspec, fspec, fspec], out_shape=[oshape, oshape, oshape],
        scratch_shapes=[pltpu.VMEM((length, 128), F32), pltpu.VMEM((length, 128), F32)],
        compiler_params=_params("parallel", "parallel", "arbitrary"),
    )(jnp.asarray(_SLOPES), pv, pv, pv, do, o, lse)
    return dq, dk, dv


A_GATE_BLOCK = 3
A_ROWS = 256


def _lanes_of(r):
    return slice(r * A_WIDTH, (r + 1) * A_WIDTH)


def _dilated_spec(dil):
    return pl.BlockSpec((A_ROWS // dil, dil * A_WIDTH), lambda i: (i, 0))


def _attn_combine(o0, l0, o1, l1, o2, l2, proj0, *, name):
    s = proj0.shape[0]

    def body(o0_ref, l0_ref, o1_ref, l1_ref, o2_ref, l2_ref, gate_ref, y_ref, o_ref, lse_ref, so1, sl1, so2, sl2):
        for src, dst, dil in ((o1_ref, so1, DILATIONS[1]), (l1_ref, sl1, DILATIONS[1]),
                              (o2_ref, so2, DILATIONS[2]), (l2_ref, sl2, DILATIONS[2])):
            for r in range(dil):
                _put_residue(dst, r, dil, src[:, _lanes_of(r)])
        la, lb, lc = l0_ref[...], _get_tile(sl1), _get_tile(sl2)
        m = jnp.maximum(jnp.maximum(la, lb), lc)
        ea, eb, ec = jnp.exp(la - m), jnp.exp(lb - m), jnp.exp(lc - m)
        den = ea + eb + ec
        o = (ea * o0_ref[...] + eb * _get_tile(so1) + ec * _get_tile(so2)) / den
        o_ref[...] = o
        lse_ref[...] = m + jnp.log(den)
        y_ref[...] = (o * _silu_and_grad(gate_ref[...].astype(F32))[0]).astype(BF16)

    row = pl.BlockSpec((A_ROWS, A_WIDTH), lambda i: (i, 0))
    gspec = pl.BlockSpec((A_ROWS, A_WIDTH), lambda i: (i, A_GATE_BLOCK))
    d1, d2 = _dilated_spec(DILATIONS[1]), _dilated_spec(DILATIONS[2])
    return pl.pallas_call(
        body, name=name, grid=(s // A_ROWS,), in_specs=[row, row, d1, d1, d2, d2, gspec], out_specs=[row, row, row],
        out_shape=[jax.ShapeDtypeStruct((s, A_WIDTH), BF16), jax.ShapeDtypeStruct((s, A_WIDTH), F32),
                   jax.ShapeDtypeStruct((s, A_WIDTH), F32)],
        scratch_shapes=[_tile_scratch(A_ROWS, A_WIDTH)] * 4, compiler_params=_params("parallel"),
    )(o0, l0, o1, l1, o2, l2, proj0)


def _attn_combine_bwd(dy, o, lse, proj0, *, name):
    s = proj0.shape[0]

    def body(dy_ref, o_ref, lse_ref, gate_ref, dg_ref, do_ref, do1, o1, l1, do2, o2, l2, s_do, s_o, s_l):
        si, dsi = _silu_and_grad(gate_ref[...].astype(F32))
        dyv = dy_ref[...]
        ov = o_ref[...]
        do = dyv * si
        _put_tile(s_do, do)
        _put_tile(s_o, ov)
        _put_tile(s_l, lse_ref[...])
        do_ref[...] = do.astype(BF16)
        dg_ref[...] = (dyv * ov * dsi).astype(BF16)
        for (do_d, o_d, l_d), dil in (((do1, o1, l1), DILATIONS[1]), ((do2, o2, l2), DILATIONS[2])):
            for r in range(dil):
                do_d[:, _lanes_of(r)] = _get_residue(s_do, r, dil).astype(BF16)
                o_d[:, _lanes_of(r)] = _get_residue(s_o, r, dil)
                l_d[:, _lanes_of(r)] = _get_residue(s_l, r, dil)

    row = pl.BlockSpec((A_ROWS, A_WIDTH), lambda i: (i, 0))
    gspec = pl.BlockSpec((A_ROWS, A_WIDTH), lambda i: (i, A_GATE_BLOCK))
    shp = jax.ShapeDtypeStruct((s, A_WIDTH), BF16)
    dilated = lambda dil, dtype: jax.ShapeDtypeStruct((s // dil, dil * A_WIDTH), dtype)
    d1, d2 = _dilated_spec(DILATIONS[1]), _dilated_spec(DILATIONS[2])
    return pl.pallas_call(
        body, name=name, grid=(s // A_ROWS,), in_specs=[row, row, row, gspec],
        out_specs=[row, row, d1, d1, d1, d2, d2, d2],
        out_shape=[shp, shp, dilated(DILATIONS[1], BF16), dilated(DILATIONS[1], F32), dilated(DILATIONS[1], F32),
                   dilated(DILATIONS[2], BF16), dilated(DILATIONS[2], F32), dilated(DILATIONS[2], F32)],
        scratch_shapes=[_tile_scratch(A_ROWS, A_WIDTH)] * 3, compiler_params=_params("parallel"),
    )(dy, o, lse, proj0)


def _assemble_dproj(parts0, parts1, parts2, dgate, *, name):
    s = dgate.shape[0]

    def body(*refs):
        ins, out_ref, scr = refs[:10], refs[10], refs[11]
        for p in range(3):
            out_ref[:, _lanes_of(p)] = ins[p][...]
        for g, dil in ((1, DILATIONS[1]), (2, DILATIONS[2])):
            for p in range(3):
                src = ins[3 * g + p]
                for r in range(dil):
                    _put_residue(scr, r, dil, src[:, _lanes_of(r)].astype(F32))
                out_ref[:, _lanes_of(3 * g + p)] = _get_tile(scr).astype(BF16)
        out_ref[:, _lanes_of(9)] = ins[9][...]

    row = pl.BlockSpec((A_ROWS, A_WIDTH), lambda i: (i, 0))
    d1, d2 = _dilated_spec(DILATIONS[1]), _dilated_spec(DILATIONS[2])
    return pl.pallas_call(
        body, name=name, grid=(s // A_ROWS,), in_specs=[row] * 3 + [d1] * 3 + [d2] * 3 + [row],
        out_specs=pl.BlockSpec((A_ROWS, A_IN_COLS), lambda i: (i, 0)),
        out_shape=jax.ShapeDtypeStruct((s, A_IN_COLS), BF16),
        scratch_shapes=[_tile_scratch(A_ROWS, A_WIDTH)], compiler_params=_params("parallel"),
    )(*parts0, *parts1, *parts2, dgate)


CONV_TILE = 256
CONV_SUB = 4


def _conv_taps(xe, n):
    return [xe if j == 2 else pltpu.roll(xe, (2 - j) % n, 0) for j in range(SSM_CONV)]


def _conv_fwd(xpad, w, b, *, name):
    s = xpad.shape[0] - 2 * CONV_HALO
    n = CONV_TILE + 2 * CONV_HALO
    ncol = SSM_CONV_DIM // 128

    sub = min(CONV_SUB, s // CONV_TILE)

    def body(x_ref, w_ref, b_ref, o_ref):
        base = pl.program_id(1) * (sub * CONV_TILE)

        def tile(k, carry):
            r0 = pl.multiple_of(k * CONV_TILE, CONV_TILE)
            t0 = pl.multiple_of(base + r0, CONV_TILE)
            taps = _conv_taps(x_ref[pl.ds(t0, n), :].astype(F32), n)
            pre = b_ref[...]
            for j in range(SSM_CONV):
                pre = pre + w_ref[j:j + 1, :] * taps[j]
            o_ref[pl.ds(r0, CONV_TILE), :] = _silu_and_grad(pre[CONV_HALO:CONV_HALO + CONV_TILE])[0].astype(BF16)
            return carry

        lax.fori_loop(0, sub, tile, 0)

    return pl.pallas_call(
        body, name=name, grid=(ncol, s // (sub * CONV_TILE)),
        in_specs=[pl.BlockSpec((s + 2 * CONV_HALO, 128), lambda j, i: (0, j)),
                  pl.BlockSpec((SSM_CONV, 128), lambda j, i: (0, j)), pl.BlockSpec((1, 128), lambda j, i: (0, j))],
        out_specs=pl.BlockSpec((sub * CONV_TILE, 128), lambda j, i: (i, j)),
        out_shape=jax.ShapeDtypeStruct((s, SSM_CONV_DIM), BF16), compiler_params=_params("parallel", "arbitrary"),
    )(xpad, w, b)


def _conv_bwd(xpad, dapad, w, b, *, name):
    s = xpad.shape[0] - 2 * CONV_HALO
    n = CONV_TILE + 2 * CONV_HALO
    ncol = SSM_CONV_DIM // 128
    mid = slice(CONV_HALO, CONV_HALO + CONV_TILE)
    sub = min(CONV_SUB, s // CONV_TILE)

    def body(x_ref, da_ref, w_ref, b_ref, dx_ref, dw_ref, db_ref):
        @pl.when(pl.program_id(1) == 0)
        def _():
            dw_ref[...] = jnp.zeros_like(dw_ref)
            db_ref[...] = jnp.zeros_like(db_ref)

        base = pl.program_id(1) * (sub * CONV_TILE)

        def tile(k, carry):
            r0 = pl.multiple_of(k * CONV_TILE, CONV_TILE)
            t0 = pl.multiple_of(base + r0, CONV_TILE)
            taps = _conv_taps(x_ref[pl.ds(t0, n), :].astype(F32), n)
            pre = b_ref[...]
            for j in range(SSM_CONV):
                pre = pre + w_ref[j:j + 1, :] * taps[j]
            dpre = da_ref[pl.ds(t0, n), :] * _silu_and_grad(pre)[1]
            dx = jnp.zeros((CONV_TILE, 128), F32)
            for j in range(SSM_CONV):
                back = dpre if j == 2 else pltpu.roll(dpre, (j - 2) % n, 0)
                dx = dx + w_ref[j:j + 1, :] * back[mid]
                dw_ref[j:j + 1, :] += jnp.sum(dpre[mid] * taps[j][mid], axis=0, keepdims=True)
            dx_ref[pl.ds(r0, CONV_TILE), :] = dx.astype(BF16)
            db_ref[...] += jnp.sum(dpre[mid], axis=0, keepdims=True)
            return carry

        lax.fori_loop(0, sub, tile, 0)

    full = pl.BlockSpec((s + 2 * CONV_HALO, 128), lambda j, i: (0, j))
    wspec = pl.BlockSpec((SSM_CONV, 128), lambda j, i: (0, j))
    bspec = pl.BlockSpec((1, 128), lambda j, i: (0, j))
    return pl.pallas_call(
        body, name=name, grid=(ncol, s // (sub * CONV_TILE)), in_specs=[full, full, wspec, bspec],
        out_specs=[pl.BlockSpec((sub * CONV_TILE, 128), lambda j, i: (i, j)), wspec, bspec],
        out_shape=[jax.ShapeDtypeStruct((s, SSM_CONV_DIM), BF16), jax.ShapeDtypeStruct((SSM_CONV, SSM_CONV_DIM), F32),
                   jax.ShapeDtypeStruct((1, SSM_CONV_DIM), F32)],
        compiler_params=_params("parallel", "arbitrary"),
    )(xpad, dapad, w, b)


HPG = SSM_HEADS // SSM_GROUPS
GW = HPG * SSM_HEAD_DIM
T = SSM_CHUNK


def _ssd_specs(nc):
    ceff = lambda d, c: jnp.where(d == 0, c, nc - 1 - c)
    return ceff, [
        pl.BlockSpec((T, GW), lambda d, g, c: (ceff(d, c), g)),
        pl.BlockSpec((T, SSM_STATE), lambda d, g, c: (ceff(d, c), SSM_INNER // 128 + g)),
        pl.BlockSpec((T, SSM_STATE), lambda d, g, c: (ceff(d, c), SSM_INNER // 128 + SSM_GROUPS + g)),
        pl.BlockSpec((None, None, T, HPG), lambda d, g, c: (d, g, ceff(d, c), 0)),
        pl.BlockSpec((None, None, HPG, T), lambda d, g, c: (d, g, 0, ceff(d, c))),
        pl.BlockSpec((None, None, 2, HPG), lambda d, g, c: (d, g, 0, 0)),
        pl.BlockSpec((None, None, HPG, 2), lambda d, g, c: (d, g, 0, 0)),
    ]


def _ssd_chunk_common(d, dt_ref, dtt_ref, prr_ref, prc_ref):
    sgn = 1 - 2 * d
    ri = lax.broadcasted_iota(jnp.int32, (T, T), 0)
    ci = lax.broadcasted_iota(jnp.int32, (T, T), 1)
    mask = ((ri - ci) * sgn) >= 0
    maskf = mask.astype(F32)
    bias_r, a_r = prr_ref[0:1, :], prr_ref[1:2, :]
    bias_c, a_c = prc_ref[:, 0:1], prc_ref[:, 1:2]
    raw = dt_ref[...] + bias_r
    dt_rows = _softplus(raw)
    dt_lanes = _softplus(dtt_ref[...] + bias_c)
    a_rows = dt_rows * a_r
    acum_rows = jnp.dot(maskf, a_rows, precision=HIGHEST, preferred_element_type=F32)
    acum_lanes = lax.dot_general(dt_lanes * a_c, maskf, _DIMS["nt"], precision=HIGHEST, preferred_element_type=F32)
    tot = jnp.sum(a_rows, axis=0, keepdims=True)
    return mask, maskf, raw, dt_rows, a_r, acum_rows, acum_lanes, tot


def _lanes_per_head(pieces):
    return jnp.concatenate([jnp.broadcast_to(p, (p.shape[0], SSM_HEAD_DIM)) for p in pieces], axis=1)


def _ssd_fwd_v1(xbc, dtr, dtt, pr_rows, pr_cols, *, name):
    s = xbc.shape[0]
    nc = s // T
    ceff, in_specs = _ssd_specs(nc)

    def body(x_ref, b_ref, c_ref, dt_ref, dtt_ref, prr_ref, prc_ref, y_ref, hs_ref, st_ref):
        d, c = pl.program_id(0), pl.program_id(2)

        @pl.when(c == 0)
        def _():
            st_ref[...] = jnp.zeros_like(st_ref)

        mask, _, _, dt_rows, _, acum_rows, acum_lanes, tot = _ssd_chunk_common(d, dt_ref, dtt_ref, prr_ref, prc_ref)
        xs = x_ref[...].astype(F32)
        bm, cm = b_ref[...], c_ref[...]
        hprev = st_ref[...]
        hs_ref[...] = hprev
        cb = lax.dot_general(cm, bm, _DIMS["nt"], preferred_element_type=F32)
        ch = jnp.dot(cm, hprev.astype(BF16), preferred_element_type=F32)
        ys, xgds, etots = [], [], []
        for j in range(HPG):
            sl = slice(j * SSM_HEAD_DIM, (j + 1) * SSM_HEAD_DIM)
            ac, al = acum_rows[:, j:j + 1], acum_lanes[j:j + 1, :]
            lm = jnp.where(mask, jnp.exp(jnp.minimum(ac - al, 0.0)), 0.0)
            xg = xs[:, sl] * dt_rows[:, j:j + 1]
            yd = jnp.dot((cb * lm).astype(BF16), xg.astype(BF16), preferred_element_type=F32)
            ys.append(yd + jnp.exp(ac) * ch[:, sl])
            xgds.append(xg * jnp.exp(tot[:, j:j + 1] - ac))
            etots.append(jnp.exp(tot[:, j:j + 1]))
        y_ref[...] = jnp.concatenate(ys, axis=1)
        new = lax.dot_general(bm, jnp.concatenate(xgds, axis=1).astype(BF16), _DIMS["tn"], preferred_element_type=F32)
        st_ref[...] = hprev * _lanes_per_head(etots) + new

    return pl.pallas_call(
        body, name=name, grid=(2, SSM_GROUPS, nc), in_specs=in_specs,
        out_specs=[pl.BlockSpec((None, T, GW), lambda d, g, c: (d, ceff(d, c), g)),
                   pl.BlockSpec((None, None, None, SSM_STATE, GW), lambda d, g, c: (d, ceff(d, c), g, 0, 0))],
        out_shape=[jax.ShapeDtypeStruct((2, s, SSM_INNER), F32),
                   jax.ShapeDtypeStruct((2, nc, SSM_GROUPS, SSM_STATE, GW), F32)],
        scratch_shapes=[pltpu.VMEM((SSM_STATE, GW), F32)],
        compiler_params=_params("parallel", "parallel", "arbitrary"),
    )(xbc, xbc, xbc, dtr, dtt, pr_rows, pr_cols)


def _put_lane(j, col):
    lane = lax.broadcasted_iota(jnp.int32, (col.shape[0], HPG), 1)
    return jnp.where(lane == j, col, 0.0)


def _ssd_bwd_v1(xbc, dtr, dtt, pr_rows, pr_cols, dvec, hs, dy, *, name):
    s = xbc.shape[0]
    nc = s // T
    cb_of = lambda d, c: jnp.where(d == 0, nc - 1 - c, c)
    in_specs = [
        pl.BlockSpec((T, GW), lambda d, g, c: (cb_of(d, c), g)),
        pl.BlockSpec((T, SSM_STATE), lambda d, g, c: (cb_of(d, c), SSM_INNER // 128 + g)),
        pl.BlockSpec((T, SSM_STATE), lambda d, g, c: (cb_of(d, c), SSM_INNER // 128 + SSM_GROUPS + g)),
        pl.BlockSpec((None, None, T, HPG), lambda d, g, c: (d, g, cb_of(d, c), 0)),
        pl.BlockSpec((None, None, HPG, T), lambda d, g, c: (d, g, 0, cb_of(d, c))),
        pl.BlockSpec((None, None, 2, HPG), lambda d, g, c: (d, g, 0, 0)),
        pl.BlockSpec((None, None, HPG, 2), lambda d, g, c: (d, g, 0, 0)),
        pl.BlockSpec((1, GW), lambda d, g, c: (0, g)),
        pl.BlockSpec((None, None, None, SSM_STATE, GW), lambda d, g, c: (d, cb_of(d, c), g, 0, 0)),
        pl.BlockSpec((T, GW), lambda d, g, c: (cb_of(d, c), g)),
    ]

    def body(x_ref, b_ref, c_ref, dt_ref, dtt_ref, prr_ref, prc_ref, dvec_ref, hs_ref, dy_ref,
             dxs_ref, db_ref, dc_ref, ddt_ref, dalog_ref, dbias_ref, g_ref):
        d, c = pl.program_id(0), pl.program_id(2)

        @pl.when(c == 0)
        def _():
            g_ref[...] = jnp.zeros_like(g_ref)
            dalog_ref[...] = jnp.zeros_like(dalog_ref)
            dbias_ref[...] = jnp.zeros_like(dbias_ref)

        mask, maskf, raw, dt_rows, a_r, acum_rows, acum_lanes, tot = _ssd_chunk_common(
            d, dt_ref, dtt_ref, prr_ref, prc_ref)
        xs = x_ref[...].astype(F32)
        bm, cm = b_ref[...], c_ref[...]
        hst = hs_ref[...]
        gst = g_ref[...]
        dyv = dy_ref[...]
        dyb = dyv.astype(BF16)
        dv = dvec_ref[...] * (1 - d).astype(F32)
        cb = lax.dot_general(cm, bm, _DIMS["nt"], preferred_element_type=F32)
        ch = jnp.dot(cm, hst.astype(BF16), preferred_element_type=F32)
        bg = jnp.dot(bm, gst.astype(BF16), preferred_element_type=F32)
        hg = hst * gst
        dcb = jnp.zeros((T, T), F32)
        dacum = jnp.zeros((T, HPG), F32)
        rx = jnp.zeros((T, HPG), F32)
        dtot = jnp.zeros((1, HPG), F32)
        dxss, dyes, xgds, etots = [], [], [], []
        for j in range(HPG):
            sl = slice(j * SSM_HEAD_DIM, (j + 1) * SSM_HEAD_DIM)
            ac, al = acum_rows[:, j:j + 1], acum_lanes[j:j + 1, :]
            lm = jnp.where(mask, jnp.exp(jnp.minimum(ac - al, 0.0)), 0.0)
            m = cb * lm
            dtj = dt_rows[:, j:j + 1]
            xsj = xs[:, sl]
            xg = xsj * dtj
            dyj = dyv[:, sl]
            ec = jnp.exp(ac)
            etot = jnp.exp(tot[:, j:j + 1])
            decay = jnp.exp(tot[:, j:j + 1] - ac)
            dm = lax.dot_general(dyb[:, sl], xg.astype(BF16), _DIMS["nt"], preferred_element_type=F32)
            w = dm * m
            dcb = dcb + dm * lm
            bgj = bg[:, sl]
            dxg = lax.dot_general(m.astype(BF16), dyb[:, sl], _DIMS["tn"], preferred_element_type=F32) + decay * bgj
            xb = decay * jnp.sum(xg * bgj, axis=-1, keepdims=True)
            da_j = (jnp.sum(w, axis=-1, keepdims=True) - jnp.sum(w.T, axis=-1, keepdims=True)
                    + jnp.sum(ec * ch[:, sl] * dyj, axis=-1, keepdims=True) - xb)
            dacum = dacum + _put_lane(j, da_j)
            rx = rx + _put_lane(j, jnp.sum(dxg * xsj, axis=-1, keepdims=True))
            dtot_j = (etot * jnp.sum(jnp.sum(hg[:, sl], axis=0, keepdims=True), axis=1, keepdims=True)
                      + jnp.sum(xb, axis=0, keepdims=True))
            dtot = dtot + _put_lane(j, dtot_j)
            dxss.append(dxg * dtj + dv[:, sl] * dyj)
            dyes.append(dyj * ec)
            xgds.append(xg * decay)
            etots.append(etot)
        da = lax.dot_general(maskf, dacum, _DIMS["tn"], precision=HIGHEST, preferred_element_type=F32) + dtot
        ddt = da * a_r + rx
        draw = ddt * _sigmoid(raw)
        ddt_ref[...] = draw
        dbias_ref[...] += jnp.sum(draw, axis=0, keepdims=True)
        dalog_ref[...] += jnp.sum(da * dt_rows, axis=0, keepdims=True) * a_r
        dxs_ref[...] = jnp.concatenate(dxss, axis=1)
        dye = jnp.concatenate(dyes, axis=1).astype(BF16)
        xgd = jnp.concatenate(xgds, axis=1).astype(BF16)
        dcbb = dcb.astype(BF16)
        dc_ref[...] = (jnp.dot(dcbb, bm, preferred_element_type=F32)
                       + lax.dot_general(dye, hst.astype(BF16), _DIMS["nt"], preferred_element_type=F32))
        db_ref[...] = (lax.dot_general(dcbb, cm, _DIMS["tn"], preferred_element_type=F32)
                       + lax.dot_general(xgd, gst.astype(BF16), _DIMS["nt"], preferred_element_type=F32))
        g_ref[...] = lax.dot_general(cm, dye, _DIMS["tn"], preferred_element_type=F32) + gst * _lanes_per_head(etots)

    small = pl.BlockSpec((None, None, 1, HPG), lambda d, g, c: (d, g, 0, 0))
    sshape = jax.ShapeDtypeStruct((2, SSM_GROUPS, 1, HPG), F32)
    return pl.pallas_call(
        body, name=name, grid=(2, SSM_GROUPS, nc), in_specs=in_specs,
        out_specs=[pl.BlockSpec((None, T, GW), lambda d, g, c: (d, cb_of(d, c), g)),
                   pl.BlockSpec((None, T, SSM_STATE), lambda d, g, c: (d, cb_of(d, c), g)),
                   pl.BlockSpec((None, T, SSM_STATE), lambda d, g, c: (d, cb_of(d, c), g)),
                   pl.BlockSpec((None, None, T, HPG), lambda d, g, c: (d, g, cb_of(d, c), 0)), small, small],
        out_shape=[jax.ShapeDtypeStruct((2, s, SSM_INNER), F32),
                   jax.ShapeDtypeStruct((2, s, SSM_GROUPS * SSM_STATE), F32),
                   jax.ShapeDtypeStruct((2, s, SSM_GROUPS * SSM_STATE), F32),
                   jax.ShapeDtypeStruct((2, SSM_GROUPS, s, HPG), F32), sshape, sshape],
        scratch_shapes=[pltpu.VMEM((SSM_STATE, GW), F32)],
        compiler_params=_params("parallel", "parallel", "arbitrary"),
    )(xbc, xbc, xbc, dtr, dtt, pr_rows, pr_cols, dvec, hs, dy)


PAIRS = HPG // 2


def _scan_lanes(x, forward):
    lane = lax.broadcasted_iota(jnp.int32, x.shape, 1)
    p = x
    k = 1
    while k < T:
        p = p + jnp.where(lane >= k, pltpu.roll(p, k, 1), 0.0)
        k *= 2
    tot = p[:, T - 1:T]
    return jnp.where(forward, p, tot - p + x), tot


def _ssd_chunk(d, dt_ref, dtt_ref, prr_ref, prc_ref):
    sgn = 1 - 2 * d
    ri = lax.broadcasted_iota(jnp.int32, (T, T), 0)
    ci = lax.broadcasted_iota(jnp.int32, (T, T), 1)
    mask = ((ri - ci) * sgn) >= 0
    mask_t = ((ci - ri) * sgn) >= 0
    bias_r = prr_ref[0:1, :]
    bias_c, a_c = prc_ref[:, 0:1], prc_ref[:, 1:2]
    dt_rows = _softplus(dt_ref[...] + bias_r)
    raw_lanes = dtt_ref[...] + bias_c
    dt_lanes = _softplus(raw_lanes)
    acum_lanes, tot = _scan_lanes(dt_lanes * a_c, d == 0)
    return dict(mask=mask, mask_t=mask_t, head0=ci < SSM_HEAD_DIM, dt_rows=dt_rows, raw_lanes=raw_lanes,
                dt_lanes=dt_lanes, a_c=a_c, acum_lanes=acum_lanes, acum_rows=acum_lanes.T, tot=tot)


def _ssd_pair(ck, q):
    h0 = ck["head0"]
    colb = lambda rows, j: jnp.broadcast_to(rows[:, j:j + 1], (T, T))
    rowb = lambda lanes, j: jnp.broadcast_to(lanes[j:j + 1, :], (T, T))
    lms, lmts, acs = [], [], []
    for j in (2 * q, 2 * q + 1):
        ac, al = colb(ck["acum_rows"], j), rowb(ck["acum_lanes"], j)
        lms.append(jnp.where(ck["mask"], jnp.exp(jnp.minimum(ac - al, 0.0)), 0.0))
        lmts.append(jnp.where(ck["mask_t"], jnp.exp(jnp.minimum(al - ac, 0.0)), 0.0))
        acs.append(ac)
    ac_pair = jnp.where(h0, acs[0], acs[1])
    dt_pair = jnp.where(h0, colb(ck["dt_rows"], 2 * q), colb(ck["dt_rows"], 2 * q + 1))
    tot_pair = jnp.where(h0[0:1], ck["tot"][2 * q:2 * q + 1, :], ck["tot"][2 * q + 1:2 * q + 2, :])
    return dict(lm=lms, lmt=lmts, dt=dt_pair, ec=jnp.exp(ac_pair), decay=jnp.exp(tot_pair - ac_pair),
                etot=jnp.exp(tot_pair))


def _split_heads(h0, v):
    zero = jnp.zeros_like(v)
    return jnp.where(h0, v, zero), jnp.where(h0, zero, v)


GPS = 2
GSTEPS = SSM_GROUPS // GPS
B_BLOCK0 = SSM_INNER // (GPS * SSM_STATE)
C_BLOCK0 = (SSM_INNER + SSM_GROUPS * SSM_STATE) // (GPS * SSM_STATE)


def _ssd_in_specs(chunk):
    return [
        pl.BlockSpec((T, GPS * GW), lambda d, g, c: (chunk(d, c), g)),
        pl.BlockSpec((T, GPS * SSM_STATE), lambda d, g, c: (chunk(d, c), B_BLOCK0 + g)),
        pl.BlockSpec((T, GPS * SSM_STATE), lambda d, g, c: (chunk(d, c), C_BLOCK0 + g)),
        pl.BlockSpec((GPS * SSM_STATE, T), lambda d, g, c: (g, chunk(d, c))),
        pl.BlockSpec((GPS * SSM_STATE, T), lambda d, g, c: (g, chunk(d, c))),
        pl.BlockSpec((None, GPS, T, HPG), lambda d, g, c: (d, g, chunk(d, c), 0)),
        pl.BlockSpec((None, GPS, HPG, T), lambda d, g, c: (d, g, 0, chunk(d, c))),
        pl.BlockSpec((None, GPS, 2, HPG), lambda d, g, c: (d, g, 0, 0)),
        pl.BlockSpec((None, GPS, HPG, 2), lambda d, g, c: (d, g, 0, 0)),
    ]


def _group_refs(gi, wide, state_wide, t_wide, lead):
    return ([r.at[:, pl.ds(gi * GW, GW)] for r in wide] + [r.at[:, pl.ds(gi * SSM_STATE, SSM_STATE)] for r in state_wide]
            + [r.at[pl.ds(gi * SSM_STATE, SSM_STATE), :] for r in t_wide] + [r.at[gi] for r in lead])


def _ssd_fwd(xbc, bt, ct, dtr, dtt, pr_rows, pr_cols, *, name, riders=()):
    s = xbc.shape[0]
    nc = s // T
    chunk = lambda d, c: jnp.where(d == 0, c, nc - 1 - c)

    def body(x_ref, b_ref, c_ref, bt_ref, ct_ref, dt_ref, dtt_ref, prr_ref, prc_ref, y_ref, hs_ref, st_ref):
        @pl.when(pl.program_id(2) == 0)
        def _():
            st_ref[...] = jnp.zeros_like(st_ref)

        for gi in range(GPS):
            group(*_group_refs(gi, [x_ref, y_ref], [b_ref, c_ref], [bt_ref, ct_ref],
                               [dt_ref, dtt_ref, prr_ref, prc_ref, hs_ref, st_ref]))

    def group(x_ref, y_ref, b_ref, c_ref, bt_ref, ct_ref, dt_ref, dtt_ref, prr_ref, prc_ref, hs_ref, st_ref):
        d = pl.program_id(0)
        ck = _ssd_chunk(d, dt_ref, dtt_ref, prr_ref, prc_ref)
        xs = x_ref[...].astype(F32)
        cm = c_ref[...]
        hprev = st_ref[...]
        hs_ref[...] = hprev
        cb = lax.dot_general(cm, b_ref[...], _DIMS["nt"], preferred_element_type=F32)
        ch = jnp.dot(cm, hprev.astype(BF16), preferred_element_type=F32)
        ys, xgds, etots = [], [], []
        for q in range(PAIRS):
            sl = slice(q * 128, (q + 1) * 128)
            pr = _ssd_pair(ck, q)
            xg = xs[:, sl] * pr["dt"]
            xg0, xg1 = _split_heads(ck["head0"], xg.astype(BF16))
            yd = (jnp.dot((cb * pr["lm"][0]).astype(BF16), xg0, preferred_element_type=F32)
                  + jnp.dot((cb * pr["lm"][1]).astype(BF16), xg1, preferred_element_type=F32))
            ys.append(yd + pr["ec"] * ch[:, sl])
            xgds.append(xg * pr["decay"])
            etots.append(pr["etot"])
        y_ref[...] = jnp.concatenate(ys, axis=1)
        new = jnp.dot(bt_ref[...], jnp.concatenate(xgds, axis=1).astype(BF16), preferred_element_type=F32)
        st_ref[...] = hprev * jnp.concatenate(etots, axis=1) + new

    grid = (2, GSTEPS, nc)
    anywhere, exchanged, sems = _rider_specs(riders)
    outs = pl.pallas_call(
        _riding(body, riders, 9, 2, grid), name=name, grid=grid, in_specs=_ssd_in_specs(chunk) + anywhere,
        out_specs=[pl.BlockSpec((None, T, GPS * GW), lambda d, g, c: (d, chunk(d, c), g)),
                   pl.BlockSpec((None, None, GPS, SSM_STATE, GW), lambda d, g, c: (d, chunk(d, c), g, 0, 0))] + anywhere,
        out_shape=[jax.ShapeDtypeStruct((2, s, SSM_INNER), F32),
                   jax.ShapeDtypeStruct((2, nc, SSM_GROUPS, SSM_STATE, GW), F32)] + exchanged,
        scratch_shapes=[pltpu.VMEM((GPS, SSM_STATE, GW), F32)] + sems,
        compiler_params=_params(*(("arbitrary",) * 3 if riders else ("parallel", "parallel", "arbitrary"))),
    )(xbc, xbc, xbc, bt, ct, dtr, dtt, pr_rows, pr_cols, *[x for x, _ in riders])
    return outs[0], outs[1], list(outs[2:])


def _ssd_bwd(xbc, bt, ct, dtr, dtt, pr_rows, pr_cols, dvec, hs, y2, dy, *, name, riders=()):
    s = xbc.shape[0]
    nc = s // T
    chunk = lambda d, c: jnp.where(d == 0, nc - 1 - c, c)
    in_specs = _ssd_in_specs(chunk) + [
        pl.BlockSpec((1, GPS * GW), lambda d, g, c: (0, g)),
        pl.BlockSpec((None, None, GPS, SSM_STATE, GW), lambda d, g, c: (d, chunk(d, c), g, 0, 0)),
        pl.BlockSpec((None, T, GPS * GW), lambda d, g, c: (d, chunk(d, c), g)),
        pl.BlockSpec((T, GPS * GW), lambda d, g, c: (chunk(d, c), g)),
    ]

    def body(x_ref, b_ref, c_ref, bt_ref, ct_ref, dt_ref, dtt_ref, prr_ref, prc_ref, dvec_ref, hs_ref, y_ref, dy_ref,
             dxs_ref, db_ref, dc_ref, ddt_ref, dalog_ref, dbias_ref, g_ref):
        @pl.when(pl.program_id(2) == 0)
        def _():
            g_ref[...] = jnp.zeros_like(g_ref)
            dalog_ref[...] = jnp.zeros_like(dalog_ref)
            dbias_ref[...] = jnp.zeros_like(dbias_ref)

        for gi in range(GPS):
            group(*_group_refs(gi, [x_ref, dvec_ref, y_ref, dy_ref, dxs_ref], [b_ref, c_ref, db_ref, dc_ref], [bt_ref, ct_ref],
                               [dt_ref, dtt_ref, prr_ref, prc_ref, hs_ref, ddt_ref, dalog_ref, dbias_ref, g_ref]))

    def group(x_ref, dvec_ref, y_ref, dy_ref, dxs_ref, b_ref, c_ref, db_ref, dc_ref, bt_ref, ct_ref,
              dt_ref, dtt_ref, prr_ref, prc_ref, hs_ref, ddt_ref, dalog_ref, dbias_ref, g_ref):
        d = pl.program_id(0)
        ck = _ssd_chunk(d, dt_ref, dtt_ref, prr_ref, prc_ref)
        h0 = ck["head0"]
        xs = x_ref[...].astype(F32)
        bm, cm = b_ref[...], c_ref[...]
        hst = hs_ref[...]
        gst = g_ref[...]
        dyv = dy_ref[...]
        yv = y_ref[...]
        dv = dvec_ref[...] * (1 - d).astype(F32)
        cb = lax.dot_general(cm, bm, _DIMS["nt"], preferred_element_type=F32)
        cbt = jnp.dot(bm, ct_ref[...], preferred_element_type=F32)
        ch = jnp.dot(cm, hst.astype(BF16), preferred_element_type=F32)
        bg = jnp.dot(bm, gst.astype(BF16), preferred_element_type=F32)
        hg_cols = jnp.sum(hst * gst, axis=0, keepdims=True)
        lane16 = lax.broadcasted_iota(jnp.int32, (T, 2 * HPG), 1)
        sub8 = lax.broadcasted_iota(jnp.int32, (HPG, 1), 0)
        dcb = jnp.zeros((T, T), F32)
        acc16 = jnp.zeros((T, 2 * HPG), F32)
        dtot = jnp.zeros((HPG, 1), F32)
        dxss, dyes, xgds, etots = [], [], [], []
        for q in range(PAIRS):
            sl = slice(q * 128, (q + 1) * 128)
            pr = _ssd_pair(ck, q)
            xsp, dyp = xs[:, sl], dyv[:, sl]
            xg = xsp * pr["dt"]
            xgb = xg.astype(BF16)
            dyb = dyp.astype(BF16)
            dy0, dy1 = _split_heads(h0, dyb)
            dcb = dcb + (lax.dot_general(dy0, xgb, _DIMS["nt"], preferred_element_type=F32) * pr["lm"][0]
                         + lax.dot_general(dy1, xgb, _DIMS["nt"], preferred_element_type=F32) * pr["lm"][1])
            dxg_in = (jnp.dot((cbt * pr["lmt"][0]).astype(BF16), dy0, preferred_element_type=F32)
                      + jnp.dot((cbt * pr["lmt"][1]).astype(BF16), dy1, preferred_element_type=F32))
            xgd = xg * pr["decay"]
            xb = xgd * bg[:, sl]
            dxg = dxg_in + pr["decay"] * bg[:, sl]
            yo = pr["ec"] * ch[:, sl]
            dac = dyb.astype(F32) * (yv[:, sl] - yo) + dyp * yo - xgb.astype(F32) * dxg_in - xb
            d_0, d_1 = _split_heads(h0, dac)
            r_0, r_1 = _split_heads(h0, dxg * xsp)
            for hh, (d_h, r_h) in enumerate(((d_0, r_0), (d_1, r_1))):
                j = 2 * q + hh
                acc16 = (acc16 + jnp.where(lane16 == j, jnp.sum(d_h, axis=-1, keepdims=True), 0.0)
                         + jnp.where(lane16 == HPG + j, jnp.sum(r_h, axis=-1, keepdims=True), 0.0))
            tcols = pr["etot"] * hg_cols[:, sl] + jnp.sum(xb, axis=0, keepdims=True)
            t0, t1 = _split_heads(h0[0:1], tcols)
            dtot = (dtot + jnp.where(sub8 == 2 * q, jnp.sum(t0, axis=-1, keepdims=True), 0.0)
                    + jnp.where(sub8 == 2 * q + 1, jnp.sum(t1, axis=-1, keepdims=True), 0.0))
            dxss.append(dxg * pr["dt"] + dv[:, sl] * dyp)
            dyes.append(dyp * pr["ec"])
            xgds.append(xgd)
            etots.append(pr["etot"])
        acc_t = acc16.T
        da_lanes = _scan_lanes(acc_t[0:HPG], d != 0)[0] + dtot
        ddt = da_lanes * ck["a_c"] + acc_t[HPG:2 * HPG]
        draw = ddt * _sigmoid(ck["raw_lanes"])
        ddt_ref[...] = draw
        dbias_ref[...] += jnp.sum(draw, axis=-1, keepdims=True)
        dalog_ref[...] += jnp.sum(da_lanes * ck["dt_lanes"], axis=-1, keepdims=True) * ck["a_c"]
        dxs_ref[...] = jnp.concatenate(dxss, axis=1)
        dye = jnp.concatenate(dyes, axis=1).astype(BF16)
        xgd_all = jnp.concatenate(xgds, axis=1).astype(BF16)
        dcbb = dcb.astype(BF16)
        dc_ref[...] = (jnp.dot(dcbb, bm, preferred_element_type=F32)
                       + lax.dot_general(dye, hst.astype(BF16), _DIMS["nt"], preferred_element_type=F32))
        db_ref[...] = (lax.dot_general(dcbb, cm, _DIMS["tn"], preferred_element_type=F32)
                       + lax.dot_general(xgd_all, gst.astype(BF16), _DIMS["nt"], preferred_element_type=F32))
        g_ref[...] = jnp.dot(ct_ref[...], dye, preferred_element_type=F32) + gst * jnp.concatenate(etots, axis=1)

    small = pl.BlockSpec((None, GPS, HPG, 1), lambda d, g, c: (d, g, 0, 0))
    sshape = jax.ShapeDtypeStruct((2, SSM_GROUPS, HPG, 1), F32)
    grid = (2, GSTEPS, nc)
    anywhere, exchanged, sems = _rider_specs(riders)
    outs = pl.pallas_call(
        _riding(body, riders, 13, 6, grid), name=name, grid=grid, in_specs=in_specs + anywhere,
        out_specs=[pl.BlockSpec((None, T, GPS * GW), lambda d, g, c: (d, chunk(d, c), g)),
                   pl.BlockSpec((None, T, GPS * SSM_STATE), lambda d, g, c: (d, chunk(d, c), g)),
                   pl.BlockSpec((None, T, GPS * SSM_STATE), lambda d, g, c: (d, chunk(d, c), g)),
                   pl.BlockSpec((None, GPS, HPG, T), lambda d, g, c: (d, g, 0, chunk(d, c))), small, small] + anywhere,
        out_shape=[jax.ShapeDtypeStruct((2, s, SSM_INNER), F32),
                   jax.ShapeDtypeStruct((2, s, SSM_GROUPS * SSM_STATE), F32),
                   jax.ShapeDtypeStruct((2, s, SSM_GROUPS * SSM_STATE), F32),
                   jax.ShapeDtypeStruct((2, SSM_GROUPS, HPG, s), F32), sshape, sshape] + exchanged,
        scratch_shapes=[pltpu.VMEM((GPS, SSM_STATE, GW), F32)] + sems,
        compiler_params=_params(*(("arbitrary",) * 3 if riders else ("parallel", "parallel", "arbitrary"))),
    )(xbc, xbc, xbc, bt, ct, dtr, dtt, pr_rows, pr_cols, dvec, hs, y2, dy, *[x for x, _ in riders])
    return (*outs[:6], list(outs[6:]))


def _gate_norm_fwd(y2, xbc, proj, dvec, nw, *, name):
    s = xbc.shape[0]
    tr = 256

    def body(y_ref, xs_ref, z_ref, dv_ref, w_ref, u_ref):
        yt = y_ref[0] + y_ref[1] + dv_ref[...] * xs_ref[...].astype(F32)
        yg = yt * _silu_and_grad(z_ref[...].astype(F32))[0]
        u_ref[...] = (yg * lax.rsqrt(jnp.mean(yg * yg, axis=-1, keepdims=True) + RMS_EPS) * w_ref[...]).astype(BF16)

    row = pl.BlockSpec((tr, SSM_INNER), lambda i: (i, 0))
    vec = pl.BlockSpec((1, SSM_INNER), lambda i: (0, 0))
    return pl.pallas_call(
        body, name=name, grid=(s // tr,),
        in_specs=[pl.BlockSpec((2, tr, SSM_INNER), lambda i: (0, i, 0)), row, row, vec, vec], out_specs=row,
        out_shape=jax.ShapeDtypeStruct((s, SSM_INNER), BF16), compiler_params=_params("parallel"),
    )(y2, xbc, proj, dvec, nw)


def _gate_norm_bwd(du, y2, xbc, proj, dvec, nw, *, name):
    s = xbc.shape[0]
    tr = 256

    def body(du_ref, y_ref, xs_ref, z_ref, dv_ref, w_ref, dy_ref, dz_ref, dw_ref, dd_ref):
        @pl.when(pl.program_id(0) == 0)
        def _():
            dw_ref[...] = jnp.zeros_like(dw_ref)
            dd_ref[...] = jnp.zeros_like(dd_ref)

        xs = xs_ref[...].astype(F32)
        yt = y_ref[0] + y_ref[1] + dv_ref[...] * xs
        si, dsi = _silu_and_grad(z_ref[...].astype(F32))
        yg = yt * si
        rstd = lax.rsqrt(jnp.mean(yg * yg, axis=-1, keepdims=True) + RMS_EPS)
        yhat = yg * rstd
        du = du_ref[...]
        dyn = du * w_ref[...]
        dyg = rstd * (dyn - yhat * jnp.mean(dyn * yhat, axis=-1, keepdims=True))
        dyt = dyg * si
        dy_ref[...] = dyt
        dz_ref[...] = (dyg * yt * dsi).astype(BF16)
        dw_ref[...] += jnp.sum(du * yhat, axis=0, keepdims=True)
        dd_ref[...] += jnp.sum(dyt * xs, axis=0, keepdims=True)

    row = pl.BlockSpec((tr, SSM_INNER), lambda i: (i, 0))
    vec = pl.BlockSpec((1, SSM_INNER), lambda i: (0, 0))
    vshape = jax.ShapeDtypeStruct((1, SSM_INNER), F32)
    return pl.pallas_call(
        body, name=name, grid=(s // tr,),
        in_specs=[row, pl.BlockSpec((2, tr, SSM_INNER), lambda i: (0, i, 0)), row, row, vec, vec],
        out_specs=[row, row, vec, vec],
        out_shape=[jax.ShapeDtypeStruct((s, SSM_INNER), F32), jax.ShapeDtypeStruct((s, SSM_INNER), BF16), vshape, vshape],
        compiler_params=_params("arbitrary"),
    )(du, y2, xbc, proj, dvec, nw)


def _with_riders(result, riders):
    return result if riders else (result, [])


def _layer_a_fwd(x, mod, w_in, w_out, ln_g, ln_b, tag, riders=()):
    shift, scale, gate = mod
    h = _modulate(x, scale, shift, name=f"{tag}_modulate")
    w0 = jnp.concatenate([w_in[:, :A_GROUP_COLS], w_in[:, 3 * A_GROUP_COLS:]], axis=1)
    projs = [_mm(h, w0, mode="nn", out_dtype=BF16, tm=1024, tn=1024, tk=1024, name=f"{tag}_mm_in0")]
    for grp in (1, 2):
        projs.append(_mm_dilated(h, w_in[:, grp * A_GROUP_COLS:(grp + 1) * A_GROUP_COLS], DILATIONS[grp],
                                 name=f"{tag}_mm_in{grp}"))
    ol, exchanged = [], []
    for grp in range(3):
        res = _attn_fwd(projs[grp], grp, name=f"{tag}_attn_fwd{grp}", riders=riders if grp == 0 else ())
        ol.extend(res[:2])
        exchanged.extend(res[2] if len(res) > 2 else [])
    y, o, lse = _attn_combine(*ol, projs[0], name=f"{tag}_combine")
    out = _mm(y, w_out, mode="nn", out_dtype=F32, tm=512, tn=1024, tk=1024, name=f"{tag}_mm_out")
    xn = _resid_ln_fwd(x, out, gate, ln_g, ln_b, name=f"{tag}_resid_ln")
    return xn, (x, h, projs, y, o, lse, out), exchanged


def _layer_a_bwd(dxn, saved, mod, w_in, w_out, ln_g, tag, riders=()):
    x, h, projs, y, o, lse, out = saved
    shift, scale, gate = mod
    dx_part, dout, dgate, dln_g, dln_b = _resid_ln_bwd(x, out, gate, ln_g, dxn, name=f"{tag}_resid_ln_bwd")
    dw_out = _mm(y, dout, mode="tn", out_dtype=F32, tm=1024, tn=1024, tk=512, name=f"{tag}_mm_dw_out")
    dy = _mm(dout, w_out, mode="nt", out_dtype=F32, tm=512, tn=1024, tk=1024, name=f"{tag}_mm_dy")
    dgp, do0, do1, o1, lse1, do2, o2, lse2 = _attn_combine_bwd(dy, o, lse, projs[0], name=f"{tag}_combine_bwd")
    parts = [_attn_bwd(projs[grp], grp, *dol, name=f"{tag}_attn_bwd{grp}")
             for grp, dol in enumerate(((do0, o, lse), (do1, o1, lse1), (do2, o2, lse2)))]
    dproj = _assemble_dproj(*parts, dgp, name=f"{tag}_assemble_dproj")
    dw_in, exchanged = _with_riders(_mm(h.T, dproj, mode="nn", out_dtype=F32, tm=1024, tn=1024, tk=1024,
                                        name=f"{tag}_mm_dw_in", riders=riders), riders)
    dx, dscale, dshift = _mm_dh(dproj, w_in, dx_part, x, scale, tm=512, tk=2048, name=f"{tag}_mm_dh")
    grads = dict(w_in=dw_in, w_out=dw_out, ln_g=dln_g, ln_b=dln_b, mod=jnp.concatenate([dshift, dscale, dgate], axis=1))
    return dx, grads, exchanged


def _ssd_param_views(dt_raw, dt_bias, a_log):
    s = dt_raw.shape[0]
    r4 = dt_raw.reshape(s, 2, SSM_GROUPS, HPG)
    dtr = r4.transpose(1, 2, 0, 3)
    dtt = r4.transpose(1, 2, 3, 0)
    a = -jnp.exp(a_log)
    pr_rows = jnp.stack([dt_bias.reshape(2, SSM_GROUPS, HPG), a.reshape(2, SSM_GROUPS, HPG)], axis=2)
    return dtr, dtt, pr_rows, pr_rows.transpose(0, 1, 3, 2)


def _layer_b_fwd(x, mod, w_in, w_out, p, ln_g, ln_b, tag, riders=()):
    shift, scale, gate = mod
    s = x.shape[0]
    h = _modulate(x, scale, shift, name=f"{tag}_modulate")
    proj = _mm(h, w_in[:, :SSM_MAIN_COLS], mode="nn", out_dtype=BF16, tm=512, tn=1024, tk=1024, name=f"{tag}_mm_in")
    dt_raw = _mm(h, w_in[:, SSM_MAIN_COLS:SSM_IN_COLS], mode="nn", out_dtype=F32, tm=512, tn=64, tk=1024,
                 name=f"{tag}_mm_dt")
    xpad = jnp.pad(proj[:, SSM_INNER:], ((CONV_HALO, CONV_HALO), (0, 0)))
    xbc = _conv_fwd(xpad, p["conv_w"], p["conv_b"], name=f"{tag}_conv")
    views = (xbc[:, SSM_INNER:SSM_INNER + SSM_GROUPS * SSM_STATE].T, xbc[:, SSM_INNER + SSM_GROUPS * SSM_STATE:].T,
             *_ssd_param_views(dt_raw, p["dt_bias"], p["a_log"]))
    y2, hs, exchanged = _ssd_fwd(xbc, *views, name=f"{tag}_ssd_fwd", riders=riders)
    u = _gate_norm_fwd(y2, xbc, proj, p["dvec"], p["norm_w"], name=f"{tag}_gate_norm")
    out = _mm(u, w_out, mode="nn", out_dtype=F32, tm=512, tn=1024, tk=2048, name=f"{tag}_mm_out")
    xn = _resid_ln_fwd(x, out, gate, ln_g, ln_b, name=f"{tag}_resid_ln")
    return xn, (x, h, proj, xpad, xbc, views, y2, hs, u, out), exchanged


def _layer_b_bwd(dxn, saved, mod, w_in, w_out, p, ln_g, tag, riders=()):
    x, h, proj, xpad, xbc, views, y2, hs, u, out = saved
    shift, scale, gate = mod
    s = x.shape[0]
    dx_part, dout, dgate, dln_g, dln_b = _resid_ln_bwd(x, out, gate, ln_g, dxn, name=f"{tag}_resid_ln_bwd")
    dw_out = _mm(u, dout, mode="tn", out_dtype=F32, tm=1024, tn=1024, tk=512, name=f"{tag}_mm_dw_out")
    du = _mm(dout, w_out, mode="nt", out_dtype=F32, tm=512, tn=1024, tk=1024, name=f"{tag}_mm_du")
    dy, dz, dnorm_w, dd_lanes = _gate_norm_bwd(du, y2, xbc, proj, p["dvec"], p["norm_w"], name=f"{tag}_gate_norm_bwd")
    dxs2, db2, dc2, ddt4, dalog, dbias, exchanged = _ssd_bwd(xbc, *views, p["dvec"], hs, y2, dy, name=f"{tag}_ssd_bwd",
                                                             riders=riders)
    dact = jnp.concatenate([dxs2[0] + dxs2[1], db2[0] + db2[1], dc2[0] + dc2[1]], axis=1)
    dapad = jnp.pad(dact, ((CONV_HALO, CONV_HALO), (0, 0)))
    dxbc, dconv_w, dconv_b = _conv_bwd(xpad, dapad, p["conv_w"], p["conv_b"], name=f"{tag}_conv_bwd")
    ddt_raw = ddt4.transpose(3, 0, 1, 2).reshape(s, 2 * SSM_HEADS).astype(BF16)
    dproj = jnp.concatenate([dz, dxbc, ddt_raw, jnp.zeros((s, SSM_PAD_COLS - SSM_IN_COLS), BF16)], axis=1)
    dw_in = _mm(h.T, dproj, mode="nn", out_dtype=F32, tm=1024, tn=896, tk=1024, name=f"{tag}_mm_dw_in")[:, :SSM_IN_COLS]
    w_pad = jnp.pad(w_in, ((0, 0), (0, SSM_PAD_COLS - SSM_IN_COLS)))
    dx, dscale, dshift = _mm_dh(dproj, w_pad, dx_part, x, scale, tm=512, tk=1792, name=f"{tag}_mm_dh")
    grads = dict(
        w_in=dw_in, w_out=dw_out, ln_g=dln_g, ln_b=dln_b, mod=jnp.concatenate([dshift, dscale, dgate], axis=1),
        conv_w=dconv_w, conv_b=dconv_b, norm_w=dnorm_w, dt_bias=dbias.reshape(2, SSM_HEADS),
        a_log=dalog.reshape(2, SSM_HEADS), d=jnp.sum(dd_lanes.reshape(SSM_HEADS, SSM_HEAD_DIM), axis=1))
    return dx, grads, exchanged


def _full_cols(g):
    return g.transpose(1, 0, 2).reshape(g.shape[1], -1)


def _full_rows(g):
    return g.reshape(-1, g.shape[2])


def _col_blocks(dw):
    r, c = dw.shape
    return dw.reshape(r, N_DEV, c // N_DEV).transpose(1, 0, 2).astype(BF16)


def _row_blocks(dw):
    r, c = dw.shape
    return dw.reshape(N_DEV, r // N_DEV, c).astype(BF16)


def _local_step(x, target, mods, ln_g, ln_b, layer_w, b_params, shards=None):
    layer_w = list(layer_w)
    saved = []
    for i in range(DEPTH):
        riders = ()
        if shards is not None and i + 1 < DEPTH:
            riders = ((shards[i + 1][0], True), (shards[i + 1][1], True))
        small = () if i % 2 == 0 else (b_params[i // 2],)
        fwd = _layer_a_fwd if i % 2 == 0 else _layer_b_fwd
        x, sv, got = fwd(x, mods[i], *layer_w[i], *small, ln_g[i:i + 1], ln_b[i:i + 1], f"l{i}", riders)
        if riders:
            layer_w.append((_full_cols(got[0]), _full_rows(got[1])))
        saved.append(sv)
    dx, loss = _loss_and_grad(x, target, name="loss")
    grads, received = [None] * DEPTH, [None] * DEPTH
    riders = ()
    for i in reversed(range(DEPTH)):
        small = () if i % 2 == 0 else (b_params[i // 2],)
        bwd = _layer_a_bwd if i % 2 == 0 else _layer_b_bwd
        dx, grads[i], got = bwd(dx, saved[i], mods[i], *layer_w[i], *small, ln_g[i:i + 1], f"l{i}", riders)
        if riders:
            received[i + 1] = got
        if shards is not None:
            riders = ((_col_blocks(grads[i]["w_in"]), False), (_row_blocks(grads[i]["w_out"]), False))
    if shards is not None:
        received[0] = [_all_to_all(riders[0][0], name="scatter_w_in0"), _all_to_all(riders[1][0], name="scatter_w_out0")]
    return loss, dx, grads, received


def _mesh_pos():
    return lax.axis_index("x"), lax.axis_index("y"), lax.axis_index("c")


def _all_gather(x, *, name):
    def body(x_ref, out_ref, send_sems, recv_sems, local_sem):
        ax, ay, ac = _mesh_pos()
        me, sibling = (ax, ay, ac), (ax, ay, 1 - ac)
        chips = [(1 - ax, ay), (ax, 1 - ay), (1 - ax, 1 - ay)]

        def slot(px, py, pc):
            return out_ref.at[4 * px + 2 * py + pc]

        def copy(k, block, to, src=None):
            return pltpu.make_async_remote_copy(
                src_ref=slot(*block) if src is None else src, dst_ref=slot(*block),
                send_sem=send_sems.at[k], recv_sem=recv_sems.at[k], device_id=to, device_id_type=MESH)

        mine = pltpu.make_async_copy(x_ref, slot(*me), local_sem)
        mine.start()
        first = [copy(0, me, sibling, src=x_ref)]
        first += [copy(1 + j, me, (*chip, ac), src=x_ref) for j, chip in enumerate(chips)]
        for cp in first:
            cp.start()
        passed = [copy(4 + j, (*chip, ac), sibling) for j, chip in enumerate(chips)]
        for j, chip in enumerate(chips):
            copy(1 + j, (*chip, ac), me).wait_recv()
            passed[j].start()
        copy(0, sibling, me).wait_recv()
        for j, chip in enumerate(chips):
            copy(4 + j, (*chip, 1 - ac), me).wait_recv()
        for cp in first + passed:
            cp.wait_send()
        mine.wait()

    return pl.pallas_call(
        body, name=name, out_shape=jax.ShapeDtypeStruct((N_DEV,) + x.shape, x.dtype),
        in_specs=[pl.BlockSpec(memory_space=pl.ANY)], out_specs=pl.BlockSpec(memory_space=pl.ANY),
        scratch_shapes=[pltpu.SemaphoreType.DMA((7,)), pltpu.SemaphoreType.DMA((7,)), pltpu.SemaphoreType.DMA],
    )(x)


def _all_to_all(x, *, name):
    def body(x_ref, out_ref, send_sems, recv_sems, local_sem):
        ax, ay, ac = _mesh_pos()
        me = 4 * ax + 2 * ay + ac
        mine = pltpu.make_async_copy(x_ref.at[me], out_ref.at[me], local_sem)
        mine.start()
        copies = []
        for k in range(1, N_DEV):
            px = 1 - ax if k & 4 else ax
            py = 1 - ay if k & 2 else ay
            pc = 1 - ac if k & 1 else ac
            copies.append(pltpu.make_async_remote_copy(
                src_ref=x_ref.at[4 * px + 2 * py + pc], dst_ref=out_ref.at[me],
                send_sem=send_sems.at[k - 1], recv_sem=recv_sems.at[k - 1], device_id=(px, py, pc), device_id_type=MESH))
        for cp in copies:
            cp.start()
        for cp in copies:
            cp.wait()
        mine.wait()

    return pl.pallas_call(
        body, name=name, out_shape=jax.ShapeDtypeStruct(x.shape, x.dtype),
        in_specs=[pl.BlockSpec(memory_space=pl.ANY)], out_specs=pl.BlockSpec(memory_space=pl.ANY),
        scratch_shapes=[pltpu.SemaphoreType.DMA((7,)), pltpu.SemaphoreType.DMA((7,)), pltpu.SemaphoreType.DMA],
    )(x)


ADA_LOCAL = 3 * D_MODEL // N_DEV


def _ada_mod(c_all, ada_w, ada_b_local, *, name):
    def body(c_ref, w_ref, b_ref, o_ref):
        cond = _silu_and_grad(c_ref[...])[0]
        o_ref[...] = jnp.dot(cond, w_ref[...], precision=HIGHEST, preferred_element_type=F32) + b_ref[...]

    return pl.pallas_call(
        body, name=name, grid=(DEPTH,),
        in_specs=[pl.BlockSpec((N_DEV, D_MODEL), lambda i: (0, 0)), pl.BlockSpec((None, D_MODEL, ADA_LOCAL), lambda i: (i, 0, 0)),
                  pl.BlockSpec((None, 1, ADA_LOCAL), lambda i: (i, 0, 0))],
        out_specs=pl.BlockSpec((None, N_DEV, ADA_LOCAL), lambda i: (i, 0, 0)),
        out_shape=jax.ShapeDtypeStruct((DEPTH, N_DEV, ADA_LOCAL), F32), compiler_params=_params("parallel"),
    )(c_all, ada_w, ada_b_local)


def _ada_grad(c_all_t, dmod_local, *, name):
    def body(ct_ref, dm_ref, o_ref):
        cond_t = _silu_and_grad(ct_ref[...])[0]
        dm = dm_ref[...]
        acc = cond_t[:, 0:1] * dm[0:1, :]
        for smp in range(1, N_DEV):
            acc = acc + cond_t[:, smp:smp + 1] * dm[smp:smp + 1, :]
        o_ref[...] = acc

    return pl.pallas_call(
        body, name=name, grid=(DEPTH,),
        in_specs=[pl.BlockSpec((D_MODEL, N_DEV), lambda i: (0, 0)), pl.BlockSpec((None, N_DEV, ADA_LOCAL), lambda i: (i, 0, 0))],
        out_specs=pl.BlockSpec((None, D_MODEL, ADA_LOCAL), lambda i: (i, 0, 0)),
        out_shape=jax.ShapeDtypeStruct((DEPTH, D_MODEL, ADA_LOCAL), F32), compiler_params=_params("parallel"),
    )(c_all_t, dmod_local)


def _sum_devices(parts, *, name):
    n = parts.shape[1]

    def body(p_ref, o_ref):
        acc = p_ref[0:1, :]
        for dev in range(1, N_DEV):
            acc = acc + p_ref[dev:dev + 1, :]
        o_ref[...] = acc

    return pl.pallas_call(
        body, name=name, out_shape=jax.ShapeDtypeStruct((1, n), F32),
        in_specs=[pl.BlockSpec(memory_space=pltpu.VMEM)], out_specs=pl.BlockSpec(memory_space=pltpu.VMEM),
        compiler_params=pltpu.CompilerParams(vmem_limit_bytes=VMEM_LIMIT_BYTES),
    )(parts)


ADAMW_VMEM_BYTES = 24 * 1024 * 1024


def _adamw(w, m, v, g, *, name):
    r, c = w.shape
    summed = g.ndim == 3
    tr = r
    arrays = 7 + (N_DEV if summed else 1)
    while tr % 16 == 0 and 2 * arrays * tr * c * 4 > ADAMW_VMEM_BYTES:
        tr //= 2

    def body(w_ref, m_ref, v_ref, g_ref, go_ref, d_ref, mo_ref, vo_ref):
        if summed:
            g = g_ref[0].astype(F32)
            for dev in range(1, N_DEV):
                g = g + g_ref[dev].astype(F32)
        else:
            g = g_ref[...]
        mn = ADAM_B1 * m_ref[...] + (1.0 - ADAM_B1) * g
        vn = ADAM_B2 * v_ref[...] + (1.0 - ADAM_B2) * (g * g)
        m_hat = mn / (1.0 - ADAM_B1 ** ADAM_STEP)
        v_hat = vn / (1.0 - ADAM_B2 ** ADAM_STEP)
        go_ref[...] = g
        d_ref[...] = -ADAM_LR * (m_hat / (jnp.sqrt(v_hat) + ADAM_EPS) + ADAM_WD * w_ref[...])
        mo_ref[...] = mn
        vo_ref[...] = vn

    row = pl.BlockSpec((tr, c), lambda i: (i, 0))
    gspec = pl.BlockSpec((N_DEV, tr, c), lambda i: (0, i, 0)) if summed else row
    shp = jax.ShapeDtypeStruct((r, c), F32)
    return pl.pallas_call(
        body, name=name, grid=(r // tr,), in_specs=[row, row, row, gspec], out_specs=[row] * 4, out_shape=[shp] * 4,
        compiler_params=_params("parallel"),
    )(w, m, v, g)


def _pack(arrays):
    flat = jnp.concatenate([a.reshape(-1) for a in arrays])
    n = flat.shape[0]
    return jnp.pad(flat, (0, -n % 128)).reshape(1, -1)


def _unpack(vec, shapes):
    out, at = [], 0
    for shp in shapes:
        n = math.prod(shp)
        out.append(vec[at:at + n].reshape(shp))
        at += n
    return out


def _unpack_rows(rows, shapes):
    out, at = [], 0
    for shp in shapes:
        n = math.prod(shp)
        out.append(rows[:, at:at + n].reshape((rows.shape[0],) + tuple(shp)))
        at += n
    return out


def _my_shard(full, me, axis):
    width = full.shape[axis] // N_DEV
    return lax.dynamic_slice_in_dim(full, me * width, width, axis)


def _gather_cols(g, lead):
    nd = g.ndim
    perm = tuple(range(1, nd - 1)) + (0, nd - 1)
    t = g.transpose(perm)
    return t.reshape(t.shape[:-2] + (t.shape[-2] * t.shape[-1],))


def kernel(x, c, ada_w, ada_b, ln_g, ln_b, a_w_in, a_w_out, b_w_in, b_conv_w, b_conv_b, b_dt_bias, b_a_log, b_d, b_norm_w, b_w_out, loss_target, m_ada_w, m_ada_b, m_ln_g, m_ln_b, m_a_w_in, m_a_w_out, m_b_w_in, m_b_conv_w, m_b_conv_b, m_b_dt_bias, m_b_a_log, m_b_d, m_b_norm_w, m_b_w_out, v_ada_w, v_ada_b, v_ln_g, v_ln_b, v_a_w_in, v_a_w_out, v_b_w_in, v_b_conv_w, v_b_conv_b, v_b_dt_bias, v_b_a_log, v_b_d, v_b_norm_w, v_b_w_out):
    ax, ay, ac = _mesh_pos()
    me = 4 * ax + 2 * ay + ac
    seq = x.shape[1]

    small_shapes = [(1, D_MODEL), (2, SSM_CONV, ADA_LOCAL), (2, ADA_LOCAL), (2, SSM_INNER // N_DEV)]
    sg = _all_gather(_pack([c, b_conv_w, b_conv_b, b_norm_w]), name="gather_small")[:, 0, :]
    c_all, conv_w_g, conv_b_g, norm_w_g = _unpack_rows(sg, small_shapes)
    c_all = c_all[:, 0, :]
    conv_w = _gather_cols(conv_w_g, 2)
    conv_b = _gather_cols(conv_b_g[:, :, None, :], 2)
    norm_w = _gather_cols(norm_w_g[:, :, None, :], 2)

    shards = [(w_in[i // 2].astype(BF16), w_out[i // 2].astype(BF16))
              for i, (w_in, w_out) in enumerate(((a_w_in, a_w_out), (b_w_in, b_w_out)) * 2)]
    layer0_w = (_full_cols(_all_gather(shards[0][0], name="gather_w_in0")),
                _full_rows(_all_gather(shards[0][1], name="gather_w_out0")))

    ada_b_local = _my_shard(ada_b, me, 1)[:, None, :]
    mod_cols = _ada_mod(c_all, ada_w, ada_b_local, name="ada_mod")
    mod_g = _all_gather(mod_cols.reshape(1, -1), name="gather_mod").reshape(N_DEV, DEPTH, N_DEV, ADA_LOCAL)
    mod = lax.dynamic_index_in_dim(mod_g, me, axis=2, keepdims=False).transpose(1, 0, 2).reshape(DEPTH, 3 * D_MODEL)
    mods = [tuple(mod[i:i + 1, k * D_MODEL:(k + 1) * D_MODEL] for k in range(3)) for i in range(DEPTH)]

    b_params = [dict(conv_w=conv_w[j], conv_b=conv_b[j], norm_w=norm_w[j],
                     dt_bias=b_dt_bias[j], a_log=b_a_log[j], dvec=jnp.repeat(b_d[j], SSM_HEAD_DIM)[None, :])
                for j in range(2)]
    loss_lanes, dx, grads, received = _local_step(x[0], loss_target[0], mods, ln_g, ln_b, [layer0_w], b_params, shards)
    loss = lax.psum(loss_lanes[0, 0], ("x", "y", "c"))
    grad_x = dx[None]

    a_layers, b_layers = (grads[0], grads[2]), (grads[1], grads[3])
    part_shapes = [(DEPTH, 3 * D_MODEL), (DEPTH, D_MODEL), (DEPTH, D_MODEL), (2, SSM_CONV, SSM_CONV_DIM),
                   (2, SSM_CONV_DIM), (2, SSM_INNER), (2, 2, SSM_HEADS), (2, 2, SSM_HEADS), (2, SSM_HEADS)]
    parts = _pack([
        jnp.concatenate([g["mod"] for g in grads]), jnp.concatenate([g["ln_g"] for g in grads]),
        jnp.concatenate([g["ln_b"] for g in grads]), jnp.stack([g["conv_w"] for g in b_layers]),
        jnp.stack([g["conv_b"][0] for g in b_layers]), jnp.stack([g["norm_w"][0] for g in b_layers]),
        jnp.stack([g["dt_bias"] for g in b_layers]), jnp.stack([g["a_log"] for g in b_layers]),
        jnp.stack([g["d"] for g in b_layers])])
    parts_g = _all_gather(parts, name="gather_small_grads")[:, 0, :]
    (g_ada_b, g_ln_g, g_ln_b, g_conv_w, g_conv_b, g_norm_w, g_dt_bias, g_a_log, g_d) = _unpack(
        _sum_devices(parts_g, name="sum_small_grads")[0], part_shapes)
    dmod_all = parts_g[:, :DEPTH * 3 * D_MODEL].reshape(N_DEV, DEPTH, N_DEV, ADA_LOCAL)
    dmod_local = lax.dynamic_index_in_dim(dmod_all, me, axis=2, keepdims=False).transpose(1, 0, 2)
    g_ada_w = _ada_grad(c_all.T, dmod_local, name="ada_grad")

    r_a_w_in = jnp.concatenate([received[0][0], received[2][0]], axis=1)
    r_a_w_out = jnp.concatenate([received[0][1], received[2][1]], axis=1)
    r_b_w_in = jnp.concatenate([received[1][0], received[3][0]], axis=1)
    r_b_w_out = jnp.concatenate([received[1][1], received[3][1]], axis=1)

    def update(w, m, v, g, name):
        two_d = (-1, w.shape[-1])
        outs = _adamw(w.reshape(two_d), m.reshape(two_d), v.reshape(two_d), g, name=name)
        return [o.reshape(w.shape) for o in outs]

    up_ada_w = update(ada_w, m_ada_w, v_ada_w, g_ada_w.reshape(-1, ADA_LOCAL), "adamw_ada_w")
    up_a_w_in = update(a_w_in, m_a_w_in, v_a_w_in, r_a_w_in, "adamw_a_w_in")
    up_a_w_out = update(a_w_out, m_a_w_out, v_a_w_out, r_a_w_out, "adamw_a_w_out")
    up_b_w_in = update(b_w_in, m_b_w_in, v_b_w_in, r_b_w_in, "adamw_b_w_in")
    up_b_w_out = update(b_w_out, m_b_w_out, v_b_w_out, r_b_w_out, "adamw_b_w_out")

    small_w = [ada_b, ln_g, ln_b, b_conv_w, b_conv_b, b_dt_bias, b_a_log, b_d, b_norm_w]
    small_m = [m_ada_b, m_ln_g, m_ln_b, m_b_conv_w, m_b_conv_b, m_b_dt_bias, m_b_a_log, m_b_d, m_b_norm_w]
    small_v = [v_ada_b, v_ln_g, v_ln_b, v_b_conv_w, v_b_conv_b, v_b_dt_bias, v_b_a_log, v_b_d, v_b_norm_w]
    small_g = [g_ada_b, g_ln_g, g_ln_b, _my_shard(g_conv_w, me, 2), _my_shard(g_conv_b, me, 1), g_dt_bias, g_a_log, g_d,
               _my_shard(g_norm_w, me, 1)]
    shapes = [w.shape for w in small_w]
    packed = _adamw(_pack(small_w), _pack(small_m), _pack(small_v), _pack(small_g), name="adamw_small")
    (up_ada_b, up_ln_g, up_ln_b, up_conv_w, up_conv_b, up_dt_bias, up_a_log, up_d, up_norm_w) = zip(
        *[_unpack(p[0], shapes) for p in packed])

    ordered = [up_ada_w, up_ada_b, up_ln_g, up_ln_b, up_a_w_in, up_a_w_out, up_b_w_in, up_conv_w, up_conv_b,
               up_dt_bias, up_a_log, up_d, up_norm_w, up_b_w_out]
    return (loss, grad_x, *[u[0] for u in ordered], *[u[1] for u in ordered], *[u[2] for u in ordered],
            *[u[3] for u in ordered])
```

```python
import functools
import math

import jax
import jax.numpy as jnp
import numpy as np
from jax import lax
from jax.experimental import pallas as pl
from jax.experimental.pallas import tpu as pltpu

F32 = jnp.float32
BF16 = jnp.bfloat16
HIGHEST = lax.Precision.HIGHEST
MESH = pl.DeviceIdType.MESH

D_MODEL = 1024
DEPTH = 4
A_HEADS = 16
A_HEAD_DIM = 64
A_WIDTH = 1024
DILATIONS = (1, 4, 16)
A_RADIUS = 64
A_QBLOCK = 128
A_IN_COLS = 10240
SSM_INNER = 2048
SSM_HEADS = 32
SSM_HEAD_DIM = 64
SSM_STATE = 128
SSM_GROUPS = 4
SSM_CHUNK = 128
SSM_CONV = 5
SSM_CONV_DIM = 3072
SSM_IN_COLS = 5184
SSM_MAIN_COLS = 5120
SSM_PAD_COLS = 5376
CONV_HALO = 16
ALPHA = (2 * DEPTH) ** 0.25
LN_EPS = 1e-5
RMS_EPS = 1e-5
ADAM_LR, ADAM_B1, ADAM_B2, ADAM_EPS, ADAM_WD, ADAM_STEP = 0.001, 0.9, 0.999, 1e-08, 0.01, 10
N_DEV = 8
VMEM_LIMIT_BYTES = 56 * 1024 * 1024
NEG_BIG = -1e30


def _params(*sem):
    return pltpu.CompilerParams(dimension_semantics=sem, vmem_limit_bytes=VMEM_LIMIT_BYTES)


def _sigmoid(x):
    return 1.0 / (1.0 + jnp.exp(-x))


def _silu_and_grad(x):
    sg = _sigmoid(x)
    return x * sg, sg * (1.0 + x * (1.0 - sg))


def _softplus(x):
    e = jnp.exp(-jnp.abs(x))
    u = 1.0 + e
    log1p = jnp.where(u == 1.0, e, jnp.log(u) * (e / jnp.where(u == 1.0, 1.0, u - 1.0)))
    return jnp.maximum(x, 0.0) + log1p


_DIMS = {"nn": (((1,), (0,)), ((), ())), "nt": (((1,), (1,)), ((), ())), "tn": (((0,), (0,)), ((), ()))}


def _exchange_copies(x_ref, out_ref, send_sems, recv_sems, local_sem, gather):
    ax, ay, ac = lax.axis_index("x"), lax.axis_index("y"), lax.axis_index("c")
    me = 4 * ax + 2 * ay + ac
    copies = [pltpu.make_async_copy(x_ref if gather else x_ref.at[me], out_ref.at[me], local_sem)]
    for k in range(1, N_DEV):
        px = 1 - ax if k & 4 else ax
        py = 1 - ay if k & 2 else ay
        pc = 1 - ac if k & 1 else ac
        copies.append(pltpu.make_async_remote_copy(
            src_ref=x_ref if gather else x_ref.at[4 * px + 2 * py + pc], dst_ref=out_ref.at[me],
            send_sem=send_sems.at[k - 1], recv_sem=recv_sems.at[k - 1], device_id=(px, py, pc), device_id_type=MESH))
    return copies


def _rider_specs(riders):
    anywhere = [pl.BlockSpec(memory_space=pl.ANY)] * len(riders)
    shapes = [jax.ShapeDtypeStruct(((N_DEV,) + x.shape) if gather else x.shape, x.dtype) for x, gather in riders]
    sems = [pltpu.SemaphoreType.DMA((N_DEV - 1,)), pltpu.SemaphoreType.DMA((N_DEV - 1,)), pltpu.SemaphoreType.DMA]
    return anywhere, shapes, sems * len(riders)


def _riding(body, riders, n_in, n_out, grid):
    nr = len(riders)
    if not nr:
        return body

    def wrapped(*refs):
        ins, xs = refs[:n_in], refs[n_in:n_in + nr]
        outs, ys = refs[n_in + nr:n_in + nr + n_out], refs[n_in + nr + n_out:n_in + 2 * nr + n_out]
        scratch = refs[n_in + 2 * nr + n_out:]
        own, sems = scratch[:len(scratch) - 3 * nr], scratch[len(scratch) - 3 * nr:]
        ids = [pl.program_id(ax) for ax in range(len(grid))]

        def at(steps):
            cond = ids[0] == steps[0]
            for i, st in zip(ids[1:], steps[1:]):
                cond = jnp.logical_and(cond, i == st)
            return cond

        def exchanges():
            return [_exchange_copies(xs[r], ys[r], *sems[3 * r:3 * r + 3], riders[r][1]) for r in range(nr)]

        @pl.when(at([0] * len(grid)))
        def _():
            for copies in exchanges():
                for cp in copies:
                    cp.start()

        body(*ins, *outs, *own)

        @pl.when(at([g - 1 for g in grid]))
        def _():
            for copies in exchanges():
                for cp in copies:
                    cp.wait()

    return wrapped


def _mm(a, b, *, mode, out_dtype, tm, tn, tk, name, riders=()):
    if mode == "nn":
        (m, k), (_, n) = a.shape, b.shape
    elif mode == "nt":
        (m, k), (n, _) = a.shape, b.shape
    else:
        (k, m), (_, n) = a.shape, b.shape
    tm, tn, tk = min(tm, m), min(tn, n), min(tk, k)
    assert m % tm == 0 and n % tn == 0 and k % tk == 0, (name, a.shape, b.shape)
    nk = k // tk
    dims = _DIMS[mode]
    nr = len(riders)
    grid = (m // tm, n // tn, nk)

    def body(a_ref, b_ref, *rest):
        xs, o_ref, ys, scratch = rest[:nr], rest[nr], rest[nr + 1:2 * nr + 1], rest[2 * nr + 1:]
        sems = scratch[len(scratch) - 3 * nr:]
        ids = [pl.program_id(ax) for ax in range(3)]

        def exchanges():
            return [_exchange_copies(xs[r], ys[r], *sems[3 * r:3 * r + 3], riders[r][1]) for r in range(nr)]

        if nr:
            @pl.when(jnp.logical_and(jnp.logical_and(ids[0] == 0, ids[1] == 0), ids[2] == 0))
            def _():
                for copies in exchanges():
                    for cp in copies:
                        cp.start()

        part = lax.dot_general(a_ref[...], b_ref[...], dims, preferred_element_type=F32)
        if nk == 1:
            o_ref[...] = part.astype(o_ref.dtype)
        else:
            acc_ref = scratch[0]
            kk = ids[2]

            @pl.when(kk == 0)
            def _():
                acc_ref[...] = part

            @pl.when(kk > 0)
            def _():
                acc_ref[...] += part

            @pl.when(kk == nk - 1)
            def _():
                o_ref[...] = acc_ref[...].astype(o_ref.dtype)

        if nr:
            @pl.when(jnp.logical_and(jnp.logical_and(ids[0] == grid[0] - 1, ids[1] == grid[1] - 1), ids[2] == grid[2] - 1))
            def _():
                for copies in exchanges():
                    for cp in copies:
                        cp.wait()

    if mode == "tn":
        a_spec = pl.BlockSpec((tk, tm), lambda i, j, kk: (kk, i))
    else:
        a_spec = pl.BlockSpec((tm, tk), lambda i, j, kk: (i, kk))
    if mode == "nt":
        b_spec = pl.BlockSpec((tn, tk), lambda i, j, kk: (j, kk))
    else:
        b_spec = pl.BlockSpec((tk, tn), lambda i, j, kk: (kk, j))
    anywhere = pl.BlockSpec(memory_space=pl.ANY)
    exchanged = [jax.ShapeDtypeStruct(((N_DEV,) + x.shape) if gather else x.shape, x.dtype) for x, gather in riders]
    sems = [pltpu.SemaphoreType.DMA((N_DEV - 1,)), pltpu.SemaphoreType.DMA((N_DEV - 1,)), pltpu.SemaphoreType.DMA] * nr
    outs = pl.pallas_call(
        body, name=name, grid=grid,
        in_specs=[a_spec, b_spec] + [anywhere] * nr,
        out_specs=[pl.BlockSpec((tm, tn), lambda i, j, kk: (i, j))] + [anywhere] * nr,
        out_shape=[jax.ShapeDtypeStruct((m, n), out_dtype)] + exchanged,
        scratch_shapes=([] if nk == 1 else [pltpu.VMEM((tm, tn), F32)]) + sems,
        compiler_params=_params(*(("arbitrary",) * 3 if nr else ("parallel", "parallel", "arbitrary"))),
    )(a, b, *[x for x, _ in riders])
    return (outs[0], list(outs[1:])) if nr else outs[0]


def _mm_dh(dproj, w, dx_part, x, scale, *, tm, tk, name, riders=()):
    s, k = dproj.shape
    d = w.shape[0]
    tk = min(tk, k)
    assert s % tm == 0 and k % tk == 0
    nk = k // tk

    def body(a_ref, w_ref, dxp_ref, x_ref, sc_ref, dx_ref, dsc_ref, dsh_ref, acc_ref):
        i, kk = pl.program_id(0), pl.program_id(1)
        part = lax.dot_general(a_ref[...], w_ref[...], _DIMS["nt"], preferred_element_type=F32)

        @pl.when(kk == 0)
        def _():
            acc_ref[...] = part

        @pl.when(kk > 0)
        def _():
            acc_ref[...] += part

        @pl.when(jnp.logical_and(i == 0, kk == 0))
        def _():
            dsc_ref[...] = jnp.zeros_like(dsc_ref)
            dsh_ref[...] = jnp.zeros_like(dsh_ref)

        @pl.when(kk == nk - 1)
        def _():
            dh = acc_ref[...]
            dx_ref[...] = dxp_ref[...] + dh * (1.0 + sc_ref[...])
            dsc_ref[...] += jnp.sum(dh * x_ref[...], axis=0, keepdims=True)
            dsh_ref[...] += jnp.sum(dh, axis=0, keepdims=True)

    row = pl.BlockSpec((tm, d), lambda i, kk: (i, 0))
    vec = pl.BlockSpec((1, d), lambda i, kk: (0, 0))
    grid = (s // tm, nk)
    anywhere, exchanged, sems = _rider_specs(riders)
    outs = pl.pallas_call(
        _riding(body, riders, 5, 3, grid), name=name, grid=grid,
        in_specs=[pl.BlockSpec((tm, tk), lambda i, kk: (i, kk)), pl.BlockSpec((d, tk), lambda i, kk: (0, kk)),
                  row, row, vec] + anywhere,
        out_specs=[row, vec, vec] + anywhere,
        out_shape=[jax.ShapeDtypeStruct((s, d), F32), jax.ShapeDtypeStruct((1, d), F32),
                   jax.ShapeDtypeStruct((1, d), F32)] + exchanged,
        scratch_shapes=[pltpu.VMEM((tm, d), F32)] + sems,
        compiler_params=_params("arbitrary", "arbitrary"),
    )(dproj, w, dx_part, x, scale, *[x_r for x_r, _ in riders])
    return (*outs[:3], list(outs[3:])) if riders else tuple(outs)


ROW_TILE = 512


def _modulate(x, scale, shift, *, name):
    s, d = x.shape

    def body(x_ref, sc_ref, sh_ref, h_ref):
        h_ref[...] = (x_ref[...] * (1.0 + sc_ref[...]) + sh_ref[...]).astype(BF16)

    row = pl.BlockSpec((ROW_TILE, d), lambda i: (i, 0))
    vec = pl.BlockSpec((1, d), lambda i: (0, 0))
    return pl.pallas_call(
        body, name=name, grid=(s // ROW_TILE,), in_specs=[row, vec, vec], out_specs=row,
        out_shape=jax.ShapeDtypeStruct((s, d), BF16), compiler_params=_params("parallel"),
    )(x, scale, shift)


def _resid_ln_fwd(x, out, gate, g, b, *, name):
    s, d = x.shape

    def body(x_ref, o_ref, gate_ref, g_ref, b_ref, y_ref):
        r = ALPHA * x_ref[...] + gate_ref[...] * o_ref[...]
        mu = jnp.mean(r, axis=-1, keepdims=True)
        rc = r - mu
        var = jnp.mean(rc * rc, axis=-1, keepdims=True)
        y_ref[...] = rc * lax.rsqrt(var + LN_EPS) * g_ref[...] + b_ref[...]

    row = pl.BlockSpec((ROW_TILE, d), lambda i: (i, 0))
    vec = pl.BlockSpec((1, d), lambda i: (0, 0))
    return pl.pallas_call(
        body, name=name, grid=(s // ROW_TILE,), in_specs=[row, row, vec, vec, vec], out_specs=row,
        out_shape=jax.ShapeDtypeStruct((s, d), F32), compiler_params=_params("parallel"),
    )(x, out, gate, g, b)


def _resid_ln_bwd(x, out, gate, g, dy, *, name):
    s, d = x.shape

    def body(x_ref, o_ref, gate_ref, g_ref, dy_ref, dxp_ref, dout_ref, dgate_ref, dg_ref, db_ref):
        @pl.when(pl.program_id(0) == 0)
        def _():
            dgate_ref[...] = jnp.zeros_like(dgate_ref)
            dg_ref[...] = jnp.zeros_like(dg_ref)
            db_ref[...] = jnp.zeros_like(db_ref)

        o = o_ref[...]
        r = ALPHA * x_ref[...] + gate_ref[...] * o
        mu = jnp.mean(r, axis=-1, keepdims=True)
        rc = r - mu
        var = jnp.mean(rc * rc, axis=-1, keepdims=True)
        rstd = lax.rsqrt(var + LN_EPS)
        xhat = rc * rstd
        dy = dy_ref[...]
        dxh = dy * g_ref[...]
        dr = rstd * (dxh - jnp.mean(dxh, axis=-1, keepdims=True) - xhat * jnp.mean(dxh * xhat, axis=-1, keepdims=True))
        dxp_ref[...] = ALPHA * dr
        dout_ref[...] = (gate_ref[...] * dr).astype(BF16)
        dgate_ref[...] += jnp.sum(dr * o, axis=0, keepdims=True)
        dg_ref[...] += jnp.sum(dy * xhat, axis=0, keepdims=True)
        db_ref[...] += jnp.sum(dy, axis=0, keepdims=True)

    row = pl.BlockSpec((ROW_TILE, d), lambda i: (i, 0))
    vec = pl.BlockSpec((1, d), lambda i: (0, 0))
    vshape = jax.ShapeDtypeStruct((1, d), F32)
    return pl.pallas_call(
        body, name=name, grid=(s // ROW_TILE,), in_specs=[row, row, vec, vec, row],
        out_specs=[row, row, vec, vec, vec],
        out_shape=[jax.ShapeDtypeStruct((s, d), F32), jax.ShapeDtypeStruct((s, d), BF16), vshape, vshape, vshape],
        compiler_params=_params("arbitrary"),
    )(x, out, gate, g, dy)


def _loss_and_grad(y, target, *, name):
    s, d = y.shape

    def body(y_ref, t_ref, dy_ref, loss_ref):
        @pl.when(pl.program_id(0) == 0)
        def _():
            loss_ref[...] = jnp.zeros_like(loss_ref)

        e = y_ref[...] - t_ref[...]
        dy_ref[...] = e * (1.0 / d)
        loss_ref[...] += jnp.sum(jnp.sum(e * e, axis=0, keepdims=True), axis=1, keepdims=True) * (0.5 / d)

    row = pl.BlockSpec((ROW_TILE, d), lambda i: (i, 0))
    return pl.pallas_call(
        body, name=name, grid=(s // ROW_TILE,), in_specs=[row, row],
        out_specs=[row, pl.BlockSpec((1, 128), lambda i: (0, 0))],
        out_shape=[jax.ShapeDtypeStruct((s, d), F32), jax.ShapeDtypeStruct((1, 128), F32)],
        compiler_params=_params("arbitrary"),
    )(y, target)


_SLOPES = np.asarray(2.0 ** (-8.0 * (np.arange(A_HEADS, dtype=np.float32) + 1.0) / A_HEADS), dtype=np.float32)


def _attn_scores(q, kw, slope, dist, valid):
    s = lax.dot_general(q, kw, _DIMS["nt"], preferred_element_type=F32) * (1.0 / math.sqrt(A_HEAD_DIM))
    return jnp.where(valid, s - slope * dist, NEG_BIG)


def _attn_window(blk, length, win, dil):
    start = pl.multiple_of(jnp.clip(blk * A_QBLOCK - A_RADIUS, 0, length - win), A_RADIUS)
    qpos = blk * A_QBLOCK + lax.broadcasted_iota(jnp.int32, (A_QBLOCK, win), 0)
    kpos = start + lax.broadcasted_iota(jnp.int32, (A_QBLOCK, win), 1)
    delta = jnp.abs(kpos - qpos)
    return start, (delta * dil).astype(F32), delta <= A_RADIUS


A_BLOCKS_PER_STEP = 4
A_GROUP_COLS = 3 * A_WIDTH


def _tile_scratch(rows, width):
    return pltpu.VMEM((width // 128, rows, 128), F32)


def _put_tile(scr, val):
    for j in range(scr.shape[0]):
        scr[j] = val[:, j * 128:(j + 1) * 128]


def _get_tile(scr):
    return jnp.concatenate([scr[j] for j in range(scr.shape[0])], axis=1)


def _get_residue(scr, r, dil):
    rows = pl.ds(r, scr.shape[1] // dil, stride=dil)
    return jnp.concatenate([scr.at[j][rows, :] for j in range(scr.shape[0])], axis=1)


def _put_residue(scr, r, dil, val):
    rows = pl.ds(r, scr.shape[1] // dil, stride=dil)
    for j in range(scr.shape[0]):
        scr.at[j][rows, :] = val[:, j * 128:(j + 1) * 128]


def _mm_dilated(a, b, dil, *, name):
    m, k = a.shape
    n = b.shape[1]
    tm = 512

    def body(a_ref, b_ref, o_ref, acc_ref):
        _put_tile(acc_ref, jnp.dot(a_ref[...], b_ref[...], preferred_element_type=F32))
        for r in range(dil):
            o_ref[:, r * n:(r + 1) * n] = _get_residue(acc_ref, r, dil).astype(BF16)

    return pl.pallas_call(
        body, name=name, grid=(m // tm,),
        in_specs=[pl.BlockSpec((tm, k), lambda i: (i, 0)), pl.BlockSpec((k, n), lambda i: (0, 0))],
        out_specs=pl.BlockSpec((tm // dil, dil * n), lambda i: (i, 0)),
        out_shape=jax.ShapeDtypeStruct((m // dil, dil * n), BF16), scratch_shapes=[_tile_scratch(tm, n)],
        compiler_params=_params("parallel"),
    )(a, b)


def _attn_fwd(pv, group, *, name, riders=()):
    dil = DILATIONS[group]
    length = pv.shape[0]
    cb, qoff = pv.shape[1] // (128 * dil), 0
    win = min(2 * A_QBLOCK, length)
    nblk = length // A_QBLOCK
    per = A_BLOCKS_PER_STEP if nblk % A_BLOCKS_PER_STEP == 0 else 1

    def body(slope_ref, q_ref, k_ref, v_ref, o_ref, lse_ref):
        hp = pl.program_id(1)
        for u in range(per):
            rows = slice(u * A_QBLOCK, (u + 1) * A_QBLOCK)
            start, dist, valid = _attn_window(pl.program_id(2) * per + u, length, win, dil)
            kw = k_ref[pl.ds(start, win), :]
            vw = v_ref[pl.ds(start, win), :]
            q = q_ref[rows, :]
            outs, lses = [], []
            for hh in range(2):
                sl = slice(hh * A_HEAD_DIM, (hh + 1) * A_HEAD_DIM)
                sc = _attn_scores(q[:, sl], kw[:, sl], slope_ref[hp * 2 + hh], dist, valid)
                m = jnp.max(sc, axis=-1, keepdims=True)
                p = jnp.exp(sc - m)
                z = jnp.sum(p, axis=-1, keepdims=True)
                o = jnp.dot(p.astype(BF16), vw[:, sl], preferred_element_type=F32) / z
                outs.append(o)
                lses.append(jnp.broadcast_to(m + jnp.log(z), (A_QBLOCK, A_HEAD_DIM)))
            o_ref[rows, :] = jnp.concatenate(outs, axis=1)
            lse_ref[rows, :] = jnp.concatenate(lses, axis=1)

    qspec = pl.BlockSpec((per * A_QBLOCK, 128), lambda r, hp, b: (b, r * cb + qoff + hp))
    kspec = pl.BlockSpec((length, 128), lambda r, hp, b: (0, r * cb + qoff + 8 + hp))
    vspec = pl.BlockSpec((length, 128), lambda r, hp, b: (0, r * cb + qoff + 16 + hp))
    ospec = pl.BlockSpec((per * A_QBLOCK, 128), lambda r, hp, b: (b, r * 8 + hp))
    oshape = jax.ShapeDtypeStruct((length, dil * A_WIDTH), F32)
    grid = (dil, 8, nblk // per)
    anywhere, exchanged, sems = _rider_specs(riders)
    outs = pl.pallas_call(
        _riding(body, riders, 4, 2, grid), name=name, grid=grid,
        in_specs=[pl.BlockSpec(memory_space=pltpu.SMEM), qspec, kspec, vspec] + anywhere, out_specs=[ospec, ospec] + anywhere,
        out_shape=[oshape, oshape] + exchanged, scratch_shapes=sems,
        compiler_params=_params(*(("arbitrary",) * 3 if riders else ("parallel", "parallel", "arbitrary"))),
    )(jnp.asarray(_SLOPES), pv, pv, pv, *[x for x, _ in riders])
    return (outs[0], outs[1], list(outs[2:])) if riders else (outs[0], outs[1])


def _attn_bwd(pv, group, do, o, lse, *, name):
    dil = DILATIONS[group]
    length = pv.shape[0]
    cb, qoff = pv.shape[1] // (128 * dil), 0
    win = min(2 * A_QBLOCK, length)
    nblk = length // A_QBLOCK
    per = A_BLOCKS_PER_STEP if nblk % A_BLOCKS_PER_STEP == 0 else 1
    nstep = nblk // per
    scale = 1.0 / math.sqrt(A_HEAD_DIM)

    def body(slope_ref, q_ref, k_ref, v_ref, do_ref, o_ref, lse_ref, dq_ref, dk_ref, dv_ref, dk_acc, dv_acc):
        hp, step = pl.program_id(1), pl.program_id(2)

        @pl.when(step == 0)
        def _():
            dk_acc[...] = jnp.zeros_like(dk_acc)
            dv_acc[...] = jnp.zeros_like(dv_acc)

        for u in range(per):
            rows = slice(u * A_QBLOCK, (u + 1) * A_QBLOCK)
            start, dist, valid = _attn_window(step * per + u, length, win, dil)
            kw = k_ref[pl.ds(start, win), :]
            vw = v_ref[pl.ds(start, win), :]
            q = q_ref[rows, :]
            do_b = do_ref[rows, :]
            dsum = do_b.astype(F32) * o_ref[rows, :]
            lse_b = lse_ref[rows, :]
            dqs, dks, dvs = [], [], []
            for hh in range(2):
                sl = slice(hh * A_HEAD_DIM, (hh + 1) * A_HEAD_DIM)
                sc = _attn_scores(q[:, sl], kw[:, sl], slope_ref[hp * 2 + hh], dist, valid)
                p = jnp.exp(sc - lse_b[:, hh * A_HEAD_DIM:hh * A_HEAD_DIM + 1])
                dp = lax.dot_general(do_b[:, sl], vw[:, sl], _DIMS["nt"], preferred_element_type=F32)
                ds = (p * (dp - jnp.sum(dsum[:, sl], axis=-1, keepdims=True))).astype(BF16)
                dqs.append(jnp.dot(ds, kw[:, sl], preferred_element_type=F32) * scale)
                dks.append(lax.dot_general(ds, q[:, sl], _DIMS["tn"], preferred_element_type=F32) * scale)
                dvs.append(lax.dot_general(p.astype(BF16), do_b[:, sl], _DIMS["tn"], preferred_element_type=F32))
            dq_ref[rows, :] = jnp.concatenate(dqs, axis=1).astype(BF16)
            dk_acc[pl.ds(start, win), :] += jnp.concatenate(dks, axis=1)
            dv_acc[pl.ds(start, win), :] += jnp.concatenate(dvs, axis=1)

        @pl.when(step == nstep - 1)
        def _():
            dk_ref[...] = dk_acc[...].astype(BF16)
            dv_ref[...] = dv_acc[...].astype(BF16)

    qspec = pl.BlockSpec((per * A_QBLOCK, 128), lambda r, hp, b: (b, r * cb + qoff + hp))
    kspec = pl.BlockSpec((length, 128), lambda r, hp, b: (0, r * cb + qoff + 8 + hp))
    vspec = pl.BlockSpec((length, 128), lambda r, hp, b: (0, r * cb + qoff + 16 + hp))
    bspec = pl.BlockSpec((per * A_QBLOCK, 128), lambda r, hp, b: (b, r * 8 + hp))
    fspec = pl.BlockSpec((length, 128), lambda r, hp, b: (0, r * 8 + hp))
    oshape = jax.ShapeDtypeStruct((length, dil * A_WIDTH), BF16)
    dq, dk, dv = pl.pallas_call(
        body, name=name, grid=(dil, 8, nstep),
        in_specs=[pl.BlockSpec(memory_space=pltpu.SMEM), qspec, kspec, vspec, bspec, bspec, bspec],
        out_specs=[bspec, fspec, fspec], out_shape=[oshape, oshape, oshape],
        scratch_shapes=[pltpu.VMEM((length, 128), F32), pltpu.VMEM((length, 128), F32)],
        compiler_params=_params("parallel", "parallel", "arbitrary"),
    )(jnp.asarray(_SLOPES), pv, pv, pv, do, o, lse)
    return dq, dk, dv


A_GATE_BLOCK = 3
A_ROWS = 256


def _lanes_of(r):
    return slice(r * A_WIDTH, (r + 1) * A_WIDTH)


def _dilated_spec(dil):
    return pl.BlockSpec((A_ROWS // dil, dil * A_WIDTH), lambda i: (i, 0))


def _attn_combine(o0, l0, o1, l1, o2, l2, proj0, *, name):
    s = proj0.shape[0]

    def body(o0_ref, l0_ref, o1_ref, l1_ref, o2_ref, l2_ref, gate_ref, y_ref, o_ref, lse_ref, so1, sl1, so2, sl2):
        for src, dst, dil in ((o1_ref, so1, DILATIONS[1]), (l1_ref, sl1, DILATIONS[1]),
                              (o2_ref, so2, DILATIONS[2]), (l2_ref, sl2, DILATIONS[2])):
            for r in range(dil):
                _put_residue(dst, r, dil, src[:, _lanes_of(r)])
        la, lb, lc = l0_ref[...], _get_tile(sl1), _get_tile(sl2)
        m = jnp.maximum(jnp.maximum(la, lb), lc)
        ea, eb, ec = jnp.exp(la - m), jnp.exp(lb - m), jnp.exp(lc - m)
        den = ea + eb + ec
        o = (ea * o0_ref[...] + eb * _get_tile(so1) + ec * _get_tile(so2)) / den
        o_ref[...] = o
        lse_ref[...] = m + jnp.log(den)
        y_ref[...] = (o * _silu_and_grad(gate_ref[...].astype(F32))[0]).astype(BF16)

    row = pl.BlockSpec((A_ROWS, A_WIDTH), lambda i: (i, 0))
    gspec = pl.BlockSpec((A_ROWS, A_WIDTH), lambda i: (i, A_GATE_BLOCK))
    d1, d2 = _dilated_spec(DILATIONS[1]), _dilated_spec(DILATIONS[2])
    return pl.pallas_call(
        body, name=name, grid=(s // A_ROWS,), in_specs=[row, row, d1, d1, d2, d2, gspec], out_specs=[row, row, row],
        out_shape=[jax.ShapeDtypeStruct((s, A_WIDTH), BF16), jax.ShapeDtypeStruct((s, A_WIDTH), F32),
                   jax.ShapeDtypeStruct((s, A_WIDTH), F32)],
        scratch_shapes=[_tile_scratch(A_ROWS, A_WIDTH)] * 4, compiler_params=_params("parallel"),
    )(o0, l0, o1, l1, o2, l2, proj0)


def _attn_combine_bwd(dy, o, lse, proj0, *, name):
    s = proj0.shape[0]

    def body(dy_ref, o_ref, lse_ref, gate_ref, dg_ref, do_ref, do1, o1, l1, do2, o2, l2, s_do, s_o, s_l):
        si, dsi = _silu_and_grad(gate_ref[...].astype(F32))
        dyv = dy_ref[...]
        ov = o_ref[...]
        do = dyv * si
        _put_tile(s_do, do)
        _put_tile(s_o, ov)
        _put_tile(s_l, lse_ref[...])
        do_ref[...] = do.astype(BF16)
        dg_ref[...] = (dyv * ov * dsi).astype(BF16)
        for (do_d, o_d, l_d), dil in (((do1, o1, l1), DILATIONS[1]), ((do2, o2, l2), DILATIONS[2])):
            for r in range(dil):
                do_d[:, _lanes_of(r)] = _get_residue(s_do, r, dil).astype(BF16)
                o_d[:, _lanes_of(r)] = _get_residue(s_o, r, dil)
                l_d[:, _lanes_of(r)] = _get_residue(s_l, r, dil)

    row = pl.BlockSpec((A_ROWS, A_WIDTH), lambda i: (i, 0))
    gspec = pl.BlockSpec((A_ROWS, A_WIDTH), lambda i: (i, A_GATE_BLOCK))
    shp = jax.ShapeDtypeStruct((s, A_WIDTH), BF16)
    dilated = lambda dil, dtype: jax.ShapeDtypeStruct((s // dil, dil * A_WIDTH), dtype)
    d1, d2 = _dilated_spec(DILATIONS[1]), _dilated_spec(DILATIONS[2])
    return pl.pallas_call(
        body, name=name, grid=(s // A_ROWS,), in_specs=[row, row, row, gspec],
        out_specs=[row, row, d1, d1, d1, d2, d2, d2],
        out_shape=[shp, shp, dilated(DILATIONS[1], BF16), dilated(DILATIONS[1], F32), dilated(DILATIONS[1], F32),
                   dilated(DILATIONS[2], BF16), dilated(DILATIONS[2], F32), dilated(DILATIONS[2], F32)],
        scratch_shapes=[_tile_scratch(A_ROWS, A_WIDTH)] * 3, compiler_params=_params("parallel"),
    )(dy, o, lse, proj0)


def _assemble_dproj(parts0, parts1, parts2, dgate, *, name):
    s = dgate.shape[0]

    def body(*refs):
        ins, out_ref, scr = refs[:10], refs[10], refs[11]
        for p in range(3):
            out_ref[:, _lanes_of(p)] = ins[p][...]
        for g, dil in ((1, DILATIONS[1]), (2, DILATIONS[2])):
            for p in range(3):
                src = ins[3 * g + p]
                for r in range(dil):
                    _put_residue(scr, r, dil, src[:, _lanes_of(r)].astype(F32))
                out_ref[:, _lanes_of(3 * g + p)] = _get_tile(scr).astype(BF16)
        out_ref[:, _lanes_of(9)] = ins[9][...]

    row = pl.BlockSpec((A_ROWS, A_WIDTH), lambda i: (i, 0))
    d1, d2 = _dilated_spec(DILATIONS[1]), _dilated_spec(DILATIONS[2])
    return pl.pallas_call(
        body, name=name, grid=(s // A_ROWS,), in_specs=[row] * 3 + [d1] * 3 + [d2] * 3 + [row],
        out_specs=pl.BlockSpec((A_ROWS, A_IN_COLS), lambda i: (i, 0)),
        out_shape=jax.ShapeDtypeStruct((s, A_IN_COLS), BF16),
        scratch_shapes=[_tile_scratch(A_ROWS, A_WIDTH)], compiler_params=_params("parallel"),
    )(*parts0, *parts1, *parts2, dgate)


CONV_TILE = 256
CONV_SUB = 4


def _conv_taps(xe, n):
    return [xe if j == 2 else pltpu.roll(xe, (2 - j) % n, 0) for j in range(SSM_CONV)]


def _conv_fwd(xpad, w, b, *, name):
    s = xpad.shape[0] - 2 * CONV_HALO
    n = CONV_TILE + 2 * CONV_HALO
    ncol = SSM_CONV_DIM // 128

    sub = min(CONV_SUB, s // CONV_TILE)

    def body(x_ref, w_ref, b_ref, o_ref):
        base = pl.program_id(1) * (sub * CONV_TILE)

        def tile(k, carry):
            r0 = pl.multiple_of(k * CONV_TILE, CONV_TILE)
            t0 = pl.multiple_of(base + r0, CONV_TILE)
            taps = _conv_taps(x_ref[pl.ds(t0, n), :].astype(F32), n)
            pre = b_ref[...]
            for j in range(SSM_CONV):
                pre = pre + w_ref[j:j + 1, :] * taps[j]
            o_ref[pl.ds(r0, CONV_TILE), :] = _silu_and_grad(pre[CONV_HALO:CONV_HALO + CONV_TILE])[0].astype(BF16)
            return carry

        lax.fori_loop(0, sub, tile, 0)

    return pl.pallas_call(
        body, name=name, grid=(ncol, s // (sub * CONV_TILE)),
        in_specs=[pl.BlockSpec((s + 2 * CONV_HALO, 128), lambda j, i: (0, j)),
                  pl.BlockSpec((SSM_CONV, 128), lambda j, i: (0, j)), pl.BlockSpec((1, 128), lambda j, i: (0, j))],
        out_specs=pl.BlockSpec((sub * CONV_TILE, 128), lambda j, i: (i, j)),
        out_shape=jax.ShapeDtypeStruct((s, SSM_CONV_DIM), BF16), compiler_params=_params("parallel", "arbitrary"),
    )(xpad, w, b)


def _conv_bwd(xpad, dapad, w, b, *, name):
    s = xpad.shape[0] - 2 * CONV_HALO
    n = CONV_TILE + 2 * CONV_HALO
    ncol = SSM_CONV_DIM // 128
    mid = slice(CONV_HALO, CONV_HALO + CONV_TILE)
    sub = min(CONV_SUB, s // CONV_TILE)

    def body(x_ref, da_ref, w_ref, b_ref, dx_ref, dw_ref, db_ref):
        @pl.when(pl.program_id(1) == 0)
        def _():
            dw_ref[...] = jnp.zeros_like(dw_ref)
            db_ref[...] = jnp.zeros_like(db_ref)

        base = pl.program_id(1) * (sub * CONV_TILE)

        def tile(k, carry):
            r0 = pl.multiple_of(k * CONV_TILE, CONV_TILE)
            t0 = pl.multiple_of(base + r0, CONV_TILE)
            taps = _conv_taps(x_ref[pl.ds(t0, n), :].astype(F32), n)
            pre = b_ref[...]
            for j in range(SSM_CONV):
                pre = pre + w_ref[j:j + 1, :] * taps[j]
            dpre = da_ref[pl.ds(t0, n), :] * _silu_and_grad(pre)[1]
            dx = jnp.zeros((CONV_TILE, 128), F32)
            for j in range(SSM_CONV):
                back = dpre if j == 2 else pltpu.roll(dpre, (j - 2) % n, 0)
                dx = dx + w_ref[j:j + 1, :] * back[mid]
                dw_ref[j:j + 1, :] += jnp.sum(dpre[mid] * taps[j][mid], axis=0, keepdims=True)
            dx_ref[pl.ds(r0, CONV_TILE), :] = dx.astype(BF16)
            db_ref[...] += jnp.sum(dpre[mid], axis=0, keepdims=True)
            return carry

        lax.fori_loop(0, sub, tile, 0)

    full = pl.BlockSpec((s + 2 * CONV_HALO, 128), lambda j, i: (0, j))
    wspec = pl.BlockSpec((SSM_CONV, 128), lambda j, i: (0, j))
    bspec = pl.BlockSpec((1, 128), lambda j, i: (0, j))
    return pl.pallas_call(
        body, name=name, grid=(ncol, s // (sub * CONV_TILE)), in_specs=[full, full, wspec, bspec],
        out_specs=[pl.BlockSpec((sub * CONV_TILE, 128), lambda j, i: (i, j)), wspec, bspec],
        out_shape=[jax.ShapeDtypeStruct((s, SSM_CONV_DIM), BF16), jax.ShapeDtypeStruct((SSM_CONV, SSM_CONV_DIM), F32),
                   jax.ShapeDtypeStruct((1, SSM_CONV_DIM), F32)],
        compiler_params=_params("parallel", "arbitrary"),
    )(xpad, dapad, w, b)


HPG = SSM_HEADS // SSM_GROUPS
GW = HPG * SSM_HEAD_DIM
T = SSM_CHUNK


def _ssd_specs(nc):
    ceff = lambda d, c: jnp.where(d == 0, c, nc - 1 - c)
    return ceff, [
        pl.BlockSpec((T, GW), lambda d, g, c: (ceff(d, c), g)),
        pl.BlockSpec((T, SSM_STATE), lambda d, g, c: (ceff(d, c), SSM_INNER // 128 + g)),
        pl.BlockSpec((T, SSM_STATE), lambda d, g, c: (ceff(d, c), SSM_INNER // 128 + SSM_GROUPS + g)),
        pl.BlockSpec((None, None, T, HPG), lambda d, g, c: (d, g, ceff(d, c), 0)),
        pl.BlockSpec((None, None, HPG, T), lambda d, g, c: (d, g, 0, ceff(d, c))),
        pl.BlockSpec((None, None, 2, HPG), lambda d, g, c: (d, g, 0, 0)),
        pl.BlockSpec((None, None, HPG, 2), lambda d, g, c: (d, g, 0, 0)),
    ]


def _ssd_chunk_common(d, dt_ref, dtt_ref, prr_ref, prc_ref):
    sgn = 1 - 2 * d
    ri = lax.broadcasted_iota(jnp.int32, (T, T), 0)
    ci = lax.broadcasted_iota(jnp.int32, (T, T), 1)
    mask = ((ri - ci) * sgn) >= 0
    maskf = mask.astype(F32)
    bias_r, a_r = prr_ref[0:1, :], prr_ref[1:2, :]
    bias_c, a_c = prc_ref[:, 0:1], prc_ref[:, 1:2]
    raw = dt_ref[...] + bias_r
    dt_rows = _softplus(raw)
    dt_lanes = _softplus(dtt_ref[...] + bias_c)
    a_rows = dt_rows * a_r
    acum_rows = jnp.dot(maskf, a_rows, precision=HIGHEST, preferred_element_type=F32)
    acum_lanes = lax.dot_general(dt_lanes * a_c, maskf, _DIMS["nt"], precision=HIGHEST, preferred_element_type=F32)
    tot = jnp.sum(a_rows, axis=0, keepdims=True)
    return mask, maskf, raw, dt_rows, a_r, acum_rows, acum_lanes, tot


def _lanes_per_head(pieces):
    return jnp.concatenate([jnp.broadcast_to(p, (p.shape[0], SSM_HEAD_DIM)) for p in pieces], axis=1)


def _ssd_fwd_v1(xbc, dtr, dtt, pr_rows, pr_cols, *, name):
    s = xbc.shape[0]
    nc = s // T
    ceff, in_specs = _ssd_specs(nc)

    def body(x_ref, b_ref, c_ref, dt_ref, dtt_ref, prr_ref, prc_ref, y_ref, hs_ref, st_ref):
        d, c = pl.program_id(0), pl.program_id(2)

        @pl.when(c == 0)
        def _():
            st_ref[...] = jnp.zeros_like(st_ref)

        mask, _, _, dt_rows, _, acum_rows, acum_lanes, tot = _ssd_chunk_common(d, dt_ref, dtt_ref, prr_ref, prc_ref)
        xs = x_ref[...].astype(F32)
        bm, cm = b_ref[...], c_ref[...]
        hprev = st_ref[...]
        hs_ref[...] = hprev
        cb = lax.dot_general(cm, bm, _DIMS["nt"], preferred_element_type=F32)
        ch = jnp.dot(cm, hprev.astype(BF16), preferred_element_type=F32)
        ys, xgds, etots = [], [], []
        for j in range(HPG):
            sl = slice(j * SSM_HEAD_DIM, (j + 1) * SSM_HEAD_DIM)
            ac, al = acum_rows[:, j:j + 1], acum_lanes[j:j + 1, :]
            lm = jnp.where(mask, jnp.exp(jnp.minimum(ac - al, 0.0)), 0.0)
            xg = xs[:, sl] * dt_rows[:, j:j + 1]
            yd = jnp.dot((cb * lm).astype(BF16), xg.astype(BF16), preferred_element_type=F32)
            ys.append(yd + jnp.exp(ac) * ch[:, sl])
            xgds.append(xg * jnp.exp(tot[:, j:j + 1] - ac))
            etots.append(jnp.exp(tot[:, j:j + 1]))
        y_ref[...] = jnp.concatenate(ys, axis=1)
        new = lax.dot_general(bm, jnp.concatenate(xgds, axis=1).astype(BF16), _DIMS["tn"], preferred_element_type=F32)
        st_ref[...] = hprev * _lanes_per_head(etots) + new

    return pl.pallas_call(
        body, name=name, grid=(2, SSM_GROUPS, nc), in_specs=in_specs,
        out_specs=[pl.BlockSpec((None, T, GW), lambda d, g, c: (d, ceff(d, c), g)),
                   pl.BlockSpec((None, None, None, SSM_STATE, GW), lambda d, g, c: (d, ceff(d, c), g, 0, 0))],
        out_shape=[jax.ShapeDtypeStruct((2, s, SSM_INNER), F32),
                   jax.ShapeDtypeStruct((2, nc, SSM_GROUPS, SSM_STATE, GW), F32)],
        scratch_shapes=[pltpu.VMEM((SSM_STATE, GW), F32)],
        compiler_params=_params("parallel", "parallel", "arbitrary"),
    )(xbc, xbc, xbc, dtr, dtt, pr_rows, pr_cols)


def _put_lane(j, col):
    lane = lax.broadcasted_iota(jnp.int32, (col.shape[0], HPG), 1)
    return jnp.where(lane == j, col, 0.0)


def _ssd_bwd_v1(xbc, dtr, dtt, pr_rows, pr_cols, dvec, hs, dy, *, name):
    s = xbc.shape[0]
    nc = s // T
    cb_of = lambda d, c: jnp.where(d == 0, nc - 1 - c, c)
    in_specs = [
        pl.BlockSpec((T, GW), lambda d, g, c: (cb_of(d, c), g)),
        pl.BlockSpec((T, SSM_STATE), lambda d, g, c: (cb_of(d, c), SSM_INNER // 128 + g)),
        pl.BlockSpec((T, SSM_STATE), lambda d, g, c: (cb_of(d, c), SSM_INNER // 128 + SSM_GROUPS + g)),
        pl.BlockSpec((None, None, T, HPG), lambda d, g, c: (d, g, cb_of(d, c), 0)),
        pl.BlockSpec((None, None, HPG, T), lambda d, g, c: (d, g, 0, cb_of(d, c))),
        pl.BlockSpec((None, None, 2, HPG), lambda d, g, c: (d, g, 0, 0)),
        pl.BlockSpec((None, None, HPG, 2), lambda d, g, c: (d, g, 0, 0)),
        pl.BlockSpec((1, GW), lambda d, g, c: (0, g)),
        pl.BlockSpec((None, None, None, SSM_STATE, GW), lambda d, g, c: (d, cb_of(d, c), g, 0, 0)),
        pl.BlockSpec((T, GW), lambda d, g, c: (cb_of(d, c), g)),
    ]

    def body(x_ref, b_ref, c_ref, dt_ref, dtt_ref, prr_ref, prc_ref, dvec_ref, hs_ref, dy_ref,
             dxs_ref, db_ref, dc_ref, ddt_ref, dalog_ref, dbias_ref, g_ref):
        d, c = pl.program_id(0), pl.program_id(2)

        @pl.when(c == 0)
        def _():
            g_ref[...] = jnp.zeros_like(g_ref)
            dalog_ref[...] = jnp.zeros_like(dalog_ref)
            dbias_ref[...] = jnp.zeros_like(dbias_ref)

        mask, maskf, raw, dt_rows, a_r, acum_rows, acum_lanes, tot = _ssd_chunk_common(
            d, dt_ref, dtt_ref, prr_ref, prc_ref)
        xs = x_ref[...].astype(F32)
        bm, cm = b_ref[...], c_ref[...]
        hst = hs_ref[...]
        gst = g_ref[...]
        dyv = dy_ref[...]
        dyb = dyv.astype(BF16)
        dv = dvec_ref[...] * (1 - d).astype(F32)
        cb = lax.dot_general(cm, bm, _DIMS["nt"], preferred_element_type=F32)
        ch = jnp.dot(cm, hst.astype(BF16), preferred_element_type=F32)
        bg = jnp.dot(bm, gst.astype(BF16), preferred_element_type=F32)
        hg = hst * gst
        dcb = jnp.zeros((T, T), F32)
        dacum = jnp.zeros((T, HPG), F32)
        rx = jnp.zeros((T, HPG), F32)
        dtot = jnp.zeros((1, HPG), F32)
        dxss, dyes, xgds, etots = [], [], [], []
        for j in range(HPG):
            sl = slice(j * SSM_HEAD_DIM, (j + 1) * SSM_HEAD_DIM)
            ac, al = acum_rows[:, j:j + 1], acum_lanes[j:j + 1, :]
            lm = jnp.where(mask, jnp.exp(jnp.minimum(ac - al, 0.0)), 0.0)
            m = cb * lm
            dtj = dt_rows[:, j:j + 1]
            xsj = xs[:, sl]
            xg = xsj * dtj
            dyj = dyv[:, sl]
            ec = jnp.exp(ac)
            etot = jnp.exp(tot[:, j:j + 1])
            decay = jnp.exp(tot[:, j:j + 1] - ac)
            dm = lax.dot_general(dyb[:, sl], xg.astype(BF16), _DIMS["nt"], preferred_element_type=F32)
            w = dm * m
            dcb = dcb + dm * lm
            bgj = bg[:, sl]
            dxg = lax.dot_general(m.astype(BF16), dyb[:, sl], _DIMS["tn"], preferred_element_type=F32) + decay * bgj
            xb = decay * jnp.sum(xg * bgj, axis=-1, keepdims=True)
            da_j = (jnp.sum(w, axis=-1, keepdims=True) - jnp.sum(w.T, axis=-1, keepdims=True)
                    + jnp.sum(ec * ch[:, sl] * dyj, axis=-1, keepdims=True) - xb)
            dacum = dacum + _put_lane(j, da_j)
            rx = rx + _put_lane(j, jnp.sum(dxg * xsj, axis=-1, keepdims=True))
            dtot_j = (etot * jnp.sum(jnp.sum(hg[:, sl], axis=0, keepdims=True), axis=1, keepdims=True)
                      + jnp.sum(xb, axis=0, keepdims=True))
            dtot = dtot + _put_lane(j, dtot_j)
            dxss.append(dxg * dtj + dv[:, sl] * dyj)
            dyes.append(dyj * ec)
            xgds.append(xg * decay)
            etots.append(etot)
        da = lax.dot_general(maskf, dacum, _DIMS["tn"], precision=HIGHEST, preferred_element_type=F32) + dtot
        ddt = da * a_r + rx
        draw = ddt * _sigmoid(raw)
        ddt_ref[...] = draw
        dbias_ref[...] += jnp.sum(draw, axis=0, keepdims=True)
        dalog_ref[...] += jnp.sum(da * dt_rows, axis=0, keepdims=True) * a_r
        dxs_ref[...] = jnp.concatenate(dxss, axis=1)
        dye = jnp.concatenate(dyes, axis=1).astype(BF16)
        xgd = jnp.concatenate(xgds, axis=1).astype(BF16)
        dcbb = dcb.astype(BF16)
        dc_ref[...] = (jnp.dot(dcbb, bm, preferred_element_type=F32)
                       + lax.dot_general(dye, hst.astype(BF16), _DIMS["nt"], preferred_element_type=F32))
        db_ref[...] = (lax.dot_general(dcbb, cm, _DIMS["tn"], preferred_element_type=F32)
                       + lax.dot_general(xgd, gst.astype(BF16), _DIMS["nt"], preferred_element_type=F32))
        g_ref[...] = lax.dot_general(cm, dye, _DIMS["tn"], preferred_element_type=F32) + gst * _lanes_per_head(etots)

    small = pl.BlockSpec((None, None, 1, HPG), lambda d, g, c: (d, g, 0, 0))
    sshape = jax.ShapeDtypeStruct((2, SSM_GROUPS, 1, HPG), F32)
    return pl.pallas_call(
        body, name=name, grid=(2, SSM_GROUPS, nc), in_specs=in_specs,
        out_specs=[pl.BlockSpec((None, T, GW), lambda d, g, c: (d, cb_of(d, c), g)),
                   pl.BlockSpec((None, T, SSM_STATE), lambda d, g, c: (d, cb_of(d, c), g)),
                   pl.BlockSpec((None, T, SSM_STATE), lambda d, g, c: (d, cb_of(d, c), g)),
                   pl.BlockSpec((None, None, T, HPG), lambda d, g, c: (d, g, cb_of(d, c), 0)), small, small],
        out_shape=[jax.ShapeDtypeStruct((2, s, SSM_INNER), F32),
                   jax.ShapeDtypeStruct((2, s, SSM_GROUPS * SSM_STATE), F32),
                   jax.ShapeDtypeStruct((2, s, SSM_GROUPS * SSM_STATE), F32),
                   jax.ShapeDtypeStruct((2, SSM_GROUPS, s, HPG), F32), sshape, sshape],
        scratch_shapes=[pltpu.VMEM((SSM_STATE, GW), F32)],
        compiler_params=_params("parallel", "parallel", "arbitrary"),
    )(xbc, xbc, xbc, dtr, dtt, pr_rows, pr_cols, dvec, hs, dy)


PAIRS = HPG // 2


def _scan_lanes(x, forward):
    lane = lax.broadcasted_iota(jnp.int32, x.shape, 1)
    p = x
    k = 1
    while k < T:
        p = p + jnp.where(lane >= k, pltpu.roll(p, k, 1), 0.0)
        k *= 2
    tot = p[:, T - 1:T]
    return jnp.where(forward, p, tot - p + x), tot


def _ssd_chunk(d, dt_ref, dtt_ref, prr_ref, prc_ref):
    sgn = 1 - 2 * d
    ri = lax.broadcasted_iota(jnp.int32, (T, T), 0)
    ci = lax.broadcasted_iota(jnp.int32, (T, T), 1)
    mask = ((ri - ci) * sgn) >= 0
    mask_t = ((ci - ri) * sgn) >= 0
    bias_r = prr_ref[0:1, :]
    bias_c, a_c = prc_ref[:, 0:1], prc_ref[:, 1:2]
    dt_rows = _softplus(dt_ref[...] + bias_r)
    raw_lanes = dtt_ref[...] + bias_c
    dt_lanes = _softplus(raw_lanes)
    acum_lanes, tot = _scan_lanes(dt_lanes * a_c, d == 0)
    return dict(mask=mask, mask_t=mask_t, head0=ci < SSM_HEAD_DIM, dt_rows=dt_rows, raw_lanes=raw_lanes,
                dt_lanes=dt_lanes, a_c=a_c, acum_lanes=acum_lanes, acum_rows=acum_lanes.T, tot=tot)


def _ssd_pair(ck, q):
    h0 = ck["head0"]
    colb = lambda rows, j: jnp.broadcast_to(rows[:, j:j + 1], (T, T))
    rowb = lambda lanes, j: jnp.broadcast_to(lanes[j:j + 1, :], (T, T))
    lms, lmts, acs = [], [], []
    for j in (2 * q, 2 * q + 1):
        ac, al = colb(ck["acum_rows"], j), rowb(ck["acum_lanes"], j)
        lms.append(jnp.where(ck["mask"], jnp.exp(jnp.minimum(ac - al, 0.0)), 0.0))
        lmts.append(jnp.where(ck["mask_t"], jnp.exp(jnp.minimum(al - ac, 0.0)), 0.0))
        acs.append(ac)
    ac_pair = jnp.where(h0, acs[0], acs[1])
    dt_pair = jnp.where(h0, colb(ck["dt_rows"], 2 * q), colb(ck["dt_rows"], 2 * q + 1))
    tot_pair = jnp.where(h0[0:1], ck["tot"][2 * q:2 * q + 1, :], ck["tot"][2 * q + 1:2 * q + 2, :])
    return dict(lm=lms, lmt=lmts, dt=dt_pair, ec=jnp.exp(ac_pair), decay=jnp.exp(tot_pair - ac_pair),
                etot=jnp.exp(tot_pair))


def _split_heads(h0, v):
    zero = jnp.zeros_like(v)
    return jnp.where(h0, v, zero), jnp.where(h0, zero, v)


GPS = 2
GSTEPS = SSM_GROUPS // GPS
B_BLOCK0 = SSM_INNER // (GPS * SSM_STATE)
C_BLOCK0 = (SSM_INNER + SSM_GROUPS * SSM_STATE) // (GPS * SSM_STATE)


def _ssd_in_specs(chunk):
    return [
        pl.BlockSpec((T, GPS * GW), lambda d, g, c: (chunk(d, c), g)),
        pl.BlockSpec((T, GPS * SSM_STATE), lambda d, g, c: (chunk(d, c), B_BLOCK0 + g)),
        pl.BlockSpec((T, GPS * SSM_STATE), lambda d, g, c: (chunk(d, c), C_BLOCK0 + g)),
        pl.BlockSpec((GPS * SSM_STATE, T), lambda d, g, c: (g, chunk(d, c))),
        pl.BlockSpec((GPS * SSM_STATE, T), lambda d, g, c: (g, chunk(d, c))),
        pl.BlockSpec((None, GPS, T, HPG), lambda d, g, c: (d, g, chunk(d, c), 0)),
        pl.BlockSpec((None, GPS, HPG, T), lambda d, g, c: (d, g, 0, chunk(d, c))),
        pl.BlockSpec((None, GPS, 2, HPG), lambda d, g, c: (d, g, 0, 0)),
        pl.BlockSpec((None, GPS, HPG, 2), lambda d, g, c: (d, g, 0, 0)),
    ]


def _group_refs(gi, wide, state_wide, t_wide, lead):
    return ([r.at[:, pl.ds(gi * GW, GW)] for r in wide] + [r.at[:, pl.ds(gi * SSM_STATE, SSM_STATE)] for r in state_wide]
            + [r.at[pl.ds(gi * SSM_STATE, SSM_STATE), :] for r in t_wide] + [r.at[gi] for r in lead])


def _ssd_fwd(xbc, bt, ct, dtr, dtt, pr_rows, pr_cols, *, name, riders=()):
    s = xbc.shape[0]
    nc = s // T
    chunk = lambda d, c: jnp.where(d == 0, c, nc - 1 - c)

    def body(x_ref, b_ref, c_ref, bt_ref, ct_ref, dt_ref, dtt_ref, prr_ref, prc_ref, y_ref, hs_ref, st_ref):
        @pl.when(pl.program_id(2) == 0)
        def _():
            st_ref[...] = jnp.zeros_like(st_ref)

        for gi in range(GPS):
            group(*_group_refs(gi, [x_ref, y_ref], [b_ref, c_ref], [bt_ref, ct_ref],
                               [dt_ref, dtt_ref, prr_ref, prc_ref, hs_ref, st_ref]))

    def group(x_ref, y_ref, b_ref, c_ref, bt_ref, ct_ref, dt_ref, dtt_ref, prr_ref, prc_ref, hs_ref, st_ref):
        d = pl.program_id(0)
        ck = _ssd_chunk(d, dt_ref, dtt_ref, prr_ref, prc_ref)
        xs = x_ref[...].astype(F32)
        cm = c_ref[...]
        hprev = st_ref[...]
        hs_ref[...] = hprev
        cb = lax.dot_general(cm, b_ref[...], _DIMS["nt"], preferred_element_type=F32)
        ch = jnp.dot(cm, hprev.astype(BF16), preferred_element_type=F32)
        ys, xgds, etots = [], [], []
        for q in range(PAIRS):
            sl = slice(q * 128, (q + 1) * 128)
            pr = _ssd_pair(ck, q)
            xg = xs[:, sl] * pr["dt"]
            xg0, xg1 = _split_heads(ck["head0"], xg.astype(BF16))
            yd = (jnp.dot((cb * pr["lm"][0]).astype(BF16), xg0, preferred_element_type=F32)
                  + jnp.dot((cb * pr["lm"][1]).astype(BF16), xg1, preferred_element_type=F32))
            ys.append(yd + pr["ec"] * ch[:, sl])
            xgds.append(xg * pr["decay"])
            etots.append(pr["etot"])
        y_ref[...] = jnp.concatenate(ys, axis=1)
        new = jnp.dot(bt_ref[...], jnp.concatenate(xgds, axis=1).astype(BF16), preferred_element_type=F32)
        st_ref[...] = hprev * jnp.concatenate(etots, axis=1) + new

    grid = (2, GSTEPS, nc)
    anywhere, exchanged, sems = _rider_specs(riders)
    outs = pl.pallas_call(
        _riding(body, riders, 9, 2, grid), name=name, grid=grid, in_specs=_ssd_in_specs(chunk) + anywhere,
        out_specs=[pl.BlockSpec((None, T, GPS * GW), lambda d, g, c: (d, chunk(d, c), g)),
                   pl.BlockSpec((None, None, GPS, SSM_STATE, GW), lambda d, g, c: (d, chunk(d, c), g, 0, 0))] + anywhere,
        out_shape=[jax.ShapeDtypeStruct((2, s, SSM_INNER), F32),
                   jax.ShapeDtypeStruct((2, nc, SSM_GROUPS, SSM_STATE, GW), F32)] + exchanged,
        scratch_shapes=[pltpu.VMEM((GPS, SSM_STATE, GW), F32)] + sems,
        compiler_params=_params(*(("arbitrary",) * 3 if riders else ("parallel", "parallel", "arbitrary"))),
    )(xbc, xbc, xbc, bt, ct, dtr, dtt, pr_rows, pr_cols, *[x for x, _ in riders])
    return outs[0], outs[1], list(outs[2:])


def _ssd_bwd(xbc, bt, ct, dtr, dtt, pr_rows, pr_cols, dvec, hs, y2, dy, *, name, riders=()):
    s = xbc.shape[0]
    nc = s // T
    chunk = lambda d, c: jnp.where(d == 0, nc - 1 - c, c)
    in_specs = _ssd_in_specs(chunk) + [
        pl.BlockSpec((1, GPS * GW), lambda d, g, c: (0, g)),
        pl.BlockSpec((None, None, GPS, SSM_STATE, GW), lambda d, g, c: (d, chunk(d, c), g, 0, 0)),
        pl.BlockSpec((None, T, GPS * GW), lambda d, g, c: (d, chunk(d, c), g)),
        pl.BlockSpec((T, GPS * GW), lambda d, g, c: (chunk(d, c), g)),
    ]

    def body(x_ref, b_ref, c_ref, bt_ref, ct_ref, dt_ref, dtt_ref, prr_ref, prc_ref, dvec_ref, hs_ref, y_ref, dy_ref,
             dxs_ref, db_ref, dc_ref, ddt_ref, dalog_ref, dbias_ref, g_ref):
        @pl.when(pl.program_id(2) == 0)
        def _():
            g_ref[...] = jnp.zeros_like(g_ref)
            dalog_ref[...] = jnp.zeros_like(dalog_ref)
            dbias_ref[...] = jnp.zeros_like(dbias_ref)

        for gi in range(GPS):
            group(*_group_refs(gi, [x_ref, dvec_ref, y_ref, dy_ref, dxs_ref], [b_ref, c_ref, db_ref, dc_ref], [bt_ref, ct_ref],
                               [dt_ref, dtt_ref, prr_ref, prc_ref, hs_ref, ddt_ref, dalog_ref, dbias_ref, g_ref]))

    def group(x_ref, dvec_ref, y_ref, dy_ref, dxs_ref, b_ref, c_ref, db_ref, dc_ref, bt_ref, ct_ref,
              dt_ref, dtt_ref, prr_ref, prc_ref, hs_ref, ddt_ref, dalog_ref, dbias_ref, g_ref):
        d = pl.program_id(0)
        ck = _ssd_chunk(d, dt_ref, dtt_ref, prr_ref, prc_ref)
        h0 = ck["head0"]
        xs = x_ref[...].astype(F32)
        bm, cm = b_ref[...], c_ref[...]
        hst = hs_ref[...]
        gst = g_ref[...]
        dyv = dy_ref[...]
        yv = y_ref[...]
        dv = dvec_ref[...] * (1 - d).astype(F32)
        cb = lax.dot_general(cm, bm, _DIMS["nt"], preferred_element_type=F32)
        cbt = jnp.dot(bm, ct_ref[...], preferred_element_type=F32)
        ch = jnp.dot(cm, hst.astype(BF16), preferred_element_type=F32)
        bg = jnp.dot(bm, gst.astype(BF16), preferred_element_type=F32)
        hg_cols = jnp.sum(hst * gst, axis=0, keepdims=True)
        lane16 = lax.broadcasted_iota(jnp.int32, (T, 2 * HPG), 1)
        sub8 = lax.broadcasted_iota(jnp.int32, (HPG, 1), 0)
        dcb = jnp.zeros((T, T), F32)
        acc16 = jnp.zeros((T, 2 * HPG), F32)
        dtot = jnp.zeros((HPG, 1), F32)
        dxss, dyes, xgds, etots = [], [], [], []
        for q in range(PAIRS):
            sl = slice(q * 128, (q + 1) * 128)
            pr = _ssd_pair(ck, q)
            xsp, dyp = xs[:, sl], dyv[:, sl]
            xg = xsp * pr["dt"]
            xgb = xg.astype(BF16)
            dyb = dyp.astype(BF16)
            dy0, dy1 = _split_heads(h0, dyb)
            dcb = dcb + (lax.dot_general(dy0, xgb, _DIMS["nt"], preferred_element_type=F32) * pr["lm"][0]
                         + lax.dot_general(dy1, xgb, _DIMS["nt"], preferred_element_type=F32) * pr["lm"][1])
            dxg_in = (jnp.dot((cbt * pr["lmt"][0]).astype(BF16), dy0, preferred_element_type=F32)
                      + jnp.dot((cbt * pr["lmt"][1]).astype(BF16), dy1, preferred_element_type=F32))
            xgd = xg * pr["decay"]
            xb = xgd * bg[:, sl]
            dxg = dxg_in + pr["decay"] * bg[:, sl]
            yo = pr["ec"] * ch[:, sl]
            dac = dyb.astype(F32) * (yv[:, sl] - yo) + dyp * yo - xgb.astype(F32) * dxg_in - xb
            d_0, d_1 = _split_heads(h0, dac)
            r_0, r_1 = _split_heads(h0, dxg * xsp)
            for hh, (d_h, r_h) in enumerate(((d_0, r_0), (d_1, r_1))):
                j = 2 * q + hh
                acc16 = (acc16 + jnp.where(lane16 == j, jnp.sum(d_h, axis=-1, keepdims=True), 0.0)
                         + jnp.where(lane16 == HPG + j, jnp.sum(r_h, axis=-1, keepdims=True), 0.0))
            tcols = pr["etot"] * hg_cols[:, sl] + jnp.sum(xb, axis=0, keepdims=True)
            t0, t1 = _split_heads(h0[0:1], tcols)
            dtot = (dtot + jnp.where(sub8 == 2 * q, jnp.sum(t0, axis=-1, keepdims=True), 0.0)
                    + jnp.where(sub8 == 2 * q + 1, jnp.sum(t1, axis=-1, keepdims=True), 0.0))
            dxss.append(dxg * pr["dt"] + dv[:, sl] * dyp)
            dyes.append(dyp * pr["ec"])
            xgds.append(xgd)
            etots.append(pr["etot"])
        acc_t = acc16.T
        da_lanes = _scan_lanes(acc_t[0:HPG], d != 0)[0] + dtot
        ddt = da_lanes * ck["a_c"] + acc_t[HPG:2 * HPG]
        draw = ddt * _sigmoid(ck["raw_lanes"])
        ddt_ref[...] = draw
        dbias_ref[...] += jnp.sum(draw, axis=-1, keepdims=True)
        dalog_ref[...] += jnp.sum(da_lanes * ck["dt_lanes"], axis=-1, keepdims=True) * ck["a_c"]
        dxs_ref[...] = jnp.concatenate(dxss, axis=1)
        dye = jnp.concatenate(dyes, axis=1).astype(BF16)
        xgd_all = jnp.concatenate(xgds, axis=1).astype(BF16)
        dcbb = dcb.astype(BF16)
        dc_ref[...] = (jnp.dot(dcbb, bm, preferred_element_type=F32)
                       + lax.dot_general(dye, hst.astype(BF16), _DIMS["nt"], preferred_element_type=F32))
        db_ref[...] = (lax.dot_general(dcbb, cm, _DIMS["tn"], preferred_element_type=F32)
                       + lax.dot_general(xgd_all, gst.astype(BF16), _DIMS["nt"], preferred_element_type=F32))
        g_ref[...] = jnp.dot(ct_ref[...], dye, preferred_element_type=F32) + gst * jnp.concatenate(etots, axis=1)

    small = pl.BlockSpec((None, GPS, HPG, 1), lambda d, g, c: (d, g, 0, 0))
    sshape = jax.ShapeDtypeStruct((2, SSM_GROUPS, HPG, 1), F32)
    grid = (2, GSTEPS, nc)
    anywhere, exchanged, sems = _rider_specs(riders)
    outs = pl.pallas_call(
        _riding(body, riders, 13, 6, grid), name=name, grid=grid, in_specs=in_specs + anywhere,
        out_specs=[pl.BlockSpec((None, T, GPS * GW), lambda d, g, c: (d, chunk(d, c), g)),
                   pl.BlockSpec((None, T, GPS * SSM_STATE), lambda d, g, c: (d, chunk(d, c), g)),
                   pl.BlockSpec((None, T, GPS * SSM_STATE), lambda d, g, c: (d, chunk(d, c), g)),
                   pl.BlockSpec((None, GPS, HPG, T), lambda d, g, c: (d, g, 0, chunk(d, c))), small, small] + anywhere,
        out_shape=[jax.ShapeDtypeStruct((2, s, SSM_INNER), F32),
                   jax.ShapeDtypeStruct((2, s, SSM_GROUPS * SSM_STATE), F32),
                   jax.ShapeDtypeStruct((2, s, SSM_GROUPS * SSM_STATE), F32),
                   jax.ShapeDtypeStruct((2, SSM_GROUPS, HPG, s), F32), sshape, sshape] + exchanged,
        scratch_shapes=[pltpu.VMEM((GPS, SSM_STATE, GW), F32)] + sems,
        compiler_params=_params(*(("arbitrary",) * 3 if riders else ("parallel", "parallel", "arbitrary"))),
    )(xbc, xbc, xbc, bt, ct, dtr, dtt, pr_rows, pr_cols, dvec, hs, y2, dy, *[x for x, _ in riders])
    return (*outs[:6], list(outs[6:]))


def _gate_norm_fwd(y2, xbc, proj, dvec, nw, *, name):
    s = xbc.shape[0]
    tr = 256

    def body(y_ref, xs_ref, z_ref, dv_ref, w_ref, u_ref):
        yt = y_ref[0] + y_ref[1] + dv_ref[...] * xs_ref[...].astype(F32)
        yg = yt * _silu_and_grad(z_ref[...].astype(F32))[0]
        u_ref[...] = (yg * lax.rsqrt(jnp.mean(yg * yg, axis=-1, keepdims=True) + RMS_EPS) * w_ref[...]).astype(BF16)

    row = pl.BlockSpec((tr, SSM_INNER), lambda i: (i, 0))
    vec = pl.BlockSpec((1, SSM_INNER), lambda i: (0, 0))
    return pl.pallas_call(
        body, name=name, grid=(s // tr,),
        in_specs=[pl.BlockSpec((2, tr, SSM_INNER), lambda i: (0, i, 0)), row, row, vec, vec], out_specs=row,
        out_shape=jax.ShapeDtypeStruct((s, SSM_INNER), BF16), compiler_params=_params("parallel"),
    )(y2, xbc, proj, dvec, nw)


def _gate_norm_bwd(du, y2, xbc, proj, dvec, nw, *, name):
    s = xbc.shape[0]
    tr = 256

    def body(du_ref, y_ref, xs_ref, z_ref, dv_ref, w_ref, dy_ref, dz_ref, dw_ref, dd_ref):
        @pl.when(pl.program_id(0) == 0)
        def _():
            dw_ref[...] = jnp.zeros_like(dw_ref)
            dd_ref[...] = jnp.zeros_like(dd_ref)

        xs = xs_ref[...].astype(F32)
        yt = y_ref[0] + y_ref[1] + dv_ref[...] * xs
        si, dsi = _silu_and_grad(z_ref[...].astype(F32))
        yg = yt * si
        rstd = lax.rsqrt(jnp.mean(yg * yg, axis=-1, keepdims=True) + RMS_EPS)
        yhat = yg * rstd
        du = du_ref[...]
        dyn = du * w_ref[...]
        dyg = rstd * (dyn - yhat * jnp.mean(dyn * yhat, axis=-1, keepdims=True))
        dyt = dyg * si
        dy_ref[...] = dyt
        dz_ref[...] = (dyg * yt * dsi).astype(BF16)
        dw_ref[...] += jnp.sum(du * yhat, axis=0, keepdims=True)
        dd_ref[...] += jnp.sum(dyt * xs, axis=0, keepdims=True)

    row = pl.BlockSpec((tr, SSM_INNER), lambda i: (i, 0))
    vec = pl.BlockSpec((1, SSM_INNER), lambda i: (0, 0))
    vshape = jax.ShapeDtypeStruct((1, SSM_INNER), F32)
    return pl.pallas_call(
        body, name=name, grid=(s // tr,),
        in_specs=[row, pl.BlockSpec((2, tr, SSM_INNER), lambda i: (0, i, 0)), row, row, vec, vec],
        out_specs=[row, row, vec, vec],
        out_shape=[jax.ShapeDtypeStruct((s, SSM_INNER), F32), jax.ShapeDtypeStruct((s, SSM_INNER), BF16), vshape, vshape],
        compiler_params=_params("arbitrary"),
    )(du, y2, xbc, proj, dvec, nw)


def _with_riders(result, riders):
    return result if riders else (result, [])


def _layer_a_fwd(x, mod, w_in, w_out, ln_g, ln_b, tag, riders=()):
    shift, scale, gate = mod
    h = _modulate(x, scale, shift, name=f"{tag}_modulate")
    w0 = jnp.concatenate([w_in[:, :A_GROUP_COLS], w_in[:, 3 * A_GROUP_COLS:]], axis=1)
    projs = [_mm(h, w0, mode="nn", out_dtype=BF16, tm=1024, tn=1024, tk=1024, name=f"{tag}_mm_in0")]
    for grp in (1, 2):
        projs.append(_mm_dilated(h, w_in[:, grp * A_GROUP_COLS:(grp + 1) * A_GROUP_COLS], DILATIONS[grp],
                                 name=f"{tag}_mm_in{grp}"))
    ol, exchanged = [], []
    for grp in range(3):
        res = _attn_fwd(projs[grp], grp, name=f"{tag}_attn_fwd{grp}", riders=riders if grp == 0 else ())
        ol.extend(res[:2])
        exchanged.extend(res[2] if len(res) > 2 else [])
    y, o, lse = _attn_combine(*ol, projs[0], name=f"{tag}_combine")
    out = _mm(y, w_out, mode="nn", out_dtype=F32, tm=512, tn=1024, tk=1024, name=f"{tag}_mm_out")
    xn = _resid_ln_fwd(x, out, gate, ln_g, ln_b, name=f"{tag}_resid_ln")
    return xn, (x, h, projs, y, o, lse, out), exchanged


def _layer_a_bwd(dxn, saved, mod, w_in, w_out, ln_g, tag, riders=(), scatter_own=False):
    x, h, projs, y, o, lse, out = saved
    shift, scale, gate = mod
    dx_part, dout, dgate, dln_g, dln_b = _resid_ln_bwd(x, out, gate, ln_g, dxn, name=f"{tag}_resid_ln_bwd")
    dw_out = _mm(y, dout, mode="tn", out_dtype=F32, tm=1024, tn=1024, tk=512, name=f"{tag}_mm_dw_out")
    dy = _mm(dout, w_out, mode="nt", out_dtype=F32, tm=512, tn=1024, tk=1024, name=f"{tag}_mm_dy")
    dgp, do0, do1, o1, lse1, do2, o2, lse2 = _attn_combine_bwd(dy, o, lse, projs[0], name=f"{tag}_combine_bwd")
    parts = [_attn_bwd(projs[grp], grp, *dol, name=f"{tag}_attn_bwd{grp}")
             for grp, dol in enumerate(((do0, o, lse), (do1, o1, lse1), (do2, o2, lse2)))]
    dproj = _assemble_dproj(*parts, dgp, name=f"{tag}_assemble_dproj")
    dw_in, exchanged = _with_riders(_mm(h.T, dproj, mode="nn", out_dtype=F32, tm=1024, tn=1024, tk=1024,
                                        name=f"{tag}_mm_dw_in", riders=riders), riders)
    own = ((_col_blocks(dw_in), False), (_row_blocks(dw_out), False)) if scatter_own else ()
    res = _mm_dh(dproj, w_in, dx_part, x, scale, tm=512, tk=2048, name=f"{tag}_mm_dh", riders=own)
    dx, dscale, dshift = res[:3]
    grads = dict(w_in=dw_in, w_out=dw_out, ln_g=dln_g, ln_b=dln_b, mod=jnp.concatenate([dshift, dscale, dgate], axis=1))
    return dx, grads, exchanged, (res[3] if scatter_own else None)


def _ssd_param_views(dt_raw, dt_bias, a_log):
    s = dt_raw.shape[0]
    r4 = dt_raw.reshape(s, 2, SSM_GROUPS, HPG)
    dtr = r4.transpose(1, 2, 0, 3)
    dtt = r4.transpose(1, 2, 3, 0)
    a = -jnp.exp(a_log)
    pr_rows = jnp.stack([dt_bias.reshape(2, SSM_GROUPS, HPG), a.reshape(2, SSM_GROUPS, HPG)], axis=2)
    return dtr, dtt, pr_rows, pr_rows.transpose(0, 1, 3, 2)


def _layer_b_fwd(x, mod, w_in, w_out, p, ln_g, ln_b, tag, riders=()):
    shift, scale, gate = mod
    s = x.shape[0]
    h = _modulate(x, scale, shift, name=f"{tag}_modulate")
    proj = _mm(h, w_in[:, :SSM_MAIN_COLS], mode="nn", out_dtype=BF16, tm=512, tn=1024, tk=1024, name=f"{tag}_mm_in")
    dt_raw = _mm(h, w_in[:, SSM_MAIN_COLS:SSM_IN_COLS], mode="nn", out_dtype=F32, tm=512, tn=64, tk=1024,
                 name=f"{tag}_mm_dt")
    xpad = jnp.pad(proj[:, SSM_INNER:], ((CONV_HALO, CONV_HALO), (0, 0)))
    xbc = _conv_fwd(xpad, p["conv_w"], p["conv_b"], name=f"{tag}_conv")
    views = (xbc[:, SSM_INNER:SSM_INNER + SSM_GROUPS * SSM_STATE].T, xbc[:, SSM_INNER + SSM_GROUPS * SSM_STATE:].T,
             *_ssd_param_views(dt_raw, p["dt_bias"], p["a_log"]))
    y2, hs, exchanged = _ssd_fwd(xbc, *views, name=f"{tag}_ssd_fwd", riders=riders)
    u = _gate_norm_fwd(y2, xbc, proj, p["dvec"], p["norm_w"], name=f"{tag}_gate_norm")
    out = _mm(u, w_out, mode="nn", out_dtype=F32, tm=512, tn=1024, tk=2048, name=f"{tag}_mm_out")
    xn = _resid_ln_fwd(x, out, gate, ln_g, ln_b, name=f"{tag}_resid_ln")
    return xn, (x, h, proj, xpad, xbc, views, y2, hs, u, out), exchanged


def _layer_b_bwd(dxn, saved, mod, w_in, w_out, p, ln_g, tag, riders=()):
    x, h, proj, xpad, xbc, views, y2, hs, u, out = saved
    shift, scale, gate = mod
    s = x.shape[0]
    dx_part, dout, dgate, dln_g, dln_b = _resid_ln_bwd(x, out, gate, ln_g, dxn, name=f"{tag}_resid_ln_bwd")
    dw_out = _mm(u, dout, mode="tn", out_dtype=F32, tm=1024, tn=1024, tk=512, name=f"{tag}_mm_dw_out")
    du = _mm(dout, w_out, mode="nt", out_dtype=F32, tm=512, tn=1024, tk=1024, name=f"{tag}_mm_du")
    dy, dz, dnorm_w, dd_lanes = _gate_norm_bwd(du, y2, xbc, proj, p["dvec"], p["norm_w"], name=f"{tag}_gate_norm_bwd")
    dxs2, db2, dc2, ddt4, dalog, dbias, exchanged = _ssd_bwd(xbc, *views, p["dvec"], hs, y2, dy, name=f"{tag}_ssd_bwd",
                                                             riders=riders)
    dact = jnp.concatenate([dxs2[0] + dxs2[1], db2[0] + db2[1], dc2[0] + dc2[1]], axis=1)
    dapad = jnp.pad(dact, ((CONV_HALO, CONV_HALO), (0, 0)))
    dxbc, dconv_w, dconv_b = _conv_bwd(xpad, dapad, p["conv_w"], p["conv_b"], name=f"{tag}_conv_bwd")
    ddt_raw = ddt4.transpose(3, 0, 1, 2).reshape(s, 2 * SSM_HEADS).astype(BF16)
    dproj = jnp.concatenate([dz, dxbc, ddt_raw, jnp.zeros((s, SSM_PAD_COLS - SSM_IN_COLS), BF16)], axis=1)
    dw_in = _mm(h.T, dproj, mode="nn", out_dtype=F32, tm=1024, tn=896, tk=1024, name=f"{tag}_mm_dw_in")[:, :SSM_IN_COLS]
    w_pad = jnp.pad(w_in, ((0, 0), (0, SSM_PAD_COLS - SSM_IN_COLS)))
    dx, dscale, dshift = _mm_dh(dproj, w_pad, dx_part, x, scale, tm=512, tk=1792, name=f"{tag}_mm_dh")
    grads = dict(
        w_in=dw_in, w_out=dw_out, ln_g=dln_g, ln_b=dln_b, mod=jnp.concatenate([dshift, dscale, dgate], axis=1),
        conv_w=dconv_w, conv_b=dconv_b, norm_w=dnorm_w, dt_bias=dbias.reshape(2, SSM_HEADS),
        a_log=dalog.reshape(2, SSM_HEADS), d=jnp.sum(dd_lanes.reshape(SSM_HEADS, SSM_HEAD_DIM), axis=1))
    return dx, grads, exchanged


def _full_cols(g):
    return g.transpose(1, 0, 2).reshape(g.shape[1], -1)


def _full_rows(g):
    return g.reshape(-1, g.shape[2])


def _col_blocks(dw):
    r, c = dw.shape
    return dw.reshape(r, N_DEV, c // N_DEV).transpose(1, 0, 2).astype(BF16)


def _row_blocks(dw):
    r, c = dw.shape
    return dw.reshape(N_DEV, r // N_DEV, c).astype(BF16)


def _local_step(x, target, mods, ln_g, ln_b, layer_w, b_params, shards=None):
    layer_w = list(layer_w)
    saved = []
    for i in range(DEPTH):
        riders = ()
        if shards is not None and i + 1 < DEPTH:
            riders = ((shards[i + 1][0], True), (shards[i + 1][1], True))
        small = () if i % 2 == 0 else (b_params[i // 2],)
        fwd = _layer_a_fwd if i % 2 == 0 else _layer_b_fwd
        x, sv, got = fwd(x, mods[i], *layer_w[i], *small, ln_g[i:i + 1], ln_b[i:i + 1], f"l{i}", riders)
        if riders:
            layer_w.append((_full_cols(got[0]), _full_rows(got[1])))
        saved.append(sv)
    dx, loss = _loss_and_grad(x, target, name="loss")
    grads, received = [None] * DEPTH, [None] * DEPTH
    riders = ()
    for i in reversed(range(DEPTH)):
        small = () if i % 2 == 0 else (b_params[i // 2],)
        bwd = _layer_a_bwd if i % 2 == 0 else _layer_b_bwd
        last = (True,) if shards is not None and i == 0 else ()
        res = bwd(dx, saved[i], mods[i], *layer_w[i], *small, ln_g[i:i + 1], f"l{i}", riders, *last)
        dx, grads[i], got = res[:3]
        if riders:
            received[i + 1] = got
        if last:
            received[0] = res[3]
        if shards is not None:
            riders = ((_col_blocks(grads[i]["w_in"]), False), (_row_blocks(grads[i]["w_out"]), False))
    return loss, dx, grads, received


def _mesh_pos():
    return lax.axis_index("x"), lax.axis_index("y"), lax.axis_index("c")


def _all_gather(x, *, name):
    def body(x_ref, out_ref, send_sems, recv_sems, local_sem):
        ax, ay, ac = _mesh_pos()
        me, sibling = (ax, ay, ac), (ax, ay, 1 - ac)
        chips = [(1 - ax, ay), (ax, 1 - ay), (1 - ax, 1 - ay)]

        def slot(px, py, pc):
            return out_ref.at[4 * px + 2 * py + pc]

        def copy(k, block, to, src=None):
            return pltpu.make_async_remote_copy(
                src_ref=slot(*block) if src is None else src, dst_ref=slot(*block),
                send_sem=send_sems.at[k], recv_sem=recv_sems.at[k], device_id=to, device_id_type=MESH)

        mine = pltpu.make_async_copy(x_ref, slot(*me), local_sem)
        mine.start()
        first = [copy(0, me, sibling, src=x_ref)]
        first += [copy(1 + j, me, (*chip, ac), src=x_ref) for j, chip in enumerate(chips)]
        for cp in first:
            cp.start()
        passed = [copy(4 + j, (*chip, ac), sibling) for j, chip in enumerate(chips)]
        for j, chip in enumerate(chips):
            copy(1 + j, (*chip, ac), me).wait_recv()
            passed[j].start()
        copy(0, sibling, me).wait_recv()
        for j, chip in enumerate(chips):
            copy(4 + j, (*chip, 1 - ac), me).wait_recv()
        for cp in first + passed:
            cp.wait_send()
        mine.wait()

    return pl.pallas_call(
        body, name=name, out_shape=jax.ShapeDtypeStruct((N_DEV,) + x.shape, x.dtype),
        in_specs=[pl.BlockSpec(memory_space=pl.ANY)], out_specs=pl.BlockSpec(memory_space=pl.ANY),
        scratch_shapes=[pltpu.SemaphoreType.DMA((7,)), pltpu.SemaphoreType.DMA((7,)), pltpu.SemaphoreType.DMA],
    )(x)


def _all_to_all(x, *, name):
    def body(x_ref, out_ref, send_sems, recv_sems, local_sem):
        ax, ay, ac = _mesh_pos()
        me = 4 * ax + 2 * ay + ac
        mine = pltpu.make_async_copy(x_ref.at[me], out_ref.at[me], local_sem)
        mine.start()
        copies = []
        for k in range(1, N_DEV):
            px = 1 - ax if k & 4 else ax
            py = 1 - ay if k & 2 else ay
            pc = 1 - ac if k & 1 else ac
            copies.append(pltpu.make_async_remote_copy(
                src_ref=x_ref.at[4 * px + 2 * py + pc], dst_ref=out_ref.at[me],
                send_sem=send_sems.at[k - 1], recv_sem=recv_sems.at[k - 1], device_id=(px, py, pc), device_id_type=MESH))
        for cp in copies:
            cp.start()
        for cp in copies:
            cp.wait()
        mine.wait()

    return pl.pallas_call(
        body, name=name, out_shape=jax.ShapeDtypeStruct(x.shape, x.dtype),
        in_specs=[pl.BlockSpec(memory_space=pl.ANY)], out_specs=pl.BlockSpec(memory_space=pl.ANY),
        scratch_shapes=[pltpu.SemaphoreType.DMA((7,)), pltpu.SemaphoreType.DMA((7,)), pltpu.SemaphoreType.DMA],
    )(x)


ADA_LOCAL = 3 * D_MODEL // N_DEV


def _ada_mod(c_all, ada_w, ada_b_local, *, name):
    def body(c_ref, w_ref, b_ref, o_ref):
        cond = _silu_and_grad(c_ref[...])[0]
        o_ref[...] = jnp.dot(cond, w_ref[...], precision=HIGHEST, preferred_element_type=F32) + b_ref[...]

    return pl.pallas_call(
        body, name=name, grid=(DEPTH,),
        in_specs=[pl.BlockSpec((N_DEV, D_MODEL), lambda i: (0, 0)), pl.BlockSpec((None, D_MODEL, ADA_LOCAL), lambda i: (i, 0, 0)),
                  pl.BlockSpec((None, 1, ADA_LOCAL), lambda i: (i, 0, 0))],
        out_specs=pl.BlockSpec((None, N_DEV, ADA_LOCAL), lambda i: (i, 0, 0)),
        out_shape=jax.ShapeDtypeStruct((DEPTH, N_DEV, ADA_LOCAL), F32), compiler_params=_params("parallel"),
    )(c_all, ada_w, ada_b_local)


def _ada_grad(c_all_t, dmod_local, *, name):
    def body(ct_ref, dm_ref, o_ref):
        cond_t = _silu_and_grad(ct_ref[...])[0]
        dm = dm_ref[...]
        acc = cond_t[:, 0:1] * dm[0:1, :]
        for smp in range(1, N_DEV):
            acc = acc + cond_t[:, smp:smp + 1] * dm[smp:smp + 1, :]
        o_ref[...] = acc

    return pl.pallas_call(
        body, name=name, grid=(DEPTH,),
        in_specs=[pl.BlockSpec((D_MODEL, N_DEV), lambda i: (0, 0)), pl.BlockSpec((None, N_DEV, ADA_LOCAL), lambda i: (i, 0, 0))],
        out_specs=pl.BlockSpec((None, D_MODEL, ADA_LOCAL), lambda i: (i, 0, 0)),
        out_shape=jax.ShapeDtypeStruct((DEPTH, D_MODEL, ADA_LOCAL), F32), compiler_params=_params("parallel"),
    )(c_all_t, dmod_local)


def _sum_devices(parts, *, name):
    n = parts.shape[1]

    def body(p_ref, o_ref):
        acc = p_ref[0:1, :]
        for dev in range(1, N_DEV):
            acc = acc + p_ref[dev:dev + 1, :]
        o_ref[...] = acc

    return pl.pallas_call(
        body, name=name, out_shape=jax.ShapeDtypeStruct((1, n), F32),
        in_specs=[pl.BlockSpec(memory_space=pltpu.VMEM)], out_specs=pl.BlockSpec(memory_space=pltpu.VMEM),
        compiler_params=pltpu.CompilerParams(vmem_limit_bytes=VMEM_LIMIT_BYTES),
    )(parts)


ADAMW_VMEM_BYTES = 24 * 1024 * 1024


def _adamw(w, m, v, g, *, name):
    r, c = w.shape
    summed = g.ndim == 3
    tr = r
    arrays = 7 + (N_DEV if summed else 1)
    while tr % 16 == 0 and 2 * arrays * tr * c * 4 > ADAMW_VMEM_BYTES:
        tr //= 2

    def body(w_ref, m_ref, v_ref, g_ref, go_ref, d_ref, mo_ref, vo_ref):
        if summed:
            g = g_ref[0].astype(F32)
            for dev in range(1, N_DEV):
                g = g + g_ref[dev].astype(F32)
        else:
            g = g_ref[...]
        mn = ADAM_B1 * m_ref[...] + (1.0 - ADAM_B1) * g
        vn = ADAM_B2 * v_ref[...] + (1.0 - ADAM_B2) * (g * g)
        m_hat = mn / (1.0 - ADAM_B1 ** ADAM_STEP)
        v_hat = vn / (1.0 - ADAM_B2 ** ADAM_STEP)
        go_ref[...] = g
        d_ref[...] = -ADAM_LR * (m_hat / (jnp.sqrt(v_hat) + ADAM_EPS) + ADAM_WD * w_ref[...])
        mo_ref[...] = mn
        vo_ref[...] = vn

    row = pl.BlockSpec((tr, c), lambda i: (i, 0))
    gspec = pl.BlockSpec((N_DEV, tr, c), lambda i: (0, i, 0)) if summed else row
    shp = jax.ShapeDtypeStruct((r, c), F32)
    return pl.pallas_call(
        body, name=name, grid=(r // tr,), in_specs=[row, row, row, gspec], out_specs=[row] * 4, out_shape=[shp] * 4,
        compiler_params=_params("parallel"),
    )(w, m, v, g)


def _pack(arrays):
    flat = jnp.concatenate([a.reshape(-1) for a in arrays])
    n = flat.shape[0]
    return jnp.pad(flat, (0, -n % 128)).reshape(1, -1)


def _unpack(vec, shapes):
    out, at = [], 0
    for shp in shapes:
        n = math.prod(shp)
        out.append(vec[at:at + n].reshape(shp))
        at += n
    return out


def _unpack_rows(rows, shapes):
    out, at = [], 0
    for shp in shapes:
        n = math.prod(shp)
        out.append(rows[:, at:at + n].reshape((rows.shape[0],) + tuple(shp)))
        at += n
    return out


def _my_shard(full, me, axis):
    width = full.shape[axis] // N_DEV
    return lax.dynamic_slice_in_dim(full, me * width, width, axis)


def _gather_cols(g, lead):
    nd = g.ndim
    perm = tuple(range(1, nd - 1)) + (0, nd - 1)
    t = g.transpose(perm)
    return t.reshape(t.shape[:-2] + (t.shape[-2] * t.shape[-1],))


def kernel(x, c, ada_w, ada_b, ln_g, ln_b, a_w_in, a_w_out, b_w_in, b_conv_w, b_conv_b, b_dt_bias, b_a_log, b_d, b_norm_w, b_w_out, loss_target, m_ada_w, m_ada_b, m_ln_g, m_ln_b, m_a_w_in, m_a_w_out, m_b_w_in, m_b_conv_w, m_b_conv_b, m_b_dt_bias, m_b_a_log, m_b_d, m_b_norm_w, m_b_w_out, v_ada_w, v_ada_b, v_ln_g, v_ln_b, v_a_w_in, v_a_w_out, v_b_w_in, v_b_conv_w, v_b_conv_b, v_b_dt_bias, v_b_a_log, v_b_d, v_b_norm_w, v_b_w_out):
    ax, ay, ac = _mesh_pos()
    me = 4 * ax + 2 * ay + ac
    seq = x.shape[1]

    small_shapes = [(1, D_MODEL), (2, SSM_CONV, ADA_LOCAL), (2, ADA_LOCAL), (2, SSM_INNER // N_DEV)]
    sg = _all_gather(_pack([c, b_conv_w, b_conv_b, b_norm_w]), name="gather_small")[:, 0, :]
    c_all, conv_w_g, conv_b_g, norm_w_g = _unpack_rows(sg, small_shapes)
    c_all = c_all[:, 0, :]
    conv_w = _gather_cols(conv_w_g, 2)
    conv_b = _gather_cols(conv_b_g[:, :, None, :], 2)
    norm_w = _gather_cols(norm_w_g[:, :, None, :], 2)

    shards = [(w_in[i // 2].astype(BF16), w_out[i // 2].astype(BF16))
              for i, (w_in, w_out) in enumerate(((a_w_in, a_w_out), (b_w_in, b_w_out)) * 2)]
    layer0_w = (_full_cols(_all_gather(shards[0][0], name="gather_w_in0")),
                _full_rows(_all_gather(shards[0][1], name="gather_w_out0")))

    ada_b_local = _my_shard(ada_b, me, 1)[:, None, :]
    mod_cols = _ada_mod(c_all, ada_w, ada_b_local, name="ada_mod")
    mod_g = _all_gather(mod_cols.reshape(1, -1), name="gather_mod").reshape(N_DEV, DEPTH, N_DEV, ADA_LOCAL)
    mod = lax.dynamic_index_in_dim(mod_g, me, axis=2, keepdims=False).transpose(1, 0, 2).reshape(DEPTH, 3 * D_MODEL)
    mods = [tuple(mod[i:i + 1, k * D_MODEL:(k + 1) * D_MODEL] for k in range(3)) for i in range(DEPTH)]

    b_params = [dict(conv_w=conv_w[j], conv_b=conv_b[j], norm_w=norm_w[j],
                     dt_bias=b_dt_bias[j], a_log=b_a_log[j], dvec=jnp.repeat(b_d[j], SSM_HEAD_DIM)[None, :])
                for j in range(2)]
    loss_lanes, dx, grads, received = _local_step(x[0], loss_target[0], mods, ln_g, ln_b, [layer0_w], b_params, shards)
    loss = lax.psum(loss_lanes[0, 0], ("x", "y", "c"))
    grad_x = dx[None]

    a_layers, b_layers = (grads[0], grads[2]), (grads[1], grads[3])
    part_shapes = [(DEPTH, 3 * D_MODEL), (DEPTH, D_MODEL), (DEPTH, D_MODEL), (2, SSM_CONV, SSM_CONV_DIM),
                   (2, SSM_CONV_DIM), (2, SSM_INNER), (2, 2, SSM_HEADS), (2, 2, SSM_HEADS), (2, SSM_HEADS)]
    parts = _pack([
        jnp.concatenate([g["mod"] for g in grads]), jnp.concatenate([g["ln_g"] for g in grads]),
        jnp.concatenate([g["ln_b"] for g in grads]), jnp.stack([g["conv_w"] for g in b_layers]),
        jnp.stack([g["conv_b"][0] for g in b_layers]), jnp.stack([g["norm_w"][0] for g in b_layers]),
        jnp.stack([g["dt_bias"] for g in b_layers]), jnp.stack([g["a_log"] for g in b_layers]),
        jnp.stack([g["d"] for g in b_layers])])
    parts_g = _all_gather(parts, name="gather_small_grads")[:, 0, :]
    (g_ada_b, g_ln_g, g_ln_b, g_conv_w, g_conv_b, g_norm_w, g_dt_bias, g_a_log, g_d) = _unpack(
        _sum_devices(parts_g, name="sum_small_grads")[0], part_shapes)
    dmod_all = parts_g[:, :DEPTH * 3 * D_MODEL].reshape(N_DEV, DEPTH, N_DEV, ADA_LOCAL)
    dmod_local = lax.dynamic_index_in_dim(dmod_all, me, axis=2, keepdims=False).transpose(1, 0, 2)
    g_ada_w = _ada_grad(c_all.T, dmod_local, name="ada_grad")

    r_a_w_in = jnp.concatenate([received[0][0], received[2][0]], axis=1)
    r_a_w_out = jnp.concatenate([received[0][1], received[2][1]], axis=1)
    r_b_w_in = jnp.concatenate([received[1][0], received[3][0]], axis=1)
    r_b_w_out = jnp.concatenate([received[1][1], received[3][1]], axis=1)

    def update(w, m, v, g, name):
        two_d = (-1, w.shape[-1])
        outs = _adamw(w.reshape(two_d), m.reshape(two_d), v.reshape(two_d), g, name=name)
        return [o.reshape(w.shape) for o in outs]

    up_ada_w = update(ada_w, m_ada_w, v_ada_w, g_ada_w.reshape(-1, ADA_LOCAL), "adamw_ada_w")
    up_a_w_in = update(a_w_in, m_a_w_in, v_a_w_in, r_a_w_in, "adamw_a_w_in")
    up_a_w_out = update(a_w_out, m_a_w_out, v_a_w_out, r_a_w_out, "adamw_a_w_out")
    up_b_w_in = update(b_w_in, m_b_w_in, v_b_w_in, r_b_w_in, "adamw_b_w_in")
    up_b_w_out = update(b_w_out, m_b_w_out, v_b_w_out, r_b_w_out, "adamw_b_w_out")

    small_w = [ada_b, ln_g, ln_b, b_conv_w, b_conv_b, b_dt_bias, b_a_log, b_d, b_norm_w]
    small_m = [m_ada_b, m_ln_g, m_ln_b, m_b_conv_w, m_b_conv_b, m_b_dt_bias, m_b_a_log, m_b_d, m_b_norm_w]
    small_v = [v_ada_b, v_ln_g, v_ln_b, v_b_conv_w, v_b_conv_b, v_b_dt_bias, v_b_a_log, v_b_d, v_b_norm_w]
    small_g = [g_ada_b, g_ln_g, g_ln_b, _my_shard(g_conv_w, me, 2), _my_shard(g_conv_b, me, 1), g_dt_bias, g_a_log, g_d,
               _my_shard(g_norm_w, me, 1)]
    shapes = [w.shape for w in small_w]
    packed = _adamw(_pack(small_w), _pack(small_m), _pack(small_v), _pack(small_g), name="adamw_small")
    (up_ada_b, up_ln_g, up_ln_b, up_conv_w, up_conv_b, up_dt_bias, up_a_log, up_d, up_norm_w) = zip(
        *[_unpack(p[0], shapes) for p in packed])

    ordered = [up_ada_w, up_ada_b, up_ln_g, up_ln_b, up_a_w_in, up_a_w_out, up_b_w_in, up_conv_w, up_conv_b,
               up_dt_bias, up_a_log, up_d, up_norm_w, up_b_w_out]
    return (loss, grad_x, *[u[0] for u in ordered], *[u[1] for u in ordered], *[u[2] for u in ordered],
            *[u[3] for u in ordered])
```

```python
import functools
import math

import jax
import jax.numpy as jnp
import numpy as np
from jax import lax
from jax.experimental import pallas as pl
from jax.experimental.pallas import tpu as pltpu

F32 = jnp.float32
BF16 = jnp.bfloat16
HIGHEST = lax.Precision.HIGHEST
MESH = pl.DeviceIdType.MESH

D_MODEL = 1024
DEPTH = 4
A_HEADS = 16
A_HEAD_DIM = 64
A_WIDTH = 1024
DILATIONS = (1, 4, 16)
A_RADIUS = 64
A_QBLOCK = 128
A_IN_COLS = 10240
SSM_INNER = 2048
SSM_HEADS = 32
SSM_HEAD_DIM = 64
SSM_STATE = 128
SSM_GROUPS = 4
SSM_CHUNK = 128
SSM_CONV = 5
SSM_CONV_DIM = 3072
SSM_IN_COLS = 5184
SSM_MAIN_COLS = 5120
SSM_PAD_COLS = 5376
CONV_HALO = 16
ALPHA = (2 * DEPTH) ** 0.25
LN_EPS = 1e-5
RMS_EPS = 1e-5
ADAM_LR, ADAM_B1, ADAM_B2, ADAM_EPS, ADAM_WD, ADAM_STEP = 0.001, 0.9, 0.999, 1e-08, 0.01, 10
N_DEV = 8
VMEM_LIMIT_BYTES = 56 * 1024 * 1024
NEG_BIG = -1e30


def _params(*sem):
    return pltpu.CompilerParams(dimension_semantics=sem, vmem_limit_bytes=VMEM_LIMIT_BYTES)


def _sigmoid(x):
    return 1.0 / (1.0 + jnp.exp(-x))


def _silu_and_grad(x):
    sg = _sigmoid(x)
    return x * sg, sg * (1.0 + x * (1.0 - sg))


def _softplus(x):
    e = jnp.exp(-jnp.abs(x))
    u = 1.0 + e
    log1p = jnp.where(u == 1.0, e, jnp.log(u) * (e / jnp.where(u == 1.0, 1.0, u - 1.0)))
    return jnp.maximum(x, 0.0) + log1p


_DIMS = {"nn": (((1,), (0,)), ((), ())), "nt": (((1,), (1,)), ((), ())), "tn": (((0,), (0,)), ((), ()))}


def _exchange_copies(x_ref, out_ref, send_sems, recv_sems, local_sem, gather):
    ax, ay, ac = lax.axis_index("x"), lax.axis_index("y"), lax.axis_index("c")
    me = 4 * ax + 2 * ay + ac
    copies = [pltpu.make_async_copy(x_ref if gather else x_ref.at[me], out_ref.at[me], local_sem)]
    for k in range(1, N_DEV):
        px = 1 - ax if k & 4 else ax
        py = 1 - ay if k & 2 else ay
        pc = 1 - ac if k & 1 else ac
        copies.append(pltpu.make_async_remote_copy(
            src_ref=x_ref if gather else x_ref.at[4 * px + 2 * py + pc], dst_ref=out_ref.at[me],
            send_sem=send_sems.at[k - 1], recv_sem=recv_sems.at[k - 1], device_id=(px, py, pc), device_id_type=MESH))
    return copies


def _rider_specs(riders):
    anywhere = [pl.BlockSpec(memory_space=pl.ANY)] * len(riders)
    shapes = [jax.ShapeDtypeStruct(((N_DEV,) + x.shape) if gather else x.shape, x.dtype) for x, gather in riders]
    sems = [pltpu.SemaphoreType.DMA((N_DEV - 1,)), pltpu.SemaphoreType.DMA((N_DEV - 1,)), pltpu.SemaphoreType.DMA]
    return anywhere, shapes, sems * len(riders)


def _riding(body, riders, n_in, n_out, grid):
    nr = len(riders)
    if not nr:
        return body

    def wrapped(*refs):
        ins, xs = refs[:n_in], refs[n_in:n_in + nr]
        outs, ys = refs[n_in + nr:n_in + nr + n_out], refs[n_in + nr + n_out:n_in + 2 * nr + n_out]
        scratch = refs[n_in + 2 * nr + n_out:]
        own, sems = scratch[:len(scratch) - 3 * nr], scratch[len(scratch) - 3 * nr:]
        ids = [pl.program_id(ax) for ax in range(len(grid))]

        def at(steps):
            cond = ids[0] == steps[0]
            for i, st in zip(ids[1:], steps[1:]):
                cond = jnp.logical_and(cond, i == st)
            return cond

        def exchanges():
            return [_exchange_copies(xs[r], ys[r], *sems[3 * r:3 * r + 3], riders[r][1]) for r in range(nr)]

        @pl.when(at([0] * len(grid)))
        def _():
            for copies in exchanges():
                for cp in copies:
                    cp.start()

        body(*ins, *outs, *own)

        @pl.when(at([g - 1 for g in grid]))
        def _():
            for copies in exchanges():
                for cp in copies:
                    cp.wait()

    return wrapped


def _mm(a, b, *, mode, out_dtype, tm, tn, tk, name, riders=()):
    if mode == "nn":
        (m, k), (_, n) = a.shape, b.shape
    elif mode == "nt":
        (m, k), (n, _) = a.shape, b.shape
    else:
        (k, m), (_, n) = a.shape, b.shape
    tm, tn, tk = min(tm, m), min(tn, n), min(tk, k)
    assert m % tm == 0 and n % tn == 0 and k % tk == 0, (name, a.shape, b.shape)
    nk = k // tk
    dims = _DIMS[mode]
    nr = len(riders)
    grid = (m // tm, n // tn, nk)

    def body(a_ref, b_ref, *rest):
        xs, o_ref, ys, scratch = rest[:nr], rest[nr], rest[nr + 1:2 * nr + 1], rest[2 * nr + 1:]
        sems = scratch[len(scratch) - 3 * nr:]
        ids = [pl.program_id(ax) for ax in range(3)]

        def exchanges():
            return [_exchange_copies(xs[r], ys[r], *sems[3 * r:3 * r + 3], riders[r][1]) for r in range(nr)]

        if nr:
            @pl.when(jnp.logical_and(jnp.logical_and(ids[0] == 0, ids[1] == 0), ids[2] == 0))
            def _():
                for copies in exchanges():
                    for cp in copies:
                        cp.start()

        part = lax.dot_general(a_ref[...], b_ref[...], dims, preferred_element_type=F32)
        if nk == 1:
            o_ref[...] = part.astype(o_ref.dtype)
        else:
            acc_ref = scratch[0]
            kk = ids[2]

            @pl.when(kk == 0)
            def _():
                acc_ref[...] = part

            @pl.when(kk > 0)
            def _():
                acc_ref[...] += part

            @pl.when(kk == nk - 1)
            def _():
                o_ref[...] = acc_ref[...].astype(o_ref.dtype)

        if nr:
            @pl.when(jnp.logical_and(jnp.logical_and(ids[0] == grid[0] - 1, ids[1] == grid[1] - 1), ids[2] == grid[2] - 1))
            def _():
                for copies in exchanges():
                    for cp in copies:
                        cp.wait()

    if mode == "tn":
        a_spec = pl.BlockSpec((tk, tm), lambda i, j, kk: (kk, i))
    else:
        a_spec = pl.BlockSpec((tm, tk), lambda i, j, kk: (i, kk))
    if mode == "nt":
        b_spec = pl.BlockSpec((tn, tk), lambda i, j, kk: (j, kk))
    else:
        b_spec = pl.BlockSpec((tk, tn), lambda i, j, kk: (kk, j))
    anywhere = pl.BlockSpec(memory_space=pl.ANY)
    exchanged = [jax.ShapeDtypeStruct(((N_DEV,) + x.shape) if gather else x.shape, x.dtype) for x, gather in riders]
    sems = [pltpu.SemaphoreType.DMA((N_DEV - 1,)), pltpu.SemaphoreType.DMA((N_DEV - 1,)), pltpu.SemaphoreType.DMA] * nr
    outs = pl.pallas_call(
        body, name=name, grid=grid,
        in_specs=[a_spec, b_spec] + [anywhere] * nr,
        out_specs=[pl.BlockSpec((tm, tn), lambda i, j, kk: (i, j))] + [anywhere] * nr,
        out_shape=[jax.ShapeDtypeStruct((m, n), out_dtype)] + exchanged,
        scratch_shapes=([] if nk == 1 else [pltpu.VMEM((tm, tn), F32)]) + sems,
        compiler_params=_params(*(("arbitrary",) * 3 if nr else ("parallel", "parallel", "arbitrary"))),
    )(a, b, *[x for x, _ in riders])
    return (outs[0], list(outs[1:])) if nr else outs[0]


def _mm_dh(dproj, w, dx_part, x, scale, *, tm, tk, name, riders=()):
    s, k = dproj.shape
    d = w.shape[0]
    tk = min(tk, k)
    assert s % tm == 0 and k % tk == 0
    nk = k // tk

    def body(a_ref, w_ref, dxp_ref, x_ref, sc_ref, dx_ref, dsc_ref, dsh_ref, acc_ref):
        i, kk = pl.program_id(0), pl.program_id(1)
        part = lax.dot_general(a_ref[...], w_ref[...], _DIMS["nt"], preferred_element_type=F32)

        @pl.when(kk == 0)
        def _():
            acc_ref[...] = part

        @pl.when(kk > 0)
        def _():
            acc_ref[...] += part

        @pl.when(jnp.logical_and(i == 0, kk == 0))
        def _():
            dsc_ref[...] = jnp.zeros_like(dsc_ref)
            dsh_ref[...] = jnp.zeros_like(dsh_ref)

        @pl.when(kk == nk - 1)
        def _():
            dh = acc_ref[...]
            dx_ref[...] = dxp_ref[...] + dh * (1.0 + sc_ref[...])
            dsc_ref[...] += jnp.sum(dh * x_ref[...], axis=0, keepdims=True)
            dsh_ref[...] += jnp.sum(dh, axis=0, keepdims=True)

    row = pl.BlockSpec((tm, d), lambda i, kk: (i, 0))
    vec = pl.BlockSpec((1, d), lambda i, kk: (0, 0))
    grid = (s // tm, nk)
    anywhere, exchanged, sems = _rider_specs(riders)
    outs = pl.pallas_call(
        _riding(body, riders, 5, 3, grid), name=name, grid=grid,
        in_specs=[pl.BlockSpec((tm, tk), lambda i, kk: (i, kk)), pl.BlockSpec((d, tk), lambda i, kk: (0, kk)),
                  row, row, vec] + anywhere,
        out_specs=[row, vec, vec] + anywhere,
        out_shape=[jax.ShapeDtypeStruct((s, d), F32), jax.ShapeDtypeStruct((1, d), F32),
                   jax.ShapeDtypeStruct((1, d), F32)] + exchanged,
        scratch_shapes=[pltpu.VMEM((tm, d), F32)] + sems,
        compiler_params=_params("arbitrary", "arbitrary"),
    )(dproj, w, dx_part, x, scale, *[x_r for x_r, _ in riders])
    return (*outs[:3], list(outs[3:])) if riders else tuple(outs)


ROW_TILE = 512


def _modulate(x, scale, shift, *, name):
    s, d = x.shape

    def body(x_ref, sc_ref, sh_ref, h_ref):
        h_ref[...] = (x_ref[...] * (1.0 + sc_ref[...]) + sh_ref[...]).astype(BF16)

    row = pl.BlockSpec((ROW_TILE, d), lambda i: (i, 0))
    vec = pl.BlockSpec((1, d), lambda i: (0, 0))
    return pl.pallas_call(
        body, name=name, grid=(s // ROW_TILE,), in_specs=[row, vec, vec], out_specs=row,
        out_shape=jax.ShapeDtypeStruct((s, d), BF16), compiler_params=_params("parallel"),
    )(x, scale, shift)


def _resid_ln_fwd(x, out, gate, g, b, *, name):
    s, d = x.shape

    def body(x_ref, o_ref, gate_ref, g_ref, b_ref, y_ref):
        r = ALPHA * x_ref[...] + gate_ref[...] * o_ref[...]
        mu = jnp.mean(r, axis=-1, keepdims=True)
        rc = r - mu
        var = jnp.mean(rc * rc, axis=-1, keepdims=True)
        y_ref[...] = rc * lax.rsqrt(var + LN_EPS) * g_ref[...] + b_ref[...]

    row = pl.BlockSpec((ROW_TILE, d), lambda i: (i, 0))
    vec = pl.BlockSpec((1, d), lambda i: (0, 0))
    return pl.pallas_call(
        body, name=name, grid=(s // ROW_TILE,), in_specs=[row, row, vec, vec, vec], out_specs=row,
        out_shape=jax.ShapeDtypeStruct((s, d), F32), compiler_params=_params("parallel"),
    )(x, out, gate, g, b)


def _resid_ln_bwd(x, out, gate, g, dy, *, name):
    s, d = x.shape

    def body(x_ref, o_ref, gate_ref, g_ref, dy_ref, dxp_ref, dout_ref, dgate_ref, dg_ref, db_ref):
        @pl.when(pl.program_id(0) == 0)
        def _():
            dgate_ref[...] = jnp.zeros_like(dgate_ref)
            dg_ref[...] = jnp.zeros_like(dg_ref)
            db_ref[...] = jnp.zeros_like(db_ref)

        o = o_ref[...]
        r = ALPHA * x_ref[...] + gate_ref[...] * o
        mu = jnp.mean(r, axis=-1, keepdims=True)
        rc = r - mu
        var = jnp.mean(rc * rc, axis=-1, keepdims=True)
        rstd = lax.rsqrt(var + LN_EPS)
        xhat = rc * rstd
        dy = dy_ref[...]
        dxh = dy * g_ref[...]
        dr = rstd * (dxh - jnp.mean(dxh, axis=-1, keepdims=True) - xhat * jnp.mean(dxh * xhat, axis=-1, keepdims=True))
        dxp_ref[...] = ALPHA * dr
        dout_ref[...] = (gate_ref[...] * dr).astype(BF16)
        dgate_ref[...] += jnp.sum(dr * o, axis=0, keepdims=True)
        dg_ref[...] += jnp.sum(dy * xhat, axis=0, keepdims=True)
        db_ref[...] += jnp.sum(dy, axis=0, keepdims=True)

    row = pl.BlockSpec((ROW_TILE, d), lambda i: (i, 0))
    vec = pl.BlockSpec((1, d), lambda i: (0, 0))
    vshape = jax.ShapeDtypeStruct((1, d), F32)
    return pl.pallas_call(
        body, name=name, grid=(s // ROW_TILE,), in_specs=[row, row, vec, vec, row],
        out_specs=[row, row, vec, vec, vec],
        out_shape=[jax.ShapeDtypeStruct((s, d), F32), jax.ShapeDtypeStruct((s, d), BF16), vshape, vshape, vshape],
        compiler_params=_params("arbitrary"),
    )(x, out, gate, g, dy)


def _loss_and_grad(y, target, *, name):
    s, d = y.shape

    def body(y_ref, t_ref, dy_ref, loss_ref):
        @pl.when(pl.program_id(0) == 0)
        def _():
            loss_ref[...] = jnp.zeros_like(loss_ref)

        e = y_ref[...] - t_ref[...]
        dy_ref[...] = e * (1.0 / d)
        loss_ref[...] += jnp.sum(jnp.sum(e * e, axis=0, keepdims=True), axis=1, keepdims=True) * (0.5 / d)

    row = pl.BlockSpec((ROW_TILE, d), lambda i: (i, 0))
    return pl.pallas_call(
        body, name=name, grid=(s // ROW_TILE,), in_specs=[row, row],
        out_specs=[row, pl.BlockSpec((1, 128), lambda i: (0, 0))],
        out_shape=[jax.ShapeDtypeStruct((s, d), F32), jax.ShapeDtypeStruct((1, 128), F32)],
        compiler_params=_params("arbitrary"),
    )(y, target)


_SLOPES = np.asarray(2.0 ** (-8.0 * (np.arange(A_HEADS, dtype=np.float32) + 1.0) / A_HEADS), dtype=np.float32)


def _attn_scores(q, kw, slope, dist, valid):
    s = lax.dot_general(q, kw, _DIMS["nt"], preferred_element_type=F32) * (1.0 / math.sqrt(A_HEAD_DIM))
    return jnp.where(valid, s - slope * dist, NEG_BIG)


def _attn_window(blk, length, win, dil):
    start = pl.multiple_of(jnp.clip(blk * A_QBLOCK - A_RADIUS, 0, length - win), A_RADIUS)
    qpos = blk * A_QBLOCK + lax.broadcasted_iota(jnp.int32, (A_QBLOCK, win), 0)
    kpos = start + lax.broadcasted_iota(jnp.int32, (A_QBLOCK, win), 1)
    delta = jnp.abs(kpos - qpos)
    return start, (delta * dil).astype(F32), delta <= A_RADIUS


A_BLOCKS_PER_STEP = 4
A_GROUP_COLS = 3 * A_WIDTH


def _tile_scratch(rows, width):
    return pltpu.VMEM((width // 128, rows, 128), F32)


def _put_tile(scr, val):
    for j in range(scr.shape[0]):
        scr[j] = val[:, j * 128:(j + 1) * 128]


def _get_tile(scr):
    return jnp.concatenate([scr[j] for j in range(scr.shape[0])], axis=1)


def _get_residue(scr, r, dil):
    rows = pl.ds(r, scr.shape[1] // dil, stride=dil)
    return jnp.concatenate([scr.at[j][rows, :] for j in range(scr.shape[0])], axis=1)


def _put_residue(scr, r, dil, val):
    rows = pl.ds(r, scr.shape[1] // dil, stride=dil)
    for j in range(scr.shape[0]):
        scr.at[j][rows, :] = val[:, j * 128:(j + 1) * 128]


def _mm_dilated(a, b, dil, *, name):
    m, k = a.shape
    n = b.shape[1]
    tm = 512

    def body(a_ref, b_ref, o_ref, acc_ref):
        _put_tile(acc_ref, jnp.dot(a_ref[...], b_ref[...], preferred_element_type=F32))
        for r in range(dil):
            o_ref[:, r * n:(r + 1) * n] = _get_residue(acc_ref, r, dil).astype(BF16)

    return pl.pallas_call(
        body, name=name, grid=(m // tm,),
        in_specs=[pl.BlockSpec((tm, k), lambda i: (i, 0)), pl.BlockSpec((k, n), lambda i: (0, 0))],
        out_specs=pl.BlockSpec((tm // dil, dil * n), lambda i: (i, 0)),
        out_shape=jax.ShapeDtypeStruct((m // dil, dil * n), BF16), scratch_shapes=[_tile_scratch(tm, n)],
        compiler_params=_params("parallel"),
    )(a, b)


def _attn_fwd(pv, group, *, name, riders=()):
    dil = DILATIONS[group]
    length = pv.shape[0]
    cb, qoff = pv.shape[1] // (128 * dil), 0
    win = min(2 * A_QBLOCK, length)
    nblk = length // A_QBLOCK
    per = A_BLOCKS_PER_STEP if nblk % A_BLOCKS_PER_STEP == 0 else 1

    def body(slope_ref, q_ref, k_ref, v_ref, o_ref, lse_ref):
        hp = pl.program_id(1)
        for u in range(per):
            rows = slice(u * A_QBLOCK, (u + 1) * A_QBLOCK)
            start, dist, valid = _attn_window(pl.program_id(2) * per + u, length, win, dil)
            kw = k_ref[pl.ds(start, win), :]
            vw = v_ref[pl.ds(start, win), :]
            q = q_ref[rows, :]
            outs, lses = [], []
            for hh in range(2):
                sl = slice(hh * A_HEAD_DIM, (hh + 1) * A_HEAD_DIM)
                sc = _attn_scores(q[:, sl], kw[:, sl], slope_ref[hp * 2 + hh], dist, valid)
                m = jnp.max(sc, axis=-1, keepdims=True)
                p = jnp.exp(sc - m)
                z = jnp.sum(p, axis=-1, keepdims=True)
                o = jnp.dot(p.astype(BF16), vw[:, sl], preferred_element_type=F32) / z
                outs.append(o)
                lses.append(jnp.broadcast_to(m + jnp.log(z), (A_QBLOCK, A_HEAD_DIM)))
            o_ref[rows, :] = jnp.concatenate(outs, axis=1).astype(BF16)
            lse_ref[rows, :] = jnp.concatenate(lses, axis=1)

    qspec = pl.BlockSpec((per * A_QBLOCK, 128), lambda r, hp, b: (b, r * cb + qoff + hp))
    kspec = pl.BlockSpec((length, 128), lambda r, hp, b: (0, r * cb + qoff + 8 + hp))
    vspec = pl.BlockSpec((length, 128), lambda r, hp, b: (0, r * cb + qoff + 16 + hp))
    ospec = pl.BlockSpec((per * A_QBLOCK, 128), lambda r, hp, b: (b, r * 8 + hp))
    oshape = jax.ShapeDtypeStruct((length, dil * A_WIDTH), F32)
    grid = (dil, 8, nblk // per)
    anywhere, exchanged, sems = _rider_specs(riders)
    outs = pl.pallas_call(
        _riding(body, riders, 4, 2, grid), name=name, grid=grid,
        in_specs=[pl.BlockSpec(memory_space=pltpu.SMEM), qspec, kspec, vspec] + anywhere, out_specs=[ospec, ospec] + anywhere,
        out_shape=[jax.ShapeDtypeStruct(oshape.shape, BF16), oshape] + exchanged, scratch_shapes=sems,
        compiler_params=_params(*(("arbitrary",) * 3 if riders else ("parallel", "parallel", "arbitrary"))),
    )(jnp.asarray(_SLOPES), pv, pv, pv, *[x for x, _ in riders])
    return (outs[0], outs[1], list(outs[2:])) if riders else (outs[0], outs[1])


def _attn_bwd(pv, group, do, o, lse, *, name):
    dil = DILATIONS[group]
    length = pv.shape[0]
    cb, qoff = pv.shape[1] // (128 * dil), 0
    win = min(2 * A_QBLOCK, length)
    nblk = length // A_QBLOCK
    per = A_BLOCKS_PER_STEP if nblk % A_BLOCKS_PER_STEP == 0 else 1
    nstep = nblk // per
    scale = 1.0 / math.sqrt(A_HEAD_DIM)

    def body(slope_ref, q_ref, k_ref, v_ref, do_ref, o_ref, lse_ref, dq_ref, dk_ref, dv_ref, dk_acc, dv_acc):
        hp, step = pl.program_id(1), pl.program_id(2)

        @pl.when(step == 0)
        def _():
            dk_acc[...] = jnp.zeros_like(dk_acc)
            dv_acc[...] = jnp.zeros_like(dv_acc)

        for u in range(per):
            rows = slice(u * A_QBLOCK, (u + 1) * A_QBLOCK)
            start, dist, valid = _attn_window(step * per + u, length, win, dil)
            kw = k_ref[pl.ds(start, win), :]
            vw = v_ref[pl.ds(start, win), :]
            q = q_ref[rows, :]
            do_b = do_ref[rows, :]
            dsum = do_b.astype(F32) * o_ref[rows, :]
            lse_b = lse_ref[rows, :]
            dqs, dks, dvs = [], [], []
            for hh in range(2):
                sl = slice(hh * A_HEAD_DIM, (hh + 1) * A_HEAD_DIM)
                sc = _attn_scores(q[:, sl], kw[:, sl], slope_ref[hp * 2 + hh], dist, valid)
                p = jnp.exp(sc - lse_b[:, hh * A_HEAD_DIM:hh * A_HEAD_DIM + 1])
                dp = lax.dot_general(do_b[:, sl], vw[:, sl], _DIMS["nt"], preferred_element_type=F32)
                ds = (p * (dp - jnp.sum(dsum[:, sl], axis=-1, keepdims=True))).astype(BF16)
                dqs.append(jnp.dot(ds, kw[:, sl], preferred_element_type=F32) * scale)
                dks.append(lax.dot_general(ds, q[:, sl], _DIMS["tn"], preferred_element_type=F32) * scale)
                dvs.append(lax.dot_general(p.astype(BF16), do_b[:, sl], _DIMS["tn"], preferred_element_type=F32))
            dq_ref[rows, :] = jnp.concatenate(dqs, axis=1).astype(BF16)
            dk_acc[pl.ds(start, win), :] += jnp.concatenate(dks, axis=1)
            dv_acc[pl.ds(start, win), :] += jnp.concatenate(dvs, axis=1)

        @pl.when(step == nstep - 1)
        def _():
            dk_ref[...] = dk_acc[...].astype(BF16)
            dv_ref[...] = dv_acc[...].astype(BF16)

    qspec = pl.BlockSpec((per * A_QBLOCK, 128), lambda r, hp, b: (b, r * cb + qoff + hp))
    kspec = pl.BlockSpec((length, 128), lambda r, hp, b: (0, r * cb + qoff + 8 + hp))
    vspec = pl.BlockSpec((length, 128), lambda r, hp, b: (0, r * cb + qoff + 16 + hp))
    bspec = pl.BlockSpec((per * A_QBLOCK, 128), lambda r, hp, b: (b, r * 8 + hp))
    fspec = pl.BlockSpec((length, 128), lambda r, hp, b: (0, r * 8 + hp))
    oshape = jax.ShapeDtypeStruct((length, dil * A_WIDTH), BF16)
    dq, dk, dv = pl.pallas_call(
        body, name=name, grid=(dil, 8, nstep),
        in_specs=[pl.BlockSpec(memory_space=pltpu.SMEM), qspec, kspec, vspec, bspec, bspec, bspec],
        out_specs=[bspec, fspec, fspec], out_shape=[oshape, oshape, oshape],
        scratch_shapes=[pltpu.VMEM((length, 128), F32), pltpu.VMEM((length, 128), F32)],
        compiler_params=_params("parallel", "parallel", "arbitrary"),
    )(jnp.asarray(_SLOPES), pv, pv, pv, do, o, lse)
    return dq, dk, dv


A_GATE_BLOCK = 3
A_ROWS = 256


def _lanes_of(r):
    return slice(r * A_WIDTH, (r + 1) * A_WIDTH)


def _dilated_spec(dil):
    return pl.BlockSpec((A_ROWS // dil, dil * A_WIDTH), lambda i: (i, 0))


def _attn_combine(o0, l0, o1, l1, o2, l2, proj0, *, name):
    s = proj0.shape[0]

    def body(o0_ref, l0_ref, o1_ref, l1_ref, o2_ref, l2_ref, gate_ref, y_ref, o_ref, lse_ref, so1, sl1, so2, sl2):
        for src, dst, dil in ((o1_ref, so1, DILATIONS[1]), (l1_ref, sl1, DILATIONS[1]),
                              (o2_ref, so2, DILATIONS[2]), (l2_ref, sl2, DILATIONS[2])):
            for r in range(dil):
                _put_residue(dst, r, dil, src[:, _lanes_of(r)].astype(F32))
        la, lb, lc = l0_ref[...], _get_tile(sl1), _get_tile(sl2)
        m = jnp.maximum(jnp.maximum(la, lb), lc)
        ea, eb, ec = jnp.exp(la - m), jnp.exp(lb - m), jnp.exp(lc - m)
        den = ea + eb + ec
        o = (ea * o0_ref[...].astype(F32) + eb * _get_tile(so1) + ec * _get_tile(so2)) / den
        o_ref[...] = o
        lse_ref[...] = m + jnp.log(den)
        y_ref[...] = (o * _silu_and_grad(gate_ref[...].astype(F32))[0]).astype(BF16)

    row = pl.BlockSpec((A_ROWS, A_WIDTH), lambda i: (i, 0))
    gspec = pl.BlockSpec((A_ROWS, A_WIDTH), lambda i: (i, A_GATE_BLOCK))
    d1, d2 = _dilated_spec(DILATIONS[1]), _dilated_spec(DILATIONS[2])
    return pl.pallas_call(
        body, name=name, grid=(s // A_ROWS,), in_specs=[row, row, d1, d1, d2, d2, gspec], out_specs=[row, row, row],
        out_shape=[jax.ShapeDtypeStruct((s, A_WIDTH), BF16), jax.ShapeDtypeStruct((s, A_WIDTH), F32),
                   jax.ShapeDtypeStruct((s, A_WIDTH), F32)],
        scratch_shapes=[_tile_scratch(A_ROWS, A_WIDTH)] * 4, compiler_params=_params("parallel"),
    )(o0, l0, o1, l1, o2, l2, proj0)


def _attn_combine_bwd(dy, o, lse, proj0, *, name):
    s = proj0.shape[0]

    def body(dy_ref, o_ref, lse_ref, gate_ref, dg_ref, do_ref, do1, o1, l1, do2, o2, l2, s_do, s_o, s_l):
        si, dsi = _silu_and_grad(gate_ref[...].astype(F32))
        dyv = dy_ref[...]
        ov = o_ref[...]
        do = dyv * si
        _put_tile(s_do, do)
        _put_tile(s_o, ov)
        _put_tile(s_l, lse_ref[...])
        do_ref[...] = do.astype(BF16)
        dg_ref[...] = (dyv * ov * dsi).astype(BF16)
        for (do_d, o_d, l_d), dil in (((do1, o1, l1), DILATIONS[1]), ((do2, o2, l2), DILATIONS[2])):
            for r in range(dil):
                do_d[:, _lanes_of(r)] = _get_residue(s_do, r, dil).astype(BF16)
                o_d[:, _lanes_of(r)] = _get_residue(s_o, r, dil)
                l_d[:, _lanes_of(r)] = _get_residue(s_l, r, dil)

    row = pl.BlockSpec((A_ROWS, A_WIDTH), lambda i: (i, 0))
    gspec = pl.BlockSpec((A_ROWS, A_WIDTH), lambda i: (i, A_GATE_BLOCK))
    shp = jax.ShapeDtypeStruct((s, A_WIDTH), BF16)
    dilated = lambda dil, dtype: jax.ShapeDtypeStruct((s // dil, dil * A_WIDTH), dtype)
    d1, d2 = _dilated_spec(DILATIONS[1]), _dilated_spec(DILATIONS[2])
    return pl.pallas_call(
        body, name=name, grid=(s // A_ROWS,), in_specs=[row, row, row, gspec],
        out_specs=[row, row, d1, d1, d1, d2, d2, d2],
        out_shape=[shp, shp, dilated(DILATIONS[1], BF16), dilated(DILATIONS[1], F32), dilated(DILATIONS[1], F32),
                   dilated(DILATIONS[2], BF16), dilated(DILATIONS[2], F32), dilated(DILATIONS[2], F32)],
        scratch_shapes=[_tile_scratch(A_ROWS, A_WIDTH)] * 3, compiler_params=_params("parallel"),
    )(dy, o, lse, proj0)


def _assemble_dproj(parts0, parts1, parts2, dgate, *, name):
    s = dgate.shape[0]

    def body(*refs):
        ins, out_ref, scr = refs[:10], refs[10], refs[11]
        for p in range(3):
            out_ref[:, _lanes_of(p)] = ins[p][...]
        for g, dil in ((1, DILATIONS[1]), (2, DILATIONS[2])):
            for p in range(3):
                src = ins[3 * g + p]
                for r in range(dil):
                    _put_residue(scr, r, dil, src[:, _lanes_of(r)].astype(F32))
                out_ref[:, _lanes_of(3 * g + p)] = _get_tile(scr).astype(BF16)
        out_ref[:, _lanes_of(9)] = ins[9][...]

    row = pl.BlockSpec((A_ROWS, A_WIDTH), lambda i: (i, 0))
    d1, d2 = _dilated_spec(DILATIONS[1]), _dilated_spec(DILATIONS[2])
    return pl.pallas_call(
        body, name=name, grid=(s // A_ROWS,), in_specs=[row] * 3 + [d1] * 3 + [d2] * 3 + [row],
        out_specs=pl.BlockSpec((A_ROWS, A_IN_COLS), lambda i: (i, 0)),
        out_shape=jax.ShapeDtypeStruct((s, A_IN_COLS), BF16),
        scratch_shapes=[_tile_scratch(A_ROWS, A_WIDTH)], compiler_params=_params("parallel"),
    )(*parts0, *parts1, *parts2, dgate)


CONV_TILE = 256
CONV_SUB = 4


def _conv_taps(xe, n):
    return [xe if j == 2 else pltpu.roll(xe, (2 - j) % n, 0) for j in range(SSM_CONV)]


def _conv_fwd(xpad, w, b, *, name):
    s = xpad.shape[0] - 2 * CONV_HALO
    n = CONV_TILE + 2 * CONV_HALO
    ncol = SSM_CONV_DIM // 128

    sub = min(CONV_SUB, s // CONV_TILE)

    def body(x_ref, w_ref, b_ref, o_ref):
        base = pl.program_id(1) * (sub * CONV_TILE)

        def tile(k, carry):
            r0 = pl.multiple_of(k * CONV_TILE, CONV_TILE)
            t0 = pl.multiple_of(base + r0, CONV_TILE)
            taps = _conv_taps(x_ref[pl.ds(t0, n), :].astype(F32), n)
            pre = b_ref[...]
            for j in range(SSM_CONV):
                pre = pre + w_ref[j:j + 1, :] * taps[j]
            o_ref[pl.ds(r0, CONV_TILE), :] = _silu_and_grad(pre[CONV_HALO:CONV_HALO + CONV_TILE])[0].astype(BF16)
            return carry

        lax.fori_loop(0, sub, tile, 0)

    return pl.pallas_call(
        body, name=name, grid=(ncol, s // (sub * CONV_TILE)),
        in_specs=[pl.BlockSpec((s + 2 * CONV_HALO, 128), lambda j, i: (0, j)),
                  pl.BlockSpec((SSM_CONV, 128), lambda j, i: (0, j)), pl.BlockSpec((1, 128), lambda j, i: (0, j))],
        out_specs=pl.BlockSpec((sub * CONV_TILE, 128), lambda j, i: (i, j)),
        out_shape=jax.ShapeDtypeStruct((s, SSM_CONV_DIM), BF16), compiler_params=_params("parallel", "arbitrary"),
    )(xpad, w, b)


def _conv_bwd(xpad, dapad, w, b, *, name):
    s = xpad.shape[0] - 2 * CONV_HALO
    n = CONV_TILE + 2 * CONV_HALO
    ncol = SSM_CONV_DIM // 128
    mid = slice(CONV_HALO, CONV_HALO + CONV_TILE)
    sub = min(CONV_SUB, s // CONV_TILE)

    def body(x_ref, da_ref, w_ref, b_ref, dx_ref, dw_ref, db_ref):
        @pl.when(pl.program_id(1) == 0)
        def _():
            dw_ref[...] = jnp.zeros_like(dw_ref)
            db_ref[...] = jnp.zeros_like(db_ref)

        base = pl.program_id(1) * (sub * CONV_TILE)

        def tile(k, carry):
            r0 = pl.multiple_of(k * CONV_TILE, CONV_TILE)
            t0 = pl.multiple_of(base + r0, CONV_TILE)
            taps = _conv_taps(x_ref[pl.ds(t0, n), :].astype(F32), n)
            pre = b_ref[...]
            for j in range(SSM_CONV):
                pre = pre + w_ref[j:j + 1, :] * taps[j]
            dpre = da_ref[pl.ds(t0, n), :] * _silu_and_grad(pre)[1]
            dx = jnp.zeros((CONV_TILE, 128), F32)
            for j in range(SSM_CONV):
                back = dpre if j == 2 else pltpu.roll(dpre, (j - 2) % n, 0)
                dx = dx + w_ref[j:j + 1, :] * back[mid]
                dw_ref[j:j + 1, :] += jnp.sum(dpre[mid] * taps[j][mid], axis=0, keepdims=True)
            dx_ref[pl.ds(r0, CONV_TILE), :] = dx.astype(BF16)
            db_ref[...] += jnp.sum(dpre[mid], axis=0, keepdims=True)
            return carry

        lax.fori_loop(0, sub, tile, 0)

    full = pl.BlockSpec((s + 2 * CONV_HALO, 128), lambda j, i: (0, j))
    wspec = pl.BlockSpec((SSM_CONV, 128), lambda j, i: (0, j))
    bspec = pl.BlockSpec((1, 128), lambda j, i: (0, j))
    return pl.pallas_call(
        body, name=name, grid=(ncol, s // (sub * CONV_TILE)), in_specs=[full, full, wspec, bspec],
        out_specs=[pl.BlockSpec((sub * CONV_TILE, 128), lambda j, i: (i, j)), wspec, bspec],
        out_shape=[jax.ShapeDtypeStruct((s, SSM_CONV_DIM), BF16), jax.ShapeDtypeStruct((SSM_CONV, SSM_CONV_DIM), F32),
                   jax.ShapeDtypeStruct((1, SSM_CONV_DIM), F32)],
        compiler_params=_params("parallel", "arbitrary"),
    )(xpad, dapad, w, b)


HPG = SSM_HEADS // SSM_GROUPS
GW = HPG * SSM_HEAD_DIM
T = SSM_CHUNK


def _ssd_specs(nc):
    ceff = lambda d, c: jnp.where(d == 0, c, nc - 1 - c)
    return ceff, [
        pl.BlockSpec((T, GW), lambda d, g, c: (ceff(d, c), g)),
        pl.BlockSpec((T, SSM_STATE), lambda d, g, c: (ceff(d, c), SSM_INNER // 128 + g)),
        pl.BlockSpec((T, SSM_STATE), lambda d, g, c: (ceff(d, c), SSM_INNER // 128 + SSM_GROUPS + g)),
        pl.BlockSpec((None, None, T, HPG), lambda d, g, c: (d, g, ceff(d, c), 0)),
        pl.BlockSpec((None, None, HPG, T), lambda d, g, c: (d, g, 0, ceff(d, c))),
        pl.BlockSpec((None, None, 2, HPG), lambda d, g, c: (d, g, 0, 0)),
        pl.BlockSpec((None, None, HPG, 2), lambda d, g, c: (d, g, 0, 0)),
    ]


def _ssd_chunk_common(d, dt_ref, dtt_ref, prr_ref, prc_ref):
    sgn = 1 - 2 * d
    ri = lax.broadcasted_iota(jnp.int32, (T, T), 0)
    ci = lax.broadcasted_iota(jnp.int32, (T, T), 1)
    mask = ((ri - ci) * sgn) >= 0
    maskf = mask.astype(F32)
    bias_r, a_r = prr_ref[0:1, :], prr_ref[1:2, :]
    bias_c, a_c = prc_ref[:, 0:1], prc_ref[:, 1:2]
    raw = dt_ref[...] + bias_r
    dt_rows = _softplus(raw)
    dt_lanes = _softplus(dtt_ref[...] + bias_c)
    a_rows = dt_rows * a_r
    acum_rows = jnp.dot(maskf, a_rows, precision=HIGHEST, preferred_element_type=F32)
    acum_lanes = lax.dot_general(dt_lanes * a_c, maskf, _DIMS["nt"], precision=HIGHEST, preferred_element_type=F32)
    tot = jnp.sum(a_rows, axis=0, keepdims=True)
    return mask, maskf, raw, dt_rows, a_r, acum_rows, acum_lanes, tot


def _lanes_per_head(pieces):
    return jnp.concatenate([jnp.broadcast_to(p, (p.shape[0], SSM_HEAD_DIM)) for p in pieces], axis=1)


def _ssd_fwd_v1(xbc, dtr, dtt, pr_rows, pr_cols, *, name):
    s = xbc.shape[0]
    nc = s // T
    ceff, in_specs = _ssd_specs(nc)

    def body(x_ref, b_ref, c_ref, dt_ref, dtt_ref, prr_ref, prc_ref, y_ref, hs_ref, st_ref):
        d, c = pl.program_id(0), pl.program_id(2)

        @pl.when(c == 0)
        def _():
            st_ref[...] = jnp.zeros_like(st_ref)

        mask, _, _, dt_rows, _, acum_rows, acum_lanes, tot = _ssd_chunk_common(d, dt_ref, dtt_ref, prr_ref, prc_ref)
        xs = x_ref[...].astype(F32)
        bm, cm = b_ref[...], c_ref[...]
        hprev = st_ref[...]
        hs_ref[...] = hprev
        cb = lax.dot_general(cm, bm, _DIMS["nt"], preferred_element_type=F32)
        ch = jnp.dot(cm, hprev.astype(BF16), preferred_element_type=F32)
        ys, xgds, etots = [], [], []
        for j in range(HPG):
            sl = slice(j * SSM_HEAD_DIM, (j + 1) * SSM_HEAD_DIM)
            ac, al = acum_rows[:, j:j + 1], acum_lanes[j:j + 1, :]
            lm = jnp.where(mask, jnp.exp(jnp.minimum(ac - al, 0.0)), 0.0)
            xg = xs[:, sl] * dt_rows[:, j:j + 1]
            yd = jnp.dot((cb * lm).astype(BF16), xg.astype(BF16), preferred_element_type=F32)
            ys.append(yd + jnp.exp(ac) * ch[:, sl])
            xgds.append(xg * jnp.exp(tot[:, j:j + 1] - ac))
            etots.append(jnp.exp(tot[:, j:j + 1]))
        y_ref[...] = jnp.concatenate(ys, axis=1)
        new = lax.dot_general(bm, jnp.concatenate(xgds, axis=1).astype(BF16), _DIMS["tn"], preferred_element_type=F32)
        st_ref[...] = hprev * _lanes_per_head(etots) + new

    return pl.pallas_call(
        body, name=name, grid=(2, SSM_GROUPS, nc), in_specs=in_specs,
        out_specs=[pl.BlockSpec((None, T, GW), lambda d, g, c: (d, ceff(d, c), g)),
                   pl.BlockSpec((None, None, None, SSM_STATE, GW), lambda d, g, c: (d, ceff(d, c), g, 0, 0))],
        out_shape=[jax.ShapeDtypeStruct((2, s, SSM_INNER), F32),
                   jax.ShapeDtypeStruct((2, nc, SSM_GROUPS, SSM_STATE, GW), F32)],
        scratch_shapes=[pltpu.VMEM((SSM_STATE, GW), F32)],
        compiler_params=_params("parallel", "parallel", "arbitrary"),
    )(xbc, xbc, xbc, dtr, dtt, pr_rows, pr_cols)


def _put_lane(j, col):
    lane = lax.broadcasted_iota(jnp.int32, (col.shape[0], HPG), 1)
    return jnp.where(lane == j, col, 0.0)


def _ssd_bwd_v1(xbc, dtr, dtt, pr_rows, pr_cols, dvec, hs, dy, *, name):
    s = xbc.shape[0]
    nc = s // T
    cb_of = lambda d, c: jnp.where(d == 0, nc - 1 - c, c)
    in_specs = [
        pl.BlockSpec((T, GW), lambda d, g, c: (cb_of(d, c), g)),
        pl.BlockSpec((T, SSM_STATE), lambda d, g, c: (cb_of(d, c), SSM_INNER // 128 + g)),
        pl.BlockSpec((T, SSM_STATE), lambda d, g, c: (cb_of(d, c), SSM_INNER // 128 + SSM_GROUPS + g)),
        pl.BlockSpec((None, None, T, HPG), lambda d, g, c: (d, g, cb_of(d, c), 0)),
        pl.BlockSpec((None, None, HPG, T), lambda d, g, c: (d, g, 0, cb_of(d, c))),
        pl.BlockSpec((None, None, 2, HPG), lambda d, g, c: (d, g, 0, 0)),
        pl.BlockSpec((None, None, HPG, 2), lambda d, g, c: (d, g, 0, 0)),
        pl.BlockSpec((1, GW), lambda d, g, c: (0, g)),
        pl.BlockSpec((None, None, None, SSM_STATE, GW), lambda d, g, c: (d, cb_of(d, c), g, 0, 0)),
        pl.BlockSpec((T, GW), lambda d, g, c: (cb_of(d, c), g)),
    ]

    def body(x_ref, b_ref, c_ref, dt_ref, dtt_ref, prr_ref, prc_ref, dvec_ref, hs_ref, dy_ref,
             dxs_ref, db_ref, dc_ref, ddt_ref, dalog_ref, dbias_ref, g_ref):
        d, c = pl.program_id(0), pl.program_id(2)

        @pl.when(c == 0)
        def _():
            g_ref[...] = jnp.zeros_like(g_ref)
            dalog_ref[...] = jnp.zeros_like(dalog_ref)
            dbias_ref[...] = jnp.zeros_like(dbias_ref)

        mask, maskf, raw, dt_rows, a_r, acum_rows, acum_lanes, tot = _ssd_chunk_common(
            d, dt_ref, dtt_ref, prr_ref, prc_ref)
        xs = x_ref[...].astype(F32)
        bm, cm = b_ref[...], c_ref[...]
        hst = hs_ref[...]
        gst = g_ref[...]
        dyv = dy_ref[...]
        dyb = dyv.astype(BF16)
        dv = dvec_ref[...] * (1 - d).astype(F32)
        cb = lax.dot_general(cm, bm, _DIMS["nt"], preferred_element_type=F32)
        ch = jnp.dot(cm, hst.astype(BF16), preferred_element_type=F32)
        bg = jnp.dot(bm, gst.astype(BF16), preferred_element_type=F32)
        hg = hst * gst
        dcb = jnp.zeros((T, T), F32)
        dacum = jnp.zeros((T, HPG), F32)
        rx = jnp.zeros((T, HPG), F32)
        dtot = jnp.zeros((1, HPG), F32)
        dxss, dyes, xgds, etots = [], [], [], []
        for j in range(HPG):
            sl = slice(j * SSM_HEAD_DIM, (j + 1) * SSM_HEAD_DIM)
            ac, al = acum_rows[:, j:j + 1], acum_lanes[j:j + 1, :]
            lm = jnp.where(mask, jnp.exp(jnp.minimum(ac - al, 0.0)), 0.0)
            m = cb * lm
            dtj = dt_rows[:, j:j + 1]
            xsj = xs[:, sl]
            xg = xsj * dtj
            dyj = dyv[:, sl]
            ec = jnp.exp(ac)
            etot = jnp.exp(tot[:, j:j + 1])
            decay = jnp.exp(tot[:, j:j + 1] - ac)
            dm = lax.dot_general(dyb[:, sl], xg.astype(BF16), _DIMS["nt"], preferred_element_type=F32)
            w = dm * m
            dcb = dcb + dm * lm
            bgj = bg[:, sl]
            dxg = lax.dot_general(m.astype(BF16), dyb[:, sl], _DIMS["tn"], preferred_element_type=F32) + decay * bgj
            xb = decay * jnp.sum(xg * bgj, axis=-1, keepdims=True)
            da_j = (jnp.sum(w, axis=-1, keepdims=True) - jnp.sum(w.T, axis=-1, keepdims=True)
                    + jnp.sum(ec * ch[:, sl] * dyj, axis=-1, keepdims=True) - xb)
            dacum = dacum + _put_lane(j, da_j)
            rx = rx + _put_lane(j, jnp.sum(dxg * xsj, axis=-1, keepdims=True))
            dtot_j = (etot * jnp.sum(jnp.sum(hg[:, sl], axis=0, keepdims=True), axis=1, keepdims=True)
                      + jnp.sum(xb, axis=0, keepdims=True))
            dtot = dtot + _put_lane(j, dtot_j)
            dxss.append(dxg * dtj + dv[:, sl] * dyj)
            dyes.append(dyj * ec)
            xgds.append(xg * decay)
            etots.append(etot)
        da = lax.dot_general(maskf, dacum, _DIMS["tn"], precision=HIGHEST, preferred_element_type=F32) + dtot
        ddt = da * a_r + rx
        draw = ddt * _sigmoid(raw)
        ddt_ref[...] = draw
        dbias_ref[...] += jnp.sum(draw, axis=0, keepdims=True)
        dalog_ref[...] += jnp.sum(da * dt_rows, axis=0, keepdims=True) * a_r
        dxs_ref[...] = jnp.concatenate(dxss, axis=1)
        dye = jnp.concatenate(dyes, axis=1).astype(BF16)
        xgd = jnp.concatenate(xgds, axis=1).astype(BF16)
        dcbb = dcb.astype(BF16)
        dc_ref[...] = (jnp.dot(dcbb, bm, preferred_element_type=F32)
                       + lax.dot_general(dye, hst.astype(BF16), _DIMS["nt"], preferred_element_type=F32))
        db_ref[...] = (lax.dot_general(dcbb, cm, _DIMS["tn"], preferred_element_type=F32)
                       + lax.dot_general(xgd, gst.astype(BF16), _DIMS["nt"], preferred_element_type=F32))
        g_ref[...] = lax.dot_general(cm, dye, _DIMS["tn"], preferred_element_type=F32) + gst * _lanes_per_head(etots)

    small = pl.BlockSpec((None, None, 1, HPG), lambda d, g, c: (d, g, 0, 0))
    sshape = jax.ShapeDtypeStruct((2, SSM_GROUPS, 1, HPG), F32)
    return pl.pallas_call(
        body, name=name, grid=(2, SSM_GROUPS, nc), in_specs=in_specs,
        out_specs=[pl.BlockSpec((None, T, GW), lambda d, g, c: (d, cb_of(d, c), g)),
                   pl.BlockSpec((None, T, SSM_STATE), lambda d, g, c: (d, cb_of(d, c), g)),
                   pl.BlockSpec((None, T, SSM_STATE), lambda d, g, c: (d, cb_of(d, c), g)),
                   pl.BlockSpec((None, None, T, HPG), lambda d, g, c: (d, g, cb_of(d, c), 0)), small, small],
        out_shape=[jax.ShapeDtypeStruct((2, s, SSM_INNER), F32),
                   jax.ShapeDtypeStruct((2, s, SSM_GROUPS * SSM_STATE), F32),
                   jax.ShapeDtypeStruct((2, s, SSM_GROUPS * SSM_STATE), F32),
                   jax.ShapeDtypeStruct((2, SSM_GROUPS, s, HPG), F32), sshape, sshape],
        scratch_shapes=[pltpu.VMEM((SSM_STATE, GW), F32)],
        compiler_params=_params("parallel", "parallel", "arbitrary"),
    )(xbc, xbc, xbc, dtr, dtt, pr_rows, pr_cols, dvec, hs, dy)


PAIRS = HPG // 2


def _scan_lanes(x, forward):
    lane = lax.broadcasted_iota(jnp.int32, x.shape, 1)
    p = x
    k = 1
    while k < T:
        p = p + jnp.where(lane >= k, pltpu.roll(p, k, 1), 0.0)
        k *= 2
    tot = p[:, T - 1:T]
    return jnp.where(forward, p, tot - p + x), tot


def _ssd_chunk(d, dt_ref, dtt_ref, prr_ref, prc_ref):
    sgn = 1 - 2 * d
    ri = lax.broadcasted_iota(jnp.int32, (T, T), 0)
    ci = lax.broadcasted_iota(jnp.int32, (T, T), 1)
    mask = ((ri - ci) * sgn) >= 0
    mask_t = ((ci - ri) * sgn) >= 0
    bias_r = prr_ref[0:1, :]
    bias_c, a_c = prc_ref[:, 0:1], prc_ref[:, 1:2]
    dt_rows = _softplus(dt_ref[...] + bias_r)
    raw_lanes = dtt_ref[...] + bias_c
    dt_lanes = _softplus(raw_lanes)
    acum_lanes, tot = _scan_lanes(dt_lanes * a_c, d == 0)
    return dict(mask=mask, mask_t=mask_t, head0=ci < SSM_HEAD_DIM, dt_rows=dt_rows, raw_lanes=raw_lanes,
                dt_lanes=dt_lanes, a_c=a_c, acum_lanes=acum_lanes, acum_rows=acum_lanes.T, tot=tot)


def _ssd_pair(ck, q):
    h0 = ck["head0"]
    colb = lambda rows, j: jnp.broadcast_to(rows[:, j:j + 1], (T, T))
    rowb = lambda lanes, j: jnp.broadcast_to(lanes[j:j + 1, :], (T, T))
    lms, lmts, acs = [], [], []
    for j in (2 * q, 2 * q + 1):
        ac, al = colb(ck["acum_rows"], j), rowb(ck["acum_lanes"], j)
        lms.append(jnp.where(ck["mask"], jnp.exp(jnp.minimum(ac - al, 0.0)), 0.0))
        lmts.append(jnp.where(ck["mask_t"], jnp.exp(jnp.minimum(al - ac, 0.0)), 0.0))
        acs.append(ac)
    ac_pair = jnp.where(h0, acs[0], acs[1])
    dt_pair = jnp.where(h0, colb(ck["dt_rows"], 2 * q), colb(ck["dt_rows"], 2 * q + 1))
    tot_pair = jnp.where(h0[0:1], ck["tot"][2 * q:2 * q + 1, :], ck["tot"][2 * q + 1:2 * q + 2, :])
    return dict(lm=lms, lmt=lmts, dt=dt_pair, ec=jnp.exp(ac_pair), decay=jnp.exp(tot_pair - ac_pair),
                etot=jnp.exp(tot_pair))


def _split_heads(h0, v):
    zero = jnp.zeros_like(v)
    return jnp.where(h0, v, zero), jnp.where(h0, zero, v)


GPS = 2
GSTEPS = SSM_GROUPS // GPS
B_BLOCK0 = SSM_INNER // (GPS * SSM_STATE)
C_BLOCK0 = (SSM_INNER + SSM_GROUPS * SSM_STATE) // (GPS * SSM_STATE)


def _ssd_in_specs(chunk):
    return [
        pl.BlockSpec((T, GPS * GW), lambda d, g, c: (chunk(d, c), g)),
        pl.BlockSpec((T, GPS * SSM_STATE), lambda d, g, c: (chunk(d, c), B_BLOCK0 + g)),
        pl.BlockSpec((T, GPS * SSM_STATE), lambda d, g, c: (chunk(d, c), C_BLOCK0 + g)),
        pl.BlockSpec((GPS * SSM_STATE, T), lambda d, g, c: (g, chunk(d, c))),
        pl.BlockSpec((GPS * SSM_STATE, T), lambda d, g, c: (g, chunk(d, c))),
        pl.BlockSpec((None, GPS, T, HPG), lambda d, g, c: (d, g, chunk(d, c), 0)),
        pl.BlockSpec((None, GPS, HPG, T), lambda d, g, c: (d, g, 0, chunk(d, c))),
        pl.BlockSpec((None, GPS, 2, HPG), lambda d, g, c: (d, g, 0, 0)),
        pl.BlockSpec((None, GPS, HPG, 2), lambda d, g, c: (d, g, 0, 0)),
    ]


def _group_refs(gi, wide, state_wide, t_wide, lead):
    return ([r.at[:, pl.ds(gi * GW, GW)] for r in wide] + [r.at[:, pl.ds(gi * SSM_STATE, SSM_STATE)] for r in state_wide]
            + [r.at[pl.ds(gi * SSM_STATE, SSM_STATE), :] for r in t_wide] + [r.at[gi] for r in lead])


def _ssd_fwd(xbc, bt, ct, dtr, dtt, pr_rows, pr_cols, *, name, riders=()):
    s = xbc.shape[0]
    nc = s // T
    chunk = lambda d, c: jnp.where(d == 0, c, nc - 1 - c)

    def body(x_ref, b_ref, c_ref, bt_ref, ct_ref, dt_ref, dtt_ref, prr_ref, prc_ref, y_ref, hs_ref, st_ref):
        @pl.when(pl.program_id(2) == 0)
        def _():
            st_ref[...] = jnp.zeros_like(st_ref)

        for gi in range(GPS):
            group(*_group_refs(gi, [x_ref, y_ref], [b_ref, c_ref], [bt_ref, ct_ref],
                               [dt_ref, dtt_ref, prr_ref, prc_ref, hs_ref, st_ref]))

    def group(x_ref, y_ref, b_ref, c_ref, bt_ref, ct_ref, dt_ref, dtt_ref, prr_ref, prc_ref, hs_ref, st_ref):
        d = pl.program_id(0)
        ck = _ssd_chunk(d, dt_ref, dtt_ref, prr_ref, prc_ref)
        xs = x_ref[...].astype(F32)
        cm = c_ref[...]
        hprev = st_ref[...]
        hs_ref[...] = hprev
        cb = lax.dot_general(cm, b_ref[...], _DIMS["nt"], preferred_element_type=F32)
        ch = jnp.dot(cm, hprev.astype(BF16), preferred_element_type=F32)
        ys, xgds, etots = [], [], []
        for q in range(PAIRS):
            sl = slice(q * 128, (q + 1) * 128)
            pr = _ssd_pair(ck, q)
            xg = xs[:, sl] * pr["dt"]
            xg0, xg1 = _split_heads(ck["head0"], xg.astype(BF16))
            yd = (jnp.dot((cb * pr["lm"][0]).astype(BF16), xg0, preferred_element_type=F32)
                  + jnp.dot((cb * pr["lm"][1]).astype(BF16), xg1, preferred_element_type=F32))
            ys.append(yd + pr["ec"] * ch[:, sl])
            xgds.append(xg * pr["decay"])
            etots.append(pr["etot"])
        y_ref[...] = jnp.concatenate(ys, axis=1)
        new = jnp.dot(bt_ref[...], jnp.concatenate(xgds, axis=1).astype(BF16), preferred_element_type=F32)
        st_ref[...] = hprev * jnp.concatenate(etots, axis=1) + new

    grid = (2, GSTEPS, nc)
    anywhere, exchanged, sems = _rider_specs(riders)
    outs = pl.pallas_call(
        _riding(body, riders, 9, 2, grid), name=name, grid=grid, in_specs=_ssd_in_specs(chunk) + anywhere,
        out_specs=[pl.BlockSpec((None, T, GPS * GW), lambda d, g, c: (d, chunk(d, c), g)),
                   pl.BlockSpec((None, None, GPS, SSM_STATE, GW), lambda d, g, c: (d, chunk(d, c), g, 0, 0))] + anywhere,
        out_shape=[jax.ShapeDtypeStruct((2, s, SSM_INNER), F32),
                   jax.ShapeDtypeStruct((2, nc, SSM_GROUPS, SSM_STATE, GW), F32)] + exchanged,
        scratch_shapes=[pltpu.VMEM((GPS, SSM_STATE, GW), F32)] + sems,
        compiler_params=_params(*(("arbitrary",) * 3 if riders else ("parallel", "parallel", "arbitrary"))),
    )(xbc, xbc, xbc, bt, ct, dtr, dtt, pr_rows, pr_cols, *[x for x, _ in riders])
    return outs[0], outs[1], list(outs[2:])


def _ssd_bwd(xbc, bt, ct, dtr, dtt, pr_rows, pr_cols, dvec, hs, y2, dy, *, name, riders=()):
    s = xbc.shape[0]
    nc = s // T
    chunk = lambda d, c: jnp.where(d == 0, nc - 1 - c, c)
    in_specs = _ssd_in_specs(chunk) + [
        pl.BlockSpec((1, GPS * GW), lambda d, g, c: (0, g)),
        pl.BlockSpec((None, None, GPS, SSM_STATE, GW), lambda d, g, c: (d, chunk(d, c), g, 0, 0)),
        pl.BlockSpec((None, T, GPS * GW), lambda d, g, c: (d, chunk(d, c), g)),
        pl.BlockSpec((T, GPS * GW), lambda d, g, c: (chunk(d, c), g)),
    ]

    def body(x_ref, b_ref, c_ref, bt_ref, ct_ref, dt_ref, dtt_ref, prr_ref, prc_ref, dvec_ref, hs_ref, y_ref, dy_ref,
             dxs_ref, db_ref, dc_ref, ddt_ref, dalog_ref, dbias_ref, g_ref):
        @pl.when(pl.program_id(2) == 0)
        def _():
            g_ref[...] = jnp.zeros_like(g_ref)
            dalog_ref[...] = jnp.zeros_like(dalog_ref)
            dbias_ref[...] = jnp.zeros_like(dbias_ref)

        for gi in range(GPS):
            group(*_group_refs(gi, [x_ref, dvec_ref, y_ref, dy_ref, dxs_ref], [b_ref, c_ref, db_ref, dc_ref], [bt_ref, ct_ref],
                               [dt_ref, dtt_ref, prr_ref, prc_ref, hs_ref, ddt_ref, dalog_ref, dbias_ref, g_ref]))

    def group(x_ref, dvec_ref, y_ref, dy_ref, dxs_ref, b_ref, c_ref, db_ref, dc_ref, bt_ref, ct_ref,
              dt_ref, dtt_ref, prr_ref, prc_ref, hs_ref, ddt_ref, dalog_ref, dbias_ref, g_ref):
        d = pl.program_id(0)
        ck = _ssd_chunk(d, dt_ref, dtt_ref, prr_ref, prc_ref)
        h0 = ck["head0"]
        xs = x_ref[...].astype(F32)
        bm, cm = b_ref[...], c_ref[...]
        hst = hs_ref[...]
        gst = g_ref[...]
        dyv = dy_ref[...]
        yv = y_ref[...]
        dv = dvec_ref[...] * (1 - d).astype(F32)
        cb = lax.dot_general(cm, bm, _DIMS["nt"], preferred_element_type=F32)
        cbt = jnp.dot(bm, ct_ref[...], preferred_element_type=F32)
        ch = jnp.dot(cm, hst.astype(BF16), preferred_element_type=F32)
        bg = jnp.dot(bm, gst.astype(BF16), preferred_element_type=F32)
        hg_cols = jnp.sum(hst * gst, axis=0, keepdims=True)
        lane16 = lax.broadcasted_iota(jnp.int32, (T, 2 * HPG), 1)
        sub8 = lax.broadcasted_iota(jnp.int32, (HPG, 1), 0)
        dcb = jnp.zeros((T, T), F32)
        acc16 = jnp.zeros((T, 2 * HPG), F32)
        dtot = jnp.zeros((HPG, 1), F32)
        dxss, dyes, xgds, etots = [], [], [], []
        for q in range(PAIRS):
            sl = slice(q * 128, (q + 1) * 128)
            pr = _ssd_pair(ck, q)
            xsp, dyp = xs[:, sl], dyv[:, sl]
            xg = xsp * pr["dt"]
            xgb = xg.astype(BF16)
            dyb = dyp.astype(BF16)
            dy0, dy1 = _split_heads(h0, dyb)
            dcb = dcb + (lax.dot_general(dy0, xgb, _DIMS["nt"], preferred_element_type=F32) * pr["lm"][0]
                         + lax.dot_general(dy1, xgb, _DIMS["nt"], preferred_element_type=F32) * pr["lm"][1])
            dxg_in = (jnp.dot((cbt * pr["lmt"][0]).astype(BF16), dy0, preferred_element_type=F32)
                      + jnp.dot((cbt * pr["lmt"][1]).astype(BF16), dy1, preferred_element_type=F32))
            xgd = xg * pr["decay"]
            xb = xgd * bg[:, sl]
            dxg = dxg_in + pr["decay"] * bg[:, sl]
            yo = pr["ec"] * ch[:, sl]
            dac = dyb.astype(F32) * (yv[:, sl] - yo) + dyp * yo - xgb.astype(F32) * dxg_in - xb
            d_0, d_1 = _split_heads(h0, dac)
            r_0, r_1 = _split_heads(h0, dxg * xsp)
            for hh, (d_h, r_h) in enumerate(((d_0, r_0), (d_1, r_1))):
                j = 2 * q + hh
                acc16 = (acc16 + jnp.where(lane16 == j, jnp.sum(d_h, axis=-1, keepdims=True), 0.0)
                         + jnp.where(lane16 == HPG + j, jnp.sum(r_h, axis=-1, keepdims=True), 0.0))
            tcols = pr["etot"] * hg_cols[:, sl] + jnp.sum(xb, axis=0, keepdims=True)
            t0, t1 = _split_heads(h0[0:1], tcols)
            dtot = (dtot + jnp.where(sub8 == 2 * q, jnp.sum(t0, axis=-1, keepdims=True), 0.0)
                    + jnp.where(sub8 == 2 * q + 1, jnp.sum(t1, axis=-1, keepdims=True), 0.0))
            dxss.append(dxg * pr["dt"] + dv[:, sl] * dyp)
            dyes.append(dyp * pr["ec"])
            xgds.append(xgd)
            etots.append(pr["etot"])
        acc_t = acc16.T
        da_lanes = _scan_lanes(acc_t[0:HPG], d != 0)[0] + dtot
        ddt = da_lanes * ck["a_c"] + acc_t[HPG:2 * HPG]
        draw = ddt * _sigmoid(ck["raw_lanes"])
        ddt_ref[...] = draw
        dbias_ref[...] += jnp.sum(draw, axis=-1, keepdims=True)
        dalog_ref[...] += jnp.sum(da_lanes * ck["dt_lanes"], axis=-1, keepdims=True) * ck["a_c"]
        dxs_ref[...] = jnp.concatenate(dxss, axis=1)
        dye = jnp.concatenate(dyes, axis=1).astype(BF16)
        xgd_all = jnp.concatenate(xgds, axis=1).astype(BF16)
        dcbb = dcb.astype(BF16)
        dc_ref[...] = (jnp.dot(dcbb, bm, preferred_element_type=F32)
                       + lax.dot_general(dye, hst.astype(BF16), _DIMS["nt"], preferred_element_type=F32))
        db_ref[...] = (lax.dot_general(dcbb, cm, _DIMS["tn"], preferred_element_type=F32)
                       + lax.dot_general(xgd_all, gst.astype(BF16), _DIMS["nt"], preferred_element_type=F32))
        g_ref[...] = jnp.dot(ct_ref[...], dye, preferred_element_type=F32) + gst * jnp.concatenate(etots, axis=1)

    small = pl.BlockSpec((None, GPS, HPG, 1), lambda d, g, c: (d, g, 0, 0))
    sshape = jax.ShapeDtypeStruct((2, SSM_GROUPS, HPG, 1), F32)
    grid = (2, GSTEPS, nc)
    anywhere, exchanged, sems = _rider_specs(riders)
    outs = pl.pallas_call(
        _riding(body, riders, 13, 6, grid), name=name, grid=grid, in_specs=in_specs + anywhere,
        out_specs=[pl.BlockSpec((None, T, GPS * GW), lambda d, g, c: (d, chunk(d, c), g)),
                   pl.BlockSpec((None, T, GPS * SSM_STATE), lambda d, g, c: (d, chunk(d, c), g)),
                   pl.BlockSpec((None, T, GPS * SSM_STATE), lambda d, g, c: (d, chunk(d, c), g)),
                   pl.BlockSpec((None, GPS, HPG, T), lambda d, g, c: (d, g, 0, chunk(d, c))), small, small] + anywhere,
        out_shape=[jax.ShapeDtypeStruct((2, s, SSM_INNER), F32),
                   jax.ShapeDtypeStruct((2, s, SSM_GROUPS * SSM_STATE), F32),
                   jax.ShapeDtypeStruct((2, s, SSM_GROUPS * SSM_STATE), F32),
                   jax.ShapeDtypeStruct((2, SSM_GROUPS, HPG, s), F32), sshape, sshape] + exchanged,
        scratch_shapes=[pltpu.VMEM((GPS, SSM_STATE, GW), F32)] + sems,
        compiler_params=_params(*(("arbitrary",) * 3 if riders else ("parallel", "parallel", "arbitrary"))),
    )(xbc, xbc, xbc, bt, ct, dtr, dtt, pr_rows, pr_cols, dvec, hs, y2, dy, *[x for x, _ in riders])
    return (*outs[:6], list(outs[6:]))


def _gate_norm_fwd(y2, xbc, proj, dvec, nw, *, name):
    s = xbc.shape[0]
    tr = 256

    def body(y_ref, xs_ref, z_ref, dv_ref, w_ref, u_ref):
        yt = y_ref[0] + y_ref[1] + dv_ref[...] * xs_ref[...].astype(F32)
        yg = yt * _silu_and_grad(z_ref[...].astype(F32))[0]
        u_ref[...] = (yg * lax.rsqrt(jnp.mean(yg * yg, axis=-1, keepdims=True) + RMS_EPS) * w_ref[...]).astype(BF16)

    row = pl.BlockSpec((tr, SSM_INNER), lambda i: (i, 0))
    vec = pl.BlockSpec((1, SSM_INNER), lambda i: (0, 0))
    return pl.pallas_call(
        body, name=name, grid=(s // tr,),
        in_specs=[pl.BlockSpec((2, tr, SSM_INNER), lambda i: (0, i, 0)), row, row, vec, vec], out_specs=row,
        out_shape=jax.ShapeDtypeStruct((s, SSM_INNER), BF16), compiler_params=_params("parallel"),
    )(y2, xbc, proj, dvec, nw)


def _gate_norm_bwd(du, y2, xbc, proj, dvec, nw, *, name):
    s = xbc.shape[0]
    tr = 256

    def body(du_ref, y_ref, xs_ref, z_ref, dv_ref, w_ref, dy_ref, dz_ref, dw_ref, dd_ref):
        @pl.when(pl.program_id(0) == 0)
        def _():
            dw_ref[...] = jnp.zeros_like(dw_ref)
            dd_ref[...] = jnp.zeros_like(dd_ref)

        xs = xs_ref[...].astype(F32)
        yt = y_ref[0] + y_ref[1] + dv_ref[...] * xs
        si, dsi = _silu_and_grad(z_ref[...].astype(F32))
        yg = yt * si
        rstd = lax.rsqrt(jnp.mean(yg * yg, axis=-1, keepdims=True) + RMS_EPS)
        yhat = yg * rstd
        du = du_ref[...]
        dyn = du * w_ref[...]
        dyg = rstd * (dyn - yhat * jnp.mean(dyn * yhat, axis=-1, keepdims=True))
        dyt = dyg * si
        dy_ref[...] = dyt
        dz_ref[...] = (dyg * yt * dsi).astype(BF16)
        dw_ref[...] += jnp.sum(du * yhat, axis=0, keepdims=True)
        dd_ref[...] += jnp.sum(dyt * xs, axis=0, keepdims=True)

    row = pl.BlockSpec((tr, SSM_INNER), lambda i: (i, 0))
    vec = pl.BlockSpec((1, SSM_INNER), lambda i: (0, 0))
    vshape = jax.ShapeDtypeStruct((1, SSM_INNER), F32)
    return pl.pallas_call(
        body, name=name, grid=(s // tr,),
        in_specs=[row, pl.BlockSpec((2, tr, SSM_INNER), lambda i: (0, i, 0)), row, row, vec, vec],
        out_specs=[row, row, vec, vec],
        out_shape=[jax.ShapeDtypeStruct((s, SSM_INNER), F32), jax.ShapeDtypeStruct((s, SSM_INNER), BF16), vshape, vshape],
        compiler_params=_params("arbitrary"),
    )(du, y2, xbc, proj, dvec, nw)


def _with_riders(result, riders):
    return result if riders else (result, [])


def _layer_a_fwd(x, mod, w_in, w_out, ln_g, ln_b, tag, riders=()):
    shift, scale, gate = mod
    h = _modulate(x, scale, shift, name=f"{tag}_modulate")
    w0 = jnp.concatenate([w_in[:, :A_GROUP_COLS], w_in[:, 3 * A_GROUP_COLS:]], axis=1)
    projs = [_mm(h, w0, mode="nn", out_dtype=BF16, tm=1024, tn=1024, tk=1024, name=f"{tag}_mm_in0")]
    for grp in (1, 2):
        projs.append(_mm_dilated(h, w_in[:, grp * A_GROUP_COLS:(grp + 1) * A_GROUP_COLS], DILATIONS[grp],
                                 name=f"{tag}_mm_in{grp}"))
    ol, exchanged = [], []
    for grp in range(3):
        res = _attn_fwd(projs[grp], grp, name=f"{tag}_attn_fwd{grp}", riders=riders if grp == 0 else ())
        ol.extend(res[:2])
        exchanged.extend(res[2] if len(res) > 2 else [])
    y, o, lse = _attn_combine(*ol, projs[0], name=f"{tag}_combine")
    out = _mm(y, w_out, mode="nn", out_dtype=F32, tm=512, tn=1024, tk=1024, name=f"{tag}_mm_out")
    xn = _resid_ln_fwd(x, out, gate, ln_g, ln_b, name=f"{tag}_resid_ln")
    return xn, (x, h, projs, y, o, lse, out), exchanged


def _layer_a_bwd(dxn, saved, mod, w_in, w_out, ln_g, tag, riders=(), scatter_own=False):
    x, h, projs, y, o, lse, out = saved
    shift, scale, gate = mod
    dx_part, dout, dgate, dln_g, dln_b = _resid_ln_bwd(x, out, gate, ln_g, dxn, name=f"{tag}_resid_ln_bwd")
    dw_out = _mm(y, dout, mode="tn", out_dtype=F32, tm=1024, tn=1024, tk=512, name=f"{tag}_mm_dw_out")
    dy = _mm(dout, w_out, mode="nt", out_dtype=F32, tm=512, tn=1024, tk=1024, name=f"{tag}_mm_dy")
    dgp, do0, do1, o1, lse1, do2, o2, lse2 = _attn_combine_bwd(dy, o, lse, projs[0], name=f"{tag}_combine_bwd")
    parts = [_attn_bwd(projs[grp], grp, *dol, name=f"{tag}_attn_bwd{grp}")
             for grp, dol in enumerate(((do0, o, lse), (do1, o1, lse1), (do2, o2, lse2)))]
    dproj = _assemble_dproj(*parts, dgp, name=f"{tag}_assemble_dproj")
    dw_in, exchanged = _with_riders(_mm(h.T, dproj, mode="nn", out_dtype=F32, tm=1024, tn=1024, tk=1024,
                                        name=f"{tag}_mm_dw_in", riders=riders), riders)
    own = ((_col_blocks(dw_in), False), (_row_blocks(dw_out), False)) if scatter_own else ()
    res = _mm_dh(dproj, w_in, dx_part, x, scale, tm=512, tk=2048, name=f"{tag}_mm_dh", riders=own)
    dx, dscale, dshift = res[:3]
    grads = dict(w_in=dw_in, w_out=dw_out, ln_g=dln_g, ln_b=dln_b, mod=jnp.concatenate([dshift, dscale, dgate], axis=1))
    return dx, grads, exchanged, (res[3] if scatter_own else None)


def _ssd_param_views(dt_raw, dt_bias, a_log):
    s = dt_raw.shape[0]
    r4 = dt_raw.reshape(s, 2, SSM_GROUPS, HPG)
    dtr = r4.transpose(1, 2, 0, 3)
    dtt = r4.transpose(1, 2, 3, 0)
    a = -jnp.exp(a_log)
    pr_rows = jnp.stack([dt_bias.reshape(2, SSM_GROUPS, HPG), a.reshape(2, SSM_GROUPS, HPG)], axis=2)
    return dtr, dtt, pr_rows, pr_rows.transpose(0, 1, 3, 2)


def _layer_b_fwd(x, mod, w_in, w_out, p, ln_g, ln_b, tag, riders=()):
    shift, scale, gate = mod
    s = x.shape[0]
    h = _modulate(x, scale, shift, name=f"{tag}_modulate")
    proj = _mm(h, w_in[:, :SSM_MAIN_COLS], mode="nn", out_dtype=BF16, tm=512, tn=1024, tk=1024, name=f"{tag}_mm_in")
    dt_raw = _mm(h, w_in[:, SSM_MAIN_COLS:SSM_IN_COLS], mode="nn", out_dtype=F32, tm=512, tn=64, tk=1024,
                 name=f"{tag}_mm_dt")
    xpad = jnp.pad(proj[:, SSM_INNER:], ((CONV_HALO, CONV_HALO), (0, 0)))
    xbc = _conv_fwd(xpad, p["conv_w"], p["conv_b"], name=f"{tag}_conv")
    views = (xbc[:, SSM_INNER:SSM_INNER + SSM_GROUPS * SSM_STATE].T, xbc[:, SSM_INNER + SSM_GROUPS * SSM_STATE:].T,
             *_ssd_param_views(dt_raw, p["dt_bias"], p["a_log"]))
    y2, hs, exchanged = _ssd_fwd(xbc, *views, name=f"{tag}_ssd_fwd", riders=riders)
    u = _gate_norm_fwd(y2, xbc, proj, p["dvec"], p["norm_w"], name=f"{tag}_gate_norm")
    out = _mm(u, w_out, mode="nn", out_dtype=F32, tm=512, tn=1024, tk=2048, name=f"{tag}_mm_out")
    xn = _resid_ln_fwd(x, out, gate, ln_g, ln_b, name=f"{tag}_resid_ln")
    return xn, (x, h, proj, xpad, xbc, views, y2, hs, u, out), exchanged


def _layer_b_bwd(dxn, saved, mod, w_in, w_out, p, ln_g, tag, riders=()):
    x, h, proj, xpad, xbc, views, y2, hs, u, out = saved
    shift, scale, gate = mod
    s = x.shape[0]
    dx_part, dout, dgate, dln_g, dln_b = _resid_ln_bwd(x, out, gate, ln_g, dxn, name=f"{tag}_resid_ln_bwd")
    dw_out = _mm(u, dout, mode="tn", out_dtype=F32, tm=1024, tn=1024, tk=512, name=f"{tag}_mm_dw_out")
    du = _mm(dout, w_out, mode="nt", out_dtype=F32, tm=512, tn=1024, tk=1024, name=f"{tag}_mm_du")
    dy, dz, dnorm_w, dd_lanes = _gate_norm_bwd(du, y2, xbc, proj, p["dvec"], p["norm_w"], name=f"{tag}_gate_norm_bwd")
    dxs2, db2, dc2, ddt4, dalog, dbias, exchanged = _ssd_bwd(xbc, *views, p["dvec"], hs, y2, dy, name=f"{tag}_ssd_bwd",
                                                             riders=riders)
    dact = jnp.concatenate([dxs2[0] + dxs2[1], db2[0] + db2[1], dc2[0] + dc2[1]], axis=1)
    dapad = jnp.pad(dact, ((CONV_HALO, CONV_HALO), (0, 0)))
    dxbc, dconv_w, dconv_b = _conv_bwd(xpad, dapad, p["conv_w"], p["conv_b"], name=f"{tag}_conv_bwd")
    ddt_raw = ddt4.transpose(3, 0, 1, 2).reshape(s, 2 * SSM_HEADS).astype(BF16)
    dproj = jnp.concatenate([dz, dxbc, ddt_raw, jnp.zeros((s, SSM_PAD_COLS - SSM_IN_COLS), BF16)], axis=1)
    dw_in = _mm(h.T, dproj, mode="nn", out_dtype=F32, tm=1024, tn=896, tk=1024, name=f"{tag}_mm_dw_in")[:, :SSM_IN_COLS]
    w_pad = jnp.pad(w_in, ((0, 0), (0, SSM_PAD_COLS - SSM_IN_COLS)))
    dx, dscale, dshift = _mm_dh(dproj, w_pad, dx_part, x, scale, tm=512, tk=1792, name=f"{tag}_mm_dh")
    grads = dict(
        w_in=dw_in, w_out=dw_out, ln_g=dln_g, ln_b=dln_b, mod=jnp.concatenate([dshift, dscale, dgate], axis=1),
        conv_w=dconv_w, conv_b=dconv_b, norm_w=dnorm_w, dt_bias=dbias.reshape(2, SSM_HEADS),
        a_log=dalog.reshape(2, SSM_HEADS), d=jnp.sum(dd_lanes.reshape(SSM_HEADS, SSM_HEAD_DIM), axis=1))
    return dx, grads, exchanged


def _full_cols(g):
    return g.transpose(1, 0, 2).reshape(g.shape[1], -1)


def _full_rows(g):
    return g.reshape(-1, g.shape[2])


def _col_blocks(dw):
    r, c = dw.shape
    return dw.reshape(r, N_DEV, c // N_DEV).transpose(1, 0, 2).astype(BF16)


def _row_blocks(dw):
    r, c = dw.shape
    return dw.reshape(N_DEV, r // N_DEV, c).astype(BF16)


def _local_step(x, target, mods, ln_g, ln_b, layer_w, b_params, shards=None):
    layer_w = list(layer_w)
    saved = []
    for i in range(DEPTH):
        riders = ()
        if shards is not None and i + 1 < DEPTH:
            riders = ((shards[i + 1][0], True), (shards[i + 1][1], True))
        small = () if i % 2 == 0 else (b_params[i // 2],)
        fwd = _layer_a_fwd if i % 2 == 0 else _layer_b_fwd
        x, sv, got = fwd(x, mods[i], *layer_w[i], *small, ln_g[i:i + 1], ln_b[i:i + 1], f"l{i}", riders)
        if riders:
            layer_w.append((_full_cols(got[0]), _full_rows(got[1])))
        saved.append(sv)
    dx, loss = _loss_and_grad(x, target, name="loss")
    grads, received = [None] * DEPTH, [None] * DEPTH
    riders = ()
    for i in reversed(range(DEPTH)):
        small = () if i % 2 == 0 else (b_params[i // 2],)
        bwd = _layer_a_bwd if i % 2 == 0 else _layer_b_bwd
        last = (True,) if shards is not None and i == 0 else ()
        res = bwd(dx, saved[i], mods[i], *layer_w[i], *small, ln_g[i:i + 1], f"l{i}", riders, *last)
        dx, grads[i], got = res[:3]
        if riders:
            received[i + 1] = got
        if last:
            received[0] = res[3]
        if shards is not None:
            riders = ((_col_blocks(grads[i]["w_in"]), False), (_row_blocks(grads[i]["w_out"]), False))
    return loss, dx, grads, received


def _mesh_pos():
    return lax.axis_index("x"), lax.axis_index("y"), lax.axis_index("c")


def _all_gather(x, *, name):
    def body(x_ref, out_ref, send_sems, recv_sems, local_sem):
        ax, ay, ac = _mesh_pos()
        me, sibling = (ax, ay, ac), (ax, ay, 1 - ac)
        chips = [(1 - ax, ay), (ax, 1 - ay), (1 - ax, 1 - ay)]

        def slot(px, py, pc):
            return out_ref.at[4 * px + 2 * py + pc]

        def copy(k, block, to, src=None):
            return pltpu.make_async_remote_copy(
                src_ref=slot(*block) if src is None else src, dst_ref=slot(*block),
                send_sem=send_sems.at[k], recv_sem=recv_sems.at[k], device_id=to, device_id_type=MESH)

        mine = pltpu.make_async_copy(x_ref, slot(*me), local_sem)
        mine.start()
        first = [copy(0, me, sibling, src=x_ref)]
        first += [copy(1 + j, me, (*chip, ac), src=x_ref) for j, chip in enumerate(chips)]
        for cp in first:
            cp.start()
        passed = [copy(4 + j, (*chip, ac), sibling) for j, chip in enumerate(chips)]
        for j, chip in enumerate(chips):
            copy(1 + j, (*chip, ac), me).wait_recv()
            passed[j].start()
        copy(0, sibling, me).wait_recv()
        for j, chip in enumerate(chips):
            copy(4 + j, (*chip, 1 - ac), me).wait_recv()
        for cp in first + passed:
            cp.wait_send()
        mine.wait()

    return pl.pallas_call(
        body, name=name, out_shape=jax.ShapeDtypeStruct((N_DEV,) + x.shape, x.dtype),
        in_specs=[pl.BlockSpec(memory_space=pl.ANY)], out_specs=pl.BlockSpec(memory_space=pl.ANY),
        scratch_shapes=[pltpu.SemaphoreType.DMA((7,)), pltpu.SemaphoreType.DMA((7,)), pltpu.SemaphoreType.DMA],
    )(x)


def _all_to_all(x, *, name):
    def body(x_ref, out_ref, send_sems, recv_sems, local_sem):
        ax, ay, ac = _mesh_pos()
        me = 4 * ax + 2 * ay + ac
        mine = pltpu.make_async_copy(x_ref.at[me], out_ref.at[me], local_sem)
        mine.start()
        copies = []
        for k in range(1, N_DEV):
            px = 1 - ax if k & 4 else ax
            py = 1 - ay if k & 2 else ay
            pc = 1 - ac if k & 1 else ac
            copies.append(pltpu.make_async_remote_copy(
                src_ref=x_ref.at[4 * px + 2 * py + pc], dst_ref=out_ref.at[me],
                send_sem=send_sems.at[k - 1], recv_sem=recv_sems.at[k - 1], device_id=(px, py, pc), device_id_type=MESH))
        for cp in copies:
            cp.start()
        for cp in copies:
            cp.wait()
        mine.wait()

    return pl.pallas_call(
        body, name=name, out_shape=jax.ShapeDtypeStruct(x.shape, x.dtype),
        in_specs=[pl.BlockSpec(memory_space=pl.ANY)], out_specs=pl.BlockSpec(memory_space=pl.ANY),
        scratch_shapes=[pltpu.SemaphoreType.DMA((7,)), pltpu.SemaphoreType.DMA((7,)), pltpu.SemaphoreType.DMA],
    )(x)


ADA_LOCAL = 3 * D_MODEL // N_DEV


def _ada_mod(c_all, ada_w, ada_b_local, *, name):
    def body(c_ref, w_ref, b_ref, o_ref):
        cond = _silu_and_grad(c_ref[...])[0]
        o_ref[...] = jnp.dot(cond, w_ref[...], precision=HIGHEST, preferred_element_type=F32) + b_ref[...]

    return pl.pallas_call(
        body, name=name, grid=(DEPTH,),
        in_specs=[pl.BlockSpec((N_DEV, D_MODEL), lambda i: (0, 0)), pl.BlockSpec((None, D_MODEL, ADA_LOCAL), lambda i: (i, 0, 0)),
                  pl.BlockSpec((None, 1, ADA_LOCAL), lambda i: (i, 0, 0))],
        out_specs=pl.BlockSpec((None, N_DEV, ADA_LOCAL), lambda i: (i, 0, 0)),
        out_shape=jax.ShapeDtypeStruct((DEPTH, N_DEV, ADA_LOCAL), F32), compiler_params=_params("parallel"),
    )(c_all, ada_w, ada_b_local)


def _ada_grad(c_all_t, dmod_local, *, name):
    def body(ct_ref, dm_ref, o_ref):
        cond_t = _silu_and_grad(ct_ref[...])[0]
        dm = dm_ref[...]
        acc = cond_t[:, 0:1] * dm[0:1, :]
        for smp in range(1, N_DEV):
            acc = acc + cond_t[:, smp:smp + 1] * dm[smp:smp + 1, :]
        o_ref[...] = acc

    return pl.pallas_call(
        body, name=name, grid=(DEPTH,),
        in_specs=[pl.BlockSpec((D_MODEL, N_DEV), lambda i: (0, 0)), pl.BlockSpec((None, N_DEV, ADA_LOCAL), lambda i: (i, 0, 0))],
        out_specs=pl.BlockSpec((None, D_MODEL, ADA_LOCAL), lambda i: (i, 0, 0)),
        out_shape=jax.ShapeDtypeStruct((DEPTH, D_MODEL, ADA_LOCAL), F32), compiler_params=_params("parallel"),
    )(c_all_t, dmod_local)


def _sum_devices(parts, *, name):
    n = parts.shape[1]

    def body(p_ref, o_ref):
        acc = p_ref[0:1, :]
        for dev in range(1, N_DEV):
            acc = acc + p_ref[dev:dev + 1, :]
        o_ref[...] = acc

    return pl.pallas_call(
        body, name=name, out_shape=jax.ShapeDtypeStruct((1, n), F32),
        in_specs=[pl.BlockSpec(memory_space=pltpu.VMEM)], out_specs=pl.BlockSpec(memory_space=pltpu.VMEM),
        compiler_params=pltpu.CompilerParams(vmem_limit_bytes=VMEM_LIMIT_BYTES),
    )(parts)


ADAMW_VMEM_BYTES = 24 * 1024 * 1024


def _adamw(w, m, v, g, *, name):
    r, c = w.shape
    summed = g.ndim == 3
    tr = r
    arrays = 7 + (N_DEV if summed else 1)
    while tr % 16 == 0 and 2 * arrays * tr * c * 4 > ADAMW_VMEM_BYTES:
        tr //= 2

    def body(w_ref, m_ref, v_ref, g_ref, go_ref, d_ref, mo_ref, vo_ref):
        if summed:
            g = g_ref[0].astype(F32)
            for dev in range(1, N_DEV):
                g = g + g_ref[dev].astype(F32)
        else:
            g = g_ref[...]
        mn = ADAM_B1 * m_ref[...] + (1.0 - ADAM_B1) * g
        vn = ADAM_B2 * v_ref[...] + (1.0 - ADAM_B2) * (g * g)
        m_hat = mn / (1.0 - ADAM_B1 ** ADAM_STEP)
        v_hat = vn / (1.0 - ADAM_B2 ** ADAM_STEP)
        go_ref[...] = g
        d_ref[...] = -ADAM_LR * (m_hat / (jnp.sqrt(v_hat) + ADAM_EPS) + ADAM_WD * w_ref[...])
        mo_ref[...] = mn
        vo_ref[...] = vn

    row = pl.BlockSpec((tr, c), lambda i: (i, 0))
    gspec = pl.BlockSpec((N_DEV, tr, c), lambda i: (0, i, 0)) if summed else row
    shp = jax.ShapeDtypeStruct((r, c), F32)
    return pl.pallas_call(
        body, name=name, grid=(r // tr,), in_specs=[row, row, row, gspec], out_specs=[row] * 4, out_shape=[shp] * 4,
        compiler_params=_params("parallel"),
    )(w, m, v, g)


def _pack(arrays):
    flat = jnp.concatenate([a.reshape(-1) for a in arrays])
    n = flat.shape[0]
    return jnp.pad(flat, (0, -n % 128)).reshape(1, -1)


def _unpack(vec, shapes):
    out, at = [], 0
    for shp in shapes:
        n = math.prod(shp)
        out.append(vec[at:at + n].reshape(shp))
        at += n
    return out


def _unpack_rows(rows, shapes):
    out, at = [], 0
    for shp in shapes:
        n = math.prod(shp)
        out.append(rows[:, at:at + n].reshape((rows.shape[0],) + tuple(shp)))
        at += n
    return out


def _my_shard(full, me, axis):
    width = full.shape[axis] // N_DEV
    return lax.dynamic_slice_in_dim(full, me * width, width, axis)


def _gather_cols(g, lead):
    nd = g.ndim
    perm = tuple(range(1, nd - 1)) + (0, nd - 1)
    t = g.transpose(perm)
    return t.reshape(t.shape[:-2] + (t.shape[-2] * t.shape[-1],))


def kernel(x, c, ada_w, ada_b, ln_g, ln_b, a_w_in, a_w_out, b_w_in, b_conv_w, b_conv_b, b_dt_bias, b_a_log, b_d, b_norm_w, b_w_out, loss_target, m_ada_w, m_ada_b, m_ln_g, m_ln_b, m_a_w_in, m_a_w_out, m_b_w_in, m_b_conv_w, m_b_conv_b, m_b_dt_bias, m_b_a_log, m_b_d, m_b_norm_w, m_b_w_out, v_ada_w, v_ada_b, v_ln_g, v_ln_b, v_a_w_in, v_a_w_out, v_b_w_in, v_b_conv_w, v_b_conv_b, v_b_dt_bias, v_b_a_log, v_b_d, v_b_norm_w, v_b_w_out):
    ax, ay, ac = _mesh_pos()
    me = 4 * ax + 2 * ay + ac
    seq = x.shape[1]

    small_shapes = [(1, D_MODEL), (2, SSM_CONV, ADA_LOCAL), (2, ADA_LOCAL), (2, SSM_INNER // N_DEV)]
    sg = _all_gather(_pack([c, b_conv_w, b_conv_b, b_norm_w]), name="gather_small")[:, 0, :]
    c_all, conv_w_g, conv_b_g, norm_w_g = _unpack_rows(sg, small_shapes)
    c_all = c_all[:, 0, :]
    conv_w = _gather_cols(conv_w_g, 2)
    conv_b = _gather_cols(conv_b_g[:, :, None, :], 2)
    norm_w = _gather_cols(norm_w_g[:, :, None, :], 2)

    shards = [(w_in[i // 2].astype(BF16), w_out[i // 2].astype(BF16))
              for i, (w_in, w_out) in enumerate(((a_w_in, a_w_out), (b_w_in, b_w_out)) * 2)]
    layer0_w = (_full_cols(_all_gather(shards[0][0], name="gather_w_in0")),
                _full_rows(_all_gather(shards[0][1], name="gather_w_out0")))

    ada_b_local = _my_shard(ada_b, me, 1)[:, None, :]
    mod_cols = _ada_mod(c_all, ada_w, ada_b_local, name="ada_mod")
    mod_g = _all_gather(mod_cols.reshape(1, -1), name="gather_mod").reshape(N_DEV, DEPTH, N_DEV, ADA_LOCAL)
    mod = lax.dynamic_index_in_dim(mod_g, me, axis=2, keepdims=False).transpose(1, 0, 2).reshape(DEPTH, 3 * D_MODEL)
    mods = [tuple(mod[i:i + 1, k * D_MODEL:(k + 1) * D_MODEL] for k in range(3)) for i in range(DEPTH)]

    b_params = [dict(conv_w=conv_w[j], conv_b=conv_b[j], norm_w=norm_w[j],
                     dt_bias=b_dt_bias[j], a_log=b_a_log[j], dvec=jnp.repeat(b_d[j], SSM_HEAD_DIM)[None, :])
                for j in range(2)]
    loss_lanes, dx, grads, received = _local_step(x[0], loss_target[0], mods, ln_g, ln_b, [layer0_w], b_params, shards)
    loss = lax.psum(loss_lanes[0, 0], ("x", "y", "c"))
    grad_x = dx[None]

    a_layers, b_layers = (grads[0], grads[2]), (grads[1], grads[3])
    part_shapes = [(DEPTH, 3 * D_MODEL), (DEPTH, D_MODEL), (DEPTH, D_MODEL), (2, SSM_CONV, SSM_CONV_DIM),
                   (2, SSM_CONV_DIM), (2, SSM_INNER), (2, 2, SSM_HEADS), (2, 2, SSM_HEADS), (2, SSM_HEADS)]
    parts = _pack([
        jnp.concatenate([g["mod"] for g in grads]), jnp.concatenate([g["ln_g"] for g in grads]),
        jnp.concatenate([g["ln_b"] for g in grads]), jnp.stack([g["conv_w"] for g in b_layers]),
        jnp.stack([g["conv_b"][0] for g in b_layers]), jnp.stack([g["norm_w"][0] for g in b_layers]),
        jnp.stack([g["dt_bias"] for g in b_layers]), jnp.stack([g["a_log"] for g in b_layers]),
        jnp.stack([g["d"] for g in b_layers])])
    parts_g = _all_gather(parts, name="gather_small_grads")[:, 0, :]
    (g_ada_b, g_ln_g, g_ln_b, g_conv_w, g_conv_b, g_norm_w, g_dt_bias, g_a_log, g_d) = _unpack(
        _sum_devices(parts_g, name="sum_small_grads")[0], part_shapes)
    dmod_all = parts_g[:, :DEPTH * 3 * D_MODEL].reshape(N_DEV, DEPTH, N_DEV, ADA_LOCAL)
    dmod_local = lax.dynamic_index_in_dim(dmod_all, me, axis=2, keepdims=False).transpose(1, 0, 2)
    g_ada_w = _ada_grad(c_all.T, dmod_local, name="ada_grad")

    r_a_w_in = jnp.concatenate([received[0][0], received[2][0]], axis=1)
    r_a_w_out = jnp.concatenate([received[0][1], received[2][1]], axis=1)
    r_b_w_in = jnp.concatenate([received[1][0], received[3][0]], axis=1)
    r_b_w_out = jnp.concatenate([received[1][1], received[3][1]], axis=1)

    def update(w, m, v, g, name):
        two_d = (-1, w.shape[-1])
        outs = _adamw(w.reshape(two_d), m.reshape(two_d), v.reshape(two_d), g, name=name)
        return [o.reshape(w.shape) for o in outs]

    up_ada_w = update(ada_w, m_ada_w, v_ada_w, g_ada_w.reshape(-1, ADA_LOCAL), "adamw_ada_w")
    up_a_w_in = update(a_w_in, m_a_w_in, v_a_w_in, r_a_w_in, "adamw_a_w_in")
    up_a_w_out = update(a_w_out, m_a_w_out, v_a_w_out, r_a_w_out, "adamw_a_w_out")
    up_b_w_in = update(b_w_in, m_b_w_in, v_b_w_in, r_b_w_in, "adamw_b_w_in")
    up_b_w_out = update(b_w_out, m_b_w_out, v_b_w_out, r_b_w_out, "adamw_b_w_out")

    small_w = [ada_b, ln_g, ln_b, b_conv_w, b_conv_b, b_dt_bias, b_a_log, b_d, b_norm_w]
    small_m = [m_ada_b, m_ln_g, m_ln_b, m_b_conv_w, m_b_conv_b, m_b_dt_bias, m_b_a_log, m_b_d, m_b_norm_w]
    small_v = [v_ada_b, v_ln_g, v_ln_b, v_b_conv_w, v_b_conv_b, v_b_dt_bias, v_b_a_log, v_b_d, v_b_norm_w]
    small_g = [g_ada_b, g_ln_g, g_ln_b, _my_shard(g_conv_w, me, 2), _my_shard(g_conv_b, me, 1), g_dt_bias, g_a_log, g_d,
               _my_shard(g_norm_w, me, 1)]
    shapes = [w.shape for w in small_w]
    packed = _adamw(_pack(small_w), _pack(small_m), _pack(small_v), _pack(small_g), name="adamw_small")
    (up_ada_b, up_ln_g, up_ln_b, up_conv_w, up_conv_b, up_dt_bias, up_a_log, up_d, up_norm_w) = zip(
        *[_unpack(p[0], shapes) for p in packed])

    ordered = [up_ada_w, up_ada_b, up_ln_g, up_ln_b, up_a_w_in, up_a_w_out, up_b_w_in, up_conv_w, up_conv_b,
               up_dt_bias, up_a_log, up_d, up_norm_w, up_b_w_out]
    return (loss, grad_x, *[u[0] for u in ordered], *[u[1] for u in ordered], *[u[2] for u in ordered],
            *[u[3] for u in ordered])
```

```python
import functools
import math

import jax
import jax.numpy as jnp
import numpy as np
from jax import lax
from jax.experimental import pallas as pl
from jax.experimental.pallas import tpu as pltpu

F32 = jnp.float32
BF16 = jnp.bfloat16
HIGHEST = lax.Precision.HIGHEST
MESH = pl.DeviceIdType.MESH

D_MODEL = 1024
DEPTH = 4
A_HEADS = 16
A_HEAD_DIM = 64
A_WIDTH = 1024
DILATIONS = (1, 4, 16)
A_RADIUS = 64
A_QBLOCK = 128
A_IN_COLS = 10240
SSM_INNER = 2048
SSM_HEADS = 32
SSM_HEAD_DIM = 64
SSM_STATE = 128
SSM_GROUPS = 4
SSM_CHUNK = 128
SSM_CONV = 5
SSM_CONV_DIM = 3072
SSM_IN_COLS = 5184
SSM_MAIN_COLS = 5120
SSM_PAD_COLS = 5376
CONV_HALO = 16
ALPHA = (2 * DEPTH) ** 0.25
LN_EPS = 1e-5
RMS_EPS = 1e-5
ADAM_LR, ADAM_B1, ADAM_B2, ADAM_EPS, ADAM_WD, ADAM_STEP = 0.001, 0.9, 0.999, 1e-08, 0.01, 10
N_DEV = 8
VMEM_LIMIT_BYTES = 56 * 1024 * 1024
NEG_BIG = -1e30


def _params(*sem):
    return pltpu.CompilerParams(dimension_semantics=sem, vmem_limit_bytes=VMEM_LIMIT_BYTES)


def _sigmoid(x):
    return 1.0 / (1.0 + jnp.exp(-x))


def _silu_and_grad(x):
    sg = _sigmoid(x)
    return x * sg, sg * (1.0 + x * (1.0 - sg))


def _softplus(x):
    e = jnp.exp(-jnp.abs(x))
    u = 1.0 + e
    log1p = jnp.where(u == 1.0, e, jnp.log(u) * (e / jnp.where(u == 1.0, 1.0, u - 1.0)))
    return jnp.maximum(x, 0.0) + log1p


_DIMS = {"nn": (((1,), (0,)), ((), ())), "nt": (((1,), (1,)), ((), ())), "tn": (((0,), (0,)), ((), ()))}


def _exchange_copies(x_ref, out_ref, send_sems, recv_sems, local_sem, gather):
    ax, ay, ac = lax.axis_index("x"), lax.axis_index("y"), lax.axis_index("c")
    me = 4 * ax + 2 * ay + ac
    copies = [pltpu.make_async_copy(x_ref if gather else x_ref.at[me], out_ref.at[me], local_sem)]
    for k in range(1, N_DEV):
        px = 1 - ax if k & 4 else ax
        py = 1 - ay if k & 2 else ay
        pc = 1 - ac if k & 1 else ac
        copies.append(pltpu.make_async_remote_copy(
            src_ref=x_ref if gather else x_ref.at[4 * px + 2 * py + pc], dst_ref=out_ref.at[me],
            send_sem=send_sems.at[k - 1], recv_sem=recv_sems.at[k - 1], device_id=(px, py, pc), device_id_type=MESH))
    return copies


def _rider_specs(riders):
    anywhere = [pl.BlockSpec(memory_space=pl.ANY)] * len(riders)
    shapes = [jax.ShapeDtypeStruct(((N_DEV,) + x.shape) if gather else x.shape, x.dtype) for x, gather in riders]
    sems = [pltpu.SemaphoreType.DMA((N_DEV - 1,)), pltpu.SemaphoreType.DMA((N_DEV - 1,)), pltpu.SemaphoreType.DMA]
    return anywhere, shapes, sems * len(riders)


def _riding(body, riders, n_in, n_out, grid):
    nr = len(riders)
    if not nr:
        return body

    def wrapped(*refs):
        ins, xs = refs[:n_in], refs[n_in:n_in + nr]
        outs, ys = refs[n_in + nr:n_in + nr + n_out], refs[n_in + nr + n_out:n_in + 2 * nr + n_out]
        scratch = refs[n_in + 2 * nr + n_out:]
        own, sems = scratch[:len(scratch) - 3 * nr], scratch[len(scratch) - 3 * nr:]
        ids = [pl.program_id(ax) for ax in range(len(grid))]

        def at(steps):
            cond = ids[0] == steps[0]
            for i, st in zip(ids[1:], steps[1:]):
                cond = jnp.logical_and(cond, i == st)
            return cond

        def exchanges():
            return [_exchange_copies(xs[r], ys[r], *sems[3 * r:3 * r + 3], riders[r][1]) for r in range(nr)]

        @pl.when(at([0] * len(grid)))
        def _():
            for copies in exchanges():
                for cp in copies:
                    cp.start()

        body(*ins, *outs, *own)

        @pl.when(at([g - 1 for g in grid]))
        def _():
            for copies in exchanges():
                for cp in copies:
                    cp.wait()

    return wrapped


def _mm(a, b, *, mode, out_dtype, tm, tn, tk, name, riders=()):
    if mode == "nn":
        (m, k), (_, n) = a.shape, b.shape
    elif mode == "nt":
        (m, k), (n, _) = a.shape, b.shape
    else:
        (k, m), (_, n) = a.shape, b.shape
    tm, tn, tk = min(tm, m), min(tn, n), min(tk, k)
    assert m % tm == 0 and n % tn == 0 and k % tk == 0, (name, a.shape, b.shape)
    nk = k // tk
    dims = _DIMS[mode]
    nr = len(riders)
    grid = (m // tm, n // tn, nk)

    def body(a_ref, b_ref, *rest):
        xs, o_ref, ys, scratch = rest[:nr], rest[nr], rest[nr + 1:2 * nr + 1], rest[2 * nr + 1:]
        sems = scratch[len(scratch) - 3 * nr:]
        ids = [pl.program_id(ax) for ax in range(3)]

        def exchanges():
            return [_exchange_copies(xs[r], ys[r], *sems[3 * r:3 * r + 3], riders[r][1]) for r in range(nr)]

        if nr:
            @pl.when(jnp.logical_and(jnp.logical_and(ids[0] == 0, ids[1] == 0), ids[2] == 0))
            def _():
                for copies in exchanges():
                    for cp in copies:
                        cp.start()

        part = lax.dot_general(a_ref[...], b_ref[...], dims, preferred_element_type=F32)
        if nk == 1:
            o_ref[...] = part.astype(o_ref.dtype)
        else:
            acc_ref = scratch[0]
            kk = ids[2]

            @pl.when(kk == 0)
            def _():
                acc_ref[...] = part

            @pl.when(kk > 0)
            def _():
                acc_ref[...] += part

            @pl.when(kk == nk - 1)
            def _():
                o_ref[...] = acc_ref[...].astype(o_ref.dtype)

        if nr:
            @pl.when(jnp.logical_and(jnp.logical_and(ids[0] == grid[0] - 1, ids[1] == grid[1] - 1), ids[2] == grid[2] - 1))
            def _():
                for copies in exchanges():
                    for cp in copies:
                        cp.wait()

    if mode == "tn":
        a_spec = pl.BlockSpec((tk, tm), lambda i, j, kk: (kk, i))
    else:
        a_spec = pl.BlockSpec((tm, tk), lambda i, j, kk: (i, kk))
    if mode == "nt":
        b_spec = pl.BlockSpec((tn, tk), lambda i, j, kk: (j, kk))
    else:
        b_spec = pl.BlockSpec((tk, tn), lambda i, j, kk: (kk, j))
    anywhere = pl.BlockSpec(memory_space=pl.ANY)
    exchanged = [jax.ShapeDtypeStruct(((N_DEV,) + x.shape) if gather else x.shape, x.dtype) for x, gather in riders]
    sems = [pltpu.SemaphoreType.DMA((N_DEV - 1,)), pltpu.SemaphoreType.DMA((N_DEV - 1,)), pltpu.SemaphoreType.DMA] * nr
    outs = pl.pallas_call(
        body, name=name, grid=grid,
        in_specs=[a_spec, b_spec] + [anywhere] * nr,
        out_specs=[pl.BlockSpec((tm, tn), lambda i, j, kk: (i, j))] + [anywhere] * nr,
        out_shape=[jax.ShapeDtypeStruct((m, n), out_dtype)] + exchanged,
        scratch_shapes=([] if nk == 1 else [pltpu.VMEM((tm, tn), F32)]) + sems,
        compiler_params=_params(*(("arbitrary",) * 3 if nr else ("parallel", "parallel", "arbitrary"))),
    )(a, b, *[x for x, _ in riders])
    return (outs[0], list(outs[1:])) if nr else outs[0]


def _mm_dh(dproj, w, dx_part, x, scale, *, tm, tk, name, riders=()):
    s, k = dproj.shape
    d = w.shape[0]
    tk = min(tk, k)
    assert s % tm == 0 and k % tk == 0
    nk = k // tk

    def body(a_ref, w_ref, dxp_ref, x_ref, sc_ref, dx_ref, dsc_ref, dsh_ref, acc_ref):
        i, kk = pl.program_id(0), pl.program_id(1)
        part = lax.dot_general(a_ref[...], w_ref[...], _DIMS["nt"], preferred_element_type=F32)

        @pl.when(kk == 0)
        def _():
            acc_ref[...] = part

        @pl.when(kk > 0)
        def _():
            acc_ref[...] += part

        @pl.when(jnp.logical_and(i == 0, kk == 0))
        def _():
            dsc_ref[...] = jnp.zeros_like(dsc_ref)
            dsh_ref[...] = jnp.zeros_like(dsh_ref)

        @pl.when(kk == nk - 1)
        def _():
            dh = acc_ref[...]
            dx_ref[...] = dxp_ref[...] + dh * (1.0 + sc_ref[...])
            dsc_ref[...] += jnp.sum(dh * x_ref[...], axis=0, keepdims=True)
            dsh_ref[...] += jnp.sum(dh, axis=0, keepdims=True)

    row = pl.BlockSpec((tm, d), lambda i, kk: (i, 0))
    vec = pl.BlockSpec((1, d), lambda i, kk: (0, 0))
    grid = (s // tm, nk)
    anywhere, exchanged, sems = _rider_specs(riders)
    outs = pl.pallas_call(
        _riding(body, riders, 5, 3, grid), name=name, grid=grid,
        in_specs=[pl.BlockSpec((tm, tk), lambda i, kk: (i, kk)), pl.BlockSpec((d, tk), lambda i, kk: (0, kk)),
                  row, row, vec] + anywhere,
        out_specs=[row, vec, vec] + anywhere,
        out_shape=[jax.ShapeDtypeStruct((s, d), F32), jax.ShapeDtypeStruct((1, d), F32),
                   jax.ShapeDtypeStruct((1, d), F32)] + exchanged,
        scratch_shapes=[pltpu.VMEM((tm, d), F32)] + sems,
        compiler_params=_params("arbitrary", "arbitrary"),
    )(dproj, w, dx_part, x, scale, *[x_r for x_r, _ in riders])
    return (*outs[:3], list(outs[3:])) if riders else tuple(outs)


ROW_TILE = 512


def _modulate(x, scale, shift, *, name):
    s, d = x.shape

    def body(x_ref, sc_ref, sh_ref, h_ref):
        h_ref[...] = (x_ref[...] * (1.0 + sc_ref[...]) + sh_ref[...]).astype(BF16)

    row = pl.BlockSpec((ROW_TILE, d), lambda i: (i, 0))
    vec = pl.BlockSpec((1, d), lambda i: (0, 0))
    return pl.pallas_call(
        body, name=name, grid=(s // ROW_TILE,), in_specs=[row, vec, vec], out_specs=row,
        out_shape=jax.ShapeDtypeStruct((s, d), BF16), compiler_params=_params("parallel"),
    )(x, scale, shift)


def _resid_ln_fwd(x, out, gate, g, b, *, name):
    s, d = x.shape

    def body(x_ref, o_ref, gate_ref, g_ref, b_ref, y_ref):
        r = ALPHA * x_ref[...] + gate_ref[...] * o_ref[...]
        mu = jnp.mean(r, axis=-1, keepdims=True)
        rc = r - mu
        var = jnp.mean(rc * rc, axis=-1, keepdims=True)
        y_ref[...] = rc * lax.rsqrt(var + LN_EPS) * g_ref[...] + b_ref[...]

    row = pl.BlockSpec((ROW_TILE, d), lambda i: (i, 0))
    vec = pl.BlockSpec((1, d), lambda i: (0, 0))
    return pl.pallas_call(
        body, name=name, grid=(s // ROW_TILE,), in_specs=[row, row, vec, vec, vec], out_specs=row,
        out_shape=jax.ShapeDtypeStruct((s, d), F32), compiler_params=_params("parallel"),
    )(x, out, gate, g, b)


def _resid_ln_bwd(x, out, gate, g, dy, *, name):
    s, d = x.shape

    def body(x_ref, o_ref, gate_ref, g_ref, dy_ref, dxp_ref, dout_ref, dgate_ref, dg_ref, db_ref):
        @pl.when(pl.program_id(0) == 0)
        def _():
            dgate_ref[...] = jnp.zeros_like(dgate_ref)
            dg_ref[...] = jnp.zeros_like(dg_ref)
            db_ref[...] = jnp.zeros_like(db_ref)

        o = o_ref[...]
        r = ALPHA * x_ref[...] + gate_ref[...] * o
        mu = jnp.mean(r, axis=-1, keepdims=True)
        rc = r - mu
        var = jnp.mean(rc * rc, axis=-1, keepdims=True)
        rstd = lax.rsqrt(var + LN_EPS)
        xhat = rc * rstd
        dy = dy_ref[...]
        dxh = dy * g_ref[...]
        dr = rstd * (dxh - jnp.mean(dxh, axis=-1, keepdims=True) - xhat * jnp.mean(dxh * xhat, axis=-1, keepdims=True))
        dxp_ref[...] = ALPHA * dr
        dout_ref[...] = (gate_ref[...] * dr).astype(BF16)
        dgate_ref[...] += jnp.sum(dr * o, axis=0, keepdims=True)
        dg_ref[...] += jnp.sum(dy * xhat, axis=0, keepdims=True)
        db_ref[...] += jnp.sum(dy, axis=0, keepdims=True)

    row = pl.BlockSpec((ROW_TILE, d), lambda i: (i, 0))
    vec = pl.BlockSpec((1, d), lambda i: (0, 0))
    vshape = jax.ShapeDtypeStruct((1, d), F32)
    return pl.pallas_call(
        body, name=name, grid=(s // ROW_TILE,), in_specs=[row, row, vec, vec, row],
        out_specs=[row, row, vec, vec, vec],
        out_shape=[jax.ShapeDtypeStruct((s, d), F32), jax.ShapeDtypeStruct((s, d), BF16), vshape, vshape, vshape],
        compiler_params=_params("arbitrary"),
    )(x, out, gate, g, dy)


def _loss_and_grad(y, target, *, name):
    s, d = y.shape

    def body(y_ref, t_ref, dy_ref, loss_ref):
        @pl.when(pl.program_id(0) == 0)
        def _():
            loss_ref[...] = jnp.zeros_like(loss_ref)

        e = y_ref[...] - t_ref[...]
        dy_ref[...] = e * (1.0 / d)
        loss_ref[...] += jnp.sum(jnp.sum(e * e, axis=0, keepdims=True), axis=1, keepdims=True) * (0.5 / d)

    row = pl.BlockSpec((ROW_TILE, d), lambda i: (i, 0))
    return pl.pallas_call(
        body, name=name, grid=(s // ROW_TILE,), in_specs=[row, row],
        out_specs=[row, pl.BlockSpec((1, 128), lambda i: (0, 0))],
        out_shape=[jax.ShapeDtypeStruct((s, d), F32), jax.ShapeDtypeStruct((1, 128), F32)],
        compiler_params=_params("arbitrary"),
    )(y, target)


_SLOPES = np.asarray(2.0 ** (-8.0 * (np.arange(A_HEADS, dtype=np.float32) + 1.0) / A_HEADS), dtype=np.float32)


def _attn_scores(q, kw, slope, dist, valid):
    s = lax.dot_general(q, kw, _DIMS["nt"], preferred_element_type=F32) * (1.0 / math.sqrt(A_HEAD_DIM))
    return jnp.where(valid, s - slope * dist, NEG_BIG)


def _attn_window(blk, length, win, dil):
    start = pl.multiple_of(jnp.clip(blk * A_QBLOCK - A_RADIUS, 0, length - win), A_RADIUS)
    qpos = blk * A_QBLOCK + lax.broadcasted_iota(jnp.int32, (A_QBLOCK, win), 0)
    kpos = start + lax.broadcasted_iota(jnp.int32, (A_QBLOCK, win), 1)
    delta = jnp.abs(kpos - qpos)
    return start, (delta * dil).astype(F32), delta <= A_RADIUS


A_BLOCKS_PER_STEP = 4
A_GROUP_COLS = 3 * A_WIDTH


def _tile_scratch(rows, width):
    return pltpu.VMEM((width // 128, rows, 128), F32)


def _put_tile(scr, val):
    for j in range(scr.shape[0]):
        scr[j] = val[:, j * 128:(j + 1) * 128]


def _get_tile(scr):
    return jnp.concatenate([scr[j] for j in range(scr.shape[0])], axis=1)


def _get_residue(scr, r, dil):
    rows = pl.ds(r, scr.shape[1] // dil, stride=dil)
    return jnp.concatenate([scr.at[j][rows, :] for j in range(scr.shape[0])], axis=1)


def _put_residue(scr, r, dil, val):
    rows = pl.ds(r, scr.shape[1] // dil, stride=dil)
    for j in range(scr.shape[0]):
        scr.at[j][rows, :] = val[:, j * 128:(j + 1) * 128]


def _mm_dilated(a, b, dil, *, name):
    m, k = a.shape
    n = b.shape[1]
    tm = 512

    def body(a_ref, b_ref, o_ref, acc_ref):
        _put_tile(acc_ref, jnp.dot(a_ref[...], b_ref[...], preferred_element_type=F32))
        for r in range(dil):
            o_ref[:, r * n:(r + 1) * n] = _get_residue(acc_ref, r, dil).astype(BF16)

    return pl.pallas_call(
        body, name=name, grid=(m // tm,),
        in_specs=[pl.BlockSpec((tm, k), lambda i: (i, 0)), pl.BlockSpec((k, n), lambda i: (0, 0))],
        out_specs=pl.BlockSpec((tm // dil, dil * n), lambda i: (i, 0)),
        out_shape=jax.ShapeDtypeStruct((m // dil, dil * n), BF16), scratch_shapes=[_tile_scratch(tm, n)],
        compiler_params=_params("parallel"),
    )(a, b)


def _attn_fwd(pv, group, *, name, riders=()):
    dil = DILATIONS[group]
    length = pv.shape[0]
    cb, qoff = pv.shape[1] // (128 * dil), 0
    win = min(2 * A_QBLOCK, length)
    nblk = length // A_QBLOCK
    per = A_BLOCKS_PER_STEP if nblk % A_BLOCKS_PER_STEP == 0 else 1

    def body(slope_ref, q_ref, k_ref, v_ref, o_ref, lse_ref):
        hp = pl.program_id(1)
        for u in range(per):
            rows = slice(u * A_QBLOCK, (u + 1) * A_QBLOCK)
            start, dist, valid = _attn_window(pl.program_id(2) * per + u, length, win, dil)
            kw = k_ref[pl.ds(start, win), :]
            vw = v_ref[pl.ds(start, win), :]
            q = q_ref[rows, :]
            outs, lses = [], []
            for hh in range(2):
                sl = slice(hh * A_HEAD_DIM, (hh + 1) * A_HEAD_DIM)
                sc = _attn_scores(q[:, sl], kw[:, sl], slope_ref[hp * 2 + hh], dist, valid)
                m = jnp.max(sc, axis=-1, keepdims=True)
                p = jnp.exp(sc - m)
                z = jnp.sum(p, axis=-1, keepdims=True)
                o = jnp.dot(p.astype(BF16), vw[:, sl], preferred_element_type=F32) / z
                outs.append(o)
                lses.append(jnp.broadcast_to(m + jnp.log(z), (A_QBLOCK, A_HEAD_DIM)))
            o_ref[rows, :] = jnp.concatenate(outs, axis=1).astype(BF16)
            lse_ref[rows, :] = jnp.concatenate(lses, axis=1)

    qspec = pl.BlockSpec((per * A_QBLOCK, 128), lambda r, hp, b: (b, r * cb + qoff + hp))
    kspec = pl.BlockSpec((length, 128), lambda r, hp, b: (0, r * cb + qoff + 8 + hp))
    vspec = pl.BlockSpec((length, 128), lambda r, hp, b: (0, r * cb + qoff + 16 + hp))
    ospec = pl.BlockSpec((per * A_QBLOCK, 128), lambda r, hp, b: (b, r * 8 + hp))
    oshape = jax.ShapeDtypeStruct((length, dil * A_WIDTH), F32)
    grid = (dil, 8, nblk // per)
    anywhere, exchanged, sems = _rider_specs(riders)
    outs = pl.pallas_call(
        _riding(body, riders, 4, 2, grid), name=name, grid=grid,
        in_specs=[pl.BlockSpec(memory_space=pltpu.SMEM), qspec, kspec, vspec] + anywhere, out_specs=[ospec, ospec] + anywhere,
        out_shape=[jax.ShapeDtypeStruct(oshape.shape, BF16), oshape] + exchanged, scratch_shapes=sems,
        compiler_params=_params(*(("arbitrary",) * 3 if riders else ("parallel", "parallel", "arbitrary"))),
    )(jnp.asarray(_SLOPES), pv, pv, pv, *[x for x, _ in riders])
    return (outs[0], outs[1], list(outs[2:])) if riders else (outs[0], outs[1])


def _attn_bwd(pv, group, do, o, lse, *, name):
    dil = DILATIONS[group]
    length = pv.shape[0]
    cb, qoff = pv.shape[1] // (128 * dil), 0
    win = min(2 * A_QBLOCK, length)
    nblk = length // A_QBLOCK
    per = A_BLOCKS_PER_STEP if nblk % A_BLOCKS_PER_STEP == 0 else 1
    nstep = nblk // per
    scale = 1.0 / math.sqrt(A_HEAD_DIM)

    def body(slope_ref, q_ref, k_ref, v_ref, do_ref, o_ref, lse_ref, dq_ref, dk_ref, dv_ref, dk_acc, dv_acc):
        hp, step = pl.program_id(1), pl.program_id(2)

        @pl.when(step == 0)
        def _():
            dk_acc[...] = jnp.zeros_like(dk_acc)
            dv_acc[...] = jnp.zeros_like(dv_acc)

        for u in range(per):
            rows = slice(u * A_QBLOCK, (u + 1) * A_QBLOCK)
            start, dist, valid = _attn_window(step * per + u, length, win, dil)
            kw = k_ref[pl.ds(start, win), :]
            vw = v_ref[pl.ds(start, win), :]
            q = q_ref[rows, :]
            do_b = do_ref[rows, :]
            dsum = do_b.astype(F32) * o_ref[rows, :]
            lse_b = lse_ref[rows, :]
            dqs, dks, dvs = [], [], []
            for hh in range(2):
                sl = slice(hh * A_HEAD_DIM, (hh + 1) * A_HEAD_DIM)
                sc = _attn_scores(q[:, sl], kw[:, sl], slope_ref[hp * 2 + hh], dist, valid)
                p = jnp.exp(sc - lse_b[:, hh * A_HEAD_DIM:hh * A_HEAD_DIM + 1])
                dp = lax.dot_general(do_b[:, sl], vw[:, sl], _DIMS["nt"], preferred_element_type=F32)
                ds = (p * (dp - jnp.sum(dsum[:, sl], axis=-1, keepdims=True))).astype(BF16)
                dqs.append(jnp.dot(ds, kw[:, sl], preferred_element_type=F32) * scale)
                dks.append(lax.dot_general(ds, q[:, sl], _DIMS["tn"], preferred_element_type=F32) * scale)
                dvs.append(lax.dot_general(p.astype(BF16), do_b[:, sl], _DIMS["tn"], preferred_element_type=F32))
            dq_ref[rows, :] = jnp.concatenate(dqs, axis=1).astype(BF16)
            dk_acc[pl.ds(start, win), :] += jnp.concatenate(dks, axis=1)
            dv_acc[pl.ds(start, win), :] += jnp.concatenate(dvs, axis=1)

        @pl.when(step == nstep - 1)
        def _():
            dk_ref[...] = dk_acc[...].astype(BF16)
            dv_ref[...] = dv_acc[...].astype(BF16)

    qspec = pl.BlockSpec((per * A_QBLOCK, 128), lambda r, hp, b: (b, r * cb + qoff + hp))
    kspec = pl.BlockSpec((length, 128), lambda r, hp, b: (0, r * cb + qoff + 8 + hp))
    vspec = pl.BlockSpec((length, 128), lambda r, hp, b: (0, r * cb + qoff + 16 + hp))
    bspec = pl.BlockSpec((per * A_QBLOCK, 128), lambda r, hp, b: (b, r * 8 + hp))
    fspec = pl.BlockSpec((length, 128), lambda r, hp, b: (0, r * 8 + hp))
    oshape = jax.ShapeDtypeStruct((length, dil * A_WIDTH), BF16)
    dq, dk, dv = pl.pallas_call(
        body, name=name, grid=(dil, 8, nstep),
        in_specs=[pl.BlockSpec(memory_space=pltpu.SMEM), qspec, kspec, vspec, bspec, bspec, bspec],
        out_specs=[bspec, fspec, fspec], out_shape=[oshape, oshape, oshape],
        scratch_shapes=[pltpu.VMEM((length, 128), F32), pltpu.VMEM((length, 128), F32)],
        compiler_params=_params("parallel", "parallel", "arbitrary"),
    )(jnp.asarray(_SLOPES), pv, pv, pv, do, o, lse)
    return dq, dk, dv


A_GATE_BLOCK = 3
A_ROWS = 256


def _lanes_of(r):
    return slice(r * A_WIDTH, (r + 1) * A_WIDTH)


def _dilated_spec(dil):
    return pl.BlockSpec((A_ROWS // dil, dil * A_WIDTH), lambda i: (i, 0))


def _attn_combine(o0, l0, o1, l1, o2, l2, proj0, *, name):
    s = proj0.shape[0]

    def body(o0_ref, l0_ref, o1_ref, l1_ref, o2_ref, l2_ref, gate_ref, y_ref, o_ref, lse_ref, so1, sl1, so2, sl2):
        for src, dst, dil in ((o1_ref, so1, DILATIONS[1]), (l1_ref, sl1, DILATIONS[1]),
                              (o2_ref, so2, DILATIONS[2]), (l2_ref, sl2, DILATIONS[2])):
            for r in range(dil):
                _put_residue(dst, r, dil, src[:, _lanes_of(r)].astype(F32))
        la, lb, lc = l0_ref[...], _get_tile(sl1), _get_tile(sl2)
        m = jnp.maximum(jnp.maximum(la, lb), lc)
        ea, eb, ec = jnp.exp(la - m), jnp.exp(lb - m), jnp.exp(lc - m)
        den = ea + eb + ec
        o = (ea * o0_ref[...].astype(F32) + eb * _get_tile(so1) + ec * _get_tile(so2)) / den
        o_ref[...] = o
        lse_ref[...] = m + jnp.log(den)
        y_ref[...] = (o * _silu_and_grad(gate_ref[...].astype(F32))[0]).astype(BF16)

    row = pl.BlockSpec((A_ROWS, A_WIDTH), lambda i: (i, 0))
    gspec = pl.BlockSpec((A_ROWS, A_WIDTH), lambda i: (i, A_GATE_BLOCK))
    d1, d2 = _dilated_spec(DILATIONS[1]), _dilated_spec(DILATIONS[2])
    return pl.pallas_call(
        body, name=name, grid=(s // A_ROWS,), in_specs=[row, row, d1, d1, d2, d2, gspec], out_specs=[row, row, row],
        out_shape=[jax.ShapeDtypeStruct((s, A_WIDTH), BF16), jax.ShapeDtypeStruct((s, A_WIDTH), F32),
                   jax.ShapeDtypeStruct((s, A_WIDTH), F32)],
        scratch_shapes=[_tile_scratch(A_ROWS, A_WIDTH)] * 4, compiler_params=_params("parallel"),
    )(o0, l0, o1, l1, o2, l2, proj0)


def _attn_combine_bwd(dy, o, lse, proj0, *, name):
    s = proj0.shape[0]

    def body(dy_ref, o_ref, lse_ref, gate_ref, dg_ref, do_ref, do1, o1, l1, do2, o2, l2, s_do, s_o, s_l):
        si, dsi = _silu_and_grad(gate_ref[...].astype(F32))
        dyv = dy_ref[...]
        ov = o_ref[...]
        do = dyv * si
        _put_tile(s_do, do)
        _put_tile(s_o, ov)
        _put_tile(s_l, lse_ref[...])
        do_ref[...] = do.astype(BF16)
        dg_ref[...] = (dyv * ov * dsi).astype(BF16)
        for (do_d, o_d, l_d), dil in (((do1, o1, l1), DILATIONS[1]), ((do2, o2, l2), DILATIONS[2])):
            for r in range(dil):
                do_d[:, _lanes_of(r)] = _get_residue(s_do, r, dil).astype(BF16)
                o_d[:, _lanes_of(r)] = _get_residue(s_o, r, dil)
                l_d[:, _lanes_of(r)] = _get_residue(s_l, r, dil)

    row = pl.BlockSpec((A_ROWS, A_WIDTH), lambda i: (i, 0))
    gspec = pl.BlockSpec((A_ROWS, A_WIDTH), lambda i: (i, A_GATE_BLOCK))
    shp = jax.ShapeDtypeStruct((s, A_WIDTH), BF16)
    dilated = lambda dil, dtype: jax.ShapeDtypeStruct((s // dil, dil * A_WIDTH), dtype)
    d1, d2 = _dilated_spec(DILATIONS[1]), _dilated_spec(DILATIONS[2])
    return pl.pallas_call(
        body, name=name, grid=(s // A_ROWS,), in_specs=[row, row, row, gspec],
        out_specs=[row, row, d1, d1, d1, d2, d2, d2],
        out_shape=[shp, shp, dilated(DILATIONS[1], BF16), dilated(DILATIONS[1], F32), dilated(DILATIONS[1], F32),
                   dilated(DILATIONS[2], BF16), dilated(DILATIONS[2], F32), dilated(DILATIONS[2], F32)],
        scratch_shapes=[_tile_scratch(A_ROWS, A_WIDTH)] * 3, compiler_params=_params("parallel"),
    )(dy, o, lse, proj0)


def _assemble_dproj(parts0, parts1, parts2, dgate, *, name):
    s = dgate.shape[0]

    def body(*refs):
        ins, out_ref, scr = refs[:10], refs[10], refs[11]
        for p in range(3):
            out_ref[:, _lanes_of(p)] = ins[p][...]
        for g, dil in ((1, DILATIONS[1]), (2, DILATIONS[2])):
            for p in range(3):
                src = ins[3 * g + p]
                for r in range(dil):
                    _put_residue(scr, r, dil, src[:, _lanes_of(r)].astype(F32))
                out_ref[:, _lanes_of(3 * g + p)] = _get_tile(scr).astype(BF16)
        out_ref[:, _lanes_of(9)] = ins[9][...]

    row = pl.BlockSpec((A_ROWS, A_WIDTH), lambda i: (i, 0))
    d1, d2 = _dilated_spec(DILATIONS[1]), _dilated_spec(DILATIONS[2])
    return pl.pallas_call(
        body, name=name, grid=(s // A_ROWS,), in_specs=[row] * 3 + [d1] * 3 + [d2] * 3 + [row],
        out_specs=pl.BlockSpec((A_ROWS, A_IN_COLS), lambda i: (i, 0)),
        out_shape=jax.ShapeDtypeStruct((s, A_IN_COLS), BF16),
        scratch_shapes=[_tile_scratch(A_ROWS, A_WIDTH)], compiler_params=_params("parallel"),
    )(*parts0, *parts1, *parts2, dgate)


CONV_TILE = 256
CONV_SUB = 4


def _conv_taps(xe, n):
    return [xe if j == 2 else pltpu.roll(xe, (2 - j) % n, 0) for j in range(SSM_CONV)]


def _conv_fwd(xpad, w, b, *, name):
    s = xpad.shape[0] - 2 * CONV_HALO
    n = CONV_TILE + 2 * CONV_HALO
    ncol = SSM_CONV_DIM // 128

    sub = min(CONV_SUB, s // CONV_TILE)

    def body(x_ref, w_ref, b_ref, o_ref):
        base = pl.program_id(1) * (sub * CONV_TILE)

        def tile(k, carry):
            r0 = pl.multiple_of(k * CONV_TILE, CONV_TILE)
            t0 = pl.multiple_of(base + r0, CONV_TILE)
            taps = _conv_taps(x_ref[pl.ds(t0, n), :].astype(F32), n)
            pre = b_ref[...]
            for j in range(SSM_CONV):
                pre = pre + w_ref[j:j + 1, :] * taps[j]
            o_ref[pl.ds(r0, CONV_TILE), :] = _silu_and_grad(pre[CONV_HALO:CONV_HALO + CONV_TILE])[0].astype(BF16)
            return carry

        lax.fori_loop(0, sub, tile, 0)

    return pl.pallas_call(
        body, name=name, grid=(ncol, s // (sub * CONV_TILE)),
        in_specs=[pl.BlockSpec((s + 2 * CONV_HALO, 128), lambda j, i: (0, j)),
                  pl.BlockSpec((SSM_CONV, 128), lambda j, i: (0, j)), pl.BlockSpec((1, 128), lambda j, i: (0, j))],
        out_specs=pl.BlockSpec((sub * CONV_TILE, 128), lambda j, i: (i, j)),
        out_shape=jax.ShapeDtypeStruct((s, SSM_CONV_DIM), BF16), compiler_params=_params("parallel", "arbitrary"),
    )(xpad, w, b)


def _conv_bwd(xpad, dapad, w, b, *, name):
    s = xpad.shape[0] - 2 * CONV_HALO
    n = CONV_TILE + 2 * CONV_HALO
    ncol = SSM_CONV_DIM // 128
    mid = slice(CONV_HALO, CONV_HALO + CONV_TILE)
    sub = min(CONV_SUB, s // CONV_TILE)

    def body(x_ref, da_ref, w_ref, b_ref, dx_ref, dw_ref, db_ref):
        @pl.when(pl.program_id(1) == 0)
        def _():
            dw_ref[...] = jnp.zeros_like(dw_ref)
            db_ref[...] = jnp.zeros_like(db_ref)

        base = pl.program_id(1) * (sub * CONV_TILE)

        def tile(k, carry):
            r0 = pl.multiple_of(k * CONV_TILE, CONV_TILE)
            t0 = pl.multiple_of(base + r0, CONV_TILE)
            taps = _conv_taps(x_ref[pl.ds(t0, n), :].astype(F32), n)
            pre = b_ref[...]
            for j in range(SSM_CONV):
                pre = pre + w_ref[j:j + 1, :] * taps[j]
            dpre = da_ref[pl.ds(t0, n), :] * _silu_and_grad(pre)[1]
            dx = jnp.zeros((CONV_TILE, 128), F32)
            for j in range(SSM_CONV):
                back = dpre if j == 2 else pltpu.roll(dpre, (j - 2) % n, 0)
                dx = dx + w_ref[j:j + 1, :] * back[mid]
                dw_ref[j:j + 1, :] += jnp.sum(dpre[mid] * taps[j][mid], axis=0, keepdims=True)
            dx_ref[pl.ds(r0, CONV_TILE), :] = dx.astype(BF16)
            db_ref[...] += jnp.sum(dpre[mid], axis=0, keepdims=True)
            return carry

        lax.fori_loop(0, sub, tile, 0)

    full = pl.BlockSpec((s + 2 * CONV_HALO, 128), lambda j, i: (0, j))
    wspec = pl.BlockSpec((SSM_CONV, 128), lambda j, i: (0, j))
    bspec = pl.BlockSpec((1, 128), lambda j, i: (0, j))
    return pl.pallas_call(
        body, name=name, grid=(ncol, s // (sub * CONV_TILE)), in_specs=[full, full, wspec, bspec],
        out_specs=[pl.BlockSpec((sub * CONV_TILE, 128), lambda j, i: (i, j)), wspec, bspec],
        out_shape=[jax.ShapeDtypeStruct((s, SSM_CONV_DIM), BF16), jax.ShapeDtypeStruct((SSM_CONV, SSM_CONV_DIM), F32),
                   jax.ShapeDtypeStruct((1, SSM_CONV_DIM), F32)],
        compiler_params=_params("parallel", "arbitrary"),
    )(xpad, dapad, w, b)


HPG = SSM_HEADS // SSM_GROUPS
GW = HPG * SSM_HEAD_DIM
T = SSM_CHUNK


def _ssd_specs(nc):
    ceff = lambda d, c: jnp.where(d == 0, c, nc - 1 - c)
    return ceff, [
        pl.BlockSpec((T, GW), lambda d, g, c: (ceff(d, c), g)),
        pl.BlockSpec((T, SSM_STATE), lambda d, g, c: (ceff(d, c), SSM_INNER // 128 + g)),
        pl.BlockSpec((T, SSM_STATE), lambda d, g, c: (ceff(d, c), SSM_INNER // 128 + SSM_GROUPS + g)),
        pl.BlockSpec((None, None, T, HPG), lambda d, g, c: (d, g, ceff(d, c), 0)),
        pl.BlockSpec((None, None, HPG, T), lambda d, g, c: (d, g, 0, ceff(d, c))),
        pl.BlockSpec((None, None, 2, HPG), lambda d, g, c: (d, g, 0, 0)),
        pl.BlockSpec((None, None, HPG, 2), lambda d, g, c: (d, g, 0, 0)),
    ]


def _ssd_chunk_common(d, dt_ref, dtt_ref, prr_ref, prc_ref):
    sgn = 1 - 2 * d
    ri = lax.broadcasted_iota(jnp.int32, (T, T), 0)
    ci = lax.broadcasted_iota(jnp.int32, (T, T), 1)
    mask = ((ri - ci) * sgn) >= 0
    maskf = mask.astype(F32)
    bias_r, a_r = prr_ref[0:1, :], prr_ref[1:2, :]
    bias_c, a_c = prc_ref[:, 0:1], prc_ref[:, 1:2]
    raw = dt_ref[...] + bias_r
    dt_rows = _softplus(raw)
    dt_lanes = _softplus(dtt_ref[...] + bias_c)
    a_rows = dt_rows * a_r
    acum_rows = jnp.dot(maskf, a_rows, precision=HIGHEST, preferred_element_type=F32)
    acum_lanes = lax.dot_general(dt_lanes * a_c, maskf, _DIMS["nt"], precision=HIGHEST, preferred_element_type=F32)
    tot = jnp.sum(a_rows, axis=0, keepdims=True)
    return mask, maskf, raw, dt_rows, a_r, acum_rows, acum_lanes, tot


def _lanes_per_head(pieces):
    return jnp.concatenate([jnp.broadcast_to(p, (p.shape[0], SSM_HEAD_DIM)) for p in pieces], axis=1)


def _ssd_fwd_v1(xbc, dtr, dtt, pr_rows, pr_cols, *, name):
    s = xbc.shape[0]
    nc = s // T
    ceff, in_specs = _ssd_specs(nc)

    def body(x_ref, b_ref, c_ref, dt_ref, dtt_ref, prr_ref, prc_ref, y_ref, hs_ref, st_ref):
        d, c = pl.program_id(0), pl.program_id(2)

        @pl.when(c == 0)
        def _():
            st_ref[...] = jnp.zeros_like(st_ref)

        mask, _, _, dt_rows, _, acum_rows, acum_lanes, tot = _ssd_chunk_common(d, dt_ref, dtt_ref, prr_ref, prc_ref)
        xs = x_ref[...].astype(F32)
        bm, cm = b_ref[...], c_ref[...]
        hprev = st_ref[...]
        hs_ref[...] = hprev
        cb = lax.dot_general(cm, bm, _DIMS["nt"], preferred_element_type=F32)
        ch = jnp.dot(cm, hprev.astype(BF16), preferred_element_type=F32)
        ys, xgds, etots = [], [], []
        for j in range(HPG):
            sl = slice(j * SSM_HEAD_DIM, (j + 1) * SSM_HEAD_DIM)
            ac, al = acum_rows[:, j:j + 1], acum_lanes[j:j + 1, :]
            lm = jnp.where(mask, jnp.exp(jnp.minimum(ac - al, 0.0)), 0.0)
            xg = xs[:, sl] * dt_rows[:, j:j + 1]
            yd = jnp.dot((cb * lm).astype(BF16), xg.astype(BF16), preferred_element_type=F32)
            ys.append(yd + jnp.exp(ac) * ch[:, sl])
            xgds.append(xg * jnp.exp(tot[:, j:j + 1] - ac))
            etots.append(jnp.exp(tot[:, j:j + 1]))
        y_ref[...] = jnp.concatenate(ys, axis=1)
        new = lax.dot_general(bm, jnp.concatenate(xgds, axis=1).astype(BF16), _DIMS["tn"], preferred_element_type=F32)
        st_ref[...] = hprev * _lanes_per_head(etots) + new

    return pl.pallas_call(
        body, name=name, grid=(2, SSM_GROUPS, nc), in_specs=in_specs,
        out_specs=[pl.BlockSpec((None, T, GW), lambda d, g, c: (d, ceff(d, c), g)),
                   pl.BlockSpec((None, None, None, SSM_STATE, GW), lambda d, g, c: (d, ceff(d, c), g, 0, 0))],
        out_shape=[jax.ShapeDtypeStruct((2, s, SSM_INNER), F32),
                   jax.ShapeDtypeStruct((2, nc, SSM_GROUPS, SSM_STATE, GW), F32)],
        scratch_shapes=[pltpu.VMEM((SSM_STATE, GW), F32)],
        compiler_params=_params("parallel", "parallel", "arbitrary"),
    )(xbc, xbc, xbc, dtr, dtt, pr_rows, pr_cols)


def _put_lane(j, col):
    lane = lax.broadcasted_iota(jnp.int32, (col.shape[0], HPG), 1)
    return jnp.where(lane == j, col, 0.0)


def _ssd_bwd_v1(xbc, dtr, dtt, pr_rows, pr_cols, dvec, hs, dy, *, name):
    s = xbc.shape[0]
    nc = s // T
    cb_of = lambda d, c: jnp.where(d == 0, nc - 1 - c, c)
    in_specs = [
        pl.BlockSpec((T, GW), lambda d, g, c: (cb_of(d, c), g)),
        pl.BlockSpec((T, SSM_STATE), lambda d, g, c: (cb_of(d, c), SSM_INNER // 128 + g)),
        pl.BlockSpec((T, SSM_STATE), lambda d, g, c: (cb_of(d, c), SSM_INNER // 128 + SSM_GROUPS + g)),
        pl.BlockSpec((None, None, T, HPG), lambda d, g, c: (d, g, cb_of(d, c), 0)),
        pl.BlockSpec((None, None, HPG, T), lambda d, g, c: (d, g, 0, cb_of(d, c))),
        pl.BlockSpec((None, None, 2, HPG), lambda d, g, c: (d, g, 0, 0)),
        pl.BlockSpec((None, None, HPG, 2), lambda d, g, c: (d, g, 0, 0)),
        pl.BlockSpec((1, GW), lambda d, g, c: (0, g)),
        pl.BlockSpec((None, None, None, SSM_STATE, GW), lambda d, g, c: (d, cb_of(d, c), g, 0, 0)),
        pl.BlockSpec((T, GW), lambda d, g, c: (cb_of(d, c), g)),
    ]

    def body(x_ref, b_ref, c_ref, dt_ref, dtt_ref, prr_ref, prc_ref, dvec_ref, hs_ref, dy_ref,
             dxs_ref, db_ref, dc_ref, ddt_ref, dalog_ref, dbias_ref, g_ref):
        d, c = pl.program_id(0), pl.program_id(2)

        @pl.when(c == 0)
        def _():
            g_ref[...] = jnp.zeros_like(g_ref)
            dalog_ref[...] = jnp.zeros_like(dalog_ref)
            dbias_ref[...] = jnp.zeros_like(dbias_ref)

        mask, maskf, raw, dt_rows, a_r, acum_rows, acum_lanes, tot = _ssd_chunk_common(
            d, dt_ref, dtt_ref, prr_ref, prc_ref)
        xs = x_ref[...].astype(F32)
        bm, cm = b_ref[...], c_ref[...]
        hst = hs_ref[...]
        gst = g_ref[...]
        dyv = dy_ref[...]
        dyb = dyv.astype(BF16)
        dv = dvec_ref[...] * (1 - d).astype(F32)
        cb = lax.dot_general(cm, bm, _DIMS["nt"], preferred_element_type=F32)
        ch = jnp.dot(cm, hst.astype(BF16), preferred_element_type=F32)
        bg = jnp.dot(bm, gst.astype(BF16), preferred_element_type=F32)
        hg = hst * gst
        dcb = jnp.zeros((T, T), F32)
        dacum = jnp.zeros((T, HPG), F32)
        rx = jnp.zeros((T, HPG), F32)
        dtot = jnp.zeros((1, HPG), F32)
        dxss, dyes, xgds, etots = [], [], [], []
        for j in range(HPG):
            sl = slice(j * SSM_HEAD_DIM, (j + 1) * SSM_HEAD_DIM)
            ac, al = acum_rows[:, j:j + 1], acum_lanes[j:j + 1, :]
            lm = jnp.where(mask, jnp.exp(jnp.minimum(ac - al, 0.0)), 0.0)
            m = cb * lm
            dtj = dt_rows[:, j:j + 1]
            xsj = xs[:, sl]
            xg = xsj * dtj
            dyj = dyv[:, sl]
            ec = jnp.exp(ac)
            etot = jnp.exp(tot[:, j:j + 1])
            decay = jnp.exp(tot[:, j:j + 1] - ac)
            dm = lax.dot_general(dyb[:, sl], xg.astype(BF16), _DIMS["nt"], preferred_element_type=F32)
            w = dm * m
            dcb = dcb + dm * lm
            bgj = bg[:, sl]
            dxg = lax.dot_general(m.astype(BF16), dyb[:, sl], _DIMS["tn"], preferred_element_type=F32) + decay * bgj
            xb = decay * jnp.sum(xg * bgj, axis=-1, keepdims=True)
            da_j = (jnp.sum(w, axis=-1, keepdims=True) - jnp.sum(w.T, axis=-1, keepdims=True)
                    + jnp.sum(ec * ch[:, sl] * dyj, axis=-1, keepdims=True) - xb)
            dacum = dacum + _put_lane(j, da_j)
            rx = rx + _put_lane(j, jnp.sum(dxg * xsj, axis=-1, keepdims=True))
            dtot_j = (etot * jnp.sum(jnp.sum(hg[:, sl], axis=0, keepdims=True), axis=1, keepdims=True)
                      + jnp.sum(xb, axis=0, keepdims=True))
            dtot = dtot + _put_lane(j, dtot_j)
            dxss.append(dxg * dtj + dv[:, sl] * dyj)
            dyes.append(dyj * ec)
            xgds.append(xg * decay)
            etots.append(etot)
        da = lax.dot_general(maskf, dacum, _DIMS["tn"], precision=HIGHEST, preferred_element_type=F32) + dtot
        ddt = da * a_r + rx
        draw = ddt * _sigmoid(raw)
        ddt_ref[...] = draw
        dbias_ref[...] += jnp.sum(draw, axis=0, keepdims=True)
        dalog_ref[...] += jnp.sum(da * dt_rows, axis=0, keepdims=True) * a_r
        dxs_ref[...] = jnp.concatenate(dxss, axis=1)
        dye = jnp.concatenate(dyes, axis=1).astype(BF16)
        xgd = jnp.concatenate(xgds, axis=1).astype(BF16)
        dcbb = dcb.astype(BF16)
        dc_ref[...] = (jnp.dot(dcbb, bm, preferred_element_type=F32)
                       + lax.dot_general(dye, hst.astype(BF16), _DIMS["nt"], preferred_element_type=F32))
        db_ref[...] = (lax.dot_general(dcbb, cm, _DIMS["tn"], preferred_element_type=F32)
                       + lax.dot_general(xgd, gst.astype(BF16), _DIMS["nt"], preferred_element_type=F32))
        g_ref[...] = lax.dot_general(cm, dye, _DIMS["tn"], preferred_element_type=F32) + gst * _lanes_per_head(etots)

    small = pl.BlockSpec((None, None, 1, HPG), lambda d, g, c: (d, g, 0, 0))
    sshape = jax.ShapeDtypeStruct((2, SSM_GROUPS, 1, HPG), F32)
    return pl.pallas_call(
        body, name=name, grid=(2, SSM_GROUPS, nc), in_specs=in_specs,
        out_specs=[pl.BlockSpec((None, T, GW), lambda d, g, c: (d, cb_of(d, c), g)),
                   pl.BlockSpec((None, T, SSM_STATE), lambda d, g, c: (d, cb_of(d, c), g)),
                   pl.BlockSpec((None, T, SSM_STATE), lambda d, g, c: (d, cb_of(d, c), g)),
                   pl.BlockSpec((None, None, T, HPG), lambda d, g, c: (d, g, cb_of(d, c), 0)), small, small],
        out_shape=[jax.ShapeDtypeStruct((2, s, SSM_INNER), F32),
                   jax.ShapeDtypeStruct((2, s, SSM_GROUPS * SSM_STATE), F32),
                   jax.ShapeDtypeStruct((2, s, SSM_GROUPS * SSM_STATE), F32),
                   jax.ShapeDtypeStruct((2, SSM_GROUPS, s, HPG), F32), sshape, sshape],
        scratch_shapes=[pltpu.VMEM((SSM_STATE, GW), F32)],
        compiler_params=_params("parallel", "parallel", "arbitrary"),
    )(xbc, xbc, xbc, dtr, dtt, pr_rows, pr_cols, dvec, hs, dy)


PAIRS = HPG // 2


def _scan_lanes(x, forward):
    lane = lax.broadcasted_iota(jnp.int32, x.shape, 1)
    p = x
    k = 1
    while k < T:
        p = p + jnp.where(lane >= k, pltpu.roll(p, k, 1), 0.0)
        k *= 2
    tot = p[:, T - 1:T]
    return jnp.where(forward, p, tot - p + x), tot


def _ssd_chunk(d, dt_ref, dtt_ref, prr_ref, prc_ref):
    sgn = 1 - 2 * d
    ri = lax.broadcasted_iota(jnp.int32, (T, T), 0)
    ci = lax.broadcasted_iota(jnp.int32, (T, T), 1)
    mask = ((ri - ci) * sgn) >= 0
    mask_t = ((ci - ri) * sgn) >= 0
    bias_r = prr_ref[0:1, :]
    bias_c, a_c = prc_ref[:, 0:1], prc_ref[:, 1:2]
    dt_rows = _softplus(dt_ref[...] + bias_r)
    raw_lanes = dtt_ref[...] + bias_c
    dt_lanes = _softplus(raw_lanes)
    acum_lanes, tot = _scan_lanes(dt_lanes * a_c, d == 0)
    return dict(mask=mask, mask_t=mask_t, head0=ci < SSM_HEAD_DIM, dt_rows=dt_rows, raw_lanes=raw_lanes,
                dt_lanes=dt_lanes, a_c=a_c, acum_lanes=acum_lanes, acum_rows=acum_lanes.T, tot=tot)


def _ssd_pair(ck, q):
    h0 = ck["head0"]
    colb = lambda rows, j: jnp.broadcast_to(rows[:, j:j + 1], (T, T))
    rowb = lambda lanes, j: jnp.broadcast_to(lanes[j:j + 1, :], (T, T))
    lms, lmts, acs = [], [], []
    for j in (2 * q, 2 * q + 1):
        ac, al = colb(ck["acum_rows"], j), rowb(ck["acum_lanes"], j)
        lms.append(jnp.where(ck["mask"], jnp.exp(jnp.minimum(ac - al, 0.0)), 0.0))
        lmts.append(jnp.where(ck["mask_t"], jnp.exp(jnp.minimum(al - ac, 0.0)), 0.0))
        acs.append(ac)
    ac_pair = jnp.where(h0, acs[0], acs[1])
    dt_pair = jnp.where(h0, colb(ck["dt_rows"], 2 * q), colb(ck["dt_rows"], 2 * q + 1))
    tot_pair = jnp.where(h0[0:1], ck["tot"][2 * q:2 * q + 1, :], ck["tot"][2 * q + 1:2 * q + 2, :])
    return dict(lm=lms, lmt=lmts, dt=dt_pair, ec=jnp.exp(ac_pair), decay=jnp.exp(tot_pair - ac_pair),
                etot=jnp.exp(tot_pair))


def _split_heads(h0, v):
    zero = jnp.zeros_like(v)
    return jnp.where(h0, v, zero), jnp.where(h0, zero, v)


GPS = 4
GSTEPS = SSM_GROUPS // GPS
B_BLOCK0 = SSM_INNER // (GPS * SSM_STATE)
C_BLOCK0 = (SSM_INNER + SSM_GROUPS * SSM_STATE) // (GPS * SSM_STATE)


def _ssd_in_specs(chunk):
    return [
        pl.BlockSpec((T, GPS * GW), lambda d, g, c: (chunk(d, c), g)),
        pl.BlockSpec((T, GPS * SSM_STATE), lambda d, g, c: (chunk(d, c), B_BLOCK0 + g)),
        pl.BlockSpec((T, GPS * SSM_STATE), lambda d, g, c: (chunk(d, c), C_BLOCK0 + g)),
        pl.BlockSpec((GPS * SSM_STATE, T), lambda d, g, c: (g, chunk(d, c))),
        pl.BlockSpec((GPS * SSM_STATE, T), lambda d, g, c: (g, chunk(d, c))),
        pl.BlockSpec((None, GPS, T, HPG), lambda d, g, c: (d, g, chunk(d, c), 0)),
        pl.BlockSpec((None, GPS, HPG, T), lambda d, g, c: (d, g, 0, chunk(d, c))),
        pl.BlockSpec((None, GPS, 2, HPG), lambda d, g, c: (d, g, 0, 0)),
        pl.BlockSpec((None, GPS, HPG, 2), lambda d, g, c: (d, g, 0, 0)),
    ]


def _group_refs(gi, wide, state_wide, t_wide, lead):
    return ([r.at[:, pl.ds(gi * GW, GW)] for r in wide] + [r.at[:, pl.ds(gi * SSM_STATE, SSM_STATE)] for r in state_wide]
            + [r.at[pl.ds(gi * SSM_STATE, SSM_STATE), :] for r in t_wide] + [r.at[gi] for r in lead])


def _ssd_fwd(xbc, bt, ct, dtr, dtt, pr_rows, pr_cols, *, name, riders=()):
    s = xbc.shape[0]
    nc = s // T
    chunk = lambda d, c: jnp.where(d == 0, c, nc - 1 - c)

    def body(x_ref, b_ref, c_ref, bt_ref, ct_ref, dt_ref, dtt_ref, prr_ref, prc_ref, y_ref, hs_ref, st_ref):
        @pl.when(pl.program_id(2) == 0)
        def _():
            st_ref[...] = jnp.zeros_like(st_ref)

        for gi in range(GPS):
            group(*_group_refs(gi, [x_ref, y_ref], [b_ref, c_ref], [bt_ref, ct_ref],
                               [dt_ref, dtt_ref, prr_ref, prc_ref, hs_ref, st_ref]))

    def group(x_ref, y_ref, b_ref, c_ref, bt_ref, ct_ref, dt_ref, dtt_ref, prr_ref, prc_ref, hs_ref, st_ref):
        d = pl.program_id(0)
        ck = _ssd_chunk(d, dt_ref, dtt_ref, prr_ref, prc_ref)
        xs = x_ref[...].astype(F32)
        cm = c_ref[...]
        hprev = st_ref[...]
        hs_ref[...] = hprev
        cb = lax.dot_general(cm, b_ref[...], _DIMS["nt"], preferred_element_type=F32)
        ch = jnp.dot(cm, hprev.astype(BF16), preferred_element_type=F32)
        ys, xgds, etots = [], [], []
        for q in range(PAIRS):
            sl = slice(q * 128, (q + 1) * 128)
            pr = _ssd_pair(ck, q)
            xg = xs[:, sl] * pr["dt"]
            xg0, xg1 = _split_heads(ck["head0"], xg.astype(BF16))
            yd = (jnp.dot((cb * pr["lm"][0]).astype(BF16), xg0, preferred_element_type=F32)
                  + jnp.dot((cb * pr["lm"][1]).astype(BF16), xg1, preferred_element_type=F32))
            ys.append(yd + pr["ec"] * ch[:, sl])
            xgds.append(xg * pr["decay"])
            etots.append(pr["etot"])
        y_ref[...] = jnp.concatenate(ys, axis=1)
        new = jnp.dot(bt_ref[...], jnp.concatenate(xgds, axis=1).astype(BF16), preferred_element_type=F32)
        st_ref[...] = hprev * jnp.concatenate(etots, axis=1) + new

    grid = (2, GSTEPS, nc)
    anywhere, exchanged, sems = _rider_specs(riders)
    outs = pl.pallas_call(
        _riding(body, riders, 9, 2, grid), name=name, grid=grid, in_specs=_ssd_in_specs(chunk) + anywhere,
        out_specs=[pl.BlockSpec((None, T, GPS * GW), lambda d, g, c: (d, chunk(d, c), g)),
                   pl.BlockSpec((None, None, GPS, SSM_STATE, GW), lambda d, g, c: (d, chunk(d, c), g, 0, 0))] + anywhere,
        out_shape=[jax.ShapeDtypeStruct((2, s, SSM_INNER), F32),
                   jax.ShapeDtypeStruct((2, nc, SSM_GROUPS, SSM_STATE, GW), F32)] + exchanged,
        scratch_shapes=[pltpu.VMEM((GPS, SSM_STATE, GW), F32)] + sems,
        compiler_params=_params(*(("arbitrary",) * 3 if riders else ("parallel", "parallel", "arbitrary"))),
    )(xbc, xbc, xbc, bt, ct, dtr, dtt, pr_rows, pr_cols, *[x for x, _ in riders])
    return outs[0], outs[1], list(outs[2:])


def _ssd_bwd(xbc, bt, ct, dtr, dtt, pr_rows, pr_cols, dvec, hs, y2, dy, *, name, riders=()):
    s = xbc.shape[0]
    nc = s // T
    chunk = lambda d, c: jnp.where(d == 0, nc - 1 - c, c)
    in_specs = _ssd_in_specs(chunk) + [
        pl.BlockSpec((1, GPS * GW), lambda d, g, c: (0, g)),
        pl.BlockSpec((None, None, GPS, SSM_STATE, GW), lambda d, g, c: (d, chunk(d, c), g, 0, 0)),
        pl.BlockSpec((None, T, GPS * GW), lambda d, g, c: (d, chunk(d, c), g)),
        pl.BlockSpec((T, GPS * GW), lambda d, g, c: (chunk(d, c), g)),
    ]

    def body(x_ref, b_ref, c_ref, bt_ref, ct_ref, dt_ref, dtt_ref, prr_ref, prc_ref, dvec_ref, hs_ref, y_ref, dy_ref,
             dxs_ref, db_ref, dc_ref, ddt_ref, dalog_ref, dbias_ref, g_ref):
        @pl.when(pl.program_id(2) == 0)
        def _():
            g_ref[...] = jnp.zeros_like(g_ref)
            dalog_ref[...] = jnp.zeros_like(dalog_ref)
            dbias_ref[...] = jnp.zeros_like(dbias_ref)

        for gi in range(GPS):
            group(*_group_refs(gi, [x_ref, dvec_ref, y_ref, dy_ref, dxs_ref], [b_ref, c_ref, db_ref, dc_ref], [bt_ref, ct_ref],
                               [dt_ref, dtt_ref, prr_ref, prc_ref, hs_ref, ddt_ref, dalog_ref, dbias_ref, g_ref]))

    def group(x_ref, dvec_ref, y_ref, dy_ref, dxs_ref, b_ref, c_ref, db_ref, dc_ref, bt_ref, ct_ref,
              dt_ref, dtt_ref, prr_ref, prc_ref, hs_ref, ddt_ref, dalog_ref, dbias_ref, g_ref):
        d = pl.program_id(0)
        ck = _ssd_chunk(d, dt_ref, dtt_ref, prr_ref, prc_ref)
        h0 = ck["head0"]
        xs = x_ref[...].astype(F32)
        bm, cm = b_ref[...], c_ref[...]
        hst = hs_ref[...]
        gst = g_ref[...]
        dyv = dy_ref[...]
        yv = y_ref[...]
        dv = dvec_ref[...] * (1 - d).astype(F32)
        cb = lax.dot_general(cm, bm, _DIMS["nt"], preferred_element_type=F32)
        cbt = jnp.dot(bm, ct_ref[...], preferred_element_type=F32)
        ch = jnp.dot(cm, hst.astype(BF16), preferred_element_type=F32)
        bg = jnp.dot(bm, gst.astype(BF16), preferred_element_type=F32)
        hg_cols = jnp.sum(hst * gst, axis=0, keepdims=True)
        lane16 = lax.broadcasted_iota(jnp.int32, (T, 2 * HPG), 1)
        sub8 = lax.broadcasted_iota(jnp.int32, (HPG, 1), 0)
        dcb = jnp.zeros((T, T), F32)
        acc16 = jnp.zeros((T, 2 * HPG), F32)
        dtot = jnp.zeros((HPG, 1), F32)
        dxss, dyes, xgds, etots = [], [], [], []
        for q in range(PAIRS):
            sl = slice(q * 128, (q + 1) * 128)
            pr = _ssd_pair(ck, q)
            xsp, dyp = xs[:, sl], dyv[:, sl]
            xg = xsp * pr["dt"]
            xgb = xg.astype(BF16)
            dyb = dyp.astype(BF16)
            dy0, dy1 = _split_heads(h0, dyb)
            dcb = dcb + (lax.dot_general(dy0, xgb, _DIMS["nt"], preferred_element_type=F32) * pr["lm"][0]
                         + lax.dot_general(dy1, xgb, _DIMS["nt"], preferred_element_type=F32) * pr["lm"][1])
            dxg_in = (jnp.dot((cbt * pr["lmt"][0]).astype(BF16), dy0, preferred_element_type=F32)
                      + jnp.dot((cbt * pr["lmt"][1]).astype(BF16), dy1, preferred_element_type=F32))
            xgd = xg * pr["decay"]
            xb = xgd * bg[:, sl]
            dxg = dxg_in + pr["decay"] * bg[:, sl]
            yo = pr["ec"] * ch[:, sl]
            dac = dyb.astype(F32) * (yv[:, sl] - yo) + dyp * yo - xgb.astype(F32) * dxg_in - xb
            d_0, d_1 = _split_heads(h0, dac)
            r_0, r_1 = _split_heads(h0, dxg * xsp)
            for hh, (d_h, r_h) in enumerate(((d_0, r_0), (d_1, r_1))):
                j = 2 * q + hh
                acc16 = (acc16 + jnp.where(lane16 == j, jnp.sum(d_h, axis=-1, keepdims=True), 0.0)
                         + jnp.where(lane16 == HPG + j, jnp.sum(r_h, axis=-1, keepdims=True), 0.0))
            tcols = pr["etot"] * hg_cols[:, sl] + jnp.sum(xb, axis=0, keepdims=True)
            t0, t1 = _split_heads(h0[0:1], tcols)
            dtot = (dtot + jnp.where(sub8 == 2 * q, jnp.sum(t0, axis=-1, keepdims=True), 0.0)
                    + jnp.where(sub8 == 2 * q + 1, jnp.sum(t1, axis=-1, keepdims=True), 0.0))
            dxss.append(dxg * pr["dt"] + dv[:, sl] * dyp)
            dyes.append(dyp * pr["ec"])
            xgds.append(xgd)
            etots.append(pr["etot"])
        acc_t = acc16.T
        da_lanes = _scan_lanes(acc_t[0:HPG], d != 0)[0] + dtot
        ddt = da_lanes * ck["a_c"] + acc_t[HPG:2 * HPG]
        draw = ddt * _sigmoid(ck["raw_lanes"])
        ddt_ref[...] = draw
        dbias_ref[...] += jnp.sum(draw, axis=-1, keepdims=True)
        dalog_ref[...] += jnp.sum(da_lanes * ck["dt_lanes"], axis=-1, keepdims=True) * ck["a_c"]
        dxs_ref[...] = jnp.concatenate(dxss, axis=1)
        dye = jnp.concatenate(dyes, axis=1).astype(BF16)
        xgd_all = jnp.concatenate(xgds, axis=1).astype(BF16)
        dcbb = dcb.astype(BF16)
        dc_ref[...] = (jnp.dot(dcbb, bm, preferred_element_type=F32)
                       + lax.dot_general(dye, hst.astype(BF16), _DIMS["nt"], preferred_element_type=F32))
        db_ref[...] = (lax.dot_general(dcbb, cm, _DIMS["tn"], preferred_element_type=F32)
                       + lax.dot_general(xgd_all, gst.astype(BF16), _DIMS["nt"], preferred_element_type=F32))
        g_ref[...] = jnp.dot(ct_ref[...], dye, preferred_element_type=F32) + gst * jnp.concatenate(etots, axis=1)

    small = pl.BlockSpec((None, GPS, HPG, 1), lambda d, g, c: (d, g, 0, 0))
    sshape = jax.ShapeDtypeStruct((2, SSM_GROUPS, HPG, 1), F32)
    grid = (2, GSTEPS, nc)
    anywhere, exchanged, sems = _rider_specs(riders)
    outs = pl.pallas_call(
        _riding(body, riders, 13, 6, grid), name=name, grid=grid, in_specs=in_specs + anywhere,
        out_specs=[pl.BlockSpec((None, T, GPS * GW), lambda d, g, c: (d, chunk(d, c), g)),
                   pl.BlockSpec((None, T, GPS * SSM_STATE), lambda d, g, c: (d, chunk(d, c), g)),
                   pl.BlockSpec((None, T, GPS * SSM_STATE), lambda d, g, c: (d, chunk(d, c), g)),
                   pl.BlockSpec((None, GPS, HPG, T), lambda d, g, c: (d, g, 0, chunk(d, c))), small, small] + anywhere,
        out_shape=[jax.ShapeDtypeStruct((2, s, SSM_INNER), F32),
                   jax.ShapeDtypeStruct((2, s, SSM_GROUPS * SSM_STATE), F32),
                   jax.ShapeDtypeStruct((2, s, SSM_GROUPS * SSM_STATE), F32),
                   jax.ShapeDtypeStruct((2, SSM_GROUPS, HPG, s), F32), sshape, sshape] + exchanged,
        scratch_shapes=[pltpu.VMEM((GPS, SSM_STATE, GW), F32)] + sems,
        compiler_params=_params(*(("arbitrary",) * 3 if riders else ("parallel", "parallel", "arbitrary"))),
    )(xbc, xbc, xbc, bt, ct, dtr, dtt, pr_rows, pr_cols, dvec, hs, y2, dy, *[x for x, _ in riders])
    return (*outs[:6], list(outs[6:]))


def _gate_norm_fwd(y2, xbc, proj, dvec, nw, *, name):
    s = xbc.shape[0]
    tr = 256

    def body(y_ref, xs_ref, z_ref, dv_ref, w_ref, u_ref):
        yt = y_ref[0] + y_ref[1] + dv_ref[...] * xs_ref[...].astype(F32)
        yg = yt * _silu_and_grad(z_ref[...].astype(F32))[0]
        u_ref[...] = (yg * lax.rsqrt(jnp.mean(yg * yg, axis=-1, keepdims=True) + RMS_EPS) * w_ref[...]).astype(BF16)

    row = pl.BlockSpec((tr, SSM_INNER), lambda i: (i, 0))
    vec = pl.BlockSpec((1, SSM_INNER), lambda i: (0, 0))
    return pl.pallas_call(
        body, name=name, grid=(s // tr,),
        in_specs=[pl.BlockSpec((2, tr, SSM_INNER), lambda i: (0, i, 0)), row, row, vec, vec], out_specs=row,
        out_shape=jax.ShapeDtypeStruct((s, SSM_INNER), BF16), compiler_params=_params("parallel"),
    )(y2, xbc, proj, dvec, nw)


def _gate_norm_bwd(du, y2, xbc, proj, dvec, nw, *, name):
    s = xbc.shape[0]
    tr = 256

    def body(du_ref, y_ref, xs_ref, z_ref, dv_ref, w_ref, dy_ref, dz_ref, dw_ref, dd_ref):
        @pl.when(pl.program_id(0) == 0)
        def _():
            dw_ref[...] = jnp.zeros_like(dw_ref)
            dd_ref[...] = jnp.zeros_like(dd_ref)

        xs = xs_ref[...].astype(F32)
        yt = y_ref[0] + y_ref[1] + dv_ref[...] * xs
        si, dsi = _silu_and_grad(z_ref[...].astype(F32))
        yg = yt * si
        rstd = lax.rsqrt(jnp.mean(yg * yg, axis=-1, keepdims=True) + RMS_EPS)
        yhat = yg * rstd
        du = du_ref[...]
        dyn = du * w_ref[...]
        dyg = rstd * (dyn - yhat * jnp.mean(dyn * yhat, axis=-1, keepdims=True))
        dyt = dyg * si
        dy_ref[...] = dyt
        dz_ref[...] = (dyg * yt * dsi).astype(BF16)
        dw_ref[...] += jnp.sum(du * yhat, axis=0, keepdims=True)
        dd_ref[...] += jnp.sum(dyt * xs, axis=0, keepdims=True)

    row = pl.BlockSpec((tr, SSM_INNER), lambda i: (i, 0))
    vec = pl.BlockSpec((1, SSM_INNER), lambda i: (0, 0))
    vshape = jax.ShapeDtypeStruct((1, SSM_INNER), F32)
    return pl.pallas_call(
        body, name=name, grid=(s // tr,),
        in_specs=[row, pl.BlockSpec((2, tr, SSM_INNER), lambda i: (0, i, 0)), row, row, vec, vec],
        out_specs=[row, row, vec, vec],
        out_shape=[jax.ShapeDtypeStruct((s, SSM_INNER), F32), jax.ShapeDtypeStruct((s, SSM_INNER), BF16), vshape, vshape],
        compiler_params=_params("arbitrary"),
    )(du, y2, xbc, proj, dvec, nw)


def _with_riders(result, riders):
    return result if riders else (result, [])


def _layer_a_fwd(x, mod, w_in, w_out, ln_g, ln_b, tag, riders=()):
    shift, scale, gate = mod
    h = _modulate(x, scale, shift, name=f"{tag}_modulate")
    w0 = jnp.concatenate([w_in[:, :A_GROUP_COLS], w_in[:, 3 * A_GROUP_COLS:]], axis=1)
    projs = [_mm(h, w0, mode="nn", out_dtype=BF16, tm=1024, tn=1024, tk=1024, name=f"{tag}_mm_in0")]
    for grp in (1, 2):
        projs.append(_mm_dilated(h, w_in[:, grp * A_GROUP_COLS:(grp + 1) * A_GROUP_COLS], DILATIONS[grp],
                                 name=f"{tag}_mm_in{grp}"))
    ol, exchanged = [], []
    for grp in range(3):
        res = _attn_fwd(projs[grp], grp, name=f"{tag}_attn_fwd{grp}", riders=riders if grp == 0 else ())
        ol.extend(res[:2])
        exchanged.extend(res[2] if len(res) > 2 else [])
    y, o, lse = _attn_combine(*ol, projs[0], name=f"{tag}_combine")
    out = _mm(y, w_out, mode="nn", out_dtype=F32, tm=512, tn=1024, tk=1024, name=f"{tag}_mm_out")
    xn = _resid_ln_fwd(x, out, gate, ln_g, ln_b, name=f"{tag}_resid_ln")
    return xn, (x, h, projs, y, o, lse, out), exchanged


def _layer_a_bwd(dxn, saved, mod, w_in, w_out, ln_g, tag, riders=(), scatter_own=False):
    x, h, projs, y, o, lse, out = saved
    shift, scale, gate = mod
    dx_part, dout, dgate, dln_g, dln_b = _resid_ln_bwd(x, out, gate, ln_g, dxn, name=f"{tag}_resid_ln_bwd")
    dw_out = _mm(y, dout, mode="tn", out_dtype=F32, tm=1024, tn=1024, tk=512, name=f"{tag}_mm_dw_out")
    dy = _mm(dout, w_out, mode="nt", out_dtype=F32, tm=512, tn=1024, tk=1024, name=f"{tag}_mm_dy")
    dgp, do0, do1, o1, lse1, do2, o2, lse2 = _attn_combine_bwd(dy, o, lse, projs[0], name=f"{tag}_combine_bwd")
    parts = [_attn_bwd(projs[grp], grp, *dol, name=f"{tag}_attn_bwd{grp}")
             for grp, dol in enumerate(((do0, o, lse), (do1, o1, lse1), (do2, o2, lse2)))]
    dproj = _assemble_dproj(*parts, dgp, name=f"{tag}_assemble_dproj")
    dw_in, exchanged = _with_riders(_mm(h.T, dproj, mode="nn", out_dtype=F32, tm=1024, tn=1024, tk=1024,
                                        name=f"{tag}_mm_dw_in", riders=riders), riders)
    own = ((_col_blocks(dw_in), False), (_row_blocks(dw_out), False)) if scatter_own else ()
    res = _mm_dh(dproj, w_in, dx_part, x, scale, tm=512, tk=2048, name=f"{tag}_mm_dh", riders=own)
    dx, dscale, dshift = res[:3]
    grads = dict(w_in=dw_in, w_out=dw_out, ln_g=dln_g, ln_b=dln_b, mod=jnp.concatenate([dshift, dscale, dgate], axis=1))
    return dx, grads, exchanged, (res[3] if scatter_own else None)


def _ssd_param_views(dt_raw, dt_bias, a_log):
    s = dt_raw.shape[0]
    r4 = dt_raw.reshape(s, 2, SSM_GROUPS, HPG)
    dtr = r4.transpose(1, 2, 0, 3)
    dtt = r4.transpose(1, 2, 3, 0)
    a = -jnp.exp(a_log)
    pr_rows = jnp.stack([dt_bias.reshape(2, SSM_GROUPS, HPG), a.reshape(2, SSM_GROUPS, HPG)], axis=2)
    return dtr, dtt, pr_rows, pr_rows.transpose(0, 1, 3, 2)


def _layer_b_fwd(x, mod, w_in, w_out, p, ln_g, ln_b, tag, riders=()):
    shift, scale, gate = mod
    s = x.shape[0]
    h = _modulate(x, scale, shift, name=f"{tag}_modulate")
    proj = _mm(h, w_in[:, :SSM_MAIN_COLS], mode="nn", out_dtype=BF16, tm=512, tn=1024, tk=1024, name=f"{tag}_mm_in")
    dt_raw = _mm(h, w_in[:, SSM_MAIN_COLS:SSM_IN_COLS], mode="nn", out_dtype=F32, tm=512, tn=64, tk=1024,
                 name=f"{tag}_mm_dt")
    xpad = jnp.pad(proj[:, SSM_INNER:], ((CONV_HALO, CONV_HALO), (0, 0)))
    xbc = _conv_fwd(xpad, p["conv_w"], p["conv_b"], name=f"{tag}_conv")
    views = (xbc[:, SSM_INNER:SSM_INNER + SSM_GROUPS * SSM_STATE].T, xbc[:, SSM_INNER + SSM_GROUPS * SSM_STATE:].T,
             *_ssd_param_views(dt_raw, p["dt_bias"], p["a_log"]))
    y2, hs, exchanged = _ssd_fwd(xbc, *views, name=f"{tag}_ssd_fwd", riders=riders)
    u = _gate_norm_fwd(y2, xbc, proj, p["dvec"], p["norm_w"], name=f"{tag}_gate_norm")
    out = _mm(u, w_out, mode="nn", out_dtype=F32, tm=512, tn=1024, tk=2048, name=f"{tag}_mm_out")
    xn = _resid_ln_fwd(x, out, gate, ln_g, ln_b, name=f"{tag}_resid_ln")
    return xn, (x, h, proj, xpad, xbc, views, y2, hs, u, out), exchanged


def _layer_b_bwd(dxn, saved, mod, w_in, w_out, p, ln_g, tag, riders=()):
    x, h, proj, xpad, xbc, views, y2, hs, u, out = saved
    shift, scale, gate = mod
    s = x.shape[0]
    dx_part, dout, dgate, dln_g, dln_b = _resid_ln_bwd(x, out, gate, ln_g, dxn, name=f"{tag}_resid_ln_bwd")
    dw_out = _mm(u, dout, mode="tn", out_dtype=F32, tm=1024, tn=1024, tk=512, name=f"{tag}_mm_dw_out")
    du = _mm(dout, w_out, mode="nt", out_dtype=F32, tm=512, tn=1024, tk=1024, name=f"{tag}_mm_du")
    dy, dz, dnorm_w, dd_lanes = _gate_norm_bwd(du, y2, xbc, proj, p["dvec"], p["norm_w"], name=f"{tag}_gate_norm_bwd")
    dxs2, db2, dc2, ddt4, dalog, dbias, exchanged = _ssd_bwd(xbc, *views, p["dvec"], hs, y2, dy, name=f"{tag}_ssd_bwd",
                                                             riders=riders)
    dact = jnp.concatenate([dxs2[0] + dxs2[1], db2[0] + db2[1], dc2[0] + dc2[1]], axis=1)
    dapad = jnp.pad(dact, ((CONV_HALO, CONV_HALO), (0, 0)))
    dxbc, dconv_w, dconv_b = _conv_bwd(xpad, dapad, p["conv_w"], p["conv_b"], name=f"{tag}_conv_bwd")
    ddt_raw = ddt4.transpose(3, 0, 1, 2).reshape(s, 2 * SSM_HEADS).astype(BF16)
    dproj = jnp.concatenate([dz, dxbc, ddt_raw, jnp.zeros((s, SSM_PAD_COLS - SSM_IN_COLS), BF16)], axis=1)
    dw_in = _mm(h.T, dproj, mode="nn", out_dtype=F32, tm=1024, tn=896, tk=1024, name=f"{tag}_mm_dw_in")[:, :SSM_IN_COLS]
    w_pad = jnp.pad(w_in, ((0, 0), (0, SSM_PAD_COLS - SSM_IN_COLS)))
    dx, dscale, dshift = _mm_dh(dproj, w_pad, dx_part, x, scale, tm=512, tk=1792, name=f"{tag}_mm_dh")
    grads = dict(
        w_in=dw_in, w_out=dw_out, ln_g=dln_g, ln_b=dln_b, mod=jnp.concatenate([dshift, dscale, dgate], axis=1),
        conv_w=dconv_w, conv_b=dconv_b, norm_w=dnorm_w, dt_bias=dbias.reshape(2, SSM_HEADS),
        a_log=dalog.reshape(2, SSM_HEADS), d=jnp.sum(dd_lanes.reshape(SSM_HEADS, SSM_HEAD_DIM), axis=1))
    return dx, grads, exchanged


def _full_cols(g):
    return g.transpose(1, 0, 2).reshape(g.shape[1], -1)


def _full_rows(g):
    return g.reshape(-1, g.shape[2])


def _col_blocks(dw):
    r, c = dw.shape
    return dw.reshape(r, N_DEV, c // N_DEV).transpose(1, 0, 2).astype(BF16)


def _row_blocks(dw):
    r, c = dw.shape
    return dw.reshape(N_DEV, r // N_DEV, c).astype(BF16)


def _local_step(x, target, mods, ln_g, ln_b, layer_w, b_params, shards=None):
    layer_w = list(layer_w)
    saved = []
    for i in range(DEPTH):
        riders = ()
        if shards is not None and i + 1 < DEPTH:
            riders = ((shards[i + 1][0], True), (shards[i + 1][1], True))
        small = () if i % 2 == 0 else (b_params[i // 2],)
        fwd = _layer_a_fwd if i % 2 == 0 else _layer_b_fwd
        x, sv, got = fwd(x, mods[i], *layer_w[i], *small, ln_g[i:i + 1], ln_b[i:i + 1], f"l{i}", riders)
        if riders:
            layer_w.append((_full_cols(got[0]), _full_rows(got[1])))
        saved.append(sv)
    dx, loss = _loss_and_grad(x, target, name="loss")
    grads, received = [None] * DEPTH, [None] * DEPTH
    riders = ()
    for i in reversed(range(DEPTH)):
        small = () if i % 2 == 0 else (b_params[i // 2],)
        bwd = _layer_a_bwd if i % 2 == 0 else _layer_b_bwd
        last = (True,) if shards is not None and i == 0 else ()
        res = bwd(dx, saved[i], mods[i], *layer_w[i], *small, ln_g[i:i + 1], f"l{i}", riders, *last)
        dx, grads[i], got = res[:3]
        if riders:
            received[i + 1] = got
        if last:
            received[0] = res[3]
        if shards is not None:
            riders = ((_col_blocks(grads[i]["w_in"]), False), (_row_blocks(grads[i]["w_out"]), False))
    return loss, dx, grads, received


def _mesh_pos():
    return lax.axis_index("x"), lax.axis_index("y"), lax.axis_index("c")


def _all_gather(x, *, name):
    def body(x_ref, out_ref, send_sems, recv_sems, local_sem):
        ax, ay, ac = _mesh_pos()
        me, sibling = (ax, ay, ac), (ax, ay, 1 - ac)
        chips = [(1 - ax, ay), (ax, 1 - ay), (1 - ax, 1 - ay)]

        def slot(px, py, pc):
            return out_ref.at[4 * px + 2 * py + pc]

        def copy(k, block, to, src=None):
            return pltpu.make_async_remote_copy(
                src_ref=slot(*block) if src is None else src, dst_ref=slot(*block),
                send_sem=send_sems.at[k], recv_sem=recv_sems.at[k], device_id=to, device_id_type=MESH)

        mine = pltpu.make_async_copy(x_ref, slot(*me), local_sem)
        mine.start()
        first = [copy(0, me, sibling, src=x_ref)]
        first += [copy(1 + j, me, (*chip, ac), src=x_ref) for j, chip in enumerate(chips)]
        for cp in first:
            cp.start()
        passed = [copy(4 + j, (*chip, ac), sibling) for j, chip in enumerate(chips)]
        for j, chip in enumerate(chips):
            copy(1 + j, (*chip, ac), me).wait_recv()
            passed[j].start()
        copy(0, sibling, me).wait_recv()
        for j, chip in enumerate(chips):
            copy(4 + j, (*chip, 1 - ac), me).wait_recv()
        for cp in first + passed:
            cp.wait_send()
        mine.wait()

    return pl.pallas_call(
        body, name=name, out_shape=jax.ShapeDtypeStruct((N_DEV,) + x.shape, x.dtype),
        in_specs=[pl.BlockSpec(memory_space=pl.ANY)], out_specs=pl.BlockSpec(memory_space=pl.ANY),
        scratch_shapes=[pltpu.SemaphoreType.DMA((7,)), pltpu.SemaphoreType.DMA((7,)), pltpu.SemaphoreType.DMA],
    )(x)


def _all_to_all(x, *, name):
    def body(x_ref, out_ref, send_sems, recv_sems, local_sem):
        ax, ay, ac = _mesh_pos()
        me = 4 * ax + 2 * ay + ac
        mine = pltpu.make_async_copy(x_ref.at[me], out_ref.at[me], local_sem)
        mine.start()
        copies = []
        for k in range(1, N_DEV):
            px = 1 - ax if k & 4 else ax
            py = 1 - ay if k & 2 else ay
            pc = 1 - ac if k & 1 else ac
            copies.append(pltpu.make_async_remote_copy(
                src_ref=x_ref.at[4 * px + 2 * py + pc], dst_ref=out_ref.at[me],
                send_sem=send_sems.at[k - 1], recv_sem=recv_sems.at[k - 1], device_id=(px, py, pc), device_id_type=MESH))
        for cp in copies:
            cp.start()
        for cp in copies:
            cp.wait()
        mine.wait()

    return pl.pallas_call(
        body, name=name, out_shape=jax.ShapeDtypeStruct(x.shape, x.dtype),
        in_specs=[pl.BlockSpec(memory_space=pl.ANY)], out_specs=pl.BlockSpec(memory_space=pl.ANY),
        scratch_shapes=[pltpu.SemaphoreType.DMA((7,)), pltpu.SemaphoreType.DMA((7,)), pltpu.SemaphoreType.DMA],
    )(x)


ADA_LOCAL = 3 * D_MODEL // N_DEV


def _ada_mod(c_all, ada_w, ada_b_local, *, name):
    def body(c_ref, w_ref, b_ref, o_ref):
        cond = _silu_and_grad(c_ref[...])[0]
        o_ref[...] = jnp.dot(cond, w_ref[...], precision=HIGHEST, preferred_element_type=F32) + b_ref[...]

    return pl.pallas_call(
        body, name=name, grid=(DEPTH,),
        in_specs=[pl.BlockSpec((N_DEV, D_MODEL), lambda i: (0, 0)), pl.BlockSpec((None, D_MODEL, ADA_LOCAL), lambda i: (i, 0, 0)),
                  pl.BlockSpec((None, 1, ADA_LOCAL), lambda i: (i, 0, 0))],
        out_specs=pl.BlockSpec((None, N_DEV, ADA_LOCAL), lambda i: (i, 0, 0)),
        out_shape=jax.ShapeDtypeStruct((DEPTH, N_DEV, ADA_LOCAL), F32), compiler_params=_params("parallel"),
    )(c_all, ada_w, ada_b_local)


def _ada_grad(c_all_t, dmod_local, *, name):
    def body(ct_ref, dm_ref, o_ref):
        cond_t = _silu_and_grad(ct_ref[...])[0]
        dm = dm_ref[...]
        acc = cond_t[:, 0:1] * dm[0:1, :]
        for smp in range(1, N_DEV):
            acc = acc + cond_t[:, smp:smp + 1] * dm[smp:smp + 1, :]
        o_ref[...] = acc

    return pl.pallas_call(
        body, name=name, grid=(DEPTH,),
        in_specs=[pl.BlockSpec((D_MODEL, N_DEV), lambda i: (0, 0)), pl.BlockSpec((None, N_DEV, ADA_LOCAL), lambda i: (i, 0, 0))],
        out_specs=pl.BlockSpec((None, D_MODEL, ADA_LOCAL), lambda i: (i, 0, 0)),
        out_shape=jax.ShapeDtypeStruct((DEPTH, D_MODEL, ADA_LOCAL), F32), compiler_params=_params("parallel"),
    )(c_all_t, dmod_local)


def _sum_devices(parts, *, name):
    n = parts.shape[1]

    def body(p_ref, o_ref):
        acc = p_ref[0:1, :]
        for dev in range(1, N_DEV):
            acc = acc + p_ref[dev:dev + 1, :]
        o_ref[...] = acc

    return pl.pallas_call(
        body, name=name, out_shape=jax.ShapeDtypeStruct((1, n), F32),
        in_specs=[pl.BlockSpec(memory_space=pltpu.VMEM)], out_specs=pl.BlockSpec(memory_space=pltpu.VMEM),
        compiler_params=pltpu.CompilerParams(vmem_limit_bytes=VMEM_LIMIT_BYTES),
    )(parts)


ADAMW_VMEM_BYTES = 24 * 1024 * 1024


def _adamw(w, m, v, g, *, name):
    r, c = w.shape
    summed = g.ndim == 3
    tr = r
    arrays = 7 + (N_DEV if summed else 1)
    while tr % 16 == 0 and 2 * arrays * tr * c * 4 > ADAMW_VMEM_BYTES:
        tr //= 2

    def body(w_ref, m_ref, v_ref, g_ref, go_ref, d_ref, mo_ref, vo_ref):
        if summed:
            g = g_ref[0].astype(F32)
            for dev in range(1, N_DEV):
                g = g + g_ref[dev].astype(F32)
        else:
            g = g_ref[...]
        mn = ADAM_B1 * m_ref[...] + (1.0 - ADAM_B1) * g
        vn = ADAM_B2 * v_ref[...] + (1.0 - ADAM_B2) * (g * g)
        m_hat = mn / (1.0 - ADAM_B1 ** ADAM_STEP)
        v_hat = vn / (1.0 - ADAM_B2 ** ADAM_STEP)
        go_ref[...] = g
        d_ref[...] = -ADAM_LR * (m_hat / (jnp.sqrt(v_hat) + ADAM_EPS) + ADAM_WD * w_ref[...])
        mo_ref[...] = mn
        vo_ref[...] = vn

    row = pl.BlockSpec((tr, c), lambda i: (i, 0))
    gspec = pl.BlockSpec((N_DEV, tr, c), lambda i: (0, i, 0)) if summed else row
    shp = jax.ShapeDtypeStruct((r, c), F32)
    return pl.pallas_call(
        body, name=name, grid=(r // tr,), in_specs=[row, row, row, gspec], out_specs=[row] * 4, out_shape=[shp] * 4,
        compiler_params=_params("parallel"),
    )(w, m, v, g)


def _pack(arrays):
    flat = jnp.concatenate([a.reshape(-1) for a in arrays])
    n = flat.shape[0]
    return jnp.pad(flat, (0, -n % 128)).reshape(1, -1)


def _unpack(vec, shapes):
    out, at = [], 0
    for shp in shapes:
        n = math.prod(shp)
        out.append(vec[at:at + n].reshape(shp))
        at += n
    return out


def _unpack_rows(rows, shapes):
    out, at = [], 0
    for shp in shapes:
        n = math.prod(shp)
        out.append(rows[:, at:at + n].reshape((rows.shape[0],) + tuple(shp)))
        at += n
    return out


def _my_shard(full, me, axis):
    width = full.shape[axis] // N_DEV
    return lax.dynamic_slice_in_dim(full, me * width, width, axis)


def _gather_cols(g, lead):
    nd = g.ndim
    perm = tuple(range(1, nd - 1)) + (0, nd - 1)
    t = g.transpose(perm)
    return t.reshape(t.shape[:-2] + (t.shape[-2] * t.shape[-1],))


def kernel(x, c, ada_w, ada_b, ln_g, ln_b, a_w_in, a_w_out, b_w_in, b_conv_w, b_conv_b, b_dt_bias, b_a_log, b_d, b_norm_w, b_w_out, loss_target, m_ada_w, m_ada_b, m_ln_g, m_ln_b, m_a_w_in, m_a_w_out, m_b_w_in, m_b_conv_w, m_b_conv_b, m_b_dt_bias, m_b_a_log, m_b_d, m_b_norm_w, m_b_w_out, v_ada_w, v_ada_b, v_ln_g, v_ln_b, v_a_w_in, v_a_w_out, v_b_w_in, v_b_conv_w, v_b_conv_b, v_b_dt_bias, v_b_a_log, v_b_d, v_b_norm_w, v_b_w_out):
    ax, ay, ac = _mesh_pos()
    me = 4 * ax + 2 * ay + ac
    seq = x.shape[1]

    small_shapes = [(1, D_MODEL), (2, SSM_CONV, ADA_LOCAL), (2, ADA_LOCAL), (2, SSM_INNER // N_DEV)]
    sg = _all_gather(_pack([c, b_conv_w, b_conv_b, b_norm_w]), name="gather_small")[:, 0, :]
    c_all, conv_w_g, conv_b_g, norm_w_g = _unpack_rows(sg, small_shapes)
    c_all = c_all[:, 0, :]
    conv_w = _gather_cols(conv_w_g, 2)
    conv_b = _gather_cols(conv_b_g[:, :, None, :], 2)
    norm_w = _gather_cols(norm_w_g[:, :, None, :], 2)

    shards = [(w_in[i // 2].astype(BF16), w_out[i // 2].astype(BF16))
              for i, (w_in, w_out) in enumerate(((a_w_in, a_w_out), (b_w_in, b_w_out)) * 2)]
    layer0_w = (_full_cols(_all_gather(shards[0][0], name="gather_w_in0")),
                _full_rows(_all_gather(shards[0][1], name="gather_w_out0")))

    ada_b_local = _my_shard(ada_b, me, 1)[:, None, :]
    mod_cols = _ada_mod(c_all, ada_w, ada_b_local, name="ada_mod")
    mod_g = _all_gather(mod_cols.reshape(1, -1), name="gather_mod").reshape(N_DEV, DEPTH, N_DEV, ADA_LOCAL)
    mod = lax.dynamic_index_in_dim(mod_g, me, axis=2, keepdims=False).transpose(1, 0, 2).reshape(DEPTH, 3 * D_MODEL)
    mods = [tuple(mod[i:i + 1, k * D_MODEL:(k + 1) * D_MODEL] for k in range(3)) for i in range(DEPTH)]

    b_params = [dict(conv_w=conv_w[j], conv_b=conv_b[j], norm_w=norm_w[j],
                     dt_bias=b_dt_bias[j], a_log=b_a_log[j], dvec=jnp.repeat(b_d[j], SSM_HEAD_DIM)[None, :])
                for j in range(2)]
    loss_lanes, dx, grads, received = _local_step(x[0], loss_target[0], mods, ln_g, ln_b, [layer0_w], b_params, shards)
    loss = lax.psum(loss_lanes[0, 0], ("x", "y", "c"))
    grad_x = dx[None]

    a_layers, b_layers = (grads[0], grads[2]), (grads[1], grads[3])
    part_shapes = [(DEPTH, 3 * D_MODEL), (DEPTH, D_MODEL), (DEPTH, D_MODEL), (2, SSM_CONV, SSM_CONV_DIM),
                   (2, SSM_CONV_DIM), (2, SSM_INNER), (2, 2, SSM_HEADS), (2, 2, SSM_HEADS), (2, SSM_HEADS)]
    parts = _pack([
        jnp.concatenate([g["mod"] for g in grads]), jnp.concatenate([g["ln_g"] for g in grads]),
        jnp.concatenate([g["ln_b"] for g in grads]), jnp.stack([g["conv_w"] for g in b_layers]),
        jnp.stack([g["conv_b"][0] for g in b_layers]), jnp.stack([g["norm_w"][0] for g in b_layers]),
        jnp.stack([g["dt_bias"] for g in b_layers]), jnp.stack([g["a_log"] for g in b_layers]),
        jnp.stack([g["d"] for g in b_layers])])
    parts_g = _all_gather(parts, name="gather_small_grads")[:, 0, :]
    (g_ada_b, g_ln_g, g_ln_b, g_conv_w, g_conv_b, g_norm_w, g_dt_bias, g_a_log, g_d) = _unpack(
        _sum_devices(parts_g, name="sum_small_grads")[0], part_shapes)
    dmod_all = parts_g[:, :DEPTH * 3 * D_MODEL].reshape(N_DEV, DEPTH, N_DEV, ADA_LOCAL)
    dmod_local = lax.dynamic_index_in_dim(dmod_all, me, axis=2, keepdims=False).transpose(1, 0, 2)
    g_ada_w = _ada_grad(c_all.T, dmod_local, name="ada_grad")

    r_a_w_in = jnp.concatenate([received[0][0], received[2][0]], axis=1)
    r_a_w_out = jnp.concatenate([received[0][1], received[2][1]], axis=1)
    r_b_w_in = jnp.concatenate([received[1][0], received[3][0]], axis=1)
    r_b_w_out = jnp.concatenate([received[1][1], received[3][1]], axis=1)

    def update(w, m, v, g, name):
        two_d = (-1, w.shape[-1])
        outs = _adamw(w.reshape(two_d), m.reshape(two_d), v.reshape(two_d), g, name=name)
        return [o.reshape(w.shape) for o in outs]

    up_ada_w = update(ada_w, m_ada_w, v_ada_w, g_ada_w.reshape(-1, ADA_LOCAL), "adamw_ada_w")
    up_a_w_in = update(a_w_in, m_a_w_in, v_a_w_in, r_a_w_in, "adamw_a_w_in")
    up_a_w_out = update(a_w_out, m_a_w_out, v_a_w_out, r_a_w_out, "adamw_a_w_out")
    up_b_w_in = update(b_w_in, m_b_w_in, v_b_w_in, r_b_w_in, "adamw_b_w_in")
    up_b_w_out = update(b_w_out, m_b_w_out, v_b_w_out, r_b_w_out, "adamw_b_w_out")

    small_w = [ada_b, ln_g, ln_b, b_conv_w, b_conv_b, b_dt_bias, b_a_log, b_d, b_norm_w]
    small_m = [m_ada_b, m_ln_g, m_ln_b, m_b_conv_w, m_b_conv_b, m_b_dt_bias, m_b_a_log, m_b_d, m_b_norm_w]
    small_v = [v_ada_b, v_ln_g, v_ln_b, v_b_conv_w, v_b_conv_b, v_b_dt_bias, v_b_a_log, v_b_d, v_b_norm_w]
    small_g = [g_ada_b, g_ln_g, g_ln_b, _my_shard(g_conv_w, me, 2), _my_shard(g_conv_b, me, 1), g_dt_bias, g_a_log, g_d,
               _my_shard(g_norm_w, me, 1)]
    shapes = [w.shape for w in small_w]
    packed = _adamw(_pack(small_w), _pack(small_m), _pack(small_v), _pack(small_g), name="adamw_small")
    (up_ada_b, up_ln_g, up_ln_b, up_conv_w, up_conv_b, up_dt_bias, up_a_log, up_d, up_norm_w) = zip(
        *[_unpack(p[0], shapes) for p in packed])

    ordered = [up_ada_w, up_ada_b, up_ln_g, up_ln_b, up_a_w_in, up_a_w_out, up_b_w_in, up_conv_w, up_conv_b,
               up_dt_bias, up_a_log, up_d, up_norm_w, up_b_w_out]
    return (loss, grad_x, *[u[0] for u in ordered], *[u[1] for u in ordered], *[u[2] for u in ordered],
            *[u[3] for u in ordered])
```
